```python
import jax
import jax.numpy as jnp
from jax import lax
import numpy as np

D_MODEL = 1024
BATCH = 8
SEQ = 4096
DEPTH = 4

N_MIXERS = 4
EPS = 1e-6
D_FF = 4 * D_MODEL
CONV_WIDTH = 31
POOL_WINDOWS = (2, 4, 8, 16)
N_POOL_GROUPS = len(POOL_WINDOWS)
POOL_GROUP_DIM = D_MODEL // N_POOL_GROUPS
SHORT_CONV_WIDTH = 3
RET_HEADS = 4
RET_QK_DIM = D_MODEL // RET_HEADS
RET_V_DIM = 2 * D_MODEL // RET_HEADS
RET_QK_TOTAL = RET_HEADS * RET_QK_DIM
RET_V_TOTAL = RET_HEADS * RET_V_DIM
RET_CHUNK = 128
ROPE_BASE = 10000.0

kernel_name = "hybrid_interleaved_conv_pool_shortconv_retention"


def rms_norm(x, g):
    xf = x.astype(jnp.float32)
    y = xf * lax.rsqrt(jnp.mean(xf * xf, axis=-1, keepdims=True) + EPS)
    return (y * g.astype(jnp.float32)).astype(x.dtype)


def layer_norm(x, g, b):
    xf = x.astype(jnp.float32)
    mu = jnp.mean(xf, axis=-1, keepdims=True)
    xc = xf - mu
    y = xc * lax.rsqrt(jnp.mean(xc * xc, axis=-1, keepdims=True) + EPS)
    return (y * g.astype(jnp.float32) + b.astype(jnp.float32)).astype(x.dtype)


def causal_depthwise_conv(x, w):
    width = w.shape[0]
    return lax.conv_general_dilated(
        x, w[:, None, :].astype(x.dtype), window_strides=(1,),
        padding=[(width - 1, 0)], dimension_numbers=('NWC', 'WIO', 'NWC'),
        feature_group_count=x.shape[-1])


def conformer_conv(h, w_in, b_in, w_dw, b_dw, ln_g, ln_b, w_out, b_out):
    u = h @ w_in + b_in
    a, gate = jnp.split(u, 2, axis=-1)
    u = a * jax.nn.sigmoid(gate)
    u = causal_depthwise_conv(u, w_dw) + b_dw
    u = jax.nn.silu(layer_norm(u, ln_g, ln_b))
    return u @ w_out + b_out


def multiscale_pool(h, w_group, scale):
    b, s, d = h.shape
    hg = h.reshape(b, s, N_POOL_GROUPS, POOL_GROUP_DIM)
    hf = hg.astype(jnp.float32)
    cs = jnp.cumsum(hf, axis=1)
    t = jnp.arange(1, s + 1, dtype=jnp.float32)
    means = []
    for g, win in enumerate(POOL_WINDOWS):
        csg = cs[:, :, g]
        lagged = jnp.pad(csg, ((0, 0), (win, 0), (0, 0)))[:, :s]
        count = jnp.minimum(t, float(win))[None, :, None]
        means.append((csg - lagged) / count)
    pooled = jnp.stack(means, axis=2)
    mixed = (pooled - hf).astype(h.dtype)
    y = jnp.einsum('bsgc,gce->bsge', mixed, w_group).reshape(b, s, d)
    return y * scale


def short_gated_conv(h, w_in, w_dw, w_out):
    b_gate, c_gate, v = jnp.split(h @ w_in, 3, axis=-1)
    u = causal_depthwise_conv(c_gate * v, w_dw)
    return (b_gate * u) @ w_out


def rotary(x, positions):
    half = x.shape[-1] // 2
    inv_freq = ROPE_BASE ** (-jnp.arange(half, dtype=jnp.float32) / half)
    ang = positions.astype(jnp.float32)[..., None] * inv_freq
    cos = jnp.cos(ang)[:, :, None, :]
    sin = jnp.sin(ang)[:, :, None, :]
    x1, x2 = x[..., :half], x[..., half:]
    return jnp.concatenate([x1 * cos - x2 * sin, x1 * sin + x2 * cos], axis=-1)


def retention(h, positions, w_in, w_out):
    b, s, _ = h.shape
    n_chunks = s // RET_CHUNK
    q, k, v, g = jnp.split(h @ w_in, [RET_QK_TOTAL, 2 * RET_QK_TOTAL, 2 * RET_QK_TOTAL + RET_V_TOTAL], axis=-1)
    q = rotary(q.astype(jnp.float32).reshape(b, s, RET_HEADS, RET_QK_DIM), positions)
    k = rotary(k.astype(jnp.float32).reshape(b, s, RET_HEADS, RET_QK_DIM), positions) * (RET_QK_DIM ** -0.5)
    v = v.astype(jnp.float32).reshape(b, s, RET_HEADS, RET_V_DIM)

    def to_chunks(t):
        return t.reshape(b, n_chunks, RET_CHUNK, RET_HEADS, -1).transpose(1, 0, 3, 2, 4)

    qc, kc, vc = to_chunks(q), to_chunks(k), to_chunks(v)
    log_gamma = jnp.log1p(-jnp.exp2(-5.0 - jnp.arange(RET_HEADS, dtype=jnp.float32)))
    idx = jnp.arange(RET_CHUNK, dtype=jnp.float32)
    rel = idx[:, None] - idx[None, :]
    decay_mask = jnp.where(rel >= 0, jnp.exp(log_gamma[:, None, None] * jnp.maximum(rel, 0.0)), 0.0)
    q_decay = jnp.exp(log_gamma[:, None] * (idx + 1.0))[None, :, :, None]
    k_decay = jnp.exp(log_gamma[:, None] * (RET_CHUNK - 1.0 - idx))[None, :, :, None]
    chunk_decay = jnp.exp(log_gamma * RET_CHUNK)[None, :, None, None]

    def step(state, inp):
        q_n, k_n, v_n = inp
        scores = jnp.einsum('bhcd,bhmd->bhcm', q_n, k_n) * decay_mask[None]
        intra = jnp.einsum('bhcm,bhme->bhce', scores, v_n)
        cross = jnp.einsum('bhcd,bhde->bhce', q_n * q_decay, state)
        state = state * chunk_decay + jnp.einsum('bhcd,bhce->bhde', k_n * k_decay, v_n)
        return state, intra + cross

    state0 = jnp.zeros((b, RET_HEADS, RET_QK_DIM, RET_V_DIM), jnp.float32)
    _, o = lax.scan(step, state0, (qc, kc, vc))
    o = o.transpose(1, 0, 3, 2, 4).reshape(b, s, RET_HEADS, RET_V_DIM)
    o = o * lax.rsqrt(jnp.mean(o * o, axis=-1, keepdims=True) + EPS)
    o = o.reshape(b, s, RET_V_TOTAL).astype(h.dtype)
    return (jax.nn.silu(g) * o) @ w_out


def squared_relu_mlp(h, w_up, w_down):
    return jnp.square(jax.nn.relu(h @ w_up)) @ w_down


def _fwd_setup_inputs(seed: int = 0) -> dict:
    key = jax.random.key(seed)
    ks = jax.random.split(key, 24)
    d = D_MODEL
    nrm = lambda k, shape, fan_in: jax.random.normal(k, shape, jnp.float32) * (fan_in ** -0.5)
    small = lambda k, shape: 0.02 * jax.random.normal(k, shape, jnp.float32)
    x = jax.random.normal(ks[0], (BATCH, SEQ, d), jnp.float32)
    positions = jnp.broadcast_to(jnp.arange(SEQ, dtype=jnp.int32)[None, :], (BATCH, SEQ))
    norm_g = 1.0 + 0.05 * jax.random.normal(ks[1], (DEPTH, 4, d), jnp.float32)
    mlp_up = nrm(ks[2], (DEPTH, d, D_FF), d)
    mlp_down = nrm(ks[3], (DEPTH, D_FF, d), D_FF)
    conv_w_in = nrm(ks[4], (d, 2 * d), d)
    conv_b_in = small(ks[5], (2 * d,))
    conv_dw = nrm(ks[6], (CONV_WIDTH, d), CONV_WIDTH)
    conv_dw_b = small(ks[7], (d,))
    conv_ln_g = 1.0 + 0.05 * jax.random.normal(ks[8], (d,), jnp.float32)
    conv_ln_b = small(ks[9], (d,))
    conv_w_out = nrm(ks[10], (d, d), d)
    conv_b_out = small(ks[11], (d,))
    pool_w = nrm(ks[12], (N_POOL_GROUPS, POOL_GROUP_DIM, POOL_GROUP_DIM), POOL_GROUP_DIM)
    pool_scale = 1.0 + 0.1 * jax.random.normal(ks[13], (d,), jnp.float32)
    sc_w_in = nrm(ks[14], (d, 3 * d), d)
    sc_dw = nrm(ks[15], (SHORT_CONV_WIDTH, d), SHORT_CONV_WIDTH)
    sc_w_out = nrm(ks[16], (d, d), d)
    ret_w_in = nrm(ks[17], (d, 2 * RET_QK_TOTAL + 2 * RET_V_TOTAL), d)
    ret_w_out = nrm(ks[18], (RET_V_TOTAL, d), RET_V_TOTAL)
    return {"x": x, "positions": positions, "norm_g": norm_g, "mlp_up": mlp_up, "mlp_down": mlp_down,
            "conv_w_in": conv_w_in, "conv_b_in": conv_b_in, "conv_dw": conv_dw, "conv_dw_b": conv_dw_b,
            "conv_ln_g": conv_ln_g, "conv_ln_b": conv_ln_b, "conv_w_out": conv_w_out, "conv_b_out": conv_b_out,
            "pool_w": pool_w, "pool_scale": pool_scale,
            "sc_w_in": sc_w_in, "sc_dw": sc_dw, "sc_w_out": sc_w_out,
            "ret_w_in": ret_w_in, "ret_w_out": ret_w_out}


def _fwd_reference(x, positions, norm_g, mlp_up, mlp_down,
              conv_w_in, conv_b_in, conv_dw, conv_dw_b, conv_ln_g, conv_ln_b, conv_w_out, conv_b_out,
              pool_w, pool_scale,
              sc_w_in, sc_dw, sc_w_out,
              ret_w_in, ret_w_out):
    h = x
    for i in range(DEPTH):
        mixer = i % N_MIXERS
        u = rms_norm(h, norm_g[i, 0])
        if mixer == 0:
            u = conformer_conv(u, conv_w_in, conv_b_in, conv_dw, conv_dw_b, conv_ln_g, conv_ln_b, conv_w_out, conv_b_out)
        elif mixer == 1:
            u = multiscale_pool(u, pool_w, pool_scale)
        elif mixer == 2:
            u = short_gated_conv(u, sc_w_in, sc_dw, sc_w_out)
        else:
            u = retention(u, positions, ret_w_in, ret_w_out)
        h = h + rms_norm(u, norm_g[i, 1])
        u = squared_relu_mlp(rms_norm(h, norm_g[i, 2]), mlp_up[i], mlp_down[i])
        h = h + rms_norm(u, norm_g[i, 3])
    return h


import jax as _jax
import jax.numpy as _jnp

TWIN_FORMAT = 'train_step'
FWD_PARAMS = ['x', 'positions', 'norm_g', 'mlp_up', 'mlp_down', 'conv_w_in', 'conv_b_in', 'conv_dw', 'conv_dw_b', 'conv_ln_g', 'conv_ln_b', 'conv_w_out', 'conv_b_out', 'pool_w', 'pool_scale', 'sc_w_in', 'sc_dw', 'sc_w_out', 'ret_w_in', 'ret_w_out']
TWIN_WEIGHTS = ['norm_g', 'mlp_up', 'mlp_down', 'conv_w_in', 'conv_b_in', 'conv_dw', 'conv_dw_b', 'conv_ln_g', 'conv_ln_b', 'conv_w_out', 'conv_b_out', 'pool_w', 'pool_scale', 'sc_w_in', 'sc_dw', 'sc_w_out', 'ret_w_in', 'ret_w_out']
TWIN_DIFF_INPUT = 'x'
TWIN_INPUTS = ['x', 'positions', 'norm_g', 'mlp_up', 'mlp_down', 'conv_w_in', 'conv_b_in', 'conv_dw', 'conv_dw_b', 'conv_ln_g', 'conv_ln_b', 'conv_w_out', 'conv_b_out', 'pool_w', 'pool_scale', 'sc_w_in', 'sc_dw', 'sc_w_out', 'ret_w_in', 'ret_w_out', 'loss_target', 'm_norm_g', 'm_mlp_up', 'm_mlp_down', 'm_conv_w_in', 'm_conv_b_in', 'm_conv_dw', 'm_conv_dw_b', 'm_conv_ln_g', 'm_conv_ln_b', 'm_conv_w_out', 'm_conv_b_out', 'm_pool_w', 'm_pool_scale', 'm_sc_w_in', 'm_sc_dw', 'm_sc_w_out', 'm_ret_w_in', 'm_ret_w_out', 'v_norm_g', 'v_mlp_up', 'v_mlp_down', 'v_conv_w_in', 'v_conv_b_in', 'v_conv_dw', 'v_conv_dw_b', 'v_conv_ln_g', 'v_conv_ln_b', 'v_conv_w_out', 'v_conv_b_out', 'v_pool_w', 'v_pool_scale', 'v_sc_w_in', 'v_sc_dw', 'v_sc_w_out', 'v_ret_w_in', 'v_ret_w_out']
TWIN_OUTPUTS = ['loss', 'grad_x', 'grad_norm_g', 'grad_mlp_up', 'grad_mlp_down', 'grad_conv_w_in', 'grad_conv_b_in', 'grad_conv_dw', 'grad_conv_dw_b', 'grad_conv_ln_g', 'grad_conv_ln_b', 'grad_conv_w_out', 'grad_conv_b_out', 'grad_pool_w', 'grad_pool_scale', 'grad_sc_w_in', 'grad_sc_dw', 'grad_sc_w_out', 'grad_ret_w_in', 'grad_ret_w_out', 'delta_norm_g', 'delta_mlp_up', 'delta_mlp_down', 'delta_conv_w_in', 'delta_conv_b_in', 'delta_conv_dw', 'delta_conv_dw_b', 'delta_conv_ln_g', 'delta_conv_ln_b', 'delta_conv_w_out', 'delta_conv_b_out', 'delta_pool_w', 'delta_pool_scale', 'delta_sc_w_in', 'delta_sc_dw', 'delta_sc_w_out', 'delta_ret_w_in', 'delta_ret_w_out', 'new_m_norm_g', 'new_m_mlp_up', 'new_m_mlp_down', 'new_m_conv_w_in', 'new_m_conv_b_in', 'new_m_conv_dw', 'new_m_conv_dw_b', 'new_m_conv_ln_g', 'new_m_conv_ln_b', 'new_m_conv_w_out', 'new_m_conv_b_out', 'new_m_pool_w', 'new_m_pool_scale', 'new_m_sc_w_in', 'new_m_sc_dw', 'new_m_sc_w_out', 'new_m_ret_w_in', 'new_m_ret_w_out', 'new_v_norm_g', 'new_v_mlp_up', 'new_v_mlp_down', 'new_v_conv_w_in', 'new_v_conv_b_in', 'new_v_conv_dw', 'new_v_conv_dw_b', 'new_v_conv_ln_g', 'new_v_conv_ln_b', 'new_v_conv_w_out', 'new_v_conv_b_out', 'new_v_pool_w', 'new_v_pool_scale', 'new_v_sc_w_in', 'new_v_sc_dw', 'new_v_sc_w_out', 'new_v_ret_w_in', 'new_v_ret_w_out']
TWIN_LEAF_KINDS = {'loss': 'loss', 'grad_x': 'grad_x', 'grad_norm_g': 'grad_w', 'grad_mlp_up': 'grad_w', 'grad_mlp_down': 'grad_w', 'grad_conv_w_in': 'grad_w', 'grad_conv_b_in': 'grad_w', 'grad_conv_dw': 'grad_w', 'grad_conv_dw_b': 'grad_w', 'grad_conv_ln_g': 'grad_w', 'grad_conv_ln_b': 'grad_w', 'grad_conv_w_out': 'grad_w', 'grad_conv_b_out': 'grad_w', 'grad_pool_w': 'grad_w', 'grad_pool_scale': 'grad_w', 'grad_sc_w_in': 'grad_w', 'grad_sc_dw': 'grad_w', 'grad_sc_w_out': 'grad_w', 'grad_ret_w_in': 'grad_w', 'grad_ret_w_out': 'grad_w', 'delta_norm_g': 'delta_w', 'delta_mlp_up': 'delta_w', 'delta_mlp_down': 'delta_w', 'delta_conv_w_in': 'delta_w', 'delta_conv_b_in': 'delta_w', 'delta_conv_dw': 'delta_w', 'delta_conv_dw_b': 'delta_w', 'delta_conv_ln_g': 'delta_w', 'delta_conv_ln_b': 'delta_w', 'delta_conv_w_out': 'delta_w', 'delta_conv_b_out': 'delta_w', 'delta_pool_w': 'delta_w', 'delta_pool_scale': 'delta_w', 'delta_sc_w_in': 'delta_w', 'delta_sc_dw': 'delta_w', 'delta_sc_w_out': 'delta_w', 'delta_ret_w_in': 'delta_w', 'delta_ret_w_out': 'delta_w', 'new_m_norm_g': 'new_m', 'new_m_mlp_up': 'new_m', 'new_m_mlp_down': 'new_m', 'new_m_conv_w_in': 'new_m', 'new_m_conv_b_in': 'new_m', 'new_m_conv_dw': 'new_m', 'new_m_conv_dw_b': 'new_m', 'new_m_conv_ln_g': 'new_m', 'new_m_conv_ln_b': 'new_m', 'new_m_conv_w_out': 'new_m', 'new_m_conv_b_out': 'new_m', 'new_m_pool_w': 'new_m', 'new_m_pool_scale': 'new_m', 'new_m_sc_w_in': 'new_m', 'new_m_sc_dw': 'new_m', 'new_m_sc_w_out': 'new_m', 'new_m_ret_w_in': 'new_m', 'new_m_ret_w_out': 'new_m', 'new_v_norm_g': 'new_v', 'new_v_mlp_up': 'new_v', 'new_v_mlp_down': 'new_v', 'new_v_conv_w_in': 'new_v', 'new_v_conv_b_in': 'new_v', 'new_v_conv_dw': 'new_v', 'new_v_conv_dw_b': 'new_v', 'new_v_conv_ln_g': 'new_v', 'new_v_conv_ln_b': 'new_v', 'new_v_conv_w_out': 'new_v', 'new_v_conv_b_out': 'new_v', 'new_v_pool_w': 'new_v', 'new_v_pool_scale': 'new_v', 'new_v_sc_w_in': 'new_v', 'new_v_sc_dw': 'new_v', 'new_v_sc_w_out': 'new_v', 'new_v_ret_w_in': 'new_v', 'new_v_ret_w_out': 'new_v'}


def _forward(args):
    return _fwd_reference(*[args[k] for k in FWD_PARAMS])


def _output_shape():
    out = _jax.eval_shape(lambda: _forward(_fwd_setup_inputs(0)))
    return out.shape, out.dtype

N_MICROBATCH = 1
ADAM_LR = 0.001
ADAM_B1 = 0.9
ADAM_B2 = 0.999
ADAM_EPS = 1e-08
ADAM_WD = 0.01
ADAM_STEP = 10
PER_EXAMPLE_BATCH_AXIS = {'x': 0, 'positions': 0, 'loss_target': 0}
SHARED_INPUTS = []
_WEIGHT_DTYPES = {'norm_g': _jnp.float32, 'mlp_up': _jnp.float32, 'mlp_down': _jnp.float32, 'conv_w_in': _jnp.float32, 'conv_b_in': _jnp.float32, 'conv_dw': _jnp.float32, 'conv_dw_b': _jnp.float32, 'conv_ln_g': _jnp.float32, 'conv_ln_b': _jnp.float32, 'conv_w_out': _jnp.float32, 'conv_b_out': _jnp.float32, 'pool_w': _jnp.float32, 'pool_scale': _jnp.float32, 'sc_w_in': _jnp.float32, 'sc_dw': _jnp.float32, 'sc_w_out': _jnp.float32, 'ret_w_in': _jnp.float32, 'ret_w_out': _jnp.float32}
MOMENT_SCALE = {'norm_g': 2.394666e+01, 'mlp_up': 2.432654e+00, 'mlp_down': 1.254523e+01, 'conv_w_in': 1.408309e+00, 'conv_b_in': 1.923052e+01, 'conv_dw': 2.913073e+00, 'conv_dw_b': 5.002856e+01, 'conv_ln_g': 1.896795e+01, 'conv_ln_b': 2.936604e+01, 'conv_w_out': 1.119605e+01, 'conv_b_out': 6.070186e+01, 'pool_w': 2.624178e+00, 'pool_scale': 4.025453e+00, 'sc_w_in': 1.065439e+00, 'sc_dw': 1.187062e+00, 'sc_w_out': 1.138809e+00, 'ret_w_in': 1.359194e+00, 'ret_w_out': 1.773516e+00}


def _to_microbatches(a, axis):
    t = _jnp.moveaxis(a, axis, 0)
    t = t.reshape((N_MICROBATCH, t.shape[0] // N_MICROBATCH) + t.shape[1:])
    return _jnp.moveaxis(t, 1, axis + 1)


def setup_inputs(seed: int = 0) -> dict:
    inp = _fwd_setup_inputs(seed)
    key = _jax.random.fold_in(_jax.random.key(seed), 7919)
    shape, _ = _output_shape()
    out = dict(inp)
    out["loss_target"] = _jax.random.normal(_jax.random.fold_in(key, 0), shape, _jnp.float32)
    for i, name in enumerate(TWIN_WEIGHTS):
        w = inp[name].astype(_jnp.float32)
        if MOMENT_SCALE is None:
            s = _jnp.sqrt(_jnp.mean(_jnp.square(w)) + 1e-30)
        else:
            s = MOMENT_SCALE[name]
        km, kv = _jax.random.split(_jax.random.fold_in(key, i + 1))
        out[name] = w
        out["m_" + name] = s * _jax.random.normal(km, w.shape, _jnp.float32)
        out["v_" + name] = (s * s) * _jax.random.uniform(kv, w.shape, _jnp.float32, 0.5, 1.5)
    if N_MICROBATCH > 1:
        for name, axis in PER_EXAMPLE_BATCH_AXIS.items():
            out[name] = _to_microbatches(out[name], axis)
    return {'x': out['x'], 'positions': out['positions'], 'norm_g': out['norm_g'], 'mlp_up': out['mlp_up'], 'mlp_down': out['mlp_down'], 'conv_w_in': out['conv_w_in'], 'conv_b_in': out['conv_b_in'], 'conv_dw': out['conv_dw'], 'conv_dw_b': out['conv_dw_b'], 'conv_ln_g': out['conv_ln_g'], 'conv_ln_b': out['conv_ln_b'], 'conv_w_out': out['conv_w_out'], 'conv_b_out': out['conv_b_out'], 'pool_w': out['pool_w'], 'pool_scale': out['pool_scale'], 'sc_w_in': out['sc_w_in'], 'sc_dw': out['sc_dw'], 'sc_w_out': out['sc_w_out'], 'ret_w_in': out['ret_w_in'], 'ret_w_out': out['ret_w_out'], 'loss_target': out['loss_target'], 'm_norm_g': out['m_norm_g'], 'm_mlp_up': out['m_mlp_up'], 'm_mlp_down': out['m_mlp_down'], 'm_conv_w_in': out['m_conv_w_in'], 'm_conv_b_in': out['m_conv_b_in'], 'm_conv_dw': out['m_conv_dw'], 'm_conv_dw_b': out['m_conv_dw_b'], 'm_conv_ln_g': out['m_conv_ln_g'], 'm_conv_ln_b': out['m_conv_ln_b'], 'm_conv_w_out': out['m_conv_w_out'], 'm_conv_b_out': out['m_conv_b_out'], 'm_pool_w': out['m_pool_w'], 'm_pool_scale': out['m_pool_scale'], 'm_sc_w_in': out['m_sc_w_in'], 'm_sc_dw': out['m_sc_dw'], 'm_sc_w_out': out['m_sc_w_out'], 'm_ret_w_in': out['m_ret_w_in'], 'm_ret_w_out': out['m_ret_w_out'], 'v_norm_g': out['v_norm_g'], 'v_mlp_up': out['v_mlp_up'], 'v_mlp_down': out['v_mlp_down'], 'v_conv_w_in': out['v_conv_w_in'], 'v_conv_b_in': out['v_conv_b_in'], 'v_conv_dw': out['v_conv_dw'], 'v_conv_dw_b': out['v_conv_dw_b'], 'v_conv_ln_g': out['v_conv_ln_g'], 'v_conv_ln_b': out['v_conv_ln_b'], 'v_conv_w_out': out['v_conv_w_out'], 'v_conv_b_out': out['v_conv_b_out'], 'v_pool_w': out['v_pool_w'], 'v_pool_scale': out['v_pool_scale'], 'v_sc_w_in': out['v_sc_w_in'], 'v_sc_dw': out['v_sc_dw'], 'v_sc_w_out': out['v_sc_w_out'], 'v_ret_w_in': out['v_ret_w_in'], 'v_ret_w_out': out['v_ret_w_out']}


def _loss(weights, diff, rest, loss_target):
    with _jax.named_scope("forward"):
        args = {**rest, TWIN_DIFF_INPUT: diff, **{k: w.astype(_WEIGHT_DTYPES[k]) for k, w in weights.items()}}
        y = _forward(args)
    with _jax.named_scope("loss_head"):
        err = _jnp.square(y.astype(_jnp.float32) - loss_target)
        return 0.5 * _jnp.sum(_jnp.mean(err, axis=-1)) if err.ndim else 0.5 * err


def _adamw(w, g, m, v):
    m = ADAM_B1 * m + (1.0 - ADAM_B1) * g
    v = ADAM_B2 * v + (1.0 - ADAM_B2) * _jnp.square(g)
    m_hat = m / (1.0 - ADAM_B1 ** ADAM_STEP)
    v_hat = v / (1.0 - ADAM_B2 ** ADAM_STEP)
    delta = -ADAM_LR * (m_hat / (_jnp.sqrt(v_hat) + ADAM_EPS) + ADAM_WD * w)
    return delta, m, v


def reference(x, positions, norm_g, mlp_up, mlp_down, conv_w_in, conv_b_in, conv_dw, conv_dw_b, conv_ln_g, conv_ln_b, conv_w_out, conv_b_out, pool_w, pool_scale, sc_w_in, sc_dw, sc_w_out, ret_w_in, ret_w_out, loss_target, m_norm_g, m_mlp_up, m_mlp_down, m_conv_w_in, m_conv_b_in, m_conv_dw, m_conv_dw_b, m_conv_ln_g, m_conv_ln_b, m_conv_w_out, m_conv_b_out, m_pool_w, m_pool_scale, m_sc_w_in, m_sc_dw, m_sc_w_out, m_ret_w_in, m_ret_w_out, v_norm_g, v_mlp_up, v_mlp_down, v_conv_w_in, v_conv_b_in, v_conv_dw, v_conv_dw_b, v_conv_ln_g, v_conv_ln_b, v_conv_w_out, v_conv_b_out, v_pool_w, v_pool_scale, v_sc_w_in, v_sc_dw, v_sc_w_out, v_ret_w_in, v_ret_w_out):
    given = dict(x=x, positions=positions, norm_g=norm_g, mlp_up=mlp_up, mlp_down=mlp_down, conv_w_in=conv_w_in, conv_b_in=conv_b_in, conv_dw=conv_dw, conv_dw_b=conv_dw_b, conv_ln_g=conv_ln_g, conv_ln_b=conv_ln_b, conv_w_out=conv_w_out, conv_b_out=conv_b_out, pool_w=pool_w, pool_scale=pool_scale, sc_w_in=sc_w_in, sc_dw=sc_dw, sc_w_out=sc_w_out, ret_w_in=ret_w_in, ret_w_out=ret_w_out, loss_target=loss_target, m_norm_g=m_norm_g, m_mlp_up=m_mlp_up, m_mlp_down=m_mlp_down, m_conv_w_in=m_conv_w_in, m_conv_b_in=m_conv_b_in, m_conv_dw=m_conv_dw, m_conv_dw_b=m_conv_dw_b, m_conv_ln_g=m_conv_ln_g, m_conv_ln_b=m_conv_ln_b, m_conv_w_out=m_conv_w_out, m_conv_b_out=m_conv_b_out, m_pool_w=m_pool_w, m_pool_scale=m_pool_scale, m_sc_w_in=m_sc_w_in, m_sc_dw=m_sc_dw, m_sc_w_out=m_sc_w_out, m_ret_w_in=m_ret_w_in, m_ret_w_out=m_ret_w_out, v_norm_g=v_norm_g, v_mlp_up=v_mlp_up, v_mlp_down=v_mlp_down, v_conv_w_in=v_conv_w_in, v_conv_b_in=v_conv_b_in, v_conv_dw=v_conv_dw, v_conv_dw_b=v_conv_dw_b, v_conv_ln_g=v_conv_ln_g, v_conv_ln_b=v_conv_ln_b, v_conv_w_out=v_conv_w_out, v_conv_b_out=v_conv_b_out, v_pool_w=v_pool_w, v_pool_scale=v_pool_scale, v_sc_w_in=v_sc_w_in, v_sc_dw=v_sc_dw, v_sc_w_out=v_sc_w_out, v_ret_w_in=v_ret_w_in, v_ret_w_out=v_ret_w_out)
    weights = {n: given[n] for n in TWIN_WEIGHTS}
    shared = {n: given[n] for n in SHARED_INPUTS}
    per_example = {n: given[n] for n in ['x', 'positions']}
    grad_fn = _jax.value_and_grad(_loss, argnums=(0, 1))

    def one_microbatch(ex, loss_target):
        ex = dict(ex)
        diff = ex.pop(TWIN_DIFF_INPUT)
        return grad_fn(weights, diff, {**shared, **ex}, loss_target)

    if N_MICROBATCH == 1:
        loss, (grad_w, grad_x) = one_microbatch(per_example, given["loss_target"])
    else:
        def body(carry, xs):
            loss_sum, grad_sum = carry
            l_k, (gw_k, gx_k) = one_microbatch(xs[0], xs[1])
            with _jax.named_scope("update"):
                return (loss_sum + l_k, _jax.tree.map(_jnp.add, grad_sum, gw_k)), gx_k

        init = (_jnp.zeros((), _jnp.float32), _jax.tree.map(_jnp.zeros_like, weights))
        (loss, grad_w), grad_x = _jax.lax.scan(body, init, (per_example, given["loss_target"]))
    with _jax.named_scope("update"):
        delta_w, new_m, new_v = {}, {}, {}
        for n in TWIN_WEIGHTS:
            delta_w[n], new_m[n], new_v[n] = _adamw(weights[n], grad_w[n], given["m_" + n], given["v_" + n])
    return (loss, grad_x, *[grad_w[n] for n in TWIN_WEIGHTS], *[delta_w[n] for n in TWIN_WEIGHTS],
            *[new_m[n] for n in TWIN_WEIGHTS], *[new_v[n] for n in TWIN_WEIGHTS])
```

```python
import functools
import math

import numpy as np
import jax
import jax.numpy as jnp
from jax import lax
from jax.experimental import pallas as pl
from jax.experimental.pallas import tpu as pltpu

F32, BF16 = jnp.float32, jnp.bfloat16
MESH = pl.DeviceIdType.MESH
AXES = ("x", "y", "c")
N_DEV = 8
N_CHIP = 4
EPS = 1e-6
DEPTH = 4
CONV_WIDTH = 31
POOL_WINDOWS = (2, 4, 8, 16)
SHORT_CONV_WIDTH = 3
RET_HEADS = 4
RET_CHUNK = 128
ROPE_BASE = 10000.0
ADAM_LR, ADAM_B1, ADAM_B2, ADAM_EPS, ADAM_WD, ADAM_STEP = 0.001, 0.9, 0.999, 1e-08, 0.01, 10
LANES = 128
SEQ_PAD = 32
VMEM_LIMIT = 56 * 1024 * 1024

NN = (((1,), (0,)), ((), ()))
NT = (((1,), (1,)), ((), ()))
TN = (((0,), (0,)), ((), ()))


def _pcall(body, **kw):
    return pl.pallas_call(body, **kw)


def _params(sem):
    return pltpu.CompilerParams(dimension_semantics=sem, vmem_limit_bytes=VMEM_LIMIT)


def _tile(n, pref):
    t = min(n, pref)
    while n % t:
        t //= 2
    return t


def _rowwise(name, fn, ins, outs, rows, tr):
    n_in = len(ins)
    in_specs = []
    for arr, nc, cb in ins:
        if nc is None:
            in_specs.append(pl.BlockSpec(arr.shape, lambda i, nd=arr.ndim: (0,) * nd))
        else:
            in_specs.append(pl.BlockSpec((tr, nc), lambda i, cb=cb: (i, cb)))
    out_specs, out_shapes = [], []
    for o in outs:
        if o[0] == "row":
            out_shapes.append(jax.ShapeDtypeStruct((rows, o[1]), o[2]))
            out_specs.append(pl.BlockSpec((tr, o[1]), lambda i: (i, 0)))
        else:
            out_shapes.append(jax.ShapeDtypeStruct(o[1], F32))
            out_specs.append(pl.BlockSpec(o[1], lambda i: (0, 0)))

    def body(*refs):
        res = fn(*[r[...] for r in refs[:n_in]])
        if not isinstance(res, (tuple, list)):
            res = (res,)
        i = pl.program_id(0)
        for o, ref, r in zip(outs, refs[n_in:], res):
            if o[0] == "row":
                ref[...] = r.astype(o[2])
            else:
                @pl.when(i == 0)
                def _(ref=ref, r=r):
                    ref[...] = r

                @pl.when(i > 0)
                def _(ref=ref, r=r):
                    ref[...] += r

    return _pcall(body, name=name, grid=(rows // tr,), in_specs=in_specs, out_specs=out_specs,
                  out_shape=out_shapes, compiler_params=_params(("arbitrary",)))(*[a for a, _, _ in ins])


def _rms(x, g):
    return x * lax.rsqrt(jnp.mean(x * x, axis=-1, keepdims=True) + EPS) * g


def _colsum(x):
    return jnp.sum(x, axis=0, keepdims=True)


def _ln_silu(c, g, b):
    mu = jnp.mean(c, axis=-1, keepdims=True)
    xc = c - mu
    y = xc * lax.rsqrt(jnp.mean(xc * xc, axis=-1, keepdims=True) + EPS) * g + b
    return y * jax.nn.sigmoid(y)


def _glu(za, zg):
    return za * jax.nn.sigmoid(zg)


def _gate_norm(o, g):
    dv = o.shape[-1] // RET_HEADS
    parts = []
    for h in range(RET_HEADS):
        oh = o[:, h * dv:(h + 1) * dv]
        parts.append(oh * lax.rsqrt(jnp.mean(oh * oh, axis=-1, keepdims=True) + EPS))
    return (g * jax.nn.sigmoid(g)) * jnp.concatenate(parts, axis=-1)


def _rot(x, cos, sin, sign):
    dk = x.shape[-1] // RET_HEADS
    half = dk // 2
    parts = []
    for h in range(RET_HEADS):
        x1 = x[:, h * dk:h * dk + half]
        x2 = x[:, h * dk + half:(h + 1) * dk]
        parts.append(x1 * cos - sign * (x2 * sin))
        parts.append(sign * (x1 * sin) + x2 * cos)
    return jnp.concatenate(parts, axis=-1)


def _mm(name, mode, grid, ins, outs, epi=None, aliases=None):
    dn = {"nn": NN, "nt": NT, "tn": TN}[mode]
    gk = grid[2]
    n_in, n_out = len(ins), len(outs)

    def body(*refs):
        out_refs = refs[n_in:n_in + n_out]
        part = lax.dot_general(refs[0][...], refs[1][...], dn, preferred_element_type=F32)

        def finish(acc):
            res = epi(acc, *[r[...] for r in refs[2:n_in]]) if epi is not None else (acc,)
            for o, r in zip(out_refs, res):
                o[...] = r.astype(o.dtype)

        if gk == 1:
            finish(part)
        else:
            acc_ref = refs[-1]
            k = pl.program_id(2)

            @pl.when(k == 0)
            def _():
                acc_ref[...] = part

            @pl.when(k > 0)
            def _():
                acc_ref[...] += part

            @pl.when(k == gk - 1)
            def _():
                finish(acc_ref[...])

    scratch = []
    if gk > 1:
        blk = [d for d in outs[0][2].block_shape if d is not None]
        scratch = [pltpu.VMEM(tuple(blk), F32)]
    kw = {}
    if aliases:
        kw["input_output_aliases"] = aliases
    return _pcall(body, name=name, grid=grid, in_specs=[s for _, s in ins], out_specs=[s for _, _, s in outs],
                  out_shape=[jax.ShapeDtypeStruct(sh, dt) for sh, dt, _ in outs], scratch_shapes=scratch,
                  compiler_params=_params(("parallel", "parallel", "arbitrary")), **kw)(*[a for a, _ in ins])


def _spec(block, fn):
    return pl.BlockSpec(block, fn)


def _mm_cols(name, a, wg, lead, T, epi=None, extras=(), n_out=1, out_dtype=F32):
    K, nb = wg.shape[-2], wg.shape[-1]
    tm = _tile(T, 1024)
    wblock = (None,) * (1 + len(lead)) + (K, nb)
    ins = [(a, _spec((tm, K), lambda i, j, k: (i, 0))), (wg, _spec(wblock, lambda i, j, k: (j,) + tuple(lead) + (0, 0)))]
    for e in extras:
        if e.shape[0] == 1:
            ins.append((e, _spec((1, nb), lambda i, j, k: (0, j))))
        else:
            ins.append((e, _spec((tm, nb), lambda i, j, k: (i, j))))
    outs = [((T, N_DEV * nb), out_dtype, _spec((tm, nb), lambda i, j, k: (i, j))) for _ in range(n_out)]
    return _mm(name, "nn", (T // tm, N_DEV, 1), ins, outs, epi)


def _mm_cols_t(name, d, wg, lead, T, epi=None, extras=(), out_dtype=F32):
    K, nb = wg.shape[-2], wg.shape[-1]
    tm = _tile(T, 1024)
    wblock = (None,) * (1 + len(lead)) + (K, nb)
    ins = [(d, _spec((tm, nb), lambda i, j, k: (i, k))), (wg, _spec(wblock, lambda i, j, k: (k,) + tuple(lead) + (0, 0)))]
    for e in extras:
        ins.append((e, _spec((tm, K), lambda i, j, k: (i, 0))))
    outs = [((T, K), out_dtype, _spec((tm, K), lambda i, j, k: (i, 0)))]
    return _mm(name, "nt", (T // tm, 1, N_DEV), ins, outs, epi)[0]


def _mm_cols_grad(name, a, d, wg_shape, lead, T, into=None):
    K, nb = wg_shape[-2], wg_shape[-1]
    tk = _tile(T, 1024)
    wblock = (None,) * (1 + len(lead)) + (K, nb)
    ins = [(a, _spec((tk, K), lambda i, j, k: (k, 0))), (d, _spec((tk, nb), lambda i, j, k: (k, j)))]
    aliases = None
    if into is not None:
        ins.append((into, pl.BlockSpec(memory_space=pl.ANY)))
        aliases = {2: 0}
    outs = [(tuple(wg_shape), BF16, _spec(wblock, lambda i, j, k: (j,) + tuple(lead) + (0, 0)))]
    epi = (lambda acc, *_: (acc,)) if into is None else None
    return _mm_alias(name, "tn", (1, N_DEV, T // tk), ins, outs, aliases) if into is not None else \
        _mm(name, "tn", (1, N_DEV, T // tk), ins, outs, epi)[0]


def _mm_alias(name, mode, grid, ins, outs, aliases):
    dn = {"nn": NN, "nt": NT, "tn": TN}[mode]
    gk = grid[2]

    def body(a_ref, b_ref, _into, o_ref, acc_ref):
        part = lax.dot_general(a_ref[...], b_ref[...], dn, preferred_element_type=F32)
        k = pl.program_id(2)

        @pl.when(k == 0)
        def _():
            acc_ref[...] = part

        @pl.when(k > 0)
        def _():
            acc_ref[...] += part

        @pl.when(k == gk - 1)
        def _():
            o_ref[...] = acc_ref[...].astype(o_ref.dtype)

    blk = [d for d in outs[0][2].block_shape if d is not None]
    return _pcall(body, name=name, grid=grid, in_specs=[s for _, s in ins], out_specs=[s for _, _, s in outs],
                  out_shape=[jax.ShapeDtypeStruct(sh, dt) for sh, dt, _ in outs],
                  scratch_shapes=[pltpu.VMEM(tuple(blk), F32)], input_output_aliases=aliases,
                  compiler_params=_params(("parallel", "parallel", "arbitrary")))(*[a for a, _ in ins])[0]


def _mm_rows(name, a, w2, T, epi=None, extras=(), out_dtype=F32):
    Kin, N = w2.shape
    tm, tn = _tile(T, 1024), _tile(N, 512)
    ins = [(a, _spec((tm, Kin), lambda i, j, k: (i, 0))), (w2, _spec((Kin, tn), lambda i, j, k: (0, j)))]
    for e in extras:
        ins.append((e, _spec((1, tn), lambda i, j, k: (0, j))))
    outs = [((T, N), out_dtype, _spec((tm, tn), lambda i, j, k: (i, j)))]
    return _mm(name, "nn", (T // tm, N // tn, 1), ins, outs, epi)[0]


def _mm_rows_t(name, d, w2, T, out_dtype=F32):
    Kin, N = w2.shape
    tm, tn = _tile(T, 1024), _tile(Kin, 512)
    ins = [(d, _spec((tm, N), lambda i, j, k: (i, 0))), (w2, _spec((tn, N), lambda i, j, k: (j, 0)))]
    outs = [((T, Kin), out_dtype, _spec((tm, tn), lambda i, j, k: (i, j)))]
    return _mm(name, "nt", (T // tm, Kin // tn, 1), ins, outs)[0]


def _mm_rows_grad(name, a, d, T):
    Kin, N = a.shape[1], d.shape[1]
    tk, tm, tn = _tile(T, 1024), _tile(Kin, 512), _tile(N, 1024)
    ins = [(a, _spec((tk, tm), lambda i, j, k: (k, i))), (d, _spec((tk, tn), lambda i, j, k: (k, j)))]
    outs = [((Kin, N), BF16, _spec((tm, tn), lambda i, j, k: (i, j)))]
    return _mm(name, "tn", (Kin // tm, N // tn, T // tk), ins, outs, lambda acc: (acc,))[0]


def _dwconv_fwd(name, x, w, bias, T):
    W, C = w.shape
    cb = min(C, LANES)
    tc = _tile(T, 256)

    def body(x_ref, w_ref, b_ref, o_ref, xp_ref):
        xp_ref[pl.ds(0, SEQ_PAD), :] = jnp.zeros((SEQ_PAD, cb), F32)

        def fill(c, _):
            base = pl.multiple_of(c * tc, tc)
            xp_ref[pl.ds(base + SEQ_PAD, tc), :] = x_ref[pl.ds(base, tc), :]
            return 0

        lax.fori_loop(0, T // tc, fill, 0)

        def chunk(c, _):
            base = pl.multiple_of(c * tc, tc)
            acc = jnp.zeros((tc, cb), F32) + b_ref[...]
            for k in range(W):
                acc = acc + w_ref[pl.ds(k, 1), :] * xp_ref[pl.ds(base + SEQ_PAD - (W - 1) + k, tc), :]
            o_ref[pl.ds(base, tc), :] = acc
            return 0

        lax.fori_loop(0, T // tc, chunk, 0)

    return _pcall(body, name=name, grid=(C // cb,),
                  in_specs=[pl.BlockSpec((T, cb), lambda i: (0, i)), pl.BlockSpec((W, cb), lambda i: (0, i)),
                            pl.BlockSpec((1, cb), lambda i: (0, i))],
                  out_specs=pl.BlockSpec((T, cb), lambda i: (0, i)), out_shape=jax.ShapeDtypeStruct((T, C), F32),
                  scratch_shapes=[pltpu.VMEM((T + SEQ_PAD, cb), F32)],
                  compiler_params=_params(("parallel",)))(x, w, bias)


def _dwconv_bwd(name, dc, x, w, T):
    W, C = w.shape
    cb = min(C, LANES)
    tc = _tile(T, 128)

    def body(dc_ref, x_ref, w_ref, dx_ref, dw_ref, xp_ref, dp_ref, acc_ref):
        xp_ref[pl.ds(0, SEQ_PAD), :] = jnp.zeros((SEQ_PAD, cb), F32)
        dp_ref[pl.ds(T, SEQ_PAD), :] = jnp.zeros((SEQ_PAD, cb), F32)
        acc_ref[...] = jnp.zeros((W * 8, cb), F32)

        def fill(c, _):
            base = pl.multiple_of(c * tc, tc)
            xp_ref[pl.ds(base + SEQ_PAD, tc), :] = x_ref[pl.ds(base, tc), :]
            dp_ref[pl.ds(base, tc), :] = dc_ref[pl.ds(base, tc), :]
            return 0

        lax.fori_loop(0, T // tc, fill, 0)

        def chunk(c, _):
            base = pl.multiple_of(c * tc, tc)
            d = dc_ref[pl.ds(base, tc), :]
            acc = jnp.zeros((tc, cb), F32)
            for k in range(W):
                acc = acc + w_ref[pl.ds(k, 1), :] * dp_ref[pl.ds(base + (W - 1) - k, tc), :]
                prod = d * xp_ref[pl.ds(base + SEQ_PAD - (W - 1) + k, tc), :]
                acc_ref[pl.ds(8 * k, 8), :] += jnp.sum(prod.reshape(tc // 8, 8, cb), axis=0)
            dx_ref[pl.ds(base, tc), :] = acc
            return 0

        lax.fori_loop(0, T // tc, chunk, 0)
        for k in range(W):
            dw_ref[pl.ds(k, 1), :] = jnp.sum(acc_ref[pl.ds(8 * k, 8), :], axis=0, keepdims=True)

    return _pcall(body, name=name, grid=(C // cb,),
                  in_specs=[pl.BlockSpec((T, cb), lambda i: (0, i)), pl.BlockSpec((T, cb), lambda i: (0, i)),
                            pl.BlockSpec((W, cb), lambda i: (0, i))],
                  out_specs=[pl.BlockSpec((T, cb), lambda i: (0, i)), pl.BlockSpec((W, cb), lambda i: (0, i))],
                  out_shape=[jax.ShapeDtypeStruct((T, C), F32), jax.ShapeDtypeStruct((W, C), F32)],
                  scratch_shapes=[pltpu.VMEM((T + SEQ_PAD, cb), F32), pltpu.VMEM((T + SEQ_PAD, cb), F32),
                                  pltpu.VMEM((W * 8, cb), F32)],
                  compiler_params=_params(("parallel",)))(dc, x, w)


def _pool_count(base, tc, cg, win):
    t = (lax.broadcasted_iota(jnp.int32, (tc, cg), 0) + base + 1).astype(F32)
    return jnp.minimum(t, float(win))


def _pool_fwd(name, u, T):
    D = u.shape[1]
    G = len(POOL_WINDOWS)
    cg = min(D // G, LANES)
    nb = (D // G) // cg
    tc = _tile(T, 128)

    def body(u_ref, o_ref, up_ref):
        g = pl.program_id(0)
        up_ref[pl.ds(0, SEQ_PAD), :] = jnp.zeros((SEQ_PAD, cg), F32)

        def fill(c, _):
            base = pl.multiple_of(c * tc, tc)
            up_ref[pl.ds(base + SEQ_PAD, tc), :] = u_ref[pl.ds(base, tc), :]
            return 0

        lax.fori_loop(0, T // tc, fill, 0)
        for gi, win in enumerate(POOL_WINDOWS):
            @pl.when(g == gi)
            def _(win=win):
                def chunk(c, _):
                    base = pl.multiple_of(c * tc, tc)
                    acc = up_ref[pl.ds(base + SEQ_PAD, tc), :]
                    for j in range(1, win):
                        acc = acc + up_ref[pl.ds(base + SEQ_PAD - j, tc), :]
                    mixed = acc / _pool_count(base, tc, cg, win) - u_ref[pl.ds(base, tc), :]
                    o_ref[pl.ds(base, tc), :] = mixed.astype(BF16)
                    return 0

                lax.fori_loop(0, T // tc, chunk, 0)

    return _pcall(body, name=name, grid=(G, nb), in_specs=[pl.BlockSpec((T, cg), lambda g, i: (0, g * nb + i))],
                  out_specs=pl.BlockSpec((T, cg), lambda g, i: (0, g * nb + i)), out_shape=jax.ShapeDtypeStruct((T, D), BF16),
                  scratch_shapes=[pltpu.VMEM((T + SEQ_PAD, cg), F32)], compiler_params=_params(("parallel", "parallel")))(u)


def _pool_bwd(name, dm, T):
    D = dm.shape[1]
    G = len(POOL_WINDOWS)
    cg = min(D // G, LANES)
    nb = (D // G) // cg
    tc = _tile(T, 128)

    def body(d_ref, o_ref, qp_ref):
        g = pl.program_id(0)
        qp_ref[pl.ds(T, SEQ_PAD), :] = jnp.zeros((SEQ_PAD, cg), F32)
        for gi, win in enumerate(POOL_WINDOWS):
            @pl.when(g == gi)
            def _(win=win):
                def fill(c, _):
                    base = pl.multiple_of(c * tc, tc)
                    qp_ref[pl.ds(base, tc), :] = d_ref[pl.ds(base, tc), :] / _pool_count(base, tc, cg, win)
                    return 0

                lax.fori_loop(0, T // tc, fill, 0)

                def chunk(c, _):
                    base = pl.multiple_of(c * tc, tc)
                    acc = qp_ref[pl.ds(base, tc), :]
                    for j in range(1, win):
                        acc = acc + qp_ref[pl.ds(base + j, tc), :]
                    o_ref[pl.ds(base, tc), :] = acc - d_ref[pl.ds(base, tc), :]
                    return 0

                lax.fori_loop(0, T // tc, chunk, 0)

    return _pcall(body, name=name, grid=(G, nb), in_specs=[pl.BlockSpec((T, cg), lambda g, i: (0, g * nb + i))],
                  out_specs=pl.BlockSpec((T, cg), lambda g, i: (0, g * nb + i)), out_shape=jax.ShapeDtypeStruct((T, D), F32),
                  scratch_shapes=[pltpu.VMEM((T + SEQ_PAD, cg), F32)], compiler_params=_params(("parallel", "parallel")))(dm)


def _pool_mm(name, mode, a, b, T, scale=None):
    G = len(POOL_WINDOWS)
    D = a.shape[1]
    cg = D // G
    tm = _tile(T, 1024)
    if mode == "tn":
        ins = [(a, _spec((tm, cg), lambda i, j, k: (k, j))), (b, _spec((tm, cg), lambda i, j, k: (k, j)))]
        outs = [((G, cg, cg), F32, _spec((None, cg, cg), lambda i, j, k: (j, 0, 0)))]
        return _mm(name, "tn", (1, G, T // tm), ins, outs, lambda acc: (acc,))[0]
    ins = [(a, _spec((tm, cg), lambda i, j, k: (i, j))), (b, _spec((None, cg, cg), lambda i, j, k: (j, 0, 0)))]
    if scale is not None:
        ins.append((scale, _spec((1, cg), lambda i, j, k: (0, j))))
        outs = [((T, D), F32, _spec((tm, cg), lambda i, j, k: (i, j))), ((T, D), BF16, _spec((tm, cg), lambda i, j, k: (i, j)))]
        return _mm(name, mode, (T // tm, G, 1), ins, outs, lambda acc, s: (acc * s, acc))
    outs = [((T, D), F32, _spec((tm, cg), lambda i, j, k: (i, j)))]
    return _mm(name, mode, (T // tm, G, 1), ins, outs)[0]


def _sc_mid_fwd(name, z, w, T):
    D = w.shape[1]
    W = w.shape[0]
    cb = min(D, LANES)
    nb = D // cb
    tc = _tile(T, 256)

    def body(b_ref, c_ref, v_ref, w_ref, o_ref, pp_ref):
        pp_ref[pl.ds(0, SEQ_PAD), :] = jnp.zeros((SEQ_PAD, cb), F32)

        def fill(c, _):
            base = pl.multiple_of(c * tc, tc)
            pp_ref[pl.ds(base + SEQ_PAD, tc), :] = c_ref[pl.ds(base, tc), :] * v_ref[pl.ds(base, tc), :]
            return 0

        lax.fori_loop(0, T // tc, fill, 0)

        def chunk(c, _):
            base = pl.multiple_of(c * tc, tc)
            q = jnp.zeros((tc, cb), F32)
            for k in range(W):
                q = q + w_ref[pl.ds(k, 1), :] * pp_ref[pl.ds(base + SEQ_PAD - (W - 1) + k, tc), :]
            o_ref[pl.ds(base, tc), :] = (b_ref[pl.ds(base, tc), :] * q).astype(BF16)
            return 0

        lax.fori_loop(0, T // tc, chunk, 0)

    return _pcall(body, name=name, grid=(nb,),
                  in_specs=[pl.BlockSpec((T, cb), lambda i: (0, i)), pl.BlockSpec((T, cb), lambda i: (0, nb + i)),
                            pl.BlockSpec((T, cb), lambda i: (0, 2 * nb + i)), pl.BlockSpec((W, cb), lambda i: (0, i))],
                  out_specs=pl.BlockSpec((T, cb), lambda i: (0, i)), out_shape=jax.ShapeDtypeStruct((T, D), BF16),
                  scratch_shapes=[pltpu.VMEM((T + SEQ_PAD, cb), F32)],
                  compiler_params=_params(("parallel",)))(z, z, z, w)


def _sc_mid_bwd(name, z, dm, w, T):
    D = w.shape[1]
    W = w.shape[0]
    cb = min(D, LANES)
    nb = D // cb
    tc = _tile(T, 128)

    def body(b_ref, c_ref, v_ref, d_ref, w_ref, dz_ref, dw_ref, pp_ref, dq_ref, acc_ref):
        part = pl.program_id(0)
        pp_ref[pl.ds(0, SEQ_PAD), :] = jnp.zeros((SEQ_PAD, cb), F32)
        dq_ref[pl.ds(T, SEQ_PAD), :] = jnp.zeros((SEQ_PAD, cb), F32)
        acc_ref[...] = jnp.zeros((W * 8, cb), F32)

        def fill(c, _):
            base = pl.multiple_of(c * tc, tc)
            pp_ref[pl.ds(base + SEQ_PAD, tc), :] = c_ref[pl.ds(base, tc), :] * v_ref[pl.ds(base, tc), :]
            dq_ref[pl.ds(base, tc), :] = d_ref[pl.ds(base, tc), :] * b_ref[pl.ds(base, tc), :]
            return 0

        lax.fori_loop(0, T // tc, fill, 0)

        @pl.when(part == 0)
        def _():
            def chunk(c, _):
                base = pl.multiple_of(c * tc, tc)
                dq = dq_ref[pl.ds(base, tc), :]
                q = jnp.zeros((tc, cb), F32)
                for k in range(W):
                    p = pp_ref[pl.ds(base + SEQ_PAD - (W - 1) + k, tc), :]
                    q = q + w_ref[pl.ds(k, 1), :] * p
                    acc_ref[pl.ds(8 * k, 8), :] += jnp.sum((dq * p).reshape(tc // 8, 8, cb), axis=0)
                dz_ref[pl.ds(base, tc), :] = (d_ref[pl.ds(base, tc), :] * q).astype(BF16)
                return 0

            lax.fori_loop(0, T // tc, chunk, 0)

        @pl.when(part > 0)
        def _():
            def chunk(c, _):
                base = pl.multiple_of(c * tc, tc)
                dp = jnp.zeros((tc, cb), F32)
                for k in range(W):
                    dp = dp + w_ref[pl.ds(k, 1), :] * dq_ref[pl.ds(base + (W - 1) - k, tc), :]
                other = jnp.where(part == 1, v_ref[pl.ds(base, tc), :], c_ref[pl.ds(base, tc), :])
                dz_ref[pl.ds(base, tc), :] = (dp * other).astype(BF16)
                return 0

            lax.fori_loop(0, T // tc, chunk, 0)

        for k in range(W):
            dw_ref[pl.ds(k, 1), :] = jnp.sum(acc_ref[pl.ds(8 * k, 8), :], axis=0, keepdims=True)

    dz, dw3 = _pcall(body, name=name, grid=(3, nb),
                     in_specs=[pl.BlockSpec((T, cb), lambda p, i: (0, i)), pl.BlockSpec((T, cb), lambda p, i: (0, nb + i)),
                               pl.BlockSpec((T, cb), lambda p, i: (0, 2 * nb + i)), pl.BlockSpec((T, cb), lambda p, i: (0, i)),
                               pl.BlockSpec((W, cb), lambda p, i: (0, i))],
                     out_specs=[pl.BlockSpec((T, cb), lambda p, i: (0, p * nb + i)),
                                pl.BlockSpec((None, W, cb), lambda p, i: (p, 0, i))],
                     out_shape=[jax.ShapeDtypeStruct((T, 3 * D), BF16), jax.ShapeDtypeStruct((3, W, D), F32)],
                     scratch_shapes=[pltpu.VMEM((T + SEQ_PAD, cb), F32), pltpu.VMEM((T + SEQ_PAD, cb), F32),
                                     pltpu.VMEM((W * 8, cb), F32)],
                     compiler_params=_params(("parallel", "parallel")))(z, z, z, dm, w)
    return dz, dw3


def _ret_tables(dk, dv):
    C = RET_CHUNK
    lg = np.log1p(-np.exp2(-5.0 - np.arange(RET_HEADS, dtype=np.float64)))
    idx = np.arange(C, dtype=np.float64)
    rel = idx[:, None] - idx[None, :]
    mask = np.where(rel >= 0, np.exp(lg[:, None, None] * np.maximum(rel, 0.0)), 0.0)
    qd = np.exp(lg[:, None] * (idx + 1.0))[:, :, None] * np.ones((1, 1, dk))
    kd = np.exp(lg[:, None] * (C - 1.0 - idx))[:, :, None] * np.ones((1, 1, dk))
    cd = np.exp(lg * C)[:, None, None] * np.ones((1, 1, dv))
    return tuple(jnp.asarray(a, F32) for a in (mask, qd, kd, cd))


def _ret_fwd(name, q, k, v, tabs, T):
    H, C = RET_HEADS, RET_CHUNK
    dk, dv = q.shape[1] // H, v.shape[1] // H
    nch = T // C
    mask, qd, kd, cd = tabs

    def body(q_ref, k_ref, v_ref, m_ref, qd_ref, kd_ref, cd_ref, o_ref, st_ref, s_ref):
        n = pl.program_id(1)

        @pl.when(n == 0)
        def _():
            s_ref[...] = jnp.zeros((dk, dv), F32)

        qv, kv, vv = q_ref[...], k_ref[...], v_ref[...]
        s = s_ref[...]
        sb = s.astype(BF16)
        st_ref[...] = sb
        scores = lax.dot_general(qv, kv, NT, preferred_element_type=F32) * m_ref[...]
        intra = lax.dot_general(scores.astype(BF16), vv, NN, preferred_element_type=F32)
        cross = lax.dot_general((qv.astype(F32) * qd_ref[...]).astype(BF16), sb, NN, preferred_element_type=F32)
        o_ref[...] = intra + cross
        upd = lax.dot_general((kv.astype(F32) * kd_ref[...]).astype(BF16), vv, TN, preferred_element_type=F32)
        s_ref[...] = s * cd_ref[...] + upd

    return _pcall(body, name=name, grid=(H, nch),
                  in_specs=[pl.BlockSpec((C, dk), lambda h, n: (n, h)), pl.BlockSpec((C, dk), lambda h, n: (n, h)),
                            pl.BlockSpec((C, dv), lambda h, n: (n, h)), pl.BlockSpec((None, C, C), lambda h, n: (h, 0, 0)),
                            pl.BlockSpec((None, C, dk), lambda h, n: (h, 0, 0)), pl.BlockSpec((None, C, dk), lambda h, n: (h, 0, 0)),
                            pl.BlockSpec((None, 1, dv), lambda h, n: (h, 0, 0))],
                  out_specs=[pl.BlockSpec((C, dv), lambda h, n: (n, h)),
                             pl.BlockSpec((None, None, dk, dv), lambda h, n: (h, n, 0, 0))],
                  out_shape=[jax.ShapeDtypeStruct((T, H * dv), F32), jax.ShapeDtypeStruct((H, nch, dk, dv), BF16)],
                  scratch_shapes=[pltpu.VMEM((dk, dv), F32)],
                  compiler_params=_params(("parallel", "arbitrary")))(q, k, v, mask, qd, kd, cd)


def _ret_bwd(name, q, k, v, do, st, tabs, T):
    H, C = RET_HEADS, RET_CHUNK
    dk, dv = q.shape[1] // H, v.shape[1] // H
    nch = T // C
    mask, qd, kd, cd = tabs

    def body(q_ref, k_ref, v_ref, do_ref, st_ref, m_ref, qd_ref, kd_ref, cd_ref, dq_ref, dk_ref, dv_ref, ds_ref):
        n = pl.program_id(1)

        @pl.when(n == 0)
        def _():
            ds_ref[...] = jnp.zeros((dk, dv), F32)

        qv, kv, vv, dov, sb = q_ref[...], k_ref[...], v_ref[...], do_ref[...], st_ref[...]
        m = m_ref[...]
        ds = ds_ref[...]
        dsb = ds.astype(BF16)
        a = (lax.dot_general(qv, kv, NT, preferred_element_type=F32) * m).astype(BF16)
        dp = (lax.dot_general(dov, vv, NT, preferred_element_type=F32) * m).astype(BF16)
        qa = (qv.astype(F32) * qd_ref[...]).astype(BF16)
        kb = (kv.astype(F32) * kd_ref[...]).astype(BF16)
        dq_ref[...] = (lax.dot_general(dp, kv, NN, preferred_element_type=F32)
                       + qd_ref[...] * lax.dot_general(dov, sb, NT, preferred_element_type=F32))
        dk_ref[...] = (lax.dot_general(dp, qv, TN, preferred_element_type=F32)
                       + kd_ref[...] * lax.dot_general(vv, dsb, NT, preferred_element_type=F32))
        dv_ref[...] = (lax.dot_general(a, dov, TN, preferred_element_type=F32)
                       + lax.dot_general(kb, dsb, NN, preferred_element_type=F32)).astype(BF16)
        ds_ref[...] = ds * cd_ref[...] + lax.dot_general(qa, dov, TN, preferred_element_type=F32)

    rev = lambda h, n: (nch - 1 - n, h)
    return _pcall(body, name=name, grid=(H, nch),
                  in_specs=[pl.BlockSpec((C, dk), rev), pl.BlockSpec((C, dk), rev), pl.BlockSpec((C, dv), rev),
                            pl.BlockSpec((C, dv), rev), pl.BlockSpec((None, None, dk, dv), lambda h, n: (h, nch - 1 - n, 0, 0)),
                            pl.BlockSpec((None, C, C), lambda h, n: (h, 0, 0)), pl.BlockSpec((None, C, dk), lambda h, n: (h, 0, 0)),
                            pl.BlockSpec((None, C, dk), lambda h, n: (h, 0, 0)), pl.BlockSpec((None, 1, dv), lambda h, n: (h, 0, 0))],
                  out_specs=[pl.BlockSpec((C, dk), rev), pl.BlockSpec((C, dk), rev), pl.BlockSpec((C, dv), rev)],
                  out_shape=[jax.ShapeDtypeStruct((T, H * dk), F32), jax.ShapeDtypeStruct((T, H * dk), F32),
                             jax.ShapeDtypeStruct((T, H * dv), BF16)],
                  scratch_shapes=[pltpu.VMEM((dk, dv), F32)],
                  compiler_params=_params(("parallel", "arbitrary")))(q, k, v, do, st, mask, qd, kd, cd)


def _local_step(x, pos, tgt, wts, T, D):
    ng = wts["norm_g"]
    tr = _tile(T, 256)
    trw = _tile(T, 128)
    row = lambda i: ng[i:i + 1]
    g = {}

    u = _rowwise("rms_first", lambda xv, gv: _rms(xv, gv), [(x, D, 0), (row(0), None, None)], [("row", D, BF16)], T, tr)[0]
    h = x
    saved = []
    cos = sin = None
    tabs = None
    for l in range(DEPTH):
        mixer = l % 4
        sv = {"h_in": h, "u": u}
        if mixer == 0:
            z = _mm_cols("conv_in", u, wts["conv_w_in"], (), T, lambda acc, b: (acc + b,), (wts["conv_b_in"],))[0]
            v = _rowwise("conv_glu", _glu, [(z, D, 0), (z, D, 1)], [("row", D, F32)], T, tr)[0]
            c = _dwconv_fwd("conv_dw", v, wts["conv_dw"], wts["conv_dw_b"], T)
            s = _rowwise("conv_ln", _ln_silu, [(c, D, 0), (wts["conv_ln_g"], None, None), (wts["conv_ln_b"], None, None)],
                         [("row", D, BF16)], T, tr)[0]
            y = _mm_rows("conv_out", s, wts["conv_w_out"], T, lambda acc, b: (acc + b,), (wts["conv_b_out"],))
            sv.update(z=z, v=v, c=c, s=s)
        elif mixer == 1:
            mixed = _pool_fwd("pool_fwd", u, T)
            y, ycat = _pool_mm("pool_mm", "nn", mixed, wts["pool_w"], T, wts["pool_scale"])
            sv.update(mixed=mixed, ycat=ycat)
        elif mixer == 2:
            z = _mm_cols("sc_in", u, wts["sc_w_in"], (), T)[0]
            m = _sc_mid_fwd("sc_mid", z, wts["sc_dw"], T)
            y = _mm_rows("sc_out", m, wts["sc_w_out"], T)
            sv.update(z=z, m=m)
        else:
            dk = D // RET_HEADS
            half = dk // 2
            inv_freq = jnp.asarray((ROPE_BASE ** (-np.arange(half, dtype=np.float64) / half)).astype(np.float32)).reshape(1, half)

            def _cs(p, f):
                ang = p * f
                return jnp.cos(ang), jnp.sin(ang)

            cos, sin = _rowwise("rope_tab", _cs, [(pos, 1, 0), (inv_freq, None, None)],
                                [("row", half, F32), ("row", half, F32)], T, tr)
            tabs = _ret_tables(dk, 2 * dk)
            z = _mm_cols("ret_in", u, wts["ret_w_in"], (), T)[0]

            def _pre(zq, zk, zv, cv, sn):
                return _rot(zq, cv, sn, 1.0), _rot(zk, cv, sn, 1.0) * (dk ** -0.5), zv

            qr, kr, vb = _rowwise("ret_pre", _pre, [(z, D, 0), (z, D, 1), (z, 2 * D, 1), (cos, half, 0), (sin, half, 0)],
                                  [("row", D, BF16), ("row", D, BF16), ("row", 2 * D, BF16)], T, trw)
            o, st = _ret_fwd("ret_fwd", qr, kr, vb, tabs, T)
            out = _rowwise("ret_post", _gate_norm, [(o, 2 * D, 0), (z, 2 * D, 2)], [("row", 2 * D, BF16)], T, trw)[0]
            y = _mm_rows("ret_out", out, wts["ret_w_out"], T)
            sv.update(z=z, qr=qr, kr=kr, vb=vb, o=o, st=st, out=out)
        sv["y1"] = y
        h, u = _rowwise("bound_a", lambda hv, yv, gp, gn: (lambda hn: (hn, _rms(hn, gn)))(hv + _rms(yv, gp)),
                        [(h, D, 0), (y, D, 0), (row(4 * l + 1), None, None), (row(4 * l + 2), None, None)],
                        [("row", D, F32), ("row", D, BF16)], T, tr)
        sv["h_mid"], sv["u2"] = h, u
        r, hid = _mm_cols("mlp_up", u, wts["mlp_up"], (l,), T,
                          lambda acc: (lambda rr: (rr, rr * rr))(jnp.maximum(acc, 0.0)), (), 2, BF16)
        tm = _tile(T, 1024)
        y = _mm("mlp_down", "nn", (T // tm, 1, N_DEV),
                [(hid, _spec((tm, hid.shape[1] // N_DEV), lambda i, j, k: (i, k))),
                 (wts["mlp_down"], _spec((None, None, hid.shape[1] // N_DEV, D), lambda i, j, k, l=l: (k, l, 0, 0)))],
                [((T, D), F32, _spec((tm, D), lambda i, j, k: (i, 0)))])[0]
        sv.update(r=r, hid=hid, y2=y)
        if l + 1 < DEPTH:
            udt = F32 if (l + 1) % 4 == 1 else BF16
            h, u = _rowwise("bound_b", lambda hv, yv, gp, gn: (lambda hn: (hn, _rms(hn, gn)))(hv + _rms(yv, gp)),
                            [(h, D, 0), (y, D, 0), (row(4 * l + 3), None, None), (row(4 * l + 4), None, None)],
                            [("row", D, F32), ("row", D, udt)], T, tr)
        saved.append(sv)

    def _final(hv, yv, gp, tv):
        hn = hv + _rms(yv, gp)
        e = hn - tv
        return e * (1.0 / D), _colsum(e * e)

    dh, loss_row = _rowwise("final", _final, [(h, D, 0), (y, D, 0), (row(4 * DEPTH - 1), None, None), (tgt, D, 0)],
                            [("row", D, F32), ("acc", (1, D))], T, tr)

    gn = [None] * (4 * DEPTH)
    F = saved[0]["hid"].shape[1]
    g["mlp_up"] = jnp.zeros(wts["mlp_up"].shape, BF16)
    g["mlp_down"] = jnp.zeros(wts["mlp_down"].shape, BF16)
    du_next = None
    for l in reversed(range(DEPTH)):
        sv = saved[l]
        mixer = l % 4
        if du_next is None:
            def _bb(dhv, yv, gp):
                _, vjp = jax.vjp(_rms, yv, gp)
                dy, dgp = vjp(dhv)
                return dy, dgp

            dy, gn[4 * l + 3] = _rowwise("bwd_last", _bb, [(dh, D, 0), (sv["y2"], D, 0), (row(4 * l + 3), None, None)],
                                         [("row", D, BF16), ("acc", (1, D))], T, tr)
        else:
            dh, dy, gn[4 * l + 3], gn[4 * l + 4], _ = _bound_bwd("bwd_b", dh, du_next, saved[l + 1]["h_in"], sv["y2"],
                                                              row(4 * l + 3), row(4 * l + 4), T, D, tr)
        F8 = F // N_DEV
        tm = _tile(T, 1024)
        da = _mm("mlp_dhid", "nt", (T // tm, N_DEV, 1),
                 [(dy, _spec((tm, D), lambda i, j, k: (i, 0))),
                  (wts["mlp_down"], _spec((None, None, F8, D), lambda i, j, k, l=l: (j, l, 0, 0))),
                  (sv["r"], _spec((tm, F8), lambda i, j, k: (i, j)))],
                 [((T, F), BF16, _spec((tm, F8), lambda i, j, k: (i, j)))],
                 lambda acc, rr: (acc * (2.0 * rr.astype(F32)),))[0]
        tk = _tile(T, 1024)
        g["mlp_down"] = _mm_alias("mlp_dwdown", "tn", (1, N_DEV, T // tk),
                                  [(sv["hid"], _spec((tk, F8), lambda i, j, k: (k, j))), (dy, _spec((tk, D), lambda i, j, k: (k, 0))),
                                   (g["mlp_down"], pl.BlockSpec(memory_space=pl.ANY))],
                                  [(wts["mlp_down"].shape, BF16, _spec((None, None, F8, D), lambda i, j, k, l=l: (j, l, 0, 0)))], {2: 0})
        g["mlp_up"] = _mm_cols_grad("mlp_dwup", sv["u2"], da, wts["mlp_up"].shape, (l,), T, into=g["mlp_up"])
        du = _mm_cols_t("mlp_du", da, wts["mlp_up"], (l,), T)
        dh, dy, gn[4 * l + 1], gn[4 * l + 2], dy_sum = _bound_bwd("bwd_a", dh, du, sv["h_mid"], sv["y1"],
                                                                row(4 * l + 1), row(4 * l + 2), T, D, tr)
        u = sv["u"]
        if mixer == 0:
            g["conv_b_out"] = dy_sum
            ds = _mm_rows_t("conv_ds", dy, wts["conv_w_out"], T)
            g["conv_w_out"] = _mm_rows_grad("conv_dwout", sv["s"], dy, T)

            def _lnb(cv, gv, bv, dsv):
                _, vjp = jax.vjp(_ln_silu, cv, gv, bv)
                dc, dg, db = vjp(dsv)
                return dc, dg, db, _colsum(dc)

            dc, g["conv_ln_g"], g["conv_ln_b"], g["conv_dw_b"] = _rowwise(
                "conv_ln_bwd", _lnb, [(sv["c"], D, 0), (wts["conv_ln_g"], None, None), (wts["conv_ln_b"], None, None), (ds, D, 0)],
                [("row", D, F32), ("acc", (1, D)), ("acc", (1, D)), ("acc", (1, D))], T, tr)
            dv, g["conv_dw"] = _dwconv_bwd("conv_dw_bwd", dc, sv["v"], wts["conv_dw"], T)

            def _glub(za, zg, dvv):
                _, vjp = jax.vjp(_glu, za, zg)
                dza, dzg = vjp(dvv)
                dz = jnp.concatenate([dza, dzg], axis=-1)
                return dz, _colsum(dz)

            dz, g["conv_b_in"] = _rowwise("conv_glu_bwd", _glub, [(sv["z"], D, 0), (sv["z"], D, 1), (dv, D, 0)],
                                          [("row", 2 * D, BF16), ("acc", (1, 2 * D))], T, tr)
            g["conv_w_in"] = _mm_cols_grad("conv_dwin", u, dz, wts["conv_w_in"].shape, (), T)
            du = _mm_cols_t("conv_du", dz, wts["conv_w_in"], (), T)
        elif mixer == 1:
            def _psb(dyv, yc, sc):
                return dyv * sc, _colsum(dyv.astype(F32) * yc.astype(F32))

            dyg, g["pool_scale"] = _rowwise("pool_scale_bwd", _psb, [(dy, D, 0), (sv["ycat"], D, 0), (wts["pool_scale"], None, None)],
                                            [("row", D, BF16), ("acc", (1, D))], T, tr)
            dmix = _pool_mm("pool_dmix", "nt", dyg, wts["pool_w"], T)
            g["pool_w"] = _pool_mm("pool_dw", "tn", sv["mixed"], dyg, T)
            du = _pool_bwd("pool_bwd", dmix, T)
        elif mixer == 2:
            dm = _mm_rows_t("sc_dm", dy, wts["sc_w_out"], T)
            g["sc_w_out"] = _mm_rows_grad("sc_dwout", sv["m"], dy, T)
            dz, dw3 = _sc_mid_bwd("sc_mid_bwd", sv["z"], dm, wts["sc_dw"], T)
            g["sc_dw"] = dw3[0]
            g["sc_w_in"] = _mm_cols_grad("sc_dwin", u, dz, wts["sc_w_in"].shape, (), T)
            du = _mm_cols_t("sc_du", dz, wts["sc_w_in"], (), T)
        else:
            dk = D // RET_HEADS
            half = dk // 2
            dout = _mm_rows_t("ret_dout", dy, wts["ret_w_out"], T)
            g["ret_w_out"] = _mm_rows_grad("ret_dwout", sv["out"], dy, T)

            def _postb(ov, gv, dv_):
                _, vjp = jax.vjp(_gate_norm, ov, gv)
                return vjp(dv_)

            do, dzg = _rowwise("ret_post_bwd", _postb, [(sv["o"], 2 * D, 0), (sv["z"], 2 * D, 2), (dout, 2 * D, 0)],
                               [("row", 2 * D, BF16), ("row", 2 * D, BF16)], T, trw)
            dqr, dkr, dvb = _ret_bwd("ret_bwd", sv["qr"], sv["kr"], sv["vb"], do, sv["st"], tabs, T)

            def _preb(dq, dk_, dvv, dgv, cv, sn):
                return jnp.concatenate([_rot(dq, cv, sn, -1.0).astype(BF16), (_rot(dk_, cv, sn, -1.0) * (dk ** -0.5)).astype(BF16),
                                        dvv, dgv], axis=-1)

            dz = _rowwise("ret_pre_bwd", _preb, [(dqr, D, 0), (dkr, D, 0), (dvb, 2 * D, 0), (dzg, 2 * D, 0), (cos, half, 0), (sin, half, 0)],
                          [("row", 6 * D, BF16)], T, trw)[0]
            g["ret_w_in"] = _mm_cols_grad("ret_dwin", u, dz, wts["ret_w_in"].shape, (), T)
            du = _mm_cols_t("ret_du", dz, wts["ret_w_in"], (), T)
        du_next = du

    def _fb(dhv, duv, xv, gv):
        _, vjp = jax.vjp(_rms, xv, gv)
        dx, dg = vjp(duv)
        return dhv + dx, dg

    grad_x, gn[0] = _rowwise("bwd_first", _fb, [(dh, D, 0), (du_next, D, 0), (x, D, 0), (row(0), None, None)],
                             [("row", D, F32), ("acc", (1, D))], T, tr)
    g["norm_g"] = jnp.concatenate(gn, axis=0)
    return loss_row, grad_x, g


def _bound_bwd(name, dh, du, h_new, y, g_post, g_next, T, D, tr):
    def fn(dhv, duv, hn, yv, gp, gnx):
        _, vjp1 = jax.vjp(_rms, hn, gnx)
        d1, dgn = vjp1(duv.astype(F32))
        dht = dhv + d1
        _, vjp2 = jax.vjp(_rms, yv, gp)
        dy, dgp = vjp2(dht)
        return dht, dy, dgp, dgn, _colsum(dy)

    return _rowwise(name, fn, [(dh, D, 0), (du, D, 0), (h_new, D, 0), (y, D, 0), (g_post, None, None), (g_next, None, None)],
                    [("row", D, F32), ("row", D, BF16), ("acc", (1, D)), ("acc", (1, D)), ("acc", (1, D))], T, tr)


def _adamw(name, parts, w, m, v):
    P, R, C = parts.shape
    tr = _tile(R, 256)

    def body(p_ref, w_ref, m_ref, v_ref, g_ref, d_ref, m2_ref, v2_ref):
        g = p_ref[0].astype(F32)
        for i in range(1, P):
            g = g + p_ref[i].astype(F32)
        wv = w_ref[...]
        m2 = ADAM_B1 * m_ref[...] + (1.0 - ADAM_B1) * g
        v2 = ADAM_B2 * v_ref[...] + (1.0 - ADAM_B2) * (g * g)
        m_hat = m2 / (1.0 - ADAM_B1 ** ADAM_STEP)
        v_hat = v2 / (1.0 - ADAM_B2 ** ADAM_STEP)
        g_ref[...] = g
        d_ref[...] = -ADAM_LR * (m_hat / (jnp.sqrt(v_hat) + ADAM_EPS) + ADAM_WD * wv)
        m2_ref[...] = m2
        v2_ref[...] = v2

    blk = pl.BlockSpec((tr, C), lambda i: (i, 0))
    return _pcall(body, name=name, grid=(R // tr,),
                  in_specs=[pl.BlockSpec((P, tr, C), lambda i: (0, i, 0)), blk, blk, blk], out_specs=[blk] * 4,
                  out_shape=[jax.ShapeDtypeStruct((R, C), F32)] * 4, compiler_params=_params(("parallel",)))(parts, w, m, v)


def _pair_add(name, a, b):
    Q, R, C = a.shape
    tr = _tile(R, 256)
    blk = pl.BlockSpec((None, tr, C), lambda q, i: (q, i, 0))

    def body(a_ref, b_ref, o_ref):
        o_ref[...] = (a_ref[...].astype(F32) + b_ref[...].astype(F32)).astype(BF16)

    return _pcall(body, name=name, grid=(Q, R // tr), in_specs=[blk, blk], out_specs=blk,
                  out_shape=jax.ShapeDtypeStruct((Q, R, C), BF16), compiler_params=_params(("parallel", "parallel")))(a, b)


ANY = pl.BlockSpec(memory_space=pl.ANY)


def _place():
    x, y, c = lax.axis_index("x"), lax.axis_index("y"), lax.axis_index("c")
    chips = [(1 - x, y), (x, 1 - y), (1 - x, 1 - y)]
    return x, y, c, chips


def _gather_all(locs):
    n = len(locs)

    def body(*refs):
        ins, outs = refs[:n], refs[n:2 * n]
        send, recv, lsem = refs[2 * n:]
        x, y, c, chips = _place()
        me = 4 * x + 2 * y + c
        sib = (x, y, 1 - c)

        def cp(a, k, slot, to, src=None):
            return pltpu.make_async_remote_copy(src_ref=outs[a].at[slot] if src is None else src, dst_ref=outs[a].at[slot],
                                                send_sem=send.at[a, k], recv_sem=recv.at[a, k], device_id=to, device_id_type=MESH)

        mine = [pltpu.make_async_copy(ins[a], outs[a].at[me], lsem.at[a]) for a in range(n)]
        for d in mine:
            d.start()
        first = []
        for a in range(n):
            first.append(cp(a, 0, me, sib, src=ins[a]))
            for j, chip in enumerate(chips):
                first.append(cp(a, 1 + j, me, (chip[0], chip[1], c), src=ins[a]))
        for d in first:
            d.start()
        passed = []
        for j, chip in enumerate(chips):
            slot = 4 * chip[0] + 2 * chip[1] + c
            for a in range(n):
                cp(a, 1 + j, slot, sib).wait_recv()
                p = cp(a, 4 + j, slot, sib)
                p.start()
                passed.append(p)
        for a in range(n):
            cp(a, 0, 4 * x + 2 * y + (1 - c), sib).wait_recv()
        for j, chip in enumerate(chips):
            for a in range(n):
                cp(a, 4 + j, 4 * chip[0] + 2 * chip[1] + (1 - c), sib).wait_recv()
        for d in first + passed:
            d.wait_send()
        for d in mine:
            d.wait()

    return _pcall(body, name="gather_all", in_specs=[ANY] * n, out_specs=[ANY] * n,
                  out_shape=[jax.ShapeDtypeStruct((N_DEV,) + a.shape, a.dtype) for a in locs],
                  scratch_shapes=[pltpu.SemaphoreType.DMA((n, 7)), pltpu.SemaphoreType.DMA((n, 7)), pltpu.SemaphoreType.DMA((n,))],
                  compiler_params=pltpu.CompilerParams(has_side_effects=True))(*locs)


def _pair_exchange(gs):
    n = len(gs)

    def body(*refs):
        ins, kept, got = refs[:n], refs[n:2 * n], refs[2 * n:3 * n]
        send, recv, lsem = refs[3 * n:]
        x, y, c, _ = _place()
        sib = (x, y, 1 - c)
        loc = [pltpu.make_async_copy(ins[a].at[:, c], kept[a], lsem.at[a]) for a in range(n)]
        rem = [pltpu.make_async_remote_copy(src_ref=ins[a].at[:, 1 - c], dst_ref=got[a], send_sem=send.at[a], recv_sem=recv.at[a],
                                            device_id=sib, device_id_type=MESH) for a in range(n)]
        for d in loc + rem:
            d.start()
        for d in rem:
            d.wait()
        for d in loc:
            d.wait()

    half = [jax.ShapeDtypeStruct((N_CHIP,) + g.shape[2:], g.dtype) for g in gs]
    res = _pcall(body, name="pair_exchange", in_specs=[ANY] * n, out_specs=[ANY] * (2 * n), out_shape=half + half,
                 scratch_shapes=[pltpu.SemaphoreType.DMA((n,)), pltpu.SemaphoreType.DMA((n,)), pltpu.SemaphoreType.DMA((n,))],
                 compiler_params=pltpu.CompilerParams(has_side_effects=True))(*gs)
    return res[:n], res[n:]


def _chip_exchange(cs, sm_sh, sm_rep):
    n = len(cs)

    def body(*refs):
        ins, sh, rep = refs[:n], refs[n], refs[n + 1]
        outs, osh, orep = refs[n + 2:2 * n + 2], refs[2 * n + 2], refs[2 * n + 3]
        send, recv, lsem, ssend, srecv, slsem = refs[2 * n + 4:]
        x, y, c, chips = _place()
        qme = 2 * x + y
        me = 4 * x + 2 * y + c
        loc = [pltpu.make_async_copy(ins[a].at[qme], outs[a].at[qme], lsem.at[a]) for a in range(n)]
        loc.append(pltpu.make_async_copy(sh.at[me], osh.at[me], slsem.at[0]))
        loc.append(pltpu.make_async_copy(rep, orep.at[me], slsem.at[1]))
        rem, waits = [], []
        for j, chip in enumerate(chips):
            q = 2 * chip[0] + chip[1]
            to = (chip[0], chip[1], c)
            for a in range(n):
                rem.append(pltpu.make_async_remote_copy(src_ref=ins[a].at[q], dst_ref=outs[a].at[qme], send_sem=send.at[a, j],
                                                        recv_sem=recv.at[a, j], device_id=to, device_id_type=MESH))
                waits.append(pltpu.make_async_remote_copy(src_ref=ins[a].at[q], dst_ref=outs[a].at[q], send_sem=send.at[a, j],
                                                          recv_sem=recv.at[a, j], device_id=to, device_id_type=MESH))
        for r in range(1, N_DEV):
            fx, fy, fc = (r >> 2) & 1, (r >> 1) & 1, r & 1
            px = 1 - x if fx else x
            py = 1 - y if fy else y
            pc = 1 - c if fc else c
            t = 4 * px + 2 * py + pc
            to = (px, py, pc)
            rem.append(pltpu.make_async_remote_copy(src_ref=sh.at[t], dst_ref=osh.at[me], send_sem=ssend.at[0, r - 1],
                                                    recv_sem=srecv.at[0, r - 1], device_id=to, device_id_type=MESH))
            waits.append(pltpu.make_async_remote_copy(src_ref=sh.at[t], dst_ref=osh.at[t], send_sem=ssend.at[0, r - 1],
                                                      recv_sem=srecv.at[0, r - 1], device_id=to, device_id_type=MESH))
            rem.append(pltpu.make_async_remote_copy(src_ref=rep, dst_ref=orep.at[me], send_sem=ssend.at[1, r - 1],
                                                    recv_sem=srecv.at[1, r - 1], device_id=to, device_id_type=MESH))
            waits.append(pltpu.make_async_remote_copy(src_ref=rep, dst_ref=orep.at[t], send_sem=ssend.at[1, r - 1],
                                                      recv_sem=srecv.at[1, r - 1], device_id=to, device_id_type=MESH))
        for d in loc + rem:
            d.start()
        for d in waits:
            d.wait_recv()
        for d in rem:
            d.wait_send()
        for d in loc:
            d.wait()

    out_shape = [jax.ShapeDtypeStruct(a.shape, a.dtype) for a in cs]
    out_shape += [jax.ShapeDtypeStruct(sm_sh.shape, F32), jax.ShapeDtypeStruct((N_DEV,) + sm_rep.shape, F32)]
    res = _pcall(body, name="chip_exchange", in_specs=[ANY] * (n + 2), out_specs=[ANY] * (n + 2), out_shape=out_shape,
                 scratch_shapes=[pltpu.SemaphoreType.DMA((n, 3)), pltpu.SemaphoreType.DMA((n, 3)), pltpu.SemaphoreType.DMA((n,)),
                                 pltpu.SemaphoreType.DMA((2, 7)), pltpu.SemaphoreType.DMA((2, 7)), pltpu.SemaphoreType.DMA((2,))],
                 compiler_params=pltpu.CompilerParams(has_side_effects=True))(*cs, sm_sh, sm_rep)
    return res[:n], res[n], res[n + 1]


BIG = ("mlp_up", "mlp_down", "conv_w_in", "conv_w_out", "pool_w", "sc_w_in", "sc_w_out", "ret_w_in", "ret_w_out")
SMALL_SPLIT = ("norm_g", "conv_dw", "sc_dw")
SMALL_REP = ("conv_b_in", "conv_dw_b", "conv_ln_g", "conv_ln_b", "conv_b_out", "pool_scale")
WEIGHTS = ("norm_g", "mlp_up", "mlp_down", "conv_w_in", "conv_b_in", "conv_dw", "conv_dw_b", "conv_ln_g", "conv_ln_b",
           "conv_w_out", "conv_b_out", "pool_w", "pool_scale", "sc_w_in", "sc_dw", "sc_w_out", "ret_w_in", "ret_w_out")


def _pad_rows(a, rows):
    return jnp.pad(a, ((0, rows - a.shape[0]), (0, 0)))


def _split_rows(names, shards):
    out = []
    for nme in names:
        r = int(np.prod(shards[nme].shape[:-1]))
        out.append((nme, r, -(-r // 8) * 8))
    return out


def _pack_split(names, arrs):
    parts = []
    for nme, r, rp in _split_rows(names, arrs):
        parts.append(_pad_rows(arrs[nme].reshape(r, arrs[nme].shape[-1]), rp))
    return jnp.concatenate(parts, axis=0)


def _pack_rep(arrs, D):
    parts = [arrs[nme].reshape(-1, D) for nme in SMALL_REP]
    rows = sum(p.shape[0] for p in parts)
    return _pad_rows(jnp.concatenate(parts, axis=0), -(-rows // 8) * 8)


def kernel(x, positions, norm_g, mlp_up, mlp_down, conv_w_in, conv_b_in, conv_dw, conv_dw_b, conv_ln_g, conv_ln_b, conv_w_out, conv_b_out, pool_w, pool_scale, sc_w_in, sc_dw, sc_w_out, ret_w_in, ret_w_out, loss_target, m_norm_g, m_mlp_up, m_mlp_down, m_conv_w_in, m_conv_b_in, m_conv_dw, m_conv_dw_b, m_conv_ln_g, m_conv_ln_b, m_conv_w_out, m_conv_b_out, m_pool_w, m_pool_scale, m_sc_w_in, m_sc_dw, m_sc_w_out, m_ret_w_in, m_ret_w_out, v_norm_g, v_mlp_up, v_mlp_down, v_conv_w_in, v_conv_b_in, v_conv_dw, v_conv_dw_b, v_conv_ln_g, v_conv_ln_b, v_conv_w_out, v_conv_b_out, v_pool_w, v_pool_scale, v_sc_w_in, v_sc_dw, v_sc_w_out, v_ret_w_in, v_ret_w_out):
    w = dict(norm_g=norm_g, mlp_up=mlp_up, mlp_down=mlp_down, conv_w_in=conv_w_in, conv_b_in=conv_b_in, conv_dw=conv_dw,
             conv_dw_b=conv_dw_b, conv_ln_g=conv_ln_g, conv_ln_b=conv_ln_b, conv_w_out=conv_w_out, conv_b_out=conv_b_out,
             pool_w=pool_w, pool_scale=pool_scale, sc_w_in=sc_w_in, sc_dw=sc_dw, sc_w_out=sc_w_out, ret_w_in=ret_w_in, ret_w_out=ret_w_out)
    m = dict(norm_g=m_norm_g, mlp_up=m_mlp_up, mlp_down=m_mlp_down, conv_w_in=m_conv_w_in, conv_b_in=m_conv_b_in, conv_dw=m_conv_dw,
             conv_dw_b=m_conv_dw_b, conv_ln_g=m_conv_ln_g, conv_ln_b=m_conv_ln_b, conv_w_out=m_conv_w_out, conv_b_out=m_conv_b_out,
             pool_w=m_pool_w, pool_scale=m_pool_scale, sc_w_in=m_sc_w_in, sc_dw=m_sc_dw, sc_w_out=m_sc_w_out, ret_w_in=m_ret_w_in,
             ret_w_out=m_ret_w_out)
    v = dict(norm_g=v_norm_g, mlp_up=v_mlp_up, mlp_down=v_mlp_down, conv_w_in=v_conv_w_in, conv_b_in=v_conv_b_in, conv_dw=v_conv_dw,
             conv_dw_b=v_conv_dw_b, conv_ln_g=v_conv_ln_g, conv_ln_b=v_conv_ln_b, conv_w_out=v_conv_w_out, conv_b_out=v_conv_b_out,
             pool_w=v_pool_w, pool_scale=v_pool_scale, sc_w_in=v_sc_w_in, sc_dw=v_sc_dw, sc_w_out=v_sc_w_out, ret_w_in=v_ret_w_in,
             ret_w_out=v_ret_w_out)
    T, D = x.shape[1], x.shape[2]
    L = D // N_DEV
    G = len(POOL_WINDOWS)
    cg = D // G

    locs = [w[k].astype(BF16) for k in BIG] + [_pack_split(SMALL_SPLIT, w)]
    gath = _gather_all(locs)
    gw = dict(zip(BIG, gath[:len(BIG)]))
    small = gath[len(BIG)]
    wts = {"mlp_up": gw["mlp_up"], "mlp_down": gw["mlp_down"], "conv_w_in": gw["conv_w_in"], "sc_w_in": gw["sc_w_in"],
           "ret_w_in": gw["ret_w_in"]}
    for k in ("conv_w_out", "sc_w_out", "ret_w_out"):
        wts[k] = gw[k].reshape(-1, D)
    wts["pool_w"] = gw["pool_w"].transpose(1, 0, 2, 3).reshape(G, cg, cg)
    off = 0
    for nme, r, rp in _split_rows(SMALL_SPLIT, w):
        wts[nme] = small[:, off:off + r, :].transpose(1, 0, 2).reshape(r, D)
        off += rp
    for k in SMALL_REP:
        wts[k] = w[k].reshape(1, -1)

    pos = positions.reshape(T, 1).astype(F32)
    loss_row, grad_x, g = _local_step(x.reshape(T, D), pos, loss_target.reshape(T, D), wts, T, D)
    loss = lax.psum(0.5 * jnp.sum(loss_row) / D, AXES)

    big = dict(g)
    for k in ("conv_w_out", "sc_w_out", "ret_w_out"):
        big[k] = g[k].reshape((N_DEV,) + w[k].shape)
    big["pool_w"] = g["pool_w"].reshape(G, N_DEV, cg // N_DEV, cg).transpose(1, 0, 2, 3).astype(BF16)
    stacks = []
    for k in BIG:
        C = w[k].shape[-1]
        stacks.append(big[k].reshape(N_CHIP, 2, -1, C))
    kept, got = _pair_exchange(stacks)
    sums = [_pair_add("pair_add", a, b) for a, b in zip(kept, got)]
    gsm = {}
    for nme, r, rp in _split_rows(SMALL_SPLIT, w):
        gsm[nme] = jnp.pad(g[nme].reshape(r, N_DEV, L), ((0, rp - r), (0, 0), (0, 0)))
    sm_sh = jnp.concatenate([gsm[nme] for nme in SMALL_SPLIT], axis=0).transpose(1, 0, 2)
    sm_rep = _pack_rep(g, D)
    parts, r_sh, r_rep = _chip_exchange(sums, sm_sh, sm_rep)

    res = {}
    for k, p in zip(BIG, parts):
        C = w[k].shape[-1]
        out = _adamw("adamw_" + k, p, w[k].reshape(-1, C), m[k].reshape(-1, C), v[k].reshape(-1, C))
        res[k] = [o.reshape(w[k].shape) for o in out]
    out = _adamw("adamw_split", r_sh, _pack_split(SMALL_SPLIT, w), _pack_split(SMALL_SPLIT, m), _pack_split(SMALL_SPLIT, v))
    off = 0
    for nme, r, rp in _split_rows(SMALL_SPLIT, w):
        res[nme] = [o[off:off + r].reshape(w[nme].shape) for o in out]
        off += rp
    out = _adamw("adamw_rep", r_rep, _pack_rep(w, D), _pack_rep(m, D), _pack_rep(v, D))
    off = 0
    for nme in SMALL_REP:
        r = w[nme].shape[0] // D
        res[nme] = [o[off:off + r].reshape(w[nme].shape) for o in out]
        off += r
    outs = [loss, grad_x.reshape(x.shape)]
    for i in range(4):
        outs += [res[k][i] for k in WEIGHTS]
    return tuple(outs)
```

```python
import functools
import math

import numpy as np
import jax
import jax.numpy as jnp
from jax import lax
from jax.experimental import pallas as pl
from jax.experimental.pallas import tpu as pltpu

F32, BF16 = jnp.float32, jnp.bfloat16
MESH = pl.DeviceIdType.MESH
AXES = ("x", "y", "c")
N_DEV = 8
N_CHIP = 4
EPS = 1e-6
DEPTH = 4
CONV_WIDTH = 31
POOL_WINDOWS = (2, 4, 8, 16)
SHORT_CONV_WIDTH = 3
RET_HEADS = 4
RET_CHUNK = 128
ROPE_BASE = 10000.0
ADAM_LR, ADAM_B1, ADAM_B2, ADAM_EPS, ADAM_WD, ADAM_STEP = 0.001, 0.9, 0.999, 1e-08, 0.01, 10
LANES = 128
SEQ_PAD = 32
VMEM_LIMIT = 56 * 1024 * 1024

NN = (((1,), (0,)), ((), ()))
NT = (((1,), (1,)), ((), ()))
TN = (((0,), (0,)), ((), ()))


def _pcall(body, **kw):
    return pl.pallas_call(body, **kw)


def _params(sem):
    return pltpu.CompilerParams(dimension_semantics=sem, vmem_limit_bytes=VMEM_LIMIT)


def _tile(n, pref):
    t = min(n, pref)
    while n % t:
        t //= 2
    return t


def _rowwise(name, fn, ins, outs, rows, tr):
    n_in = len(ins)
    in_specs = []
    for arr, nc, cb in ins:
        if nc is None:
            in_specs.append(pl.BlockSpec(arr.shape, lambda i, nd=arr.ndim: (0,) * nd))
        else:
            in_specs.append(pl.BlockSpec((tr, nc), lambda i, cb=cb: (i, cb)))
    out_specs, out_shapes = [], []
    for o in outs:
        if o[0] == "row":
            out_shapes.append(jax.ShapeDtypeStruct((rows, o[1]), o[2]))
            out_specs.append(pl.BlockSpec((tr, o[1]), lambda i: (i, 0)))
        else:
            out_shapes.append(jax.ShapeDtypeStruct(o[1], F32))
            out_specs.append(pl.BlockSpec(o[1], lambda i: (0, 0)))

    def body(*refs):
        res = fn(*[r[...] for r in refs[:n_in]])
        if not isinstance(res, (tuple, list)):
            res = (res,)
        i = pl.program_id(0)
        for o, ref, r in zip(outs, refs[n_in:], res):
            if o[0] == "row":
                ref[...] = r.astype(o[2])
            else:
                @pl.when(i == 0)
                def _(ref=ref, r=r):
                    ref[...] = r

                @pl.when(i > 0)
                def _(ref=ref, r=r):
                    ref[...] += r

    return _pcall(body, name=name, grid=(rows // tr,), in_specs=in_specs, out_specs=out_specs,
                  out_shape=out_shapes, compiler_params=_params(("arbitrary",)))(*[a for a, _, _ in ins])


def _rms(x, g):
    return x * lax.rsqrt(jnp.mean(x * x, axis=-1, keepdims=True) + EPS) * g


def _colsum(x):
    return jnp.sum(x, axis=0, keepdims=True)


def _square(r):
    rf = r.astype(F32)
    return (rf * rf).astype(r.dtype)


def _ln_silu(c, g, b):
    mu = jnp.mean(c, axis=-1, keepdims=True)
    xc = c - mu
    y = xc * lax.rsqrt(jnp.mean(xc * xc, axis=-1, keepdims=True) + EPS) * g + b
    return y * jax.nn.sigmoid(y)


def _glu(za, zg):
    return za * jax.nn.sigmoid(zg)


def _gate_norm(o, g):
    dv = o.shape[-1] // RET_HEADS
    parts = []
    for h in range(RET_HEADS):
        oh = o[:, h * dv:(h + 1) * dv]
        parts.append(oh * lax.rsqrt(jnp.mean(oh * oh, axis=-1, keepdims=True) + EPS))
    return (g * jax.nn.sigmoid(g)) * jnp.concatenate(parts, axis=-1)


def _rot(x, cos, sin, sign):
    dk = x.shape[-1] // RET_HEADS
    half = dk // 2
    parts = []
    for h in range(RET_HEADS):
        x1 = x[:, h * dk:h * dk + half]
        x2 = x[:, h * dk + half:(h + 1) * dk]
        parts.append(x1 * cos - sign * (x2 * sin))
        parts.append(sign * (x1 * sin) + x2 * cos)
    return jnp.concatenate(parts, axis=-1)


def _mm(name, mode, grid, ins, outs, epi=None, aliases=None, pre=None):
    dn = {"nn": NN, "nt": NT, "tn": TN}[mode]
    gk = grid[2]
    n_in, n_out = len(ins), len(outs)

    def body(*refs):
        out_refs = refs[n_in:n_in + n_out]
        a = refs[0][...] if pre is None else pre(refs[0][...])
        part = lax.dot_general(a, refs[1][...], dn, preferred_element_type=F32)

        def finish(acc):
            res = epi(acc, *[r[...] for r in refs[2:n_in]]) if epi is not None else (acc,)
            for o, r in zip(out_refs, res):
                o[...] = r.astype(o.dtype)

        if gk == 1:
            finish(part)
        else:
            acc_ref = refs[-1]
            k = pl.program_id(2)

            @pl.when(k == 0)
            def _():
                acc_ref[...] = part

            @pl.when(k > 0)
            def _():
                acc_ref[...] += part

            @pl.when(k == gk - 1)
            def _():
                finish(acc_ref[...])

    scratch = []
    if gk > 1:
        blk = [d for d in outs[0][2].block_shape if d is not None]
        scratch = [pltpu.VMEM(tuple(blk), F32)]
    kw = {}
    if aliases:
        kw["input_output_aliases"] = aliases
    return _pcall(body, name=name, grid=grid, in_specs=[s for _, s in ins], out_specs=[s for _, _, s in outs],
                  out_shape=[jax.ShapeDtypeStruct(sh, dt) for sh, dt, _ in outs], scratch_shapes=scratch,
                  compiler_params=_params(("parallel", "parallel", "arbitrary")), **kw)(*[a for a, _ in ins])


def _spec(block, fn):
    return pl.BlockSpec(block, fn)


def _mm_cols(name, a, wg, lead, T, epi=None, extras=(), n_out=1, out_dtype=F32):
    K, nb = wg.shape[-2], wg.shape[-1]
    tm = _tile(T, 2048)
    wblock = (None,) * (1 + len(lead)) + (K, nb)
    ins = [(a, _spec((tm, K), lambda i, j, k: (i, 0))), (wg, _spec(wblock, lambda i, j, k: (j,) + tuple(lead) + (0, 0)))]
    for e in extras:
        if e.shape[0] == 1:
            ins.append((e, _spec((1, nb), lambda i, j, k: (0, j))))
        else:
            ins.append((e, _spec((tm, nb), lambda i, j, k: (i, j))))
    outs = [((T, N_DEV * nb), out_dtype, _spec((tm, nb), lambda i, j, k: (i, j))) for _ in range(n_out)]
    return _mm(name, "nn", (T // tm, N_DEV, 1), ins, outs, epi)


def _mm_cols_t(name, d, wg, lead, T, epi=None, extras=(), out_dtype=F32):
    K, nb = wg.shape[-2], wg.shape[-1]
    tm = _tile(T, 1024)
    wblock = (None,) * (1 + len(lead)) + (K, nb)
    ins = [(d, _spec((tm, nb), lambda i, j, k: (i, k))), (wg, _spec(wblock, lambda i, j, k: (k,) + tuple(lead) + (0, 0)))]
    for e in extras:
        ins.append((e, _spec((tm, K), lambda i, j, k: (i, 0))))
    outs = [((T, K), out_dtype, _spec((tm, K), lambda i, j, k: (i, 0)))]
    return _mm(name, "nt", (T // tm, 1, N_DEV), ins, outs, epi)[0]


def _mm_cols_grad(name, a, d, wg_shape, lead, T, into=None):
    K, nb = wg_shape[-2], wg_shape[-1]
    tk = _tile(T, 1024)
    wblock = (None,) * (1 + len(lead)) + (K, nb)
    ins = [(a, _spec((tk, K), lambda i, j, k: (k, 0))), (d, _spec((tk, nb), lambda i, j, k: (k, j)))]
    aliases = None
    if into is not None:
        ins.append((into, pl.BlockSpec(memory_space=pl.ANY)))
        aliases = {2: 0}
    outs = [(tuple(wg_shape), BF16, _spec(wblock, lambda i, j, k: (j,) + tuple(lead) + (0, 0)))]
    epi = (lambda acc, *_: (acc,)) if into is None else None
    return _mm_alias(name, "tn", (1, N_DEV, T // tk), ins, outs, aliases) if into is not None else \
        _mm(name, "tn", (1, N_DEV, T // tk), ins, outs, epi)[0]


def _mm_alias(name, mode, grid, ins, outs, aliases, pre=None):
    dn = {"nn": NN, "nt": NT, "tn": TN}[mode]
    gk = grid[2]

    def body(a_ref, b_ref, _into, o_ref, acc_ref):
        a = a_ref[...] if pre is None else pre(a_ref[...])
        part = lax.dot_general(a, b_ref[...], dn, preferred_element_type=F32)
        k = pl.program_id(2)

        @pl.when(k == 0)
        def _():
            acc_ref[...] = part

        @pl.when(k > 0)
        def _():
            acc_ref[...] += part

        @pl.when(k == gk - 1)
        def _():
            o_ref[...] = acc_ref[...].astype(o_ref.dtype)

    blk = [d for d in outs[0][2].block_shape if d is not None]
    return _pcall(body, name=name, grid=grid, in_specs=[s for _, s in ins], out_specs=[s for _, _, s in outs],
                  out_shape=[jax.ShapeDtypeStruct(sh, dt) for sh, dt, _ in outs],
                  scratch_shapes=[pltpu.VMEM(tuple(blk), F32)], input_output_aliases=aliases,
                  compiler_params=_params(("parallel", "parallel", "arbitrary")))(*[a for a, _ in ins])[0]


def _mm_rows(name, a, w2, T, epi=None, extras=(), out_dtype=F32):
    Kin, N = w2.shape
    tm, tn = _tile(T, 1024), _tile(N, 512)
    ins = [(a, _spec((tm, Kin), lambda i, j, k: (i, 0))), (w2, _spec((Kin, tn), lambda i, j, k: (0, j)))]
    for e in extras:
        ins.append((e, _spec((1, tn), lambda i, j, k: (0, j))))
    outs = [((T, N), out_dtype, _spec((tm, tn), lambda i, j, k: (i, j)))]
    return _mm(name, "nn", (T // tm, N // tn, 1), ins, outs, epi)[0]


def _mm_rows_t(name, d, w2, T, out_dtype=F32):
    Kin, N = w2.shape
    tm, tn = _tile(T, 1024), _tile(Kin, 512)
    ins = [(d, _spec((tm, N), lambda i, j, k: (i, 0))), (w2, _spec((tn, N), lambda i, j, k: (j, 0)))]
    outs = [((T, Kin), out_dtype, _spec((tm, tn), lambda i, j, k: (i, j)))]
    return _mm(name, "nt", (T // tm, Kin // tn, 1), ins, outs)[0]


def _mm_rows_grad(name, a, d, T):
    Kin, N = a.shape[1], d.shape[1]
    tk, tm, tn = _tile(T, 1024), _tile(Kin, 512), _tile(N, 1024)
    ins = [(a, _spec((tk, tm), lambda i, j, k: (k, i))), (d, _spec((tk, tn), lambda i, j, k: (k, j)))]
    outs = [((Kin, N), BF16, _spec((tm, tn), lambda i, j, k: (i, j)))]
    return _mm(name, "tn", (Kin // tm, N // tn, T // tk), ins, outs, lambda acc: (acc,))[0]


def _dwconv_fwd(name, x, w, bias, T):
    W, C = w.shape
    cb = min(C, LANES)
    tc = _tile(T, 256)

    def body(x_ref, w_ref, b_ref, o_ref, xp_ref):
        xp_ref[pl.ds(0, SEQ_PAD), :] = jnp.zeros((SEQ_PAD, cb), F32)

        def fill(c, _):
            base = pl.multiple_of(c * tc, tc)
            xp_ref[pl.ds(base + SEQ_PAD, tc), :] = x_ref[pl.ds(base, tc), :]
            return 0

        lax.fori_loop(0, T // tc, fill, 0)

        def chunk(c, _):
            base = pl.multiple_of(c * tc, tc)
            acc = jnp.zeros((tc, cb), F32) + b_ref[...]
            for k in range(W):
                acc = acc + w_ref[pl.ds(k, 1), :] * xp_ref[pl.ds(base + SEQ_PAD - (W - 1) + k, tc), :]
            o_ref[pl.ds(base, tc), :] = acc
            return 0

        lax.fori_loop(0, T // tc, chunk, 0)

    return _pcall(body, name=name, grid=(C // cb,),
                  in_specs=[pl.BlockSpec((T, cb), lambda i: (0, i)), pl.BlockSpec((W, cb), lambda i: (0, i)),
                            pl.BlockSpec((1, cb), lambda i: (0, i))],
                  out_specs=pl.BlockSpec((T, cb), lambda i: (0, i)), out_shape=jax.ShapeDtypeStruct((T, C), F32),
                  scratch_shapes=[pltpu.VMEM((T + SEQ_PAD, cb), F32)],
                  compiler_params=_params(("parallel",)))(x, w, bias)


def _dwconv_bwd(name, dc, x, w, T):
    W, C = w.shape
    cb = min(C, LANES)
    tc = _tile(T, 128)

    def body(dc_ref, x_ref, w_ref, dx_ref, dw_ref, xp_ref, dp_ref, acc_ref):
        xp_ref[pl.ds(0, SEQ_PAD), :] = jnp.zeros((SEQ_PAD, cb), F32)
        dp_ref[pl.ds(T, SEQ_PAD), :] = jnp.zeros((SEQ_PAD, cb), F32)
        acc_ref[...] = jnp.zeros((W * 8, cb), F32)

        def fill(c, _):
            base = pl.multiple_of(c * tc, tc)
            xp_ref[pl.ds(base + SEQ_PAD, tc), :] = x_ref[pl.ds(base, tc), :]
            dp_ref[pl.ds(base, tc), :] = dc_ref[pl.ds(base, tc), :]
            return 0

        lax.fori_loop(0, T // tc, fill, 0)

        def chunk(c, _):
            base = pl.multiple_of(c * tc, tc)
            d = dc_ref[pl.ds(base, tc), :]
            acc = jnp.zeros((tc, cb), F32)
            for k in range(W):
                acc = acc + w_ref[pl.ds(k, 1), :] * dp_ref[pl.ds(base + (W - 1) - k, tc), :]
                prod = d * xp_ref[pl.ds(base + SEQ_PAD - (W - 1) + k, tc), :]
                acc_ref[pl.ds(8 * k, 8), :] += jnp.sum(prod.reshape(tc // 8, 8, cb), axis=0)
            dx_ref[pl.ds(base, tc), :] = acc
            return 0

        lax.fori_loop(0, T // tc, chunk, 0)
        for k in range(W):
            dw_ref[pl.ds(k, 1), :] = jnp.sum(acc_ref[pl.ds(8 * k, 8), :], axis=0, keepdims=True)

    return _pcall(body, name=name, grid=(C // cb,),
                  in_specs=[pl.BlockSpec((T, cb), lambda i: (0, i)), pl.BlockSpec((T, cb), lambda i: (0, i)),
                            pl.BlockSpec((W, cb), lambda i: (0, i))],
                  out_specs=[pl.BlockSpec((T, cb), lambda i: (0, i)), pl.BlockSpec((W, cb), lambda i: (0, i))],
                  out_shape=[jax.ShapeDtypeStruct((T, C), F32), jax.ShapeDtypeStruct((W, C), F32)],
                  scratch_shapes=[pltpu.VMEM((T + SEQ_PAD, cb), F32), pltpu.VMEM((T + SEQ_PAD, cb), F32),
                                  pltpu.VMEM((W * 8, cb), F32)],
                  compiler_params=_params(("parallel",)))(dc, x, w)


def _pool_count(base, tc, cg, win):
    t = (lax.broadcasted_iota(jnp.int32, (tc, cg), 0) + base + 1).astype(F32)
    return jnp.minimum(t, float(win))


def _pool_fwd(name, u, T):
    D = u.shape[1]
    G = len(POOL_WINDOWS)
    cg = min(D // G, LANES)
    nb = (D // G) // cg
    tc = _tile(T, 128)

    def body(u_ref, o_ref, up_ref):
        g = pl.program_id(0)
        up_ref[pl.ds(0, SEQ_PAD), :] = jnp.zeros((SEQ_PAD, cg), F32)

        def fill(c, _):
            base = pl.multiple_of(c * tc, tc)
            up_ref[pl.ds(base + SEQ_PAD, tc), :] = u_ref[pl.ds(base, tc), :]
            return 0

        lax.fori_loop(0, T // tc, fill, 0)
        for gi, win in enumerate(POOL_WINDOWS):
            @pl.when(g == gi)
            def _(win=win):
                def chunk(c, _):
                    base = pl.multiple_of(c * tc, tc)
                    acc = up_ref[pl.ds(base + SEQ_PAD, tc), :]
                    for j in range(1, win):
                        acc = acc + up_ref[pl.ds(base + SEQ_PAD - j, tc), :]
                    mixed = acc / _pool_count(base, tc, cg, win) - u_ref[pl.ds(base, tc), :]
                    o_ref[pl.ds(base, tc), :] = mixed.astype(BF16)
                    return 0

                lax.fori_loop(0, T // tc, chunk, 0)

    return _pcall(body, name=name, grid=(G, nb), in_specs=[pl.BlockSpec((T, cg), lambda g, i: (0, g * nb + i))],
                  out_specs=pl.BlockSpec((T, cg), lambda g, i: (0, g * nb + i)), out_shape=jax.ShapeDtypeStruct((T, D), BF16),
                  scratch_shapes=[pltpu.VMEM((T + SEQ_PAD, cg), F32)], compiler_params=_params(("parallel", "parallel")))(u)


def _pool_bwd(name, dm, T):
    D = dm.shape[1]
    G = len(POOL_WINDOWS)
    cg = min(D // G, LANES)
    nb = (D // G) // cg
    tc = _tile(T, 128)

    def body(d_ref, o_ref, qp_ref):
        g = pl.program_id(0)
        qp_ref[pl.ds(T, SEQ_PAD), :] = jnp.zeros((SEQ_PAD, cg), F32)
        for gi, win in enumerate(POOL_WINDOWS):
            @pl.when(g == gi)
            def _(win=win):
                def fill(c, _):
                    base = pl.multiple_of(c * tc, tc)
                    qp_ref[pl.ds(base, tc), :] = d_ref[pl.ds(base, tc), :] / _pool_count(base, tc, cg, win)
                    return 0

                lax.fori_loop(0, T // tc, fill, 0)

                def chunk(c, _):
                    base = pl.multiple_of(c * tc, tc)
                    acc = qp_ref[pl.ds(base, tc), :]
                    for j in range(1, win):
                        acc = acc + qp_ref[pl.ds(base + j, tc), :]
                    o_ref[pl.ds(base, tc), :] = acc - d_ref[pl.ds(base, tc), :]
                    return 0

                lax.fori_loop(0, T // tc, chunk, 0)

    return _pcall(body, name=name, grid=(G, nb), in_specs=[pl.BlockSpec((T, cg), lambda g, i: (0, g * nb + i))],
                  out_specs=pl.BlockSpec((T, cg), lambda g, i: (0, g * nb + i)), out_shape=jax.ShapeDtypeStruct((T, D), F32),
                  scratch_shapes=[pltpu.VMEM((T + SEQ_PAD, cg), F32)], compiler_params=_params(("parallel", "parallel")))(dm)


def _pool_mm(name, mode, a, b, T, scale=None):
    G = len(POOL_WINDOWS)
    D = a.shape[1]
    cg = D // G
    tm = _tile(T, 1024)
    if mode == "tn":
        ins = [(a, _spec((tm, cg), lambda i, j, k: (k, j))), (b, _spec((tm, cg), lambda i, j, k: (k, j)))]
        outs = [((G, cg, cg), F32, _spec((None, cg, cg), lambda i, j, k: (j, 0, 0)))]
        return _mm(name, "tn", (1, G, T // tm), ins, outs, lambda acc: (acc,))[0]
    ins = [(a, _spec((tm, cg), lambda i, j, k: (i, j))), (b, _spec((None, cg, cg), lambda i, j, k: (j, 0, 0)))]
    if scale is not None:
        ins.append((scale, _spec((1, cg), lambda i, j, k: (0, j))))
        outs = [((T, D), F32, _spec((tm, cg), lambda i, j, k: (i, j))), ((T, D), BF16, _spec((tm, cg), lambda i, j, k: (i, j)))]
        return _mm(name, mode, (T // tm, G, 1), ins, outs, lambda acc, s: (acc * s, acc))
    outs = [((T, D), F32, _spec((tm, cg), lambda i, j, k: (i, j)))]
    return _mm(name, mode, (T // tm, G, 1), ins, outs)[0]


def _sc_mid_fwd(name, z, w, T):
    D = w.shape[1]
    W = w.shape[0]
    cb = min(D, LANES)
    nb = D // cb
    tc = _tile(T, 256)

    def body(b_ref, c_ref, v_ref, w_ref, o_ref, pp_ref):
        pp_ref[pl.ds(0, SEQ_PAD), :] = jnp.zeros((SEQ_PAD, cb), F32)

        def fill(c, _):
            base = pl.multiple_of(c * tc, tc)
            pp_ref[pl.ds(base + SEQ_PAD, tc), :] = c_ref[pl.ds(base, tc), :] * v_ref[pl.ds(base, tc), :]
            return 0

        lax.fori_loop(0, T // tc, fill, 0)

        def chunk(c, _):
            base = pl.multiple_of(c * tc, tc)
            q = jnp.zeros((tc, cb), F32)
            for k in range(W):
                q = q + w_ref[pl.ds(k, 1), :] * pp_ref[pl.ds(base + SEQ_PAD - (W - 1) + k, tc), :]
            o_ref[pl.ds(base, tc), :] = (b_ref[pl.ds(base, tc), :] * q).astype(BF16)
            return 0

        lax.fori_loop(0, T // tc, chunk, 0)

    return _pcall(body, name=name, grid=(nb,),
                  in_specs=[pl.BlockSpec((T, cb), lambda i: (0, i)), pl.BlockSpec((T, cb), lambda i: (0, nb + i)),
                            pl.BlockSpec((T, cb), lambda i: (0, 2 * nb + i)), pl.BlockSpec((W, cb), lambda i: (0, i))],
                  out_specs=pl.BlockSpec((T, cb), lambda i: (0, i)), out_shape=jax.ShapeDtypeStruct((T, D), BF16),
                  scratch_shapes=[pltpu.VMEM((T + SEQ_PAD, cb), F32)],
                  compiler_params=_params(("parallel",)))(z, z, z, w)


def _sc_mid_bwd(name, z, dm, w, T):
    D = w.shape[1]
    W = w.shape[0]
    cb = min(D, LANES)
    nb = D // cb
    tc = _tile(T, 128)

    def body(b_ref, c_ref, v_ref, d_ref, w_ref, dz_ref, dw_ref, pp_ref, dq_ref, acc_ref):
        part = pl.program_id(0)
        pp_ref[pl.ds(0, SEQ_PAD), :] = jnp.zeros((SEQ_PAD, cb), F32)
        dq_ref[pl.ds(T, SEQ_PAD), :] = jnp.zeros((SEQ_PAD, cb), F32)
        acc_ref[...] = jnp.zeros((W * 8, cb), F32)

        def fill(c, _):
            base = pl.multiple_of(c * tc, tc)
            pp_ref[pl.ds(base + SEQ_PAD, tc), :] = c_ref[pl.ds(base, tc), :] * v_ref[pl.ds(base, tc), :]
            dq_ref[pl.ds(base, tc), :] = d_ref[pl.ds(base, tc), :] * b_ref[pl.ds(base, tc), :]
            return 0

        lax.fori_loop(0, T // tc, fill, 0)

        @pl.when(part == 0)
        def _():
            def chunk(c, _):
                base = pl.multiple_of(c * tc, tc)
                dq = dq_ref[pl.ds(base, tc), :]
                q = jnp.zeros((tc, cb), F32)
                for k in range(W):
                    p = pp_ref[pl.ds(base + SEQ_PAD - (W - 1) + k, tc), :]
                    q = q + w_ref[pl.ds(k, 1), :] * p
                    acc_ref[pl.ds(8 * k, 8), :] += jnp.sum((dq * p).reshape(tc // 8, 8, cb), axis=0)
                dz_ref[pl.ds(base, tc), :] = (d_ref[pl.ds(base, tc), :] * q).astype(BF16)
                return 0

            lax.fori_loop(0, T // tc, chunk, 0)

        @pl.when(part > 0)
        def _():
            def chunk(c, _):
                base = pl.multiple_of(c * tc, tc)
                dp = jnp.zeros((tc, cb), F32)
                for k in range(W):
                    dp = dp + w_ref[pl.ds(k, 1), :] * dq_ref[pl.ds(base + (W - 1) - k, tc), :]
                other = jnp.where(part == 1, v_ref[pl.ds(base, tc), :], c_ref[pl.ds(base, tc), :])
                dz_ref[pl.ds(base, tc), :] = (dp * other).astype(BF16)
                return 0

            lax.fori_loop(0, T // tc, chunk, 0)

        for k in range(W):
            dw_ref[pl.ds(k, 1), :] = jnp.sum(acc_ref[pl.ds(8 * k, 8), :], axis=0, keepdims=True)

    dz, dw3 = _pcall(body, name=name, grid=(3, nb),
                     in_specs=[pl.BlockSpec((T, cb), lambda p, i: (0, i)), pl.BlockSpec((T, cb), lambda p, i: (0, nb + i)),
                               pl.BlockSpec((T, cb), lambda p, i: (0, 2 * nb + i)), pl.BlockSpec((T, cb), lambda p, i: (0, i)),
                               pl.BlockSpec((W, cb), lambda p, i: (0, i))],
                     out_specs=[pl.BlockSpec((T, cb), lambda p, i: (0, p * nb + i)),
                                pl.BlockSpec((None, W, cb), lambda p, i: (p, 0, i))],
                     out_shape=[jax.ShapeDtypeStruct((T, 3 * D), BF16), jax.ShapeDtypeStruct((3, W, D), F32)],
                     scratch_shapes=[pltpu.VMEM((T + SEQ_PAD, cb), F32), pltpu.VMEM((T + SEQ_PAD, cb), F32),
                                     pltpu.VMEM((W * 8, cb), F32)],
                     compiler_params=_params(("parallel", "parallel")))(z, z, z, dm, w)
    return dz, dw3


def _ret_tables(dk, dv):
    C = RET_CHUNK
    lg = np.log1p(-np.exp2(-5.0 - np.arange(RET_HEADS, dtype=np.float64)))
    idx = np.arange(C, dtype=np.float64)
    rel = idx[:, None] - idx[None, :]
    mask = np.where(rel >= 0, np.exp(lg[:, None, None] * np.maximum(rel, 0.0)), 0.0)
    qd = np.exp(lg[:, None] * (idx + 1.0))[:, :, None] * np.ones((1, 1, dk))
    kd = np.exp(lg[:, None] * (C - 1.0 - idx))[:, :, None] * np.ones((1, 1, dk))
    cd = np.exp(lg * C)[:, None, None] * np.ones((1, 1, dv))
    return tuple(jnp.asarray(a, F32) for a in (mask, qd, kd, cd))


def _ret_fwd(name, q, k, v, tabs, T):
    H, C = RET_HEADS, RET_CHUNK
    dk, dv = q.shape[1] // H, v.shape[1] // H
    nch = T // C
    mask, qd, kd, cd = tabs

    def body(q_ref, k_ref, v_ref, m_ref, qd_ref, kd_ref, cd_ref, o_ref, st_ref, s_ref):
        n = pl.program_id(1)

        @pl.when(n == 0)
        def _():
            s_ref[...] = jnp.zeros((dk, dv), F32)

        qv, kv, vv = q_ref[...], k_ref[...], v_ref[...]
        s = s_ref[...]
        sb = s.astype(BF16)
        st_ref[...] = sb
        scores = lax.dot_general(qv, kv, NT, preferred_element_type=F32) * m_ref[...]
        intra = lax.dot_general(scores.astype(BF16), vv, NN, preferred_element_type=F32)
        cross = lax.dot_general((qv.astype(F32) * qd_ref[...]).astype(BF16), sb, NN, preferred_element_type=F32)
        o_ref[...] = intra + cross
        upd = lax.dot_general((kv.astype(F32) * kd_ref[...]).astype(BF16), vv, TN, preferred_element_type=F32)
        s_ref[...] = s * cd_ref[...] + upd

    return _pcall(body, name=name, grid=(H, nch),
                  in_specs=[pl.BlockSpec((C, dk), lambda h, n: (n, h)), pl.BlockSpec((C, dk), lambda h, n: (n, h)),
                            pl.BlockSpec((C, dv), lambda h, n: (n, h)), pl.BlockSpec((None, C, C), lambda h, n: (h, 0, 0)),
                            pl.BlockSpec((None, C, dk), lambda h, n: (h, 0, 0)), pl.BlockSpec((None, C, dk), lambda h, n: (h, 0, 0)),
                            pl.BlockSpec((None, 1, dv), lambda h, n: (h, 0, 0))],
                  out_specs=[pl.BlockSpec((C, dv), lambda h, n: (n, h)),
                             pl.BlockSpec((None, None, dk, dv), lambda h, n: (h, n, 0, 0))],
                  out_shape=[jax.ShapeDtypeStruct((T, H * dv), F32), jax.ShapeDtypeStruct((H, nch, dk, dv), BF16)],
                  scratch_shapes=[pltpu.VMEM((dk, dv), F32)],
                  compiler_params=_params(("parallel", "arbitrary")))(q, k, v, mask, qd, kd, cd)


def _ret_bwd(name, q, k, v, do, st, tabs, T):
    H, C = RET_HEADS, RET_CHUNK
    dk, dv = q.shape[1] // H, v.shape[1] // H
    nch = T // C
    mask, qd, kd, cd = tabs

    def body(q_ref, k_ref, v_ref, do_ref, st_ref, m_ref, qd_ref, kd_ref, cd_ref, dq_ref, dk_ref, dv_ref, ds_ref):
        n = pl.program_id(1)

        @pl.when(n == 0)
        def _():
            ds_ref[...] = jnp.zeros((dk, dv), F32)

        qv, kv, vv, dov, sb = q_ref[...], k_ref[...], v_ref[...], do_ref[...], st_ref[...]
        m = m_ref[...]
        ds = ds_ref[...]
        dsb = ds.astype(BF16)
        a = (lax.dot_general(qv, kv, NT, preferred_element_type=F32) * m).astype(BF16)
        dp = (lax.dot_general(dov, vv, NT, preferred_element_type=F32) * m).astype(BF16)
        qa = (qv.astype(F32) * qd_ref[...]).astype(BF16)
        kb = (kv.astype(F32) * kd_ref[...]).astype(BF16)
        dq_ref[...] = (lax.dot_general(dp, kv, NN, preferred_element_type=F32)
                       + qd_ref[...] * lax.dot_general(dov, sb, NT, preferred_element_type=F32))
        dk_ref[...] = (lax.dot_general(dp, qv, TN, preferred_element_type=F32)
                       + kd_ref[...] * lax.dot_general(vv, dsb, NT, preferred_element_type=F32))
        dv_ref[...] = (lax.dot_general(a, dov, TN, preferred_element_type=F32)
                       + lax.dot_general(kb, dsb, NN, preferred_element_type=F32)).astype(BF16)
        ds_ref[...] = ds * cd_ref[...] + lax.dot_general(qa, dov, TN, preferred_element_type=F32)

    rev = lambda h, n: (nch - 1 - n, h)
    return _pcall(body, name=name, grid=(H, nch),
                  in_specs=[pl.BlockSpec((C, dk), rev), pl.BlockSpec((C, dk), rev), pl.BlockSpec((C, dv), rev),
                            pl.BlockSpec((C, dv), rev), pl.BlockSpec((None, None, dk, dv), lambda h, n: (h, nch - 1 - n, 0, 0)),
                            pl.BlockSpec((None, C, C), lambda h, n: (h, 0, 0)), pl.BlockSpec((None, C, dk), lambda h, n: (h, 0, 0)),
                            pl.BlockSpec((None, C, dk), lambda h, n: (h, 0, 0)), pl.BlockSpec((None, 1, dv), lambda h, n: (h, 0, 0))],
                  out_specs=[pl.BlockSpec((C, dk), rev), pl.BlockSpec((C, dk), rev), pl.BlockSpec((C, dv), rev)],
                  out_shape=[jax.ShapeDtypeStruct((T, H * dk), F32), jax.ShapeDtypeStruct((T, H * dk), F32),
                             jax.ShapeDtypeStruct((T, H * dv), BF16)],
                  scratch_shapes=[pltpu.VMEM((dk, dv), F32)],
                  compiler_params=_params(("parallel", "arbitrary")))(q, k, v, do, st, mask, qd, kd, cd)


def _local_step(x, pos, tgt, wts, T, D):
    ng = wts["norm_g"]
    tr = _tile(T, 256)
    trw = _tile(T, 128)
    row = lambda i: ng[i:i + 1]
    g = {}

    u = _rowwise("rms_first", lambda xv, gv: _rms(xv, gv), [(x, D, 0), (row(0), None, None)], [("row", D, BF16)], T, tr)[0]
    h = x
    saved = []
    cos = sin = None
    tabs = None
    for l in range(DEPTH):
        mixer = l % 4
        sv = {"h_in": h, "u": u}
        if mixer == 0:
            z = _mm_cols("conv_in", u, wts["conv_w_in"], (), T, lambda acc, b: (acc + b,), (wts["conv_b_in"],))[0]
            v = _rowwise("conv_glu", _glu, [(z, D, 0), (z, D, 1)], [("row", D, F32)], T, tr)[0]
            c = _dwconv_fwd("conv_dw", v, wts["conv_dw"], wts["conv_dw_b"], T)
            s = _rowwise("conv_ln", _ln_silu, [(c, D, 0), (wts["conv_ln_g"], None, None), (wts["conv_ln_b"], None, None)],
                         [("row", D, BF16)], T, tr)[0]
            y = _mm_rows("conv_out", s, wts["conv_w_out"], T, lambda acc, b: (acc + b,), (wts["conv_b_out"],))
            sv.update(z=z, v=v, c=c, s=s)
        elif mixer == 1:
            mixed = _pool_fwd("pool_fwd", u, T)
            y, ycat = _pool_mm("pool_mm", "nn", mixed, wts["pool_w"], T, wts["pool_scale"])
            sv.update(mixed=mixed, ycat=ycat)
        elif mixer == 2:
            z = _mm_cols("sc_in", u, wts["sc_w_in"], (), T)[0]
            m = _sc_mid_fwd("sc_mid", z, wts["sc_dw"], T)
            y = _mm_rows("sc_out", m, wts["sc_w_out"], T)
            sv.update(z=z, m=m)
        else:
            dk = D // RET_HEADS
            half = dk // 2
            inv_freq = jnp.asarray((ROPE_BASE ** (-np.arange(half, dtype=np.float64) / half)).astype(np.float32)).reshape(1, half)

            def _cs(p, f):
                ang = p * f
                return jnp.cos(ang), jnp.sin(ang)

            cos, sin = _rowwise("rope_tab", _cs, [(pos, 1, 0), (inv_freq, None, None)],
                                [("row", half, F32), ("row", half, F32)], T, tr)
            tabs = _ret_tables(dk, 2 * dk)
            z = _mm_cols("ret_in", u, wts["ret_w_in"], (), T)[0]

            def _pre(zq, zk, zv, cv, sn):
                return _rot(zq, cv, sn, 1.0), _rot(zk, cv, sn, 1.0) * (dk ** -0.5), zv

            qr, kr, vb = _rowwise("ret_pre", _pre, [(z, D, 0), (z, D, 1), (z, 2 * D, 1), (cos, half, 0), (sin, half, 0)],
                                  [("row", D, BF16), ("row", D, BF16), ("row", 2 * D, BF16)], T, trw)
            o, st = _ret_fwd("ret_fwd", qr, kr, vb, tabs, T)
            out = _rowwise("ret_post", _gate_norm, [(o, 2 * D, 0), (z, 2 * D, 2)], [("row", 2 * D, BF16)], T, trw)[0]
            y = _mm_rows("ret_out", out, wts["ret_w_out"], T)
            sv.update(z=z, qr=qr, kr=kr, vb=vb, o=o, st=st, out=out)
        sv["y1"] = y
        h, u = _rowwise("bound_a", lambda hv, yv, gp, gn: (lambda hn: (hn, _rms(hn, gn)))(hv + _rms(yv, gp)),
                        [(h, D, 0), (y, D, 0), (row(4 * l + 1), None, None), (row(4 * l + 2), None, None)],
                        [("row", D, F32), ("row", D, BF16)], T, tr)
        sv["h_mid"], sv["u2"] = h, u
        r = _mm_cols("mlp_up", u, wts["mlp_up"], (l,), T, lambda acc: (jnp.maximum(acc, 0.0),), (), 1, BF16)[0]
        tm = _tile(T, 1024)
        y = _mm("mlp_down", "nn", (T // tm, 1, N_DEV),
                [(r, _spec((tm, r.shape[1] // N_DEV), lambda i, j, k: (i, k))),
                 (wts["mlp_down"], _spec((None, None, r.shape[1] // N_DEV, D), lambda i, j, k, l=l: (k, l, 0, 0)))],
                [((T, D), F32, _spec((tm, D), lambda i, j, k: (i, 0)))], pre=_square)[0]
        sv.update(r=r, y2=y)
        if l + 1 < DEPTH:
            udt = F32 if (l + 1) % 4 == 1 else BF16
            h, u = _rowwise("bound_b", lambda hv, yv, gp, gn: (lambda hn: (hn, _rms(hn, gn)))(hv + _rms(yv, gp)),
                            [(h, D, 0), (y, D, 0), (row(4 * l + 3), None, None), (row(4 * l + 4), None, None)],
                            [("row", D, F32), ("row", D, udt)], T, tr)
        saved.append(sv)

    def _final(hv, yv, gp, tv):
        hn = hv + _rms(yv, gp)
        e = hn - tv
        return e * (1.0 / D), _colsum(e * e)

    dh, loss_row = _rowwise("final", _final, [(h, D, 0), (y, D, 0), (row(4 * DEPTH - 1), None, None), (tgt, D, 0)],
                            [("row", D, F32), ("acc", (1, D))], T, tr)

    gn = [None] * (4 * DEPTH)
    F = saved[0]["r"].shape[1]
    g["mlp_up"] = jnp.zeros(wts["mlp_up"].shape, BF16)
    g["mlp_down"] = jnp.zeros(wts["mlp_down"].shape, BF16)
    du_next = None
    for l in reversed(range(DEPTH)):
        sv = saved[l]
        mixer = l % 4
        if du_next is None:
            def _bb(dhv, yv, gp):
                _, vjp = jax.vjp(_rms, yv, gp)
                dy, dgp = vjp(dhv)
                return dy, dgp

            dy, gn[4 * l + 3] = _rowwise("bwd_last", _bb, [(dh, D, 0), (sv["y2"], D, 0), (row(4 * l + 3), None, None)],
                                         [("row", D, BF16), ("acc", (1, D))], T, tr)
        else:
            dh, dy, gn[4 * l + 3], gn[4 * l + 4], _ = _bound_bwd("bwd_b", dh, du_next, saved[l + 1]["h_in"], sv["y2"],
                                                              row(4 * l + 3), row(4 * l + 4), T, D, tr)
        F8 = F // N_DEV
        tm = _tile(T, 1024)
        da = _mm("mlp_dhid", "nt", (T // tm, N_DEV, 1),
                 [(dy, _spec((tm, D), lambda i, j, k: (i, 0))),
                  (wts["mlp_down"], _spec((None, None, F8, D), lambda i, j, k, l=l: (j, l, 0, 0))),
                  (sv["r"], _spec((tm, F8), lambda i, j, k: (i, j)))],
                 [((T, F), BF16, _spec((tm, F8), lambda i, j, k: (i, j)))],
                 lambda acc, rr: (acc * (2.0 * rr.astype(F32)),))[0]
        tk = _tile(T, 1024)
        g["mlp_down"] = _mm_alias("mlp_dwdown", "tn", (1, N_DEV, T // tk),
                                  [(sv["r"], _spec((tk, F8), lambda i, j, k: (k, j))), (dy, _spec((tk, D), lambda i, j, k: (k, 0))),
                                   (g["mlp_down"], pl.BlockSpec(memory_space=pl.ANY))],
                                  [(wts["mlp_down"].shape, BF16, _spec((None, None, F8, D), lambda i, j, k, l=l: (j, l, 0, 0)))], {2: 0},
                                  pre=_square)
        g["mlp_up"] = _mm_cols_grad("mlp_dwup", sv["u2"], da, wts["mlp_up"].shape, (l,), T, into=g["mlp_up"])
        du = _mm_cols_t("mlp_du", da, wts["mlp_up"], (l,), T)
        dh, dy, gn[4 * l + 1], gn[4 * l + 2], dy_sum = _bound_bwd("bwd_a", dh, du, sv["h_mid"], sv["y1"],
                                                                row(4 * l + 1), row(4 * l + 2), T, D, tr)
        u = sv["u"]
        if mixer == 0:
            g["conv_b_out"] = dy_sum
            ds = _mm_rows_t("conv_ds", dy, wts["conv_w_out"], T)
            g["conv_w_out"] = _mm_rows_grad("conv_dwout", sv["s"], dy, T)

            def _lnb(cv, gv, bv, dsv):
                _, vjp = jax.vjp(_ln_silu, cv, gv, bv)
                dc, dg, db = vjp(dsv)
                return dc, dg, db, _colsum(dc)

            dc, g["conv_ln_g"], g["conv_ln_b"], g["conv_dw_b"] = _rowwise(
                "conv_ln_bwd", _lnb, [(sv["c"], D, 0), (wts["conv_ln_g"], None, None), (wts["conv_ln_b"], None, None), (ds, D, 0)],
                [("row", D, F32), ("acc", (1, D)), ("acc", (1, D)), ("acc", (1, D))], T, tr)
            dv, g["conv_dw"] = _dwconv_bwd("conv_dw_bwd", dc, sv["v"], wts["conv_dw"], T)

            def _glub(za, zg, dvv):
                _, vjp = jax.vjp(_glu, za, zg)
                dza, dzg = vjp(dvv)
                dz = jnp.concatenate([dza, dzg], axis=-1)
                return dz, _colsum(dz)

            dz, g["conv_b_in"] = _rowwise("conv_glu_bwd", _glub, [(sv["z"], D, 0), (sv["z"], D, 1), (dv, D, 0)],
                                          [("row", 2 * D, BF16), ("acc", (1, 2 * D))], T, tr)
            g["conv_w_in"] = _mm_cols_grad("conv_dwin", u, dz, wts["conv_w_in"].shape, (), T)
            du = _mm_cols_t("conv_du", dz, wts["conv_w_in"], (), T)
        elif mixer == 1:
            def _psb(dyv, yc, sc):
                return dyv * sc, _colsum(dyv.astype(F32) * yc.astype(F32))

            dyg, g["pool_scale"] = _rowwise("pool_scale_bwd", _psb, [(dy, D, 0), (sv["ycat"], D, 0), (wts["pool_scale"], None, None)],
                                            [("row", D, BF16), ("acc", (1, D))], T, tr)
            dmix = _pool_mm("pool_dmix", "nt", dyg, wts["pool_w"], T)
            g["pool_w"] = _pool_mm("pool_dw", "tn", sv["mixed"], dyg, T)
            du = _pool_bwd("pool_bwd", dmix, T)
        elif mixer == 2:
            dm = _mm_rows_t("sc_dm", dy, wts["sc_w_out"], T)
            g["sc_w_out"] = _mm_rows_grad("sc_dwout", sv["m"], dy, T)
            dz, dw3 = _sc_mid_bwd("sc_mid_bwd", sv["z"], dm, wts["sc_dw"], T)
            g["sc_dw"] = dw3[0]
            g["sc_w_in"] = _mm_cols_grad("sc_dwin", u, dz, wts["sc_w_in"].shape, (), T)
            du = _mm_cols_t("sc_du", dz, wts["sc_w_in"], (), T)
        else:
            dk = D // RET_HEADS
            half = dk // 2
            dout = _mm_rows_t("ret_dout", dy, wts["ret_w_out"], T)
            g["ret_w_out"] = _mm_rows_grad("ret_dwout", sv["out"], dy, T)

            def _postb(ov, gv, dv_):
                _, vjp = jax.vjp(_gate_norm, ov, gv)
                return vjp(dv_)

            do, dzg = _rowwise("ret_post_bwd", _postb, [(sv["o"], 2 * D, 0), (sv["z"], 2 * D, 2), (dout, 2 * D, 0)],
                               [("row", 2 * D, BF16), ("row", 2 * D, BF16)], T, trw)
            dqr, dkr, dvb = _ret_bwd("ret_bwd", sv["qr"], sv["kr"], sv["vb"], do, sv["st"], tabs, T)

            def _preb(dq, dk_, dvv, dgv, cv, sn):
                return jnp.concatenate([_rot(dq, cv, sn, -1.0).astype(BF16), (_rot(dk_, cv, sn, -1.0) * (dk ** -0.5)).astype(BF16),
                                        dvv, dgv], axis=-1)

            dz = _rowwise("ret_pre_bwd", _preb, [(dqr, D, 0), (dkr, D, 0), (dvb, 2 * D, 0), (dzg, 2 * D, 0), (cos, half, 0), (sin, half, 0)],
                          [("row", 6 * D, BF16)], T, trw)[0]
            g["ret_w_in"] = _mm_cols_grad("ret_dwin", u, dz, wts["ret_w_in"].shape, (), T)
            du = _mm_cols_t("ret_du", dz, wts["ret_w_in"], (), T)
        du_next = du

    def _fb(dhv, duv, xv, gv):
        _, vjp = jax.vjp(_rms, xv, gv)
        dx, dg = vjp(duv)
        return dhv + dx, dg

    grad_x, gn[0] = _rowwise("bwd_first", _fb, [(dh, D, 0), (du_next, D, 0), (x, D, 0), (row(0), None, None)],
                             [("row", D, F32), ("acc", (1, D))], T, tr)
    g["norm_g"] = jnp.concatenate(gn, axis=0)
    return loss_row, grad_x, g


def _bound_bwd(name, dh, du, h_new, y, g_post, g_next, T, D, tr):
    def fn(dhv, duv, hn, yv, gp, gnx):
        _, vjp1 = jax.vjp(_rms, hn, gnx)
        d1, dgn = vjp1(duv.astype(F32))
        dht = dhv + d1
        _, vjp2 = jax.vjp(_rms, yv, gp)
        dy, dgp = vjp2(dht)
        return dht, dy, dgp, dgn, _colsum(dy)

    return _rowwise(name, fn, [(dh, D, 0), (du, D, 0), (h_new, D, 0), (y, D, 0), (g_post, None, None), (g_next, None, None)],
                    [("row", D, F32), ("row", D, BF16), ("acc", (1, D)), ("acc", (1, D)), ("acc", (1, D))], T, tr)


def _adamw(name, parts, w, m, v):
    P, R, C = parts.shape
    tr = _tile(R, 256)

    def body(p_ref, w_ref, m_ref, v_ref, g_ref, d_ref, m2_ref, v2_ref):
        g = p_ref[0].astype(F32)
        for i in range(1, P):
            g = g + p_ref[i].astype(F32)
        wv = w_ref[...]
        m2 = ADAM_B1 * m_ref[...] + (1.0 - ADAM_B1) * g
        v2 = ADAM_B2 * v_ref[...] + (1.0 - ADAM_B2) * (g * g)
        m_hat = m2 / (1.0 - ADAM_B1 ** ADAM_STEP)
        v_hat = v2 / (1.0 - ADAM_B2 ** ADAM_STEP)
        g_ref[...] = g
        d_ref[...] = -ADAM_LR * (m_hat / (jnp.sqrt(v_hat) + ADAM_EPS) + ADAM_WD * wv)
        m2_ref[...] = m2
        v2_ref[...] = v2

    blk = pl.BlockSpec((tr, C), lambda i: (i, 0))
    return _pcall(body, name=name, grid=(R // tr,),
                  in_specs=[pl.BlockSpec((P, tr, C), lambda i: (0, i, 0)), blk, blk, blk], out_specs=[blk] * 4,
                  out_shape=[jax.ShapeDtypeStruct((R, C), F32)] * 4, compiler_params=_params(("parallel",)))(parts, w, m, v)


def _pair_add(name, a, b):
    Q, R, C = a.shape
    tr = _tile(R, 1024)
    blk = pl.BlockSpec((None, tr, C), lambda q, i: (q, i, 0))

    def body(a_ref, b_ref, o_ref):
        o_ref[...] = (a_ref[...].astype(F32) + b_ref[...].astype(F32)).astype(BF16)

    return _pcall(body, name=name, grid=(Q, R // tr), in_specs=[blk, blk], out_specs=blk,
                  out_shape=jax.ShapeDtypeStruct((Q, R, C), BF16), compiler_params=_params(("parallel", "parallel")))(a, b)


ANY = pl.BlockSpec(memory_space=pl.ANY)


def _place():
    x, y, c = lax.axis_index("x"), lax.axis_index("y"), lax.axis_index("c")
    chips = [(1 - x, y), (x, 1 - y), (1 - x, 1 - y)]
    return x, y, c, chips


def _gather_all(locs):
    n = len(locs)

    def body(*refs):
        ins, outs = refs[:n], refs[n:2 * n]
        send, recv, lsem = refs[2 * n:]
        x, y, c, chips = _place()
        me = 4 * x + 2 * y + c
        sib = (x, y, 1 - c)

        def cp(a, k, slot, to, src=None):
            return pltpu.make_async_remote_copy(src_ref=outs[a].at[slot] if src is None else src, dst_ref=outs[a].at[slot],
                                                send_sem=send.at[a, k], recv_sem=recv.at[a, k], device_id=to, device_id_type=MESH)

        mine = [pltpu.make_async_copy(ins[a], outs[a].at[me], lsem.at[a]) for a in range(n)]
        for d in mine:
            d.start()
        first = []
        for a in range(n):
            first.append(cp(a, 0, me, sib, src=ins[a]))
            for j, chip in enumerate(chips):
                first.append(cp(a, 1 + j, me, (chip[0], chip[1], c), src=ins[a]))
        for d in first:
            d.start()
        passed = []
        for j, chip in enumerate(chips):
            slot = 4 * chip[0] + 2 * chip[1] + c
            for a in range(n):
                cp(a, 1 + j, slot, sib).wait_recv()
                p = cp(a, 4 + j, slot, sib)
                p.start()
                passed.append(p)
        for a in range(n):
            cp(a, 0, 4 * x + 2 * y + (1 - c), sib).wait_recv()
        for j, chip in enumerate(chips):
            for a in range(n):
                cp(a, 4 + j, 4 * chip[0] + 2 * chip[1] + (1 - c), sib).wait_recv()
        for d in first + passed:
            d.wait_send()
        for d in mine:
            d.wait()

    return _pcall(body, name="gather_all", in_specs=[ANY] * n, out_specs=[ANY] * n,
                  out_shape=[jax.ShapeDtypeStruct((N_DEV,) + a.shape, a.dtype) for a in locs],
                  scratch_shapes=[pltpu.SemaphoreType.DMA((n, 7)), pltpu.SemaphoreType.DMA((n, 7)), pltpu.SemaphoreType.DMA((n,))],
                  compiler_params=pltpu.CompilerParams(has_side_effects=True))(*locs)


def _pair_exchange(gs):
    n = len(gs)

    def body(*refs):
        ins, kept, got = refs[:n], refs[n:2 * n], refs[2 * n:3 * n]
        send, recv, lsem = refs[3 * n:]
        x, y, c, _ = _place()
        sib = (x, y, 1 - c)
        loc = [pltpu.make_async_copy(ins[a].at[q, c], kept[a].at[q], lsem.at[a, q]) for a in range(n) for q in range(N_CHIP)]
        rem = [pltpu.make_async_remote_copy(src_ref=ins[a].at[q, 1 - c], dst_ref=got[a].at[q], send_sem=send.at[a, q],
                                            recv_sem=recv.at[a, q], device_id=sib, device_id_type=MESH)
               for a in range(n) for q in range(N_CHIP)]
        for d in loc + rem:
            d.start()
        for d in rem:
            d.wait()
        for d in loc:
            d.wait()

    half = [jax.ShapeDtypeStruct((N_CHIP,) + g.shape[2:], g.dtype) for g in gs]
    res = _pcall(body, name="pair_exchange", in_specs=[ANY] * n, out_specs=[ANY] * (2 * n), out_shape=half + half,
                 scratch_shapes=[pltpu.SemaphoreType.DMA((n, N_CHIP)), pltpu.SemaphoreType.DMA((n, N_CHIP)),
                                 pltpu.SemaphoreType.DMA((n, N_CHIP))],
                 compiler_params=pltpu.CompilerParams(has_side_effects=True))(*gs)
    return res[:n], res[n:]


def _chip_exchange(cs, sm_sh, sm_rep):
    n = len(cs)

    def body(*refs):
        ins, sh, rep = refs[:n], refs[n], refs[n + 1]
        outs, osh, orep = refs[n + 2:2 * n + 2], refs[2 * n + 2], refs[2 * n + 3]
        send, recv, lsem, ssend, srecv, slsem = refs[2 * n + 4:]
        x, y, c, chips = _place()
        qme = 2 * x + y
        me = 4 * x + 2 * y + c
        loc = [pltpu.make_async_copy(ins[a].at[qme], outs[a].at[qme], lsem.at[a]) for a in range(n)]
        loc.append(pltpu.make_async_copy(sh.at[me], osh.at[me], slsem.at[0]))
        loc.append(pltpu.make_async_copy(rep, orep.at[me], slsem.at[1]))
        rem, waits = [], []
        for j, chip in enumerate(chips):
            q = 2 * chip[0] + chip[1]
            to = (chip[0], chip[1], c)
            for a in range(n):
                rem.append(pltpu.make_async_remote_copy(src_ref=ins[a].at[q], dst_ref=outs[a].at[qme], send_sem=send.at[a, j],
                                                        recv_sem=recv.at[a, j], device_id=to, device_id_type=MESH))
                waits.append(pltpu.make_async_remote_copy(src_ref=ins[a].at[q], dst_ref=outs[a].at[q], send_sem=send.at[a, j],
                                                          recv_sem=recv.at[a, j], device_id=to, device_id_type=MESH))
        for r in range(1, N_DEV):
            fx, fy, fc = (r >> 2) & 1, (r >> 1) & 1, r & 1
            px = 1 - x if fx else x
            py = 1 - y if fy else y
            pc = 1 - c if fc else c
            t = 4 * px + 2 * py + pc
            to = (px, py, pc)
            rem.append(pltpu.make_async_remote_copy(src_ref=sh.at[t], dst_ref=osh.at[me], send_sem=ssend.at[0, r - 1],
                                                    recv_sem=srecv.at[0, r - 1], device_id=to, device_id_type=MESH))
            waits.append(pltpu.make_async_remote_copy(src_ref=sh.at[t], dst_ref=osh.at[t], send_sem=ssend.at[0, r - 1],
                                                      recv_sem=srecv.at[0, r - 1], device_id=to, device_id_type=MESH))
            rem.append(pltpu.make_async_remote_copy(src_ref=rep, dst_ref=orep.at[me], send_sem=ssend.at[1, r - 1],
                                                    recv_sem=srecv.at[1, r - 1], device_id=to, device_id_type=MESH))
            waits.append(pltpu.make_async_remote_copy(src_ref=rep, dst_ref=orep.at[t], send_sem=ssend.at[1, r - 1],
                                                      recv_sem=srecv.at[1, r - 1], device_id=to, device_id_type=MESH))
        for d in loc + rem:
            d.start()
        for d in waits:
            d.wait_recv()
        for d in rem:
            d.wait_send()
        for d in loc:
            d.wait()

    out_shape = [jax.ShapeDtypeStruct(a.shape, a.dtype) for a in cs]
    out_shape += [jax.ShapeDtypeStruct(sm_sh.shape, F32), jax.ShapeDtypeStruct((N_DEV,) + sm_rep.shape, F32)]
    res = _pcall(body, name="chip_exchange", in_specs=[ANY] * (n + 2), out_specs=[ANY] * (n + 2), out_shape=out_shape,
                 scratch_shapes=[pltpu.SemaphoreType.DMA((n, 3)), pltpu.SemaphoreType.DMA((n, 3)), pltpu.SemaphoreType.DMA((n,)),
                                 pltpu.SemaphoreType.DMA((2, 7)), pltpu.SemaphoreType.DMA((2, 7)), pltpu.SemaphoreType.DMA((2,))],
                 compiler_params=pltpu.CompilerParams(has_side_effects=True))(*cs, sm_sh, sm_rep)
    return res[:n], res[n], res[n + 1]


BIG = ("mlp_up", "mlp_down", "conv_w_in", "conv_w_out", "pool_w", "sc_w_in", "sc_w_out", "ret_w_in", "ret_w_out")
SMALL_SPLIT = ("norm_g", "conv_dw", "sc_dw")
SMALL_REP = ("conv_b_in", "conv_dw_b", "conv_ln_g", "conv_ln_b", "conv_b_out", "pool_scale")
WEIGHTS = ("norm_g", "mlp_up", "mlp_down", "conv_w_in", "conv_b_in", "conv_dw", "conv_dw_b", "conv_ln_g", "conv_ln_b",
           "conv_w_out", "conv_b_out", "pool_w", "pool_scale", "sc_w_in", "sc_dw", "sc_w_out", "ret_w_in", "ret_w_out")


def _pad_rows(a, rows):
    return jnp.pad(a, ((0, rows - a.shape[0]), (0, 0)))


def _split_rows(names, shards):
    out = []
    for nme in names:
        r = int(np.prod(shards[nme].shape[:-1]))
        out.append((nme, r, -(-r // 8) * 8))
    return out


def _pack_split(names, arrs):
    parts = []
    for nme, r, rp in _split_rows(names, arrs):
        parts.append(_pad_rows(arrs[nme].reshape(r, arrs[nme].shape[-1]), rp))
    return jnp.concatenate(parts, axis=0)


def _pack_rep(arrs, D):
    parts = [arrs[nme].reshape(-1, D) for nme in SMALL_REP]
    rows = sum(p.shape[0] for p in parts)
    return _pad_rows(jnp.concatenate(parts, axis=0), -(-rows // 8) * 8)


def kernel(x, positions, norm_g, mlp_up, mlp_down, conv_w_in, conv_b_in, conv_dw, conv_dw_b, conv_ln_g, conv_ln_b, conv_w_out, conv_b_out, pool_w, pool_scale, sc_w_in, sc_dw, sc_w_out, ret_w_in, ret_w_out, loss_target, m_norm_g, m_mlp_up, m_mlp_down, m_conv_w_in, m_conv_b_in, m_conv_dw, m_conv_dw_b, m_conv_ln_g, m_conv_ln_b, m_conv_w_out, m_conv_b_out, m_pool_w, m_pool_scale, m_sc_w_in, m_sc_dw, m_sc_w_out, m_ret_w_in, m_ret_w_out, v_norm_g, v_mlp_up, v_mlp_down, v_conv_w_in, v_conv_b_in, v_conv_dw, v_conv_dw_b, v_conv_ln_g, v_conv_ln_b, v_conv_w_out, v_conv_b_out, v_pool_w, v_pool_scale, v_sc_w_in, v_sc_dw, v_sc_w_out, v_ret_w_in, v_ret_w_out):
    w = dict(norm_g=norm_g, mlp_up=mlp_up, mlp_down=mlp_down, conv_w_in=conv_w_in, conv_b_in=conv_b_in, conv_dw=conv_dw,
             conv_dw_b=conv_dw_b, conv_ln_g=conv_ln_g, conv_ln_b=conv_ln_b, conv_w_out=conv_w_out, conv_b_out=conv_b_out,
             pool_w=pool_w, pool_scale=pool_scale, sc_w_in=sc_w_in, sc_dw=sc_dw, sc_w_out=sc_w_out, ret_w_in=ret_w_in, ret_w_out=ret_w_out)
    m = dict(norm_g=m_norm_g, mlp_up=m_mlp_up, mlp_down=m_mlp_down, conv_w_in=m_conv_w_in, conv_b_in=m_conv_b_in, conv_dw=m_conv_dw,
             conv_dw_b=m_conv_dw_b, conv_ln_g=m_conv_ln_g, conv_ln_b=m_conv_ln_b, conv_w_out=m_conv_w_out, conv_b_out=m_conv_b_out,
             pool_w=m_pool_w, pool_scale=m_pool_scale, sc_w_in=m_sc_w_in, sc_dw=m_sc_dw, sc_w_out=m_sc_w_out, ret_w_in=m_ret_w_in,
             ret_w_out=m_ret_w_out)
    v = dict(norm_g=v_norm_g, mlp_up=v_mlp_up, mlp_down=v_mlp_down, conv_w_in=v_conv_w_in, conv_b_in=v_conv_b_in, conv_dw=v_conv_dw,
             conv_dw_b=v_conv_dw_b, conv_ln_g=v_conv_ln_g, conv_ln_b=v_conv_ln_b, conv_w_out=v_conv_w_out, conv_b_out=v_conv_b_out,
             pool_w=v_pool_w, pool_scale=v_pool_scale, sc_w_in=v_sc_w_in, sc_dw=v_sc_dw, sc_w_out=v_sc_w_out, ret_w_in=v_ret_w_in,
             ret_w_out=v_ret_w_out)
    T, D = x.shape[1], x.shape[2]
    L = D // N_DEV
    G = len(POOL_WINDOWS)
    cg = D // G

    locs = [w[k].astype(BF16) for k in BIG] + [_pack_split(SMALL_SPLIT, w)]
    gath = _gather_all(locs)
    gw = dict(zip(BIG, gath[:len(BIG)]))
    small = gath[len(BIG)]
    wts = {"mlp_up": gw["mlp_up"], "mlp_down": gw["mlp_down"], "conv_w_in": gw["conv_w_in"], "sc_w_in": gw["sc_w_in"],
           "ret_w_in": gw["ret_w_in"]}
    for k in ("conv_w_out", "sc_w_out", "ret_w_out"):
        wts[k] = gw[k].reshape(-1, D)
    wts["pool_w"] = gw["pool_w"].transpose(1, 0, 2, 3).reshape(G, cg, cg)
    off = 0
    for nme, r, rp in _split_rows(SMALL_SPLIT, w):
        wts[nme] = small[:, off:off + r, :].transpose(1, 0, 2).reshape(r, D)
        off += rp
    for k in SMALL_REP:
        wts[k] = w[k].reshape(1, -1)

    pos = positions.reshape(T, 1).astype(F32)
    loss_row, grad_x, g = _local_step(x.reshape(T, D), pos, loss_target.reshape(T, D), wts, T, D)
    loss = lax.psum(0.5 * jnp.sum(loss_row) / D, AXES)

    big = dict(g)
    for k in ("conv_w_out", "sc_w_out", "ret_w_out"):
        big[k] = g[k].reshape((N_DEV,) + w[k].shape)
    big["pool_w"] = g["pool_w"].reshape(G, N_DEV, cg // N_DEV, cg).transpose(1, 0, 2, 3).astype(BF16)
    stacks = []
    for k in BIG:
        C = w[k].shape[-1]
        stacks.append(big[k].reshape(N_CHIP, 2, -1, C))
    kept, got = _pair_exchange(stacks)
    sums = [_pair_add("pair_add", a, b) for a, b in zip(kept, got)]
    gsm = {}
    for nme, r, rp in _split_rows(SMALL_SPLIT, w):
        gsm[nme] = jnp.pad(g[nme].reshape(r, N_DEV, L), ((0, rp - r), (0, 0), (0, 0)))
    sm_sh = jnp.concatenate([gsm[nme] for nme in SMALL_SPLIT], axis=0).transpose(1, 0, 2)
    sm_rep = _pack_rep(g, D)
    parts, r_sh, r_rep = _chip_exchange(sums, sm_sh, sm_rep)

    res = {}
    for k, p in zip(BIG, parts):
        C = w[k].shape[-1]
        out = _adamw("adamw_" + k, p, w[k].reshape(-1, C), m[k].reshape(-1, C), v[k].reshape(-1, C))
        res[k] = [o.reshape(w[k].shape) for o in out]
    out = _adamw("adamw_split", r_sh, _pack_split(SMALL_SPLIT, w), _pack_split(SMALL_SPLIT, m), _pack_split(SMALL_SPLIT, v))
    off = 0
    for nme, r, rp in _split_rows(SMALL_SPLIT, w):
        res[nme] = [o[off:off + r].reshape(w[nme].shape) for o in out]
        off += rp
    out = _adamw("adamw_rep", r_rep, _pack_rep(w, D), _pack_rep(m, D), _pack_rep(v, D))
    off = 0
    for nme in SMALL_REP:
        r = w[nme].shape[0] // D
        res[nme] = [o[off:off + r].reshape(w[nme].shape) for o in out]
        off += r
    outs = [loss, grad_x.reshape(x.shape)]
    for i in range(4):
        outs += [res[k][i] for k in WEIGHTS]
    return tuple(outs)
```

```python
import functools
import math

import numpy as np
import jax
import jax.numpy as jnp
from jax import lax
from jax.experimental import pallas as pl
from jax.experimental.pallas import tpu as pltpu

F32, BF16 = jnp.float32, jnp.bfloat16
MESH = pl.DeviceIdType.MESH
AXES = ("x", "y", "c")
N_DEV = 8
N_CHIP = 4
EPS = 1e-6
DEPTH = 4
CONV_WIDTH = 31
POOL_WINDOWS = (2, 4, 8, 16)
SHORT_CONV_WIDTH = 3
RET_HEADS = 4
RET_CHUNK = 128
ROPE_BASE = 10000.0
ADAM_LR, ADAM_B1, ADAM_B2, ADAM_EPS, ADAM_WD, ADAM_STEP = 0.001, 0.9, 0.999, 1e-08, 0.01, 10
LANES = 128
SEQ_PAD = 32
VMEM_LIMIT = 56 * 1024 * 1024

NN = (((1,), (0,)), ((), ()))
NT = (((1,), (1,)), ((), ()))
TN = (((0,), (0,)), ((), ()))


def _pcall(body, **kw):
    return pl.pallas_call(body, **kw)


def _params(sem):
    return pltpu.CompilerParams(dimension_semantics=sem, vmem_limit_bytes=VMEM_LIMIT)


def _tile(n, pref):
    t = min(n, pref)
    while n % t:
        t //= 2
    return t


def _rowwise(name, fn, ins, outs, rows, tr):
    n_in = len(ins)
    in_specs = []
    for arr, nc, cb in ins:
        if nc is None:
            in_specs.append(pl.BlockSpec(arr.shape, lambda i, nd=arr.ndim: (0,) * nd))
        else:
            in_specs.append(pl.BlockSpec((tr, nc), lambda i, cb=cb: (i, cb)))
    out_specs, out_shapes = [], []
    for o in outs:
        if o[0] == "row":
            out_shapes.append(jax.ShapeDtypeStruct((rows, o[1]), o[2]))
            out_specs.append(pl.BlockSpec((tr, o[1]), lambda i: (i, 0)))
        else:
            out_shapes.append(jax.ShapeDtypeStruct(o[1], F32))
            out_specs.append(pl.BlockSpec(o[1], lambda i: (0, 0)))

    def body(*refs):
        res = fn(*[r[...] for r in refs[:n_in]])
        if not isinstance(res, (tuple, list)):
            res = (res,)
        i = pl.program_id(0)
        for o, ref, r in zip(outs, refs[n_in:], res):
            if o[0] == "row":
                ref[...] = r.astype(o[2])
            else:
                @pl.when(i == 0)
                def _(ref=ref, r=r):
                    ref[...] = r

                @pl.when(i > 0)
                def _(ref=ref, r=r):
                    ref[...] += r

    return _pcall(body, name=name, grid=(rows // tr,), in_specs=in_specs, out_specs=out_specs,
                  out_shape=out_shapes, compiler_params=_params(("arbitrary",)))(*[a for a, _, _ in ins])


def _rms(x, g):
    return x * lax.rsqrt(jnp.mean(x * x, axis=-1, keepdims=True) + EPS) * g


def _colsum(x):
    return jnp.sum(x, axis=0, keepdims=True)


def _square(r):
    rf = r.astype(F32)
    return (rf * rf).astype(r.dtype)


def _ln_silu(c, g, b):
    mu = jnp.mean(c, axis=-1, keepdims=True)
    xc = c - mu
    y = xc * lax.rsqrt(jnp.mean(xc * xc, axis=-1, keepdims=True) + EPS) * g + b
    return y * jax.nn.sigmoid(y)


def _glu(za, zg):
    return za * jax.nn.sigmoid(zg)


def _gate_norm(o, g):
    dv = o.shape[-1] // RET_HEADS
    parts = []
    for h in range(RET_HEADS):
        oh = o[:, h * dv:(h + 1) * dv]
        parts.append(oh * lax.rsqrt(jnp.mean(oh * oh, axis=-1, keepdims=True) + EPS))
    return (g * jax.nn.sigmoid(g)) * jnp.concatenate(parts, axis=-1)


def _rot(x, cos, sin, sign):
    dk = x.shape[-1] // RET_HEADS
    half = dk // 2
    parts = []
    for h in range(RET_HEADS):
        x1 = x[:, h * dk:h * dk + half]
        x2 = x[:, h * dk + half:(h + 1) * dk]
        parts.append(x1 * cos - sign * (x2 * sin))
        parts.append(sign * (x1 * sin) + x2 * cos)
    return jnp.concatenate(parts, axis=-1)


def _mm(name, mode, grid, ins, outs, epi=None, aliases=None, pre=None):
    dn = {"nn": NN, "nt": NT, "tn": TN}[mode]
    gk = grid[2]
    n_in, n_out = len(ins), len(outs)

    def body(*refs):
        out_refs = refs[n_in:n_in + n_out]
        a = refs[0][...] if pre is None else pre(refs[0][...])
        part = lax.dot_general(a, refs[1][...], dn, preferred_element_type=F32)

        def finish(acc):
            res = epi(acc, *[r[...] for r in refs[2:n_in]]) if epi is not None else (acc,)
            for o, r in zip(out_refs, res):
                o[...] = r.astype(o.dtype)

        if gk == 1:
            finish(part)
        else:
            acc_ref = refs[-1]
            k = pl.program_id(2)

            @pl.when(k == 0)
            def _():
                acc_ref[...] = part

            @pl.when(k > 0)
            def _():
                acc_ref[...] += part

            @pl.when(k == gk - 1)
            def _():
                finish(acc_ref[...])

    scratch = []
    if gk > 1:
        blk = [d for d in outs[0][2].block_shape if d is not None]
        scratch = [pltpu.VMEM(tuple(blk), F32)]
    kw = {}
    if aliases:
        kw["input_output_aliases"] = aliases
    return _pcall(body, name=name, grid=grid, in_specs=[s for _, s in ins], out_specs=[s for _, _, s in outs],
                  out_shape=[jax.ShapeDtypeStruct(sh, dt) for sh, dt, _ in outs], scratch_shapes=scratch,
                  compiler_params=_params(("parallel", "parallel", "arbitrary")), **kw)(*[a for a, _ in ins])


def _spec(block, fn):
    return pl.BlockSpec(block, fn)


def _mm_cols(name, a, wg, lead, T, epi=None, extras=(), n_out=1, out_dtype=F32):
    K, nb = wg.shape[-2], wg.shape[-1]
    tm = _tile(T, 2048)
    wblock = (None,) * (1 + len(lead)) + (K, nb)
    ins = [(a, _spec((tm, K), lambda i, j, k: (i, 0))), (wg, _spec(wblock, lambda i, j, k: (j,) + tuple(lead) + (0, 0)))]
    for e in extras:
        if e.shape[0] == 1:
            ins.append((e, _spec((1, nb), lambda i, j, k: (0, j))))
        else:
            ins.append((e, _spec((tm, nb), lambda i, j, k: (i, j))))
    outs = [((T, N_DEV * nb), out_dtype, _spec((tm, nb), lambda i, j, k: (i, j))) for _ in range(n_out)]
    return _mm(name, "nn", (T // tm, N_DEV, 1), ins, outs, epi)


def _mm_cols_t(name, d, wg, lead, T, epi=None, extras=(), out_dtype=F32):
    K, nb = wg.shape[-2], wg.shape[-1]
    tm = _tile(T, 2048)
    wblock = (None,) * (1 + len(lead)) + (K, nb)
    ins = [(d, _spec((tm, nb), lambda i, j, k: (i, k))), (wg, _spec(wblock, lambda i, j, k: (k,) + tuple(lead) + (0, 0)))]
    for e in extras:
        ins.append((e, _spec((tm, K), lambda i, j, k: (i, 0))))
    outs = [((T, K), out_dtype, _spec((tm, K), lambda i, j, k: (i, 0)))]
    return _mm(name, "nt", (T // tm, 1, N_DEV), ins, outs, epi)[0]


def _mm_cols_grad(name, a, d, wg_shape, lead, T, into=None):
    K, nb = wg_shape[-2], wg_shape[-1]
    tk = _tile(T, 2048)
    wblock = (None,) * (1 + len(lead)) + (K, nb)
    ins = [(a, _spec((tk, K), lambda i, j, k: (k, 0))), (d, _spec((tk, nb), lambda i, j, k: (k, j)))]
    aliases = None
    if into is not None:
        ins.append((into, pl.BlockSpec(memory_space=pl.ANY)))
        aliases = {2: 0}
    outs = [(tuple(wg_shape), BF16, _spec(wblock, lambda i, j, k: (j,) + tuple(lead) + (0, 0)))]
    epi = (lambda acc, *_: (acc,)) if into is None else None
    return _mm_alias(name, "tn", (1, N_DEV, T // tk), ins, outs, aliases) if into is not None else \
        _mm(name, "tn", (1, N_DEV, T // tk), ins, outs, epi)[0]


def _mm_alias(name, mode, grid, ins, outs, aliases, pre=None):
    dn = {"nn": NN, "nt": NT, "tn": TN}[mode]
    gk = grid[2]

    def body(a_ref, b_ref, _into, o_ref, acc_ref):
        a = a_ref[...] if pre is None else pre(a_ref[...])
        part = lax.dot_general(a, b_ref[...], dn, preferred_element_type=F32)
        k = pl.program_id(2)

        @pl.when(k == 0)
        def _():
            acc_ref[...] = part

        @pl.when(k > 0)
        def _():
            acc_ref[...] += part

        @pl.when(k == gk - 1)
        def _():
            o_ref[...] = acc_ref[...].astype(o_ref.dtype)

    blk = [d for d in outs[0][2].block_shape if d is not None]
    return _pcall(body, name=name, grid=grid, in_specs=[s for _, s in ins], out_specs=[s for _, _, s in outs],
                  out_shape=[jax.ShapeDtypeStruct(sh, dt) for sh, dt, _ in outs],
                  scratch_shapes=[pltpu.VMEM(tuple(blk), F32)], input_output_aliases=aliases,
                  compiler_params=_params(("parallel", "parallel", "arbitrary")))(*[a for a, _ in ins])[0]


def _mm_rows(name, a, w2, T, epi=None, extras=(), out_dtype=F32):
    Kin, N = w2.shape
    tm, tn = _tile(T, 2048), _tile(N, 512)
    ins = [(a, _spec((tm, Kin), lambda i, j, k: (i, 0))), (w2, _spec((Kin, tn), lambda i, j, k: (0, j)))]
    for e in extras:
        ins.append((e, _spec((1, tn), lambda i, j, k: (0, j))))
    outs = [((T, N), out_dtype, _spec((tm, tn), lambda i, j, k: (i, j)))]
    return _mm(name, "nn", (T // tm, N // tn, 1), ins, outs, epi)[0]


def _mm_rows_t(name, d, w2, T, out_dtype=F32):
    Kin, N = w2.shape
    tm, tn = _tile(T, 2048), _tile(Kin, 512)
    ins = [(d, _spec((tm, N), lambda i, j, k: (i, 0))), (w2, _spec((tn, N), lambda i, j, k: (j, 0)))]
    outs = [((T, Kin), out_dtype, _spec((tm, tn), lambda i, j, k: (i, j)))]
    return _mm(name, "nt", (T // tm, Kin // tn, 1), ins, outs)[0]


def _mm_rows_grad(name, a, d, T):
    Kin, N = a.shape[1], d.shape[1]
    tk, tm, tn = _tile(T, 2048), _tile(Kin, 512), _tile(N, 1024)
    ins = [(a, _spec((tk, tm), lambda i, j, k: (k, i))), (d, _spec((tk, tn), lambda i, j, k: (k, j)))]
    outs = [((Kin, N), BF16, _spec((tm, tn), lambda i, j, k: (i, j)))]
    return _mm(name, "tn", (Kin // tm, N // tn, T // tk), ins, outs, lambda acc: (acc,))[0]


def _dwconv_fwd(name, x, w, bias, T):
    W, C = w.shape
    cb = min(C, LANES)
    tc = _tile(T, 256)

    def body(x_ref, w_ref, b_ref, o_ref, xp_ref):
        xp_ref[pl.ds(0, SEQ_PAD), :] = jnp.zeros((SEQ_PAD, cb), F32)

        def fill(c, _):
            base = pl.multiple_of(c * tc, tc)
            xp_ref[pl.ds(base + SEQ_PAD, tc), :] = x_ref[pl.ds(base, tc), :]
            return 0

        lax.fori_loop(0, T // tc, fill, 0)

        def chunk(c, _):
            base = pl.multiple_of(c * tc, tc)
            acc = jnp.zeros((tc, cb), F32) + b_ref[...]
            for k in range(W):
                acc = acc + w_ref[pl.ds(k, 1), :] * xp_ref[pl.ds(base + SEQ_PAD - (W - 1) + k, tc), :]
            o_ref[pl.ds(base, tc), :] = acc
            return 0

        lax.fori_loop(0, T // tc, chunk, 0)

    return _pcall(body, name=name, grid=(C // cb,),
                  in_specs=[pl.BlockSpec((T, cb), lambda i: (0, i)), pl.BlockSpec((W, cb), lambda i: (0, i)),
                            pl.BlockSpec((1, cb), lambda i: (0, i))],
                  out_specs=pl.BlockSpec((T, cb), lambda i: (0, i)), out_shape=jax.ShapeDtypeStruct((T, C), F32),
                  scratch_shapes=[pltpu.VMEM((T + SEQ_PAD, cb), F32)],
                  compiler_params=_params(("parallel",)))(x, w, bias)


def _dwconv_bwd(name, dc, x, w, T):
    W, C = w.shape
    cb = min(C, LANES)
    tc = _tile(T, 128)

    def body(dc_ref, x_ref, w_ref, dx_ref, dw_ref, xp_ref, dp_ref, acc_ref):
        xp_ref[pl.ds(0, SEQ_PAD), :] = jnp.zeros((SEQ_PAD, cb), F32)
        dp_ref[pl.ds(T, SEQ_PAD), :] = jnp.zeros((SEQ_PAD, cb), F32)
        acc_ref[...] = jnp.zeros((W * 8, cb), F32)

        def fill(c, _):
            base = pl.multiple_of(c * tc, tc)
            xp_ref[pl.ds(base + SEQ_PAD, tc), :] = x_ref[pl.ds(base, tc), :]
            dp_ref[pl.ds(base, tc), :] = dc_ref[pl.ds(base, tc), :]
            return 0

        lax.fori_loop(0, T // tc, fill, 0)

        def chunk(c, _):
            base = pl.multiple_of(c * tc, tc)
            d = dc_ref[pl.ds(base, tc), :]
            acc = jnp.zeros((tc, cb), F32)
            for k in range(W):
                acc = acc + w_ref[pl.ds(k, 1), :] * dp_ref[pl.ds(base + (W - 1) - k, tc), :]
                prod = d * xp_ref[pl.ds(base + SEQ_PAD - (W - 1) + k, tc), :]
                acc_ref[pl.ds(8 * k, 8), :] += jnp.sum(prod.reshape(tc // 8, 8, cb), axis=0)
            dx_ref[pl.ds(base, tc), :] = acc
            return 0

        lax.fori_loop(0, T // tc, chunk, 0)
        for k in range(W):
            dw_ref[pl.ds(k, 1), :] = jnp.sum(acc_ref[pl.ds(8 * k, 8), :], axis=0, keepdims=True)

    return _pcall(body, name=name, grid=(C // cb,),
                  in_specs=[pl.BlockSpec((T, cb), lambda i: (0, i)), pl.BlockSpec((T, cb), lambda i: (0, i)),
                            pl.BlockSpec((W, cb), lambda i: (0, i))],
                  out_specs=[pl.BlockSpec((T, cb), lambda i: (0, i)), pl.BlockSpec((W, cb), lambda i: (0, i))],
                  out_shape=[jax.ShapeDtypeStruct((T, C), F32), jax.ShapeDtypeStruct((W, C), F32)],
                  scratch_shapes=[pltpu.VMEM((T + SEQ_PAD, cb), F32), pltpu.VMEM((T + SEQ_PAD, cb), F32),
                                  pltpu.VMEM((W * 8, cb), F32)],
                  compiler_params=_params(("parallel",)))(dc, x, w)


def _pool_count(base, tc, cg, win):
    t = (lax.broadcasted_iota(jnp.int32, (tc, cg), 0) + base + 1).astype(F32)
    return jnp.minimum(t, float(win))


def _pool_fwd(name, u, T):
    D = u.shape[1]
    G = len(POOL_WINDOWS)
    cg = min(D // G, LANES)
    nb = (D // G) // cg
    tc = _tile(T, 128)

    def body(u_ref, o_ref, up_ref):
        g = pl.program_id(0)
        up_ref[pl.ds(0, SEQ_PAD), :] = jnp.zeros((SEQ_PAD, cg), F32)

        def fill(c, _):
            base = pl.multiple_of(c * tc, tc)
            up_ref[pl.ds(base + SEQ_PAD, tc), :] = u_ref[pl.ds(base, tc), :]
            return 0

        lax.fori_loop(0, T // tc, fill, 0)
        for gi, win in enumerate(POOL_WINDOWS):
            @pl.when(g == gi)
            def _(win=win):
                def chunk(c, _):
                    base = pl.multiple_of(c * tc, tc)
                    acc = up_ref[pl.ds(base + SEQ_PAD, tc), :]
                    for j in range(1, win):
                        acc = acc + up_ref[pl.ds(base + SEQ_PAD - j, tc), :]
                    mixed = acc / _pool_count(base, tc, cg, win) - u_ref[pl.ds(base, tc), :]
                    o_ref[pl.ds(base, tc), :] = mixed.astype(BF16)
                    return 0

                lax.fori_loop(0, T // tc, chunk, 0)

    return _pcall(body, name=name, grid=(G, nb), in_specs=[pl.BlockSpec((T, cg), lambda g, i: (0, g * nb + i))],
                  out_specs=pl.BlockSpec((T, cg), lambda g, i: (0, g * nb + i)), out_shape=jax.ShapeDtypeStruct((T, D), BF16),
                  scratch_shapes=[pltpu.VMEM((T + SEQ_PAD, cg), F32)], compiler_params=_params(("parallel", "parallel")))(u)


def _pool_bwd(name, dm, T):
    D = dm.shape[1]
    G = len(POOL_WINDOWS)
    cg = min(D // G, LANES)
    nb = (D // G) // cg
    tc = _tile(T, 128)

    def body(d_ref, o_ref, qp_ref):
        g = pl.program_id(0)
        qp_ref[pl.ds(T, SEQ_PAD), :] = jnp.zeros((SEQ_PAD, cg), F32)
        for gi, win in enumerate(POOL_WINDOWS):
            @pl.when(g == gi)
            def _(win=win):
                def fill(c, _):
                    base = pl.multiple_of(c * tc, tc)
                    qp_ref[pl.ds(base, tc), :] = d_ref[pl.ds(base, tc), :] / _pool_count(base, tc, cg, win)
                    return 0

                lax.fori_loop(0, T // tc, fill, 0)

                def chunk(c, _):
                    base = pl.multiple_of(c * tc, tc)
                    acc = qp_ref[pl.ds(base, tc), :]
                    for j in range(1, win):
                        acc = acc + qp_ref[pl.ds(base + j, tc), :]
                    o_ref[pl.ds(base, tc), :] = acc - d_ref[pl.ds(base, tc), :]
                    return 0

                lax.fori_loop(0, T // tc, chunk, 0)

    return _pcall(body, name=name, grid=(G, nb), in_specs=[pl.BlockSpec((T, cg), lambda g, i: (0, g * nb + i))],
                  out_specs=pl.BlockSpec((T, cg), lambda g, i: (0, g * nb + i)), out_shape=jax.ShapeDtypeStruct((T, D), F32),
                  scratch_shapes=[pltpu.VMEM((T + SEQ_PAD, cg), F32)], compiler_params=_params(("parallel", "parallel")))(dm)


def _pool_mm(name, mode, a, b, T, scale=None):
    G = len(POOL_WINDOWS)
    D = a.shape[1]
    cg = D // G
    tm = _tile(T, 2048)
    if mode == "tn":
        ins = [(a, _spec((tm, cg), lambda i, j, k: (k, j))), (b, _spec((tm, cg), lambda i, j, k: (k, j)))]
        outs = [((G, cg, cg), F32, _spec((None, cg, cg), lambda i, j, k: (j, 0, 0)))]
        return _mm(name, "tn", (1, G, T // tm), ins, outs, lambda acc: (acc,))[0]
    ins = [(a, _spec((tm, cg), lambda i, j, k: (i, j))), (b, _spec((None, cg, cg), lambda i, j, k: (j, 0, 0)))]
    if scale is not None:
        ins.append((scale, _spec((1, cg), lambda i, j, k: (0, j))))
        outs = [((T, D), F32, _spec((tm, cg), lambda i, j, k: (i, j))), ((T, D), BF16, _spec((tm, cg), lambda i, j, k: (i, j)))]
        return _mm(name, mode, (T // tm, G, 1), ins, outs, lambda acc, s: (acc * s, acc))
    outs = [((T, D), F32, _spec((tm, cg), lambda i, j, k: (i, j)))]
    return _mm(name, mode, (T // tm, G, 1), ins, outs)[0]


def _sc_mid_fwd(name, z, w, T):
    D = w.shape[1]
    W = w.shape[0]
    cb = min(D, LANES)
    nb = D // cb
    tc = _tile(T, 256)

    def body(b_ref, c_ref, v_ref, w_ref, o_ref, pp_ref):
        pp_ref[pl.ds(0, SEQ_PAD), :] = jnp.zeros((SEQ_PAD, cb), F32)

        def fill(c, _):
            base = pl.multiple_of(c * tc, tc)
            pp_ref[pl.ds(base + SEQ_PAD, tc), :] = c_ref[pl.ds(base, tc), :] * v_ref[pl.ds(base, tc), :]
            return 0

        lax.fori_loop(0, T // tc, fill, 0)

        def chunk(c, _):
            base = pl.multiple_of(c * tc, tc)
            q = jnp.zeros((tc, cb), F32)
            for k in range(W):
                q = q + w_ref[pl.ds(k, 1), :] * pp_ref[pl.ds(base + SEQ_PAD - (W - 1) + k, tc), :]
            o_ref[pl.ds(base, tc), :] = (b_ref[pl.ds(base, tc), :] * q).astype(BF16)
            return 0

        lax.fori_loop(0, T // tc, chunk, 0)

    return _pcall(body, name=name, grid=(nb,),
                  in_specs=[pl.BlockSpec((T, cb), lambda i: (0, i)), pl.BlockSpec((T, cb), lambda i: (0, nb + i)),
                            pl.BlockSpec((T, cb), lambda i: (0, 2 * nb + i)), pl.BlockSpec((W, cb), lambda i: (0, i))],
                  out_specs=pl.BlockSpec((T, cb), lambda i: (0, i)), out_shape=jax.ShapeDtypeStruct((T, D), BF16),
                  scratch_shapes=[pltpu.VMEM((T + SEQ_PAD, cb), F32)],
                  compiler_params=_params(("parallel",)))(z, z, z, w)


def _sc_mid_bwd(name, z, dm, w, T):
    D = w.shape[1]
    W = w.shape[0]
    cb = min(D, LANES)
    nb = D // cb
    tc = _tile(T, 128)

    def body(b_ref, c_ref, v_ref, d_ref, w_ref, dz_ref, dw_ref, pp_ref, dq_ref, acc_ref):
        part = pl.program_id(0)
        pp_ref[pl.ds(0, SEQ_PAD), :] = jnp.zeros((SEQ_PAD, cb), F32)
        dq_ref[pl.ds(T, SEQ_PAD), :] = jnp.zeros((SEQ_PAD, cb), F32)
        acc_ref[...] = jnp.zeros((W * 8, cb), F32)

        def fill(c, _):
            base = pl.multiple_of(c * tc, tc)
            pp_ref[pl.ds(base + SEQ_PAD, tc), :] = c_ref[pl.ds(base, tc), :] * v_ref[pl.ds(base, tc), :]
            dq_ref[pl.ds(base, tc), :] = d_ref[pl.ds(base, tc), :] * b_ref[pl.ds(base, tc), :]
            return 0

        lax.fori_loop(0, T // tc, fill, 0)

        @pl.when(part == 0)
        def _():
            def chunk(c, _):
                base = pl.multiple_of(c * tc, tc)
                dq = dq_ref[pl.ds(base, tc), :]
                q = jnp.zeros((tc, cb), F32)
                for k in range(W):
                    p = pp_ref[pl.ds(base + SEQ_PAD - (W - 1) + k, tc), :]
                    q = q + w_ref[pl.ds(k, 1), :] * p
                    acc_ref[pl.ds(8 * k, 8), :] += jnp.sum((dq * p).reshape(tc // 8, 8, cb), axis=0)
                dz_ref[pl.ds(base, tc), :] = (d_ref[pl.ds(base, tc), :] * q).astype(BF16)
                return 0

            lax.fori_loop(0, T // tc, chunk, 0)

        @pl.when(part > 0)
        def _():
            def chunk(c, _):
                base = pl.multiple_of(c * tc, tc)
                dp = jnp.zeros((tc, cb), F32)
                for k in range(W):
                    dp = dp + w_ref[pl.ds(k, 1), :] * dq_ref[pl.ds(base + (W - 1) - k, tc), :]
                other = jnp.where(part == 1, v_ref[pl.ds(base, tc), :], c_ref[pl.ds(base, tc), :])
                dz_ref[pl.ds(base, tc), :] = (dp * other).astype(BF16)
                return 0

            lax.fori_loop(0, T // tc, chunk, 0)

        for k in range(W):
            dw_ref[pl.ds(k, 1), :] = jnp.sum(acc_ref[pl.ds(8 * k, 8), :], axis=0, keepdims=True)

    dz, dw3 = _pcall(body, name=name, grid=(3, nb),
                     in_specs=[pl.BlockSpec((T, cb), lambda p, i: (0, i)), pl.BlockSpec((T, cb), lambda p, i: (0, nb + i)),
                               pl.BlockSpec((T, cb), lambda p, i: (0, 2 * nb + i)), pl.BlockSpec((T, cb), lambda p, i: (0, i)),
                               pl.BlockSpec((W, cb), lambda p, i: (0, i))],
                     out_specs=[pl.BlockSpec((T, cb), lambda p, i: (0, p * nb + i)),
                                pl.BlockSpec((None, W, cb), lambda p, i: (p, 0, i))],
                     out_shape=[jax.ShapeDtypeStruct((T, 3 * D), BF16), jax.ShapeDtypeStruct((3, W, D), F32)],
                     scratch_shapes=[pltpu.VMEM((T + SEQ_PAD, cb), F32), pltpu.VMEM((T + SEQ_PAD, cb), F32),
                                     pltpu.VMEM((W * 8, cb), F32)],
                     compiler_params=_params(("parallel", "parallel")))(z, z, z, dm, w)
    return dz, dw3


def _ret_tables(dk, dv):
    C = RET_CHUNK
    lg = np.log1p(-np.exp2(-5.0 - np.arange(RET_HEADS, dtype=np.float64)))
    idx = np.arange(C, dtype=np.float64)
    rel = idx[:, None] - idx[None, :]
    mask = np.where(rel >= 0, np.exp(lg[:, None, None] * np.maximum(rel, 0.0)), 0.0)
    qd = np.exp(lg[:, None] * (idx + 1.0))[:, :, None] * np.ones((1, 1, dk))
    kd = np.exp(lg[:, None] * (C - 1.0 - idx))[:, :, None] * np.ones((1, 1, dk))
    cd = np.exp(lg * C)[:, None, None] * np.ones((1, 1, dv))
    return tuple(jnp.asarray(a, F32) for a in (mask, qd, kd, cd))


def _ret_fwd(name, q, k, v, tabs, T):
    H, C = RET_HEADS, RET_CHUNK
    dk, dv = q.shape[1] // H, v.shape[1] // H
    nch = T // C
    mask, qd, kd, cd = tabs

    def body(q_ref, k_ref, v_ref, m_ref, qd_ref, kd_ref, cd_ref, o_ref, st_ref, s_ref):
        n = pl.program_id(1)

        @pl.when(n == 0)
        def _():
            s_ref[...] = jnp.zeros((dk, dv), F32)

        qv, kv, vv = q_ref[...], k_ref[...], v_ref[...]
        s = s_ref[...]
        sb = s.astype(BF16)
        st_ref[...] = sb
        scores = lax.dot_general(qv, kv, NT, preferred_element_type=F32) * m_ref[...]
        intra = lax.dot_general(scores.astype(BF16), vv, NN, preferred_element_type=F32)
        cross = lax.dot_general((qv.astype(F32) * qd_ref[...]).astype(BF16), sb, NN, preferred_element_type=F32)
        o_ref[...] = intra + cross
        upd = lax.dot_general((kv.astype(F32) * kd_ref[...]).astype(BF16), vv, TN, preferred_element_type=F32)
        s_ref[...] = s * cd_ref[...] + upd

    return _pcall(body, name=name, grid=(H, nch),
                  in_specs=[pl.BlockSpec((C, dk), lambda h, n: (n, h)), pl.BlockSpec((C, dk), lambda h, n: (n, h)),
                            pl.BlockSpec((C, dv), lambda h, n: (n, h)), pl.BlockSpec((None, C, C), lambda h, n: (h, 0, 0)),
                            pl.BlockSpec((None, C, dk), lambda h, n: (h, 0, 0)), pl.BlockSpec((None, C, dk), lambda h, n: (h, 0, 0)),
                            pl.BlockSpec((None, 1, dv), lambda h, n: (h, 0, 0))],
                  out_specs=[pl.BlockSpec((C, dv), lambda h, n: (n, h)),
                             pl.BlockSpec((None, None, dk, dv), lambda h, n: (h, n, 0, 0))],
                  out_shape=[jax.ShapeDtypeStruct((T, H * dv), F32), jax.ShapeDtypeStruct((H, nch, dk, dv), BF16)],
                  scratch_shapes=[pltpu.VMEM((dk, dv), F32)],
                  compiler_params=_params(("parallel", "arbitrary")))(q, k, v, mask, qd, kd, cd)


def _ret_bwd(name, q, k, v, do, st, tabs, T):
    H, C = RET_HEADS, RET_CHUNK
    dk, dv = q.shape[1] // H, v.shape[1] // H
    nch = T // C
    mask, qd, kd, cd = tabs

    def body(q_ref, k_ref, v_ref, do_ref, st_ref, m_ref, qd_ref, kd_ref, cd_ref, dq_ref, dk_ref, dv_ref, ds_ref):
        n = pl.program_id(1)

        @pl.when(n == 0)
        def _():
            ds_ref[...] = jnp.zeros((dk, dv), F32)

        qv, kv, vv, dov, sb = q_ref[...], k_ref[...], v_ref[...], do_ref[...], st_ref[...]
        m = m_ref[...]
        ds = ds_ref[...]
        dsb = ds.astype(BF16)
        a = (lax.dot_general(qv, kv, NT, preferred_element_type=F32) * m).astype(BF16)
        dp = (lax.dot_general(dov, vv, NT, preferred_element_type=F32) * m).astype(BF16)
        qa = (qv.astype(F32) * qd_ref[...]).astype(BF16)
        kb = (kv.astype(F32) * kd_ref[...]).astype(BF16)
        dq_ref[...] = (lax.dot_general(dp, kv, NN, preferred_element_type=F32)
                       + qd_ref[...] * lax.dot_general(dov, sb, NT, preferred_element_type=F32))
        dk_ref[...] = (lax.dot_general(dp, qv, TN, preferred_element_type=F32)
                       + kd_ref[...] * lax.dot_general(vv, dsb, NT, preferred_element_type=F32))
        dv_ref[...] = (lax.dot_general(a, dov, TN, preferred_element_type=F32)
                       + lax.dot_general(kb, dsb, NN, preferred_element_type=F32)).astype(BF16)
        ds_ref[...] = ds * cd_ref[...] + lax.dot_general(qa, dov, TN, preferred_element_type=F32)

    rev = lambda h, n: (nch - 1 - n, h)
    return _pcall(body, name=name, grid=(H, nch),
                  in_specs=[pl.BlockSpec((C, dk), rev), pl.BlockSpec((C, dk), rev), pl.BlockSpec((C, dv), rev),
                            pl.BlockSpec((C, dv), rev), pl.BlockSpec((None, None, dk, dv), lambda h, n: (h, nch - 1 - n, 0, 0)),
                            pl.BlockSpec((None, C, C), lambda h, n: (h, 0, 0)), pl.BlockSpec((None, C, dk), lambda h, n: (h, 0, 0)),
                            pl.BlockSpec((None, C, dk), lambda h, n: (h, 0, 0)), pl.BlockSpec((None, 1, dv), lambda h, n: (h, 0, 0))],
                  out_specs=[pl.BlockSpec((C, dk), rev), pl.BlockSpec((C, dk), rev), pl.BlockSpec((C, dv), rev)],
                  out_shape=[jax.ShapeDtypeStruct((T, H * dk), F32), jax.ShapeDtypeStruct((T, H * dk), F32),
                             jax.ShapeDtypeStruct((T, H * dv), BF16)],
                  scratch_shapes=[pltpu.VMEM((dk, dv), F32)],
                  compiler_params=_params(("parallel", "arbitrary")))(q, k, v, do, st, mask, qd, kd, cd)


def _local_step(x, pos, tgt, wts, T, D):
    ng = wts["norm_g"]
    tr = _tile(T, 256)
    trw = _tile(T, 128)
    row = lambda i: ng[i:i + 1]
    g = {}

    u = _rowwise("rms_first", lambda xv, gv: _rms(xv, gv), [(x, D, 0), (row(0), None, None)], [("row", D, BF16)], T, tr)[0]
    h = x
    saved = []
    cos = sin = None
    tabs = None
    for l in range(DEPTH):
        mixer = l % 4
        sv = {"h_in": h, "u": u}
        if mixer == 0:
            z = _mm_cols("conv_in", u, wts["conv_w_in"], (), T, lambda acc, b: (acc + b,), (wts["conv_b_in"],))[0]
            v = _rowwise("conv_glu", _glu, [(z, D, 0), (z, D, 1)], [("row", D, F32)], T, tr)[0]
            c = _dwconv_fwd("conv_dw", v, wts["conv_dw"], wts["conv_dw_b"], T)
            s = _rowwise("conv_ln", _ln_silu, [(c, D, 0), (wts["conv_ln_g"], None, None), (wts["conv_ln_b"], None, None)],
                         [("row", D, BF16)], T, tr)[0]
            y = _mm_rows("conv_out", s, wts["conv_w_out"], T, lambda acc, b: (acc + b,), (wts["conv_b_out"],))
            sv.update(z=z, v=v, c=c, s=s)
        elif mixer == 1:
            mixed = _pool_fwd("pool_fwd", u, T)
            y, ycat = _pool_mm("pool_mm", "nn", mixed, wts["pool_w"], T, wts["pool_scale"])
            sv.update(mixed=mixed, ycat=ycat)
        elif mixer == 2:
            z = _mm_cols("sc_in", u, wts["sc_w_in"], (), T)[0]
            m = _sc_mid_fwd("sc_mid", z, wts["sc_dw"], T)
            y = _mm_rows("sc_out", m, wts["sc_w_out"], T)
            sv.update(z=z, m=m)
        else:
            dk = D // RET_HEADS
            half = dk // 2
            inv_freq = jnp.asarray((ROPE_BASE ** (-np.arange(half, dtype=np.float64) / half)).astype(np.float32)).reshape(1, half)

            def _cs(p, f):
                ang = p * f
                return jnp.cos(ang), jnp.sin(ang)

            cos, sin = _rowwise("rope_tab", _cs, [(pos, 1, 0), (inv_freq, None, None)],
                                [("row", half, F32), ("row", half, F32)], T, tr)
            tabs = _ret_tables(dk, 2 * dk)
            z = _mm_cols("ret_in", u, wts["ret_w_in"], (), T)[0]

            def _pre(zq, zk, zv, cv, sn):
                return _rot(zq, cv, sn, 1.0), _rot(zk, cv, sn, 1.0) * (dk ** -0.5), zv

            qr, kr, vb = _rowwise("ret_pre", _pre, [(z, D, 0), (z, D, 1), (z, 2 * D, 1), (cos, half, 0), (sin, half, 0)],
                                  [("row", D, BF16), ("row", D, BF16), ("row", 2 * D, BF16)], T, trw)
            o, st = _ret_fwd("ret_fwd", qr, kr, vb, tabs, T)
            out = _rowwise("ret_post", _gate_norm, [(o, 2 * D, 0), (z, 2 * D, 2)], [("row", 2 * D, BF16)], T, trw)[0]
            y = _mm_rows("ret_out", out, wts["ret_w_out"], T)
            sv.update(z=z, qr=qr, kr=kr, vb=vb, o=o, st=st, out=out)
        sv["y1"] = y
        h, u = _rowwise("bound_a", lambda hv, yv, gp, gn: (lambda hn: (hn, _rms(hn, gn)))(hv + _rms(yv, gp)),
                        [(h, D, 0), (y, D, 0), (row(4 * l + 1), None, None), (row(4 * l + 2), None, None)],
                        [("row", D, F32), ("row", D, BF16)], T, tr)
        sv["h_mid"], sv["u2"] = h, u
        r = _mm_cols("mlp_up", u, wts["mlp_up"], (l,), T, lambda acc: (jnp.maximum(acc, 0.0),), (), 1, BF16)[0]
        tm = _tile(T, 2048)
        y = _mm("mlp_down", "nn", (T // tm, 1, N_DEV),
                [(r, _spec((tm, r.shape[1] // N_DEV), lambda i, j, k: (i, k))),
                 (wts["mlp_down"], _spec((None, None, r.shape[1] // N_DEV, D), lambda i, j, k, l=l: (k, l, 0, 0)))],
                [((T, D), F32, _spec((tm, D), lambda i, j, k: (i, 0)))], pre=_square)[0]
        sv.update(r=r, y2=y)
        if l + 1 < DEPTH:
            udt = F32 if (l + 1) % 4 == 1 else BF16
            h, u = _rowwise("bound_b", lambda hv, yv, gp, gn: (lambda hn: (hn, _rms(hn, gn)))(hv + _rms(yv, gp)),
                            [(h, D, 0), (y, D, 0), (row(4 * l + 3), None, None), (row(4 * l + 4), None, None)],
                            [("row", D, F32), ("row", D, udt)], T, tr)
        saved.append(sv)

    def _final(hv, yv, gp, tv):
        hn = hv + _rms(yv, gp)
        e = hn - tv
        return e * (1.0 / D), _colsum(e * e)

    dh, loss_row = _rowwise("final", _final, [(h, D, 0), (y, D, 0), (row(4 * DEPTH - 1), None, None), (tgt, D, 0)],
                            [("row", D, F32), ("acc", (1, D))], T, tr)

    gn = [None] * (4 * DEPTH)
    F = saved[0]["r"].shape[1]
    g["mlp_up"] = jnp.zeros(wts["mlp_up"].shape, BF16)
    g["mlp_down"] = jnp.zeros(wts["mlp_down"].shape, BF16)
    du_next = None
    for l in reversed(range(DEPTH)):
        sv = saved[l]
        mixer = l % 4
        if du_next is None:
            def _bb(dhv, yv, gp):
                _, vjp = jax.vjp(_rms, yv, gp)
                dy, dgp = vjp(dhv)
                return dy, dgp

            dy, gn[4 * l + 3] = _rowwise("bwd_last", _bb, [(dh, D, 0), (sv["y2"], D, 0), (row(4 * l + 3), None, None)],
                                         [("row", D, BF16), ("acc", (1, D))], T, tr)
        else:
            dh, dy, gn[4 * l + 3], gn[4 * l + 4], _ = _bound_bwd("bwd_b", dh, du_next, saved[l + 1]["h_in"], sv["y2"],
                                                              row(4 * l + 3), row(4 * l + 4), T, D, tr)
        F8 = F // N_DEV
        tm = _tile(T, 2048)
        da = _mm("mlp_dhid", "nt", (T // tm, N_DEV, 1),
                 [(dy, _spec((tm, D), lambda i, j, k: (i, 0))),
                  (wts["mlp_down"], _spec((None, None, F8, D), lambda i, j, k, l=l: (j, l, 0, 0))),
                  (sv["r"], _spec((tm, F8), lambda i, j, k: (i, j)))],
                 [((T, F), BF16, _spec((tm, F8), lambda i, j, k: (i, j)))],
                 lambda acc, rr: (acc * (2.0 * rr.astype(F32)),))[0]
        tk = _tile(T, 2048)
        g["mlp_down"] = _mm_alias("mlp_dwdown", "tn", (1, N_DEV, T // tk),
                                  [(sv["r"], _spec((tk, F8), lambda i, j, k: (k, j))), (dy, _spec((tk, D), lambda i, j, k: (k, 0))),
                                   (g["mlp_down"], pl.BlockSpec(memory_space=pl.ANY))],
                                  [(wts["mlp_down"].shape, BF16, _spec((None, None, F8, D), lambda i, j, k, l=l: (j, l, 0, 0)))], {2: 0},
                                  pre=_square)
        g["mlp_up"] = _mm_cols_grad("mlp_dwup", sv["u2"], da, wts["mlp_up"].shape, (l,), T, into=g["mlp_up"])
        du = _mm_cols_t("mlp_du", da, wts["mlp_up"], (l,), T)
        dh, dy, gn[4 * l + 1], gn[4 * l + 2], dy_sum = _bound_bwd("bwd_a", dh, du, sv["h_mid"], sv["y1"],
                                                                row(4 * l + 1), row(4 * l + 2), T, D, tr)
        u = sv["u"]
        if mixer == 0:
            g["conv_b_out"] = dy_sum
            ds = _mm_rows_t("conv_ds", dy, wts["conv_w_out"], T)
            g["conv_w_out"] = _mm_rows_grad("conv_dwout", sv["s"], dy, T)

            def _lnb(cv, gv, bv, dsv):
                _, vjp = jax.vjp(_ln_silu, cv, gv, bv)
                dc, dg, db = vjp(dsv)
                return dc, dg, db, _colsum(dc)

            dc, g["conv_ln_g"], g["conv_ln_b"], g["conv_dw_b"] = _rowwise(
                "conv_ln_bwd", _lnb, [(sv["c"], D, 0), (wts["conv_ln_g"], None, None), (wts["conv_ln_b"], None, None), (ds, D, 0)],
                [("row", D, F32), ("acc", (1, D)), ("acc", (1, D)), ("acc", (1, D))], T, tr)
            dv, g["conv_dw"] = _dwconv_bwd("conv_dw_bwd", dc, sv["v"], wts["conv_dw"], T)

            def _glub(za, zg, dvv):
                _, vjp = jax.vjp(_glu, za, zg)
                dza, dzg = vjp(dvv)
                dz = jnp.concatenate([dza, dzg], axis=-1)
                return dz, _colsum(dz)

            dz, g["conv_b_in"] = _rowwise("conv_glu_bwd", _glub, [(sv["z"], D, 0), (sv["z"], D, 1), (dv, D, 0)],
                                          [("row", 2 * D, BF16), ("acc", (1, 2 * D))], T, tr)
            g["conv_w_in"] = _mm_cols_grad("conv_dwin", u, dz, wts["conv_w_in"].shape, (), T)
            du = _mm_cols_t("conv_du", dz, wts["conv_w_in"], (), T)
        elif mixer == 1:
            def _psb(dyv, yc, sc):
                return dyv * sc, _colsum(dyv.astype(F32) * yc.astype(F32))

            dyg, g["pool_scale"] = _rowwise("pool_scale_bwd", _psb, [(dy, D, 0), (sv["ycat"], D, 0), (wts["pool_scale"], None, None)],
                                            [("row", D, BF16), ("acc", (1, D))], T, tr)
            dmix = _pool_mm("pool_dmix", "nt", dyg, wts["pool_w"], T)
            g["pool_w"] = _pool_mm("pool_dw", "tn", sv["mixed"], dyg, T)
            du = _pool_bwd("pool_bwd", dmix, T)
        elif mixer == 2:
            dm = _mm_rows_t("sc_dm", dy, wts["sc_w_out"], T)
            g["sc_w_out"] = _mm_rows_grad("sc_dwout", sv["m"], dy, T)
            dz, dw3 = _sc_mid_bwd("sc_mid_bwd", sv["z"], dm, wts["sc_dw"], T)
            g["sc_dw"] = dw3[0]
            g["sc_w_in"] = _mm_cols_grad("sc_dwin", u, dz, wts["sc_w_in"].shape, (), T)
            du = _mm_cols_t("sc_du", dz, wts["sc_w_in"], (), T)
        else:
            dk = D // RET_HEADS
            half = dk // 2
            dout = _mm_rows_t("ret_dout", dy, wts["ret_w_out"], T)
            g["ret_w_out"] = _mm_rows_grad("ret_dwout", sv["out"], dy, T)

            def _postb(ov, gv, dv_):
                _, vjp = jax.vjp(_gate_norm, ov, gv)
                return vjp(dv_)

            do, dzg = _rowwise("ret_post_bwd", _postb, [(sv["o"], 2 * D, 0), (sv["z"], 2 * D, 2), (dout, 2 * D, 0)],
                               [("row", 2 * D, BF16), ("row", 2 * D, BF16)], T, trw)
            dqr, dkr, dvb = _ret_bwd("ret_bwd", sv["qr"], sv["kr"], sv["vb"], do, sv["st"], tabs, T)

            def _preb(dq, dk_, dvv, dgv, cv, sn):
                return jnp.concatenate([_rot(dq, cv, sn, -1.0).astype(BF16), (_rot(dk_, cv, sn, -1.0) * (dk ** -0.5)).astype(BF16),
                                        dvv, dgv], axis=-1)

            dz = _rowwise("ret_pre_bwd", _preb, [(dqr, D, 0), (dkr, D, 0), (dvb, 2 * D, 0), (dzg, 2 * D, 0), (cos, half, 0), (sin, half, 0)],
                          [("row", 6 * D, BF16)], T, trw)[0]
            g["ret_w_in"] = _mm_cols_grad("ret_dwin", u, dz, wts["ret_w_in"].shape, (), T)
            du = _mm_cols_t("ret_du", dz, wts["ret_w_in"], (), T)
        du_next = du

    def _fb(dhv, duv, xv, gv):
        _, vjp = jax.vjp(_rms, xv, gv)
        dx, dg = vjp(duv)
        return dhv + dx, dg

    grad_x, gn[0] = _rowwise("bwd_first", _fb, [(dh, D, 0), (du_next, D, 0), (x, D, 0), (row(0), None, None)],
                             [("row", D, F32), ("acc", (1, D))], T, tr)
    g["norm_g"] = jnp.concatenate(gn, axis=0)
    return loss_row, grad_x, g


def _bound_bwd(name, dh, du, h_new, y, g_post, g_next, T, D, tr):
    def fn(dhv, duv, hn, yv, gp, gnx):
        _, vjp1 = jax.vjp(_rms, hn, gnx)
        d1, dgn = vjp1(duv.astype(F32))
        dht = dhv + d1
        _, vjp2 = jax.vjp(_rms, yv, gp)
        dy, dgp = vjp2(dht)
        return dht, dy, dgp, dgn, _colsum(dy)

    return _rowwise(name, fn, [(dh, D, 0), (du, D, 0), (h_new, D, 0), (y, D, 0), (g_post, None, None), (g_next, None, None)],
                    [("row", D, F32), ("row", D, BF16), ("acc", (1, D)), ("acc", (1, D)), ("acc", (1, D))], T, tr)


def _adamw(name, parts, w, m, v):
    P, R, C = parts.shape
    tr = _tile(R, 256)

    def body(p_ref, w_ref, m_ref, v_ref, g_ref, d_ref, m2_ref, v2_ref):
        g = p_ref[0].astype(F32)
        for i in range(1, P):
            g = g + p_ref[i].astype(F32)
        wv = w_ref[...]
        m2 = ADAM_B1 * m_ref[...] + (1.0 - ADAM_B1) * g
        v2 = ADAM_B2 * v_ref[...] + (1.0 - ADAM_B2) * (g * g)
        m_hat = m2 / (1.0 - ADAM_B1 ** ADAM_STEP)
        v_hat = v2 / (1.0 - ADAM_B2 ** ADAM_STEP)
        g_ref[...] = g
        d_ref[...] = -ADAM_LR * (m_hat / (jnp.sqrt(v_hat) + ADAM_EPS) + ADAM_WD * wv)
        m2_ref[...] = m2
        v2_ref[...] = v2

    blk = pl.BlockSpec((tr, C), lambda i: (i, 0))
    return _pcall(body, name=name, grid=(R // tr,),
                  in_specs=[pl.BlockSpec((P, tr, C), lambda i: (0, i, 0)), blk, blk, blk], out_specs=[blk] * 4,
                  out_shape=[jax.ShapeDtypeStruct((R, C), F32)] * 4, compiler_params=_params(("parallel",)))(parts, w, m, v)


def _pair_add(name, a, b):
    Q, _, R, C = a.shape
    tr = _tile(R, 1024)
    blk = pl.BlockSpec((None, tr, C), lambda q, i: (q, i, 0))

    def body(a_ref, b_ref, o_ref):
        mine = jnp.where(lax.axis_index("c") == 0, a_ref[0], a_ref[1])
        o_ref[...] = (mine.astype(F32) + b_ref[...].astype(F32)).astype(BF16)

    return _pcall(body, name=name, grid=(Q, R // tr),
                  in_specs=[pl.BlockSpec((None, 2, tr, C), lambda q, i: (q, 0, i, 0)), blk], out_specs=blk,
                  out_shape=jax.ShapeDtypeStruct((Q, R, C), BF16), compiler_params=_params(("parallel", "parallel")))(a, b)


ANY = pl.BlockSpec(memory_space=pl.ANY)


def _place():
    x, y, c = lax.axis_index("x"), lax.axis_index("y"), lax.axis_index("c")
    chips = [(1 - x, y), (x, 1 - y), (1 - x, 1 - y)]
    return x, y, c, chips


def _gather_all(locs):
    n = len(locs)

    def body(*refs):
        ins, outs = refs[:n], refs[n:2 * n]
        send, recv, lsem = refs[2 * n:]
        x, y, c, chips = _place()
        me = 4 * x + 2 * y + c
        sib = (x, y, 1 - c)

        def cp(a, k, slot, to, src=None):
            return pltpu.make_async_remote_copy(src_ref=outs[a].at[slot] if src is None else src, dst_ref=outs[a].at[slot],
                                                send_sem=send.at[a, k], recv_sem=recv.at[a, k], device_id=to, device_id_type=MESH)

        mine = [pltpu.make_async_copy(ins[a], outs[a].at[me], lsem.at[a]) for a in range(n)]
        for d in mine:
            d.start()
        first = []
        for a in range(n):
            first.append(cp(a, 0, me, sib, src=ins[a]))
            for j, chip in enumerate(chips):
                first.append(cp(a, 1 + j, me, (chip[0], chip[1], c), src=ins[a]))
        for d in first:
            d.start()
        passed = []
        for j, chip in enumerate(chips):
            slot = 4 * chip[0] + 2 * chip[1] + c
            for a in range(n):
                cp(a, 1 + j, slot, sib).wait_recv()
                p = cp(a, 4 + j, slot, sib)
                p.start()
                passed.append(p)
        for a in range(n):
            cp(a, 0, 4 * x + 2 * y + (1 - c), sib).wait_recv()
        for j, chip in enumerate(chips):
            for a in range(n):
                cp(a, 4 + j, 4 * chip[0] + 2 * chip[1] + (1 - c), sib).wait_recv()
        for d in first + passed:
            d.wait_send()
        for d in mine:
            d.wait()

    return _pcall(body, name="gather_all", in_specs=[ANY] * n, out_specs=[ANY] * n,
                  out_shape=[jax.ShapeDtypeStruct((N_DEV,) + a.shape, a.dtype) for a in locs],
                  scratch_shapes=[pltpu.SemaphoreType.DMA((n, 7)), pltpu.SemaphoreType.DMA((n, 7)), pltpu.SemaphoreType.DMA((n,))],
                  compiler_params=pltpu.CompilerParams(has_side_effects=True))(*locs)


def _pair_exchange(gs):
    n = len(gs)

    def body(*refs):
        ins, got = refs[:n], refs[n:2 * n]
        send, recv = refs[2 * n:]
        x, y, c, _ = _place()
        sib = (x, y, 1 - c)
        rem = [pltpu.make_async_remote_copy(src_ref=ins[a].at[q, 1 - c], dst_ref=got[a].at[q], send_sem=send.at[a, q],
                                            recv_sem=recv.at[a, q], device_id=sib, device_id_type=MESH)
               for a in range(n) for q in range(N_CHIP)]
        for d in rem:
            d.start()
        for d in rem:
            d.wait()

    half = [jax.ShapeDtypeStruct((N_CHIP,) + g.shape[2:], g.dtype) for g in gs]
    return _pcall(body, name="pair_exchange", in_specs=[ANY] * n, out_specs=[ANY] * n, out_shape=half,
                  scratch_shapes=[pltpu.SemaphoreType.DMA((n, N_CHIP)), pltpu.SemaphoreType.DMA((n, N_CHIP))],
                  compiler_params=pltpu.CompilerParams(has_side_effects=True))(*gs)


def _chip_exchange(cs, sm_sh, sm_rep):
    n = len(cs)

    def body(*refs):
        ins, sh, rep = refs[:n], refs[n], refs[n + 1]
        outs, osh, orep = refs[n + 2:2 * n + 2], refs[2 * n + 2], refs[2 * n + 3]
        send, recv, lsem, ssend, srecv, slsem = refs[2 * n + 4:]
        x, y, c, chips = _place()
        qme = 2 * x + y
        me = 4 * x + 2 * y + c
        loc = [pltpu.make_async_copy(ins[a].at[qme], outs[a].at[qme], lsem.at[a]) for a in range(n)]
        loc.append(pltpu.make_async_copy(sh.at[me], osh.at[me], slsem.at[0]))
        loc.append(pltpu.make_async_copy(rep, orep.at[me], slsem.at[1]))
        rem, waits = [], []
        for j, chip in enumerate(chips):
            q = 2 * chip[0] + chip[1]
            to = (chip[0], chip[1], c)
            for a in range(n):
                rem.append(pltpu.make_async_remote_copy(src_ref=ins[a].at[q], dst_ref=outs[a].at[qme], send_sem=send.at[a, j],
                                                        recv_sem=recv.at[a, j], device_id=to, device_id_type=MESH))
                waits.append(pltpu.make_async_remote_copy(src_ref=ins[a].at[q], dst_ref=outs[a].at[q], send_sem=send.at[a, j],
                                                          recv_sem=recv.at[a, j], device_id=to, device_id_type=MESH))
        for r in range(1, N_DEV):
            fx, fy, fc = (r >> 2) & 1, (r >> 1) & 1, r & 1
            px = 1 - x if fx else x
            py = 1 - y if fy else y
            pc = 1 - c if fc else c
            t = 4 * px + 2 * py + pc
            to = (px, py, pc)
            rem.append(pltpu.make_async_remote_copy(src_ref=sh.at[t], dst_ref=osh.at[me], send_sem=ssend.at[0, r - 1],
                                                    recv_sem=srecv.at[0, r - 1], device_id=to, device_id_type=MESH))
            waits.append(pltpu.make_async_remote_copy(src_ref=sh.at[t], dst_ref=osh.at[t], send_sem=ssend.at[0, r - 1],
                                                      recv_sem=srecv.at[0, r - 1], device_id=to, device_id_type=MESH))
            rem.append(pltpu.make_async_remote_copy(src_ref=rep, dst_ref=orep.at[me], send_sem=ssend.at[1, r - 1],
                                                    recv_sem=srecv.at[1, r - 1], device_id=to, device_id_type=MESH))
            waits.append(pltpu.make_async_remote_copy(src_ref=rep, dst_ref=orep.at[t], send_sem=ssend.at[1, r - 1],
                                                      recv_sem=srecv.at[1, r - 1], device_id=to, device_id_type=MESH))
        for d in loc + rem:
            d.start()
        for d in waits:
            d.wait_recv()
        for d in rem:
            d.wait_send()
        for d in loc:
            d.wait()

    out_shape = [jax.ShapeDtypeStruct(a.shape, a.dtype) for a in cs]
    out_shape += [jax.ShapeDtypeStruct(sm_sh.shape, F32), jax.ShapeDtypeStruct((N_DEV,) + sm_rep.shape, F32)]
    res = _pcall(body, name="chip_exchange", in_specs=[ANY] * (n + 2), out_specs=[ANY] * (n + 2), out_shape=out_shape,
                 scratch_shapes=[pltpu.SemaphoreType.DMA((n, 3)), pltpu.SemaphoreType.DMA((n, 3)), pltpu.SemaphoreType.DMA((n,)),
                                 pltpu.SemaphoreType.DMA((2, 7)), pltpu.SemaphoreType.DMA((2, 7)), pltpu.SemaphoreType.DMA((2,))],
                 compiler_params=pltpu.CompilerParams(has_side_effects=True))(*cs, sm_sh, sm_rep)
    return res[:n], res[n], res[n + 1]


BIG = ("mlp_up", "mlp_down", "conv_w_in", "conv_w_out", "pool_w", "sc_w_in", "sc_w_out", "ret_w_in", "ret_w_out")
SMALL_SPLIT = ("norm_g", "conv_dw", "sc_dw")
SMALL_REP = ("conv_b_in", "conv_dw_b", "conv_ln_g", "conv_ln_b", "conv_b_out", "pool_scale")
WEIGHTS = ("norm_g", "mlp_up", "mlp_down", "conv_w_in", "conv_b_in", "conv_dw", "conv_dw_b", "conv_ln_g", "conv_ln_b",
           "conv_w_out", "conv_b_out", "pool_w", "pool_scale", "sc_w_in", "sc_dw", "sc_w_out", "ret_w_in", "ret_w_out")


def _pad_rows(a, rows):
    return jnp.pad(a, ((0, rows - a.shape[0]), (0, 0)))


def _split_rows(names, shards):
    out = []
    for nme in names:
        r = int(np.prod(shards[nme].shape[:-1]))
        out.append((nme, r, -(-r // 8) * 8))
    return out


def _pack_split(names, arrs):
    parts = []
    for nme, r, rp in _split_rows(names, arrs):
        parts.append(_pad_rows(arrs[nme].reshape(r, arrs[nme].shape[-1]), rp))
    return jnp.concatenate(parts, axis=0)


def _pack_rep(arrs, D):
    parts = [arrs[nme].reshape(-1, D) for nme in SMALL_REP]
    rows = sum(p.shape[0] for p in parts)
    return _pad_rows(jnp.concatenate(parts, axis=0), -(-rows // 8) * 8)


def kernel(x, positions, norm_g, mlp_up, mlp_down, conv_w_in, conv_b_in, conv_dw, conv_dw_b, conv_ln_g, conv_ln_b, conv_w_out, conv_b_out, pool_w, pool_scale, sc_w_in, sc_dw, sc_w_out, ret_w_in, ret_w_out, loss_target, m_norm_g, m_mlp_up, m_mlp_down, m_conv_w_in, m_conv_b_in, m_conv_dw, m_conv_dw_b, m_conv_ln_g, m_conv_ln_b, m_conv_w_out, m_conv_b_out, m_pool_w, m_pool_scale, m_sc_w_in, m_sc_dw, m_sc_w_out, m_ret_w_in, m_ret_w_out, v_norm_g, v_mlp_up, v_mlp_down, v_conv_w_in, v_conv_b_in, v_conv_dw, v_conv_dw_b, v_conv_ln_g, v_conv_ln_b, v_conv_w_out, v_conv_b_out, v_pool_w, v_pool_scale, v_sc_w_in, v_sc_dw, v_sc_w_out, v_ret_w_in, v_ret_w_out):
    w = dict(norm_g=norm_g, mlp_up=mlp_up, mlp_down=mlp_down, conv_w_in=conv_w_in, conv_b_in=conv_b_in, conv_dw=conv_dw,
             conv_dw_b=conv_dw_b, conv_ln_g=conv_ln_g, conv_ln_b=conv_ln_b, conv_w_out=conv_w_out, conv_b_out=conv_b_out,
             pool_w=pool_w, pool_scale=pool_scale, sc_w_in=sc_w_in, sc_dw=sc_dw, sc_w_out=sc_w_out, ret_w_in=ret_w_in, ret_w_out=ret_w_out)
    m = dict(norm_g=m_norm_g, mlp_up=m_mlp_up, mlp_down=m_mlp_down, conv_w_in=m_conv_w_in, conv_b_in=m_conv_b_in, conv_dw=m_conv_dw,
             conv_dw_b=m_conv_dw_b, conv_ln_g=m_conv_ln_g, conv_ln_b=m_conv_ln_b, conv_w_out=m_conv_w_out, conv_b_out=m_conv_b_out,
             pool_w=m_pool_w, pool_scale=m_pool_scale, sc_w_in=m_sc_w_in, sc_dw=m_sc_dw, sc_w_out=m_sc_w_out, ret_w_in=m_ret_w_in,
             ret_w_out=m_ret_w_out)
    v = dict(norm_g=v_norm_g, mlp_up=v_mlp_up, mlp_down=v_mlp_down, conv_w_in=v_conv_w_in, conv_b_in=v_conv_b_in, conv_dw=v_conv_dw,
             conv_dw_b=v_conv_dw_b, conv_ln_g=v_conv_ln_g, conv_ln_b=v_conv_ln_b, conv_w_out=v_conv_w_out, conv_b_out=v_conv_b_out,
             pool_w=v_pool_w, pool_scale=v_pool_scale, sc_w_in=v_sc_w_in, sc_dw=v_sc_dw, sc_w_out=v_sc_w_out, ret_w_in=v_ret_w_in,
             ret_w_out=v_ret_w_out)
    T, D = x.shape[1], x.shape[2]
    L = D // N_DEV
    G = len(POOL_WINDOWS)
    cg = D // G

    locs = [w[k].astype(BF16) for k in BIG] + [_pack_split(SMALL_SPLIT, w)]
    gath = _gather_all(locs)
    gw = dict(zip(BIG, gath[:len(BIG)]))
    small = gath[len(BIG)]
    wts = {"mlp_up": gw["mlp_up"], "mlp_down": gw["mlp_down"], "conv_w_in": gw["conv_w_in"], "sc_w_in": gw["sc_w_in"],
           "ret_w_in": gw["ret_w_in"]}
    for k in ("conv_w_out", "sc_w_out", "ret_w_out"):
        wts[k] = gw[k].reshape(-1, D)
    wts["pool_w"] = gw["pool_w"].transpose(1, 0, 2, 3).reshape(G, cg, cg)
    off = 0
    for nme, r, rp in _split_rows(SMALL_SPLIT, w):
        wts[nme] = small[:, off:off + r, :].transpose(1, 0, 2).reshape(r, D)
        off += rp
    for k in SMALL_REP:
        wts[k] = w[k].reshape(1, -1)

    pos = positions.reshape(T, 1).astype(F32)
    loss_row, grad_x, g = _local_step(x.reshape(T, D), pos, loss_target.reshape(T, D), wts, T, D)
    loss = lax.psum(0.5 * jnp.sum(loss_row) / D, AXES)

    big = dict(g)
    for k in ("conv_w_out", "sc_w_out", "ret_w_out"):
        big[k] = g[k].reshape((N_DEV,) + w[k].shape)
    big["pool_w"] = g["pool_w"].reshape(G, N_DEV, cg // N_DEV, cg).transpose(1, 0, 2, 3).astype(BF16)
    stacks = []
    for k in BIG:
        C = w[k].shape[-1]
        stacks.append(big[k].reshape(N_CHIP, 2, -1, C))
    got = _pair_exchange(stacks)
    sums = [_pair_add("pair_add", a, b) for a, b in zip(stacks, got)]
    gsm = {}
    for nme, r, rp in _split_rows(SMALL_SPLIT, w):
        gsm[nme] = jnp.pad(g[nme].reshape(r, N_DEV, L), ((0, rp - r), (0, 0), (0, 0)))
    sm_sh = jnp.concatenate([gsm[nme] for nme in SMALL_SPLIT], axis=0).transpose(1, 0, 2)
    sm_rep = _pack_rep(g, D)
    parts, r_sh, r_rep = _chip_exchange(sums, sm_sh, sm_rep)

    res = {}
    for k, p in zip(BIG, parts):
        C = w[k].shape[-1]
        out = _adamw("adamw_" + k, p, w[k].reshape(-1, C), m[k].reshape(-1, C), v[k].reshape(-1, C))
        res[k] = [o.reshape(w[k].shape) for o in out]
    out = _adamw("adamw_split", r_sh, _pack_split(SMALL_SPLIT, w), _pack_split(SMALL_SPLIT, m), _pack_split(SMALL_SPLIT, v))
    off = 0
    for nme, r, rp in _split_rows(SMALL_SPLIT, w):
        res[nme] = [o[off:off + r].reshape(w[nme].shape) for o in out]
        off += rp
    out = _adamw("adamw_rep", r_rep, _pack_rep(w, D), _pack_rep(m, D), _pack_rep(v, D))
    off = 0
    for nme in SMALL_REP:
        r = w[nme].shape[0] // D
        res[nme] = [o[off:off + r].reshape(w[nme].shape) for o in out]
        off += r
    outs = [loss, grad_x.reshape(x.shape)]
    for i in range(4):
        outs += [res[k][i] for k in WEIGHTS]
    return tuple(outs)
```

```python
import functools
import math

import numpy as np
import jax
import jax.numpy as jnp
from jax import lax
from jax.experimental import pallas as pl
from jax.experimental.pallas import tpu as pltpu

F32, BF16 = jnp.float32, jnp.bfloat16
MESH = pl.DeviceIdType.MESH
AXES = ("x", "y", "c")
N_DEV = 8
N_CHIP = 4
EPS = 1e-6
DEPTH = 4
CONV_WIDTH = 31
POOL_WINDOWS = (2, 4, 8, 16)
SHORT_CONV_WIDTH = 3
RET_HEADS = 4
RET_CHUNK = 128
ROPE_BASE = 10000.0
ADAM_LR, ADAM_B1, ADAM_B2, ADAM_EPS, ADAM_WD, ADAM_STEP = 0.001, 0.9, 0.999, 1e-08, 0.01, 10
LANES = 128
SEQ_PAD = 32
VMEM_LIMIT = 56 * 1024 * 1024

NN = (((1,), (0,)), ((), ()))
NT = (((1,), (1,)), ((), ()))
TN = (((0,), (0,)), ((), ()))


def _pcall(body, **kw):
    return pl.pallas_call(body, **kw)


def _params(sem):
    return pltpu.CompilerParams(dimension_semantics=sem, vmem_limit_bytes=VMEM_LIMIT)


def _tile(n, pref):
    t = min(n, pref)
    while n % t:
        t //= 2
    return t


def _rowwise(name, fn, ins, outs, rows, tr, deps=()):
    n_in = len(ins)
    n_dep = len(deps)
    in_specs = []
    for arr, nc, cb in ins:
        if nc is None:
            in_specs.append(pl.BlockSpec(arr.shape, lambda i, nd=arr.ndim: (0,) * nd))
        else:
            in_specs.append(pl.BlockSpec((tr, nc), lambda i, cb=cb: (i, cb)))
    out_specs, out_shapes = [], []
    for o in outs:
        if o[0] == "row":
            out_shapes.append(jax.ShapeDtypeStruct((rows, o[1]), o[2]))
            out_specs.append(pl.BlockSpec((tr, o[1]), lambda i: (i, 0)))
        else:
            out_shapes.append(jax.ShapeDtypeStruct(o[1], F32))
            out_specs.append(pl.BlockSpec(o[1], lambda i: (0, 0)))

    in_specs = in_specs + [pl.BlockSpec(memory_space=pl.ANY)] * n_dep

    def body(*refs):
        res = fn(*[r[...] for r in refs[:n_in]])
        if not isinstance(res, (tuple, list)):
            res = (res,)
        i = pl.program_id(0)
        for o, ref, r in zip(outs, refs[n_in + n_dep:], res):
            if o[0] == "row":
                ref[...] = r.astype(o[2])
            else:
                @pl.when(i == 0)
                def _(ref=ref, r=r):
                    ref[...] = r

                @pl.when(i > 0)
                def _(ref=ref, r=r):
                    ref[...] += r

    return _pcall(body, name=name, grid=(rows // tr,), in_specs=in_specs, out_specs=out_specs,
                  out_shape=out_shapes, compiler_params=_params(("arbitrary",)))(*[a for a, _, _ in ins], *deps)


def _rms(x, g):
    return x * lax.rsqrt(jnp.mean(x * x, axis=-1, keepdims=True) + EPS) * g


def _colsum(x):
    return jnp.sum(x, axis=0, keepdims=True)


def _square(r):
    rf = r.astype(F32)
    return (rf * rf).astype(r.dtype)


def _ln_silu(c, g, b):
    mu = jnp.mean(c, axis=-1, keepdims=True)
    xc = c - mu
    y = xc * lax.rsqrt(jnp.mean(xc * xc, axis=-1, keepdims=True) + EPS) * g + b
    return y * jax.nn.sigmoid(y)


def _glu(za, zg):
    return za * jax.nn.sigmoid(zg)


def _gate_norm(o, g):
    dv = o.shape[-1] // RET_HEADS
    parts = []
    for h in range(RET_HEADS):
        oh = o[:, h * dv:(h + 1) * dv]
        parts.append(oh * lax.rsqrt(jnp.mean(oh * oh, axis=-1, keepdims=True) + EPS))
    return (g * jax.nn.sigmoid(g)) * jnp.concatenate(parts, axis=-1)


def _rot(x, cos, sin, sign):
    dk = x.shape[-1] // RET_HEADS
    half = dk // 2
    parts = []
    for h in range(RET_HEADS):
        x1 = x[:, h * dk:h * dk + half]
        x2 = x[:, h * dk + half:(h + 1) * dk]
        parts.append(x1 * cos - sign * (x2 * sin))
        parts.append(sign * (x1 * sin) + x2 * cos)
    return jnp.concatenate(parts, axis=-1)


def _mm(name, mode, grid, ins, outs, epi=None, aliases=None, pre=None):
    dn = {"nn": NN, "nt": NT, "tn": TN}[mode]
    gk = grid[2]
    n_in, n_out = len(ins), len(outs)

    def body(*refs):
        out_refs = refs[n_in:n_in + n_out]
        a = refs[0][...] if pre is None else pre(refs[0][...])
        part = lax.dot_general(a, refs[1][...], dn, preferred_element_type=F32)

        def finish(acc):
            res = epi(acc, *[r[...] for r in refs[2:n_in]]) if epi is not None else (acc,)
            for o, r in zip(out_refs, res):
                o[...] = r.astype(o.dtype)

        if gk == 1:
            finish(part)
        else:
            acc_ref = refs[-1]
            k = pl.program_id(2)

            @pl.when(k == 0)
            def _():
                acc_ref[...] = part

            @pl.when(k > 0)
            def _():
                acc_ref[...] += part

            @pl.when(k == gk - 1)
            def _():
                finish(acc_ref[...])

    scratch = []
    if gk > 1:
        blk = [d for d in outs[0][2].block_shape if d is not None]
        scratch = [pltpu.VMEM(tuple(blk), F32)]
    kw = {}
    if aliases:
        kw["input_output_aliases"] = aliases
    return _pcall(body, name=name, grid=grid, in_specs=[s for _, s in ins], out_specs=[s for _, _, s in outs],
                  out_shape=[jax.ShapeDtypeStruct(sh, dt) for sh, dt, _ in outs], scratch_shapes=scratch,
                  compiler_params=_params(("parallel", "parallel", "arbitrary")), **kw)(*[a for a, _ in ins])


def _spec(block, fn):
    return pl.BlockSpec(block, fn)


def _mm_cols(name, a, wg, lead, T, epi=None, extras=(), n_out=1, out_dtype=F32):
    K, nb = wg.shape[-2], wg.shape[-1]
    tm = _tile(T, 2048)
    wblock = (None,) * (1 + len(lead)) + (K, nb)
    ins = [(a, _spec((tm, K), lambda i, j, k: (i, 0))), (wg, _spec(wblock, lambda i, j, k: (j,) + tuple(lead) + (0, 0)))]
    for e in extras:
        if e.shape[0] == 1:
            ins.append((e, _spec((1, nb), lambda i, j, k: (0, j))))
        else:
            ins.append((e, _spec((tm, nb), lambda i, j, k: (i, j))))
    outs = [((T, N_DEV * nb), out_dtype, _spec((tm, nb), lambda i, j, k: (i, j))) for _ in range(n_out)]
    return _mm(name, "nn", (T // tm, N_DEV, 1), ins, outs, epi)


def _mm_cols_t(name, d, wg, lead, T, epi=None, extras=(), out_dtype=F32):
    K, nb = wg.shape[-2], wg.shape[-1]
    tm = _tile(T, 2048)
    wblock = (None,) * (1 + len(lead)) + (K, nb)
    ins = [(d, _spec((tm, nb), lambda i, j, k: (i, k))), (wg, _spec(wblock, lambda i, j, k: (k,) + tuple(lead) + (0, 0)))]
    for e in extras:
        ins.append((e, _spec((tm, K), lambda i, j, k: (i, 0))))
    outs = [((T, K), out_dtype, _spec((tm, K), lambda i, j, k: (i, 0)))]
    return _mm(name, "nt", (T // tm, 1, N_DEV), ins, outs, epi)[0]


def _mm_cols_grad(name, a, d, wg_shape, lead, T):
    K, nb = wg_shape[-2], wg_shape[-1]
    tk = _tile(T, 2048)
    wblock = (None,) * (1 + len(lead)) + (K, nb)
    ins = [(a, _spec((tk, K), lambda i, j, k: (k, 0))), (d, _spec((tk, nb), lambda i, j, k: (k, j)))]
    outs = [(tuple(wg_shape), BF16, _spec(wblock, lambda i, j, k: (j,) + tuple(lead) + (0, 0)))]
    return _mm(name, "tn", (1, N_DEV, T // tk), ins, outs, lambda acc: (acc,))[0]


def _mm_rows(name, a, w2, T, epi=None, extras=(), out_dtype=F32):
    Kin, N = w2.shape
    tm, tn = _tile(T, 2048), _tile(N, 512)
    ins = [(a, _spec((tm, Kin), lambda i, j, k: (i, 0))), (w2, _spec((Kin, tn), lambda i, j, k: (0, j)))]
    for e in extras:
        ins.append((e, _spec((1, tn), lambda i, j, k: (0, j))))
    outs = [((T, N), out_dtype, _spec((tm, tn), lambda i, j, k: (i, j)))]
    return _mm(name, "nn", (T // tm, N // tn, 1), ins, outs, epi)[0]


def _mm_rows_t(name, d, w2, T, out_dtype=F32):
    Kin, N = w2.shape
    tm, tn = _tile(T, 2048), _tile(Kin, 512)
    ins = [(d, _spec((tm, N), lambda i, j, k: (i, 0))), (w2, _spec((tn, N), lambda i, j, k: (j, 0)))]
    outs = [((T, Kin), out_dtype, _spec((tm, tn), lambda i, j, k: (i, j)))]
    return _mm(name, "nt", (T // tm, Kin // tn, 1), ins, outs)[0]


def _mm_rows_grad(name, a, d, T):
    Kin, N = a.shape[1], d.shape[1]
    tk, tm, tn = _tile(T, 2048), _tile(Kin, 512), _tile(N, 1024)
    ins = [(a, _spec((tk, tm), lambda i, j, k: (k, i))), (d, _spec((tk, tn), lambda i, j, k: (k, j)))]
    outs = [((Kin, N), BF16, _spec((tm, tn), lambda i, j, k: (i, j)))]
    return _mm(name, "tn", (Kin // tm, N // tn, T // tk), ins, outs, lambda acc: (acc,))[0]


def _dwconv_fwd(name, x, w, bias, T):
    W, C = w.shape
    cb = min(C, LANES)
    tc = _tile(T, 256)

    def body(x_ref, w_ref, b_ref, o_ref, xp_ref):
        xp_ref[pl.ds(0, SEQ_PAD), :] = jnp.zeros((SEQ_PAD, cb), F32)

        def fill(c, _):
            base = pl.multiple_of(c * tc, tc)
            xp_ref[pl.ds(base + SEQ_PAD, tc), :] = x_ref[pl.ds(base, tc), :]
            return 0

        lax.fori_loop(0, T // tc, fill, 0)

        def chunk(c, _):
            base = pl.multiple_of(c * tc, tc)
            acc = jnp.zeros((tc, cb), F32) + b_ref[...]
            for k in range(W):
                acc = acc + w_ref[pl.ds(k, 1), :] * xp_ref[pl.ds(base + SEQ_PAD - (W - 1) + k, tc), :]
            o_ref[pl.ds(base, tc), :] = acc
            return 0

        lax.fori_loop(0, T // tc, chunk, 0)

    return _pcall(body, name=name, grid=(C // cb,),
                  in_specs=[pl.BlockSpec((T, cb), lambda i: (0, i)), pl.BlockSpec((W, cb), lambda i: (0, i)),
                            pl.BlockSpec((1, cb), lambda i: (0, i))],
                  out_specs=pl.BlockSpec((T, cb), lambda i: (0, i)), out_shape=jax.ShapeDtypeStruct((T, C), F32),
                  scratch_shapes=[pltpu.VMEM((T + SEQ_PAD, cb), F32)],
                  compiler_params=_params(("parallel",)))(x, w, bias)


def _dwconv_bwd(name, dc, x, w, T):
    W, C = w.shape
    cb = min(C, LANES)
    tc = _tile(T, 128)

    def body(dc_ref, x_ref, w_ref, dx_ref, dw_ref, xp_ref, dp_ref, acc_ref):
        xp_ref[pl.ds(0, SEQ_PAD), :] = jnp.zeros((SEQ_PAD, cb), F32)
        dp_ref[pl.ds(T, SEQ_PAD), :] = jnp.zeros((SEQ_PAD, cb), F32)
        acc_ref[...] = jnp.zeros((W * 8, cb), F32)

        def fill(c, _):
            base = pl.multiple_of(c * tc, tc)
            xp_ref[pl.ds(base + SEQ_PAD, tc), :] = x_ref[pl.ds(base, tc), :]
            dp_ref[pl.ds(base, tc), :] = dc_ref[pl.ds(base, tc), :]
            return 0

        lax.fori_loop(0, T // tc, fill, 0)

        def chunk(c, _):
            base = pl.multiple_of(c * tc, tc)
            d = dc_ref[pl.ds(base, tc), :]
            acc = jnp.zeros((tc, cb), F32)
            for k in range(W):
                acc = acc + w_ref[pl.ds(k, 1), :] * dp_ref[pl.ds(base + (W - 1) - k, tc), :]
                prod = d * xp_ref[pl.ds(base + SEQ_PAD - (W - 1) + k, tc), :]
                acc_ref[pl.ds(8 * k, 8), :] += jnp.sum(prod.reshape(tc // 8, 8, cb), axis=0)
            dx_ref[pl.ds(base, tc), :] = acc
            return 0

        lax.fori_loop(0, T // tc, chunk, 0)
        for k in range(W):
            dw_ref[pl.ds(k, 1), :] = jnp.sum(acc_ref[pl.ds(8 * k, 8), :], axis=0, keepdims=True)

    return _pcall(body, name=name, grid=(C // cb,),
                  in_specs=[pl.BlockSpec((T, cb), lambda i: (0, i)), pl.BlockSpec((T, cb), lambda i: (0, i)),
                            pl.BlockSpec((W, cb), lambda i: (0, i))],
                  out_specs=[pl.BlockSpec((T, cb), lambda i: (0, i)), pl.BlockSpec((W, cb), lambda i: (0, i))],
                  out_shape=[jax.ShapeDtypeStruct((T, C), F32), jax.ShapeDtypeStruct((W, C), F32)],
                  scratch_shapes=[pltpu.VMEM((T + SEQ_PAD, cb), F32), pltpu.VMEM((T + SEQ_PAD, cb), F32),
                                  pltpu.VMEM((W * 8, cb), F32)],
                  compiler_params=_params(("parallel",)))(dc, x, w)


def _pool_count(base, tc, cg, win):
    t = (lax.broadcasted_iota(jnp.int32, (tc, cg), 0) + base + 1).astype(F32)
    return jnp.minimum(t, float(win))


def _pool_fwd(name, u, T):
    D = u.shape[1]
    G = len(POOL_WINDOWS)
    cg = min(D // G, LANES)
    nb = (D // G) // cg
    tc = _tile(T, 128)

    def body(u_ref, o_ref, up_ref):
        g = pl.program_id(0)
        up_ref[pl.ds(0, SEQ_PAD), :] = jnp.zeros((SEQ_PAD, cg), F32)

        def fill(c, _):
            base = pl.multiple_of(c * tc, tc)
            up_ref[pl.ds(base + SEQ_PAD, tc), :] = u_ref[pl.ds(base, tc), :]
            return 0

        lax.fori_loop(0, T // tc, fill, 0)
        for gi, win in enumerate(POOL_WINDOWS):
            @pl.when(g == gi)
            def _(win=win):
                def chunk(c, _):
                    base = pl.multiple_of(c * tc, tc)
                    acc = up_ref[pl.ds(base + SEQ_PAD, tc), :]
                    for j in range(1, win):
                        acc = acc + up_ref[pl.ds(base + SEQ_PAD - j, tc), :]
                    mixed = acc / _pool_count(base, tc, cg, win) - u_ref[pl.ds(base, tc), :]
                    o_ref[pl.ds(base, tc), :] = mixed.astype(BF16)
                    return 0

                lax.fori_loop(0, T // tc, chunk, 0)

    return _pcall(body, name=name, grid=(G, nb), in_specs=[pl.BlockSpec((T, cg), lambda g, i: (0, g * nb + i))],
                  out_specs=pl.BlockSpec((T, cg), lambda g, i: (0, g * nb + i)), out_shape=jax.ShapeDtypeStruct((T, D), BF16),
                  scratch_shapes=[pltpu.VMEM((T + SEQ_PAD, cg), F32)], compiler_params=_params(("parallel", "parallel")))(u)


def _pool_bwd(name, dm, T):
    D = dm.shape[1]
    G = len(POOL_WINDOWS)
    cg = min(D // G, LANES)
    nb = (D // G) // cg
    tc = _tile(T, 128)

    def body(d_ref, o_ref, qp_ref):
        g = pl.program_id(0)
        qp_ref[pl.ds(T, SEQ_PAD), :] = jnp.zeros((SEQ_PAD, cg), F32)
        for gi, win in enumerate(POOL_WINDOWS):
            @pl.when(g == gi)
            def _(win=win):
                def fill(c, _):
                    base = pl.multiple_of(c * tc, tc)
                    qp_ref[pl.ds(base, tc), :] = d_ref[pl.ds(base, tc), :] / _pool_count(base, tc, cg, win)
                    return 0

                lax.fori_loop(0, T // tc, fill, 0)

                def chunk(c, _):
                    base = pl.multiple_of(c * tc, tc)
                    acc = qp_ref[pl.ds(base, tc), :]
                    for j in range(1, win):
                        acc = acc + qp_ref[pl.ds(base + j, tc), :]
                    o_ref[pl.ds(base, tc), :] = acc - d_ref[pl.ds(base, tc), :]
                    return 0

                lax.fori_loop(0, T // tc, chunk, 0)

    return _pcall(body, name=name, grid=(G, nb), in_specs=[pl.BlockSpec((T, cg), lambda g, i: (0, g * nb + i))],
                  out_specs=pl.BlockSpec((T, cg), lambda g, i: (0, g * nb + i)), out_shape=jax.ShapeDtypeStruct((T, D), F32),
                  scratch_shapes=[pltpu.VMEM((T + SEQ_PAD, cg), F32)], compiler_params=_params(("parallel", "parallel")))(dm)


def _pool_mm(name, mode, a, b, T, scale=None):
    G = len(POOL_WINDOWS)
    D = a.shape[1]
    cg = D // G
    tm = _tile(T, 2048)
    if mode == "tn":
        ins = [(a, _spec((tm, cg), lambda i, j, k: (k, j))), (b, _spec((tm, cg), lambda i, j, k: (k, j)))]
        outs = [((G, cg, cg), F32, _spec((None, cg, cg), lambda i, j, k: (j, 0, 0)))]
        return _mm(name, "tn", (1, G, T // tm), ins, outs, lambda acc: (acc,))[0]
    ins = [(a, _spec((tm, cg), lambda i, j, k: (i, j))), (b, _spec((None, cg, cg), lambda i, j, k: (j, 0, 0)))]
    if scale is not None:
        ins.append((scale, _spec((1, cg), lambda i, j, k: (0, j))))
        outs = [((T, D), F32, _spec((tm, cg), lambda i, j, k: (i, j))), ((T, D), BF16, _spec((tm, cg), lambda i, j, k: (i, j)))]
        return _mm(name, mode, (T // tm, G, 1), ins, outs, lambda acc, s: (acc * s, acc))
    outs = [((T, D), F32, _spec((tm, cg), lambda i, j, k: (i, j)))]
    return _mm(name, mode, (T // tm, G, 1), ins, outs)[0]


def _sc_mid_fwd(name, z, w, T):
    D = w.shape[1]
    W = w.shape[0]
    cb = min(D, LANES)
    nb = D // cb
    tc = _tile(T, 256)

    def body(b_ref, c_ref, v_ref, w_ref, o_ref, pp_ref):
        pp_ref[pl.ds(0, SEQ_PAD), :] = jnp.zeros((SEQ_PAD, cb), F32)

        def fill(c, _):
            base = pl.multiple_of(c * tc, tc)
            pp_ref[pl.ds(base + SEQ_PAD, tc), :] = c_ref[pl.ds(base, tc), :] * v_ref[pl.ds(base, tc), :]
            return 0

        lax.fori_loop(0, T // tc, fill, 0)

        def chunk(c, _):
            base = pl.multiple_of(c * tc, tc)
            q = jnp.zeros((tc, cb), F32)
            for k in range(W):
                q = q + w_ref[pl.ds(k, 1), :] * pp_ref[pl.ds(base + SEQ_PAD - (W - 1) + k, tc), :]
            o_ref[pl.ds(base, tc), :] = (b_ref[pl.ds(base, tc), :] * q).astype(BF16)
            return 0

        lax.fori_loop(0, T // tc, chunk, 0)

    return _pcall(body, name=name, grid=(nb,),
                  in_specs=[pl.BlockSpec((T, cb), lambda i: (0, i)), pl.BlockSpec((T, cb), lambda i: (0, nb + i)),
                            pl.BlockSpec((T, cb), lambda i: (0, 2 * nb + i)), pl.BlockSpec((W, cb), lambda i: (0, i))],
                  out_specs=pl.BlockSpec((T, cb), lambda i: (0, i)), out_shape=jax.ShapeDtypeStruct((T, D), BF16),
                  scratch_shapes=[pltpu.VMEM((T + SEQ_PAD, cb), F32)],
                  compiler_params=_params(("parallel",)))(z, z, z, w)


def _sc_mid_bwd(name, z, dm, w, T):
    D = w.shape[1]
    W = w.shape[0]
    cb = min(D, LANES)
    nb = D // cb
    tc = _tile(T, 128)

    def body(b_ref, c_ref, v_ref, d_ref, w_ref, dz_ref, dw_ref, pp_ref, dq_ref, acc_ref):
        part = pl.program_id(0)
        pp_ref[pl.ds(0, SEQ_PAD), :] = jnp.zeros((SEQ_PAD, cb), F32)
        dq_ref[pl.ds(T, SEQ_PAD), :] = jnp.zeros((SEQ_PAD, cb), F32)
        acc_ref[...] = jnp.zeros((W * 8, cb), F32)

        def fill(c, _):
            base = pl.multiple_of(c * tc, tc)
            pp_ref[pl.ds(base + SEQ_PAD, tc), :] = c_ref[pl.ds(base, tc), :] * v_ref[pl.ds(base, tc), :]
            dq_ref[pl.ds(base, tc), :] = d_ref[pl.ds(base, tc), :] * b_ref[pl.ds(base, tc), :]
            return 0

        lax.fori_loop(0, T // tc, fill, 0)

        @pl.when(part == 0)
        def _():
            def chunk(c, _):
                base = pl.multiple_of(c * tc, tc)
                dq = dq_ref[pl.ds(base, tc), :]
                q = jnp.zeros((tc, cb), F32)
                for k in range(W):
                    p = pp_ref[pl.ds(base + SEQ_PAD - (W - 1) + k, tc), :]
                    q = q + w_ref[pl.ds(k, 1), :] * p
                    acc_ref[pl.ds(8 * k, 8), :] += jnp.sum((dq * p).reshape(tc // 8, 8, cb), axis=0)
                dz_ref[pl.ds(base, tc), :] = (d_ref[pl.ds(base, tc), :] * q).astype(BF16)
                return 0

            lax.fori_loop(0, T // tc, chunk, 0)

        @pl.when(part > 0)
        def _():
            def chunk(c, _):
                base = pl.multiple_of(c * tc, tc)
                dp = jnp.zeros((tc, cb), F32)
                for k in range(W):
                    dp = dp + w_ref[pl.ds(k, 1), :] * dq_ref[pl.ds(base + (W - 1) - k, tc), :]
                other = jnp.where(part == 1, v_ref[pl.ds(base, tc), :], c_ref[pl.ds(base, tc), :])
                dz_ref[pl.ds(base, tc), :] = (dp * other).astype(BF16)
                return 0

            lax.fori_loop(0, T // tc, chunk, 0)

        for k in range(W):
            dw_ref[pl.ds(k, 1), :] = jnp.sum(acc_ref[pl.ds(8 * k, 8), :], axis=0, keepdims=True)

    dz, dw3 = _pcall(body, name=name, grid=(3, nb),
                     in_specs=[pl.BlockSpec((T, cb), lambda p, i: (0, i)), pl.BlockSpec((T, cb), lambda p, i: (0, nb + i)),
                               pl.BlockSpec((T, cb), lambda p, i: (0, 2 * nb + i)), pl.BlockSpec((T, cb), lambda p, i: (0, i)),
                               pl.BlockSpec((W, cb), lambda p, i: (0, i))],
                     out_specs=[pl.BlockSpec((T, cb), lambda p, i: (0, p * nb + i)),
                                pl.BlockSpec((None, W, cb), lambda p, i: (p, 0, i))],
                     out_shape=[jax.ShapeDtypeStruct((T, 3 * D), BF16), jax.ShapeDtypeStruct((3, W, D), F32)],
                     scratch_shapes=[pltpu.VMEM((T + SEQ_PAD, cb), F32), pltpu.VMEM((T + SEQ_PAD, cb), F32),
                                     pltpu.VMEM((W * 8, cb), F32)],
                     compiler_params=_params(("parallel", "parallel")))(z, z, z, dm, w)
    return dz, dw3


def _ret_tables(dk, dv):
    C = RET_CHUNK
    lg = np.log1p(-np.exp2(-5.0 - np.arange(RET_HEADS, dtype=np.float64)))
    idx = np.arange(C, dtype=np.float64)
    rel = idx[:, None] - idx[None, :]
    mask = np.where(rel >= 0, np.exp(lg[:, None, None] * np.maximum(rel, 0.0)), 0.0)
    qd = np.exp(lg[:, None] * (idx + 1.0))[:, :, None] * np.ones((1, 1, dk))
    kd = np.exp(lg[:, None] * (C - 1.0 - idx))[:, :, None] * np.ones((1, 1, dk))
    cd = np.exp(lg * C)[:, None, None] * np.ones((1, 1, dv))
    return tuple(jnp.asarray(a, F32) for a in (mask, qd, kd, cd))


def _ret_fwd(name, q, k, v, tabs, T):
    H, C = RET_HEADS, RET_CHUNK
    dk, dv = q.shape[1] // H, v.shape[1] // H
    nch = T // C
    mask, qd, kd, cd = tabs

    def body(q_ref, k_ref, v_ref, m_ref, qd_ref, kd_ref, cd_ref, o_ref, st_ref, s_ref):
        n = pl.program_id(1)

        @pl.when(n == 0)
        def _():
            s_ref[...] = jnp.zeros((dk, dv), F32)

        qv, kv, vv = q_ref[...], k_ref[...], v_ref[...]
        s = s_ref[...]
        sb = s.astype(BF16)
        st_ref[...] = sb
        scores = lax.dot_general(qv, kv, NT, preferred_element_type=F32) * m_ref[...]
        intra = lax.dot_general(scores.astype(BF16), vv, NN, preferred_element_type=F32)
        cross = lax.dot_general((qv.astype(F32) * qd_ref[...]).astype(BF16), sb, NN, preferred_element_type=F32)
        o_ref[...] = intra + cross
        upd = lax.dot_general((kv.astype(F32) * kd_ref[...]).astype(BF16), vv, TN, preferred_element_type=F32)
        s_ref[...] = s * cd_ref[...] + upd

    return _pcall(body, name=name, grid=(H, nch),
                  in_specs=[pl.BlockSpec((C, dk), lambda h, n: (n, h)), pl.BlockSpec((C, dk), lambda h, n: (n, h)),
                            pl.BlockSpec((C, dv), lambda h, n: (n, h)), pl.BlockSpec((None, C, C), lambda h, n: (h, 0, 0)),
                            pl.BlockSpec((None, C, dk), lambda h, n: (h, 0, 0)), pl.BlockSpec((None, C, dk), lambda h, n: (h, 0, 0)),
                            pl.BlockSpec((None, 1, dv), lambda h, n: (h, 0, 0))],
                  out_specs=[pl.BlockSpec((C, dv), lambda h, n: (n, h)),
                             pl.BlockSpec((None, None, dk, dv), lambda h, n: (h, n, 0, 0))],
                  out_shape=[jax.ShapeDtypeStruct((T, H * dv), F32), jax.ShapeDtypeStruct((H, nch, dk, dv), BF16)],
                  scratch_shapes=[pltpu.VMEM((dk, dv), F32)],
                  compiler_params=_params(("parallel", "arbitrary")))(q, k, v, mask, qd, kd, cd)


def _ret_bwd(name, q, k, v, do, st, tabs, T):
    H, C = RET_HEADS, RET_CHUNK
    dk, dv = q.shape[1] // H, v.shape[1] // H
    nch = T // C
    mask, qd, kd, cd = tabs

    def body(q_ref, k_ref, v_ref, do_ref, st_ref, m_ref, qd_ref, kd_ref, cd_ref, dq_ref, dk_ref, dv_ref, ds_ref):
        n = pl.program_id(1)

        @pl.when(n == 0)
        def _():
            ds_ref[...] = jnp.zeros((dk, dv), F32)

        qv, kv, vv, dov, sb = q_ref[...], k_ref[...], v_ref[...], do_ref[...], st_ref[...]
        m = m_ref[...]
        ds = ds_ref[...]
        dsb = ds.astype(BF16)
        a = (lax.dot_general(qv, kv, NT, preferred_element_type=F32) * m).astype(BF16)
        dp = (lax.dot_general(dov, vv, NT, preferred_element_type=F32) * m).astype(BF16)
        qa = (qv.astype(F32) * qd_ref[...]).astype(BF16)
        kb = (kv.astype(F32) * kd_ref[...]).astype(BF16)
        dq_ref[...] = (lax.dot_general(dp, kv, NN, preferred_element_type=F32)
                       + qd_ref[...] * lax.dot_general(dov, sb, NT, preferred_element_type=F32))
        dk_ref[...] = (lax.dot_general(dp, qv, TN, preferred_element_type=F32)
                       + kd_ref[...] * lax.dot_general(vv, dsb, NT, preferred_element_type=F32))
        dv_ref[...] = (lax.dot_general(a, dov, TN, preferred_element_type=F32)
                       + lax.dot_general(kb, dsb, NN, preferred_element_type=F32)).astype(BF16)
        ds_ref[...] = ds * cd_ref[...] + lax.dot_general(qa, dov, TN, preferred_element_type=F32)

    rev = lambda h, n: (nch - 1 - n, h)
    return _pcall(body, name=name, grid=(H, nch),
                  in_specs=[pl.BlockSpec((C, dk), rev), pl.BlockSpec((C, dk), rev), pl.BlockSpec((C, dv), rev),
                            pl.BlockSpec((C, dv), rev), pl.BlockSpec((None, None, dk, dv), lambda h, n: (h, nch - 1 - n, 0, 0)),
                            pl.BlockSpec((None, C, C), lambda h, n: (h, 0, 0)), pl.BlockSpec((None, C, dk), lambda h, n: (h, 0, 0)),
                            pl.BlockSpec((None, C, dk), lambda h, n: (h, 0, 0)), pl.BlockSpec((None, 1, dv), lambda h, n: (h, 0, 0))],
                  out_specs=[pl.BlockSpec((C, dk), rev), pl.BlockSpec((C, dk), rev), pl.BlockSpec((C, dv), rev)],
                  out_shape=[jax.ShapeDtypeStruct((T, H * dk), F32), jax.ShapeDtypeStruct((T, H * dk), F32),
                             jax.ShapeDtypeStruct((T, H * dv), BF16)],
                  scratch_shapes=[pltpu.VMEM((dk, dv), F32)],
                  compiler_params=_params(("parallel", "arbitrary")))(q, k, v, do, st, mask, qd, kd, cd)


def _local_step(x, pos, tgt, ng, get_w, put_g, first_deps, T, D):
    tr = _tile(T, 256)
    trw = _tile(T, 128)
    row = lambda i: ng[i:i + 1]

    u = _rowwise("rms_first", lambda xv, gv: _rms(xv, gv), [(x, D, 0), (row(0), None, None)], [("row", D, BF16)], T, tr,
                 deps=first_deps)[0]
    h = x
    saved = []
    cos = sin = None
    tabs = None
    for l in range(DEPTH):
        mixer = l % 4
        wts = get_w(l, h)
        sv = {"h_in": h, "u": u, "w": wts}
        if mixer == 0:
            z = _mm_cols("conv_in", u, wts["conv_w_in"], (), T, lambda acc, b: (acc + b,), (wts["conv_b_in"],))[0]
            v = _rowwise("conv_glu", _glu, [(z, D, 0), (z, D, 1)], [("row", D, F32)], T, tr)[0]
            c = _dwconv_fwd("conv_dw", v, wts["conv_dw"], wts["conv_dw_b"], T)
            s = _rowwise("conv_ln", _ln_silu, [(c, D, 0), (wts["conv_ln_g"], None, None), (wts["conv_ln_b"], None, None)],
                         [("row", D, BF16)], T, tr)[0]
            y = _mm_rows("conv_out", s, wts["conv_w_out"], T, lambda acc, b: (acc + b,), (wts["conv_b_out"],))
            sv.update(z=z, v=v, c=c, s=s)
        elif mixer == 1:
            mixed = _pool_fwd("pool_fwd", u, T)
            y, ycat = _pool_mm("pool_mm", "nn", mixed, wts["pool_w"], T, wts["pool_scale"])
            sv.update(mixed=mixed, ycat=ycat)
        elif mixer == 2:
            z = _mm_cols("sc_in", u, wts["sc_w_in"], (), T)[0]
            m = _sc_mid_fwd("sc_mid", z, wts["sc_dw"], T)
            y = _mm_rows("sc_out", m, wts["sc_w_out"], T)
            sv.update(z=z, m=m)
        else:
            dk = D // RET_HEADS
            half = dk // 2
            inv_freq = jnp.asarray((ROPE_BASE ** (-np.arange(half, dtype=np.float64) / half)).astype(np.float32)).reshape(1, half)

            def _cs(p, f):
                ang = p * f
                return jnp.cos(ang), jnp.sin(ang)

            cos, sin = _rowwise("rope_tab", _cs, [(pos, 1, 0), (inv_freq, None, None)],
                                [("row", half, F32), ("row", half, F32)], T, tr)
            tabs = _ret_tables(dk, 2 * dk)
            z = _mm_cols("ret_in", u, wts["ret_w_in"], (), T)[0]

            def _pre(zq, zk, zv, cv, sn):
                return _rot(zq, cv, sn, 1.0), _rot(zk, cv, sn, 1.0) * (dk ** -0.5), zv

            qr, kr, vb = _rowwise("ret_pre", _pre, [(z, D, 0), (z, D, 1), (z, 2 * D, 1), (cos, half, 0), (sin, half, 0)],
                                  [("row", D, BF16), ("row", D, BF16), ("row", 2 * D, BF16)], T, trw)
            o, st = _ret_fwd("ret_fwd", qr, kr, vb, tabs, T)
            out = _rowwise("ret_post", _gate_norm, [(o, 2 * D, 0), (z, 2 * D, 2)], [("row", 2 * D, BF16)], T, trw)[0]
            y = _mm_rows("ret_out", out, wts["ret_w_out"], T)
            sv.update(z=z, qr=qr, kr=kr, vb=vb, o=o, st=st, out=out)
        sv["y1"] = y
        h, u = _rowwise("bound_a", lambda hv, yv, gp, gn: (lambda hn: (hn, _rms(hn, gn)))(hv + _rms(yv, gp)),
                        [(h, D, 0), (y, D, 0), (row(4 * l + 1), None, None), (row(4 * l + 2), None, None)],
                        [("row", D, F32), ("row", D, BF16)], T, tr)
        sv["h_mid"], sv["u2"] = h, u
        r = _mm_cols("mlp_up", u, wts["mlp_up"], (), T, lambda acc: (jnp.maximum(acc, 0.0),), (), 1, BF16)[0]
        tm = _tile(T, 2048)
        y = _mm("mlp_down", "nn", (T // tm, 1, N_DEV),
                [(r, _spec((tm, r.shape[1] // N_DEV), lambda i, j, k: (i, k))),
                 (wts["mlp_down"], _spec((None, r.shape[1] // N_DEV, D), lambda i, j, k: (k, 0, 0)))],
                [((T, D), F32, _spec((tm, D), lambda i, j, k: (i, 0)))], pre=_square)[0]
        sv.update(r=r, y2=y)
        if l + 1 < DEPTH:
            udt = F32 if (l + 1) % 4 == 1 else BF16
            h, u = _rowwise("bound_b", lambda hv, yv, gp, gn: (lambda hn: (hn, _rms(hn, gn)))(hv + _rms(yv, gp)),
                            [(h, D, 0), (y, D, 0), (row(4 * l + 3), None, None), (row(4 * l + 4), None, None)],
                            [("row", D, F32), ("row", D, udt)], T, tr)
        saved.append(sv)

    def _final(hv, yv, gp, tv):
        hn = hv + _rms(yv, gp)
        e = hn - tv
        return e * (1.0 / D), _colsum(e * e)

    dh, loss_row = _rowwise("final", _final, [(h, D, 0), (y, D, 0), (row(4 * DEPTH - 1), None, None), (tgt, D, 0)],
                            [("row", D, F32), ("acc", (1, D))], T, tr)

    gn = [None] * (4 * DEPTH)
    F = saved[0]["r"].shape[1]
    du_next = None
    tok = None
    for l in reversed(range(DEPTH)):
        sv = saved[l]
        wts = sv["w"]
        g = {}
        deps = [] if tok is None else [tok]
        mixer = l % 4
        if du_next is None:
            def _bb(dhv, yv, gp):
                _, vjp = jax.vjp(_rms, yv, gp)
                dy, dgp = vjp(dhv)
                return dy, dgp

            dy, gn[4 * l + 3] = _rowwise("bwd_last", _bb, [(dh, D, 0), (sv["y2"], D, 0), (row(4 * l + 3), None, None)],
                                         [("row", D, BF16), ("acc", (1, D))], T, tr, deps)
        else:
            dh, dy, gn[4 * l + 3], gn[4 * l + 4], _ = _bound_bwd("bwd_b", dh, du_next, saved[l + 1]["h_in"], sv["y2"],
                                                              row(4 * l + 3), row(4 * l + 4), T, D, tr, deps)
        F8 = F // N_DEV
        tm = _tile(T, 2048)
        da = _mm("mlp_dhid", "nt", (T // tm, N_DEV, 1),
                 [(dy, _spec((tm, D), lambda i, j, k: (i, 0))),
                  (wts["mlp_down"], _spec((None, F8, D), lambda i, j, k: (j, 0, 0))),
                  (sv["r"], _spec((tm, F8), lambda i, j, k: (i, j)))],
                 [((T, F), BF16, _spec((tm, F8), lambda i, j, k: (i, j)))],
                 lambda acc, rr: (acc * (2.0 * rr.astype(F32)),))[0]
        tk = _tile(T, 2048)
        g["mlp_down"] = _mm("mlp_dwdown", "tn", (1, N_DEV, T // tk),
                            [(sv["r"], _spec((tk, F8), lambda i, j, k: (k, j))), (dy, _spec((tk, D), lambda i, j, k: (k, 0)))],
                            [(wts["mlp_down"].shape, BF16, _spec((None, F8, D), lambda i, j, k: (j, 0, 0)))],
                            lambda acc: (acc,), pre=_square)[0]
        g["mlp_up"] = _mm_cols_grad("mlp_dwup", sv["u2"], da, wts["mlp_up"].shape, (), T)
        du = _mm_cols_t("mlp_du", da, wts["mlp_up"], (), T)
        dh, dy, gn[4 * l + 1], gn[4 * l + 2], dy_sum = _bound_bwd("bwd_a", dh, du, sv["h_mid"], sv["y1"],
                                                                row(4 * l + 1), row(4 * l + 2), T, D, tr)
        u = sv["u"]
        if mixer == 0:
            g["conv_b_out"] = dy_sum
            ds = _mm_rows_t("conv_ds", dy, wts["conv_w_out"], T)
            g["conv_w_out"] = _mm_rows_grad("conv_dwout", sv["s"], dy, T)

            def _lnb(cv, gv, bv, dsv):
                _, vjp = jax.vjp(_ln_silu, cv, gv, bv)
                dc, dg, db = vjp(dsv)
                return dc, dg, db, _colsum(dc)

            dc, g["conv_ln_g"], g["conv_ln_b"], g["conv_dw_b"] = _rowwise(
                "conv_ln_bwd", _lnb, [(sv["c"], D, 0), (wts["conv_ln_g"], None, None), (wts["conv_ln_b"], None, None), (ds, D, 0)],
                [("row", D, F32), ("acc", (1, D)), ("acc", (1, D)), ("acc", (1, D))], T, tr)
            dv, g["conv_dw"] = _dwconv_bwd("conv_dw_bwd", dc, sv["v"], wts["conv_dw"], T)

            def _glub(za, zg, dvv):
                _, vjp = jax.vjp(_glu, za, zg)
                dza, dzg = vjp(dvv)
                dz = jnp.concatenate([dza, dzg], axis=-1)
                return dz, _colsum(dz)

            dz, g["conv_b_in"] = _rowwise("conv_glu_bwd", _glub, [(sv["z"], D, 0), (sv["z"], D, 1), (dv, D, 0)],
                                          [("row", 2 * D, BF16), ("acc", (1, 2 * D))], T, tr)
            g["conv_w_in"] = _mm_cols_grad("conv_dwin", u, dz, wts["conv_w_in"].shape, (), T)
            du = _mm_cols_t("conv_du", dz, wts["conv_w_in"], (), T)
        elif mixer == 1:
            def _psb(dyv, yc, sc):
                return dyv * sc, _colsum(dyv.astype(F32) * yc.astype(F32))

            dyg, g["pool_scale"] = _rowwise("pool_scale_bwd", _psb, [(dy, D, 0), (sv["ycat"], D, 0), (wts["pool_scale"], None, None)],
                                            [("row", D, BF16), ("acc", (1, D))], T, tr)
            dmix = _pool_mm("pool_dmix", "nt", dyg, wts["pool_w"], T)
            g["pool_w"] = _pool_mm("pool_dw", "tn", sv["mixed"], dyg, T)
            du = _pool_bwd("pool_bwd", dmix, T)
        elif mixer == 2:
            dm = _mm_rows_t("sc_dm", dy, wts["sc_w_out"], T)
            g["sc_w_out"] = _mm_rows_grad("sc_dwout", sv["m"], dy, T)
            dz, dw3 = _sc_mid_bwd("sc_mid_bwd", sv["z"], dm, wts["sc_dw"], T)
            g["sc_dw"] = dw3[0]
            g["sc_w_in"] = _mm_cols_grad("sc_dwin", u, dz, wts["sc_w_in"].shape, (), T)
            du = _mm_cols_t("sc_du", dz, wts["sc_w_in"], (), T)
        else:
            dk = D // RET_HEADS
            half = dk // 2
            dout = _mm_rows_t("ret_dout", dy, wts["ret_w_out"], T)
            g["ret_w_out"] = _mm_rows_grad("ret_dwout", sv["out"], dy, T)

            def _postb(ov, gv, dv_):
                _, vjp = jax.vjp(_gate_norm, ov, gv)
                return vjp(dv_)

            do, dzg = _rowwise("ret_post_bwd", _postb, [(sv["o"], 2 * D, 0), (sv["z"], 2 * D, 2), (dout, 2 * D, 0)],
                               [("row", 2 * D, BF16), ("row", 2 * D, BF16)], T, trw)
            dqr, dkr, dvb = _ret_bwd("ret_bwd", sv["qr"], sv["kr"], sv["vb"], do, sv["st"], tabs, T)

            def _preb(dq, dk_, dvv, dgv, cv, sn):
                return jnp.concatenate([_rot(dq, cv, sn, -1.0).astype(BF16), (_rot(dk_, cv, sn, -1.0) * (dk ** -0.5)).astype(BF16),
                                        dvv, dgv], axis=-1)

            dz = _rowwise("ret_pre_bwd", _preb, [(dqr, D, 0), (dkr, D, 0), (dvb, 2 * D, 0), (dzg, 2 * D, 0), (cos, half, 0), (sin, half, 0)],
                          [("row", 6 * D, BF16)], T, trw)[0]
            g["ret_w_in"] = _mm_cols_grad("ret_dwin", u, dz, wts["ret_w_in"].shape, (), T)
            du = _mm_cols_t("ret_du", dz, wts["ret_w_in"], (), T)
        du_next = du
        if l > 0:
            tok = put_g(l, g, du, None)

    def _fb(dhv, duv, xv, gv):
        _, vjp = jax.vjp(_rms, xv, gv)
        dx, dg = vjp(duv)
        return dhv + dx, dg

    grad_x, gn[0] = _rowwise("bwd_first", _fb, [(dh, D, 0), (du_next, D, 0), (x, D, 0), (row(0), None, None)],
                             [("row", D, F32), ("acc", (1, D))], T, tr)
    put_g(0, g, grad_x, jnp.concatenate(gn, axis=0))
    return loss_row, grad_x


def _bound_bwd(name, dh, du, h_new, y, g_post, g_next, T, D, tr, deps=()):
    def fn(dhv, duv, hn, yv, gp, gnx):
        _, vjp1 = jax.vjp(_rms, hn, gnx)
        d1, dgn = vjp1(duv.astype(F32))
        dht = dhv + d1
        _, vjp2 = jax.vjp(_rms, yv, gp)
        dy, dgp = vjp2(dht)
        return dht, dy, dgp, dgn, _colsum(dy)

    return _rowwise(name, fn, [(dh, D, 0), (du, D, 0), (h_new, D, 0), (y, D, 0), (g_post, None, None), (g_next, None, None)],
                    [("row", D, F32), ("row", D, BF16), ("acc", (1, D)), ("acc", (1, D)), ("acc", (1, D))], T, tr, deps)


def _adamw(name, parts, w, m, v, own=None):
    P, R, C = parts.shape
    tr = _tile(R, 256)

    def body(*refs):
        p_ref = refs[0]
        w_ref, m_ref, v_ref, g_ref, d_ref, m2_ref, v2_ref = refs[-7:]

        def slot(i):
            if own is None:
                return p_ref[i].astype(F32)
            here = 2 * lax.axis_index("x") + lax.axis_index("y") == i
            return jnp.where(here, refs[1][i], p_ref[i]).astype(F32)

        g = slot(0)
        for i in range(1, P):
            g = g + slot(i)
        wv = w_ref[...]
        m2 = ADAM_B1 * m_ref[...] + (1.0 - ADAM_B1) * g
        v2 = ADAM_B2 * v_ref[...] + (1.0 - ADAM_B2) * (g * g)
        m_hat = m2 / (1.0 - ADAM_B1 ** ADAM_STEP)
        v_hat = v2 / (1.0 - ADAM_B2 ** ADAM_STEP)
        g_ref[...] = g
        d_ref[...] = -ADAM_LR * (m_hat / (jnp.sqrt(v_hat) + ADAM_EPS) + ADAM_WD * wv)
        m2_ref[...] = m2
        v2_ref[...] = v2

    blk = pl.BlockSpec((tr, C), lambda i: (i, 0))
    pblk = pl.BlockSpec((P, tr, C), lambda i: (0, i, 0))
    stacks = [parts] if own is None else [parts, own]
    return _pcall(body, name=name, grid=(R // tr,),
                  in_specs=[pblk] * len(stacks) + [blk, blk, blk], out_specs=[blk] * 4,
                  out_shape=[jax.ShapeDtypeStruct((R, C), F32)] * 4, compiler_params=_params(("parallel",)))(*stacks, w, m, v)


def _pair_add(name, a, b):
    Q, _, R, C = a.shape
    tr = _tile(R, 1024)
    blk = pl.BlockSpec((None, tr, C), lambda q, i: (q, i, 0))

    def body(a_ref, b_ref, o_ref):
        mine = jnp.where(lax.axis_index("c") == 0, a_ref[0], a_ref[1])
        o_ref[...] = (mine.astype(F32) + b_ref[...].astype(F32)).astype(BF16)

    return _pcall(body, name=name, grid=(Q, R // tr),
                  in_specs=[pl.BlockSpec((None, 2, tr, C), lambda q, i: (q, 0, i, 0)), blk], out_specs=blk,
                  out_shape=jax.ShapeDtypeStruct((Q, R, C), BF16), compiler_params=_params(("parallel", "parallel")))(a, b)


ANY = pl.BlockSpec(memory_space=pl.ANY)


def _place():
    x, y, c = lax.axis_index("x"), lax.axis_index("y"), lax.axis_index("c")
    chips = [(1 - x, y), (x, 1 - y), (1 - x, 1 - y)]
    return x, y, c, chips


def _gather_all(locs):
    n = len(locs)

    def body(*refs):
        ins, outs = refs[:n], refs[n:2 * n]
        send, recv, lsem = refs[2 * n:]
        x, y, c, chips = _place()
        me = 4 * x + 2 * y + c
        sib = (x, y, 1 - c)

        def cp(a, k, slot, to, src=None):
            return pltpu.make_async_remote_copy(src_ref=outs[a].at[slot] if src is None else src, dst_ref=outs[a].at[slot],
                                                send_sem=send.at[a, k], recv_sem=recv.at[a, k], device_id=to, device_id_type=MESH)

        mine = [pltpu.make_async_copy(ins[a], outs[a].at[me], lsem.at[a]) for a in range(n)]
        for d in mine:
            d.start()
        first = []
        for a in range(n):
            first.append(cp(a, 0, me, sib, src=ins[a]))
            for j, chip in enumerate(chips):
                first.append(cp(a, 1 + j, me, (chip[0], chip[1], c), src=ins[a]))
        for d in first:
            d.start()
        passed = []
        for j, chip in enumerate(chips):
            slot = 4 * chip[0] + 2 * chip[1] + c
            for a in range(n):
                cp(a, 1 + j, slot, sib).wait_recv()
                p = cp(a, 4 + j, slot, sib)
                p.start()
                passed.append(p)
        for a in range(n):
            cp(a, 0, 4 * x + 2 * y + (1 - c), sib).wait_recv()
        for j, chip in enumerate(chips):
            for a in range(n):
                cp(a, 4 + j, 4 * chip[0] + 2 * chip[1] + (1 - c), sib).wait_recv()
        for d in first + passed:
            d.wait_send()
        for d in mine:
            d.wait()

    return _pcall(body, name="gather_all", in_specs=[ANY] * n, out_specs=[ANY] * n,
                  out_shape=[jax.ShapeDtypeStruct((N_DEV,) + a.shape, a.dtype) for a in locs],
                  scratch_shapes=[pltpu.SemaphoreType.DMA((n, 7)), pltpu.SemaphoreType.DMA((n, 7)), pltpu.SemaphoreType.DMA((n,))],
                  compiler_params=pltpu.CompilerParams(has_side_effects=True))(*locs)


HBM = pl.BlockSpec(memory_space=pltpu.HBM)
SEM = pl.BlockSpec(memory_space=pltpu.SEMAPHORE)
DATAFLOW = pltpu.SideEffectType.DATAFLOW_SIDE_EFFECTING


def _peers():
    x, y, c = lax.axis_index("x"), lax.axis_index("y"), lax.axis_index("c")
    out = []
    for r in range(1, N_DEV):
        px = 1 - x if (r >> 2) & 1 else x
        py = 1 - y if (r >> 1) & 1 else y
        pc = 1 - c if r & 1 else c
        out.append(((px, py, pc), 4 * px + 2 * py + pc))
    return 4 * x + 2 * y + c, out


def _hbm(a):
    return pltpu.with_memory_space_constraint(a, pltpu.HBM)


def _gather_start(name, locs, lands, deps):
    n = len(locs)

    def body(*refs):
        ins, lnd = refs[:n], refs[n:2 * n]
        send, recv = refs[2 * n + len(deps)], refs[2 * n + len(deps) + 1]
        token = refs[-1]
        me, peers = _peers()
        for r, (to, _) in enumerate(peers):
            for a in range(n):
                pltpu.make_async_remote_copy(src_ref=ins[a], dst_ref=lnd[a].at[me], send_sem=send.at[a * 7 + r],
                                             recv_sem=recv.at[a * 7 + r], device_id=to, device_id_type=MESH).start()
        token[...] = jnp.zeros_like(token)

    out_shape = ((pltpu.SemaphoreType.DMA((n * 7,)), pltpu.SemaphoreType.DMA((n * 7,)))
                 + tuple(pltpu.HBM(a.shape, a.dtype) for a in locs) + tuple(pltpu.HBM(a.shape, a.dtype) for a in lands)
                 + (jax.ShapeDtypeStruct((8, LANES), F32),))
    res = _pcall(body, name=name, out_shape=out_shape, in_specs=[HBM] * (2 * n) + [ANY] * len(deps),
                 out_specs=(SEM, SEM) + (HBM,) * (2 * n) + (pl.BlockSpec(memory_space=pltpu.VMEM),),
                 input_output_aliases={a: 2 + a for a in range(2 * n)},
                 compiler_params=pltpu.CompilerParams(has_side_effects=DATAFLOW))(
        *[_hbm(a) for a in locs], *[_hbm(a) for a in lands], *deps)
    return res[0], res[1], res[2:2 + n], res[2 + n:2 + 2 * n], res[-1]


def _gather_wait(name, send, recv, shards, lands, after):
    n = len(shards)

    def body(*refs):
        ins, lnd = refs[:n], refs[n:2 * n]
        send, recv = refs[2 * n], refs[2 * n + 1]
        _, peers = _peers()
        for r, (to, idx) in enumerate(peers):
            for a in range(n):
                cp = pltpu.make_async_remote_copy(src_ref=ins[a], dst_ref=lnd[a].at[idx], send_sem=send.at[a * 7 + r],
                                                  recv_sem=recv.at[a * 7 + r], device_id=to, device_id_type=MESH)
                cp.wait_send()
                cp.wait_recv()

    out_shape = tuple(pltpu.HBM(a.shape, a.dtype) for a in shards) + tuple(pltpu.HBM(a.shape, a.dtype) for a in lands)
    res = _pcall(body, name=name, out_shape=out_shape, in_specs=[HBM] * (2 * n) + [SEM, SEM, ANY], out_specs=(HBM,) * (2 * n),
                 input_output_aliases={a: a for a in range(2 * n)},
                 compiler_params=pltpu.CompilerParams(has_side_effects=DATAFLOW))(*shards, *lands, send, recv, after)
    return res[n:]


def _chip_start(name, cs, lands):
    n = len(cs)

    def body(*refs):
        ins, lnd = refs[:n], refs[n:2 * n]
        send, recv = refs[2 * n], refs[2 * n + 1]
        token = refs[-1]
        x, y, c, chips = _place()
        qme = 2 * x + y
        for j, chip in enumerate(chips):
            for a in range(n):
                pltpu.make_async_remote_copy(src_ref=ins[a].at[2 * chip[0] + chip[1]], dst_ref=lnd[a].at[qme], send_sem=send.at[a * 3 + j],
                                             recv_sem=recv.at[a * 3 + j], device_id=(chip[0], chip[1], c), device_id_type=MESH).start()
        token[...] = jnp.zeros_like(token)

    out_shape = ((pltpu.SemaphoreType.DMA((n * 3,)), pltpu.SemaphoreType.DMA((n * 3,)))
                 + tuple(pltpu.HBM(a.shape, a.dtype) for a in cs) + tuple(pltpu.HBM(a.shape, a.dtype) for a in lands)
                 + (jax.ShapeDtypeStruct((8, LANES), F32),))
    res = _pcall(body, name=name, out_shape=out_shape, in_specs=[HBM] * (2 * n),
                 out_specs=(SEM, SEM) + (HBM,) * (2 * n) + (pl.BlockSpec(memory_space=pltpu.VMEM),),
                 input_output_aliases={a: 2 + a for a in range(2 * n)},
                 compiler_params=pltpu.CompilerParams(has_side_effects=DATAFLOW))(*[_hbm(a) for a in cs], *[_hbm(a) for a in lands])
    return res[0], res[1], res[2:2 + n], res[2 + n:2 + 2 * n], res[-1]


def _chip_wait(name, send, recv, cs, lands, after):
    n = len(cs)

    def body(*refs):
        ins, lnd = refs[:n], refs[n:2 * n]
        send, recv = refs[2 * n], refs[2 * n + 1]
        x, y, c, chips = _place()
        for j, chip in enumerate(chips):
            q = 2 * chip[0] + chip[1]
            for a in range(n):
                cp = pltpu.make_async_remote_copy(src_ref=ins[a].at[q], dst_ref=lnd[a].at[q], send_sem=send.at[a * 3 + j],
                                                  recv_sem=recv.at[a * 3 + j], device_id=(chip[0], chip[1], c), device_id_type=MESH)
                cp.wait_send()
                cp.wait_recv()

    out_shape = tuple(pltpu.HBM(a.shape, a.dtype) for a in cs) + tuple(pltpu.HBM(a.shape, a.dtype) for a in lands)
    res = _pcall(body, name=name, out_shape=out_shape, in_specs=[HBM] * (2 * n) + [SEM, SEM, ANY], out_specs=(HBM,) * (2 * n),
                 input_output_aliases={a: a for a in range(2 * n)},
                 compiler_params=pltpu.CompilerParams(has_side_effects=DATAFLOW))(*cs, *lands, send, recv, after)
    return res[:n], res[n:]


def _pair_exchange(name, gs):
    n = len(gs)

    def body(*refs):
        ins, got = refs[:n], refs[n:2 * n]
        send, recv = refs[2 * n:]
        x, y, c, _ = _place()
        sib = (x, y, 1 - c)
        rem = [pltpu.make_async_remote_copy(src_ref=ins[a].at[q, 1 - c], dst_ref=got[a].at[q], send_sem=send.at[a, q],
                                            recv_sem=recv.at[a, q], device_id=sib, device_id_type=MESH)
               for a in range(n) for q in range(N_CHIP)]
        for d in rem:
            d.start()
        for d in rem:
            d.wait()

    half = [jax.ShapeDtypeStruct((N_CHIP,) + g.shape[2:], g.dtype) for g in gs]
    return _pcall(body, name=name, in_specs=[ANY] * n, out_specs=[ANY] * n, out_shape=half,
                  scratch_shapes=[pltpu.SemaphoreType.DMA((n, N_CHIP)), pltpu.SemaphoreType.DMA((n, N_CHIP))],
                  compiler_params=pltpu.CompilerParams(has_side_effects=True))(*gs)


def _chip_exchange(cs, sm_sh, sm_rep):
    n = len(cs)

    def body(*refs):
        ins, sh, rep = refs[:n], refs[n], refs[n + 1]
        outs, osh, orep = refs[n + 2:2 * n + 2], refs[2 * n + 2], refs[2 * n + 3]
        send, recv, lsem, ssend, srecv, slsem = refs[2 * n + 4:]
        x, y, c, chips = _place()
        qme = 2 * x + y
        me = 4 * x + 2 * y + c
        loc = [pltpu.make_async_copy(ins[a].at[qme], outs[a].at[qme], lsem.at[a]) for a in range(n)]
        loc.append(pltpu.make_async_copy(sh.at[me], osh.at[me], slsem.at[0]))
        loc.append(pltpu.make_async_copy(rep, orep.at[me], slsem.at[1]))
        rem, waits = [], []
        for j, chip in enumerate(chips):
            q = 2 * chip[0] + chip[1]
            to = (chip[0], chip[1], c)
            for a in range(n):
                rem.append(pltpu.make_async_remote_copy(src_ref=ins[a].at[q], dst_ref=outs[a].at[qme], send_sem=send.at[a, j],
                                                        recv_sem=recv.at[a, j], device_id=to, device_id_type=MESH))
                waits.append(pltpu.make_async_remote_copy(src_ref=ins[a].at[q], dst_ref=outs[a].at[q], send_sem=send.at[a, j],
                                                          recv_sem=recv.at[a, j], device_id=to, device_id_type=MESH))
        for r in range(1, N_DEV):
            fx, fy, fc = (r >> 2) & 1, (r >> 1) & 1, r & 1
            px = 1 - x if fx else x
            py = 1 - y if fy else y
            pc = 1 - c if fc else c
            t = 4 * px + 2 * py + pc
            to = (px, py, pc)
            rem.append(pltpu.make_async_remote_copy(src_ref=sh.at[t], dst_ref=osh.at[me], send_sem=ssend.at[0, r - 1],
                                                    recv_sem=srecv.at[0, r - 1], device_id=to, device_id_type=MESH))
            waits.append(pltpu.make_async_remote_copy(src_ref=sh.at[t], dst_ref=osh.at[t], send_sem=ssend.at[0, r - 1],
                                                      recv_sem=srecv.at[0, r - 1], device_id=to, device_id_type=MESH))
            rem.append(pltpu.make_async_remote_copy(src_ref=rep, dst_ref=orep.at[me], send_sem=ssend.at[1, r - 1],
                                                    recv_sem=srecv.at[1, r - 1], device_id=to, device_id_type=MESH))
            waits.append(pltpu.make_async_remote_copy(src_ref=rep, dst_ref=orep.at[t], send_sem=ssend.at[1, r - 1],
                                                      recv_sem=srecv.at[1, r - 1], device_id=to, device_id_type=MESH))
        for d in loc + rem:
            d.start()
        for d in waits:
            d.wait_recv()
        for d in rem:
            d.wait_send()
        for d in loc:
            d.wait()

    out_shape = [jax.ShapeDtypeStruct(a.shape, a.dtype) for a in cs]
    out_shape += [jax.ShapeDtypeStruct(sm_sh.shape, F32), jax.ShapeDtypeStruct((N_DEV,) + sm_rep.shape, F32)]
    res = _pcall(body, name="chip_exchange", in_specs=[ANY] * (n + 2), out_specs=[ANY] * (n + 2), out_shape=out_shape,
                 scratch_shapes=[pltpu.SemaphoreType.DMA((n, 3)), pltpu.SemaphoreType.DMA((n, 3)), pltpu.SemaphoreType.DMA((n,)),
                                 pltpu.SemaphoreType.DMA((2, 7)), pltpu.SemaphoreType.DMA((2, 7)), pltpu.SemaphoreType.DMA((2,))],
                 compiler_params=pltpu.CompilerParams(has_side_effects=True))(*cs, sm_sh, sm_rep)
    return res[:n], res[n], res[n + 1]


BIG = ("mlp_up", "mlp_down", "conv_w_in", "conv_w_out", "pool_w", "sc_w_in", "sc_w_out", "ret_w_in", "ret_w_out")
SMALL_SPLIT = ("norm_g", "conv_dw", "sc_dw")
SMALL_REP = ("conv_b_in", "conv_dw_b", "conv_ln_g", "conv_ln_b", "conv_b_out", "pool_scale")
WEIGHTS = ("norm_g", "mlp_up", "mlp_down", "conv_w_in", "conv_b_in", "conv_dw", "conv_dw_b", "conv_ln_g", "conv_ln_b",
           "conv_w_out", "conv_b_out", "pool_w", "pool_scale", "sc_w_in", "sc_dw", "sc_w_out", "ret_w_in", "ret_w_out")


def _pad_rows(a, rows):
    return jnp.pad(a, ((0, rows - a.shape[0]), (0, 0)))


def _split_rows(names, shards):
    out = []
    for nme in names:
        r = int(np.prod(shards[nme].shape[:-1]))
        out.append((nme, r, -(-r // 8) * 8))
    return out


def _pack_split(names, arrs):
    parts = []
    for nme, r, rp in _split_rows(names, arrs):
        parts.append(_pad_rows(arrs[nme].reshape(r, arrs[nme].shape[-1]), rp))
    return jnp.concatenate(parts, axis=0)


def _pack_rep(arrs, D):
    parts = [arrs[nme].reshape(-1, D) for nme in SMALL_REP]
    rows = sum(p.shape[0] for p in parts)
    return _pad_rows(jnp.concatenate(parts, axis=0), -(-rows // 8) * 8)


def kernel(x, positions, norm_g, mlp_up, mlp_down, conv_w_in, conv_b_in, conv_dw, conv_dw_b, conv_ln_g, conv_ln_b, conv_w_out, conv_b_out, pool_w, pool_scale, sc_w_in, sc_dw, sc_w_out, ret_w_in, ret_w_out, loss_target, m_norm_g, m_mlp_up, m_mlp_down, m_conv_w_in, m_conv_b_in, m_conv_dw, m_conv_dw_b, m_conv_ln_g, m_conv_ln_b, m_conv_w_out, m_conv_b_out, m_pool_w, m_pool_scale, m_sc_w_in, m_sc_dw, m_sc_w_out, m_ret_w_in, m_ret_w_out, v_norm_g, v_mlp_up, v_mlp_down, v_conv_w_in, v_conv_b_in, v_conv_dw, v_conv_dw_b, v_conv_ln_g, v_conv_ln_b, v_conv_w_out, v_conv_b_out, v_pool_w, v_pool_scale, v_sc_w_in, v_sc_dw, v_sc_w_out, v_ret_w_in, v_ret_w_out):
    w = dict(norm_g=norm_g, mlp_up=mlp_up, mlp_down=mlp_down, conv_w_in=conv_w_in, conv_b_in=conv_b_in, conv_dw=conv_dw,
             conv_dw_b=conv_dw_b, conv_ln_g=conv_ln_g, conv_ln_b=conv_ln_b, conv_w_out=conv_w_out, conv_b_out=conv_b_out,
             pool_w=pool_w, pool_scale=pool_scale, sc_w_in=sc_w_in, sc_dw=sc_dw, sc_w_out=sc_w_out, ret_w_in=ret_w_in, ret_w_out=ret_w_out)
    m = dict(norm_g=m_norm_g, mlp_up=m_mlp_up, mlp_down=m_mlp_down, conv_w_in=m_conv_w_in, conv_b_in=m_conv_b_in, conv_dw=m_conv_dw,
             conv_dw_b=m_conv_dw_b, conv_ln_g=m_conv_ln_g, conv_ln_b=m_conv_ln_b, conv_w_out=m_conv_w_out, conv_b_out=m_conv_b_out,
             pool_w=m_pool_w, pool_scale=m_pool_scale, sc_w_in=m_sc_w_in, sc_dw=m_sc_dw, sc_w_out=m_sc_w_out, ret_w_in=m_ret_w_in,
             ret_w_out=m_ret_w_out)
    v = dict(norm_g=v_norm_g, mlp_up=v_mlp_up, mlp_down=v_mlp_down, conv_w_in=v_conv_w_in, conv_b_in=v_conv_b_in, conv_dw=v_conv_dw,
             conv_dw_b=v_conv_dw_b, conv_ln_g=v_conv_ln_g, conv_ln_b=v_conv_ln_b, conv_w_out=v_conv_w_out, conv_b_out=v_conv_b_out,
             pool_w=v_pool_w, pool_scale=v_pool_scale, sc_w_in=v_sc_w_in, sc_dw=v_sc_dw, sc_w_out=v_sc_w_out, ret_w_in=v_ret_w_in,
             ret_w_out=v_ret_w_out)
    T, D = x.shape[1], x.shape[2]
    L = D // N_DEV
    G = len(POOL_WINDOWS)
    cg = D // G

    me = 4 * lax.axis_index("x") + 2 * lax.axis_index("y") + lax.axis_index("c")
    MIX = (("conv_w_in", "conv_w_out"), ("pool_w",), ("sc_w_in", "sc_w_out"), ("ret_w_in", "ret_w_out"))

    def shards(l):
        out = {k: w[k].astype(BF16) for k in MIX[l]}
        out["mlp_up"], out["mlp_down"] = w["mlp_up"][l].astype(BF16), w["mlp_down"][l].astype(BF16)
        return out

    sh0 = shards(0)
    gath0 = _gather_all(list(sh0.values()) + [_pack_split(SMALL_SPLIT, w)])
    small = gath0[-1]
    common = {k: w[k].reshape(1, -1) for k in SMALL_REP}
    off = 0
    for nme, r, rp in _split_rows(SMALL_SPLIT, w):
        common[nme] = small[:, off:off + r, :].transpose(1, 0, 2).reshape(r, D)
        off += rp
    flying = {}
    dep = small
    for l in range(1, DEPTH):
        shl = shards(l)
        lands = [lax.dynamic_update_slice(jnp.zeros((N_DEV,) + a.shape, a.dtype), a[None], (me,) + (0,) * a.ndim) for a in shl.values()]
        send, recv, thru, lands, dep = _gather_start("gather_start_%d" % l, list(shl.values()), lands, [dep])
        flying[l] = (list(shl.keys()), send, recv, thru, lands)

    def get_w(l, after):
        if l == 0:
            gw = dict(zip(sh0.keys(), gath0[:-1]))
        else:
            names, send, recv, thru, lands = flying[l]
            gw = dict(zip(names, _gather_wait("gather_wait_%d" % l, send, recv, thru, lands, after)))
        out = dict(common)
        for k, a in gw.items():
            if k.endswith("_w_out"):
                out[k] = a.reshape(-1, D)
            elif k == "pool_w":
                out[k] = a.transpose(1, 0, 2, 3).reshape(G, cg, cg)
            else:
                out[k] = a
        return out

    res = {}
    mlp_res = {"mlp_up": [None] * DEPTH, "mlp_down": [None] * DEPTH}
    pending = []
    small_g = {}

    def shard2d(src, k, l):
        a = src[k][l] if k in mlp_res else src[k]
        return a.reshape(-1, a.shape[-1])

    def adam_layer(l, names, parts, own):
        for i, k in enumerate(names):
            out = _adamw("adamw_%s_%d" % (k, l), parts[i], shard2d(w, k, l), shard2d(m, k, l), shard2d(v, k, l),
                         None if own is None else own[i])
            if k in mlp_res:
                mlp_res[k][l] = out
            else:
                res[k] = [o.reshape(w[k].shape) for o in out]

    def put_g(l, g, after, norm_grads):
        if pending:
            lp, names, send, recv, sums, lands = pending.pop()
            sums, parts = _chip_wait("chip_wait_%d" % lp, send, recv, sums, lands, after)
            adam_layer(lp, names, parts, sums)
        names = list(MIX[l]) + ["mlp_up", "mlp_down"]
        stacks = []
        for k in names:
            a = g[k]
            if k == "pool_w":
                a = a.reshape(G, N_DEV, cg // N_DEV, cg).transpose(1, 0, 2, 3).astype(BF16)
            stacks.append(a.reshape(N_CHIP, 2, -1, w[k].shape[-1]))
        got = _pair_exchange("pair_exchange_%d" % l, stacks)
        sums = [_pair_add("pair_add", a, b) for a, b in zip(stacks, got)]
        small_g.update({k: g[k] for k in g if k in SMALL_SPLIT or k in SMALL_REP})
        if l > 0:
            send, recv, sums, lands, tok = _chip_start("chip_start_%d" % l, sums, [jnp.zeros(a.shape, a.dtype) for a in sums])
            pending.append((l, names, send, recv, sums, lands))
            return tok
        small_g["norm_g"] = norm_grads
        gsm = []
        for nme, r, rp in _split_rows(SMALL_SPLIT, w):
            gsm.append(jnp.pad(small_g[nme].reshape(r, N_DEV, L), ((0, rp - r), (0, 0), (0, 0))))
        sm_sh = jnp.concatenate(gsm, axis=0).transpose(1, 0, 2)
        parts, r_sh, r_rep = _chip_exchange(sums, sm_sh, _pack_rep(small_g, D))
        adam_layer(0, names, parts, None)
        out = _adamw("adamw_split", r_sh, _pack_split(SMALL_SPLIT, w), _pack_split(SMALL_SPLIT, m), _pack_split(SMALL_SPLIT, v))
        off = 0
        for nme, r, rp in _split_rows(SMALL_SPLIT, w):
            res[nme] = [o[off:off + r].reshape(w[nme].shape) for o in out]
            off += rp
        out = _adamw("adamw_rep", r_rep, _pack_rep(w, D), _pack_rep(m, D), _pack_rep(v, D))
        off = 0
        for nme in SMALL_REP:
            r = w[nme].shape[0] // D
            res[nme] = [o[off:off + r].reshape(w[nme].shape) for o in out]
            off += r
        return None

    pos = positions.reshape(T, 1).astype(F32)
    loss_row, grad_x = _local_step(x.reshape(T, D), pos, loss_target.reshape(T, D), common["norm_g"], get_w, put_g, [dep], T, D)
    loss = lax.psum(0.5 * jnp.sum(loss_row) / D, AXES)
    for k in mlp_res:
        res[k] = [jnp.stack([mlp_res[k][l][i] for l in range(DEPTH)]).reshape(w[k].shape) for i in range(4)]
    outs = [loss, grad_x.reshape(x.shape)]
    for i in range(4):
        outs += [res[k][i] for k in WEIGHTS]
    return tuple(outs)
```

```python
import functools
import math

import numpy as np
import jax
import jax.numpy as jnp
from jax import lax
from jax.experimental import pallas as pl
from jax.experimental.pallas import tpu as pltpu

F32, BF16 = jnp.float32, jnp.bfloat16
MESH = pl.DeviceIdType.MESH
AXES = ("x", "y", "c")
N_DEV = 8
N_CHIP = 4
EPS = 1e-6
DEPTH = 4
CONV_WIDTH = 31
POOL_WINDOWS = (2, 4, 8, 16)
SHORT_CONV_WIDTH = 3
RET_HEADS = 4
RET_CHUNK = 128
ROPE_BASE = 10000.0
ADAM_LR, ADAM_B1, ADAM_B2, ADAM_EPS, ADAM_WD, ADAM_STEP = 0.001, 0.9, 0.999, 1e-08, 0.01, 10
LANES = 128
SEQ_PAD = 32
VMEM_LIMIT = 56 * 1024 * 1024

NN = (((1,), (0,)), ((), ()))
NT = (((1,), (1,)), ((), ()))
TN = (((0,), (0,)), ((), ()))


def _pcall(body, **kw):
    return pl.pallas_call(body, **kw)


def _params(sem):
    return pltpu.CompilerParams(dimension_semantics=sem, vmem_limit_bytes=VMEM_LIMIT)


def _tile(n, pref):
    t = min(n, pref)
    while n % t:
        t //= 2
    return t


def _rowwise(name, fn, ins, outs, rows, tr, deps=()):
    n_in = len(ins)
    n_dep = len(deps)
    in_specs = []
    for arr, nc, cb in ins:
        if nc is None:
            in_specs.append(pl.BlockSpec(arr.shape, lambda i, nd=arr.ndim: (0,) * nd))
        else:
            in_specs.append(pl.BlockSpec((tr, nc), lambda i, cb=cb: (i, cb)))
    out_specs, out_shapes = [], []
    for o in outs:
        if o[0] == "row":
            out_shapes.append(jax.ShapeDtypeStruct((rows, o[1]), o[2]))
            out_specs.append(pl.BlockSpec((tr, o[1]), lambda i: (i, 0)))
        else:
            out_shapes.append(jax.ShapeDtypeStruct(o[1], F32))
            out_specs.append(pl.BlockSpec(o[1], lambda i: (0, 0)))

    in_specs = in_specs + [pl.BlockSpec(memory_space=pl.ANY)] * n_dep

    def body(*refs):
        res = fn(*[r[...] for r in refs[:n_in]])
        if not isinstance(res, (tuple, list)):
            res = (res,)
        i = pl.program_id(0)
        for o, ref, r in zip(outs, refs[n_in + n_dep:], res):
            if o[0] == "row":
                ref[...] = r.astype(o[2])
            else:
                @pl.when(i == 0)
                def _(ref=ref, r=r):
                    ref[...] = r

                @pl.when(i > 0)
                def _(ref=ref, r=r):
                    ref[...] += r

    return _pcall(body, name=name, grid=(rows // tr,), in_specs=in_specs, out_specs=out_specs,
                  out_shape=out_shapes, compiler_params=_params(("arbitrary",)))(*[a for a, _, _ in ins], *deps)


def _rms(x, g):
    return x * lax.rsqrt(jnp.mean(x * x, axis=-1, keepdims=True) + EPS) * g


def _colsum(x):
    return jnp.sum(x, axis=0, keepdims=True)


def _square(r):
    rf = r.astype(F32)
    return (rf * rf).astype(r.dtype)


def _ln_silu(c, g, b):
    mu = jnp.mean(c, axis=-1, keepdims=True)
    xc = c - mu
    y = xc * lax.rsqrt(jnp.mean(xc * xc, axis=-1, keepdims=True) + EPS) * g + b
    return y * jax.nn.sigmoid(y)


def _glu(za, zg):
    return za * jax.nn.sigmoid(zg)


def _gate_norm(o, g):
    dv = o.shape[-1] // RET_HEADS
    parts = []
    for h in range(RET_HEADS):
        oh = o[:, h * dv:(h + 1) * dv]
        parts.append(oh * lax.rsqrt(jnp.mean(oh * oh, axis=-1, keepdims=True) + EPS))
    return (g * jax.nn.sigmoid(g)) * jnp.concatenate(parts, axis=-1)


def _rot(x, cos, sin, sign):
    dk = x.shape[-1] // RET_HEADS
    half = dk // 2
    parts = []
    for h in range(RET_HEADS):
        x1 = x[:, h * dk:h * dk + half]
        x2 = x[:, h * dk + half:(h + 1) * dk]
        parts.append(x1 * cos - sign * (x2 * sin))
        parts.append(sign * (x1 * sin) + x2 * cos)
    return jnp.concatenate(parts, axis=-1)


def _mm(name, mode, grid, ins, outs, epi=None, aliases=None, pre=None):
    dn = {"nn": NN, "nt": NT, "tn": TN}[mode]
    gk = grid[2]
    n_in, n_out = len(ins), len(outs)

    def body(*refs):
        out_refs = refs[n_in:n_in + n_out]
        a = refs[0][...] if pre is None else pre(refs[0][...])
        part = lax.dot_general(a, refs[1][...], dn, preferred_element_type=F32)

        def finish(acc):
            res = epi(acc, *[r[...] for r in refs[2:n_in]]) if epi is not None else (acc,)
            for o, r in zip(out_refs, res):
                o[...] = r.astype(o.dtype)

        if gk == 1:
            finish(part)
        else:
            acc_ref = refs[-1]
            k = pl.program_id(2)

            @pl.when(k == 0)
            def _():
                acc_ref[...] = part

            @pl.when(k > 0)
            def _():
                acc_ref[...] += part

            @pl.when(k == gk - 1)
            def _():
                finish(acc_ref[...])

    scratch = []
    if gk > 1:
        blk = [d for d in outs[0][2].block_shape if d is not None]
        scratch = [pltpu.VMEM(tuple(blk), F32)]
    kw = {}
    if aliases:
        kw["input_output_aliases"] = aliases
    return _pcall(body, name=name, grid=grid, in_specs=[s for _, s in ins], out_specs=[s for _, _, s in outs],
                  out_shape=[jax.ShapeDtypeStruct(sh, dt) for sh, dt, _ in outs], scratch_shapes=scratch,
                  compiler_params=_params(("parallel", "parallel", "arbitrary")), **kw)(*[a for a, _ in ins])


def _spec(block, fn):
    return pl.BlockSpec(block, fn)


def _mm_cols(name, a, wg, lead, T, epi=None, extras=(), n_out=1, out_dtype=F32):
    K, nb = wg.shape[-2], wg.shape[-1]
    tm = _tile(T, 2048)
    wblock = (None,) * (1 + len(lead)) + (K, nb)
    ins = [(a, _spec((tm, K), lambda i, j, k: (i, 0))), (wg, _spec(wblock, lambda i, j, k: (j,) + tuple(lead) + (0, 0)))]
    for e in extras:
        if e.shape[0] == 1:
            ins.append((e, _spec((1, nb), lambda i, j, k: (0, j))))
        else:
            ins.append((e, _spec((tm, nb), lambda i, j, k: (i, j))))
    outs = [((T, N_DEV * nb), out_dtype, _spec((tm, nb), lambda i, j, k: (i, j))) for _ in range(n_out)]
    return _mm(name, "nn", (T // tm, N_DEV, 1), ins, outs, epi)


def _mm_cols_t(name, d, wg, lead, T, epi=None, extras=(), out_dtype=F32):
    K, nb = wg.shape[-2], wg.shape[-1]
    tm = _tile(T, 2048)
    wblock = (None,) * (1 + len(lead)) + (K, nb)
    ins = [(d, _spec((tm, nb), lambda i, j, k: (i, k))), (wg, _spec(wblock, lambda i, j, k: (k,) + tuple(lead) + (0, 0)))]
    for e in extras:
        ins.append((e, _spec((tm, K), lambda i, j, k: (i, 0))))
    outs = [((T, K), out_dtype, _spec((tm, K), lambda i, j, k: (i, 0)))]
    return _mm(name, "nt", (T // tm, 1, N_DEV), ins, outs, epi)[0]


def _mm_cols_grad(name, a, d, wg_shape, lead, T):
    K, nb = wg_shape[-2], wg_shape[-1]
    tk = _tile(T, 2048)
    wblock = (None,) * (1 + len(lead)) + (K, nb)
    ins = [(a, _spec((tk, K), lambda i, j, k: (k, 0))), (d, _spec((tk, nb), lambda i, j, k: (k, j)))]
    outs = [(tuple(wg_shape), BF16, _spec(wblock, lambda i, j, k: (j,) + tuple(lead) + (0, 0)))]
    return _mm(name, "tn", (1, N_DEV, T // tk), ins, outs, lambda acc: (acc,))[0]


def _mm_whole_k(name, a, w3, T, split, pre=None):
    nblk, d1, d2 = w3.shape
    kb = d1 if split == "rows" else d2
    N = d2 if split == "rows" else d1
    tm = _tile(T, 512)

    def body(a_ref, w_ref, o_ref, *scratch):
        av = a_ref[...] if pre is None else pre(a_ref[...])
        if split == "rows":
            o_ref[...] = lax.dot_general(av, w_ref[...].reshape(nblk * kb, N), NN, preferred_element_type=F32)
        else:
            wcat = scratch[0]

            @pl.when(pl.program_id(0) == 0)
            def _():
                for j in range(nblk):
                    wcat[:, j * kb:(j + 1) * kb] = w_ref[j]

            o_ref[...] = lax.dot_general(av, wcat[...], NT, preferred_element_type=F32)

    scratch = [] if split == "rows" else [pltpu.VMEM((N, nblk * kb), w3.dtype)]
    return _pcall(body, name=name, grid=(T // tm,),
                  in_specs=[pl.BlockSpec((tm, nblk * kb), lambda i: (i, 0)), pl.BlockSpec(w3.shape, lambda i: (0, 0, 0))],
                  out_specs=pl.BlockSpec((tm, N), lambda i: (i, 0)), out_shape=jax.ShapeDtypeStruct((T, N), F32),
                  scratch_shapes=scratch, compiler_params=_params(("arbitrary",)))(a, w3)


def _mm_rows(name, a, w2, T, epi=None, extras=(), out_dtype=F32):
    Kin, N = w2.shape
    tm, tn = _tile(T, 2048), _tile(N, 512)
    ins = [(a, _spec((tm, Kin), lambda i, j, k: (i, 0))), (w2, _spec((Kin, tn), lambda i, j, k: (0, j)))]
    for e in extras:
        ins.append((e, _spec((1, tn), lambda i, j, k: (0, j))))
    outs = [((T, N), out_dtype, _spec((tm, tn), lambda i, j, k: (i, j)))]
    return _mm(name, "nn", (T // tm, N // tn, 1), ins, outs, epi)[0]


def _mm_rows_t(name, d, w2, T, out_dtype=F32):
    Kin, N = w2.shape
    tm, tn = _tile(T, 2048), _tile(Kin, 512)
    ins = [(d, _spec((tm, N), lambda i, j, k: (i, 0))), (w2, _spec((tn, N), lambda i, j, k: (j, 0)))]
    outs = [((T, Kin), out_dtype, _spec((tm, tn), lambda i, j, k: (i, j)))]
    return _mm(name, "nt", (T // tm, Kin // tn, 1), ins, outs)[0]


def _mm_rows_grad(name, a, d, T):
    Kin, N = a.shape[1], d.shape[1]
    tk, tm, tn = _tile(T, 2048), _tile(Kin, 512), _tile(N, 1024)
    ins = [(a, _spec((tk, tm), lambda i, j, k: (k, i))), (d, _spec((tk, tn), lambda i, j, k: (k, j)))]
    outs = [((Kin, N), BF16, _spec((tm, tn), lambda i, j, k: (i, j)))]
    return _mm(name, "tn", (Kin // tm, N // tn, T // tk), ins, outs, lambda acc: (acc,))[0]


def _dwconv_fwd(name, x, w, bias, T):
    W, C = w.shape
    cb = min(C, LANES)
    tc = _tile(T, 256)

    def body(x_ref, w_ref, b_ref, o_ref, xp_ref):
        xp_ref[pl.ds(0, SEQ_PAD), :] = jnp.zeros((SEQ_PAD, cb), F32)

        def fill(c, _):
            base = pl.multiple_of(c * tc, tc)
            xp_ref[pl.ds(base + SEQ_PAD, tc), :] = x_ref[pl.ds(base, tc), :]
            return 0

        lax.fori_loop(0, T // tc, fill, 0)

        def chunk(c, _):
            base = pl.multiple_of(c * tc, tc)
            acc = jnp.zeros((tc, cb), F32) + b_ref[...]
            for k in range(W):
                acc = acc + w_ref[pl.ds(k, 1), :] * xp_ref[pl.ds(base + SEQ_PAD - (W - 1) + k, tc), :]
            o_ref[pl.ds(base, tc), :] = acc
            return 0

        lax.fori_loop(0, T // tc, chunk, 0)

    return _pcall(body, name=name, grid=(C // cb,),
                  in_specs=[pl.BlockSpec((T, cb), lambda i: (0, i)), pl.BlockSpec((W, cb), lambda i: (0, i)),
                            pl.BlockSpec((1, cb), lambda i: (0, i))],
                  out_specs=pl.BlockSpec((T, cb), lambda i: (0, i)), out_shape=jax.ShapeDtypeStruct((T, C), F32),
                  scratch_shapes=[pltpu.VMEM((T + SEQ_PAD, cb), F32)],
                  compiler_params=_params(("parallel",)))(x, w, bias)


def _dwconv_bwd(name, dc, x, w, T):
    W, C = w.shape
    cb = min(C, LANES)
    tc = _tile(T, 128)

    def body(dc_ref, x_ref, w_ref, dx_ref, dw_ref, xp_ref, dp_ref, acc_ref):
        xp_ref[pl.ds(0, SEQ_PAD), :] = jnp.zeros((SEQ_PAD, cb), F32)
        dp_ref[pl.ds(T, SEQ_PAD), :] = jnp.zeros((SEQ_PAD, cb), F32)
        acc_ref[...] = jnp.zeros((W * 8, cb), F32)

        def fill(c, _):
            base = pl.multiple_of(c * tc, tc)
            xp_ref[pl.ds(base + SEQ_PAD, tc), :] = x_ref[pl.ds(base, tc), :]
            dp_ref[pl.ds(base, tc), :] = dc_ref[pl.ds(base, tc), :]
            return 0

        lax.fori_loop(0, T // tc, fill, 0)

        def chunk(c, _):
            base = pl.multiple_of(c * tc, tc)
            d = dc_ref[pl.ds(base, tc), :]
            acc = jnp.zeros((tc, cb), F32)
            for k in range(W):
                acc = acc + w_ref[pl.ds(k, 1), :] * dp_ref[pl.ds(base + (W - 1) - k, tc), :]
                prod = d * xp_ref[pl.ds(base + SEQ_PAD - (W - 1) + k, tc), :]
                acc_ref[pl.ds(8 * k, 8), :] += jnp.sum(prod.reshape(tc // 8, 8, cb), axis=0)
            dx_ref[pl.ds(base, tc), :] = acc
            return 0

        lax.fori_loop(0, T // tc, chunk, 0)
        for k in range(W):
            dw_ref[pl.ds(k, 1), :] = jnp.sum(acc_ref[pl.ds(8 * k, 8), :], axis=0, keepdims=True)

    return _pcall(body, name=name, grid=(C // cb,),
                  in_specs=[pl.BlockSpec((T, cb), lambda i: (0, i)), pl.BlockSpec((T, cb), lambda i: (0, i)),
                            pl.BlockSpec((W, cb), lambda i: (0, i))],
                  out_specs=[pl.BlockSpec((T, cb), lambda i: (0, i)), pl.BlockSpec((W, cb), lambda i: (0, i))],
                  out_shape=[jax.ShapeDtypeStruct((T, C), F32), jax.ShapeDtypeStruct((W, C), F32)],
                  scratch_shapes=[pltpu.VMEM((T + SEQ_PAD, cb), F32), pltpu.VMEM((T + SEQ_PAD, cb), F32),
                                  pltpu.VMEM((W * 8, cb), F32)],
                  compiler_params=_params(("parallel",)))(dc, x, w)


def _pool_count(base, tc, cg, win):
    t = (lax.broadcasted_iota(jnp.int32, (tc, cg), 0) + base + 1).astype(F32)
    return jnp.minimum(t, float(win))


def _pool_fwd(name, u, T):
    D = u.shape[1]
    G = len(POOL_WINDOWS)
    cg = min(D // G, LANES)
    nb = (D // G) // cg
    tc = _tile(T, 128)

    def body(u_ref, o_ref, up_ref):
        g = pl.program_id(0)
        up_ref[pl.ds(0, SEQ_PAD), :] = jnp.zeros((SEQ_PAD, cg), F32)

        def fill(c, _):
            base = pl.multiple_of(c * tc, tc)
            up_ref[pl.ds(base + SEQ_PAD, tc), :] = u_ref[pl.ds(base, tc), :]
            return 0

        lax.fori_loop(0, T // tc, fill, 0)
        for gi, win in enumerate(POOL_WINDOWS):
            @pl.when(g == gi)
            def _(win=win):
                def chunk(c, _):
                    base = pl.multiple_of(c * tc, tc)
                    acc = up_ref[pl.ds(base + SEQ_PAD, tc), :]
                    for j in range(1, win):
                        acc = acc + up_ref[pl.ds(base + SEQ_PAD - j, tc), :]
                    mixed = acc / _pool_count(base, tc, cg, win) - u_ref[pl.ds(base, tc), :]
                    o_ref[pl.ds(base, tc), :] = mixed.astype(BF16)
                    return 0

                lax.fori_loop(0, T // tc, chunk, 0)

    return _pcall(body, name=name, grid=(G, nb), in_specs=[pl.BlockSpec((T, cg), lambda g, i: (0, g * nb + i))],
                  out_specs=pl.BlockSpec((T, cg), lambda g, i: (0, g * nb + i)), out_shape=jax.ShapeDtypeStruct((T, D), BF16),
                  scratch_shapes=[pltpu.VMEM((T + SEQ_PAD, cg), F32)], compiler_params=_params(("parallel", "parallel")))(u)


def _pool_bwd(name, dm, T):
    D = dm.shape[1]
    G = len(POOL_WINDOWS)
    cg = min(D // G, LANES)
    nb = (D // G) // cg
    tc = _tile(T, 128)

    def body(d_ref, o_ref, qp_ref):
        g = pl.program_id(0)
        qp_ref[pl.ds(T, SEQ_PAD), :] = jnp.zeros((SEQ_PAD, cg), F32)
        for gi, win in enumerate(POOL_WINDOWS):
            @pl.when(g == gi)
            def _(win=win):
                def fill(c, _):
                    base = pl.multiple_of(c * tc, tc)
                    qp_ref[pl.ds(base, tc), :] = d_ref[pl.ds(base, tc), :] / _pool_count(base, tc, cg, win)
                    return 0

                lax.fori_loop(0, T // tc, fill, 0)

                def chunk(c, _):
                    base = pl.multiple_of(c * tc, tc)
                    acc = qp_ref[pl.ds(base, tc), :]
                    for j in range(1, win):
                        acc = acc + qp_ref[pl.ds(base + j, tc), :]
                    o_ref[pl.ds(base, tc), :] = acc - d_ref[pl.ds(base, tc), :]
                    return 0

                lax.fori_loop(0, T // tc, chunk, 0)

    return _pcall(body, name=name, grid=(G, nb), in_specs=[pl.BlockSpec((T, cg), lambda g, i: (0, g * nb + i))],
                  out_specs=pl.BlockSpec((T, cg), lambda g, i: (0, g * nb + i)), out_shape=jax.ShapeDtypeStruct((T, D), F32),
                  scratch_shapes=[pltpu.VMEM((T + SEQ_PAD, cg), F32)], compiler_params=_params(("parallel", "parallel")))(dm)


def _pool_mm(name, mode, a, b, T, scale=None):
    G = len(POOL_WINDOWS)
    D = a.shape[1]
    cg = D // G
    tm = _tile(T, 2048)
    if mode == "tn":
        ins = [(a, _spec((tm, cg), lambda i, j, k: (k, j))), (b, _spec((tm, cg), lambda i, j, k: (k, j)))]
        outs = [((G, cg, cg), F32, _spec((None, cg, cg), lambda i, j, k: (j, 0, 0)))]
        return _mm(name, "tn", (1, G, T // tm), ins, outs, lambda acc: (acc,))[0]
    ins = [(a, _spec((tm, cg), lambda i, j, k: (i, j))), (b, _spec((None, cg, cg), lambda i, j, k: (j, 0, 0)))]
    if scale is not None:
        ins.append((scale, _spec((1, cg), lambda i, j, k: (0, j))))
        outs = [((T, D), F32, _spec((tm, cg), lambda i, j, k: (i, j))), ((T, D), BF16, _spec((tm, cg), lambda i, j, k: (i, j)))]
        return _mm(name, mode, (T // tm, G, 1), ins, outs, lambda acc, s: (acc * s, acc))
    outs = [((T, D), F32, _spec((tm, cg), lambda i, j, k: (i, j)))]
    return _mm(name, mode, (T // tm, G, 1), ins, outs)[0]


def _sc_mid_fwd(name, z, w, T):
    D = w.shape[1]
    W = w.shape[0]
    cb = min(D, LANES)
    nb = D // cb
    tc = _tile(T, 256)

    def body(b_ref, c_ref, v_ref, w_ref, o_ref, pp_ref):
        pp_ref[pl.ds(0, SEQ_PAD), :] = jnp.zeros((SEQ_PAD, cb), F32)

        def fill(c, _):
            base = pl.multiple_of(c * tc, tc)
            pp_ref[pl.ds(base + SEQ_PAD, tc), :] = c_ref[pl.ds(base, tc), :] * v_ref[pl.ds(base, tc), :]
            return 0

        lax.fori_loop(0, T // tc, fill, 0)

        def chunk(c, _):
            base = pl.multiple_of(c * tc, tc)
            q = jnp.zeros((tc, cb), F32)
            for k in range(W):
                q = q + w_ref[pl.ds(k, 1), :] * pp_ref[pl.ds(base + SEQ_PAD - (W - 1) + k, tc), :]
            o_ref[pl.ds(base, tc), :] = (b_ref[pl.ds(base, tc), :] * q).astype(BF16)
            return 0

        lax.fori_loop(0, T // tc, chunk, 0)

    return _pcall(body, name=name, grid=(nb,),
                  in_specs=[pl.BlockSpec((T, cb), lambda i: (0, i)), pl.BlockSpec((T, cb), lambda i: (0, nb + i)),
                            pl.BlockSpec((T, cb), lambda i: (0, 2 * nb + i)), pl.BlockSpec((W, cb), lambda i: (0, i))],
                  out_specs=pl.BlockSpec((T, cb), lambda i: (0, i)), out_shape=jax.ShapeDtypeStruct((T, D), BF16),
                  scratch_shapes=[pltpu.VMEM((T + SEQ_PAD, cb), F32)],
                  compiler_params=_params(("parallel",)))(z, z, z, w)


def _sc_mid_bwd(name, z, dm, w, T):
    D = w.shape[1]
    W = w.shape[0]
    cb = min(D, LANES)
    nb = D // cb
    tc = _tile(T, 128)

    def body(b_ref, c_ref, v_ref, d_ref, w_ref, dz_ref, dw_ref, pp_ref, dq_ref, acc_ref):
        part = pl.program_id(0)
        pp_ref[pl.ds(0, SEQ_PAD), :] = jnp.zeros((SEQ_PAD, cb), F32)
        dq_ref[pl.ds(T, SEQ_PAD), :] = jnp.zeros((SEQ_PAD, cb), F32)
        acc_ref[...] = jnp.zeros((W * 8, cb), F32)

        def fill(c, _):
            base = pl.multiple_of(c * tc, tc)
            pp_ref[pl.ds(base + SEQ_PAD, tc), :] = c_ref[pl.ds(base, tc), :] * v_ref[pl.ds(base, tc), :]
            dq_ref[pl.ds(base, tc), :] = d_ref[pl.ds(base, tc), :] * b_ref[pl.ds(base, tc), :]
            return 0

        lax.fori_loop(0, T // tc, fill, 0)

        @pl.when(part == 0)
        def _():
            def chunk(c, _):
                base = pl.multiple_of(c * tc, tc)
                dq = dq_ref[pl.ds(base, tc), :]
                q = jnp.zeros((tc, cb), F32)
                for k in range(W):
                    p = pp_ref[pl.ds(base + SEQ_PAD - (W - 1) + k, tc), :]
                    q = q + w_ref[pl.ds(k, 1), :] * p
                    acc_ref[pl.ds(8 * k, 8), :] += jnp.sum((dq * p).reshape(tc // 8, 8, cb), axis=0)
                dz_ref[pl.ds(base, tc), :] = (d_ref[pl.ds(base, tc), :] * q).astype(BF16)
                return 0

            lax.fori_loop(0, T // tc, chunk, 0)

        @pl.when(part > 0)
        def _():
            def chunk(c, _):
                base = pl.multiple_of(c * tc, tc)
                dp = jnp.zeros((tc, cb), F32)
                for k in range(W):
                    dp = dp + w_ref[pl.ds(k, 1), :] * dq_ref[pl.ds(base + (W - 1) - k, tc), :]
                other = jnp.where(part == 1, v_ref[pl.ds(base, tc), :], c_ref[pl.ds(base, tc), :])
                dz_ref[pl.ds(base, tc), :] = (dp * other).astype(BF16)
                return 0

            lax.fori_loop(0, T // tc, chunk, 0)

        for k in range(W):
            dw_ref[pl.ds(k, 1), :] = jnp.sum(acc_ref[pl.ds(8 * k, 8), :], axis=0, keepdims=True)

    dz, dw3 = _pcall(body, name=name, grid=(3, nb),
                     in_specs=[pl.BlockSpec((T, cb), lambda p, i: (0, i)), pl.BlockSpec((T, cb), lambda p, i: (0, nb + i)),
                               pl.BlockSpec((T, cb), lambda p, i: (0, 2 * nb + i)), pl.BlockSpec((T, cb), lambda p, i: (0, i)),
                               pl.BlockSpec((W, cb), lambda p, i: (0, i))],
                     out_specs=[pl.BlockSpec((T, cb), lambda p, i: (0, p * nb + i)),
                                pl.BlockSpec((None, W, cb), lambda p, i: (p, 0, i))],
                     out_shape=[jax.ShapeDtypeStruct((T, 3 * D), BF16), jax.ShapeDtypeStruct((3, W, D), F32)],
                     scratch_shapes=[pltpu.VMEM((T + SEQ_PAD, cb), F32), pltpu.VMEM((T + SEQ_PAD, cb), F32),
                                     pltpu.VMEM((W * 8, cb), F32)],
                     compiler_params=_params(("parallel", "parallel")))(z, z, z, dm, w)
    return dz, dw3


def _ret_tables(dk, dv):
    C = RET_CHUNK
    lg = np.log1p(-np.exp2(-5.0 - np.arange(RET_HEADS, dtype=np.float64)))
    idx = np.arange(C, dtype=np.float64)
    rel = idx[:, None] - idx[None, :]
    mask = np.where(rel >= 0, np.exp(lg[:, None, None] * np.maximum(rel, 0.0)), 0.0)
    qd = np.exp(lg[:, None] * (idx + 1.0))[:, :, None] * np.ones((1, 1, dk))
    kd = np.exp(lg[:, None] * (C - 1.0 - idx))[:, :, None] * np.ones((1, 1, dk))
    cd = np.exp(lg * C)[:, None, None] * np.ones((1, 1, dv))
    return tuple(jnp.asarray(a, F32) for a in (mask, qd, kd, cd))


def _ret_fwd(name, q, k, v, tabs, T):
    H, C = RET_HEADS, RET_CHUNK
    dk, dv = q.shape[1] // H, v.shape[1] // H
    nch = T // C
    mask, qd, kd, cd = tabs

    def body(q_ref, k_ref, v_ref, m_ref, qd_ref, kd_ref, cd_ref, o_ref, st_ref, s_ref):
        n = pl.program_id(1)

        @pl.when(n == 0)
        def _():
            s_ref[...] = jnp.zeros((dk, dv), F32)

        qv, kv, vv = q_ref[...], k_ref[...], v_ref[...]
        s = s_ref[...]
        sb = s.astype(BF16)
        st_ref[...] = sb
        scores = lax.dot_general(qv, kv, NT, preferred_element_type=F32) * m_ref[...]
        intra = lax.dot_general(scores.astype(BF16), vv, NN, preferred_element_type=F32)
        cross = lax.dot_general((qv.astype(F32) * qd_ref[...]).astype(BF16), sb, NN, preferred_element_type=F32)
        o_ref[...] = intra + cross
        upd = lax.dot_general((kv.astype(F32) * kd_ref[...]).astype(BF16), vv, TN, preferred_element_type=F32)
        s_ref[...] = s * cd_ref[...] + upd

    return _pcall(body, name=name, grid=(H, nch),
                  in_specs=[pl.BlockSpec((C, dk), lambda h, n: (n, h)), pl.BlockSpec((C, dk), lambda h, n: (n, h)),
                            pl.BlockSpec((C, dv), lambda h, n: (n, h)), pl.BlockSpec((None, C, C), lambda h, n: (h, 0, 0)),
                            pl.BlockSpec((None, C, dk), lambda h, n: (h, 0, 0)), pl.BlockSpec((None, C, dk), lambda h, n: (h, 0, 0)),
                            pl.BlockSpec((None, 1, dv), lambda h, n: (h, 0, 0))],
                  out_specs=[pl.BlockSpec((C, dv), lambda h, n: (n, h)),
                             pl.BlockSpec((None, None, dk, dv), lambda h, n: (h, n, 0, 0))],
                  out_shape=[jax.ShapeDtypeStruct((T, H * dv), F32), jax.ShapeDtypeStruct((H, nch, dk, dv), BF16)],
                  scratch_shapes=[pltpu.VMEM((dk, dv), F32)],
                  compiler_params=_params(("parallel", "arbitrary")))(q, k, v, mask, qd, kd, cd)


def _ret_bwd(name, q, k, v, do, st, tabs, T):
    H, C = RET_HEADS, RET_CHUNK
    dk, dv = q.shape[1] // H, v.shape[1] // H
    nch = T // C
    mask, qd, kd, cd = tabs

    def body(q_ref, k_ref, v_ref, do_ref, st_ref, m_ref, qd_ref, kd_ref, cd_ref, dq_ref, dk_ref, dv_ref, ds_ref):
        n = pl.program_id(1)

        @pl.when(n == 0)
        def _():
            ds_ref[...] = jnp.zeros((dk, dv), F32)

        qv, kv, vv, dov, sb = q_ref[...], k_ref[...], v_ref[...], do_ref[...], st_ref[...]
        m = m_ref[...]
        ds = ds_ref[...]
        dsb = ds.astype(BF16)
        a = (lax.dot_general(qv, kv, NT, preferred_element_type=F32) * m).astype(BF16)
        dp = (lax.dot_general(dov, vv, NT, preferred_element_type=F32) * m).astype(BF16)
        qa = (qv.astype(F32) * qd_ref[...]).astype(BF16)
        kb = (kv.astype(F32) * kd_ref[...]).astype(BF16)
        dq_ref[...] = (lax.dot_general(dp, kv, NN, preferred_element_type=F32)
                       + qd_ref[...] * lax.dot_general(dov, sb, NT, preferred_element_type=F32))
        dk_ref[...] = (lax.dot_general(dp, qv, TN, preferred_element_type=F32)
                       + kd_ref[...] * lax.dot_general(vv, dsb, NT, preferred_element_type=F32))
        dv_ref[...] = (lax.dot_general(a, dov, TN, preferred_element_type=F32)
                       + lax.dot_general(kb, dsb, NN, preferred_element_type=F32)).astype(BF16)
        ds_ref[...] = ds * cd_ref[...] + lax.dot_general(qa, dov, TN, preferred_element_type=F32)

    rev = lambda h, n: (nch - 1 - n, h)
    return _pcall(body, name=name, grid=(H, nch),
                  in_specs=[pl.BlockSpec((C, dk), rev), pl.BlockSpec((C, dk), rev), pl.BlockSpec((C, dv), rev),
                            pl.BlockSpec((C, dv), rev), pl.BlockSpec((None, None, dk, dv), lambda h, n: (h, nch - 1 - n, 0, 0)),
                            pl.BlockSpec((None, C, C), lambda h, n: (h, 0, 0)), pl.BlockSpec((None, C, dk), lambda h, n: (h, 0, 0)),
                            pl.BlockSpec((None, C, dk), lambda h, n: (h, 0, 0)), pl.BlockSpec((None, 1, dv), lambda h, n: (h, 0, 0))],
                  out_specs=[pl.BlockSpec((C, dk), rev), pl.BlockSpec((C, dk), rev), pl.BlockSpec((C, dv), rev)],
                  out_shape=[jax.ShapeDtypeStruct((T, H * dk), F32), jax.ShapeDtypeStruct((T, H * dk), F32),
                             jax.ShapeDtypeStruct((T, H * dv), BF16)],
                  scratch_shapes=[pltpu.VMEM((dk, dv), F32)],
                  compiler_params=_params(("parallel", "arbitrary")))(q, k, v, do, st, mask, qd, kd, cd)


def _local_step(x, pos, tgt, ng, get_w, put_g, first_deps, T, D):
    tr = _tile(T, 256)
    trw = _tile(T, 128)
    row = lambda i: ng[i:i + 1]

    u = _rowwise("rms_first", lambda xv, gv: _rms(xv, gv), [(x, D, 0), (row(0), None, None)], [("row", D, BF16)], T, tr,
                 deps=first_deps)[0]
    h = x
    saved = []
    cos = sin = None
    tabs = None
    for l in range(DEPTH):
        mixer = l % 4
        wts = get_w(l, "mix", h)
        sv = {"h_in": h, "u": u}
        if mixer == 0:
            z = _mm_cols("conv_in", u, wts["conv_w_in"], (), T, lambda acc, b: (acc + b,), (wts["conv_b_in"],))[0]
            v = _rowwise("conv_glu", _glu, [(z, D, 0), (z, D, 1)], [("row", D, F32)], T, tr)[0]
            c = _dwconv_fwd("conv_dw", v, wts["conv_dw"], wts["conv_dw_b"], T)
            s = _rowwise("conv_ln", _ln_silu, [(c, D, 0), (wts["conv_ln_g"], None, None), (wts["conv_ln_b"], None, None)],
                         [("row", D, BF16)], T, tr)[0]
            y = _mm_rows("conv_out", s, wts["conv_w_out"], T, lambda acc, b: (acc + b,), (wts["conv_b_out"],))
            sv.update(z=z, v=v, c=c, s=s)
        elif mixer == 1:
            mixed = _pool_fwd("pool_fwd", u, T)
            y, ycat = _pool_mm("pool_mm", "nn", mixed, wts["pool_w"], T, wts["pool_scale"])
            sv.update(mixed=mixed, ycat=ycat)
        elif mixer == 2:
            z = _mm_cols("sc_in", u, wts["sc_w_in"], (), T)[0]
            m = _sc_mid_fwd("sc_mid", z, wts["sc_dw"], T)
            y = _mm_rows("sc_out", m, wts["sc_w_out"], T)
            sv.update(z=z, m=m)
        else:
            dk = D // RET_HEADS
            half = dk // 2
            inv_freq = jnp.asarray((ROPE_BASE ** (-np.arange(half, dtype=np.float64) / half)).astype(np.float32)).reshape(1, half)

            def _cs(p, f):
                ang = p * f
                return jnp.cos(ang), jnp.sin(ang)

            cos, sin = _rowwise("rope_tab", _cs, [(pos, 1, 0), (inv_freq, None, None)],
                                [("row", half, F32), ("row", half, F32)], T, tr)
            tabs = _ret_tables(dk, 2 * dk)
            z = _mm_cols("ret_in", u, wts["ret_w_in"], (), T)[0]

            def _pre(zq, zk, zv, cv, sn):
                return _rot(zq, cv, sn, 1.0), _rot(zk, cv, sn, 1.0) * (dk ** -0.5), zv

            qr, kr, vb = _rowwise("ret_pre", _pre, [(z, D, 0), (z, D, 1), (z, 2 * D, 1), (cos, half, 0), (sin, half, 0)],
                                  [("row", D, BF16), ("row", D, BF16), ("row", 2 * D, BF16)], T, trw)
            o, st = _ret_fwd("ret_fwd", qr, kr, vb, tabs, T)
            out = _rowwise("ret_post", _gate_norm, [(o, 2 * D, 0), (z, 2 * D, 2)], [("row", 2 * D, BF16)], T, trw)[0]
            y = _mm_rows("ret_out", out, wts["ret_w_out"], T)
            sv.update(z=z, qr=qr, kr=kr, vb=vb, o=o, st=st, out=out)
        sv["y1"] = y
        h, u = _rowwise("bound_a", lambda hv, yv, gp, gn: (lambda hn: (hn, _rms(hn, gn)))(hv + _rms(yv, gp)),
                        [(h, D, 0), (y, D, 0), (row(4 * l + 1), None, None), (row(4 * l + 2), None, None)],
                        [("row", D, F32), ("row", D, BF16)], T, tr)
        sv["h_mid"], sv["u2"] = h, u
        wts = dict(wts, **get_w(l, "mlp", h))
        sv["w"] = wts
        r = _mm_cols("mlp_up", u, wts["mlp_up"], (), T, lambda acc: (jnp.maximum(acc, 0.0),), (), 1, BF16)[0]
        y = _mm_whole_k("mlp_down", r, wts["mlp_down"], T, "rows", pre=_square)
        sv.update(r=r, y2=y)
        if l + 1 < DEPTH:
            udt = F32 if (l + 1) % 4 == 1 else BF16
            h, u = _rowwise("bound_b", lambda hv, yv, gp, gn: (lambda hn: (hn, _rms(hn, gn)))(hv + _rms(yv, gp)),
                            [(h, D, 0), (y, D, 0), (row(4 * l + 3), None, None), (row(4 * l + 4), None, None)],
                            [("row", D, F32), ("row", D, udt)], T, tr)
        saved.append(sv)

    def _final(hv, yv, gp, tv):
        hn = hv + _rms(yv, gp)
        e = hn - tv
        return e * (1.0 / D), _colsum(e * e)

    dh, loss_row = _rowwise("final", _final, [(h, D, 0), (y, D, 0), (row(4 * DEPTH - 1), None, None), (tgt, D, 0)],
                            [("row", D, F32), ("acc", (1, D))], T, tr)

    gn = [None] * (4 * DEPTH)
    F = saved[0]["r"].shape[1]
    du_next = None
    tok = None
    for l in reversed(range(DEPTH)):
        sv = saved[l]
        wts = sv["w"]
        g = {}
        deps = [] if tok is None else [tok]
        mixer = l % 4
        if du_next is None:
            def _bb(dhv, yv, gp):
                _, vjp = jax.vjp(_rms, yv, gp)
                dy, dgp = vjp(dhv)
                return dy, dgp

            dy, gn[4 * l + 3] = _rowwise("bwd_last", _bb, [(dh, D, 0), (sv["y2"], D, 0), (row(4 * l + 3), None, None)],
                                         [("row", D, BF16), ("acc", (1, D))], T, tr, deps)
        else:
            dh, dy, gn[4 * l + 3], gn[4 * l + 4], _ = _bound_bwd("bwd_b", dh, du_next, saved[l + 1]["h_in"], sv["y2"],
                                                              row(4 * l + 3), row(4 * l + 4), T, D, tr, deps)
        F8 = F // N_DEV
        tm = _tile(T, 2048)
        da = _mm("mlp_dhid", "nt", (T // tm, N_DEV, 1),
                 [(dy, _spec((tm, D), lambda i, j, k: (i, 0))),
                  (wts["mlp_down"], _spec((None, F8, D), lambda i, j, k: (j, 0, 0))),
                  (sv["r"], _spec((tm, F8), lambda i, j, k: (i, j)))],
                 [((T, F), BF16, _spec((tm, F8), lambda i, j, k: (i, j)))],
                 lambda acc, rr: (acc * (2.0 * rr.astype(F32)),))[0]
        tk = _tile(T, 2048)
        g["mlp_down"] = _mm("mlp_dwdown", "tn", (1, N_DEV, T // tk),
                            [(sv["r"], _spec((tk, F8), lambda i, j, k: (k, j))), (dy, _spec((tk, D), lambda i, j, k: (k, 0)))],
                            [(wts["mlp_down"].shape, BF16, _spec((None, F8, D), lambda i, j, k: (j, 0, 0)))],
                            lambda acc: (acc,), pre=_square)[0]
        g["mlp_up"] = _mm_cols_grad("mlp_dwup", sv["u2"], da, wts["mlp_up"].shape, (), T)
        du = _mm_whole_k("mlp_du", da, wts["mlp_up"], T, "cols")
        tok = put_g(l, "mlp", g, du, None)
        g = {}
        dh, dy, gn[4 * l + 1], gn[4 * l + 2], dy_sum = _bound_bwd("bwd_a", dh, du, sv["h_mid"], sv["y1"],
                                                                row(4 * l + 1), row(4 * l + 2), T, D, tr, [] if tok is None else [tok])
        u = sv["u"]
        if mixer == 0:
            g["conv_b_out"] = dy_sum
            ds = _mm_rows_t("conv_ds", dy, wts["conv_w_out"], T)
            g["conv_w_out"] = _mm_rows_grad("conv_dwout", sv["s"], dy, T)

            def _lnb(cv, gv, bv, dsv):
                _, vjp = jax.vjp(_ln_silu, cv, gv, bv)
                dc, dg, db = vjp(dsv)
                return dc, dg, db, _colsum(dc)

            dc, g["conv_ln_g"], g["conv_ln_b"], g["conv_dw_b"] = _rowwise(
                "conv_ln_bwd", _lnb, [(sv["c"], D, 0), (wts["conv_ln_g"], None, None), (wts["conv_ln_b"], None, None), (ds, D, 0)],
                [("row", D, F32), ("acc", (1, D)), ("acc", (1, D)), ("acc", (1, D))], T, tr)
            dv, g["conv_dw"] = _dwconv_bwd("conv_dw_bwd", dc, sv["v"], wts["conv_dw"], T)

            def _glub(za, zg, dvv):
                _, vjp = jax.vjp(_glu, za, zg)
                dza, dzg = vjp(dvv)
                dz = jnp.concatenate([dza, dzg], axis=-1)
                return dz, _colsum(dz)

            dz, g["conv_b_in"] = _rowwise("conv_glu_bwd", _glub, [(sv["z"], D, 0), (sv["z"], D, 1), (dv, D, 0)],
                                          [("row", 2 * D, BF16), ("acc", (1, 2 * D))], T, tr)
            g["conv_w_in"] = _mm_cols_grad("conv_dwin", u, dz, wts["conv_w_in"].shape, (), T)
            du = _mm_cols_t("conv_du", dz, wts["conv_w_in"], (), T)
        elif mixer == 1:
            def _psb(dyv, yc, sc):
                return dyv * sc, _colsum(dyv.astype(F32) * yc.astype(F32))

            dyg, g["pool_scale"] = _rowwise("pool_scale_bwd", _psb, [(dy, D, 0), (sv["ycat"], D, 0), (wts["pool_scale"], None, None)],
                                            [("row", D, BF16), ("acc", (1, D))], T, tr)
            dmix = _pool_mm("pool_dmix", "nt", dyg, wts["pool_w"], T)
            g["pool_w"] = _pool_mm("pool_dw", "tn", sv["mixed"], dyg, T)
            du = _pool_bwd("pool_bwd", dmix, T)
        elif mixer == 2:
            dm = _mm_rows_t("sc_dm", dy, wts["sc_w_out"], T)
            g["sc_w_out"] = _mm_rows_grad("sc_dwout", sv["m"], dy, T)
            dz, dw3 = _sc_mid_bwd("sc_mid_bwd", sv["z"], dm, wts["sc_dw"], T)
            g["sc_dw"] = dw3[0]
            g["sc_w_in"] = _mm_cols_grad("sc_dwin", u, dz, wts["sc_w_in"].shape, (), T)
            du = _mm_cols_t("sc_du", dz, wts["sc_w_in"], (), T)
        else:
            dk = D // RET_HEADS
            half = dk // 2
            dout = _mm_rows_t("ret_dout", dy, wts["ret_w_out"], T)
            g["ret_w_out"] = _mm_rows_grad("ret_dwout", sv["out"], dy, T)

            def _postb(ov, gv, dv_):
                _, vjp = jax.vjp(_gate_norm, ov, gv)
                return vjp(dv_)

            do, dzg = _rowwise("ret_post_bwd", _postb, [(sv["o"], 2 * D, 0), (sv["z"], 2 * D, 2), (dout, 2 * D, 0)],
                               [("row", 2 * D, BF16), ("row", 2 * D, BF16)], T, trw)
            dqr, dkr, dvb = _ret_bwd("ret_bwd", sv["qr"], sv["kr"], sv["vb"], do, sv["st"], tabs, T)

            def _preb(dq, dk_, dvv, dgv, cv, sn):
                return jnp.concatenate([_rot(dq, cv, sn, -1.0).astype(BF16), (_rot(dk_, cv, sn, -1.0) * (dk ** -0.5)).astype(BF16),
                                        dvv, dgv], axis=-1)

            dz = _rowwise("ret_pre_bwd", _preb, [(dqr, D, 0), (dkr, D, 0), (dvb, 2 * D, 0), (dzg, 2 * D, 0), (cos, half, 0), (sin, half, 0)],
                          [("row", 6 * D, BF16)], T, trw)[0]
            g["ret_w_in"] = _mm_cols_grad("ret_dwin", u, dz, wts["ret_w_in"].shape, (), T)
            du = _mm_cols_t("ret_du", dz, wts["ret_w_in"], (), T)
        du_next = du
        if l > 0:
            tok = put_g(l, "mix", g, du, None)

    def _fb(dhv, duv, xv, gv):
        _, vjp = jax.vjp(_rms, xv, gv)
        dx, dg = vjp(duv)
        return dhv + dx, dg

    grad_x, gn[0] = _rowwise("bwd_first", _fb, [(dh, D, 0), (du_next, D, 0), (x, D, 0), (row(0), None, None)],
                             [("row", D, F32), ("acc", (1, D))], T, tr)
    put_g(0, "mix", g, grad_x, jnp.concatenate(gn, axis=0))
    return loss_row, grad_x


def _bound_bwd(name, dh, du, h_new, y, g_post, g_next, T, D, tr, deps=()):
    def fn(dhv, duv, hn, yv, gp, gnx):
        _, vjp1 = jax.vjp(_rms, hn, gnx)
        d1, dgn = vjp1(duv.astype(F32))
        dht = dhv + d1
        _, vjp2 = jax.vjp(_rms, yv, gp)
        dy, dgp = vjp2(dht)
        return dht, dy, dgp, dgn, _colsum(dy)

    return _rowwise(name, fn, [(dh, D, 0), (du, D, 0), (h_new, D, 0), (y, D, 0), (g_post, None, None), (g_next, None, None)],
                    [("row", D, F32), ("row", D, BF16), ("acc", (1, D)), ("acc", (1, D)), ("acc", (1, D))], T, tr, deps)


def _adamw(name, parts, w, m, v, own=None):
    P, R, C = parts.shape
    tr = _tile(R, 256)

    def body(*refs):
        p_ref = refs[0]
        w_ref, m_ref, v_ref, g_ref, d_ref, m2_ref, v2_ref = refs[-7:]

        def slot(i):
            if own is None:
                return p_ref[i].astype(F32)
            here = 2 * lax.axis_index("x") + lax.axis_index("y") == i
            return jnp.where(here, refs[1][i], p_ref[i]).astype(F32)

        g = slot(0)
        for i in range(1, P):
            g = g + slot(i)
        wv = w_ref[...]
        m2 = ADAM_B1 * m_ref[...] + (1.0 - ADAM_B1) * g
        v2 = ADAM_B2 * v_ref[...] + (1.0 - ADAM_B2) * (g * g)
        m_hat = m2 / (1.0 - ADAM_B1 ** ADAM_STEP)
        v_hat = v2 / (1.0 - ADAM_B2 ** ADAM_STEP)
        g_ref[...] = g
        d_ref[...] = -ADAM_LR * (m_hat / (jnp.sqrt(v_hat) + ADAM_EPS) + ADAM_WD * wv)
        m2_ref[...] = m2
        v2_ref[...] = v2

    blk = pl.BlockSpec((tr, C), lambda i: (i, 0))
    pblk = pl.BlockSpec((P, tr, C), lambda i: (0, i, 0))
    stacks = [parts] if own is None else [parts, own]
    return _pcall(body, name=name, grid=(R // tr,),
                  in_specs=[pblk] * len(stacks) + [blk, blk, blk], out_specs=[blk] * 4,
                  out_shape=[jax.ShapeDtypeStruct((R, C), F32)] * 4, compiler_params=_params(("parallel",)))(*stacks, w, m, v)


def _pair_add(name, a, b):
    Q, _, R, C = a.shape
    tr = _tile(R, 1024)
    blk = pl.BlockSpec((None, tr, C), lambda q, i: (q, i, 0))

    def body(a_ref, b_ref, o_ref):
        mine = jnp.where(lax.axis_index("c") == 0, a_ref[0], a_ref[1])
        o_ref[...] = (mine.astype(F32) + b_ref[...].astype(F32)).astype(BF16)

    return _pcall(body, name=name, grid=(Q, R // tr),
                  in_specs=[pl.BlockSpec((None, 2, tr, C), lambda q, i: (q, 0, i, 0)), blk], out_specs=blk,
                  out_shape=jax.ShapeDtypeStruct((Q, R, C), BF16), compiler_params=_params(("parallel", "parallel")))(a, b)


ANY = pl.BlockSpec(memory_space=pl.ANY)


def _place():
    x, y, c = lax.axis_index("x"), lax.axis_index("y"), lax.axis_index("c")
    chips = [(1 - x, y), (x, 1 - y), (1 - x, 1 - y)]
    return x, y, c, chips


def _gather_all(locs):
    n = len(locs)

    def body(*refs):
        ins, outs = refs[:n], refs[n:2 * n]
        send, recv, lsem = refs[2 * n:]
        x, y, c, chips = _place()
        me = 4 * x + 2 * y + c
        sib = (x, y, 1 - c)

        def cp(a, k, slot, to, src=None):
            return pltpu.make_async_remote_copy(src_ref=outs[a].at[slot] if src is None else src, dst_ref=outs[a].at[slot],
                                                send_sem=send.at[a, k], recv_sem=recv.at[a, k], device_id=to, device_id_type=MESH)

        mine = [pltpu.make_async_copy(ins[a], outs[a].at[me], lsem.at[a]) for a in range(n)]
        for d in mine:
            d.start()
        first = []
        for a in range(n):
            first.append(cp(a, 0, me, sib, src=ins[a]))
            for j, chip in enumerate(chips):
                first.append(cp(a, 1 + j, me, (chip[0], chip[1], c), src=ins[a]))
        for d in first:
            d.start()
        passed = []
        for j, chip in enumerate(chips):
            slot = 4 * chip[0] + 2 * chip[1] + c
            for a in range(n):
                cp(a, 1 + j, slot, sib).wait_recv()
                p = cp(a, 4 + j, slot, sib)
                p.start()
                passed.append(p)
        for a in range(n):
            cp(a, 0, 4 * x + 2 * y + (1 - c), sib).wait_recv()
        for j, chip in enumerate(chips):
            for a in range(n):
                cp(a, 4 + j, 4 * chip[0] + 2 * chip[1] + (1 - c), sib).wait_recv()
        for d in first + passed:
            d.wait_send()
        for d in mine:
            d.wait()

    return _pcall(body, name="gather_all", in_specs=[ANY] * n, out_specs=[ANY] * n,
                  out_shape=[jax.ShapeDtypeStruct((N_DEV,) + a.shape, a.dtype) for a in locs],
                  scratch_shapes=[pltpu.SemaphoreType.DMA((n, 7)), pltpu.SemaphoreType.DMA((n, 7)), pltpu.SemaphoreType.DMA((n,))],
                  compiler_params=pltpu.CompilerParams(has_side_effects=True))(*locs)


HBM = pl.BlockSpec(memory_space=pltpu.HBM)
SEM = pl.BlockSpec(memory_space=pltpu.SEMAPHORE)
DATAFLOW = pltpu.SideEffectType.DATAFLOW_SIDE_EFFECTING


def _peers():
    x, y, c = lax.axis_index("x"), lax.axis_index("y"), lax.axis_index("c")
    out = []
    for r in range(1, N_DEV):
        px = 1 - x if (r >> 2) & 1 else x
        py = 1 - y if (r >> 1) & 1 else y
        pc = 1 - c if r & 1 else c
        out.append(((px, py, pc), 4 * px + 2 * py + pc))
    return 4 * x + 2 * y + c, out


def _hbm(a):
    return pltpu.with_memory_space_constraint(a, pltpu.HBM)


def _gather_start(name, locs, lands, deps):
    n = len(locs)

    def body(*refs):
        ins, lnd = refs[:n], refs[n:2 * n]
        send, recv = refs[2 * n + len(deps)], refs[2 * n + len(deps) + 1]
        token = refs[-1]
        me, peers = _peers()
        for r, (to, _) in enumerate(peers):
            for a in range(n):
                pltpu.make_async_remote_copy(src_ref=ins[a], dst_ref=lnd[a].at[me], send_sem=send.at[a * 7 + r],
                                             recv_sem=recv.at[a * 7 + r], device_id=to, device_id_type=MESH).start()
        token[...] = jnp.zeros_like(token)

    out_shape = ((pltpu.SemaphoreType.DMA((n * 7,)), pltpu.SemaphoreType.DMA((n * 7,)))
                 + tuple(pltpu.HBM(a.shape, a.dtype) for a in locs) + tuple(pltpu.HBM(a.shape, a.dtype) for a in lands)
                 + (jax.ShapeDtypeStruct((8, LANES), F32),))
    res = _pcall(body, name=name, out_shape=out_shape, in_specs=[HBM] * (2 * n) + [ANY] * len(deps),
                 out_specs=(SEM, SEM) + (HBM,) * (2 * n) + (pl.BlockSpec(memory_space=pltpu.VMEM),),
                 input_output_aliases={a: 2 + a for a in range(2 * n)},
                 compiler_params=pltpu.CompilerParams(has_side_effects=DATAFLOW))(
        *[_hbm(a) for a in locs], *[_hbm(a) for a in lands], *deps)
    return res[0], res[1], res[2:2 + n], res[2 + n:2 + 2 * n], res[-1]


def _gather_wait(name, send, recv, shards, lands, after):
    n = len(shards)

    def body(*refs):
        ins, lnd = refs[:n], refs[n:2 * n]
        send, recv = refs[2 * n], refs[2 * n + 1]
        _, peers = _peers()
        for r, (to, idx) in enumerate(peers):
            for a in range(n):
                cp = pltpu.make_async_remote_copy(src_ref=ins[a], dst_ref=lnd[a].at[idx], send_sem=send.at[a * 7 + r],
                                                  recv_sem=recv.at[a * 7 + r], device_id=to, device_id_type=MESH)
                cp.wait_send()
                cp.wait_recv()

    out_shape = tuple(pltpu.HBM(a.shape, a.dtype) for a in shards) + tuple(pltpu.HBM(a.shape, a.dtype) for a in lands)
    res = _pcall(body, name=name, out_shape=out_shape, in_specs=[HBM] * (2 * n) + [SEM, SEM, ANY], out_specs=(HBM,) * (2 * n),
                 input_output_aliases={a: a for a in range(2 * n)},
                 compiler_params=pltpu.CompilerParams(has_side_effects=DATAFLOW))(*shards, *lands, send, recv, after)
    return res[n:]


def _chip_start(name, cs, lands):
    n = len(cs)

    def body(*refs):
        ins, lnd = refs[:n], refs[n:2 * n]
        send, recv = refs[2 * n], refs[2 * n + 1]
        token = refs[-1]
        x, y, c, chips = _place()
        qme = 2 * x + y
        for j, chip in enumerate(chips):
            for a in range(n):
                pltpu.make_async_remote_copy(src_ref=ins[a].at[2 * chip[0] + chip[1]], dst_ref=lnd[a].at[qme], send_sem=send.at[a * 3 + j],
                                             recv_sem=recv.at[a * 3 + j], device_id=(chip[0], chip[1], c), device_id_type=MESH).start()
        token[...] = jnp.zeros_like(token)

    out_shape = ((pltpu.SemaphoreType.DMA((n * 3,)), pltpu.SemaphoreType.DMA((n * 3,)))
                 + tuple(pltpu.HBM(a.shape, a.dtype) for a in cs) + tuple(pltpu.HBM(a.shape, a.dtype) for a in lands)
                 + (jax.ShapeDtypeStruct((8, LANES), F32),))
    res = _pcall(body, name=name, out_shape=out_shape, in_specs=[HBM] * (2 * n),
                 out_specs=(SEM, SEM) + (HBM,) * (2 * n) + (pl.BlockSpec(memory_space=pltpu.VMEM),),
                 input_output_aliases={a: 2 + a for a in range(2 * n)},
                 compiler_params=pltpu.CompilerParams(has_side_effects=DATAFLOW))(*[_hbm(a) for a in cs], *[_hbm(a) for a in lands])
    return res[0], res[1], res[2:2 + n], res[2 + n:2 + 2 * n], res[-1]


def _chip_wait(name, send, recv, cs, lands, after):
    n = len(cs)

    def body(*refs):
        ins, lnd = refs[:n], refs[n:2 * n]
        send, recv = refs[2 * n], refs[2 * n + 1]
        x, y, c, chips = _place()
        for j, chip in enumerate(chips):
            q = 2 * chip[0] + chip[1]
            for a in range(n):
                cp = pltpu.make_async_remote_copy(src_ref=ins[a].at[q], dst_ref=lnd[a].at[q], send_sem=send.at[a * 3 + j],
                                                  recv_sem=recv.at[a * 3 + j], device_id=(chip[0], chip[1], c), device_id_type=MESH)
                cp.wait_send()
                cp.wait_recv()

    out_shape = tuple(pltpu.HBM(a.shape, a.dtype) for a in cs) + tuple(pltpu.HBM(a.shape, a.dtype) for a in lands)
    res = _pcall(body, name=name, out_shape=out_shape, in_specs=[HBM] * (2 * n) + [SEM, SEM, ANY], out_specs=(HBM,) * (2 * n),
                 input_output_aliases={a: a for a in range(2 * n)},
                 compiler_params=pltpu.CompilerParams(has_side_effects=DATAFLOW))(*cs, *lands, send, recv, after)
    return res[:n], res[n:]


def _pair_exchange(name, gs):
    n = len(gs)

    def body(*refs):
        ins, got = refs[:n], refs[n:2 * n]
        send, recv = refs[2 * n:]
        x, y, c, _ = _place()
        sib = (x, y, 1 - c)
        rem = [pltpu.make_async_remote_copy(src_ref=ins[a].at[q, 1 - c], dst_ref=got[a].at[q], send_sem=send.at[a, q],
                                            recv_sem=recv.at[a, q], device_id=sib, device_id_type=MESH)
               for a in range(n) for q in range(N_CHIP)]
        for d in rem:
            d.start()
        for d in rem:
            d.wait()

    half = [jax.ShapeDtypeStruct((N_CHIP,) + g.shape[2:], g.dtype) for g in gs]
    return _pcall(body, name=name, in_specs=[ANY] * n, out_specs=[ANY] * n, out_shape=half,
                  scratch_shapes=[pltpu.SemaphoreType.DMA((n, N_CHIP)), pltpu.SemaphoreType.DMA((n, N_CHIP))],
                  compiler_params=pltpu.CompilerParams(has_side_effects=True))(*gs)


def _chip_exchange(cs, sm_sh, sm_rep):
    n = len(cs)

    def body(*refs):
        ins, sh, rep = refs[:n], refs[n], refs[n + 1]
        outs, osh, orep = refs[n + 2:2 * n + 2], refs[2 * n + 2], refs[2 * n + 3]
        send, recv, lsem, ssend, srecv, slsem = refs[2 * n + 4:]
        x, y, c, chips = _place()
        qme = 2 * x + y
        me = 4 * x + 2 * y + c
        loc = [pltpu.make_async_copy(ins[a].at[qme], outs[a].at[qme], lsem.at[a]) for a in range(n)]
        loc.append(pltpu.make_async_copy(sh.at[me], osh.at[me], slsem.at[0]))
        loc.append(pltpu.make_async_copy(rep, orep.at[me], slsem.at[1]))
        rem, waits = [], []
        for j, chip in enumerate(chips):
            q = 2 * chip[0] + chip[1]
            to = (chip[0], chip[1], c)
            for a in range(n):
                rem.append(pltpu.make_async_remote_copy(src_ref=ins[a].at[q], dst_ref=outs[a].at[qme], send_sem=send.at[a, j],
                                                        recv_sem=recv.at[a, j], device_id=to, device_id_type=MESH))
                waits.append(pltpu.make_async_remote_copy(src_ref=ins[a].at[q], dst_ref=outs[a].at[q], send_sem=send.at[a, j],
                                                          recv_sem=recv.at[a, j], device_id=to, device_id_type=MESH))
        for r in range(1, N_DEV):
            fx, fy, fc = (r >> 2) & 1, (r >> 1) & 1, r & 1
            px = 1 - x if fx else x
            py = 1 - y if fy else y
            pc = 1 - c if fc else c
            t = 4 * px + 2 * py + pc
            to = (px, py, pc)
            rem.append(pltpu.make_async_remote_copy(src_ref=sh.at[t], dst_ref=osh.at[me], send_sem=ssend.at[0, r - 1],
                                                    recv_sem=srecv.at[0, r - 1], device_id=to, device_id_type=MESH))
            waits.append(pltpu.make_async_remote_copy(src_ref=sh.at[t], dst_ref=osh.at[t], send_sem=ssend.at[0, r - 1],
                                                      recv_sem=srecv.at[0, r - 1], device_id=to, device_id_type=MESH))
            rem.append(pltpu.make_async_remote_copy(src_ref=rep, dst_ref=orep.at[me], send_sem=ssend.at[1, r - 1],
                                                    recv_sem=srecv.at[1, r - 1], device_id=to, device_id_type=MESH))
            waits.append(pltpu.make_async_remote_copy(src_ref=rep, dst_ref=orep.at[t], send_sem=ssend.at[1, r - 1],
                                                      recv_sem=srecv.at[1, r - 1], device_id=to, device_id_type=MESH))
        for d in loc + rem:
            d.start()
        for d in waits:
            d.wait_recv()
        for d in rem:
            d.wait_send()
        for d in loc:
            d.wait()

    out_shape = [jax.ShapeDtypeStruct(a.shape, a.dtype) for a in cs]
    out_shape += [jax.ShapeDtypeStruct(sm_sh.shape, F32), jax.ShapeDtypeStruct((N_DEV,) + sm_rep.shape, F32)]
    res = _pcall(body, name="chip_exchange", in_specs=[ANY] * (n + 2), out_specs=[ANY] * (n + 2), out_shape=out_shape,
                 scratch_shapes=[pltpu.SemaphoreType.DMA((n, 3)), pltpu.SemaphoreType.DMA((n, 3)), pltpu.SemaphoreType.DMA((n,)),
                                 pltpu.SemaphoreType.DMA((2, 7)), pltpu.SemaphoreType.DMA((2, 7)), pltpu.SemaphoreType.DMA((2,))],
                 compiler_params=pltpu.CompilerParams(has_side_effects=True))(*cs, sm_sh, sm_rep)
    return res[:n], res[n], res[n + 1]


BIG = ("mlp_up", "mlp_down", "conv_w_in", "conv_w_out", "pool_w", "sc_w_in", "sc_w_out", "ret_w_in", "ret_w_out")
SMALL_SPLIT = ("norm_g", "conv_dw", "sc_dw")
SMALL_REP = ("conv_b_in", "conv_dw_b", "conv_ln_g", "conv_ln_b", "conv_b_out", "pool_scale")
WEIGHTS = ("norm_g", "mlp_up", "mlp_down", "conv_w_in", "conv_b_in", "conv_dw", "conv_dw_b", "conv_ln_g", "conv_ln_b",
           "conv_w_out", "conv_b_out", "pool_w", "pool_scale", "sc_w_in", "sc_dw", "sc_w_out", "ret_w_in", "ret_w_out")


def _pad_rows(a, rows):
    return jnp.pad(a, ((0, rows - a.shape[0]), (0, 0)))


def _split_rows(names, shards):
    out = []
    for nme in names:
        r = int(np.prod(shards[nme].shape[:-1]))
        out.append((nme, r, -(-r // 8) * 8))
    return out


def _pack_split(names, arrs):
    parts = []
    for nme, r, rp in _split_rows(names, arrs):
        parts.append(_pad_rows(arrs[nme].reshape(r, arrs[nme].shape[-1]), rp))
    return jnp.concatenate(parts, axis=0)


def _pack_rep(arrs, D):
    parts = [arrs[nme].reshape(-1, D) for nme in SMALL_REP]
    rows = sum(p.shape[0] for p in parts)
    return _pad_rows(jnp.concatenate(parts, axis=0), -(-rows // 8) * 8)


def kernel(x, positions, norm_g, mlp_up, mlp_down, conv_w_in, conv_b_in, conv_dw, conv_dw_b, conv_ln_g, conv_ln_b, conv_w_out, conv_b_out, pool_w, pool_scale, sc_w_in, sc_dw, sc_w_out, ret_w_in, ret_w_out, loss_target, m_norm_g, m_mlp_up, m_mlp_down, m_conv_w_in, m_conv_b_in, m_conv_dw, m_conv_dw_b, m_conv_ln_g, m_conv_ln_b, m_conv_w_out, m_conv_b_out, m_pool_w, m_pool_scale, m_sc_w_in, m_sc_dw, m_sc_w_out, m_ret_w_in, m_ret_w_out, v_norm_g, v_mlp_up, v_mlp_down, v_conv_w_in, v_conv_b_in, v_conv_dw, v_conv_dw_b, v_conv_ln_g, v_conv_ln_b, v_conv_w_out, v_conv_b_out, v_pool_w, v_pool_scale, v_sc_w_in, v_sc_dw, v_sc_w_out, v_ret_w_in, v_ret_w_out):
    w = dict(norm_g=norm_g, mlp_up=mlp_up, mlp_down=mlp_down, conv_w_in=conv_w_in, conv_b_in=conv_b_in, conv_dw=conv_dw,
             conv_dw_b=conv_dw_b, conv_ln_g=conv_ln_g, conv_ln_b=conv_ln_b, conv_w_out=conv_w_out, conv_b_out=conv_b_out,
             pool_w=pool_w, pool_scale=pool_scale, sc_w_in=sc_w_in, sc_dw=sc_dw, sc_w_out=sc_w_out, ret_w_in=ret_w_in, ret_w_out=ret_w_out)
    m = dict(norm_g=m_norm_g, mlp_up=m_mlp_up, mlp_down=m_mlp_down, conv_w_in=m_conv_w_in, conv_b_in=m_conv_b_in, conv_dw=m_conv_dw,
             conv_dw_b=m_conv_dw_b, conv_ln_g=m_conv_ln_g, conv_ln_b=m_conv_ln_b, conv_w_out=m_conv_w_out, conv_b_out=m_conv_b_out,
             pool_w=m_pool_w, pool_scale=m_pool_scale, sc_w_in=m_sc_w_in, sc_dw=m_sc_dw, sc_w_out=m_sc_w_out, ret_w_in=m_ret_w_in,
             ret_w_out=m_ret_w_out)
    v = dict(norm_g=v_norm_g, mlp_up=v_mlp_up, mlp_down=v_mlp_down, conv_w_in=v_conv_w_in, conv_b_in=v_conv_b_in, conv_dw=v_conv_dw,
             conv_dw_b=v_conv_dw_b, conv_ln_g=v_conv_ln_g, conv_ln_b=v_conv_ln_b, conv_w_out=v_conv_w_out, conv_b_out=v_conv_b_out,
             pool_w=v_pool_w, pool_scale=v_pool_scale, sc_w_in=v_sc_w_in, sc_dw=v_sc_dw, sc_w_out=v_sc_w_out, ret_w_in=v_ret_w_in,
             ret_w_out=v_ret_w_out)
    T, D = x.shape[1], x.shape[2]
    L = D // N_DEV
    G = len(POOL_WINDOWS)
    cg = D // G

    me = 4 * lax.axis_index("x") + 2 * lax.axis_index("y") + lax.axis_index("c")
    MIX = (("conv_w_in", "conv_w_out"), ("pool_w",), ("sc_w_in", "sc_w_out"), ("ret_w_in", "ret_w_out"))

    def shards(l, part):
        out = {k: w[k].astype(BF16) for k in MIX[l]} if part != "mlp" else {}
        if part != "mix":
            out["mlp_up"], out["mlp_down"] = w["mlp_up"][l].astype(BF16), w["mlp_down"][l].astype(BF16)
        return out

    sh0 = shards(0, "mix")
    gath0 = _gather_all(list(sh0.values()) + [_pack_split(SMALL_SPLIT, w)])
    small = gath0[-1]
    common = {k: w[k].reshape(1, -1) for k in SMALL_REP}
    off = 0
    for nme, r, rp in _split_rows(SMALL_SPLIT, w):
        common[nme] = small[:, off:off + r, :].transpose(1, 0, 2).reshape(r, D)
        off += rp
    flying = {}
    dep = small
    for l, part in [(0, "mlp")] + [(l, "both") for l in range(1, DEPTH)]:
        shl = shards(l, part)
        lands = [lax.dynamic_update_slice(lax.empty((N_DEV,) + a.shape, a.dtype), a[None], (me,) + (0,) * a.ndim) for a in shl.values()]
        send, recv, thru, lands, dep = _gather_start("gather_start_%d" % l, list(shl.values()), lands, [dep])
        flying[l] = (list(shl.keys()), send, recv, thru, lands)

    def get_w(l, part, after):
        if part == "mlp" and l > 0:
            return {}
        if l == 0 and part == "mix":
            gw = dict(zip(sh0.keys(), gath0[:-1]))
        else:
            names, send, recv, thru, lands = flying[l]
            gw = dict(zip(names, _gather_wait("gather_wait_%d" % l, send, recv, thru, lands, after)))
        out = dict(common)
        for k, a in gw.items():
            if k.endswith("_w_out"):
                out[k] = a.reshape(-1, D)
            elif k == "pool_w":
                out[k] = a.transpose(1, 0, 2, 3).reshape(G, cg, cg)
            else:
                out[k] = a
        return out

    res = {}
    mlp_res = {"mlp_up": [None] * DEPTH, "mlp_down": [None] * DEPTH}
    pending = []
    small_g = {}

    def shard2d(src, k, l):
        a = src[k][l] if k in mlp_res else src[k]
        return a.reshape(-1, a.shape[-1])

    def adam_layer(l, names, parts, own):
        for i, k in enumerate(names):
            out = _adamw("adamw_%s_%d" % (k, l), parts[i], shard2d(w, k, l), shard2d(m, k, l), shard2d(v, k, l),
                         None if own is None else own[i])
            if k in mlp_res:
                mlp_res[k][l] = out
            else:
                res[k] = [o.reshape(w[k].shape) for o in out]

    held = {}

    def put_g(l, part, g, after, norm_grads):
        small_g.update({k: g[k] for k in g if k in SMALL_SPLIT or k in SMALL_REP})
        if part == "mlp" and l > 0:
            held.update(g)
            return None
        if pending:
            tag, lp, names, send, recv, sums, lands = pending.pop()
            sums, parts = _chip_wait("chip_wait_" + tag, send, recv, sums, lands, after)
            adam_layer(lp, names, parts, sums)
        last = l == 0 and part == "mix"
        if l > 0:
            g = dict(g, **held)
            held.clear()
            names = list(MIX[l]) + ["mlp_up", "mlp_down"]
        else:
            names = list(MIX[0]) if last else ["mlp_up", "mlp_down"]
        tag = "%d%s" % (l, "" if l > 0 else part)
        stacks = []
        for k in names:
            a = g[k]
            if k == "pool_w":
                a = a.reshape(G, N_DEV, cg // N_DEV, cg).transpose(1, 0, 2, 3).astype(BF16)
            stacks.append(a.reshape(N_CHIP, 2, -1, w[k].shape[-1]))
        got = _pair_exchange("pair_exchange_" + tag, stacks)
        sums = [_pair_add("pair_add", a, b) for a, b in zip(stacks, got)]
        if not last:
            send, recv, sums, lands, tok = _chip_start("chip_start_" + tag, sums, [lax.empty(a.shape, a.dtype) for a in sums])
            pending.append((tag, l, names, send, recv, sums, lands))
            return tok
        small_g["norm_g"] = norm_grads
        gsm = []
        for nme, r, rp in _split_rows(SMALL_SPLIT, w):
            gsm.append(jnp.pad(small_g[nme].reshape(r, N_DEV, L), ((0, rp - r), (0, 0), (0, 0))))
        sm_sh = jnp.concatenate(gsm, axis=0).transpose(1, 0, 2)
        parts, r_sh, r_rep = _chip_exchange(sums, sm_sh, _pack_rep(small_g, D))
        adam_layer(0, names, parts, None)
        out = _adamw("adamw_split", r_sh, _pack_split(SMALL_SPLIT, w), _pack_split(SMALL_SPLIT, m), _pack_split(SMALL_SPLIT, v))
        off = 0
        for nme, r, rp in _split_rows(SMALL_SPLIT, w):
            res[nme] = [o[off:off + r].reshape(w[nme].shape) for o in out]
            off += rp
        out = _adamw("adamw_rep", r_rep, _pack_rep(w, D), _pack_rep(m, D), _pack_rep(v, D))
        off = 0
        for nme in SMALL_REP:
            r = w[nme].shape[0] // D
            res[nme] = [o[off:off + r].reshape(w[nme].shape) for o in out]
            off += r
        return None

    pos = positions.reshape(T, 1).astype(F32)
    loss_row, grad_x = _local_step(x.reshape(T, D), pos, loss_target.reshape(T, D), common["norm_g"], get_w, put_g, [dep], T, D)
    loss = lax.psum(0.5 * jnp.sum(loss_row) / D, AXES)
    for k in mlp_res:
        res[k] = [jnp.stack([mlp_res[k][l][i] for l in range(DEPTH)]).reshape(w[k].shape) for i in range(4)]
    outs = [loss, grad_x.reshape(x.shape)]
    for i in range(4):
        outs += [res[k][i] for k in WEIGHTS]
    return tuple(outs)
```

```python
import functools
import math

import numpy as np
import jax
import jax.numpy as jnp
from jax import lax
from jax.experimental import pallas as pl
from jax.experimental.pallas import tpu as pltpu

F32, BF16 = jnp.float32, jnp.bfloat16
MESH = pl.DeviceIdType.MESH
AXES = ("x", "y", "c")
N_DEV = 8
N_CHIP = 4
EPS = 1e-6
DEPTH = 4
CONV_WIDTH = 31
POOL_WINDOWS = (2, 4, 8, 16)
SHORT_CONV_WIDTH = 3
RET_HEADS = 4
RET_CHUNK = 128
ROPE_BASE = 10000.0
ADAM_LR, ADAM_B1, ADAM_B2, ADAM_EPS, ADAM_WD, ADAM_STEP = 0.001, 0.9, 0.999, 1e-08, 0.01, 10
LANES = 128
SEQ_PAD = 32
VMEM_LIMIT = 56 * 1024 * 1024

NN = (((1,), (0,)), ((), ()))
NT = (((1,), (1,)), ((), ()))
TN = (((0,), (0,)), ((), ()))


def _pcall(body, **kw):
    return pl.pallas_call(body, **kw)


def _params(sem):
    return pltpu.CompilerParams(dimension_semantics=sem, vmem_limit_bytes=VMEM_LIMIT)


def _tile(n, pref):
    t = min(n, pref)
    while n % t:
        t //= 2
    return t


def _rowwise(name, fn, ins, outs, rows, tr, deps=()):
    n_in = len(ins)
    n_dep = len(deps)
    in_specs = []
    for arr, nc, cb in ins:
        if nc is None:
            in_specs.append(pl.BlockSpec(arr.shape, lambda i, nd=arr.ndim: (0,) * nd))
        else:
            in_specs.append(pl.BlockSpec((tr, nc), lambda i, cb=cb: (i, cb)))
    out_specs, out_shapes = [], []
    for o in outs:
        if o[0] == "row":
            out_shapes.append(jax.ShapeDtypeStruct((rows, o[1]), o[2]))
            out_specs.append(pl.BlockSpec((tr, o[1]), lambda i: (i, 0)))
        else:
            out_shapes.append(jax.ShapeDtypeStruct(o[1], F32))
            out_specs.append(pl.BlockSpec(o[1], lambda i: (0, 0)))

    in_specs = in_specs + [pl.BlockSpec(memory_space=pl.ANY)] * n_dep

    def body(*refs):
        res = fn(*[r[...] for r in refs[:n_in]])
        if not isinstance(res, (tuple, list)):
            res = (res,)
        i = pl.program_id(0)
        for o, ref, r in zip(outs, refs[n_in + n_dep:], res):
            if o[0] == "row":
                ref[...] = r.astype(o[2])
            else:
                @pl.when(i == 0)
                def _(ref=ref, r=r):
                    ref[...] = r

                @pl.when(i > 0)
                def _(ref=ref, r=r):
                    ref[...] += r

    return _pcall(body, name=name, grid=(rows // tr,), in_specs=in_specs, out_specs=out_specs,
                  out_shape=out_shapes, compiler_params=_params(("arbitrary",)))(*[a for a, _, _ in ins], *deps)


def _rms(x, g):
    return x * lax.rsqrt(jnp.mean(x * x, axis=-1, keepdims=True) + EPS) * g


def _colsum(x):
    return jnp.sum(x, axis=0, keepdims=True)


def _square(r):
    rf = r.astype(F32)
    return (rf * rf).astype(r.dtype)


def _ln_silu(c, g, b):
    mu = jnp.mean(c, axis=-1, keepdims=True)
    xc = c - mu
    y = xc * lax.rsqrt(jnp.mean(xc * xc, axis=-1, keepdims=True) + EPS) * g + b
    return y * jax.nn.sigmoid(y)


def _glu(za, zg):
    return za * jax.nn.sigmoid(zg)


def _gate_norm(o, g):
    dv = o.shape[-1] // RET_HEADS
    parts = []
    for h in range(RET_HEADS):
        oh = o[:, h * dv:(h + 1) * dv]
        parts.append(oh * lax.rsqrt(jnp.mean(oh * oh, axis=-1, keepdims=True) + EPS))
    return (g * jax.nn.sigmoid(g)) * jnp.concatenate(parts, axis=-1)


def _rot(x, cos, sin, sign):
    dk = x.shape[-1] // RET_HEADS
    half = dk // 2
    parts = []
    for h in range(RET_HEADS):
        x1 = x[:, h * dk:h * dk + half]
        x2 = x[:, h * dk + half:(h + 1) * dk]
        parts.append(x1 * cos - sign * (x2 * sin))
        parts.append(sign * (x1 * sin) + x2 * cos)
    return jnp.concatenate(parts, axis=-1)


def _mm(name, mode, grid, ins, outs, epi=None, aliases=None, pre=None, deps=()):
    dn = {"nn": NN, "nt": NT, "tn": TN}[mode]
    gk = grid[2]
    n_in, n_out = len(ins), len(outs)
    n_dep = len(deps)

    def body(*refs):
        out_refs = refs[n_in + n_dep:n_in + n_dep + n_out]
        a = refs[0][...] if pre is None else pre(refs[0][...])
        part = lax.dot_general(a, refs[1][...], dn, preferred_element_type=F32)

        def finish(acc):
            res = epi(acc, *[r[...] for r in refs[2:n_in]]) if epi is not None else (acc,)
            for o, r in zip(out_refs, res):
                o[...] = r.astype(o.dtype)

        if gk == 1:
            finish(part)
        else:
            acc_ref = refs[-1]
            k = pl.program_id(2)

            @pl.when(k == 0)
            def _():
                acc_ref[...] = part

            @pl.when(k > 0)
            def _():
                acc_ref[...] += part

            @pl.when(k == gk - 1)
            def _():
                finish(acc_ref[...])

    scratch = []
    if gk > 1:
        blk = [d for d in outs[0][2].block_shape if d is not None]
        scratch = [pltpu.VMEM(tuple(blk), F32)]
    kw = {}
    if aliases:
        kw["input_output_aliases"] = aliases
    return _pcall(body, name=name, grid=grid, in_specs=[s for _, s in ins] + [pl.BlockSpec(memory_space=pl.ANY)] * n_dep,
                  out_specs=[s for _, _, s in outs],
                  out_shape=[jax.ShapeDtypeStruct(sh, dt) for sh, dt, _ in outs], scratch_shapes=scratch,
                  compiler_params=_params(("parallel", "parallel", "arbitrary")), **kw)(*[a for a, _ in ins], *deps)


def _spec(block, fn):
    return pl.BlockSpec(block, fn)


def _mm_cols(name, a, wg, lead, T, epi=None, extras=(), n_out=1, out_dtype=F32, deps=()):
    K, nb = wg.shape[-2], wg.shape[-1]
    tm = _tile(T, 2048)
    wblock = (None,) * (1 + len(lead)) + (K, nb)
    ins = [(a, _spec((tm, K), lambda i, j, k: (i, 0))), (wg, _spec(wblock, lambda i, j, k: (j,) + tuple(lead) + (0, 0)))]
    for e in extras:
        if e.shape[0] == 1:
            ins.append((e, _spec((1, nb), lambda i, j, k: (0, j))))
        else:
            ins.append((e, _spec((tm, nb), lambda i, j, k: (i, j))))
    outs = [((T, N_DEV * nb), out_dtype, _spec((tm, nb), lambda i, j, k: (i, j))) for _ in range(n_out)]
    return _mm(name, "nn", (T // tm, N_DEV, 1), ins, outs, epi, deps=deps)


def _mm_cols_t(name, d, wg, lead, T, epi=None, extras=(), out_dtype=F32):
    K, nb = wg.shape[-2], wg.shape[-1]
    tm = _tile(T, 2048)
    wblock = (None,) * (1 + len(lead)) + (K, nb)
    ins = [(d, _spec((tm, nb), lambda i, j, k: (i, k))), (wg, _spec(wblock, lambda i, j, k: (k,) + tuple(lead) + (0, 0)))]
    for e in extras:
        ins.append((e, _spec((tm, K), lambda i, j, k: (i, 0))))
    outs = [((T, K), out_dtype, _spec((tm, K), lambda i, j, k: (i, 0)))]
    return _mm(name, "nt", (T // tm, 1, N_DEV), ins, outs, epi)[0]


def _mm_cols_grad(name, a, d, wg_shape, lead, T):
    K, nb = wg_shape[-2], wg_shape[-1]
    tk = _tile(T, 2048)
    wblock = (None,) * (1 + len(lead)) + (K, nb)
    ins = [(a, _spec((tk, K), lambda i, j, k: (k, 0))), (d, _spec((tk, nb), lambda i, j, k: (k, j)))]
    outs = [(tuple(wg_shape), BF16, _spec(wblock, lambda i, j, k: (j,) + tuple(lead) + (0, 0)))]
    return _mm(name, "tn", (1, N_DEV, T // tk), ins, outs, lambda acc: (acc,))[0]


def _mm_whole_k(name, a, w3, T, split, pre=None, out_dtype=F32):
    nblk, d1, d2 = w3.shape
    kb = d1 if split == "rows" else d2
    N = d2 if split == "rows" else d1
    tm = _tile(T, 512)

    def body(a_ref, w_ref, o_ref, *scratch):
        av = a_ref[...] if pre is None else pre(a_ref[...])
        if split == "rows":
            o_ref[...] = lax.dot_general(av, w_ref[...].reshape(nblk * kb, N), NN, preferred_element_type=F32).astype(out_dtype)
        else:
            wcat = scratch[0]

            @pl.when(pl.program_id(0) == 0)
            def _():
                for j in range(nblk):
                    wcat[:, j * kb:(j + 1) * kb] = w_ref[j]

            o_ref[...] = lax.dot_general(av, wcat[...], NT, preferred_element_type=F32).astype(out_dtype)

    scratch = [] if split == "rows" else [pltpu.VMEM((N, nblk * kb), w3.dtype)]
    return _pcall(body, name=name, grid=(T // tm,),
                  in_specs=[pl.BlockSpec((tm, nblk * kb), lambda i: (i, 0)), pl.BlockSpec(w3.shape, lambda i: (0, 0, 0))],
                  out_specs=pl.BlockSpec((tm, N), lambda i: (i, 0)), out_shape=jax.ShapeDtypeStruct((T, N), out_dtype),
                  scratch_shapes=scratch, compiler_params=_params(("arbitrary",)))(a, w3)


def _mm_rows(name, a, w2, T, epi=None, extras=(), out_dtype=F32):
    Kin, N = w2.shape
    tm, tn = _tile(T, 2048), _tile(N, 512)
    ins = [(a, _spec((tm, Kin), lambda i, j, k: (i, 0))), (w2, _spec((Kin, tn), lambda i, j, k: (0, j)))]
    for e in extras:
        ins.append((e, _spec((1, tn), lambda i, j, k: (0, j))))
    outs = [((T, N), out_dtype, _spec((tm, tn), lambda i, j, k: (i, j)))]
    return _mm(name, "nn", (T // tm, N // tn, 1), ins, outs, epi)[0]


def _mm_rows_t(name, d, w2, T, out_dtype=F32):
    Kin, N = w2.shape
    tm, tn = _tile(T, 2048), _tile(Kin, 512)
    ins = [(d, _spec((tm, N), lambda i, j, k: (i, 0))), (w2, _spec((tn, N), lambda i, j, k: (j, 0)))]
    outs = [((T, Kin), out_dtype, _spec((tm, tn), lambda i, j, k: (i, j)))]
    return _mm(name, "nt", (T // tm, Kin // tn, 1), ins, outs)[0]


def _mm_rows_grad(name, a, d, T):
    Kin, N = a.shape[1], d.shape[1]
    tk, tm, tn = _tile(T, 2048), _tile(Kin, 512), _tile(N, 1024)
    ins = [(a, _spec((tk, tm), lambda i, j, k: (k, i))), (d, _spec((tk, tn), lambda i, j, k: (k, j)))]
    outs = [((Kin, N), BF16, _spec((tm, tn), lambda i, j, k: (i, j)))]
    return _mm(name, "tn", (Kin // tm, N // tn, T // tk), ins, outs, lambda acc: (acc,))[0]


def _dwconv_fwd(name, x, w, bias, T):
    W, C = w.shape
    cb = min(C, LANES)
    tc = _tile(T, 256)

    def body(x_ref, w_ref, b_ref, o_ref, xp_ref):
        xp_ref[pl.ds(0, SEQ_PAD), :] = jnp.zeros((SEQ_PAD, cb), F32)

        def fill(c, _):
            base = pl.multiple_of(c * tc, tc)
            xp_ref[pl.ds(base + SEQ_PAD, tc), :] = x_ref[pl.ds(base, tc), :]
            return 0

        lax.fori_loop(0, T // tc, fill, 0)

        def chunk(c, _):
            base = pl.multiple_of(c * tc, tc)
            acc = jnp.zeros((tc, cb), F32) + b_ref[...]
            for k in range(W):
                acc = acc + w_ref[pl.ds(k, 1), :] * xp_ref[pl.ds(base + SEQ_PAD - (W - 1) + k, tc), :]
            o_ref[pl.ds(base, tc), :] = acc
            return 0

        lax.fori_loop(0, T // tc, chunk, 0)

    return _pcall(body, name=name, grid=(C // cb,),
                  in_specs=[pl.BlockSpec((T, cb), lambda i: (0, i)), pl.BlockSpec((W, cb), lambda i: (0, i)),
                            pl.BlockSpec((1, cb), lambda i: (0, i))],
                  out_specs=pl.BlockSpec((T, cb), lambda i: (0, i)), out_shape=jax.ShapeDtypeStruct((T, C), F32),
                  scratch_shapes=[pltpu.VMEM((T + SEQ_PAD, cb), F32)],
                  compiler_params=_params(("parallel",)))(x, w, bias)


def _dwconv_bwd(name, dc, x, w, T):
    W, C = w.shape
    cb = min(C, LANES)
    tc = _tile(T, 128)

    def body(dc_ref, x_ref, w_ref, dx_ref, dw_ref, xp_ref, dp_ref, acc_ref):
        xp_ref[pl.ds(0, SEQ_PAD), :] = jnp.zeros((SEQ_PAD, cb), F32)
        dp_ref[pl.ds(T, SEQ_PAD), :] = jnp.zeros((SEQ_PAD, cb), F32)
        acc_ref[...] = jnp.zeros((W * 8, cb), F32)

        def fill(c, _):
            base = pl.multiple_of(c * tc, tc)
            xp_ref[pl.ds(base + SEQ_PAD, tc), :] = x_ref[pl.ds(base, tc), :]
            dp_ref[pl.ds(base, tc), :] = dc_ref[pl.ds(base, tc), :]
            return 0

        lax.fori_loop(0, T // tc, fill, 0)

        def chunk(c, _):
            base = pl.multiple_of(c * tc, tc)
            d = dc_ref[pl.ds(base, tc), :]
            acc = jnp.zeros((tc, cb), F32)
            for k in range(W):
                acc = acc + w_ref[pl.ds(k, 1), :] * dp_ref[pl.ds(base + (W - 1) - k, tc), :]
                prod = d * xp_ref[pl.ds(base + SEQ_PAD - (W - 1) + k, tc), :]
                acc_ref[pl.ds(8 * k, 8), :] += jnp.sum(prod.reshape(tc // 8, 8, cb), axis=0)
            dx_ref[pl.ds(base, tc), :] = acc
            return 0

        lax.fori_loop(0, T // tc, chunk, 0)
        for k in range(W):
            dw_ref[pl.ds(k, 1), :] = jnp.sum(acc_ref[pl.ds(8 * k, 8), :], axis=0, keepdims=True)

    return _pcall(body, name=name, grid=(C // cb,),
                  in_specs=[pl.BlockSpec((T, cb), lambda i: (0, i)), pl.BlockSpec((T, cb), lambda i: (0, i)),
                            pl.BlockSpec((W, cb), lambda i: (0, i))],
                  out_specs=[pl.BlockSpec((T, cb), lambda i: (0, i)), pl.BlockSpec((W, cb), lambda i: (0, i))],
                  out_shape=[jax.ShapeDtypeStruct((T, C), F32), jax.ShapeDtypeStruct((W, C), F32)],
                  scratch_shapes=[pltpu.VMEM((T + SEQ_PAD, cb), F32), pltpu.VMEM((T + SEQ_PAD, cb), F32),
                                  pltpu.VMEM((W * 8, cb), F32)],
                  compiler_params=_params(("parallel",)))(dc, x, w)


def _pool_count(base, tc, cg, win):
    t = (lax.broadcasted_iota(jnp.int32, (tc, cg), 0) + base + 1).astype(F32)
    return jnp.minimum(t, float(win))


def _pool_fwd(name, u, T):
    D = u.shape[1]
    G = len(POOL_WINDOWS)
    cg = min(D // G, LANES)
    nb = (D // G) // cg
    tc = _tile(T, 128)

    def body(u_ref, o_ref, up_ref):
        g = pl.program_id(0)
        up_ref[pl.ds(0, SEQ_PAD), :] = jnp.zeros((SEQ_PAD, cg), F32)

        def fill(c, _):
            base = pl.multiple_of(c * tc, tc)
            up_ref[pl.ds(base + SEQ_PAD, tc), :] = u_ref[pl.ds(base, tc), :]
            return 0

        lax.fori_loop(0, T // tc, fill, 0)
        for gi, win in enumerate(POOL_WINDOWS):
            @pl.when(g == gi)
            def _(win=win):
                def chunk(c, _):
                    base = pl.multiple_of(c * tc, tc)
                    acc = up_ref[pl.ds(base + SEQ_PAD, tc), :]
                    for j in range(1, win):
                        acc = acc + up_ref[pl.ds(base + SEQ_PAD - j, tc), :]
                    mixed = acc / _pool_count(base, tc, cg, win) - u_ref[pl.ds(base, tc), :]
                    o_ref[pl.ds(base, tc), :] = mixed.astype(BF16)
                    return 0

                lax.fori_loop(0, T // tc, chunk, 0)

    return _pcall(body, name=name, grid=(G, nb), in_specs=[pl.BlockSpec((T, cg), lambda g, i: (0, g * nb + i))],
                  out_specs=pl.BlockSpec((T, cg), lambda g, i: (0, g * nb + i)), out_shape=jax.ShapeDtypeStruct((T, D), BF16),
                  scratch_shapes=[pltpu.VMEM((T + SEQ_PAD, cg), F32)], compiler_params=_params(("parallel", "parallel")))(u)


def _pool_bwd(name, dm, T):
    D = dm.shape[1]
    G = len(POOL_WINDOWS)
    cg = min(D // G, LANES)
    nb = (D // G) // cg
    tc = _tile(T, 128)

    def body(d_ref, o_ref, qp_ref):
        g = pl.program_id(0)
        qp_ref[pl.ds(T, SEQ_PAD), :] = jnp.zeros((SEQ_PAD, cg), F32)
        for gi, win in enumerate(POOL_WINDOWS):
            @pl.when(g == gi)
            def _(win=win):
                def fill(c, _):
                    base = pl.multiple_of(c * tc, tc)
                    qp_ref[pl.ds(base, tc), :] = d_ref[pl.ds(base, tc), :] / _pool_count(base, tc, cg, win)
                    return 0

                lax.fori_loop(0, T // tc, fill, 0)

                def chunk(c, _):
                    base = pl.multiple_of(c * tc, tc)
                    acc = qp_ref[pl.ds(base, tc), :]
                    for j in range(1, win):
                        acc = acc + qp_ref[pl.ds(base + j, tc), :]
                    o_ref[pl.ds(base, tc), :] = (acc - d_ref[pl.ds(base, tc), :]).astype(BF16)
                    return 0

                lax.fori_loop(0, T // tc, chunk, 0)

    return _pcall(body, name=name, grid=(G, nb), in_specs=[pl.BlockSpec((T, cg), lambda g, i: (0, g * nb + i))],
                  out_specs=pl.BlockSpec((T, cg), lambda g, i: (0, g * nb + i)), out_shape=jax.ShapeDtypeStruct((T, D), BF16),
                  scratch_shapes=[pltpu.VMEM((T + SEQ_PAD, cg), F32)], compiler_params=_params(("parallel", "parallel")))(dm)


def _pool_mm(name, mode, a, b, T, scale=None):
    G = len(POOL_WINDOWS)
    D = a.shape[1]
    cg = D // G
    tm = _tile(T, 2048)
    if mode == "tn":
        ins = [(a, _spec((tm, cg), lambda i, j, k: (k, j))), (b, _spec((tm, cg), lambda i, j, k: (k, j)))]
        outs = [((G, cg, cg), F32, _spec((None, cg, cg), lambda i, j, k: (j, 0, 0)))]
        return _mm(name, "tn", (1, G, T // tm), ins, outs, lambda acc: (acc,))[0]
    ins = [(a, _spec((tm, cg), lambda i, j, k: (i, j))), (b, _spec((None, cg, cg), lambda i, j, k: (j, 0, 0)))]
    if scale is not None:
        ins.append((scale, _spec((1, cg), lambda i, j, k: (0, j))))
        outs = [((T, D), BF16, _spec((tm, cg), lambda i, j, k: (i, j))), ((T, D), BF16, _spec((tm, cg), lambda i, j, k: (i, j)))]
        return _mm(name, mode, (T // tm, G, 1), ins, outs, lambda acc, s: (acc * s, acc))
    outs = [((T, D), F32, _spec((tm, cg), lambda i, j, k: (i, j)))]
    return _mm(name, mode, (T // tm, G, 1), ins, outs)[0]


def _sc_mid_fwd(name, z, w, T):
    D = w.shape[1]
    W = w.shape[0]
    cb = min(D, LANES)
    nb = D // cb
    tc = _tile(T, 256)

    def body(b_ref, c_ref, v_ref, w_ref, o_ref, pp_ref):
        pp_ref[pl.ds(0, SEQ_PAD), :] = jnp.zeros((SEQ_PAD, cb), F32)

        def fill(c, _):
            base = pl.multiple_of(c * tc, tc)
            pp_ref[pl.ds(base + SEQ_PAD, tc), :] = c_ref[pl.ds(base, tc), :] * v_ref[pl.ds(base, tc), :]
            return 0

        lax.fori_loop(0, T // tc, fill, 0)

        def chunk(c, _):
            base = pl.multiple_of(c * tc, tc)
            q = jnp.zeros((tc, cb), F32)
            for k in range(W):
                q = q + w_ref[pl.ds(k, 1), :] * pp_ref[pl.ds(base + SEQ_PAD - (W - 1) + k, tc), :]
            o_ref[pl.ds(base, tc), :] = (b_ref[pl.ds(base, tc), :] * q).astype(BF16)
            return 0

        lax.fori_loop(0, T // tc, chunk, 0)

    return _pcall(body, name=name, grid=(nb,),
                  in_specs=[pl.BlockSpec((T, cb), lambda i: (0, i)), pl.BlockSpec((T, cb), lambda i: (0, nb + i)),
                            pl.BlockSpec((T, cb), lambda i: (0, 2 * nb + i)), pl.BlockSpec((W, cb), lambda i: (0, i))],
                  out_specs=pl.BlockSpec((T, cb), lambda i: (0, i)), out_shape=jax.ShapeDtypeStruct((T, D), BF16),
                  scratch_shapes=[pltpu.VMEM((T + SEQ_PAD, cb), F32)],
                  compiler_params=_params(("parallel",)))(z, z, z, w)


def _sc_mid_bwd(name, z, dm, w, T):
    D = w.shape[1]
    W = w.shape[0]
    cb = min(D, LANES)
    nb = D // cb
    tc = _tile(T, 128)

    def body(b_ref, c_ref, v_ref, d_ref, w_ref, dz_ref, dw_ref, pp_ref, dq_ref, acc_ref):
        part = pl.program_id(0)
        pp_ref[pl.ds(0, SEQ_PAD), :] = jnp.zeros((SEQ_PAD, cb), F32)
        dq_ref[pl.ds(T, SEQ_PAD), :] = jnp.zeros((SEQ_PAD, cb), F32)
        acc_ref[...] = jnp.zeros((W * 8, cb), F32)

        def fill(c, _):
            base = pl.multiple_of(c * tc, tc)
            pp_ref[pl.ds(base + SEQ_PAD, tc), :] = c_ref[pl.ds(base, tc), :] * v_ref[pl.ds(base, tc), :]
            dq_ref[pl.ds(base, tc), :] = d_ref[pl.ds(base, tc), :] * b_ref[pl.ds(base, tc), :]
            return 0

        lax.fori_loop(0, T // tc, fill, 0)

        @pl.when(part == 0)
        def _():
            def chunk(c, _):
                base = pl.multiple_of(c * tc, tc)
                dq = dq_ref[pl.ds(base, tc), :]
                q = jnp.zeros((tc, cb), F32)
                for k in range(W):
                    p = pp_ref[pl.ds(base + SEQ_PAD - (W - 1) + k, tc), :]
                    q = q + w_ref[pl.ds(k, 1), :] * p
                    acc_ref[pl.ds(8 * k, 8), :] += jnp.sum((dq * p).reshape(tc // 8, 8, cb), axis=0)
                dz_ref[pl.ds(base, tc), :] = (d_ref[pl.ds(base, tc), :] * q).astype(BF16)
                return 0

            lax.fori_loop(0, T // tc, chunk, 0)

        @pl.when(part > 0)
        def _():
            def chunk(c, _):
                base = pl.multiple_of(c * tc, tc)
                dp = jnp.zeros((tc, cb), F32)
                for k in range(W):
                    dp = dp + w_ref[pl.ds(k, 1), :] * dq_ref[pl.ds(base + (W - 1) - k, tc), :]
                other = jnp.where(part == 1, v_ref[pl.ds(base, tc), :], c_ref[pl.ds(base, tc), :])
                dz_ref[pl.ds(base, tc), :] = (dp * other).astype(BF16)
                return 0

            lax.fori_loop(0, T // tc, chunk, 0)

        for k in range(W):
            dw_ref[pl.ds(k, 1), :] = jnp.sum(acc_ref[pl.ds(8 * k, 8), :], axis=0, keepdims=True)

    dz, dw3 = _pcall(body, name=name, grid=(3, nb),
                     in_specs=[pl.BlockSpec((T, cb), lambda p, i: (0, i)), pl.BlockSpec((T, cb), lambda p, i: (0, nb + i)),
                               pl.BlockSpec((T, cb), lambda p, i: (0, 2 * nb + i)), pl.BlockSpec((T, cb), lambda p, i: (0, i)),
                               pl.BlockSpec((W, cb), lambda p, i: (0, i))],
                     out_specs=[pl.BlockSpec((T, cb), lambda p, i: (0, p * nb + i)),
                                pl.BlockSpec((None, W, cb), lambda p, i: (p, 0, i))],
                     out_shape=[jax.ShapeDtypeStruct((T, 3 * D), BF16), jax.ShapeDtypeStruct((3, W, D), F32)],
                     scratch_shapes=[pltpu.VMEM((T + SEQ_PAD, cb), F32), pltpu.VMEM((T + SEQ_PAD, cb), F32),
                                     pltpu.VMEM((W * 8, cb), F32)],
                     compiler_params=_params(("parallel", "parallel")))(z, z, z, dm, w)
    return dz, dw3


def _ret_tables(dk, dv):
    C = RET_CHUNK
    lg = np.log1p(-np.exp2(-5.0 - np.arange(RET_HEADS, dtype=np.float64)))
    idx = np.arange(C, dtype=np.float64)
    rel = idx[:, None] - idx[None, :]
    mask = np.where(rel >= 0, np.exp(lg[:, None, None] * np.maximum(rel, 0.0)), 0.0)
    qd = np.exp(lg[:, None] * (idx + 1.0))[:, :, None] * np.ones((1, 1, dk))
    kd = np.exp(lg[:, None] * (C - 1.0 - idx))[:, :, None] * np.ones((1, 1, dk))
    cd = np.exp(lg * C)[:, None, None] * np.ones((1, 1, dv))
    return tuple(jnp.asarray(a, F32) for a in (mask, qd, kd, cd))


def _ret_fwd(name, q, k, v, tabs, T):
    H, C = RET_HEADS, RET_CHUNK
    dk, dv = q.shape[1] // H, v.shape[1] // H
    nch = T // C
    mask, qd, kd, cd = tabs

    def body(q_ref, k_ref, v_ref, m_ref, qd_ref, kd_ref, cd_ref, o_ref, st_ref, s_ref):
        n = pl.program_id(1)

        @pl.when(n == 0)
        def _():
            s_ref[...] = jnp.zeros((dk, dv), F32)

        qv, kv, vv = q_ref[...], k_ref[...], v_ref[...]
        s = s_ref[...]
        sb = s.astype(BF16)
        st_ref[...] = sb
        scores = lax.dot_general(qv, kv, NT, preferred_element_type=F32) * m_ref[...]
        intra = lax.dot_general(scores.astype(BF16), vv, NN, preferred_element_type=F32)
        cross = lax.dot_general((qv.astype(F32) * qd_ref[...]).astype(BF16), sb, NN, preferred_element_type=F32)
        o_ref[...] = intra + cross
        upd = lax.dot_general((kv.astype(F32) * kd_ref[...]).astype(BF16), vv, TN, preferred_element_type=F32)
        s_ref[...] = s * cd_ref[...] + upd

    return _pcall(body, name=name, grid=(H, nch),
                  in_specs=[pl.BlockSpec((C, dk), lambda h, n: (n, h)), pl.BlockSpec((C, dk), lambda h, n: (n, h)),
                            pl.BlockSpec((C, dv), lambda h, n: (n, h)), pl.BlockSpec((None, C, C), lambda h, n: (h, 0, 0)),
                            pl.BlockSpec((None, C, dk), lambda h, n: (h, 0, 0)), pl.BlockSpec((None, C, dk), lambda h, n: (h, 0, 0)),
                            pl.BlockSpec((None, 1, dv), lambda h, n: (h, 0, 0))],
                  out_specs=[pl.BlockSpec((C, dv), lambda h, n: (n, h)),
                             pl.BlockSpec((None, None, dk, dv), lambda h, n: (h, n, 0, 0))],
                  out_shape=[jax.ShapeDtypeStruct((T, H * dv), F32), jax.ShapeDtypeStruct((H, nch, dk, dv), BF16)],
                  scratch_shapes=[pltpu.VMEM((dk, dv), F32)],
                  compiler_params=_params(("parallel", "arbitrary")))(q, k, v, mask, qd, kd, cd)


def _ret_bwd(name, q, k, v, do, st, tabs, T):
    H, C = RET_HEADS, RET_CHUNK
    dk, dv = q.shape[1] // H, v.shape[1] // H
    nch = T // C
    mask, qd, kd, cd = tabs

    def body(q_ref, k_ref, v_ref, do_ref, st_ref, m_ref, qd_ref, kd_ref, cd_ref, dq_ref, dk_ref, dv_ref, ds_ref):
        n = pl.program_id(1)

        @pl.when(n == 0)
        def _():
            ds_ref[...] = jnp.zeros((dk, dv), F32)

        qv, kv, vv, dov, sb = q_ref[...], k_ref[...], v_ref[...], do_ref[...], st_ref[...]
        m = m_ref[...]
        ds = ds_ref[...]
        dsb = ds.astype(BF16)
        a = (lax.dot_general(qv, kv, NT, preferred_element_type=F32) * m).astype(BF16)
        dp = (lax.dot_general(dov, vv, NT, preferred_element_type=F32) * m).astype(BF16)
        qa = (qv.astype(F32) * qd_ref[...]).astype(BF16)
        kb = (kv.astype(F32) * kd_ref[...]).astype(BF16)
        dq_ref[...] = (lax.dot_general(dp, kv, NN, preferred_element_type=F32)
                       + qd_ref[...] * lax.dot_general(dov, sb, NT, preferred_element_type=F32))
        dk_ref[...] = (lax.dot_general(dp, qv, TN, preferred_element_type=F32)
                       + kd_ref[...] * lax.dot_general(vv, dsb, NT, preferred_element_type=F32))
        dv_ref[...] = (lax.dot_general(a, dov, TN, preferred_element_type=F32)
                       + lax.dot_general(kb, dsb, NN, preferred_element_type=F32)).astype(BF16)
        ds_ref[...] = ds * cd_ref[...] + lax.dot_general(qa, dov, TN, preferred_element_type=F32)

    rev = lambda h, n: (nch - 1 - n, h)
    return _pcall(body, name=name, grid=(H, nch),
                  in_specs=[pl.BlockSpec((C, dk), rev), pl.BlockSpec((C, dk), rev), pl.BlockSpec((C, dv), rev),
                            pl.BlockSpec((C, dv), rev), pl.BlockSpec((None, None, dk, dv), lambda h, n: (h, nch - 1 - n, 0, 0)),
                            pl.BlockSpec((None, C, C), lambda h, n: (h, 0, 0)), pl.BlockSpec((None, C, dk), lambda h, n: (h, 0, 0)),
                            pl.BlockSpec((None, C, dk), lambda h, n: (h, 0, 0)), pl.BlockSpec((None, 1, dv), lambda h, n: (h, 0, 0))],
                  out_specs=[pl.BlockSpec((C, dk), rev), pl.BlockSpec((C, dk), rev), pl.BlockSpec((C, dv), rev)],
                  out_shape=[jax.ShapeDtypeStruct((T, H * dk), F32), jax.ShapeDtypeStruct((T, H * dk), F32),
                             jax.ShapeDtypeStruct((T, H * dv), BF16)],
                  scratch_shapes=[pltpu.VMEM((dk, dv), F32)],
                  compiler_params=_params(("parallel", "arbitrary")))(q, k, v, do, st, mask, qd, kd, cd)


def _local_step(x, pos, tgt, ng, get_w, put_g, first_deps, T, D):
    tr = _tile(T, 256)
    trw = _tile(T, 128)
    row = lambda i: ng[i:i + 1]

    u = _rowwise("rms_first", lambda xv, gv: _rms(xv, gv), [(x, D, 0), (row(0), None, None)], [("row", D, BF16)], T, tr,
                 deps=first_deps)[0]
    h = x
    saved = []
    cos = sin = None
    tabs = None
    for l in range(DEPTH):
        mixer = l % 4
        wts = get_w(l, "mix", h)
        sv = {"h_in": h, "u": u}
        if mixer == 0:
            z = _mm_cols("conv_in", u, wts["conv_w_in"], (), T, lambda acc, b: (acc + b,), (wts["conv_b_in"],))[0]
            v = _rowwise("conv_glu", _glu, [(z, D, 0), (z, D, 1)], [("row", D, F32)], T, tr)[0]
            c = _dwconv_fwd("conv_dw", v, wts["conv_dw"], wts["conv_dw_b"], T)
            s = _rowwise("conv_ln", _ln_silu, [(c, D, 0), (wts["conv_ln_g"], None, None), (wts["conv_ln_b"], None, None)],
                         [("row", D, BF16)], T, tr)[0]
            y = _mm_rows("conv_out", s, wts["conv_w_out"], T, lambda acc, b: (acc + b,), (wts["conv_b_out"],), BF16)
            sv.update(z=z, v=v, c=c, s=s)
        elif mixer == 1:
            mixed = _pool_fwd("pool_fwd", u, T)
            y, ycat = _pool_mm("pool_mm", "nn", mixed, wts["pool_w"], T, wts["pool_scale"])
            sv.update(mixed=mixed, ycat=ycat)
        elif mixer == 2:
            z = _mm_cols("sc_in", u, wts["sc_w_in"], (), T)[0]
            m = _sc_mid_fwd("sc_mid", z, wts["sc_dw"], T)
            y = _mm_rows("sc_out", m, wts["sc_w_out"], T, out_dtype=BF16)
            sv.update(z=z, m=m)
        else:
            dk = D // RET_HEADS
            half = dk // 2
            inv_freq = jnp.asarray((ROPE_BASE ** (-np.arange(half, dtype=np.float64) / half)).astype(np.float32)).reshape(1, half)

            def _cs(p, f):
                ang = p * f
                return jnp.cos(ang), jnp.sin(ang)

            cos, sin = _rowwise("rope_tab", _cs, [(pos, 1, 0), (inv_freq, None, None)],
                                [("row", half, F32), ("row", half, F32)], T, tr)
            tabs = _ret_tables(dk, 2 * dk)
            z = _mm_cols("ret_in", u, wts["ret_w_in"], (), T)[0]

            def _pre(zq, zk, zv, cv, sn):
                return _rot(zq, cv, sn, 1.0), _rot(zk, cv, sn, 1.0) * (dk ** -0.5), zv

            qr, kr, vb = _rowwise("ret_pre", _pre, [(z, D, 0), (z, D, 1), (z, 2 * D, 1), (cos, half, 0), (sin, half, 0)],
                                  [("row", D, BF16), ("row", D, BF16), ("row", 2 * D, BF16)], T, trw)
            o, st = _ret_fwd("ret_fwd", qr, kr, vb, tabs, T)
            out = _rowwise("ret_post", _gate_norm, [(o, 2 * D, 0), (z, 2 * D, 2)], [("row", 2 * D, BF16)], T, trw)[0]
            y = _mm_rows("ret_out", out, wts["ret_w_out"], T, out_dtype=BF16)
            sv.update(z=z, qr=qr, kr=kr, vb=vb, o=o, st=st, out=out)
        sv["y1"] = y
        h, u = _rowwise("bound_a", lambda hv, yv, gp, gn: (lambda hn: (hn, _rms(hn, gn)))(hv + _rms(yv.astype(F32), gp)),
                        [(h, D, 0), (y, D, 0), (row(4 * l + 1), None, None), (row(4 * l + 2), None, None)],
                        [("row", D, F32), ("row", D, BF16)], T, tr)
        sv["h_mid"], sv["u2"] = h, u
        wm = get_w(l, "mlp", h)
        mlp_deps = wm.pop("_deps", [])
        wts = dict(wts, **wm)
        sv["w"] = wts
        r = _mm_cols("mlp_up", u, wts["mlp_up"], (), T, lambda acc: (jnp.maximum(acc, 0.0),), (), 1, BF16, deps=mlp_deps)[0]
        y = _mm_whole_k("mlp_down", r, wts["mlp_down"], T, "rows", pre=_square, out_dtype=BF16)
        sv.update(r=r, y2=y)
        if l + 1 < DEPTH:
            udt = F32 if (l + 1) % 4 == 1 else BF16
            h, u = _rowwise("bound_b", lambda hv, yv, gp, gn: (lambda hn: (hn, _rms(hn, gn)))(hv + _rms(yv.astype(F32), gp)),
                            [(h, D, 0), (y, D, 0), (row(4 * l + 3), None, None), (row(4 * l + 4), None, None)],
                            [("row", D, F32), ("row", D, udt)], T, tr)
        saved.append(sv)

    def _final(hv, yv, gp, tv):
        hn = hv + _rms(yv.astype(F32), gp)
        e = hn - tv
        return e * (1.0 / D), _colsum(e * e)

    dh, loss_row = _rowwise("final", _final, [(h, D, 0), (y, D, 0), (row(4 * DEPTH - 1), None, None), (tgt, D, 0)],
                            [("row", D, F32), ("acc", (1, D))], T, tr)

    gn = [None] * (4 * DEPTH)
    F = saved[0]["r"].shape[1]
    du_next = None
    tok = None
    for l in reversed(range(DEPTH)):
        sv = saved[l]
        wts = sv["w"]
        g = {}
        deps = [] if tok is None else [tok]
        mixer = l % 4
        if du_next is None:
            def _bb(dhv, yv, gp):
                _, vjp = jax.vjp(_rms, yv.astype(F32), gp)
                dy, dgp = vjp(dhv)
                return dy, dgp

            dy, gn[4 * l + 3] = _rowwise("bwd_last", _bb, [(dh, D, 0), (sv["y2"], D, 0), (row(4 * l + 3), None, None)],
                                         [("row", D, BF16), ("acc", (1, D))], T, tr, deps)
        else:
            dh, dy, gn[4 * l + 3], gn[4 * l + 4], _ = _bound_bwd("bwd_b", dh, du_next, saved[l + 1]["h_in"], sv["y2"],
                                                              row(4 * l + 3), row(4 * l + 4), T, D, tr, deps)
        F8 = F // N_DEV
        tm = _tile(T, 2048)
        da = _mm("mlp_dhid", "nt", (T // tm, N_DEV, 1),
                 [(dy, _spec((tm, D), lambda i, j, k: (i, 0))),
                  (wts["mlp_down"], _spec((None, F8, D), lambda i, j, k: (j, 0, 0))),
                  (sv["r"], _spec((tm, F8), lambda i, j, k: (i, j)))],
                 [((T, F), BF16, _spec((tm, F8), lambda i, j, k: (i, j)))],
                 lambda acc, rr: (acc * (2.0 * rr.astype(F32)),))[0]
        tk = _tile(T, 2048)
        g["mlp_down"] = _mm("mlp_dwdown", "tn", (1, N_DEV, T // tk),
                            [(sv["r"], _spec((tk, F8), lambda i, j, k: (k, j))), (dy, _spec((tk, D), lambda i, j, k: (k, 0)))],
                            [(wts["mlp_down"].shape, BF16, _spec((None, F8, D), lambda i, j, k: (j, 0, 0)))],
                            lambda acc: (acc,), pre=_square)[0]
        g["mlp_up"] = _mm_cols_grad("mlp_dwup", sv["u2"], da, wts["mlp_up"].shape, (), T)
        du = _mm_whole_k("mlp_du", da, wts["mlp_up"], T, "cols", out_dtype=BF16)
        tok = put_g(l, "mlp", g, du, None)
        g = {}
        dh, dy, gn[4 * l + 1], gn[4 * l + 2], dy_sum = _bound_bwd("bwd_a", dh, du, sv["h_mid"], sv["y1"],
                                                                row(4 * l + 1), row(4 * l + 2), T, D, tr, [] if tok is None else [tok])
        u = sv["u"]
        if mixer == 0:
            g["conv_b_out"] = dy_sum
            ds = _mm_rows_t("conv_ds", dy, wts["conv_w_out"], T)
            g["conv_w_out"] = _mm_rows_grad("conv_dwout", sv["s"], dy, T)

            def _lnb(cv, gv, bv, dsv):
                _, vjp = jax.vjp(_ln_silu, cv, gv, bv)
                dc, dg, db = vjp(dsv)
                return dc, dg, db, _colsum(dc)

            dc, g["conv_ln_g"], g["conv_ln_b"], g["conv_dw_b"] = _rowwise(
                "conv_ln_bwd", _lnb, [(sv["c"], D, 0), (wts["conv_ln_g"], None, None), (wts["conv_ln_b"], None, None), (ds, D, 0)],
                [("row", D, F32), ("acc", (1, D)), ("acc", (1, D)), ("acc", (1, D))], T, tr)
            dv, g["conv_dw"] = _dwconv_bwd("conv_dw_bwd", dc, sv["v"], wts["conv_dw"], T)

            def _glub(za, zg, dvv):
                _, vjp = jax.vjp(_glu, za, zg)
                dza, dzg = vjp(dvv)
                dz = jnp.concatenate([dza, dzg], axis=-1)
                return dz, _colsum(dz)

            dz, g["conv_b_in"] = _rowwise("conv_glu_bwd", _glub, [(sv["z"], D, 0), (sv["z"], D, 1), (dv, D, 0)],
                                          [("row", 2 * D, BF16), ("acc", (1, 2 * D))], T, tr)
            g["conv_w_in"] = _mm_cols_grad("conv_dwin", u, dz, wts["conv_w_in"].shape, (), T)
            du = _mm_cols_t("conv_du", dz, wts["conv_w_in"], (), T, out_dtype=BF16)
        elif mixer == 1:
            def _psb(dyv, yc, sc):
                return dyv * sc, _colsum(dyv.astype(F32) * yc.astype(F32))

            dyg, g["pool_scale"] = _rowwise("pool_scale_bwd", _psb, [(dy, D, 0), (sv["ycat"], D, 0), (wts["pool_scale"], None, None)],
                                            [("row", D, BF16), ("acc", (1, D))], T, tr)
            dmix = _pool_mm("pool_dmix", "nt", dyg, wts["pool_w"], T)
            g["pool_w"] = _pool_mm("pool_dw", "tn", sv["mixed"], dyg, T)
            du = _pool_bwd("pool_bwd", dmix, T)
        elif mixer == 2:
            dm = _mm_rows_t("sc_dm", dy, wts["sc_w_out"], T)
            g["sc_w_out"] = _mm_rows_grad("sc_dwout", sv["m"], dy, T)
            dz, dw3 = _sc_mid_bwd("sc_mid_bwd", sv["z"], dm, wts["sc_dw"], T)
            g["sc_dw"] = dw3[0]
            g["sc_w_in"] = _mm_cols_grad("sc_dwin", u, dz, wts["sc_w_in"].shape, (), T)
            du = _mm_cols_t("sc_du", dz, wts["sc_w_in"], (), T, out_dtype=BF16)
        else:
            dk = D // RET_HEADS
            half = dk // 2
            dout = _mm_rows_t("ret_dout", dy, wts["ret_w_out"], T)
            g["ret_w_out"] = _mm_rows_grad("ret_dwout", sv["out"], dy, T)

            def _postb(ov, gv, dv_):
                _, vjp = jax.vjp(_gate_norm, ov, gv)
                return vjp(dv_)

            do, dzg = _rowwise("ret_post_bwd", _postb, [(sv["o"], 2 * D, 0), (sv["z"], 2 * D, 2), (dout, 2 * D, 0)],
                               [("row", 2 * D, BF16), ("row", 2 * D, BF16)], T, trw)
            dqr, dkr, dvb = _ret_bwd("ret_bwd", sv["qr"], sv["kr"], sv["vb"], do, sv["st"], tabs, T)

            def _preb(dq, dk_, dvv, dgv, cv, sn):
                return jnp.concatenate([_rot(dq, cv, sn, -1.0).astype(BF16), (_rot(dk_, cv, sn, -1.0) * (dk ** -0.5)).astype(BF16),
                                        dvv, dgv], axis=-1)

            dz = _rowwise("ret_pre_bwd", _preb, [(dqr, D, 0), (dkr, D, 0), (dvb, 2 * D, 0), (dzg, 2 * D, 0), (cos, half, 0), (sin, half, 0)],
                          [("row", 6 * D, BF16)], T, trw)[0]
            g["ret_w_in"] = _mm_cols_grad("ret_dwin", u, dz, wts["ret_w_in"].shape, (), T)
            du = _mm_cols_t("ret_du", dz, wts["ret_w_in"], (), T, out_dtype=BF16)
        du_next = du
        if l > 0:
            tok = put_g(l, "mix", g, du, None)

    def _fb(dhv, duv, xv, gv):
        _, vjp = jax.vjp(_rms, xv, gv)
        dx, dg = vjp(duv.astype(F32))
        return dhv + dx, dg

    grad_x, gn[0] = _rowwise("bwd_first", _fb, [(dh, D, 0), (du_next, D, 0), (x, D, 0), (row(0), None, None)],
                             [("row", D, F32), ("acc", (1, D))], T, tr)
    put_g(0, "mix", g, grad_x, jnp.concatenate(gn, axis=0))
    return loss_row, grad_x


def _bound_bwd(name, dh, du, h_new, y, g_post, g_next, T, D, tr, deps=()):
    def fn(dhv, duv, hn, yv, gp, gnx):
        _, vjp1 = jax.vjp(_rms, hn, gnx)
        d1, dgn = vjp1(duv.astype(F32))
        dht = dhv + d1
        _, vjp2 = jax.vjp(_rms, yv.astype(F32), gp)
        dy, dgp = vjp2(dht)
        return dht, dy, dgp, dgn, _colsum(dy)

    return _rowwise(name, fn, [(dh, D, 0), (du, D, 0), (h_new, D, 0), (y, D, 0), (g_post, None, None), (g_next, None, None)],
                    [("row", D, F32), ("row", D, BF16), ("acc", (1, D)), ("acc", (1, D)), ("acc", (1, D))], T, tr, deps)


def _adamw(name, parts, w, m, v, own=None):
    P, R, C = parts.shape
    tr = _tile(R, 256)

    def body(*refs):
        p_ref = refs[0]
        w_ref, m_ref, v_ref, g_ref, d_ref, m2_ref, v2_ref = refs[-7:]

        def slot(i):
            if own is None:
                return p_ref[i].astype(F32)
            here = 2 * lax.axis_index("x") + lax.axis_index("y") == i
            return jnp.where(here, refs[1][i], p_ref[i]).astype(F32)

        g = slot(0)
        for i in range(1, P):
            g = g + slot(i)
        wv = w_ref[...]
        m2 = ADAM_B1 * m_ref[...] + (1.0 - ADAM_B1) * g
        v2 = ADAM_B2 * v_ref[...] + (1.0 - ADAM_B2) * (g * g)
        m_hat = m2 / (1.0 - ADAM_B1 ** ADAM_STEP)
        v_hat = v2 / (1.0 - ADAM_B2 ** ADAM_STEP)
        g_ref[...] = g
        d_ref[...] = -ADAM_LR * (m_hat / (jnp.sqrt(v_hat) + ADAM_EPS) + ADAM_WD * wv)
        m2_ref[...] = m2
        v2_ref[...] = v2

    blk = pl.BlockSpec((tr, C), lambda i: (i, 0))
    pblk = pl.BlockSpec((P, tr, C), lambda i: (0, i, 0))
    stacks = [parts] if own is None else [parts, own]
    return _pcall(body, name=name, grid=(R // tr,),
                  in_specs=[pblk] * len(stacks) + [blk, blk, blk], out_specs=[blk] * 4,
                  out_shape=[jax.ShapeDtypeStruct((R, C), F32)] * 4, compiler_params=_params(("parallel",)))(*stacks, w, m, v)


def _pair_add(name, a, b):
    Q, _, R, C = a.shape
    tr = _tile(R, 1024)
    blk = pl.BlockSpec((None, tr, C), lambda q, i: (q, i, 0))

    def body(a_ref, b_ref, o_ref):
        mine = jnp.where(lax.axis_index("c") == 0, a_ref[0], a_ref[1])
        o_ref[...] = (mine.astype(F32) + b_ref[...].astype(F32)).astype(BF16)

    return _pcall(body, name=name, grid=(Q, R // tr),
                  in_specs=[pl.BlockSpec((None, 2, tr, C), lambda q, i: (q, 0, i, 0)), blk], out_specs=blk,
                  out_shape=jax.ShapeDtypeStruct((Q, R, C), BF16), compiler_params=_params(("parallel", "parallel")))(a, b)


ANY = pl.BlockSpec(memory_space=pl.ANY)


def _place():
    x, y, c = lax.axis_index("x"), lax.axis_index("y"), lax.axis_index("c")
    chips = [(1 - x, y), (x, 1 - y), (1 - x, 1 - y)]
    return x, y, c, chips


def _gather_all(locs):
    n = len(locs)

    def body(*refs):
        ins, outs = refs[:n], refs[n:2 * n]
        send, recv, lsem = refs[2 * n:]
        x, y, c, chips = _place()
        me = 4 * x + 2 * y + c
        sib = (x, y, 1 - c)

        def cp(a, k, slot, to, src=None):
            return pltpu.make_async_remote_copy(src_ref=outs[a].at[slot] if src is None else src, dst_ref=outs[a].at[slot],
                                                send_sem=send.at[a, k], recv_sem=recv.at[a, k], device_id=to, device_id_type=MESH)

        mine = [pltpu.make_async_copy(ins[a], outs[a].at[me], lsem.at[a]) for a in range(n)]
        for d in mine:
            d.start()
        first = []
        for a in range(n):
            first.append(cp(a, 0, me, sib, src=ins[a]))
            for j, chip in enumerate(chips):
                first.append(cp(a, 1 + j, me, (chip[0], chip[1], c), src=ins[a]))
        for d in first:
            d.start()
        passed = []
        for j, chip in enumerate(chips):
            slot = 4 * chip[0] + 2 * chip[1] + c
            for a in range(n):
                cp(a, 1 + j, slot, sib).wait_recv()
                p = cp(a, 4 + j, slot, sib)
                p.start()
                passed.append(p)
        for a in range(n):
            cp(a, 0, 4 * x + 2 * y + (1 - c), sib).wait_recv()
        for j, chip in enumerate(chips):
            for a in range(n):
                cp(a, 4 + j, 4 * chip[0] + 2 * chip[1] + (1 - c), sib).wait_recv()
        for d in first + passed:
            d.wait_send()
        for d in mine:
            d.wait()

    return _pcall(body, name="gather_all", in_specs=[ANY] * n, out_specs=[ANY] * n,
                  out_shape=[jax.ShapeDtypeStruct((N_DEV,) + a.shape, a.dtype) for a in locs],
                  scratch_shapes=[pltpu.SemaphoreType.DMA((n, 7)), pltpu.SemaphoreType.DMA((n, 7)), pltpu.SemaphoreType.DMA((n,))],
                  compiler_params=pltpu.CompilerParams(has_side_effects=True))(*locs)


HBM = pl.BlockSpec(memory_space=pltpu.HBM)
SEM = pl.BlockSpec(memory_space=pltpu.SEMAPHORE)
DATAFLOW = pltpu.SideEffectType.DATAFLOW_SIDE_EFFECTING


def _peers():
    x, y, c = lax.axis_index("x"), lax.axis_index("y"), lax.axis_index("c")
    out = []
    for r in range(1, N_DEV):
        px = 1 - x if (r >> 2) & 1 else x
        py = 1 - y if (r >> 1) & 1 else y
        pc = 1 - c if r & 1 else c
        out.append(((px, py, pc), 4 * px + 2 * py + pc))
    return 4 * x + 2 * y + c, out


def _hbm(a):
    return pltpu.with_memory_space_constraint(a, pltpu.HBM)


def _gather_start(name, locs, lands, deps):
    n = len(locs)

    def body(*refs):
        ins, lnd = refs[:n], refs[n:2 * n]
        send, recv = refs[2 * n + len(deps)], refs[2 * n + len(deps) + 1]
        token = refs[-1]
        x, y, c, chips = _place()
        me = 4 * x + 2 * y + c
        for k, to in enumerate([(x, y, 1 - c)] + [(chip[0], chip[1], c) for chip in chips]):
            for a in range(n):
                pltpu.make_async_remote_copy(src_ref=ins[a], dst_ref=lnd[a].at[me], send_sem=send.at[a * 4 + k],
                                             recv_sem=recv.at[a * 4 + k], device_id=to, device_id_type=MESH).start()
        token[...] = jnp.zeros_like(token)

    out_shape = ((pltpu.SemaphoreType.DMA((n * 4,)), pltpu.SemaphoreType.DMA((n * 4,)))
                 + tuple(pltpu.HBM(a.shape, a.dtype) for a in locs) + tuple(pltpu.HBM(a.shape, a.dtype) for a in lands)
                 + (jax.ShapeDtypeStruct((8, LANES), F32),))
    res = _pcall(body, name=name, out_shape=out_shape, in_specs=[HBM] * (2 * n) + [ANY] * len(deps),
                 out_specs=(SEM, SEM) + (HBM,) * (2 * n) + (pl.BlockSpec(memory_space=pltpu.VMEM),),
                 input_output_aliases={a: 2 + a for a in range(2 * n)},
                 compiler_params=pltpu.CompilerParams(has_side_effects=DATAFLOW))(
        *[_hbm(a) for a in locs], *[_hbm(a) for a in lands], *deps)
    return res[0], res[1], res[2:2 + n], res[2 + n:2 + 2 * n], res[-1]


def _gather_pass(name, send, recv, shards, lands, after):
    n = len(shards)

    def body(*refs):
        ins, lnd = refs[:n], refs[n:2 * n]
        recv = refs[2 * n + 1]
        fsend, frecv = refs[2 * n + 3], refs[2 * n + 4]
        token = refs[-1]
        x, y, c, chips = _place()
        for j, chip in enumerate(chips):
            slot = 4 * chip[0] + 2 * chip[1] + c
            for a in range(n):
                pltpu.make_async_remote_copy(src_ref=ins[a], dst_ref=lnd[a].at[slot], send_sem=fsend.at[a * 3 + j],
                                             recv_sem=recv.at[a * 4 + 1 + j], device_id=(chip[0], chip[1], c),
                                             device_id_type=MESH).wait_recv()
                pltpu.make_async_remote_copy(src_ref=lnd[a].at[slot], dst_ref=lnd[a].at[slot], send_sem=fsend.at[a * 3 + j],
                                             recv_sem=frecv.at[a * 3 + j], device_id=(x, y, 1 - c), device_id_type=MESH).start()
        token[...] = jnp.zeros_like(token)

    out_shape = ((pltpu.SemaphoreType.DMA((n * 3,)), pltpu.SemaphoreType.DMA((n * 3,)))
                 + tuple(pltpu.HBM(a.shape, a.dtype) for a in shards) + tuple(pltpu.HBM(a.shape, a.dtype) for a in lands)
                 + (jax.ShapeDtypeStruct((8, LANES), F32),))
    res = _pcall(body, name=name, out_shape=out_shape, in_specs=[HBM] * (2 * n) + [SEM, SEM, ANY],
                 out_specs=(SEM, SEM) + (HBM,) * (2 * n) + (pl.BlockSpec(memory_space=pltpu.VMEM),),
                 input_output_aliases={a: 2 + a for a in range(2 * n)},
                 compiler_params=pltpu.CompilerParams(has_side_effects=DATAFLOW))(*shards, *lands, send, recv, after)
    return res[0], res[1], res[2:2 + n], res[2 + n:2 + 2 * n], res[-1]


def _gather_wait(name, send, recv, fsend, frecv, shards, lands, after):
    n = len(shards)

    def body(*refs):
        ins, lnd = refs[:n], refs[n:2 * n]
        send, recv, fsend, frecv = refs[2 * n:2 * n + 4]
        x, y, c, chips = _place()
        sib = (x, y, 1 - c)
        for a in range(n):
            own = pltpu.make_async_remote_copy(src_ref=ins[a], dst_ref=lnd[a].at[4 * x + 2 * y + (1 - c)], send_sem=send.at[a * 4],
                                               recv_sem=recv.at[a * 4], device_id=sib, device_id_type=MESH)
            own.wait_send()
            own.wait_recv()
            for j, chip in enumerate(chips):
                pltpu.make_async_remote_copy(src_ref=ins[a], dst_ref=lnd[a].at[0], send_sem=send.at[a * 4 + 1 + j],
                                             recv_sem=recv.at[a * 4 + 1 + j], device_id=(chip[0], chip[1], c),
                                             device_id_type=MESH).wait_send()
                fw = pltpu.make_async_remote_copy(src_ref=lnd[a].at[4 * chip[0] + 2 * chip[1] + c],
                                                  dst_ref=lnd[a].at[4 * chip[0] + 2 * chip[1] + (1 - c)], send_sem=fsend.at[a * 3 + j],
                                                  recv_sem=frecv.at[a * 3 + j], device_id=sib, device_id_type=MESH)
                fw.wait_send()
                fw.wait_recv()

    out_shape = tuple(pltpu.HBM(a.shape, a.dtype) for a in shards) + tuple(pltpu.HBM(a.shape, a.dtype) for a in lands)
    res = _pcall(body, name=name, out_shape=out_shape, in_specs=[HBM] * (2 * n) + [SEM, SEM, SEM, SEM, ANY], out_specs=(HBM,) * (2 * n),
                 input_output_aliases={a: a for a in range(2 * n)},
                 compiler_params=pltpu.CompilerParams(has_side_effects=DATAFLOW))(*shards, *lands, send, recv, fsend, frecv, after)
    return res[n:]


def _chip_start(name, cs, lands):
    n = len(cs)

    def body(*refs):
        ins, lnd = refs[:n], refs[n:2 * n]
        send, recv = refs[2 * n], refs[2 * n + 1]
        token = refs[-1]
        x, y, c, chips = _place()
        qme = 2 * x + y
        for j, chip in enumerate(chips):
            for a in range(n):
                pltpu.make_async_remote_copy(src_ref=ins[a].at[2 * chip[0] + chip[1]], dst_ref=lnd[a].at[qme], send_sem=send.at[a * 3 + j],
                                             recv_sem=recv.at[a * 3 + j], device_id=(chip[0], chip[1], c), device_id_type=MESH).start()
        token[...] = jnp.zeros_like(token)

    out_shape = ((pltpu.SemaphoreType.DMA((n * 3,)), pltpu.SemaphoreType.DMA((n * 3,)))
                 + tuple(pltpu.HBM(a.shape, a.dtype) for a in cs) + tuple(pltpu.HBM(a.shape, a.dtype) for a in lands)
                 + (jax.ShapeDtypeStruct((8, LANES), F32),))
    res = _pcall(body, name=name, out_shape=out_shape, in_specs=[HBM] * (2 * n),
                 out_specs=(SEM, SEM) + (HBM,) * (2 * n) + (pl.BlockSpec(memory_space=pltpu.VMEM),),
                 input_output_aliases={a: 2 + a for a in range(2 * n)},
                 compiler_params=pltpu.CompilerParams(has_side_effects=DATAFLOW))(*[_hbm(a) for a in cs], *[_hbm(a) for a in lands])
    return res[0], res[1], res[2:2 + n], res[2 + n:2 + 2 * n], res[-1]


def _chip_wait(name, send, recv, cs, lands, after):
    n = len(cs)

    def body(*refs):
        ins, lnd = refs[:n], refs[n:2 * n]
        send, recv = refs[2 * n], refs[2 * n + 1]
        x, y, c, chips = _place()
        for j, chip in enumerate(chips):
            q = 2 * chip[0] + chip[1]
            for a in range(n):
                cp = pltpu.make_async_remote_copy(src_ref=ins[a].at[q], dst_ref=lnd[a].at[q], send_sem=send.at[a * 3 + j],
                                                  recv_sem=recv.at[a * 3 + j], device_id=(chip[0], chip[1], c), device_id_type=MESH)
                cp.wait_send()
                cp.wait_recv()

    out_shape = tuple(pltpu.HBM(a.shape, a.dtype) for a in cs) + tuple(pltpu.HBM(a.shape, a.dtype) for a in lands)
    res = _pcall(body, name=name, out_shape=out_shape, in_specs=[HBM] * (2 * n) + [SEM, SEM, ANY], out_specs=(HBM,) * (2 * n),
                 input_output_aliases={a: a for a in range(2 * n)},
                 compiler_params=pltpu.CompilerParams(has_side_effects=DATAFLOW))(*cs, *lands, send, recv, after)
    return res[:n], res[n:]


def _pair_exchange(name, gs):
    n = len(gs)

    def body(*refs):
        ins, got = refs[:n], refs[n:2 * n]
        send, recv = refs[2 * n:]
        x, y, c, _ = _place()
        sib = (x, y, 1 - c)
        rem = [pltpu.make_async_remote_copy(src_ref=ins[a].at[q, 1 - c], dst_ref=got[a].at[q], send_sem=send.at[a, q],
                                            recv_sem=recv.at[a, q], device_id=sib, device_id_type=MESH)
               for a in range(n) for q in range(N_CHIP)]
        for d in rem:
            d.start()
        for d in rem:
            d.wait()

    half = [jax.ShapeDtypeStruct((N_CHIP,) + g.shape[2:], g.dtype) for g in gs]
    return _pcall(body, name=name, in_specs=[ANY] * n, out_specs=[ANY] * n, out_shape=half,
                  scratch_shapes=[pltpu.SemaphoreType.DMA((n, N_CHIP)), pltpu.SemaphoreType.DMA((n, N_CHIP))],
                  compiler_params=pltpu.CompilerParams(has_side_effects=True))(*gs)


def _chip_exchange(cs, sm_sh, sm_rep):
    n = len(cs)

    def body(*refs):
        ins, sh, rep = refs[:n], refs[n], refs[n + 1]
        outs, osh, orep = refs[n + 2:2 * n + 2], refs[2 * n + 2], refs[2 * n + 3]
        send, recv, lsem, ssend, srecv, slsem = refs[2 * n + 4:]
        x, y, c, chips = _place()
        qme = 2 * x + y
        me = 4 * x + 2 * y + c
        loc = [pltpu.make_async_copy(ins[a].at[qme], outs[a].at[qme], lsem.at[a]) for a in range(n)]
        loc.append(pltpu.make_async_copy(sh.at[me], osh.at[me], slsem.at[0]))
        loc.append(pltpu.make_async_copy(rep, orep.at[me], slsem.at[1]))
        rem, waits = [], []
        for j, chip in enumerate(chips):
            q = 2 * chip[0] + chip[1]
            to = (chip[0], chip[1], c)
            for a in range(n):
                rem.append(pltpu.make_async_remote_copy(src_ref=ins[a].at[q], dst_ref=outs[a].at[qme], send_sem=send.at[a, j],
                                                        recv_sem=recv.at[a, j], device_id=to, device_id_type=MESH))
                waits.append(pltpu.make_async_remote_copy(src_ref=ins[a].at[q], dst_ref=outs[a].at[q], send_sem=send.at[a, j],
                                                          recv_sem=recv.at[a, j], device_id=to, device_id_type=MESH))
        for r in range(1, N_DEV):
            fx, fy, fc = (r >> 2) & 1, (r >> 1) & 1, r & 1
            px = 1 - x if fx else x
            py = 1 - y if fy else y
            pc = 1 - c if fc else c
            t = 4 * px + 2 * py + pc
            to = (px, py, pc)
            rem.append(pltpu.make_async_remote_copy(src_ref=sh.at[t], dst_ref=osh.at[me], send_sem=ssend.at[0, r - 1],
                                                    recv_sem=srecv.at[0, r - 1], device_id=to, device_id_type=MESH))
            waits.append(pltpu.make_async_remote_copy(src_ref=sh.at[t], dst_ref=osh.at[t], send_sem=ssend.at[0, r - 1],
                                                      recv_sem=srecv.at[0, r - 1], device_id=to, device_id_type=MESH))
            rem.append(pltpu.make_async_remote_copy(src_ref=rep, dst_ref=orep.at[me], send_sem=ssend.at[1, r - 1],
                                                    recv_sem=srecv.at[1, r - 1], device_id=to, device_id_type=MESH))
            waits.append(pltpu.make_async_remote_copy(src_ref=rep, dst_ref=orep.at[t], send_sem=ssend.at[1, r - 1],
                                                      recv_sem=srecv.at[1, r - 1], device_id=to, device_id_type=MESH))
        for d in loc + rem:
            d.start()
        for d in waits:
            d.wait_recv()
        for d in rem:
            d.wait_send()
        for d in loc:
            d.wait()

    out_shape = [jax.ShapeDtypeStruct(a.shape, a.dtype) for a in cs]
    out_shape += [jax.ShapeDtypeStruct(sm_sh.shape, F32), jax.ShapeDtypeStruct((N_DEV,) + sm_rep.shape, F32)]
    res = _pcall(body, name="chip_exchange", in_specs=[ANY] * (n + 2), out_specs=[ANY] * (n + 2), out_shape=out_shape,
                 scratch_shapes=[pltpu.SemaphoreType.DMA((n, 3)), pltpu.SemaphoreType.DMA((n, 3)), pltpu.SemaphoreType.DMA((n,)),
                                 pltpu.SemaphoreType.DMA((2, 7)), pltpu.SemaphoreType.DMA((2, 7)), pltpu.SemaphoreType.DMA((2,))],
                 compiler_params=pltpu.CompilerParams(has_side_effects=True))(*cs, sm_sh, sm_rep)
    return res[:n], res[n], res[n + 1]


BIG = ("mlp_up", "mlp_down", "conv_w_in", "conv_w_out", "pool_w", "sc_w_in", "sc_w_out", "ret_w_in", "ret_w_out")
SMALL_SPLIT = ("norm_g", "conv_dw", "sc_dw")
SMALL_REP = ("conv_b_in", "conv_dw_b", "conv_ln_g", "conv_ln_b", "conv_b_out", "pool_scale")
WEIGHTS = ("norm_g", "mlp_up", "mlp_down", "conv_w_in", "conv_b_in", "conv_dw", "conv_dw_b", "conv_ln_g", "conv_ln_b",
           "conv_w_out", "conv_b_out", "pool_w", "pool_scale", "sc_w_in", "sc_dw", "sc_w_out", "ret_w_in", "ret_w_out")


def _pad_rows(a, rows):
    return jnp.pad(a, ((0, rows - a.shape[0]), (0, 0)))


def _split_rows(names, shards):
    out = []
    for nme in names:
        r = int(np.prod(shards[nme].shape[:-1]))
        out.append((nme, r, -(-r // 8) * 8))
    return out


def _pack_split(names, arrs):
    parts = []
    for nme, r, rp in _split_rows(names, arrs):
        parts.append(_pad_rows(arrs[nme].reshape(r, arrs[nme].shape[-1]), rp))
    return jnp.concatenate(parts, axis=0)


def _pack_rep(arrs, D):
    parts = [arrs[nme].reshape(-1, D) for nme in SMALL_REP]
    rows = sum(p.shape[0] for p in parts)
    return _pad_rows(jnp.concatenate(parts, axis=0), -(-rows // 8) * 8)


def kernel(x, positions, norm_g, mlp_up, mlp_down, conv_w_in, conv_b_in, conv_dw, conv_dw_b, conv_ln_g, conv_ln_b, conv_w_out, conv_b_out, pool_w, pool_scale, sc_w_in, sc_dw, sc_w_out, ret_w_in, ret_w_out, loss_target, m_norm_g, m_mlp_up, m_mlp_down, m_conv_w_in, m_conv_b_in, m_conv_dw, m_conv_dw_b, m_conv_ln_g, m_conv_ln_b, m_conv_w_out, m_conv_b_out, m_pool_w, m_pool_scale, m_sc_w_in, m_sc_dw, m_sc_w_out, m_ret_w_in, m_ret_w_out, v_norm_g, v_mlp_up, v_mlp_down, v_conv_w_in, v_conv_b_in, v_conv_dw, v_conv_dw_b, v_conv_ln_g, v_conv_ln_b, v_conv_w_out, v_conv_b_out, v_pool_w, v_pool_scale, v_sc_w_in, v_sc_dw, v_sc_w_out, v_ret_w_in, v_ret_w_out):
    w = dict(norm_g=norm_g, mlp_up=mlp_up, mlp_down=mlp_down, conv_w_in=conv_w_in, conv_b_in=conv_b_in, conv_dw=conv_dw,
             conv_dw_b=conv_dw_b, conv_ln_g=conv_ln_g, conv_ln_b=conv_ln_b, conv_w_out=conv_w_out, conv_b_out=conv_b_out,
             pool_w=pool_w, pool_scale=pool_scale, sc_w_in=sc_w_in, sc_dw=sc_dw, sc_w_out=sc_w_out, ret_w_in=ret_w_in, ret_w_out=ret_w_out)
    m = dict(norm_g=m_norm_g, mlp_up=m_mlp_up, mlp_down=m_mlp_down, conv_w_in=m_conv_w_in, conv_b_in=m_conv_b_in, conv_dw=m_conv_dw,
             conv_dw_b=m_conv_dw_b, conv_ln_g=m_conv_ln_g, conv_ln_b=m_conv_ln_b, conv_w_out=m_conv_w_out, conv_b_out=m_conv_b_out,
             pool_w=m_pool_w, pool_scale=m_pool_scale, sc_w_in=m_sc_w_in, sc_dw=m_sc_dw, sc_w_out=m_sc_w_out, ret_w_in=m_ret_w_in,
             ret_w_out=m_ret_w_out)
    v = dict(norm_g=v_norm_g, mlp_up=v_mlp_up, mlp_down=v_mlp_down, conv_w_in=v_conv_w_in, conv_b_in=v_conv_b_in, conv_dw=v_conv_dw,
             conv_dw_b=v_conv_dw_b, conv_ln_g=v_conv_ln_g, conv_ln_b=v_conv_ln_b, conv_w_out=v_conv_w_out, conv_b_out=v_conv_b_out,
             pool_w=v_pool_w, pool_scale=v_pool_scale, sc_w_in=v_sc_w_in, sc_dw=v_sc_dw, sc_w_out=v_sc_w_out, ret_w_in=v_ret_w_in,
             ret_w_out=v_ret_w_out)
    T, D = x.shape[1], x.shape[2]
    L = D // N_DEV
    G = len(POOL_WINDOWS)
    cg = D // G

    me = 4 * lax.axis_index("x") + 2 * lax.axis_index("y") + lax.axis_index("c")
    MIX = (("conv_w_in", "conv_w_out"), ("pool_w",), ("sc_w_in", "sc_w_out"), ("ret_w_in", "ret_w_out"))

    def shards(l, part):
        out = {k: w[k].astype(BF16) for k in MIX[l]} if part != "mlp" else {}
        if part != "mix":
            out["mlp_up"], out["mlp_down"] = w["mlp_up"][l].astype(BF16), w["mlp_down"][l].astype(BF16)
        return out

    sh0 = shards(0, "mix")
    gath0 = _gather_all(list(sh0.values()) + [_pack_split(SMALL_SPLIT, w)])
    small = gath0[-1]
    common = {k: w[k].reshape(1, -1) for k in SMALL_REP}
    off = 0
    for nme, r, rp in _split_rows(SMALL_SPLIT, w):
        common[nme] = small[:, off:off + r, :].transpose(1, 0, 2).reshape(r, D)
        off += rp
    flying = {}
    dep = small
    for l, part in [(0, "mlp")] + [(l, "both") for l in range(1, DEPTH)]:
        shl = shards(l, part)
        lands = [lax.dynamic_update_slice(lax.empty((N_DEV,) + a.shape, a.dtype), a[None], (me,) + (0,) * a.ndim) for a in shl.values()]
        send, recv, thru, lands, dep = _gather_start("gather_start_%d" % l, list(shl.values()), lands, [dep])
        flying[l] = dict(names=list(shl.keys()), send=send, recv=recv, thru=thru, lands=lands)

    def pass_on(l, after):
        f = flying[l]
        f["fsend"], f["frecv"], f["thru"], f["lands"], tok = _gather_pass("gather_pass_%d" % l, f["send"], f["recv"], f["thru"],
                                                                          f["lands"], after)
        return tok

    def arrive(l, after):
        f = flying[l]
        got = _gather_wait("gather_wait_%d" % l, f["send"], f["recv"], f["fsend"], f["frecv"], f["thru"], f["lands"], after)
        return dict(zip(f["names"], got))

    def get_w(l, part, after):
        if part == "mlp" and l > 0:
            return {"_deps": [pass_on(l + 1, after)]} if l + 1 < DEPTH else {}
        if l == 0 and part == "mix":
            gw = dict(zip(sh0.keys(), gath0[:-1]))
        elif l == 0:
            gw = arrive(0, pass_on(0, after))
            gw["_deps"] = [pass_on(1, gw["mlp_up"])]
        else:
            gw = arrive(l, after)
        out = dict(common)
        for k, a in gw.items():
            if k.endswith("_w_out"):
                out[k] = a.reshape(-1, D)
            elif k == "pool_w":
                out[k] = a.transpose(1, 0, 2, 3).reshape(G, cg, cg)
            else:
                out[k] = a
        return out

    res = {}
    mlp_res = {"mlp_up": [None] * DEPTH, "mlp_down": [None] * DEPTH}
    pending = []
    small_g = {}

    def shard2d(src, k, l):
        a = src[k][l] if k in mlp_res else src[k]
        return a.reshape(-1, a.shape[-1])

    def adam_layer(l, names, parts, own):
        for i, k in enumerate(names):
            out = _adamw("adamw_%s_%d" % (k, l), parts[i], shard2d(w, k, l), shard2d(m, k, l), shard2d(v, k, l),
                         None if own is None else own[i])
            if k in mlp_res:
                mlp_res[k][l] = out
            else:
                res[k] = [o.reshape(w[k].shape) for o in out]

    held = {}

    def put_g(l, part, g, after, norm_grads):
        small_g.update({k: g[k] for k in g if k in SMALL_SPLIT or k in SMALL_REP})
        if part == "mlp" and l > 0:
            held.update(g)
            return None
        if pending:
            tag, lp, names, send, recv, sums, lands = pending.pop()
            sums, parts = _chip_wait("chip_wait_" + tag, send, recv, sums, lands, after)
            adam_layer(lp, names, parts, sums)
        last = l == 0 and part == "mix"
        if l > 0:
            g = dict(g, **held)
            held.clear()
            names = list(MIX[l]) + ["mlp_up", "mlp_down"]
        else:
            names = list(MIX[0]) if last else ["mlp_up", "mlp_down"]
        tag = "%d%s" % (l, "" if l > 0 else part)
        stacks = []
        for k in names:
            a = g[k]
            if k == "pool_w":
                a = a.reshape(G, N_DEV, cg // N_DEV, cg).transpose(1, 0, 2, 3).astype(BF16)
            stacks.append(a.reshape(N_CHIP, 2, -1, w[k].shape[-1]))
        got = _pair_exchange("pair_exchange_" + tag, stacks)
        sums = [_pair_add("pair_add", a, b) for a, b in zip(stacks, got)]
        if not last:
            send, recv, sums, lands, tok = _chip_start("chip_start_" + tag, sums, [lax.empty(a.shape, a.dtype) for a in sums])
            pending.append((tag, l, names, send, recv, sums, lands))
            return tok
        small_g["norm_g"] = norm_grads
        gsm = []
        for nme, r, rp in _split_rows(SMALL_SPLIT, w):
            gsm.append(jnp.pad(small_g[nme].reshape(r, N_DEV, L), ((0, rp - r), (0, 0), (0, 0))))
        sm_sh = jnp.concatenate(gsm, axis=0).transpose(1, 0, 2)
        parts, r_sh, r_rep = _chip_exchange(sums, sm_sh, _pack_rep(small_g, D))
        adam_layer(0, names, parts, None)
        out = _adamw("adamw_split", r_sh, _pack_split(SMALL_SPLIT, w), _pack_split(SMALL_SPLIT, m), _pack_split(SMALL_SPLIT, v))
        off = 0
        for nme, r, rp in _split_rows(SMALL_SPLIT, w):
            res[nme] = [o[off:off + r].reshape(w[nme].shape) for o in out]
            off += rp
        out = _adamw("adamw_rep", r_rep, _pack_rep(w, D), _pack_rep(m, D), _pack_rep(v, D))
        off = 0
        for nme in SMALL_REP:
            r = w[nme].shape[0] // D
            res[nme] = [o[off:off + r].reshape(w[nme].shape) for o in out]
            off += r
        return None

    pos = positions.reshape(T, 1).astype(F32)
    loss_row, grad_x = _local_step(x.reshape(T, D), pos, loss_target.reshape(T, D), common["norm_g"], get_w, put_g, [dep], T, D)
    loss = lax.psum(0.5 * jnp.sum(loss_row) / D, AXES)
    for k in mlp_res:
        res[k] = [jnp.stack([mlp_res[k][l][i] for l in range(DEPTH)]).reshape(w[k].shape) for i in range(4)]
    outs = [loss, grad_x.reshape(x.shape)]
    for i in range(4):
        outs += [res[k][i] for k in WEIGHTS]
    return tuple(outs)
```

```python
import functools
import math

import numpy as np
import jax
import jax.numpy as jnp
from jax import lax
from jax.experimental import pallas as pl
from jax.experimental.pallas import tpu as pltpu

F32, BF16 = jnp.float32, jnp.bfloat16
MESH = pl.DeviceIdType.MESH
AXES = ("x", "y", "c")
N_DEV = 8
N_CHIP = 4
EPS = 1e-6
DEPTH = 4
CONV_WIDTH = 31
POOL_WINDOWS = (2, 4, 8, 16)
SHORT_CONV_WIDTH = 3
RET_HEADS = 4
RET_CHUNK = 128
ROPE_BASE = 10000.0
ADAM_LR, ADAM_B1, ADAM_B2, ADAM_EPS, ADAM_WD, ADAM_STEP = 0.001, 0.9, 0.999, 1e-08, 0.01, 10
LANES = 128
SEQ_PAD = 32
VMEM_LIMIT = 56 * 1024 * 1024

NN = (((1,), (0,)), ((), ()))
NT = (((1,), (1,)), ((), ()))
TN = (((0,), (0,)), ((), ()))


def _pcall(body, **kw):
    return pl.pallas_call(body, **kw)


def _params(sem):
    return pltpu.CompilerParams(dimension_semantics=sem, vmem_limit_bytes=VMEM_LIMIT)


def _tile(n, pref):
    t = min(n, pref)
    while n % t:
        t //= 2
    return t


def _rowwise(name, fn, ins, outs, rows, tr, deps=()):
    n_in = len(ins)
    n_dep = len(deps)
    in_specs = []
    for arr, nc, cb in ins:
        if nc is None:
            in_specs.append(pl.BlockSpec(arr.shape, lambda i, nd=arr.ndim: (0,) * nd))
        else:
            in_specs.append(pl.BlockSpec((tr, nc), lambda i, cb=cb: (i, cb)))
    out_specs, out_shapes = [], []
    for o in outs:
        if o[0] == "row":
            out_shapes.append(jax.ShapeDtypeStruct((rows, o[1]), o[2]))
            out_specs.append(pl.BlockSpec((tr, o[1]), lambda i: (i, 0)))
        else:
            out_shapes.append(jax.ShapeDtypeStruct(o[1], F32))
            out_specs.append(pl.BlockSpec(o[1], lambda i: (0, 0)))

    in_specs = in_specs + [pl.BlockSpec(memory_space=pl.ANY)] * n_dep

    def body(*refs):
        res = fn(*[r[...] for r in refs[:n_in]])
        if not isinstance(res, (tuple, list)):
            res = (res,)
        i = pl.program_id(0)
        for o, ref, r in zip(outs, refs[n_in + n_dep:], res):
            if o[0] == "row":
                ref[...] = r.astype(o[2])
            else:
                @pl.when(i == 0)
                def _(ref=ref, r=r):
                    ref[...] = r

                @pl.when(i > 0)
                def _(ref=ref, r=r):
                    ref[...] += r

    return _pcall(body, name=name, grid=(rows // tr,), in_specs=in_specs, out_specs=out_specs,
                  out_shape=out_shapes, compiler_params=_params(("arbitrary",)))(*[a for a, _, _ in ins], *deps)


def _rms(x, g):
    return x * lax.rsqrt(jnp.mean(x * x, axis=-1, keepdims=True) + EPS) * g


def _colsum(x):
    return jnp.sum(x, axis=0, keepdims=True)


def _square(r):
    rf = r.astype(F32)
    return (rf * rf).astype(r.dtype)


def _ln_silu(c, g, b):
    mu = jnp.mean(c, axis=-1, keepdims=True)
    xc = c - mu
    y = xc * lax.rsqrt(jnp.mean(xc * xc, axis=-1, keepdims=True) + EPS) * g + b
    return y * jax.nn.sigmoid(y)


def _glu(za, zg):
    return za * jax.nn.sigmoid(zg)


def _gate_norm(o, g):
    dv = o.shape[-1] // RET_HEADS
    parts = []
    for h in range(RET_HEADS):
        oh = o[:, h * dv:(h + 1) * dv]
        parts.append(oh * lax.rsqrt(jnp.mean(oh * oh, axis=-1, keepdims=True) + EPS))
    return (g * jax.nn.sigmoid(g)) * jnp.concatenate(parts, axis=-1)


def _rot(x, cos, sin, sign):
    dk = x.shape[-1] // RET_HEADS
    half = dk // 2
    parts = []
    for h in range(RET_HEADS):
        x1 = x[:, h * dk:h * dk + half]
        x2 = x[:, h * dk + half:(h + 1) * dk]
        parts.append(x1 * cos - sign * (x2 * sin))
        parts.append(sign * (x1 * sin) + x2 * cos)
    return jnp.concatenate(parts, axis=-1)


def _mm(name, mode, grid, ins, outs, epi=None, aliases=None, pre=None, deps=()):
    dn = {"nn": NN, "nt": NT, "tn": TN}[mode]
    gk = grid[2]
    n_in, n_out = len(ins), len(outs)
    n_dep = len(deps)

    def body(*refs):
        out_refs = refs[n_in + n_dep:n_in + n_dep + n_out]
        a = refs[0][...] if pre is None else pre(refs[0][...])
        part = lax.dot_general(a, refs[1][...], dn, preferred_element_type=F32)

        def finish(acc):
            res = epi(acc, *[r[...] for r in refs[2:n_in]]) if epi is not None else (acc,)
            for o, r in zip(out_refs, res):
                o[...] = r.astype(o.dtype)

        if gk == 1:
            finish(part)
        else:
            acc_ref = refs[-1]
            k = pl.program_id(2)

            @pl.when(k == 0)
            def _():
                acc_ref[...] = part

            @pl.when(k > 0)
            def _():
                acc_ref[...] += part

            @pl.when(k == gk - 1)
            def _():
                finish(acc_ref[...])

    scratch = []
    if gk > 1:
        blk = [d for d in outs[0][2].block_shape if d is not None]
        scratch = [pltpu.VMEM(tuple(blk), F32)]
    kw = {}
    if aliases:
        kw["input_output_aliases"] = aliases
    return _pcall(body, name=name, grid=grid, in_specs=[s for _, s in ins] + [pl.BlockSpec(memory_space=pl.ANY)] * n_dep,
                  out_specs=[s for _, _, s in outs],
                  out_shape=[jax.ShapeDtypeStruct(sh, dt) for sh, dt, _ in outs], scratch_shapes=scratch,
                  compiler_params=_params(("parallel", "parallel", "arbitrary")), **kw)(*[a for a, _ in ins], *deps)


def _spec(block, fn):
    return pl.BlockSpec(block, fn)


def _mm_cols(name, a, wg, lead, T, epi=None, extras=(), n_out=1, out_dtype=F32, deps=()):
    K, nb = wg.shape[-2], wg.shape[-1]
    tm = _tile(T, 2048)
    wblock = (None,) * (1 + len(lead)) + (K, nb)
    ins = [(a, _spec((tm, K), lambda i, j, k: (i, 0))), (wg, _spec(wblock, lambda i, j, k: (j,) + tuple(lead) + (0, 0)))]
    for e in extras:
        if e.shape[0] == 1:
            ins.append((e, _spec((1, nb), lambda i, j, k: (0, j))))
        else:
            ins.append((e, _spec((tm, nb), lambda i, j, k: (i, j))))
    outs = [((T, N_DEV * nb), out_dtype, _spec((tm, nb), lambda i, j, k: (i, j))) for _ in range(n_out)]
    return _mm(name, "nn", (T // tm, N_DEV, 1), ins, outs, epi, deps=deps)


def _mm_cols_t(name, d, wg, lead, T, epi=None, extras=(), out_dtype=F32):
    K, nb = wg.shape[-2], wg.shape[-1]
    tm = _tile(T, 2048)
    wblock = (None,) * (1 + len(lead)) + (K, nb)
    ins = [(d, _spec((tm, nb), lambda i, j, k: (i, k))), (wg, _spec(wblock, lambda i, j, k: (k,) + tuple(lead) + (0, 0)))]
    for e in extras:
        ins.append((e, _spec((tm, K), lambda i, j, k: (i, 0))))
    outs = [((T, K), out_dtype, _spec((tm, K), lambda i, j, k: (i, 0)))]
    return _mm(name, "nt", (T // tm, 1, N_DEV), ins, outs, epi)[0]


def _mm_cols_grad(name, a, d, wg_shape, lead, T):
    K, nb = wg_shape[-2], wg_shape[-1]
    tk = _tile(T, 2048)
    wblock = (None,) * (1 + len(lead)) + (K, nb)
    ins = [(a, _spec((tk, K), lambda i, j, k: (k, 0))), (d, _spec((tk, nb), lambda i, j, k: (k, j)))]
    outs = [(tuple(wg_shape), BF16, _spec(wblock, lambda i, j, k: (j,) + tuple(lead) + (0, 0)))]
    return _mm(name, "tn", (1, N_DEV, T // tk), ins, outs, lambda acc: (acc,))[0]


def _mm_whole_k(name, a, w3, T, split, pre=None, out_dtype=F32):
    nblk, d1, d2 = w3.shape
    kb = d1 if split == "rows" else d2
    N = d2 if split == "rows" else d1
    tm = _tile(T, 512)

    def body(a_ref, w_ref, o_ref, *scratch):
        av = a_ref[...] if pre is None else pre(a_ref[...])
        if split == "rows":
            o_ref[...] = lax.dot_general(av, w_ref[...].reshape(nblk * kb, N), NN, preferred_element_type=F32).astype(out_dtype)
        else:
            wcat = scratch[0]

            @pl.when(pl.program_id(0) == 0)
            def _():
                for j in range(nblk):
                    wcat[:, j * kb:(j + 1) * kb] = w_ref[j]

            o_ref[...] = lax.dot_general(av, wcat[...], NT, preferred_element_type=F32).astype(out_dtype)

    scratch = [] if split == "rows" else [pltpu.VMEM((N, nblk * kb), w3.dtype)]
    return _pcall(body, name=name, grid=(T // tm,),
                  in_specs=[pl.BlockSpec((tm, nblk * kb), lambda i: (i, 0)), pl.BlockSpec(w3.shape, lambda i: (0, 0, 0))],
                  out_specs=pl.BlockSpec((tm, N), lambda i: (i, 0)), out_shape=jax.ShapeDtypeStruct((T, N), out_dtype),
                  scratch_shapes=scratch, compiler_params=_params(("arbitrary",)))(a, w3)


def _mm_rows(name, a, w2, T, epi=None, extras=(), out_dtype=F32):
    Kin, N = w2.shape
    tm, tn = _tile(T, 2048), _tile(N, 512)
    ins = [(a, _spec((tm, Kin), lambda i, j, k: (i, 0))), (w2, _spec((Kin, tn), lambda i, j, k: (0, j)))]
    for e in extras:
        ins.append((e, _spec((1, tn), lambda i, j, k: (0, j))))
    outs = [((T, N), out_dtype, _spec((tm, tn), lambda i, j, k: (i, j)))]
    return _mm(name, "nn", (T // tm, N // tn, 1), ins, outs, epi)[0]


def _mm_rows_t(name, d, w2, T, out_dtype=F32):
    Kin, N = w2.shape
    tm, tn = _tile(T, 2048), _tile(Kin, 512)
    ins = [(d, _spec((tm, N), lambda i, j, k: (i, 0))), (w2, _spec((tn, N), lambda i, j, k: (j, 0)))]
    outs = [((T, Kin), out_dtype, _spec((tm, tn), lambda i, j, k: (i, j)))]
    return _mm(name, "nt", (T // tm, Kin // tn, 1), ins, outs)[0]


def _mm_rows_grad(name, a, d, T):
    Kin, N = a.shape[1], d.shape[1]
    tk, tm, tn = _tile(T, 2048), _tile(Kin, 512), _tile(N, 1024)
    ins = [(a, _spec((tk, tm), lambda i, j, k: (k, i))), (d, _spec((tk, tn), lambda i, j, k: (k, j)))]
    outs = [((Kin, N), BF16, _spec((tm, tn), lambda i, j, k: (i, j)))]
    return _mm(name, "tn", (Kin // tm, N // tn, T // tk), ins, outs, lambda acc: (acc,))[0]


def _dwconv_fwd(name, x, w, bias, T):
    W, C = w.shape
    cb = min(C, LANES)
    tc = _tile(T, 256)

    def body(x_ref, w_ref, b_ref, o_ref, xp_ref):
        xp_ref[pl.ds(0, SEQ_PAD), :] = jnp.zeros((SEQ_PAD, cb), F32)

        def fill(c, _):
            base = pl.multiple_of(c * tc, tc)
            xp_ref[pl.ds(base + SEQ_PAD, tc), :] = x_ref[pl.ds(base, tc), :]
            return 0

        lax.fori_loop(0, T // tc, fill, 0)

        def chunk(c, _):
            base = pl.multiple_of(c * tc, tc)
            acc = jnp.zeros((tc, cb), F32) + b_ref[...]
            for k in range(W):
                acc = acc + w_ref[pl.ds(k, 1), :] * xp_ref[pl.ds(base + SEQ_PAD - (W - 1) + k, tc), :]
            o_ref[pl.ds(base, tc), :] = acc
            return 0

        lax.fori_loop(0, T // tc, chunk, 0)

    return _pcall(body, name=name, grid=(C // cb,),
                  in_specs=[pl.BlockSpec((T, cb), lambda i: (0, i)), pl.BlockSpec((W, cb), lambda i: (0, i)),
                            pl.BlockSpec((1, cb), lambda i: (0, i))],
                  out_specs=pl.BlockSpec((T, cb), lambda i: (0, i)), out_shape=jax.ShapeDtypeStruct((T, C), F32),
                  scratch_shapes=[pltpu.VMEM((T + SEQ_PAD, cb), F32)],
                  compiler_params=_params(("parallel",)))(x, w, bias)


def _dwconv_bwd(name, dc, x, w, T):
    W, C = w.shape
    cb = min(C, LANES)
    tc = _tile(T, 128)

    def body(dc_ref, x_ref, w_ref, dx_ref, dw_ref, xp_ref, dp_ref, acc_ref):
        xp_ref[pl.ds(0, SEQ_PAD), :] = jnp.zeros((SEQ_PAD, cb), F32)
        dp_ref[pl.ds(T, SEQ_PAD), :] = jnp.zeros((SEQ_PAD, cb), F32)
        acc_ref[...] = jnp.zeros((W * 8, cb), F32)

        def fill(c, _):
            base = pl.multiple_of(c * tc, tc)
            xp_ref[pl.ds(base + SEQ_PAD, tc), :] = x_ref[pl.ds(base, tc), :]
            dp_ref[pl.ds(base, tc), :] = dc_ref[pl.ds(base, tc), :]
            return 0

        lax.fori_loop(0, T // tc, fill, 0)

        def chunk(c, _):
            base = pl.multiple_of(c * tc, tc)
            d = dc_ref[pl.ds(base, tc), :]
            acc = jnp.zeros((tc, cb), F32)
            for k in range(W):
                acc = acc + w_ref[pl.ds(k, 1), :] * dp_ref[pl.ds(base + (W - 1) - k, tc), :]
                prod = d * xp_ref[pl.ds(base + SEQ_PAD - (W - 1) + k, tc), :]
                acc_ref[pl.ds(8 * k, 8), :] += jnp.sum(prod.reshape(tc // 8, 8, cb), axis=0)
            dx_ref[pl.ds(base, tc), :] = acc
            return 0

        lax.fori_loop(0, T // tc, chunk, 0)
        for k in range(W):
            dw_ref[pl.ds(k, 1), :] = jnp.sum(acc_ref[pl.ds(8 * k, 8), :], axis=0, keepdims=True)

    return _pcall(body, name=name, grid=(C // cb,),
                  in_specs=[pl.BlockSpec((T, cb), lambda i: (0, i)), pl.BlockSpec((T, cb), lambda i: (0, i)),
                            pl.BlockSpec((W, cb), lambda i: (0, i))],
                  out_specs=[pl.BlockSpec((T, cb), lambda i: (0, i)), pl.BlockSpec((W, cb), lambda i: (0, i))],
                  out_shape=[jax.ShapeDtypeStruct((T, C), F32), jax.ShapeDtypeStruct((W, C), F32)],
                  scratch_shapes=[pltpu.VMEM((T + SEQ_PAD, cb), F32), pltpu.VMEM((T + SEQ_PAD, cb), F32),
                                  pltpu.VMEM((W * 8, cb), F32)],
                  compiler_params=_params(("parallel",)))(dc, x, w)


def _pool_count(base, tc, cg, win):
    t = (lax.broadcasted_iota(jnp.int32, (tc, cg), 0) + base + 1).astype(F32)
    return jnp.minimum(t, float(win))


def _pool_fwd(name, u, T):
    D = u.shape[1]
    G = len(POOL_WINDOWS)
    cg = min(D // G, LANES)
    nb = (D // G) // cg
    tc = _tile(T, 128)

    def body(u_ref, o_ref, up_ref):
        g = pl.program_id(0)
        up_ref[pl.ds(0, SEQ_PAD), :] = jnp.zeros((SEQ_PAD, cg), F32)

        def fill(c, _):
            base = pl.multiple_of(c * tc, tc)
            up_ref[pl.ds(base + SEQ_PAD, tc), :] = u_ref[pl.ds(base, tc), :]
            return 0

        lax.fori_loop(0, T // tc, fill, 0)
        for gi, win in enumerate(POOL_WINDOWS):
            @pl.when(g == gi)
            def _(win=win):
                def chunk(c, _):
                    base = pl.multiple_of(c * tc, tc)
                    acc = up_ref[pl.ds(base + SEQ_PAD, tc), :]
                    for j in range(1, win):
                        acc = acc + up_ref[pl.ds(base + SEQ_PAD - j, tc), :]
                    mixed = acc / _pool_count(base, tc, cg, win) - u_ref[pl.ds(base, tc), :]
                    o_ref[pl.ds(base, tc), :] = mixed.astype(BF16)
                    return 0

                lax.fori_loop(0, T // tc, chunk, 0)

    return _pcall(body, name=name, grid=(G, nb), in_specs=[pl.BlockSpec((T, cg), lambda g, i: (0, g * nb + i))],
                  out_specs=pl.BlockSpec((T, cg), lambda g, i: (0, g * nb + i)), out_shape=jax.ShapeDtypeStruct((T, D), BF16),
                  scratch_shapes=[pltpu.VMEM((T + SEQ_PAD, cg), F32)], compiler_params=_params(("parallel", "parallel")))(u)


def _pool_bwd(name, dm, T):
    D = dm.shape[1]
    G = len(POOL_WINDOWS)
    cg = min(D // G, LANES)
    nb = (D // G) // cg
    tc = _tile(T, 128)

    def body(d_ref, o_ref, qp_ref):
        g = pl.program_id(0)
        qp_ref[pl.ds(T, SEQ_PAD), :] = jnp.zeros((SEQ_PAD, cg), F32)
        for gi, win in enumerate(POOL_WINDOWS):
            @pl.when(g == gi)
            def _(win=win):
                def fill(c, _):
                    base = pl.multiple_of(c * tc, tc)
                    qp_ref[pl.ds(base, tc), :] = d_ref[pl.ds(base, tc), :] / _pool_count(base, tc, cg, win)
                    return 0

                lax.fori_loop(0, T // tc, fill, 0)

                def chunk(c, _):
                    base = pl.multiple_of(c * tc, tc)
                    acc = qp_ref[pl.ds(base, tc), :]
                    for j in range(1, win):
                        acc = acc + qp_ref[pl.ds(base + j, tc), :]
                    o_ref[pl.ds(base, tc), :] = (acc - d_ref[pl.ds(base, tc), :]).astype(BF16)
                    return 0

                lax.fori_loop(0, T // tc, chunk, 0)

    return _pcall(body, name=name, grid=(G, nb), in_specs=[pl.BlockSpec((T, cg), lambda g, i: (0, g * nb + i))],
                  out_specs=pl.BlockSpec((T, cg), lambda g, i: (0, g * nb + i)), out_shape=jax.ShapeDtypeStruct((T, D), BF16),
                  scratch_shapes=[pltpu.VMEM((T + SEQ_PAD, cg), F32)], compiler_params=_params(("parallel", "parallel")))(dm)


def _pool_mm(name, mode, a, b, T, scale=None):
    G = len(POOL_WINDOWS)
    D = a.shape[1]
    cg = D // G
    tm = _tile(T, 2048)
    if mode == "tn":
        ins = [(a, _spec((tm, cg), lambda i, j, k: (k, j))), (b, _spec((tm, cg), lambda i, j, k: (k, j)))]
        outs = [((G, cg, cg), F32, _spec((None, cg, cg), lambda i, j, k: (j, 0, 0)))]
        return _mm(name, "tn", (1, G, T // tm), ins, outs, lambda acc: (acc,))[0]
    ins = [(a, _spec((tm, cg), lambda i, j, k: (i, j))), (b, _spec((None, cg, cg), lambda i, j, k: (j, 0, 0)))]
    if scale is not None:
        ins.append((scale, _spec((1, cg), lambda i, j, k: (0, j))))
        outs = [((T, D), BF16, _spec((tm, cg), lambda i, j, k: (i, j))), ((T, D), BF16, _spec((tm, cg), lambda i, j, k: (i, j)))]
        return _mm(name, mode, (T // tm, G, 1), ins, outs, lambda acc, s: (acc * s, acc))
    outs = [((T, D), F32, _spec((tm, cg), lambda i, j, k: (i, j)))]
    return _mm(name, mode, (T // tm, G, 1), ins, outs)[0]


def _sc_mid_fwd(name, z, w, T):
    D = w.shape[1]
    W = w.shape[0]
    cb = min(D, LANES)
    nb = D // cb
    tc = _tile(T, 256)

    def body(b_ref, c_ref, v_ref, w_ref, o_ref, pp_ref):
        pp_ref[pl.ds(0, SEQ_PAD), :] = jnp.zeros((SEQ_PAD, cb), F32)

        def fill(c, _):
            base = pl.multiple_of(c * tc, tc)
            pp_ref[pl.ds(base + SEQ_PAD, tc), :] = c_ref[pl.ds(base, tc), :].astype(F32) * v_ref[pl.ds(base, tc), :].astype(F32)
            return 0

        lax.fori_loop(0, T // tc, fill, 0)

        def chunk(c, _):
            base = pl.multiple_of(c * tc, tc)
            q = jnp.zeros((tc, cb), F32)
            for k in range(W):
                q = q + w_ref[pl.ds(k, 1), :] * pp_ref[pl.ds(base + SEQ_PAD - (W - 1) + k, tc), :]
            o_ref[pl.ds(base, tc), :] = (b_ref[pl.ds(base, tc), :].astype(F32) * q).astype(BF16)
            return 0

        lax.fori_loop(0, T // tc, chunk, 0)

    return _pcall(body, name=name, grid=(nb,),
                  in_specs=[pl.BlockSpec((T, cb), lambda i: (0, i)), pl.BlockSpec((T, cb), lambda i: (0, nb + i)),
                            pl.BlockSpec((T, cb), lambda i: (0, 2 * nb + i)), pl.BlockSpec((W, cb), lambda i: (0, i))],
                  out_specs=pl.BlockSpec((T, cb), lambda i: (0, i)), out_shape=jax.ShapeDtypeStruct((T, D), BF16),
                  scratch_shapes=[pltpu.VMEM((T + SEQ_PAD, cb), F32)],
                  compiler_params=_params(("parallel",)))(z, z, z, w)


def _sc_mid_bwd(name, z, dm, w, T):
    D = w.shape[1]
    W = w.shape[0]
    cb = min(D, LANES)
    nb = D // cb
    tc = _tile(T, 128)

    def body(b_ref, c_ref, v_ref, d_ref, w_ref, dz_ref, dw_ref, pp_ref, dq_ref, acc_ref):
        part = pl.program_id(0)
        pp_ref[pl.ds(0, SEQ_PAD), :] = jnp.zeros((SEQ_PAD, cb), F32)
        dq_ref[pl.ds(T, SEQ_PAD), :] = jnp.zeros((SEQ_PAD, cb), F32)
        acc_ref[...] = jnp.zeros((W * 8, cb), F32)

        def fill(c, _):
            base = pl.multiple_of(c * tc, tc)
            pp_ref[pl.ds(base + SEQ_PAD, tc), :] = c_ref[pl.ds(base, tc), :].astype(F32) * v_ref[pl.ds(base, tc), :].astype(F32)
            dq_ref[pl.ds(base, tc), :] = d_ref[pl.ds(base, tc), :] * b_ref[pl.ds(base, tc), :].astype(F32)
            return 0

        lax.fori_loop(0, T // tc, fill, 0)

        @pl.when(part == 0)
        def _():
            def chunk(c, _):
                base = pl.multiple_of(c * tc, tc)
                dq = dq_ref[pl.ds(base, tc), :]
                q = jnp.zeros((tc, cb), F32)
                for k in range(W):
                    p = pp_ref[pl.ds(base + SEQ_PAD - (W - 1) + k, tc), :]
                    q = q + w_ref[pl.ds(k, 1), :] * p
                    acc_ref[pl.ds(8 * k, 8), :] += jnp.sum((dq * p).reshape(tc // 8, 8, cb), axis=0)
                dz_ref[pl.ds(base, tc), :] = (d_ref[pl.ds(base, tc), :] * q).astype(BF16)
                return 0

            lax.fori_loop(0, T // tc, chunk, 0)

        @pl.when(part > 0)
        def _():
            def chunk(c, _):
                base = pl.multiple_of(c * tc, tc)
                dp = jnp.zeros((tc, cb), F32)
                for k in range(W):
                    dp = dp + w_ref[pl.ds(k, 1), :] * dq_ref[pl.ds(base + (W - 1) - k, tc), :]
                other = jnp.where(part == 1, v_ref[pl.ds(base, tc), :], c_ref[pl.ds(base, tc), :]).astype(F32)
                dz_ref[pl.ds(base, tc), :] = (dp * other).astype(BF16)
                return 0

            lax.fori_loop(0, T // tc, chunk, 0)

        for k in range(W):
            dw_ref[pl.ds(k, 1), :] = jnp.sum(acc_ref[pl.ds(8 * k, 8), :], axis=0, keepdims=True)

    dz, dw3 = _pcall(body, name=name, grid=(3, nb),
                     in_specs=[pl.BlockSpec((T, cb), lambda p, i: (0, i)), pl.BlockSpec((T, cb), lambda p, i: (0, nb + i)),
                               pl.BlockSpec((T, cb), lambda p, i: (0, 2 * nb + i)), pl.BlockSpec((T, cb), lambda p, i: (0, i)),
                               pl.BlockSpec((W, cb), lambda p, i: (0, i))],
                     out_specs=[pl.BlockSpec((T, cb), lambda p, i: (0, p * nb + i)),
                                pl.BlockSpec((None, W, cb), lambda p, i: (p, 0, i))],
                     out_shape=[jax.ShapeDtypeStruct((T, 3 * D), BF16), jax.ShapeDtypeStruct((3, W, D), F32)],
                     scratch_shapes=[pltpu.VMEM((T + SEQ_PAD, cb), F32), pltpu.VMEM((T + SEQ_PAD, cb), F32),
                                     pltpu.VMEM((W * 8, cb), F32)],
                     compiler_params=_params(("parallel", "parallel")))(z, z, z, dm, w)
    return dz, dw3


def _ret_tables(dk, dv):
    C = RET_CHUNK
    lg = np.log1p(-np.exp2(-5.0 - np.arange(RET_HEADS, dtype=np.float64)))
    idx = np.arange(C, dtype=np.float64)
    rel = idx[:, None] - idx[None, :]
    mask = np.where(rel >= 0, np.exp(lg[:, None, None] * np.maximum(rel, 0.0)), 0.0)
    qd = np.exp(lg[:, None] * (idx + 1.0))[:, :, None] * np.ones((1, 1, dk))
    kd = np.exp(lg[:, None] * (C - 1.0 - idx))[:, :, None] * np.ones((1, 1, dk))
    cd = np.exp(lg * C)[:, None, None] * np.ones((1, 1, dv))
    return tuple(jnp.asarray(a, F32) for a in (mask, qd, kd, cd))


def _ret_fwd(name, q, k, v, tabs, T):
    H, C = RET_HEADS, RET_CHUNK
    dk, dv = q.shape[1] // H, v.shape[1] // H
    nch = T // C
    mask, qd, kd, cd = tabs

    def body(q_ref, k_ref, v_ref, m_ref, qd_ref, kd_ref, cd_ref, o_ref, st_ref, s_ref):
        @pl.when(pl.program_id(0) == 0)
        def _():
            s_ref[...] = jnp.zeros((H, dk, dv), F32)

        for h in range(H):
            qv, kv = q_ref[:, h * dk:(h + 1) * dk], k_ref[:, h * dk:(h + 1) * dk]
            vv = v_ref[:, h * dv:(h + 1) * dv]
            s = s_ref[h]
            sb = s.astype(BF16)
            st_ref[h] = sb
            scores = lax.dot_general(qv, kv, NT, preferred_element_type=F32) * m_ref[h]
            intra = lax.dot_general(scores.astype(BF16), vv, NN, preferred_element_type=F32)
            cross = lax.dot_general((qv.astype(F32) * qd_ref[h]).astype(BF16), sb, NN, preferred_element_type=F32)
            o_ref[:, h * dv:(h + 1) * dv] = intra + cross
            upd = lax.dot_general((kv.astype(F32) * kd_ref[h]).astype(BF16), vv, TN, preferred_element_type=F32)
            s_ref[h] = s * cd_ref[h] + upd

    whole = lambda a: pl.BlockSpec(a.shape, lambda n: (0, 0, 0))
    return _pcall(body, name=name, grid=(nch,),
                  in_specs=[pl.BlockSpec((C, H * dk), lambda n: (n, 0)), pl.BlockSpec((C, H * dk), lambda n: (n, 0)),
                            pl.BlockSpec((C, H * dv), lambda n: (n, 0)), whole(mask), whole(qd), whole(kd), whole(cd)],
                  out_specs=[pl.BlockSpec((C, H * dv), lambda n: (n, 0)),
                             pl.BlockSpec((H, None, dk, dv), lambda n: (0, n, 0, 0))],
                  out_shape=[jax.ShapeDtypeStruct((T, H * dv), F32), jax.ShapeDtypeStruct((H, nch, dk, dv), BF16)],
                  scratch_shapes=[pltpu.VMEM((H, dk, dv), F32)],
                  compiler_params=_params(("arbitrary",)))(q, k, v, mask, qd, kd, cd)


def _ret_bwd(name, q, k, v, do, st, tabs, T):
    H, C = RET_HEADS, RET_CHUNK
    dk, dv = q.shape[1] // H, v.shape[1] // H
    nch = T // C
    mask, qd, kd, cd = tabs

    def body(q_ref, k_ref, v_ref, do_ref, st_ref, m_ref, qd_ref, kd_ref, cd_ref, dq_ref, dk_ref, dv_ref, ds_ref):
        @pl.when(pl.program_id(0) == 0)
        def _():
            ds_ref[...] = jnp.zeros((H, dk, dv), F32)

        for h in range(H):
            qv, kv = q_ref[:, h * dk:(h + 1) * dk], k_ref[:, h * dk:(h + 1) * dk]
            vv, dov = v_ref[:, h * dv:(h + 1) * dv], do_ref[:, h * dv:(h + 1) * dv]
            sb = st_ref[h]
            m = m_ref[h]
            ds = ds_ref[h]
            dsb = ds.astype(BF16)
            a = (lax.dot_general(qv, kv, NT, preferred_element_type=F32) * m).astype(BF16)
            dp = (lax.dot_general(dov, vv, NT, preferred_element_type=F32) * m).astype(BF16)
            qa = (qv.astype(F32) * qd_ref[h]).astype(BF16)
            kb = (kv.astype(F32) * kd_ref[h]).astype(BF16)
            dq_ref[:, h * dk:(h + 1) * dk] = (lax.dot_general(dp, kv, NN, preferred_element_type=F32)
                                              + qd_ref[h] * lax.dot_general(dov, sb, NT, preferred_element_type=F32))
            dk_ref[:, h * dk:(h + 1) * dk] = (lax.dot_general(dp, qv, TN, preferred_element_type=F32)
                                              + kd_ref[h] * lax.dot_general(vv, dsb, NT, preferred_element_type=F32))
            dv_ref[:, h * dv:(h + 1) * dv] = (lax.dot_general(a, dov, TN, preferred_element_type=F32)
                                              + lax.dot_general(kb, dsb, NN, preferred_element_type=F32)).astype(BF16)
            ds_ref[h] = ds * cd_ref[h] + lax.dot_general(qa, dov, TN, preferred_element_type=F32)

    rev = lambda n: (nch - 1 - n, 0)
    whole = lambda a: pl.BlockSpec(a.shape, lambda n: (0, 0, 0))
    return _pcall(body, name=name, grid=(nch,),
                  in_specs=[pl.BlockSpec((C, H * dk), rev), pl.BlockSpec((C, H * dk), rev), pl.BlockSpec((C, H * dv), rev),
                            pl.BlockSpec((C, H * dv), rev), pl.BlockSpec((H, None, dk, dv), lambda n: (0, nch - 1 - n, 0, 0)),
                            whole(mask), whole(qd), whole(kd), whole(cd)],
                  out_specs=[pl.BlockSpec((C, H * dk), rev), pl.BlockSpec((C, H * dk), rev), pl.BlockSpec((C, H * dv), rev)],
                  out_shape=[jax.ShapeDtypeStruct((T, H * dk), F32), jax.ShapeDtypeStruct((T, H * dk), F32),
                             jax.ShapeDtypeStruct((T, H * dv), BF16)],
                  scratch_shapes=[pltpu.VMEM((H, dk, dv), F32)],
                  compiler_params=_params(("arbitrary",)))(q, k, v, do, st, mask, qd, kd, cd)


def _local_step(x, pos, tgt, ng, get_w, put_g, first_deps, T, D):
    tr = _tile(T, 512)
    trw = _tile(T, 128)
    row = lambda i: ng[i:i + 1]

    u = _rowwise("rms_first", lambda xv, gv: _rms(xv, gv), [(x, D, 0), (row(0), None, None)], [("row", D, BF16)], T, tr,
                 deps=first_deps)[0]
    h = x
    saved = []
    cos = sin = None
    tabs = None
    for l in range(DEPTH):
        mixer = l % 4
        wts = get_w(l, "mix", h)
        sv = {"h_in": h, "u": u}
        if mixer == 0:
            z = _mm_cols("conv_in", u, wts["conv_w_in"], (), T, lambda acc, b: (acc + b,), (wts["conv_b_in"],), 1, BF16)[0]
            v = _rowwise("conv_glu", lambda za, zg: _glu(za.astype(F32), zg.astype(F32)), [(z, D, 0), (z, D, 1)], [("row", D, F32)], T, tr)[0]
            c = _dwconv_fwd("conv_dw", v, wts["conv_dw"], wts["conv_dw_b"], T)
            s = _rowwise("conv_ln", _ln_silu, [(c, D, 0), (wts["conv_ln_g"], None, None), (wts["conv_ln_b"], None, None)],
                         [("row", D, BF16)], T, tr)[0]
            y = _mm_rows("conv_out", s, wts["conv_w_out"], T, lambda acc, b: (acc + b,), (wts["conv_b_out"],), BF16)
            sv.update(z=z, v=v, c=c, s=s)
        elif mixer == 1:
            mixed = _pool_fwd("pool_fwd", u, T)
            y, ycat = _pool_mm("pool_mm", "nn", mixed, wts["pool_w"], T, wts["pool_scale"])
            sv.update(mixed=mixed, ycat=ycat)
        elif mixer == 2:
            z = _mm_cols("sc_in", u, wts["sc_w_in"], (), T, out_dtype=BF16)[0]
            m = _sc_mid_fwd("sc_mid", z, wts["sc_dw"], T)
            y = _mm_rows("sc_out", m, wts["sc_w_out"], T, out_dtype=BF16)
            sv.update(z=z, m=m)
        else:
            dk = D // RET_HEADS
            half = dk // 2
            inv_freq = jnp.asarray((ROPE_BASE ** (-np.arange(half, dtype=np.float64) / half)).astype(np.float32)).reshape(1, half)

            def _cs(p, f):
                ang = p * f
                return jnp.cos(ang), jnp.sin(ang)

            cos, sin = _rowwise("rope_tab", _cs, [(pos, 1, 0), (inv_freq, None, None)],
                                [("row", half, F32), ("row", half, F32)], T, tr)
            tabs = _ret_tables(dk, 2 * dk)
            z = _mm_cols("ret_in", u, wts["ret_w_in"], (), T, out_dtype=BF16)[0]

            def _pre(zq, zk, zv, cv, sn):
                return _rot(zq.astype(F32), cv, sn, 1.0), _rot(zk.astype(F32), cv, sn, 1.0) * (dk ** -0.5), zv

            qr, kr, vb = _rowwise("ret_pre", _pre, [(z, D, 0), (z, D, 1), (z, 2 * D, 1), (cos, half, 0), (sin, half, 0)],
                                  [("row", D, BF16), ("row", D, BF16), ("row", 2 * D, BF16)], T, trw)
            o, st = _ret_fwd("ret_fwd", qr, kr, vb, tabs, T)
            out = _rowwise("ret_post", lambda ov, gv: _gate_norm(ov, gv.astype(F32)), [(o, 2 * D, 0), (z, 2 * D, 2)], [("row", 2 * D, BF16)], T, trw)[0]
            y = _mm_rows("ret_out", out, wts["ret_w_out"], T, out_dtype=BF16)
            sv.update(z=z, qr=qr, kr=kr, vb=vb, o=o, st=st, out=out)
        sv["y1"] = y
        h, u = _rowwise("bound_a", lambda hv, yv, gp, gn: (lambda hn: (hn, _rms(hn, gn)))(hv + _rms(yv.astype(F32), gp)),
                        [(h, D, 0), (y, D, 0), (row(4 * l + 1), None, None), (row(4 * l + 2), None, None)],
                        [("row", D, F32), ("row", D, BF16)], T, tr)
        sv["h_mid"], sv["u2"] = h, u
        wm = get_w(l, "mlp", h)
        mlp_deps = wm.pop("_deps", [])
        wts = dict(wts, **wm)
        sv["w"] = wts
        r = _mm_cols("mlp_up", u, wts["mlp_up"], (), T, lambda acc: (jnp.maximum(acc, 0.0),), (), 1, BF16, deps=mlp_deps)[0]
        y = _mm_whole_k("mlp_down", r, wts["mlp_down"], T, "rows", pre=_square, out_dtype=BF16)
        sv.update(r=r, y2=y)
        if l + 1 < DEPTH:
            udt = F32 if (l + 1) % 4 == 1 else BF16
            h, u = _rowwise("bound_b", lambda hv, yv, gp, gn: (lambda hn: (hn, _rms(hn, gn)))(hv + _rms(yv.astype(F32), gp)),
                            [(h, D, 0), (y, D, 0), (row(4 * l + 3), None, None), (row(4 * l + 4), None, None)],
                            [("row", D, F32), ("row", D, udt)], T, tr)
        saved.append(sv)

    def _final(hv, yv, gp, tv):
        hn = hv + _rms(yv.astype(F32), gp)
        e = hn - tv
        return e * (1.0 / D), _colsum(e * e)

    dh, loss_row = _rowwise("final", _final, [(h, D, 0), (y, D, 0), (row(4 * DEPTH - 1), None, None), (tgt, D, 0)],
                            [("row", D, F32), ("acc", (1, D))], T, tr)

    gn = [None] * (4 * DEPTH)
    F = saved[0]["r"].shape[1]
    du_next = None
    tok = None
    for l in reversed(range(DEPTH)):
        sv = saved[l]
        wts = sv["w"]
        g = {}
        deps = [] if tok is None else [tok]
        mixer = l % 4
        if du_next is None:
            def _bb(dhv, yv, gp):
                _, vjp = jax.vjp(_rms, yv.astype(F32), gp)
                dy, dgp = vjp(dhv)
                return dy, dgp

            dy, gn[4 * l + 3] = _rowwise("bwd_last", _bb, [(dh, D, 0), (sv["y2"], D, 0), (row(4 * l + 3), None, None)],
                                         [("row", D, BF16), ("acc", (1, D))], T, tr, deps)
        else:
            dh, dy, gn[4 * l + 3], gn[4 * l + 4], _ = _bound_bwd("bwd_b", dh, du_next, saved[l + 1]["h_in"], sv["y2"],
                                                              row(4 * l + 3), row(4 * l + 4), T, D, tr, deps)
        F8 = F // N_DEV
        tm = _tile(T, 2048)
        da = _mm("mlp_dhid", "nt", (T // tm, N_DEV, 1),
                 [(dy, _spec((tm, D), lambda i, j, k: (i, 0))),
                  (wts["mlp_down"], _spec((None, F8, D), lambda i, j, k: (j, 0, 0))),
                  (sv["r"], _spec((tm, F8), lambda i, j, k: (i, j)))],
                 [((T, F), BF16, _spec((tm, F8), lambda i, j, k: (i, j)))],
                 lambda acc, rr: (acc * (2.0 * rr.astype(F32)),))[0]
        tk = _tile(T, 2048)
        g["mlp_down"] = _mm("mlp_dwdown", "tn", (1, N_DEV, T // tk),
                            [(sv["r"], _spec((tk, F8), lambda i, j, k: (k, j))), (dy, _spec((tk, D), lambda i, j, k: (k, 0)))],
                            [(wts["mlp_down"].shape, BF16, _spec((None, F8, D), lambda i, j, k: (j, 0, 0)))],
                            lambda acc: (acc,), pre=_square)[0]
        g["mlp_up"] = _mm_cols_grad("mlp_dwup", sv["u2"], da, wts["mlp_up"].shape, (), T)
        du = _mm_whole_k("mlp_du", da, wts["mlp_up"], T, "cols", out_dtype=BF16)
        tok = put_g(l, "mlp", g, du, None)
        g = {}
        dh, dy, gn[4 * l + 1], gn[4 * l + 2], dy_sum = _bound_bwd("bwd_a", dh, du, sv["h_mid"], sv["y1"],
                                                                row(4 * l + 1), row(4 * l + 2), T, D, tr, [] if tok is None else [tok])
        u = sv["u"]
        if mixer == 0:
            g["conv_b_out"] = dy_sum
            ds = _mm_rows_t("conv_ds", dy, wts["conv_w_out"], T)
            g["conv_w_out"] = _mm_rows_grad("conv_dwout", sv["s"], dy, T)

            def _lnb(cv, gv, bv, dsv):
                _, vjp = jax.vjp(_ln_silu, cv, gv, bv)
                dc, dg, db = vjp(dsv)
                return dc, dg, db, _colsum(dc)

            dc, g["conv_ln_g"], g["conv_ln_b"], g["conv_dw_b"] = _rowwise(
                "conv_ln_bwd", _lnb, [(sv["c"], D, 0), (wts["conv_ln_g"], None, None), (wts["conv_ln_b"], None, None), (ds, D, 0)],
                [("row", D, F32), ("acc", (1, D)), ("acc", (1, D)), ("acc", (1, D))], T, tr)
            dv, g["conv_dw"] = _dwconv_bwd("conv_dw_bwd", dc, sv["v"], wts["conv_dw"], T)

            def _glub(za, zg, dvv):
                _, vjp = jax.vjp(_glu, za.astype(F32), zg.astype(F32))
                dza, dzg = vjp(dvv)
                dz = jnp.concatenate([dza, dzg], axis=-1)
                return dz, _colsum(dz)

            dz, g["conv_b_in"] = _rowwise("conv_glu_bwd", _glub, [(sv["z"], D, 0), (sv["z"], D, 1), (dv, D, 0)],
                                          [("row", 2 * D, BF16), ("acc", (1, 2 * D))], T, tr)
            g["conv_w_in"] = _mm_cols_grad("conv_dwin", u, dz, wts["conv_w_in"].shape, (), T)
            du = _mm_cols_t("conv_du", dz, wts["conv_w_in"], (), T, out_dtype=BF16)
        elif mixer == 1:
            def _psb(dyv, yc, sc):
                return dyv * sc, _colsum(dyv.astype(F32) * yc.astype(F32))

            dyg, g["pool_scale"] = _rowwise("pool_scale_bwd", _psb, [(dy, D, 0), (sv["ycat"], D, 0), (wts["pool_scale"], None, None)],
                                            [("row", D, BF16), ("acc", (1, D))], T, tr)
            dmix = _pool_mm("pool_dmix", "nt", dyg, wts["pool_w"], T)
            g["pool_w"] = _pool_mm("pool_dw", "tn", sv["mixed"], dyg, T)
            du = _pool_bwd("pool_bwd", dmix, T)
        elif mixer == 2:
            dm = _mm_rows_t("sc_dm", dy, wts["sc_w_out"], T)
            g["sc_w_out"] = _mm_rows_grad("sc_dwout", sv["m"], dy, T)
            dz, dw3 = _sc_mid_bwd("sc_mid_bwd", sv["z"], dm, wts["sc_dw"], T)
            g["sc_dw"] = dw3[0]
            g["sc_w_in"] = _mm_cols_grad("sc_dwin", u, dz, wts["sc_w_in"].shape, (), T)
            du = _mm_cols_t("sc_du", dz, wts["sc_w_in"], (), T, out_dtype=BF16)
        else:
            dk = D // RET_HEADS
            half = dk // 2
            dout = _mm_rows_t("ret_dout", dy, wts["ret_w_out"], T)
            g["ret_w_out"] = _mm_rows_grad("ret_dwout", sv["out"], dy, T)

            def _postb(ov, gv, dv_):
                _, vjp = jax.vjp(_gate_norm, ov, gv.astype(F32))
                return vjp(dv_)

            do, dzg = _rowwise("ret_post_bwd", _postb, [(sv["o"], 2 * D, 0), (sv["z"], 2 * D, 2), (dout, 2 * D, 0)],
                               [("row", 2 * D, BF16), ("row", 2 * D, BF16)], T, trw)
            dqr, dkr, dvb = _ret_bwd("ret_bwd", sv["qr"], sv["kr"], sv["vb"], do, sv["st"], tabs, T)

            def _preb(dq, dk_, dvv, dgv, cv, sn):
                return jnp.concatenate([_rot(dq, cv, sn, -1.0).astype(BF16), (_rot(dk_, cv, sn, -1.0) * (dk ** -0.5)).astype(BF16),
                                        dvv, dgv], axis=-1)

            dz = _rowwise("ret_pre_bwd", _preb, [(dqr, D, 0), (dkr, D, 0), (dvb, 2 * D, 0), (dzg, 2 * D, 0), (cos, half, 0), (sin, half, 0)],
                          [("row", 6 * D, BF16)], T, trw)[0]
            g["ret_w_in"] = _mm_cols_grad("ret_dwin", u, dz, wts["ret_w_in"].shape, (), T)
            du = _mm_cols_t("ret_du", dz, wts["ret_w_in"], (), T, out_dtype=BF16)
        du_next = du
        if l > 0:
            tok = put_g(l, "mix", g, du, None)

    def _fb(dhv, duv, xv, gv):
        _, vjp = jax.vjp(_rms, xv, gv)
        dx, dg = vjp(duv.astype(F32))
        return dhv + dx, dg

    grad_x, gn[0] = _rowwise("bwd_first", _fb, [(dh, D, 0), (du_next, D, 0), (x, D, 0), (row(0), None, None)],
                             [("row", D, F32), ("acc", (1, D))], T, tr)
    put_g(0, "mix", g, grad_x, jnp.concatenate(gn, axis=0))
    return loss_row, grad_x


def _bound_bwd(name, dh, du, h_new, y, g_post, g_next, T, D, tr, deps=()):
    def fn(dhv, duv, hn, yv, gp, gnx):
        _, vjp1 = jax.vjp(_rms, hn, gnx)
        d1, dgn = vjp1(duv.astype(F32))
        dht = dhv + d1
        _, vjp2 = jax.vjp(_rms, yv.astype(F32), gp)
        dy, dgp = vjp2(dht)
        return dht, dy, dgp, dgn, _colsum(dy)

    return _rowwise(name, fn, [(dh, D, 0), (du, D, 0), (h_new, D, 0), (y, D, 0), (g_post, None, None), (g_next, None, None)],
                    [("row", D, F32), ("row", D, BF16), ("acc", (1, D)), ("acc", (1, D)), ("acc", (1, D))], T, tr, deps)


def _adamw(name, parts, w, m, v, own=None, layer=None, bufs=None):
    P, R, C = parts.shape
    tr = _tile(R, 256)
    n_s = 1 if own is None else 2

    def body(*refs):
        p_ref = refs[0]
        w_ref, m_ref, v_ref = refs[n_s:n_s + 3]
        g_ref, d_ref, m2_ref, v2_ref = refs[-4:]

        def slot(i):
            if own is None:
                return p_ref[i].astype(F32)
            here = 2 * lax.axis_index("x") + lax.axis_index("y") == i
            return jnp.where(here, refs[1][i], p_ref[i]).astype(F32)

        g = slot(0)
        for i in range(1, P):
            g = g + slot(i)
        wv = w_ref[...]
        m2 = ADAM_B1 * m_ref[...] + (1.0 - ADAM_B1) * g
        v2 = ADAM_B2 * v_ref[...] + (1.0 - ADAM_B2) * (g * g)
        m_hat = m2 / (1.0 - ADAM_B1 ** ADAM_STEP)
        v_hat = v2 / (1.0 - ADAM_B2 ** ADAM_STEP)
        g_ref[...] = g
        d_ref[...] = -ADAM_LR * (m_hat / (jnp.sqrt(v_hat) + ADAM_EPS) + ADAM_WD * wv)
        m2_ref[...] = m2
        v2_ref[...] = v2

    pblk = pl.BlockSpec((P, tr, C), lambda i: (0, i, 0))
    stacks = [parts] if own is None else [parts, own]
    if layer is None:
        blk = pl.BlockSpec((tr, C), lambda i: (i, 0))
        return _pcall(body, name=name, grid=(R // tr,),
                      in_specs=[pblk] * n_s + [blk, blk, blk], out_specs=[blk] * 4,
                      out_shape=[jax.ShapeDtypeStruct((R, C), F32)] * 4, compiler_params=_params(("parallel",)))(*stacks, w, m, v)
    blk = pl.BlockSpec((None, tr, C), lambda i: (layer, i, 0))
    return _pcall(body, name=name, grid=(R // tr,),
                  in_specs=[pblk] * n_s + [blk, blk, blk] + [pl.BlockSpec(memory_space=pl.ANY)] * 4, out_specs=[blk] * 4,
                  out_shape=[jax.ShapeDtypeStruct(w.shape, F32)] * 4, input_output_aliases={n_s + 3 + i: i for i in range(4)},
                  compiler_params=_params(("parallel",)))(*stacks, w, m, v, *bufs)


def _pair_add(name, a, b):
    Q, _, R, C = a.shape
    tr = _tile(R, 1024)
    blk = pl.BlockSpec((None, tr, C), lambda q, i: (q, i, 0))

    def body(a_ref, b_ref, o_ref):
        mine = jnp.where(lax.axis_index("c") == 0, a_ref[0], a_ref[1])
        o_ref[...] = (mine.astype(F32) + b_ref[...].astype(F32)).astype(BF16)

    return _pcall(body, name=name, grid=(Q, R // tr),
                  in_specs=[pl.BlockSpec((None, 2, tr, C), lambda q, i: (q, 0, i, 0)), blk], out_specs=blk,
                  out_shape=jax.ShapeDtypeStruct((Q, R, C), BF16), compiler_params=_params(("parallel", "parallel")))(a, b)


ANY = pl.BlockSpec(memory_space=pl.ANY)


def _place():
    x, y, c = lax.axis_index("x"), lax.axis_index("y"), lax.axis_index("c")
    chips = [(1 - x, y), (x, 1 - y), (1 - x, 1 - y)]
    return x, y, c, chips


def _gather_all(locs):
    n = len(locs)

    def body(*refs):
        ins, outs = refs[:n], refs[n:2 * n]
        send, recv, lsem = refs[2 * n:]
        x, y, c, chips = _place()
        me = 4 * x + 2 * y + c
        sib = (x, y, 1 - c)

        def cp(a, k, slot, to, src=None):
            return pltpu.make_async_remote_copy(src_ref=outs[a].at[slot] if src is None else src, dst_ref=outs[a].at[slot],
                                                send_sem=send.at[a, k], recv_sem=recv.at[a, k], device_id=to, device_id_type=MESH)

        mine = [pltpu.make_async_copy(ins[a], outs[a].at[me], lsem.at[a]) for a in range(n)]
        for d in mine:
            d.start()
        first = []
        for a in range(n):
            first.append(cp(a, 0, me, sib, src=ins[a]))
            for j, chip in enumerate(chips):
                first.append(cp(a, 1 + j, me, (chip[0], chip[1], c), src=ins[a]))
        for d in first:
            d.start()
        passed = []
        for j, chip in enumerate(chips):
            slot = 4 * chip[0] + 2 * chip[1] + c
            for a in range(n):
                cp(a, 1 + j, slot, sib).wait_recv()
                p = cp(a, 4 + j, slot, sib)
                p.start()
                passed.append(p)
        for a in range(n):
            cp(a, 0, 4 * x + 2 * y + (1 - c), sib).wait_recv()
        for j, chip in enumerate(chips):
            for a in range(n):
                cp(a, 4 + j, 4 * chip[0] + 2 * chip[1] + (1 - c), sib).wait_recv()
        for d in first + passed:
            d.wait_send()
        for d in mine:
            d.wait()

    return _pcall(body, name="gather_all", in_specs=[ANY] * n, out_specs=[ANY] * n,
                  out_shape=[jax.ShapeDtypeStruct((N_DEV,) + a.shape, a.dtype) for a in locs],
                  scratch_shapes=[pltpu.SemaphoreType.DMA((n, 7)), pltpu.SemaphoreType.DMA((n, 7)), pltpu.SemaphoreType.DMA((n,))],
                  compiler_params=pltpu.CompilerParams(has_side_effects=True))(*locs)


HBM = pl.BlockSpec(memory_space=pltpu.HBM)
SEM = pl.BlockSpec(memory_space=pltpu.SEMAPHORE)
DATAFLOW = pltpu.SideEffectType.DATAFLOW_SIDE_EFFECTING


def _peers():
    x, y, c = lax.axis_index("x"), lax.axis_index("y"), lax.axis_index("c")
    out = []
    for r in range(1, N_DEV):
        px = 1 - x if (r >> 2) & 1 else x
        py = 1 - y if (r >> 1) & 1 else y
        pc = 1 - c if r & 1 else c
        out.append(((px, py, pc), 4 * px + 2 * py + pc))
    return 4 * x + 2 * y + c, out


def _hbm(a):
    return pltpu.with_memory_space_constraint(a, pltpu.HBM)


def _gather_start(name, locs, lands, deps):
    n = len(locs)

    def body(*refs):
        ins, lnd = refs[:n], refs[n:2 * n]
        send, recv = refs[2 * n + len(deps)], refs[2 * n + len(deps) + 1]
        token = refs[-1]
        x, y, c, chips = _place()
        me = 4 * x + 2 * y + c
        for k, to in enumerate([(x, y, 1 - c)] + [(chip[0], chip[1], c) for chip in chips]):
            for a in range(n):
                pltpu.make_async_remote_copy(src_ref=ins[a], dst_ref=lnd[a].at[me], send_sem=send.at[a * 4 + k],
                                             recv_sem=recv.at[a * 4 + k], device_id=to, device_id_type=MESH).start()
        token[...] = jnp.zeros_like(token)

    out_shape = ((pltpu.SemaphoreType.DMA((n * 4,)), pltpu.SemaphoreType.DMA((n * 4,)))
                 + tuple(pltpu.HBM(a.shape, a.dtype) for a in locs) + tuple(pltpu.HBM(a.shape, a.dtype) for a in lands)
                 + (jax.ShapeDtypeStruct((8, LANES), F32),))
    res = _pcall(body, name=name, out_shape=out_shape, in_specs=[HBM] * (2 * n) + [ANY] * len(deps),
                 out_specs=(SEM, SEM) + (HBM,) * (2 * n) + (pl.BlockSpec(memory_space=pltpu.VMEM),),
                 input_output_aliases={a: 2 + a for a in range(2 * n)},
                 compiler_params=pltpu.CompilerParams(has_side_effects=DATAFLOW))(
        *[_hbm(a) for a in locs], *[_hbm(a) for a in lands], *deps)
    return res[0], res[1], res[2:2 + n], res[2 + n:2 + 2 * n], res[-1]


def _gather_pass(name, send, recv, shards, lands, after):
    n = len(shards)

    def body(*refs):
        ins, lnd = refs[:n], refs[n:2 * n]
        recv = refs[2 * n + 1]
        fsend, frecv = refs[2 * n + 3], refs[2 * n + 4]
        token = refs[-1]
        x, y, c, chips = _place()
        for j, chip in enumerate(chips):
            slot = 4 * chip[0] + 2 * chip[1] + c
            for a in range(n):
                pltpu.make_async_remote_copy(src_ref=ins[a], dst_ref=lnd[a].at[slot], send_sem=fsend.at[a * 3 + j],
                                             recv_sem=recv.at[a * 4 + 1 + j], device_id=(chip[0], chip[1], c),
                                             device_id_type=MESH).wait_recv()
                pltpu.make_async_remote_copy(src_ref=lnd[a].at[slot], dst_ref=lnd[a].at[slot], send_sem=fsend.at[a * 3 + j],
                                             recv_sem=frecv.at[a * 3 + j], device_id=(x, y, 1 - c), device_id_type=MESH).start()
        token[...] = jnp.zeros_like(token)

    out_shape = ((pltpu.SemaphoreType.DMA((n * 3,)), pltpu.SemaphoreType.DMA((n * 3,)))
                 + tuple(pltpu.HBM(a.shape, a.dtype) for a in shards) + tuple(pltpu.HBM(a.shape, a.dtype) for a in lands)
                 + (jax.ShapeDtypeStruct((8, LANES), F32),))
    res = _pcall(body, name=name, out_shape=out_shape, in_specs=[HBM] * (2 * n) + [SEM, SEM, ANY],
                 out_specs=(SEM, SEM) + (HBM,) * (2 * n) + (pl.BlockSpec(memory_space=pltpu.VMEM),),
                 input_output_aliases={a: 2 + a for a in range(2 * n)},
                 compiler_params=pltpu.CompilerParams(has_side_effects=DATAFLOW))(*shards, *lands, send, recv, after)
    return res[0], res[1], res[2:2 + n], res[2 + n:2 + 2 * n], res[-1]


def _gather_wait(name, send, recv, fsend, frecv, shards, lands, after):
    n = len(shards)

    def body(*refs):
        ins, lnd = refs[:n], refs[n:2 * n]
        send, recv, fsend, frecv = refs[2 * n:2 * n + 4]
        x, y, c, chips = _place()
        sib = (x, y, 1 - c)
        for a in range(n):
            own = pltpu.make_async_remote_copy(src_ref=ins[a], dst_ref=lnd[a].at[4 * x + 2 * y + (1 - c)], send_sem=send.at[a * 4],
                                               recv_sem=recv.at[a * 4], device_id=sib, device_id_type=MESH)
            own.wait_send()
            own.wait_recv()
            for j, chip in enumerate(chips):
                pltpu.make_async_remote_copy(src_ref=ins[a], dst_ref=lnd[a].at[0], send_sem=send.at[a * 4 + 1 + j],
                                             recv_sem=recv.at[a * 4 + 1 + j], device_id=(chip[0], chip[1], c),
                                             device_id_type=MESH).wait_send()
                fw = pltpu.make_async_remote_copy(src_ref=lnd[a].at[4 * chip[0] + 2 * chip[1] + c],
                                                  dst_ref=lnd[a].at[4 * chip[0] + 2 * chip[1] + (1 - c)], send_sem=fsend.at[a * 3 + j],
                                                  recv_sem=frecv.at[a * 3 + j], device_id=sib, device_id_type=MESH)
                fw.wait_send()
                fw.wait_recv()

    out_shape = tuple(pltpu.HBM(a.shape, a.dtype) for a in shards) + tuple(pltpu.HBM(a.shape, a.dtype) for a in lands)
    res = _pcall(body, name=name, out_shape=out_shape, in_specs=[HBM] * (2 * n) + [SEM, SEM, SEM, SEM, ANY], out_specs=(HBM,) * (2 * n),
                 input_output_aliases={a: a for a in range(2 * n)},
                 compiler_params=pltpu.CompilerParams(has_side_effects=DATAFLOW))(*shards, *lands, send, recv, fsend, frecv, after)
    return res[n:]


def _chip_start(name, cs, lands):
    n = len(cs)

    def body(*refs):
        ins, lnd = refs[:n], refs[n:2 * n]
        send, recv = refs[2 * n], refs[2 * n + 1]
        token = refs[-1]
        x, y, c, chips = _place()
        qme = 2 * x + y
        for j, chip in enumerate(chips):
            for a in range(n):
                pltpu.make_async_remote_copy(src_ref=ins[a].at[2 * chip[0] + chip[1]], dst_ref=lnd[a].at[qme], send_sem=send.at[a * 3 + j],
                                             recv_sem=recv.at[a * 3 + j], device_id=(chip[0], chip[1], c), device_id_type=MESH).start()
        token[...] = jnp.zeros_like(token)

    out_shape = ((pltpu.SemaphoreType.DMA((n * 3,)), pltpu.SemaphoreType.DMA((n * 3,)))
                 + tuple(pltpu.HBM(a.shape, a.dtype) for a in cs) + tuple(pltpu.HBM(a.shape, a.dtype) for a in lands)
                 + (jax.ShapeDtypeStruct((8, LANES), F32),))
    res = _pcall(body, name=name, out_shape=out_shape, in_specs=[HBM] * (2 * n),
                 out_specs=(SEM, SEM) + (HBM,) * (2 * n) + (pl.BlockSpec(memory_space=pltpu.VMEM),),
                 input_output_aliases={a: 2 + a for a in range(2 * n)},
                 compiler_params=pltpu.CompilerParams(has_side_effects=DATAFLOW))(*[_hbm(a) for a in cs], *[_hbm(a) for a in lands])
    return res[0], res[1], res[2:2 + n], res[2 + n:2 + 2 * n], res[-1]


def _chip_wait(name, send, recv, cs, lands, after):
    n = len(cs)

    def body(*refs):
        ins, lnd = refs[:n], refs[n:2 * n]
        send, recv = refs[2 * n], refs[2 * n + 1]
        x, y, c, chips = _place()
        for j, chip in enumerate(chips):
            q = 2 * chip[0] + chip[1]
            for a in range(n):
                cp = pltpu.make_async_remote_copy(src_ref=ins[a].at[q], dst_ref=lnd[a].at[q], send_sem=send.at[a * 3 + j],
                                                  recv_sem=recv.at[a * 3 + j], device_id=(chip[0], chip[1], c), device_id_type=MESH)
                cp.wait_send()
                cp.wait_recv()

    out_shape = tuple(pltpu.HBM(a.shape, a.dtype) for a in cs) + tuple(pltpu.HBM(a.shape, a.dtype) for a in lands)
    res = _pcall(body, name=name, out_shape=out_shape, in_specs=[HBM] * (2 * n) + [SEM, SEM, ANY], out_specs=(HBM,) * (2 * n),
                 input_output_aliases={a: a for a in range(2 * n)},
                 compiler_params=pltpu.CompilerParams(has_side_effects=DATAFLOW))(*cs, *lands, send, recv, after)
    return res[:n], res[n:]


def _pair_exchange(name, gs):
    n = len(gs)

    def body(*refs):
        ins, got = refs[:n], refs[n:2 * n]
        send, recv = refs[2 * n:]
        x, y, c, _ = _place()
        sib = (x, y, 1 - c)
        rem = [pltpu.make_async_remote_copy(src_ref=ins[a].at[q, 1 - c], dst_ref=got[a].at[q], send_sem=send.at[a, q],
                                            recv_sem=recv.at[a, q], device_id=sib, device_id_type=MESH)
               for a in range(n) for q in range(N_CHIP)]
        for d in rem:
            d.start()
        for d in rem:
            d.wait()

    half = [jax.ShapeDtypeStruct((N_CHIP,) + g.shape[2:], g.dtype) for g in gs]
    return _pcall(body, name=name, in_specs=[ANY] * n, out_specs=[ANY] * n, out_shape=half,
                  scratch_shapes=[pltpu.SemaphoreType.DMA((n, N_CHIP)), pltpu.SemaphoreType.DMA((n, N_CHIP))],
                  compiler_params=pltpu.CompilerParams(has_side_effects=True))(*gs)


def _chip_exchange(cs, sm_sh, sm_rep):
    n = len(cs)

    def body(*refs):
        ins, sh, rep = refs[:n], refs[n], refs[n + 1]
        outs, osh, orep = refs[n + 2:2 * n + 2], refs[2 * n + 2], refs[2 * n + 3]
        send, recv, lsem, ssend, srecv, slsem = refs[2 * n + 4:]
        x, y, c, chips = _place()
        qme = 2 * x + y
        me = 4 * x + 2 * y + c
        loc = [pltpu.make_async_copy(ins[a].at[qme], outs[a].at[qme], lsem.at[a]) for a in range(n)]
        loc.append(pltpu.make_async_copy(sh.at[me], osh.at[me], slsem.at[0]))
        loc.append(pltpu.make_async_copy(rep, orep.at[me], slsem.at[1]))
        rem, waits = [], []
        for j, chip in enumerate(chips):
            q = 2 * chip[0] + chip[1]
            to = (chip[0], chip[1], c)
            for a in range(n):
                rem.append(pltpu.make_async_remote_copy(src_ref=ins[a].at[q], dst_ref=outs[a].at[qme], send_sem=send.at[a, j],
                                                        recv_sem=recv.at[a, j], device_id=to, device_id_type=MESH))
                waits.append(pltpu.make_async_remote_copy(src_ref=ins[a].at[q], dst_ref=outs[a].at[q], send_sem=send.at[a, j],
                                                          recv_sem=recv.at[a, j], device_id=to, device_id_type=MESH))
        for r in range(1, N_DEV):
            fx, fy, fc = (r >> 2) & 1, (r >> 1) & 1, r & 1
            px = 1 - x if fx else x
            py = 1 - y if fy else y
            pc = 1 - c if fc else c
            t = 4 * px + 2 * py + pc
            to = (px, py, pc)
            rem.append(pltpu.make_async_remote_copy(src_ref=sh.at[t], dst_ref=osh.at[me], send_sem=ssend.at[0, r - 1],
                                                    recv_sem=srecv.at[0, r - 1], device_id=to, device_id_type=MESH))
            waits.append(pltpu.make_async_remote_copy(src_ref=sh.at[t], dst_ref=osh.at[t], send_sem=ssend.at[0, r - 1],
                                                      recv_sem=srecv.at[0, r - 1], device_id=to, device_id_type=MESH))
            rem.append(pltpu.make_async_remote_copy(src_ref=rep, dst_ref=orep.at[me], send_sem=ssend.at[1, r - 1],
                                                    recv_sem=srecv.at[1, r - 1], device_id=to, device_id_type=MESH))
            waits.append(pltpu.make_async_remote_copy(src_ref=rep, dst_ref=orep.at[t], send_sem=ssend.at[1, r - 1],
                                                      recv_sem=srecv.at[1, r - 1], device_id=to, device_id_type=MESH))
        for d in loc + rem:
            d.start()
        for d in waits:
            d.wait_recv()
        for d in rem:
            d.wait_send()
        for d in loc:
            d.wait()

    out_shape = [jax.ShapeDtypeStruct(a.shape, a.dtype) for a in cs]
    out_shape += [jax.ShapeDtypeStruct(sm_sh.shape, F32), jax.ShapeDtypeStruct((N_DEV,) + sm_rep.shape, F32)]
    res = _pcall(body, name="chip_exchange", in_specs=[ANY] * (n + 2), out_specs=[ANY] * (n + 2), out_shape=out_shape,
                 scratch_shapes=[pltpu.SemaphoreType.DMA((n, 3)), pltpu.SemaphoreType.DMA((n, 3)), pltpu.SemaphoreType.DMA((n,)),
                                 pltpu.SemaphoreType.DMA((2, 7)), pltpu.SemaphoreType.DMA((2, 7)), pltpu.SemaphoreType.DMA((2,))],
                 compiler_params=pltpu.CompilerParams(has_side_effects=True))(*cs, sm_sh, sm_rep)
    return res[:n], res[n], res[n + 1]


BIG = ("mlp_up", "mlp_down", "conv_w_in", "conv_w_out", "pool_w", "sc_w_in", "sc_w_out", "ret_w_in", "ret_w_out")
SMALL_SPLIT = ("norm_g", "conv_dw", "sc_dw")
SMALL_REP = ("conv_b_in", "conv_dw_b", "conv_ln_g", "conv_ln_b", "conv_b_out", "pool_scale")
WEIGHTS = ("norm_g", "mlp_up", "mlp_down", "conv_w_in", "conv_b_in", "conv_dw", "conv_dw_b", "conv_ln_g", "conv_ln_b",
           "conv_w_out", "conv_b_out", "pool_w", "pool_scale", "sc_w_in", "sc_dw", "sc_w_out", "ret_w_in", "ret_w_out")


def _pad_rows(a, rows):
    return jnp.pad(a, ((0, rows - a.shape[0]), (0, 0)))


def _split_rows(names, shards):
    out = []
    for nme in names:
        r = int(np.prod(shards[nme].shape[:-1]))
        out.append((nme, r, -(-r // 8) * 8))
    return out


def _pack_split(names, arrs):
    parts = []
    for nme, r, rp in _split_rows(names, arrs):
        parts.append(_pad_rows(arrs[nme].reshape(r, arrs[nme].shape[-1]), rp))
    return jnp.concatenate(parts, axis=0)


def _pack_rep(arrs, D):
    parts = [arrs[nme].reshape(-1, D) for nme in SMALL_REP]
    rows = sum(p.shape[0] for p in parts)
    return _pad_rows(jnp.concatenate(parts, axis=0), -(-rows // 8) * 8)


def kernel(x, positions, norm_g, mlp_up, mlp_down, conv_w_in, conv_b_in, conv_dw, conv_dw_b, conv_ln_g, conv_ln_b, conv_w_out, conv_b_out, pool_w, pool_scale, sc_w_in, sc_dw, sc_w_out, ret_w_in, ret_w_out, loss_target, m_norm_g, m_mlp_up, m_mlp_down, m_conv_w_in, m_conv_b_in, m_conv_dw, m_conv_dw_b, m_conv_ln_g, m_conv_ln_b, m_conv_w_out, m_conv_b_out, m_pool_w, m_pool_scale, m_sc_w_in, m_sc_dw, m_sc_w_out, m_ret_w_in, m_ret_w_out, v_norm_g, v_mlp_up, v_mlp_down, v_conv_w_in, v_conv_b_in, v_conv_dw, v_conv_dw_b, v_conv_ln_g, v_conv_ln_b, v_conv_w_out, v_conv_b_out, v_pool_w, v_pool_scale, v_sc_w_in, v_sc_dw, v_sc_w_out, v_ret_w_in, v_ret_w_out):
    w = dict(norm_g=norm_g, mlp_up=mlp_up, mlp_down=mlp_down, conv_w_in=conv_w_in, conv_b_in=conv_b_in, conv_dw=conv_dw,
             conv_dw_b=conv_dw_b, conv_ln_g=conv_ln_g, conv_ln_b=conv_ln_b, conv_w_out=conv_w_out, conv_b_out=conv_b_out,
             pool_w=pool_w, pool_scale=pool_scale, sc_w_in=sc_w_in, sc_dw=sc_dw, sc_w_out=sc_w_out, ret_w_in=ret_w_in, ret_w_out=ret_w_out)
    m = dict(norm_g=m_norm_g, mlp_up=m_mlp_up, mlp_down=m_mlp_down, conv_w_in=m_conv_w_in, conv_b_in=m_conv_b_in, conv_dw=m_conv_dw,
             conv_dw_b=m_conv_dw_b, conv_ln_g=m_conv_ln_g, conv_ln_b=m_conv_ln_b, conv_w_out=m_conv_w_out, conv_b_out=m_conv_b_out,
             pool_w=m_pool_w, pool_scale=m_pool_scale, sc_w_in=m_sc_w_in, sc_dw=m_sc_dw, sc_w_out=m_sc_w_out, ret_w_in=m_ret_w_in,
             ret_w_out=m_ret_w_out)
    v = dict(norm_g=v_norm_g, mlp_up=v_mlp_up, mlp_down=v_mlp_down, conv_w_in=v_conv_w_in, conv_b_in=v_conv_b_in, conv_dw=v_conv_dw,
             conv_dw_b=v_conv_dw_b, conv_ln_g=v_conv_ln_g, conv_ln_b=v_conv_ln_b, conv_w_out=v_conv_w_out, conv_b_out=v_conv_b_out,
             pool_w=v_pool_w, pool_scale=v_pool_scale, sc_w_in=v_sc_w_in, sc_dw=v_sc_dw, sc_w_out=v_sc_w_out, ret_w_in=v_ret_w_in,
             ret_w_out=v_ret_w_out)
    T, D = x.shape[1], x.shape[2]
    L = D // N_DEV
    G = len(POOL_WINDOWS)
    cg = D // G

    me = 4 * lax.axis_index("x") + 2 * lax.axis_index("y") + lax.axis_index("c")
    MIX = (("conv_w_in", "conv_w_out"), ("pool_w",), ("sc_w_in", "sc_w_out"), ("ret_w_in", "ret_w_out"))

    def shards(l, part):
        out = {k: w[k].astype(BF16) for k in MIX[l]} if part != "mlp" else {}
        if part != "mix":
            out["mlp_up"], out["mlp_down"] = w["mlp_up"][l].astype(BF16), w["mlp_down"][l].astype(BF16)
        return out

    sh0 = shards(0, "mix")
    gath0 = _gather_all(list(sh0.values()) + [_pack_split(SMALL_SPLIT, w)])
    small = gath0[-1]
    common = {k: w[k].reshape(1, -1) for k in SMALL_REP}
    off = 0
    for nme, r, rp in _split_rows(SMALL_SPLIT, w):
        common[nme] = small[:, off:off + r, :].transpose(1, 0, 2).reshape(r, D)
        off += rp
    flying = {}
    dep = small
    for l, part in [(0, "mlp")] + [(l, "both") for l in range(1, DEPTH)]:
        shl = shards(l, part)
        lands = [lax.dynamic_update_slice(lax.empty((N_DEV,) + a.shape, a.dtype), a[None], (me,) + (0,) * a.ndim) for a in shl.values()]
        send, recv, thru, lands, dep = _gather_start("gather_start_%d" % l, list(shl.values()), lands, [dep])
        flying[l] = dict(names=list(shl.keys()), send=send, recv=recv, thru=thru, lands=lands)

    def pass_on(l, after):
        f = flying[l]
        f["fsend"], f["frecv"], f["thru"], f["lands"], tok = _gather_pass("gather_pass_%d" % l, f["send"], f["recv"], f["thru"],
                                                                          f["lands"], after)
        return tok

    def arrive(l, after):
        f = flying[l]
        got = _gather_wait("gather_wait_%d" % l, f["send"], f["recv"], f["fsend"], f["frecv"], f["thru"], f["lands"], after)
        return dict(zip(f["names"], got))

    def get_w(l, part, after):
        if part == "mlp" and l > 0:
            return {"_deps": [pass_on(l + 1, after)]} if l + 1 < DEPTH else {}
        if l == 0 and part == "mix":
            gw = dict(zip(sh0.keys(), gath0[:-1]))
        elif l == 0:
            gw = arrive(0, pass_on(0, after))
            gw["_deps"] = [pass_on(1, gw["mlp_up"])]
        else:
            gw = arrive(l, after)
        out = dict(common)
        for k, a in gw.items():
            if k.endswith("_w_out"):
                out[k] = a.reshape(-1, D)
            elif k == "pool_w":
                out[k] = a.transpose(1, 0, 2, 3).reshape(G, cg, cg)
            else:
                out[k] = a
        return out

    res = {}
    mlp_res = {k: [lax.empty(w[k].shape, F32) for _ in range(4)] for k in ("mlp_up", "mlp_down")}
    pending = []
    small_g = {}

    def shard2d(src, k, l):
        a = src[k][l] if k in mlp_res else src[k]
        return a.reshape(-1, a.shape[-1])

    def adam_layer(l, names, parts, own):
        for i, k in enumerate(names):
            mine = None if own is None else own[i]
            if k in mlp_res:
                mlp_res[k] = _adamw("adamw_%s_%d" % (k, l), parts[i], w[k], m[k], v[k], mine, layer=l, bufs=mlp_res[k])
            else:
                out = _adamw("adamw_%s_%d" % (k, l), parts[i], shard2d(w, k, l), shard2d(m, k, l), shard2d(v, k, l), mine)
                res[k] = [o.reshape(w[k].shape) for o in out]

    held = {}

    def put_g(l, part, g, after, norm_grads):
        small_g.update({k: g[k] for k in g if k in SMALL_SPLIT or k in SMALL_REP})
        if part == "mlp" and l > 0:
            held.update(g)
            return None
        if pending:
            tag, lp, names, send, recv, sums, lands = pending.pop()
            sums, parts = _chip_wait("chip_wait_" + tag, send, recv, sums, lands, after)
            adam_layer(lp, names, parts, sums)
        last = l == 0 and part == "mix"
        if l > 0:
            g = dict(g, **held)
            held.clear()
            names = list(MIX[l]) + ["mlp_up", "mlp_down"]
        else:
            names = list(MIX[0]) if last else ["mlp_up", "mlp_down"]
        tag = "%d%s" % (l, "" if l > 0 else part)
        stacks = []
        for k in names:
            a = g[k]
            if k == "pool_w":
                a = a.reshape(G, N_DEV, cg // N_DEV, cg).transpose(1, 0, 2, 3).astype(BF16)
            stacks.append(a.reshape(N_CHIP, 2, -1, w[k].shape[-1]))
        got = _pair_exchange("pair_exchange_" + tag, stacks)
        sums = [_pair_add("pair_add", a, b) for a, b in zip(stacks, got)]
        if not last:
            send, recv, sums, lands, tok = _chip_start("chip_start_" + tag, sums, [lax.empty(a.shape, a.dtype) for a in sums])
            pending.append((tag, l, names, send, recv, sums, lands))
            return tok
        small_g["norm_g"] = norm_grads
        gsm = []
        for nme, r, rp in _split_rows(SMALL_SPLIT, w):
            gsm.append(jnp.pad(small_g[nme].reshape(r, N_DEV, L), ((0, rp - r), (0, 0), (0, 0))))
        sm_sh = jnp.concatenate(gsm, axis=0).transpose(1, 0, 2)
        parts, r_sh, r_rep = _chip_exchange(sums, sm_sh, _pack_rep(small_g, D))
        adam_layer(0, names, parts, None)
        out = _adamw("adamw_split", r_sh, _pack_split(SMALL_SPLIT, w), _pack_split(SMALL_SPLIT, m), _pack_split(SMALL_SPLIT, v))
        off = 0
        for nme, r, rp in _split_rows(SMALL_SPLIT, w):
            res[nme] = [o[off:off + r].reshape(w[nme].shape) for o in out]
            off += rp
        out = _adamw("adamw_rep", r_rep, _pack_rep(w, D), _pack_rep(m, D), _pack_rep(v, D))
        off = 0
        for nme in SMALL_REP:
            r = w[nme].shape[0] // D
            res[nme] = [o[off:off + r].reshape(w[nme].shape) for o in out]
            off += r
        return None

    pos = positions.reshape(T, 1).astype(F32)
    loss_row, grad_x = _local_step(x.reshape(T, D), pos, loss_target.reshape(T, D), common["norm_g"], get_w, put_g, [dep], T, D)
    loss = lax.psum(0.5 * jnp.sum(loss_row) / D, AXES)
    res.update(mlp_res)
    outs = [loss, grad_x.reshape(x.shape)]
    for i in range(4):
        outs += [res[k][i] for k in WEIGHTS]
    return tuple(outs)
```

```python
import functools
import math

import numpy as np
import jax
import jax.numpy as jnp
from jax import lax
from jax.experimental import pallas as pl
from jax.experimental.pallas import tpu as pltpu

F32, BF16 = jnp.float32, jnp.bfloat16
MESH = pl.DeviceIdType.MESH
AXES = ("x", "y", "c")
N_DEV = 8
N_CHIP = 4
EPS = 1e-6
DEPTH = 4
CONV_WIDTH = 31
POOL_WINDOWS = (2, 4, 8, 16)
SHORT_CONV_WIDTH = 3
RET_HEADS = 4
RET_CHUNK = 128
ROPE_BASE = 10000.0
ADAM_LR, ADAM_B1, ADAM_B2, ADAM_EPS, ADAM_WD, ADAM_STEP = 0.001, 0.9, 0.999, 1e-08, 0.01, 10
LANES = 128
SEQ_PAD = 32
VMEM_LIMIT = 56 * 1024 * 1024

NN = (((1,), (0,)), ((), ()))
NT = (((1,), (1,)), ((), ()))
TN = (((0,), (0,)), ((), ()))


def _pcall(body, **kw):
    return pl.pallas_call(body, **kw)


def _params(sem):
    return pltpu.CompilerParams(dimension_semantics=sem, vmem_limit_bytes=VMEM_LIMIT)


def _tile(n, pref):
    t = min(n, pref)
    while n % t:
        t //= 2
    return t


def _rowwise(name, fn, ins, outs, rows, tr, deps=()):
    n_in = len(ins)
    n_dep = len(deps)
    in_specs = []
    for arr, nc, cb in ins:
        if nc is None:
            in_specs.append(pl.BlockSpec(arr.shape, lambda i, nd=arr.ndim: (0,) * nd))
        else:
            in_specs.append(pl.BlockSpec((tr, nc), lambda i, cb=cb: (i, cb)))
    out_specs, out_shapes = [], []
    for o in outs:
        if o[0] == "row":
            out_shapes.append(jax.ShapeDtypeStruct((rows, o[1]), o[2]))
            out_specs.append(pl.BlockSpec((tr, o[1]), lambda i: (i, 0)))
        else:
            out_shapes.append(jax.ShapeDtypeStruct(o[1], F32))
            out_specs.append(pl.BlockSpec(o[1], lambda i: (0, 0)))

    in_specs = in_specs + [pl.BlockSpec(memory_space=pl.ANY)] * n_dep

    def body(*refs):
        res = fn(*[r[...] for r in refs[:n_in]])
        if not isinstance(res, (tuple, list)):
            res = (res,)
        i = pl.program_id(0)
        for o, ref, r in zip(outs, refs[n_in + n_dep:], res):
            if o[0] == "row":
                ref[...] = r.astype(o[2])
            else:
                @pl.when(i == 0)
                def _(ref=ref, r=r):
                    ref[...] = r

                @pl.when(i > 0)
                def _(ref=ref, r=r):
                    ref[...] += r

    return _pcall(body, name=name, grid=(rows // tr,), in_specs=in_specs, out_specs=out_specs,
                  out_shape=out_shapes, compiler_params=_params(("arbitrary",)))(*[a for a, _, _ in ins], *deps)


def _rms(x, g):
    return x * lax.rsqrt(jnp.mean(x * x, axis=-1, keepdims=True) + EPS) * g


def _colsum(x):
    return jnp.sum(x, axis=0, keepdims=True)


def _square(r):
    rf = r.astype(F32)
    return (rf * rf).astype(r.dtype)


def _ln_silu(c, g, b):
    mu = jnp.mean(c, axis=-1, keepdims=True)
    xc = c - mu
    y = xc * lax.rsqrt(jnp.mean(xc * xc, axis=-1, keepdims=True) + EPS) * g + b
    return y * jax.nn.sigmoid(y)


def _glu(za, zg):
    return za * jax.nn.sigmoid(zg)


def _gate_norm(o, g):
    dv = o.shape[-1] // RET_HEADS
    parts = []
    for h in range(RET_HEADS):
        oh = o[:, h * dv:(h + 1) * dv]
        parts.append(oh * lax.rsqrt(jnp.mean(oh * oh, axis=-1, keepdims=True) + EPS))
    return (g * jax.nn.sigmoid(g)) * jnp.concatenate(parts, axis=-1)


def _rot(x, cos, sin, sign):
    dk = x.shape[-1] // RET_HEADS
    half = dk // 2
    parts = []
    for h in range(RET_HEADS):
        x1 = x[:, h * dk:h * dk + half]
        x2 = x[:, h * dk + half:(h + 1) * dk]
        parts.append(x1 * cos - sign * (x2 * sin))
        parts.append(sign * (x1 * sin) + x2 * cos)
    return jnp.concatenate(parts, axis=-1)


def _mm(name, mode, grid, ins, outs, epi=None, aliases=None, pre=None, deps=()):
    dn = {"nn": NN, "nt": NT, "tn": TN}[mode]
    gk = grid[2]
    n_in, n_out = len(ins), len(outs)
    n_dep = len(deps)

    def body(*refs):
        out_refs = refs[n_in + n_dep:n_in + n_dep + n_out]
        a = refs[0][...] if pre is None else pre(refs[0][...])
        part = lax.dot_general(a, refs[1][...], dn, preferred_element_type=F32)

        def finish(acc):
            res = epi(acc, *[r[...] for r in refs[2:n_in]]) if epi is not None else (acc,)
            for o, r in zip(out_refs, res):
                o[...] = r.astype(o.dtype)

        if gk == 1:
            finish(part)
        else:
            acc_ref = refs[-1]
            k = pl.program_id(2)

            @pl.when(k == 0)
            def _():
                acc_ref[...] = part

            @pl.when(k > 0)
            def _():
                acc_ref[...] += part

            @pl.when(k == gk - 1)
            def _():
                finish(acc_ref[...])

    scratch = []
    if gk > 1:
        blk = [d for d in outs[0][2].block_shape if d is not None]
        scratch = [pltpu.VMEM(tuple(blk), F32)]
    kw = {}
    if aliases:
        kw["input_output_aliases"] = aliases
    return _pcall(body, name=name, grid=grid, in_specs=[s for _, s in ins] + [pl.BlockSpec(memory_space=pl.ANY)] * n_dep,
                  out_specs=[s for _, _, s in outs],
                  out_shape=[jax.ShapeDtypeStruct(sh, dt) for sh, dt, _ in outs], scratch_shapes=scratch,
                  compiler_params=_params(("parallel", "parallel", "arbitrary")), **kw)(*[a for a, _ in ins], *deps)


def _spec(block, fn):
    return pl.BlockSpec(block, fn)


def _mm_cols(name, a, wg, lead, T, epi=None, extras=(), n_out=1, out_dtype=F32, deps=()):
    K, nb = wg.shape[-2], wg.shape[-1]
    tm = _tile(T, 2048)
    wblock = (None,) * (1 + len(lead)) + (K, nb)
    ins = [(a, _spec((tm, K), lambda i, j, k: (i, 0))), (wg, _spec(wblock, lambda i, j, k: (j,) + tuple(lead) + (0, 0)))]
    for e in extras:
        if e.shape[0] == 1:
            ins.append((e, _spec((1, nb), lambda i, j, k: (0, j))))
        else:
            ins.append((e, _spec((tm, nb), lambda i, j, k: (i, j))))
    outs = [((T, N_DEV * nb), out_dtype, _spec((tm, nb), lambda i, j, k: (i, j))) for _ in range(n_out)]
    return _mm(name, "nn", (T // tm, N_DEV, 1), ins, outs, epi, deps=deps)


def _mm_cols_grad(name, a, d, wg_shape, lead, T):
    K, nb = wg_shape[-2], wg_shape[-1]
    tk = T
    wblock = (None,) * (1 + len(lead)) + (K, nb)
    ins = [(a, _spec((tk, K), lambda i, j, k: (k, 0))), (d, _spec((tk, nb), lambda i, j, k: (k, j)))]
    outs = [(tuple(wg_shape), BF16, _spec(wblock, lambda i, j, k: (j,) + tuple(lead) + (0, 0)))]
    return _mm(name, "tn", (1, N_DEV, T // tk), ins, outs, lambda acc: (acc,))[0]


def _mm_whole_k(name, a, w3, T, split, pre=None, out_dtype=F32):
    nblk, d1, d2 = w3.shape
    kb = d1 if split == "rows" else d2
    N = d2 if split == "rows" else d1
    tm = _tile(T, 512)

    def body(a_ref, w_ref, o_ref, *scratch):
        av = a_ref[...] if pre is None else pre(a_ref[...])
        if split == "rows":
            o_ref[...] = lax.dot_general(av, w_ref[...].reshape(nblk * kb, N), NN, preferred_element_type=F32).astype(out_dtype)
        else:
            wcat, sem = scratch

            @pl.when(pl.program_id(0) == 0)
            def _():
                cps = [pltpu.make_async_copy(w_ref.at[j], wcat.at[:, pl.ds(j * kb, kb)], sem.at[j]) for j in range(nblk)]
                for cp in cps:
                    cp.start()
                for cp in cps:
                    cp.wait()

            o_ref[...] = lax.dot_general(av, wcat[...], NT, preferred_element_type=F32).astype(out_dtype)

    rows = split == "rows"
    scratch = [] if rows else [pltpu.VMEM((N, nblk * kb), w3.dtype), pltpu.SemaphoreType.DMA((nblk,))]
    return _pcall(body, name=name, grid=(T // tm,),
                  in_specs=[pl.BlockSpec((tm, nblk * kb), lambda i: (i, 0)),
                            pl.BlockSpec(w3.shape, lambda i: (0, 0, 0)) if rows else pl.BlockSpec(memory_space=pl.ANY)],
                  out_specs=pl.BlockSpec((tm, N), lambda i: (i, 0)), out_shape=jax.ShapeDtypeStruct((T, N), out_dtype),
                  scratch_shapes=scratch, compiler_params=_params(("arbitrary",)))(a, w3)


def _mm_rows(name, a, w2, T, epi=None, extras=(), out_dtype=F32):
    Kin, N = w2.shape
    tm, tn = _tile(T, 2048), _tile(N, 512)
    ins = [(a, _spec((tm, Kin), lambda i, j, k: (i, 0))), (w2, _spec((Kin, tn), lambda i, j, k: (0, j)))]
    for e in extras:
        ins.append((e, _spec((1, tn), lambda i, j, k: (0, j))))
    outs = [((T, N), out_dtype, _spec((tm, tn), lambda i, j, k: (i, j)))]
    return _mm(name, "nn", (T // tm, N // tn, 1), ins, outs, epi)[0]


def _mm_rows_t(name, d, w2, T, out_dtype=F32):
    Kin, N = w2.shape
    tm, tn = _tile(T, 2048), _tile(Kin, 512)
    ins = [(d, _spec((tm, N), lambda i, j, k: (i, 0))), (w2, _spec((tn, N), lambda i, j, k: (j, 0)))]
    outs = [((T, Kin), out_dtype, _spec((tm, tn), lambda i, j, k: (i, j)))]
    return _mm(name, "nt", (T // tm, Kin // tn, 1), ins, outs)[0]


def _mm_rows_grad(name, a, d, T):
    Kin, N = a.shape[1], d.shape[1]
    tk, tm, tn = _tile(T, 2048), _tile(Kin, 512), _tile(N, 1024)
    ins = [(a, _spec((tk, tm), lambda i, j, k: (k, i))), (d, _spec((tk, tn), lambda i, j, k: (k, j)))]
    outs = [((Kin, N), BF16, _spec((tm, tn), lambda i, j, k: (i, j)))]
    return _mm(name, "tn", (Kin // tm, N // tn, T // tk), ins, outs, lambda acc: (acc,))[0]


def _dwconv_fwd(name, x, w, bias, T):
    W, C = w.shape
    cb = min(C, LANES)
    tc = _tile(T, 256)

    def body(x_ref, w_ref, b_ref, o_ref, xp_ref):
        xp_ref[pl.ds(0, SEQ_PAD), :] = jnp.zeros((SEQ_PAD, cb), F32)

        def fill(c, _):
            base = pl.multiple_of(c * tc, tc)
            xp_ref[pl.ds(base + SEQ_PAD, tc), :] = x_ref[pl.ds(base, tc), :]
            return 0

        lax.fori_loop(0, T // tc, fill, 0)

        def chunk(c, _):
            base = pl.multiple_of(c * tc, tc)
            acc = jnp.zeros((tc, cb), F32) + b_ref[...]
            for k in range(W):
                acc = acc + w_ref[pl.ds(k, 1), :] * xp_ref[pl.ds(base + SEQ_PAD - (W - 1) + k, tc), :]
            o_ref[pl.ds(base, tc), :] = acc
            return 0

        lax.fori_loop(0, T // tc, chunk, 0)

    return _pcall(body, name=name, grid=(C // cb,),
                  in_specs=[pl.BlockSpec((T, cb), lambda i: (0, i)), pl.BlockSpec((W, cb), lambda i: (0, i)),
                            pl.BlockSpec((1, cb), lambda i: (0, i))],
                  out_specs=pl.BlockSpec((T, cb), lambda i: (0, i)), out_shape=jax.ShapeDtypeStruct((T, C), F32),
                  scratch_shapes=[pltpu.VMEM((T + SEQ_PAD, cb), F32)],
                  compiler_params=_params(("parallel",)))(x, w, bias)


def _dwconv_bwd(name, dc, x, w, T):
    W, C = w.shape
    cb = min(C, LANES)
    tc = _tile(T, 128)

    def body(dc_ref, x_ref, w_ref, dx_ref, dw_ref, xp_ref, dp_ref, acc_ref):
        xp_ref[pl.ds(0, SEQ_PAD), :] = jnp.zeros((SEQ_PAD, cb), F32)
        dp_ref[pl.ds(T, SEQ_PAD), :] = jnp.zeros((SEQ_PAD, cb), F32)
        acc_ref[...] = jnp.zeros((W * 8, cb), F32)

        def fill(c, _):
            base = pl.multiple_of(c * tc, tc)
            xp_ref[pl.ds(base + SEQ_PAD, tc), :] = x_ref[pl.ds(base, tc), :]
            dp_ref[pl.ds(base, tc), :] = dc_ref[pl.ds(base, tc), :]
            return 0

        lax.fori_loop(0, T // tc, fill, 0)

        def chunk(c, _):
            base = pl.multiple_of(c * tc, tc)
            d = dc_ref[pl.ds(base, tc), :]
            acc = jnp.zeros((tc, cb), F32)
            for k in range(W):
                acc = acc + w_ref[pl.ds(k, 1), :] * dp_ref[pl.ds(base + (W - 1) - k, tc), :]
                prod = d * xp_ref[pl.ds(base + SEQ_PAD - (W - 1) + k, tc), :]
                acc_ref[pl.ds(8 * k, 8), :] += jnp.sum(prod.reshape(tc // 8, 8, cb), axis=0)
            dx_ref[pl.ds(base, tc), :] = acc
            return 0

        lax.fori_loop(0, T // tc, chunk, 0)
        for k in range(W):
            dw_ref[pl.ds(k, 1), :] = jnp.sum(acc_ref[pl.ds(8 * k, 8), :], axis=0, keepdims=True)

    return _pcall(body, name=name, grid=(C // cb,),
                  in_specs=[pl.BlockSpec((T, cb), lambda i: (0, i)), pl.BlockSpec((T, cb), lambda i: (0, i)),
                            pl.BlockSpec((W, cb), lambda i: (0, i))],
                  out_specs=[pl.BlockSpec((T, cb), lambda i: (0, i)), pl.BlockSpec((W, cb), lambda i: (0, i))],
                  out_shape=[jax.ShapeDtypeStruct((T, C), F32), jax.ShapeDtypeStruct((W, C), F32)],
                  scratch_shapes=[pltpu.VMEM((T + SEQ_PAD, cb), F32), pltpu.VMEM((T + SEQ_PAD, cb), F32),
                                  pltpu.VMEM((W * 8, cb), F32)],
                  compiler_params=_params(("parallel",)))(dc, x, w)


def _pool_count(base, tc, cg, win):
    t = (lax.broadcasted_iota(jnp.int32, (tc, cg), 0) + base + 1).astype(F32)
    return jnp.minimum(t, float(win))


def _pool_fwd(name, u, T):
    D = u.shape[1]
    G = len(POOL_WINDOWS)
    cg = min(D // G, LANES)
    nb = (D // G) // cg
    tc = _tile(T, 128)

    def body(u_ref, o_ref, up_ref):
        g = pl.program_id(0)
        up_ref[pl.ds(0, SEQ_PAD), :] = jnp.zeros((SEQ_PAD, cg), F32)

        def fill(c, _):
            base = pl.multiple_of(c * tc, tc)
            up_ref[pl.ds(base + SEQ_PAD, tc), :] = u_ref[pl.ds(base, tc), :]
            return 0

        lax.fori_loop(0, T // tc, fill, 0)
        for gi, win in enumerate(POOL_WINDOWS):
            @pl.when(g == gi)
            def _(win=win):
                def chunk(c, _):
                    base = pl.multiple_of(c * tc, tc)
                    acc = up_ref[pl.ds(base + SEQ_PAD, tc), :]
                    for j in range(1, win):
                        acc = acc + up_ref[pl.ds(base + SEQ_PAD - j, tc), :]
                    mixed = acc / _pool_count(base, tc, cg, win) - u_ref[pl.ds(base, tc), :]
                    o_ref[pl.ds(base, tc), :] = mixed.astype(BF16)
                    return 0

                lax.fori_loop(0, T // tc, chunk, 0)

    return _pcall(body, name=name, grid=(G, nb), in_specs=[pl.BlockSpec((T, cg), lambda g, i: (0, g * nb + i))],
                  out_specs=pl.BlockSpec((T, cg), lambda g, i: (0, g * nb + i)), out_shape=jax.ShapeDtypeStruct((T, D), BF16),
                  scratch_shapes=[pltpu.VMEM((T + SEQ_PAD, cg), F32)], compiler_params=_params(("parallel", "parallel")))(u)


def _pool_bwd(name, dm, T):
    D = dm.shape[1]
    G = len(POOL_WINDOWS)
    cg = min(D // G, LANES)
    nb = (D // G) // cg
    tc = _tile(T, 128)

    def body(d_ref, o_ref, qp_ref):
        g = pl.program_id(0)
        qp_ref[pl.ds(T, SEQ_PAD), :] = jnp.zeros((SEQ_PAD, cg), F32)
        for gi, win in enumerate(POOL_WINDOWS):
            @pl.when(g == gi)
            def _(win=win):
                def fill(c, _):
                    base = pl.multiple_of(c * tc, tc)
                    qp_ref[pl.ds(base, tc), :] = d_ref[pl.ds(base, tc), :] / _pool_count(base, tc, cg, win)
                    return 0

                lax.fori_loop(0, T // tc, fill, 0)

                def chunk(c, _):
                    base = pl.multiple_of(c * tc, tc)
                    acc = qp_ref[pl.ds(base, tc), :]
                    for j in range(1, win):
                        acc = acc + qp_ref[pl.ds(base + j, tc), :]
                    o_ref[pl.ds(base, tc), :] = (acc - d_ref[pl.ds(base, tc), :]).astype(BF16)
                    return 0

                lax.fori_loop(0, T // tc, chunk, 0)

    return _pcall(body, name=name, grid=(G, nb), in_specs=[pl.BlockSpec((T, cg), lambda g, i: (0, g * nb + i))],
                  out_specs=pl.BlockSpec((T, cg), lambda g, i: (0, g * nb + i)), out_shape=jax.ShapeDtypeStruct((T, D), BF16),
                  scratch_shapes=[pltpu.VMEM((T + SEQ_PAD, cg), F32)], compiler_params=_params(("parallel", "parallel")))(dm)


def _pool_mm(name, mode, a, b, T, scale=None):
    G = len(POOL_WINDOWS)
    D = a.shape[1]
    cg = D // G
    tm = _tile(T, 2048)
    if mode == "tn":
        ins = [(a, _spec((tm, cg), lambda i, j, k: (k, j))), (b, _spec((tm, cg), lambda i, j, k: (k, j)))]
        outs = [((G, cg, cg), F32, _spec((None, cg, cg), lambda i, j, k: (j, 0, 0)))]
        return _mm(name, "tn", (1, G, T // tm), ins, outs, lambda acc: (acc,))[0]
    ins = [(a, _spec((tm, cg), lambda i, j, k: (i, j))), (b, _spec((None, cg, cg), lambda i, j, k: (j, 0, 0)))]
    if scale is not None:
        ins.append((scale, _spec((1, cg), lambda i, j, k: (0, j))))
        outs = [((T, D), BF16, _spec((tm, cg), lambda i, j, k: (i, j))), ((T, D), BF16, _spec((tm, cg), lambda i, j, k: (i, j)))]
        return _mm(name, mode, (T // tm, G, 1), ins, outs, lambda acc, s: (acc * s, acc))
    outs = [((T, D), F32, _spec((tm, cg), lambda i, j, k: (i, j)))]
    return _mm(name, mode, (T // tm, G, 1), ins, outs)[0]


def _sc_mid_fwd(name, z, w, T):
    D = w.shape[1]
    W = w.shape[0]
    cb = min(D, LANES)
    nb = D // cb
    tc = _tile(T, 256)

    def body(b_ref, c_ref, v_ref, w_ref, o_ref, pp_ref):
        pp_ref[pl.ds(0, SEQ_PAD), :] = jnp.zeros((SEQ_PAD, cb), F32)

        def fill(c, _):
            base = pl.multiple_of(c * tc, tc)
            pp_ref[pl.ds(base + SEQ_PAD, tc), :] = c_ref[pl.ds(base, tc), :].astype(F32) * v_ref[pl.ds(base, tc), :].astype(F32)
            return 0

        lax.fori_loop(0, T // tc, fill, 0)

        def chunk(c, _):
            base = pl.multiple_of(c * tc, tc)
            q = jnp.zeros((tc, cb), F32)
            for k in range(W):
                q = q + w_ref[pl.ds(k, 1), :] * pp_ref[pl.ds(base + SEQ_PAD - (W - 1) + k, tc), :]
            o_ref[pl.ds(base, tc), :] = (b_ref[pl.ds(base, tc), :].astype(F32) * q).astype(BF16)
            return 0

        lax.fori_loop(0, T // tc, chunk, 0)

    return _pcall(body, name=name, grid=(nb,),
                  in_specs=[pl.BlockSpec((T, cb), lambda i: (0, i)), pl.BlockSpec((T, cb), lambda i: (0, nb + i)),
                            pl.BlockSpec((T, cb), lambda i: (0, 2 * nb + i)), pl.BlockSpec((W, cb), lambda i: (0, i))],
                  out_specs=pl.BlockSpec((T, cb), lambda i: (0, i)), out_shape=jax.ShapeDtypeStruct((T, D), BF16),
                  scratch_shapes=[pltpu.VMEM((T + SEQ_PAD, cb), F32)],
                  compiler_params=_params(("parallel",)))(z, z, z, w)


def _sc_mid_bwd(name, z, dm, w, T):
    D = w.shape[1]
    W = w.shape[0]
    cb = min(D, LANES)
    nb = D // cb
    tc = _tile(T, 128)

    def body(b_ref, c_ref, v_ref, d_ref, w_ref, dz_ref, dw_ref, pp_ref, dq_ref, acc_ref):
        part = pl.program_id(0)
        pp_ref[pl.ds(0, SEQ_PAD), :] = jnp.zeros((SEQ_PAD, cb), F32)
        dq_ref[pl.ds(T, SEQ_PAD), :] = jnp.zeros((SEQ_PAD, cb), F32)
        acc_ref[...] = jnp.zeros((W * 8, cb), F32)

        def fill(c, _):
            base = pl.multiple_of(c * tc, tc)
            pp_ref[pl.ds(base + SEQ_PAD, tc), :] = c_ref[pl.ds(base, tc), :].astype(F32) * v_ref[pl.ds(base, tc), :].astype(F32)
            dq_ref[pl.ds(base, tc), :] = d_ref[pl.ds(base, tc), :] * b_ref[pl.ds(base, tc), :].astype(F32)
            return 0

        lax.fori_loop(0, T // tc, fill, 0)

        @pl.when(part == 0)
        def _():
            def chunk(c, _):
                base = pl.multiple_of(c * tc, tc)
                dq = dq_ref[pl.ds(base, tc), :]
                q = jnp.zeros((tc, cb), F32)
                for k in range(W):
                    p = pp_ref[pl.ds(base + SEQ_PAD - (W - 1) + k, tc), :]
                    q = q + w_ref[pl.ds(k, 1), :] * p
                    acc_ref[pl.ds(8 * k, 8), :] += jnp.sum((dq * p).reshape(tc // 8, 8, cb), axis=0)
                dz_ref[pl.ds(base, tc), :] = (d_ref[pl.ds(base, tc), :] * q).astype(BF16)
                return 0

            lax.fori_loop(0, T // tc, chunk, 0)

        @pl.when(part > 0)
        def _():
            def chunk(c, _):
                base = pl.multiple_of(c * tc, tc)
                dp = jnp.zeros((tc, cb), F32)
                for k in range(W):
                    dp = dp + w_ref[pl.ds(k, 1), :] * dq_ref[pl.ds(base + (W - 1) - k, tc), :]
                other = jnp.where(part == 1, v_ref[pl.ds(base, tc), :], c_ref[pl.ds(base, tc), :]).astype(F32)
                dz_ref[pl.ds(base, tc), :] = (dp * other).astype(BF16)
                return 0

            lax.fori_loop(0, T // tc, chunk, 0)

        for k in range(W):
            dw_ref[pl.ds(k, 1), :] = jnp.sum(acc_ref[pl.ds(8 * k, 8), :], axis=0, keepdims=True)

    dz, dw3 = _pcall(body, name=name, grid=(3, nb),
                     in_specs=[pl.BlockSpec((T, cb), lambda p, i: (0, i)), pl.BlockSpec((T, cb), lambda p, i: (0, nb + i)),
                               pl.BlockSpec((T, cb), lambda p, i: (0, 2 * nb + i)), pl.BlockSpec((T, cb), lambda p, i: (0, i)),
                               pl.BlockSpec((W, cb), lambda p, i: (0, i))],
                     out_specs=[pl.BlockSpec((T, cb), lambda p, i: (0, p * nb + i)),
                                pl.BlockSpec((None, W, cb), lambda p, i: (p, 0, i))],
                     out_shape=[jax.ShapeDtypeStruct((T, 3 * D), BF16), jax.ShapeDtypeStruct((3, W, D), F32)],
                     scratch_shapes=[pltpu.VMEM((T + SEQ_PAD, cb), F32), pltpu.VMEM((T + SEQ_PAD, cb), F32),
                                     pltpu.VMEM((W * 8, cb), F32)],
                     compiler_params=_params(("parallel", "parallel")))(z, z, z, dm, w)
    return dz, dw3


def _ret_tables(dk, dv):
    C = RET_CHUNK
    lg = np.log1p(-np.exp2(-5.0 - np.arange(RET_HEADS, dtype=np.float64)))
    idx = np.arange(C, dtype=np.float64)
    rel = idx[:, None] - idx[None, :]
    mask = np.where(rel >= 0, np.exp(lg[:, None, None] * np.maximum(rel, 0.0)), 0.0)
    qd = np.exp(lg[:, None] * (idx + 1.0))[:, :, None] * np.ones((1, 1, dk))
    kd = np.exp(lg[:, None] * (C - 1.0 - idx))[:, :, None] * np.ones((1, 1, dk))
    cd = np.exp(lg * C)[:, None, None] * np.ones((1, 1, dv))
    return tuple(jnp.asarray(a, F32) for a in (mask, qd, kd, cd))


def _ret_fwd(name, q, k, v, tabs, T):
    H, C = RET_HEADS, RET_CHUNK
    dk, dv = q.shape[1] // H, v.shape[1] // H
    nch = T // C
    mask, qd, kd, cd = tabs

    def body(q_ref, k_ref, v_ref, m_ref, qd_ref, kd_ref, cd_ref, o_ref, st_ref, s_ref):
        @pl.when(pl.program_id(0) == 0)
        def _():
            s_ref[...] = jnp.zeros((H, dk, dv), F32)

        for h in range(H):
            qv, kv = q_ref[:, h * dk:(h + 1) * dk], k_ref[:, h * dk:(h + 1) * dk]
            vv = v_ref[:, h * dv:(h + 1) * dv]
            s = s_ref[h]
            sb = s.astype(BF16)
            st_ref[h] = sb
            scores = lax.dot_general(qv, kv, NT, preferred_element_type=F32) * m_ref[h]
            intra = lax.dot_general(scores.astype(BF16), vv, NN, preferred_element_type=F32)
            cross = lax.dot_general((qv.astype(F32) * qd_ref[h]).astype(BF16), sb, NN, preferred_element_type=F32)
            o_ref[:, h * dv:(h + 1) * dv] = intra + cross
            upd = lax.dot_general((kv.astype(F32) * kd_ref[h]).astype(BF16), vv, TN, preferred_element_type=F32)
            s_ref[h] = s * cd_ref[h] + upd

    whole = lambda a: pl.BlockSpec(a.shape, lambda n: (0, 0, 0))
    return _pcall(body, name=name, grid=(nch,),
                  in_specs=[pl.BlockSpec((C, H * dk), lambda n: (n, 0)), pl.BlockSpec((C, H * dk), lambda n: (n, 0)),
                            pl.BlockSpec((C, H * dv), lambda n: (n, 0)), whole(mask), whole(qd), whole(kd), whole(cd)],
                  out_specs=[pl.BlockSpec((C, H * dv), lambda n: (n, 0)),
                             pl.BlockSpec((H, None, dk, dv), lambda n: (0, n, 0, 0))],
                  out_shape=[jax.ShapeDtypeStruct((T, H * dv), F32), jax.ShapeDtypeStruct((H, nch, dk, dv), BF16)],
                  scratch_shapes=[pltpu.VMEM((H, dk, dv), F32)],
                  compiler_params=_params(("arbitrary",)))(q, k, v, mask, qd, kd, cd)


def _ret_bwd(name, q, k, v, do, st, tabs, T):
    H, C = RET_HEADS, RET_CHUNK
    dk, dv = q.shape[1] // H, v.shape[1] // H
    nch = T // C
    mask, qd, kd, cd = tabs

    def body(q_ref, k_ref, v_ref, do_ref, st_ref, m_ref, qd_ref, kd_ref, cd_ref, dq_ref, dk_ref, dv_ref, ds_ref):
        @pl.when(pl.program_id(0) == 0)
        def _():
            ds_ref[...] = jnp.zeros((H, dk, dv), F32)

        for h in range(H):
            qv, kv = q_ref[:, h * dk:(h + 1) * dk], k_ref[:, h * dk:(h + 1) * dk]
            vv, dov = v_ref[:, h * dv:(h + 1) * dv], do_ref[:, h * dv:(h + 1) * dv]
            sb = st_ref[h]
            m = m_ref[h]
            ds = ds_ref[h]
            dsb = ds.astype(BF16)
            a = (lax.dot_general(qv, kv, NT, preferred_element_type=F32) * m).astype(BF16)
            dp = (lax.dot_general(dov, vv, NT, preferred_element_type=F32) * m).astype(BF16)
            qa = (qv.astype(F32) * qd_ref[h]).astype(BF16)
            kb = (kv.astype(F32) * kd_ref[h]).astype(BF16)
            dq_ref[:, h * dk:(h + 1) * dk] = (lax.dot_general(dp, kv, NN, preferred_element_type=F32)
                                              + qd_ref[h] * lax.dot_general(dov, sb, NT, preferred_element_type=F32))
            dk_ref[:, h * dk:(h + 1) * dk] = (lax.dot_general(dp, qv, TN, preferred_element_type=F32)
                                              + kd_ref[h] * lax.dot_general(vv, dsb, NT, preferred_element_type=F32))
            dv_ref[:, h * dv:(h + 1) * dv] = (lax.dot_general(a, dov, TN, preferred_element_type=F32)
                                              + lax.dot_general(kb, dsb, NN, preferred_element_type=F32)).astype(BF16)
            ds_ref[h] = ds * cd_ref[h] + lax.dot_general(qa, dov, TN, preferred_element_type=F32)

    rev = lambda n: (nch - 1 - n, 0)
    whole = lambda a: pl.BlockSpec(a.shape, lambda n: (0, 0, 0))
    return _pcall(body, name=name, grid=(nch,),
                  in_specs=[pl.BlockSpec((C, H * dk), rev), pl.BlockSpec((C, H * dk), rev), pl.BlockSpec((C, H * dv), rev),
                            pl.BlockSpec((C, H * dv), rev), pl.BlockSpec((H, None, dk, dv), lambda n: (0, nch - 1 - n, 0, 0)),
                            whole(mask), whole(qd), whole(kd), whole(cd)],
                  out_specs=[pl.BlockSpec((C, H * dk), rev), pl.BlockSpec((C, H * dk), rev), pl.BlockSpec((C, H * dv), rev)],
                  out_shape=[jax.ShapeDtypeStruct((T, H * dk), F32), jax.ShapeDtypeStruct((T, H * dk), F32),
                             jax.ShapeDtypeStruct((T, H * dv), BF16)],
                  scratch_shapes=[pltpu.VMEM((H, dk, dv), F32)],
                  compiler_params=_params(("arbitrary",)))(q, k, v, do, st, mask, qd, kd, cd)


def _local_step(x, pos, tgt, ng, get_w, put_g, first_deps, T, D):
    tr = _tile(T, 512)
    trw = _tile(T, 128)
    row = lambda i: ng[i:i + 1]

    u = _rowwise("rms_first", lambda xv, gv: _rms(xv, gv), [(x, D, 0), (row(0), None, None)], [("row", D, BF16)], T, tr,
                 deps=first_deps)[0]
    h = x
    saved = []
    cos = sin = None
    tabs = None
    for l in range(DEPTH):
        mixer = l % 4
        wts = get_w(l, "mix", h)
        sv = {"h_in": h, "u": u}
        if mixer == 0:
            z = _mm_cols("conv_in", u, wts["conv_w_in"], (), T, lambda acc, b: (acc + b,), (wts["conv_b_in"],), 1, BF16)[0]
            v = _rowwise("conv_glu", lambda za, zg: _glu(za.astype(F32), zg.astype(F32)), [(z, D, 0), (z, D, 1)], [("row", D, F32)], T, tr)[0]
            c = _dwconv_fwd("conv_dw", v, wts["conv_dw"], wts["conv_dw_b"], T)
            s = _rowwise("conv_ln", _ln_silu, [(c, D, 0), (wts["conv_ln_g"], None, None), (wts["conv_ln_b"], None, None)],
                         [("row", D, BF16)], T, tr)[0]
            y = _mm_rows("conv_out", s, wts["conv_w_out"], T, lambda acc, b: (acc + b,), (wts["conv_b_out"],), BF16)
            sv.update(z=z, v=v, c=c, s=s)
        elif mixer == 1:
            mixed = _pool_fwd("pool_fwd", u, T)
            y, ycat = _pool_mm("pool_mm", "nn", mixed, wts["pool_w"], T, wts["pool_scale"])
            sv.update(mixed=mixed, ycat=ycat)
        elif mixer == 2:
            z = _mm_cols("sc_in", u, wts["sc_w_in"], (), T, out_dtype=BF16)[0]
            m = _sc_mid_fwd("sc_mid", z, wts["sc_dw"], T)
            y = _mm_rows("sc_out", m, wts["sc_w_out"], T, out_dtype=BF16)
            sv.update(z=z, m=m)
        else:
            dk = D // RET_HEADS
            half = dk // 2
            inv_freq = jnp.asarray((ROPE_BASE ** (-np.arange(half, dtype=np.float64) / half)).astype(np.float32)).reshape(1, half)

            def _cs(p, f):
                ang = p * f
                return jnp.cos(ang), jnp.sin(ang)

            cos, sin = _rowwise("rope_tab", _cs, [(pos, 1, 0), (inv_freq, None, None)],
                                [("row", half, F32), ("row", half, F32)], T, tr)
            tabs = _ret_tables(dk, 2 * dk)
            z = _mm_cols("ret_in", u, wts["ret_w_in"], (), T, out_dtype=BF16)[0]

            def _pre(zq, zk, zv, cv, sn):
                return _rot(zq.astype(F32), cv, sn, 1.0), _rot(zk.astype(F32), cv, sn, 1.0) * (dk ** -0.5), zv

            qr, kr, vb = _rowwise("ret_pre", _pre, [(z, D, 0), (z, D, 1), (z, 2 * D, 1), (cos, half, 0), (sin, half, 0)],
                                  [("row", D, BF16), ("row", D, BF16), ("row", 2 * D, BF16)], T, trw)
            o, st = _ret_fwd("ret_fwd", qr, kr, vb, tabs, T)
            out = _rowwise("ret_post", lambda ov, gv: _gate_norm(ov, gv.astype(F32)), [(o, 2 * D, 0), (z, 2 * D, 2)], [("row", 2 * D, BF16)], T, trw)[0]
            y = _mm_rows("ret_out", out, wts["ret_w_out"], T, out_dtype=BF16)
            sv.update(z=z, qr=qr, kr=kr, vb=vb, o=o, st=st, out=out)
        sv["y1"] = y
        h, u = _rowwise("bound_a", lambda hv, yv, gp, gn: (lambda hn: (hn, _rms(hn, gn)))(hv + _rms(yv.astype(F32), gp)),
                        [(h, D, 0), (y, D, 0), (row(4 * l + 1), None, None), (row(4 * l + 2), None, None)],
                        [("row", D, F32), ("row", D, BF16)], T, tr)
        sv["h_mid"], sv["u2"] = h, u
        wm = get_w(l, "mlp", h)
        mlp_deps = wm.pop("_deps", [])
        wts = dict(wts, **wm)
        sv["w"] = wts
        r = _mm_cols("mlp_up", u, wts["mlp_up"], (), T, lambda acc: (jnp.maximum(acc, 0.0),), (), 1, BF16, deps=mlp_deps)[0]
        y = _mm_whole_k("mlp_down", r, wts["mlp_down"], T, "rows", pre=_square, out_dtype=BF16)
        sv.update(r=r, y2=y)
        if l + 1 < DEPTH:
            udt = F32 if (l + 1) % 4 == 1 else BF16
            h, u = _rowwise("bound_b", lambda hv, yv, gp, gn: (lambda hn: (hn, _rms(hn, gn)))(hv + _rms(yv.astype(F32), gp)),
                            [(h, D, 0), (y, D, 0), (row(4 * l + 3), None, None), (row(4 * l + 4), None, None)],
                            [("row", D, F32), ("row", D, udt)], T, tr)
        saved.append(sv)

    def _final(hv, yv, gp, tv):
        hn = hv + _rms(yv.astype(F32), gp)
        e = hn - tv
        return e * (1.0 / D), _colsum(e * e)

    dh, loss_row = _rowwise("final", _final, [(h, D, 0), (y, D, 0), (row(4 * DEPTH - 1), None, None), (tgt, D, 0)],
                            [("row", D, F32), ("acc", (1, D))], T, tr)

    gn = [None] * (4 * DEPTH)
    F = saved[0]["r"].shape[1]
    du_next = None
    tok = None
    for l in reversed(range(DEPTH)):
        sv = saved[l]
        wts = sv["w"]
        g = {}
        deps = [] if tok is None else [tok]
        mixer = l % 4
        if du_next is None:
            def _bb(dhv, yv, gp):
                _, vjp = jax.vjp(_rms, yv.astype(F32), gp)
                dy, dgp = vjp(dhv)
                return dy, dgp

            dy, gn[4 * l + 3] = _rowwise("bwd_last", _bb, [(dh, D, 0), (sv["y2"], D, 0), (row(4 * l + 3), None, None)],
                                         [("row", D, BF16), ("acc", (1, D))], T, tr, deps)
        else:
            dh, dy, gn[4 * l + 3], gn[4 * l + 4], _ = _bound_bwd("bwd_b", dh, du_next, saved[l + 1]["h_in"], sv["y2"],
                                                              row(4 * l + 3), row(4 * l + 4), T, D, tr, deps)
        F8 = F // N_DEV
        tm = _tile(T, 2048)
        da = _mm("mlp_dhid", "nt", (T // tm, N_DEV, 1),
                 [(dy, _spec((tm, D), lambda i, j, k: (i, 0))),
                  (wts["mlp_down"], _spec((None, F8, D), lambda i, j, k: (j, 0, 0))),
                  (sv["r"], _spec((tm, F8), lambda i, j, k: (i, j)))],
                 [((T, F), BF16, _spec((tm, F8), lambda i, j, k: (i, j)))],
                 lambda acc, rr: (acc * (2.0 * rr.astype(F32)),))[0]
        tk = T
        g["mlp_down"] = _mm("mlp_dwdown", "tn", (1, N_DEV, T // tk),
                            [(sv["r"], _spec((tk, F8), lambda i, j, k: (k, j))), (dy, _spec((tk, D), lambda i, j, k: (k, 0)))],
                            [(wts["mlp_down"].shape, BF16, _spec((None, F8, D), lambda i, j, k: (j, 0, 0)))],
                            lambda acc: (acc,), pre=_square)[0]
        g["mlp_up"] = _mm_cols_grad("mlp_dwup", sv["u2"], da, wts["mlp_up"].shape, (), T)
        du = _mm_whole_k("mlp_du", da, wts["mlp_up"], T, "cols", out_dtype=BF16)
        tok = put_g(l, "mlp", g, du, None)
        g = {}
        dh, dy, gn[4 * l + 1], gn[4 * l + 2], dy_sum = _bound_bwd("bwd_a", dh, du, sv["h_mid"], sv["y1"],
                                                                row(4 * l + 1), row(4 * l + 2), T, D, tr, [] if tok is None else [tok])
        u = sv["u"]
        if mixer == 0:
            g["conv_b_out"] = dy_sum
            ds = _mm_rows_t("conv_ds", dy, wts["conv_w_out"], T)
            g["conv_w_out"] = _mm_rows_grad("conv_dwout", sv["s"], dy, T)

            def _lnb(cv, gv, bv, dsv):
                _, vjp = jax.vjp(_ln_silu, cv, gv, bv)
                dc, dg, db = vjp(dsv)
                return dc, dg, db, _colsum(dc)

            dc, g["conv_ln_g"], g["conv_ln_b"], g["conv_dw_b"] = _rowwise(
                "conv_ln_bwd", _lnb, [(sv["c"], D, 0), (wts["conv_ln_g"], None, None), (wts["conv_ln_b"], None, None), (ds, D, 0)],
                [("row", D, F32), ("acc", (1, D)), ("acc", (1, D)), ("acc", (1, D))], T, tr)
            dv, g["conv_dw"] = _dwconv_bwd("conv_dw_bwd", dc, sv["v"], wts["conv_dw"], T)

            def _glub(za, zg, dvv):
                _, vjp = jax.vjp(_glu, za.astype(F32), zg.astype(F32))
                dza, dzg = vjp(dvv)
                dz = jnp.concatenate([dza, dzg], axis=-1)
                return dz, _colsum(dz)

            dz, g["conv_b_in"] = _rowwise("conv_glu_bwd", _glub, [(sv["z"], D, 0), (sv["z"], D, 1), (dv, D, 0)],
                                          [("row", 2 * D, BF16), ("acc", (1, 2 * D))], T, tr)
            g["conv_w_in"] = _mm_cols_grad("conv_dwin", u, dz, wts["conv_w_in"].shape, (), T)
            du = _mm_whole_k("conv_du", dz, wts["conv_w_in"], T, "cols", out_dtype=BF16)
        elif mixer == 1:
            def _psb(dyv, yc, sc):
                return dyv * sc, _colsum(dyv.astype(F32) * yc.astype(F32))

            dyg, g["pool_scale"] = _rowwise("pool_scale_bwd", _psb, [(dy, D, 0), (sv["ycat"], D, 0), (wts["pool_scale"], None, None)],
                                            [("row", D, BF16), ("acc", (1, D))], T, tr)
            dmix = _pool_mm("pool_dmix", "nt", dyg, wts["pool_w"], T)
            g["pool_w"] = _pool_mm("pool_dw", "tn", sv["mixed"], dyg, T)
            du = _pool_bwd("pool_bwd", dmix, T)
        elif mixer == 2:
            dm = _mm_rows_t("sc_dm", dy, wts["sc_w_out"], T)
            g["sc_w_out"] = _mm_rows_grad("sc_dwout", sv["m"], dy, T)
            dz, dw3 = _sc_mid_bwd("sc_mid_bwd", sv["z"], dm, wts["sc_dw"], T)
            g["sc_dw"] = dw3[0]
            g["sc_w_in"] = _mm_cols_grad("sc_dwin", u, dz, wts["sc_w_in"].shape, (), T)
            du = _mm_whole_k("sc_du", dz, wts["sc_w_in"], T, "cols", out_dtype=BF16)
        else:
            dk = D // RET_HEADS
            half = dk // 2
            dout = _mm_rows_t("ret_dout", dy, wts["ret_w_out"], T)
            g["ret_w_out"] = _mm_rows_grad("ret_dwout", sv["out"], dy, T)

            def _postb(ov, gv, dv_):
                _, vjp = jax.vjp(_gate_norm, ov, gv.astype(F32))
                return vjp(dv_)

            do, dzg = _rowwise("ret_post_bwd", _postb, [(sv["o"], 2 * D, 0), (sv["z"], 2 * D, 2), (dout, 2 * D, 0)],
                               [("row", 2 * D, BF16), ("row", 2 * D, BF16)], T, trw)
            dqr, dkr, dvb = _ret_bwd("ret_bwd", sv["qr"], sv["kr"], sv["vb"], do, sv["st"], tabs, T)

            def _preb(dq, dk_, dvv, dgv, cv, sn):
                return jnp.concatenate([_rot(dq, cv, sn, -1.0).astype(BF16), (_rot(dk_, cv, sn, -1.0) * (dk ** -0.5)).astype(BF16),
                                        dvv, dgv], axis=-1)

            dz = _rowwise("ret_pre_bwd", _preb, [(dqr, D, 0), (dkr, D, 0), (dvb, 2 * D, 0), (dzg, 2 * D, 0), (cos, half, 0), (sin, half, 0)],
                          [("row", 6 * D, BF16)], T, trw)[0]
            g["ret_w_in"] = _mm_cols_grad("ret_dwin", u, dz, wts["ret_w_in"].shape, (), T)
            du = _mm_whole_k("ret_du", dz, wts["ret_w_in"], T, "cols", out_dtype=BF16)
        du_next = du
        if l > 0:
            tok = put_g(l, "mix", g, du, None)

    def _fb(dhv, duv, xv, gv):
        _, vjp = jax.vjp(_rms, xv, gv)
        dx, dg = vjp(duv.astype(F32))
        return dhv + dx, dg

    grad_x, gn[0] = _rowwise("bwd_first", _fb, [(dh, D, 0), (du_next, D, 0), (x, D, 0), (row(0), None, None)],
                             [("row", D, F32), ("acc", (1, D))], T, tr)
    put_g(0, "mix", g, grad_x, jnp.concatenate(gn, axis=0))
    return loss_row, grad_x


def _bound_bwd(name, dh, du, h_new, y, g_post, g_next, T, D, tr, deps=()):
    def fn(dhv, duv, hn, yv, gp, gnx):
        _, vjp1 = jax.vjp(_rms, hn, gnx)
        d1, dgn = vjp1(duv.astype(F32))
        dht = dhv + d1
        _, vjp2 = jax.vjp(_rms, yv.astype(F32), gp)
        dy, dgp = vjp2(dht)
        return dht, dy, dgp, dgn, _colsum(dy)

    return _rowwise(name, fn, [(dh, D, 0), (du, D, 0), (h_new, D, 0), (y, D, 0), (g_post, None, None), (g_next, None, None)],
                    [("row", D, F32), ("row", D, BF16), ("acc", (1, D)), ("acc", (1, D)), ("acc", (1, D))], T, tr, deps)


def _adamw(name, parts, w, m, v, own=None, layer=None, bufs=None):
    P, R, C = parts.shape
    tr = _tile(R, 256)
    n_s = 1 if own is None else 2

    def body(*refs):
        p_ref = refs[0]
        w_ref, m_ref, v_ref = refs[n_s:n_s + 3]
        g_ref, d_ref, m2_ref, v2_ref = refs[-4:]

        def slot(i):
            if own is None:
                return p_ref[i].astype(F32)
            here = 2 * lax.axis_index("x") + lax.axis_index("y") == i
            return jnp.where(here, refs[1][i], p_ref[i]).astype(F32)

        g = slot(0)
        for i in range(1, P):
            g = g + slot(i)
        wv = w_ref[...]
        m2 = ADAM_B1 * m_ref[...] + (1.0 - ADAM_B1) * g
        v2 = ADAM_B2 * v_ref[...] + (1.0 - ADAM_B2) * (g * g)
        m_hat = m2 / (1.0 - ADAM_B1 ** ADAM_STEP)
        v_hat = v2 / (1.0 - ADAM_B2 ** ADAM_STEP)
        g_ref[...] = g
        d_ref[...] = -ADAM_LR * (m_hat / (jnp.sqrt(v_hat) + ADAM_EPS) + ADAM_WD * wv)
        m2_ref[...] = m2
        v2_ref[...] = v2

    pblk = pl.BlockSpec((P, tr, C), lambda i: (0, i, 0))
    stacks = [parts] if own is None else [parts, own]
    if layer is None:
        blk = pl.BlockSpec((tr, C), lambda i: (i, 0))
        return _pcall(body, name=name, grid=(R // tr,),
                      in_specs=[pblk] * n_s + [blk, blk, blk], out_specs=[blk] * 4,
                      out_shape=[jax.ShapeDtypeStruct((R, C), F32)] * 4, compiler_params=_params(("parallel",)))(*stacks, w, m, v)
    blk = pl.BlockSpec((None, tr, C), lambda i: (layer, i, 0))
    return _pcall(body, name=name, grid=(R // tr,),
                  in_specs=[pblk] * n_s + [blk, blk, blk] + [pl.BlockSpec(memory_space=pl.ANY)] * 4, out_specs=[blk] * 4,
                  out_shape=[jax.ShapeDtypeStruct(w.shape, F32)] * 4, input_output_aliases={n_s + 3 + i: i for i in range(4)},
                  compiler_params=_params(("parallel",)))(*stacks, w, m, v, *bufs)


def _pair_add(name, a, b):
    Q, _, R, C = a.shape
    tr = _tile(R, 1024)
    blk = pl.BlockSpec((None, tr, C), lambda q, i: (q, i, 0))

    def body(a_ref, b_ref, o_ref):
        mine = jnp.where(lax.axis_index("c") == 0, a_ref[0], a_ref[1])
        o_ref[...] = (mine.astype(F32) + b_ref[...].astype(F32)).astype(BF16)

    return _pcall(body, name=name, grid=(Q, R // tr),
                  in_specs=[pl.BlockSpec((None, 2, tr, C), lambda q, i: (q, 0, i, 0)), blk], out_specs=blk,
                  out_shape=jax.ShapeDtypeStruct((Q, R, C), BF16), compiler_params=_params(("parallel", "parallel")))(a, b)


ANY = pl.BlockSpec(memory_space=pl.ANY)


def _place():
    x, y, c = lax.axis_index("x"), lax.axis_index("y"), lax.axis_index("c")
    chips = [(1 - x, y), (x, 1 - y), (1 - x, 1 - y)]
    return x, y, c, chips


def _gather_all(locs):
    n = len(locs)

    def body(*refs):
        ins, outs = refs[:n], refs[n:2 * n]
        send, recv, lsem = refs[2 * n:]
        x, y, c, chips = _place()
        me = 4 * x + 2 * y + c
        sib = (x, y, 1 - c)

        def cp(a, k, slot, to, src=None):
            return pltpu.make_async_remote_copy(src_ref=outs[a].at[slot] if src is None else src, dst_ref=outs[a].at[slot],
                                                send_sem=send.at[a, k], recv_sem=recv.at[a, k], device_id=to, device_id_type=MESH)

        mine = [pltpu.make_async_copy(ins[a], outs[a].at[me], lsem.at[a]) for a in range(n)]
        for d in mine:
            d.start()
        first = []
        for a in range(n):
            first.append(cp(a, 0, me, sib, src=ins[a]))
            for j, chip in enumerate(chips):
                first.append(cp(a, 1 + j, me, (chip[0], chip[1], c), src=ins[a]))
        for d in first:
            d.start()
        passed = []
        for j, chip in enumerate(chips):
            slot = 4 * chip[0] + 2 * chip[1] + c
            for a in range(n):
                cp(a, 1 + j, slot, sib).wait_recv()
                p = cp(a, 4 + j, slot, sib)
                p.start()
                passed.append(p)
        for a in range(n):
            cp(a, 0, 4 * x + 2 * y + (1 - c), sib).wait_recv()
        for j, chip in enumerate(chips):
            for a in range(n):
                cp(a, 4 + j, 4 * chip[0] + 2 * chip[1] + (1 - c), sib).wait_recv()
        for d in first + passed:
            d.wait_send()
        for d in mine:
            d.wait()

    return _pcall(body, name="gather_all", in_specs=[ANY] * n, out_specs=[ANY] * n,
                  out_shape=[jax.ShapeDtypeStruct((N_DEV,) + a.shape, a.dtype) for a in locs],
                  scratch_shapes=[pltpu.SemaphoreType.DMA((n, 7)), pltpu.SemaphoreType.DMA((n, 7)), pltpu.SemaphoreType.DMA((n,))],
                  compiler_params=pltpu.CompilerParams(has_side_effects=True))(*locs)


HBM = pl.BlockSpec(memory_space=pltpu.HBM)
SEM = pl.BlockSpec(memory_space=pltpu.SEMAPHORE)
DATAFLOW = pltpu.SideEffectType.DATAFLOW_SIDE_EFFECTING


def _peers():
    x, y, c = lax.axis_index("x"), lax.axis_index("y"), lax.axis_index("c")
    out = []
    for r in range(1, N_DEV):
        px = 1 - x if (r >> 2) & 1 else x
        py = 1 - y if (r >> 1) & 1 else y
        pc = 1 - c if r & 1 else c
        out.append(((px, py, pc), 4 * px + 2 * py + pc))
    return 4 * x + 2 * y + c, out


def _hbm(a):
    return pltpu.with_memory_space_constraint(a, pltpu.HBM)


def _gather_start(name, locs, lands, deps):
    n = len(locs)

    def body(*refs):
        ins, lnd = refs[:n], refs[n:2 * n]
        send, recv = refs[2 * n + len(deps)], refs[2 * n + len(deps) + 1]
        token = refs[-1]
        x, y, c, chips = _place()
        me = 4 * x + 2 * y + c
        for k, to in enumerate([(x, y, 1 - c)] + [(chip[0], chip[1], c) for chip in chips]):
            for a in range(n):
                pltpu.make_async_remote_copy(src_ref=ins[a], dst_ref=lnd[a].at[me], send_sem=send.at[a * 4 + k],
                                             recv_sem=recv.at[a * 4 + k], device_id=to, device_id_type=MESH).start()
        token[...] = jnp.zeros_like(token)

    out_shape = ((pltpu.SemaphoreType.DMA((n * 4,)), pltpu.SemaphoreType.DMA((n * 4,)))
                 + tuple(pltpu.HBM(a.shape, a.dtype) for a in locs) + tuple(pltpu.HBM(a.shape, a.dtype) for a in lands)
                 + (jax.ShapeDtypeStruct((8, LANES), F32),))
    res = _pcall(body, name=name, out_shape=out_shape, in_specs=[HBM] * (2 * n) + [ANY] * len(deps),
                 out_specs=(SEM, SEM) + (HBM,) * (2 * n) + (pl.BlockSpec(memory_space=pltpu.VMEM),),
                 input_output_aliases={a: 2 + a for a in range(2 * n)},
                 compiler_params=pltpu.CompilerParams(has_side_effects=DATAFLOW))(
        *[_hbm(a) for a in locs], *[_hbm(a) for a in lands], *deps)
    return res[0], res[1], res[2:2 + n], res[2 + n:2 + 2 * n], res[-1]


def _gather_pass(name, send, recv, shards, lands, after):
    n = len(shards)

    def body(*refs):
        ins, lnd = refs[:n], refs[n:2 * n]
        recv = refs[2 * n + 1]
        fsend, frecv = refs[2 * n + 3], refs[2 * n + 4]
        token = refs[-1]
        x, y, c, chips = _place()
        for j, chip in enumerate(chips):
            slot = 4 * chip[0] + 2 * chip[1] + c
            for a in range(n):
                pltpu.make_async_remote_copy(src_ref=ins[a], dst_ref=lnd[a].at[slot], send_sem=fsend.at[a * 3 + j],
                                             recv_sem=recv.at[a * 4 + 1 + j], device_id=(chip[0], chip[1], c),
                                             device_id_type=MESH).wait_recv()
                pltpu.make_async_remote_copy(src_ref=lnd[a].at[slot], dst_ref=lnd[a].at[slot], send_sem=fsend.at[a * 3 + j],
                                             recv_sem=frecv.at[a * 3 + j], device_id=(x, y, 1 - c), device_id_type=MESH).start()
        token[...] = jnp.zeros_like(token)

    out_shape = ((pltpu.SemaphoreType.DMA((n * 3,)), pltpu.SemaphoreType.DMA((n * 3,)))
                 + tuple(pltpu.HBM(a.shape, a.dtype) for a in shards) + tuple(pltpu.HBM(a.shape, a.dtype) for a in lands)
                 + (jax.ShapeDtypeStruct((8, LANES), F32),))
    res = _pcall(body, name=name, out_shape=out_shape, in_specs=[HBM] * (2 * n) + [SEM, SEM, ANY],
                 out_specs=(SEM, SEM) + (HBM,) * (2 * n) + (pl.BlockSpec(memory_space=pltpu.VMEM),),
                 input_output_aliases={a: 2 + a for a in range(2 * n)},
                 compiler_params=pltpu.CompilerParams(has_side_effects=DATAFLOW))(*shards, *lands, send, recv, after)
    return res[0], res[1], res[2:2 + n], res[2 + n:2 + 2 * n], res[-1]


def _gather_wait(name, send, recv, fsend, frecv, shards, lands, after):
    n = len(shards)

    def body(*refs):
        ins, lnd = refs[:n], refs[n:2 * n]
        send, recv, fsend, frecv = refs[2 * n:2 * n + 4]
        x, y, c, chips = _place()
        sib = (x, y, 1 - c)
        for a in range(n):
            own = pltpu.make_async_remote_copy(src_ref=ins[a], dst_ref=lnd[a].at[4 * x + 2 * y + (1 - c)], send_sem=send.at[a * 4],
                                               recv_sem=recv.at[a * 4], device_id=sib, device_id_type=MESH)
            own.wait_send()
            own.wait_recv()
            for j, chip in enumerate(chips):
                pltpu.make_async_remote_copy(src_ref=ins[a], dst_ref=lnd[a].at[0], send_sem=send.at[a * 4 + 1 + j],
                                             recv_sem=recv.at[a * 4 + 1 + j], device_id=(chip[0], chip[1], c),
                                             device_id_type=MESH).wait_send()
                fw = pltpu.make_async_remote_copy(src_ref=lnd[a].at[4 * chip[0] + 2 * chip[1] + c],
                                                  dst_ref=lnd[a].at[4 * chip[0] + 2 * chip[1] + (1 - c)], send_sem=fsend.at[a * 3 + j],
                                                  recv_sem=frecv.at[a * 3 + j], device_id=sib, device_id_type=MESH)
                fw.wait_send()
                fw.wait_recv()

    out_shape = tuple(pltpu.HBM(a.shape, a.dtype) for a in shards) + tuple(pltpu.HBM(a.shape, a.dtype) for a in lands)
    res = _pcall(body, name=name, out_shape=out_shape, in_specs=[HBM] * (2 * n) + [SEM, SEM, SEM, SEM, ANY], out_specs=(HBM,) * (2 * n),
                 input_output_aliases={a: a for a in range(2 * n)},
                 compiler_params=pltpu.CompilerParams(has_side_effects=DATAFLOW))(*shards, *lands, send, recv, fsend, frecv, after)
    return res[n:]


def _chip_start(name, cs, lands):
    n = len(cs)

    def body(*refs):
        ins, lnd = refs[:n], refs[n:2 * n]
        send, recv = refs[2 * n], refs[2 * n + 1]
        token = refs[-1]
        x, y, c, chips = _place()
        qme = 2 * x + y
        for j, chip in enumerate(chips):
            for a in range(n):
                pltpu.make_async_remote_copy(src_ref=ins[a].at[2 * chip[0] + chip[1]], dst_ref=lnd[a].at[qme], send_sem=send.at[a * 3 + j],
                                             recv_sem=recv.at[a * 3 + j], device_id=(chip[0], chip[1], c), device_id_type=MESH).start()
        token[...] = jnp.zeros_like(token)

    out_shape = ((pltpu.SemaphoreType.DMA((n * 3,)), pltpu.SemaphoreType.DMA((n * 3,)))
                 + tuple(pltpu.HBM(a.shape, a.dtype) for a in cs) + tuple(pltpu.HBM(a.shape, a.dtype) for a in lands)
                 + (jax.ShapeDtypeStruct((8, LANES), F32),))
    res = _pcall(body, name=name, out_shape=out_shape, in_specs=[HBM] * (2 * n),
                 out_specs=(SEM, SEM) + (HBM,) * (2 * n) + (pl.BlockSpec(memory_space=pltpu.VMEM),),
                 input_output_aliases={a: 2 + a for a in range(2 * n)},
                 compiler_params=pltpu.CompilerParams(has_side_effects=DATAFLOW))(*[_hbm(a) for a in cs], *[_hbm(a) for a in lands])
    return res[0], res[1], res[2:2 + n], res[2 + n:2 + 2 * n], res[-1]


def _chip_wait(name, send, recv, cs, lands, after):
    n = len(cs)

    def body(*refs):
        ins, lnd = refs[:n], refs[n:2 * n]
        send, recv = refs[2 * n], refs[2 * n + 1]
        x, y, c, chips = _place()
        for j, chip in enumerate(chips):
            q = 2 * chip[0] + chip[1]
            for a in range(n):
                cp = pltpu.make_async_remote_copy(src_ref=ins[a].at[q], dst_ref=lnd[a].at[q], send_sem=send.at[a * 3 + j],
                                                  recv_sem=recv.at[a * 3 + j], device_id=(chip[0], chip[1], c), device_id_type=MESH)
                cp.wait_send()
                cp.wait_recv()

    out_shape = tuple(pltpu.HBM(a.shape, a.dtype) for a in cs) + tuple(pltpu.HBM(a.shape, a.dtype) for a in lands)
    res = _pcall(body, name=name, out_shape=out_shape, in_specs=[HBM] * (2 * n) + [SEM, SEM, ANY], out_specs=(HBM,) * (2 * n),
                 input_output_aliases={a: a for a in range(2 * n)},
                 compiler_params=pltpu.CompilerParams(has_side_effects=DATAFLOW))(*cs, *lands, send, recv, after)
    return res[:n], res[n:]


def _pair_exchange(name, gs):
    n = len(gs)

    def body(*refs):
        ins, got = refs[:n], refs[n:2 * n]
        send, recv = refs[2 * n:]
        x, y, c, _ = _place()
        sib = (x, y, 1 - c)
        rem = [pltpu.make_async_remote_copy(src_ref=ins[a].at[q, 1 - c], dst_ref=got[a].at[q], send_sem=send.at[a, q],
                                            recv_sem=recv.at[a, q], device_id=sib, device_id_type=MESH)
               for a in range(n) for q in range(N_CHIP)]
        for d in rem:
            d.start()
        for d in rem:
            d.wait()

    half = [jax.ShapeDtypeStruct((N_CHIP,) + g.shape[2:], g.dtype) for g in gs]
    return _pcall(body, name=name, in_specs=[ANY] * n, out_specs=[ANY] * n, out_shape=half,
                  scratch_shapes=[pltpu.SemaphoreType.DMA((n, N_CHIP)), pltpu.SemaphoreType.DMA((n, N_CHIP))],
                  compiler_params=pltpu.CompilerParams(has_side_effects=True))(*gs)


def _chip_exchange(cs, sm_sh, sm_rep):
    n = len(cs)

    def body(*refs):
        ins, sh, rep = refs[:n], refs[n], refs[n + 1]
        outs, osh, orep = refs[n + 2:2 * n + 2], refs[2 * n + 2], refs[2 * n + 3]
        send, recv, lsem, ssend, srecv, slsem = refs[2 * n + 4:]
        x, y, c, chips = _place()
        qme = 2 * x + y
        me = 4 * x + 2 * y + c
        loc = [pltpu.make_async_copy(ins[a].at[qme], outs[a].at[qme], lsem.at[a]) for a in range(n)]
        loc.append(pltpu.make_async_copy(sh.at[me], osh.at[me], slsem.at[0]))
        loc.append(pltpu.make_async_copy(rep, orep.at[me], slsem.at[1]))
        rem, waits = [], []
        for j, chip in enumerate(chips):
            q = 2 * chip[0] + chip[1]
            to = (chip[0], chip[1], c)
            for a in range(n):
                rem.append(pltpu.make_async_remote_copy(src_ref=ins[a].at[q], dst_ref=outs[a].at[qme], send_sem=send.at[a, j],
                                                        recv_sem=recv.at[a, j], device_id=to, device_id_type=MESH))
                waits.append(pltpu.make_async_remote_copy(src_ref=ins[a].at[q], dst_ref=outs[a].at[q], send_sem=send.at[a, j],
                                                          recv_sem=recv.at[a, j], device_id=to, device_id_type=MESH))
        for r in range(1, N_DEV):
            fx, fy, fc = (r >> 2) & 1, (r >> 1) & 1, r & 1
            px = 1 - x if fx else x
            py = 1 - y if fy else y
            pc = 1 - c if fc else c
            t = 4 * px + 2 * py + pc
            to = (px, py, pc)
            rem.append(pltpu.make_async_remote_copy(src_ref=sh.at[t], dst_ref=osh.at[me], send_sem=ssend.at[0, r - 1],
                                                    recv_sem=srecv.at[0, r - 1], device_id=to, device_id_type=MESH))
            waits.append(pltpu.make_async_remote_copy(src_ref=sh.at[t], dst_ref=osh.at[t], send_sem=ssend.at[0, r - 1],
                                                      recv_sem=srecv.at[0, r - 1], device_id=to, device_id_type=MESH))
            rem.append(pltpu.make_async_remote_copy(src_ref=rep, dst_ref=orep.at[me], send_sem=ssend.at[1, r - 1],
                                                    recv_sem=srecv.at[1, r - 1], device_id=to, device_id_type=MESH))
            waits.append(pltpu.make_async_remote_copy(src_ref=rep, dst_ref=orep.at[t], send_sem=ssend.at[1, r - 1],
                                                      recv_sem=srecv.at[1, r - 1], device_id=to, device_id_type=MESH))
        for d in loc + rem:
            d.start()
        for d in waits:
            d.wait_recv()
        for d in rem:
            d.wait_send()
        for d in loc:
            d.wait()

    out_shape = [jax.ShapeDtypeStruct(a.shape, a.dtype) for a in cs]
    out_shape += [jax.ShapeDtypeStruct(sm_sh.shape, F32), jax.ShapeDtypeStruct((N_DEV,) + sm_rep.shape, F32)]
    res = _pcall(body, name="chip_exchange", in_specs=[ANY] * (n + 2), out_specs=[ANY] * (n + 2), out_shape=out_shape,
                 scratch_shapes=[pltpu.SemaphoreType.DMA((n, 3)), pltpu.SemaphoreType.DMA((n, 3)), pltpu.SemaphoreType.DMA((n,)),
                                 pltpu.SemaphoreType.DMA((2, 7)), pltpu.SemaphoreType.DMA((2, 7)), pltpu.SemaphoreType.DMA((2,))],
                 compiler_params=pltpu.CompilerParams(has_side_effects=True))(*cs, sm_sh, sm_rep)
    return res[:n], res[n], res[n + 1]


BIG = ("mlp_up", "mlp_down", "conv_w_in", "conv_w_out", "pool_w", "sc_w_in", "sc_w_out", "ret_w_in", "ret_w_out")
SMALL_SPLIT = ("norm_g", "conv_dw", "sc_dw")
SMALL_REP = ("conv_b_in", "conv_dw_b", "conv_ln_g", "conv_ln_b", "conv_b_out", "pool_scale")
WEIGHTS = ("norm_g", "mlp_up", "mlp_down", "conv_w_in", "conv_b_in", "conv_dw", "conv_dw_b", "conv_ln_g", "conv_ln_b",
           "conv_w_out", "conv_b_out", "pool_w", "pool_scale", "sc_w_in", "sc_dw", "sc_w_out", "ret_w_in", "ret_w_out")


def _pad_rows(a, rows):
    return jnp.pad(a, ((0, rows - a.shape[0]), (0, 0)))


def _split_rows(names, shards):
    out = []
    for nme in names:
        r = int(np.prod(shards[nme].shape[:-1]))
        out.append((nme, r, -(-r // 8) * 8))
    return out


def _pack_split(names, arrs):
    parts = []
    for nme, r, rp in _split_rows(names, arrs):
        parts.append(_pad_rows(arrs[nme].reshape(r, arrs[nme].shape[-1]), rp))
    return jnp.concatenate(parts, axis=0)


def _pack_rep(arrs, D):
    parts = [arrs[nme].reshape(-1, D) for nme in SMALL_REP]
    rows = sum(p.shape[0] for p in parts)
    return _pad_rows(jnp.concatenate(parts, axis=0), -(-rows // 8) * 8)


def kernel(x, positions, norm_g, mlp_up, mlp_down, conv_w_in, conv_b_in, conv_dw, conv_dw_b, conv_ln_g, conv_ln_b, conv_w_out, conv_b_out, pool_w, pool_scale, sc_w_in, sc_dw, sc_w_out, ret_w_in, ret_w_out, loss_target, m_norm_g, m_mlp_up, m_mlp_down, m_conv_w_in, m_conv_b_in, m_conv_dw, m_conv_dw_b, m_conv_ln_g, m_conv_ln_b, m_conv_w_out, m_conv_b_out, m_pool_w, m_pool_scale, m_sc_w_in, m_sc_dw, m_sc_w_out, m_ret_w_in, m_ret_w_out, v_norm_g, v_mlp_up, v_mlp_down, v_conv_w_in, v_conv_b_in, v_conv_dw, v_conv_dw_b, v_conv_ln_g, v_conv_ln_b, v_conv_w_out, v_conv_b_out, v_pool_w, v_pool_scale, v_sc_w_in, v_sc_dw, v_sc_w_out, v_ret_w_in, v_ret_w_out):
    w = dict(norm_g=norm_g, mlp_up=mlp_up, mlp_down=mlp_down, conv_w_in=conv_w_in, conv_b_in=conv_b_in, conv_dw=conv_dw,
             conv_dw_b=conv_dw_b, conv_ln_g=conv_ln_g, conv_ln_b=conv_ln_b, conv_w_out=conv_w_out, conv_b_out=conv_b_out,
             pool_w=pool_w, pool_scale=pool_scale, sc_w_in=sc_w_in, sc_dw=sc_dw, sc_w_out=sc_w_out, ret_w_in=ret_w_in, ret_w_out=ret_w_out)
    m = dict(norm_g=m_norm_g, mlp_up=m_mlp_up, mlp_down=m_mlp_down, conv_w_in=m_conv_w_in, conv_b_in=m_conv_b_in, conv_dw=m_conv_dw,
             conv_dw_b=m_conv_dw_b, conv_ln_g=m_conv_ln_g, conv_ln_b=m_conv_ln_b, conv_w_out=m_conv_w_out, conv_b_out=m_conv_b_out,
             pool_w=m_pool_w, pool_scale=m_pool_scale, sc_w_in=m_sc_w_in, sc_dw=m_sc_dw, sc_w_out=m_sc_w_out, ret_w_in=m_ret_w_in,
             ret_w_out=m_ret_w_out)
    v = dict(norm_g=v_norm_g, mlp_up=v_mlp_up, mlp_down=v_mlp_down, conv_w_in=v_conv_w_in, conv_b_in=v_conv_b_in, conv_dw=v_conv_dw,
             conv_dw_b=v_conv_dw_b, conv_ln_g=v_conv_ln_g, conv_ln_b=v_conv_ln_b, conv_w_out=v_conv_w_out, conv_b_out=v_conv_b_out,
             pool_w=v_pool_w, pool_scale=v_pool_scale, sc_w_in=v_sc_w_in, sc_dw=v_sc_dw, sc_w_out=v_sc_w_out, ret_w_in=v_ret_w_in,
             ret_w_out=v_ret_w_out)
    T, D = x.shape[1], x.shape[2]
    L = D // N_DEV
    G = len(POOL_WINDOWS)
    cg = D // G

    me = 4 * lax.axis_index("x") + 2 * lax.axis_index("y") + lax.axis_index("c")
    MIX = (("conv_w_in", "conv_w_out"), ("pool_w",), ("sc_w_in", "sc_w_out"), ("ret_w_in", "ret_w_out"))

    def shards(l, part):
        out = {k: w[k].astype(BF16) for k in MIX[l]} if part != "mlp" else {}
        if part != "mix":
            out["mlp_up"], out["mlp_down"] = w["mlp_up"][l].astype(BF16), w["mlp_down"][l].astype(BF16)
        return out

    sh0 = shards(0, "mix")
    gath0 = _gather_all(list(sh0.values()) + [_pack_split(SMALL_SPLIT, w)])
    small = gath0[-1]
    common = {k: w[k].reshape(1, -1) for k in SMALL_REP}
    off = 0
    for nme, r, rp in _split_rows(SMALL_SPLIT, w):
        common[nme] = small[:, off:off + r, :].transpose(1, 0, 2).reshape(r, D)
        off += rp
    flying = {}
    dep = small
    for l, part in [(0, "mlp")] + [(l, "both") for l in range(1, DEPTH)]:
        shl = shards(l, part)
        lands = [lax.dynamic_update_slice(lax.empty((N_DEV,) + a.shape, a.dtype), a[None], (me,) + (0,) * a.ndim) for a in shl.values()]
        send, recv, thru, lands, dep = _gather_start("gather_start_%d" % l, list(shl.values()), lands, [dep])
        flying[l] = dict(names=list(shl.keys()), send=send, recv=recv, thru=thru, lands=lands)

    def pass_on(l, after):
        f = flying[l]
        f["fsend"], f["frecv"], f["thru"], f["lands"], tok = _gather_pass("gather_pass_%d" % l, f["send"], f["recv"], f["thru"],
                                                                          f["lands"], after)
        return tok

    def arrive(l, after):
        f = flying[l]
        got = _gather_wait("gather_wait_%d" % l, f["send"], f["recv"], f["fsend"], f["frecv"], f["thru"], f["lands"], after)
        return dict(zip(f["names"], got))

    def get_w(l, part, after):
        if part == "mlp" and l > 0:
            return {"_deps": [pass_on(l + 1, after)]} if l + 1 < DEPTH else {}
        if l == 0 and part == "mix":
            gw = dict(zip(sh0.keys(), gath0[:-1]))
        elif l == 0:
            gw = arrive(0, pass_on(0, after))
            gw["_deps"] = [pass_on(1, gw["mlp_up"])]
        else:
            gw = arrive(l, after)
        out = dict(common)
        for k, a in gw.items():
            if k.endswith("_w_out"):
                out[k] = a.reshape(-1, D)
            elif k == "pool_w":
                out[k] = a.transpose(1, 0, 2, 3).reshape(G, cg, cg)
            else:
                out[k] = a
        return out

    res = {}
    mlp_res = {k: [lax.empty(w[k].shape, F32) for _ in range(4)] for k in ("mlp_up", "mlp_down")}
    pending = []
    small_g = {}

    def shard2d(src, k, l):
        a = src[k][l] if k in mlp_res else src[k]
        return a.reshape(-1, a.shape[-1])

    def adam_layer(l, names, parts, own):
        for i, k in enumerate(names):
            mine = None if own is None else own[i]
            if k in mlp_res:
                mlp_res[k] = _adamw("adamw_%s_%d" % (k, l), parts[i], w[k], m[k], v[k], mine, layer=l, bufs=mlp_res[k])
            else:
                out = _adamw("adamw_%s_%d" % (k, l), parts[i], shard2d(w, k, l), shard2d(m, k, l), shard2d(v, k, l), mine)
                res[k] = [o.reshape(w[k].shape) for o in out]

    held = {}

    def put_g(l, part, g, after, norm_grads):
        small_g.update({k: g[k] for k in g if k in SMALL_SPLIT or k in SMALL_REP})
        if part == "mlp" and l > 0:
            held.update(g)
            return None
        if pending:
            tag, lp, names, send, recv, sums, lands = pending.pop()
            sums, parts = _chip_wait("chip_wait_" + tag, send, recv, sums, lands, after)
            adam_layer(lp, names, parts, sums)
        last = l == 0 and part == "mix"
        if l > 0:
            g = dict(g, **held)
            held.clear()
            names = list(MIX[l]) + ["mlp_up", "mlp_down"]
        else:
            names = list(MIX[0]) if last else ["mlp_up", "mlp_down"]
        tag = "%d%s" % (l, "" if l > 0 else part)
        stacks = []
        for k in names:
            a = g[k]
            if k == "pool_w":
                a = a.reshape(G, N_DEV, cg // N_DEV, cg).transpose(1, 0, 2, 3).astype(BF16)
            stacks.append(a.reshape(N_CHIP, 2, -1, w[k].shape[-1]))
        got = _pair_exchange("pair_exchange_" + tag, stacks)
        sums = [_pair_add("pair_add", a, b) for a, b in zip(stacks, got)]
        if not last:
            send, recv, sums, lands, tok = _chip_start("chip_start_" + tag, sums, [lax.empty(a.shape, a.dtype) for a in sums])
            pending.append((tag, l, names, send, recv, sums, lands))
            return tok
        small_g["norm_g"] = norm_grads
        gsm = []
        for nme, r, rp in _split_rows(SMALL_SPLIT, w):
            gsm.append(jnp.pad(small_g[nme].reshape(r, N_DEV, L), ((0, rp - r), (0, 0), (0, 0))))
        sm_sh = jnp.concatenate(gsm, axis=0).transpose(1, 0, 2)
        parts, r_sh, r_rep = _chip_exchange(sums, sm_sh, _pack_rep(small_g, D))
        adam_layer(0, names, parts, None)
        out = _adamw("adamw_split", r_sh, _pack_split(SMALL_SPLIT, w), _pack_split(SMALL_SPLIT, m), _pack_split(SMALL_SPLIT, v))
        off = 0
        for nme, r, rp in _split_rows(SMALL_SPLIT, w):
            res[nme] = [o[off:off + r].reshape(w[nme].shape) for o in out]
            off += rp
        out = _adamw("adamw_rep", r_rep, _pack_rep(w, D), _pack_rep(m, D), _pack_rep(v, D))
        off = 0
        for nme in SMALL_REP:
            r = w[nme].shape[0] // D
            res[nme] = [o[off:off + r].reshape(w[nme].shape) for o in out]
            off += r
        return None

    pos = positions.reshape(T, 1).astype(F32)
    loss_row, grad_x = _local_step(x.reshape(T, D), pos, loss_target.reshape(T, D), common["norm_g"], get_w, put_g, [dep], T, D)
    loss = lax.psum(0.5 * jnp.sum(loss_row) / D, AXES)
    res.update(mlp_res)
    outs = [loss, grad_x.reshape(x.shape)]
    for i in range(4):
        outs += [res[k][i] for k in WEIGHTS]
    return tuple(outs)
```

```python
import functools
import math

import numpy as np
import jax
import jax.numpy as jnp
from jax import lax
from jax.experimental import pallas as pl
from jax.experimental.pallas import tpu as pltpu

F32, BF16 = jnp.float32, jnp.bfloat16
MESH = pl.DeviceIdType.MESH
AXES = ("x", "y", "c")
N_DEV = 8
N_CHIP = 4
EPS = 1e-6
DEPTH = 4
CONV_WIDTH = 31
POOL_WINDOWS = (2, 4, 8, 16)
SHORT_CONV_WIDTH = 3
RET_HEADS = 4
RET_CHUNK = 128
ROPE_BASE = 10000.0
ADAM_LR, ADAM_B1, ADAM_B2, ADAM_EPS, ADAM_WD, ADAM_STEP = 0.001, 0.9, 0.999, 1e-08, 0.01, 10
LANES = 128
SEQ_PAD = 32
VMEM_LIMIT = 56 * 1024 * 1024

NN = (((1,), (0,)), ((), ()))
NT = (((1,), (1,)), ((), ()))
TN = (((0,), (0,)), ((), ()))


def _pcall(body, **kw):
    return pl.pallas_call(body, **kw)


def _params(sem):
    return pltpu.CompilerParams(dimension_semantics=sem, vmem_limit_bytes=VMEM_LIMIT)


def _tile(n, pref):
    t = min(n, pref)
    while n % t:
        t //= 2
    return t


def _rowwise(name, fn, ins, outs, rows, tr, deps=()):
    n_in = len(ins)
    n_dep = len(deps)
    in_specs = []
    for arr, nc, cb in ins:
        if nc is None:
            in_specs.append(pl.BlockSpec(arr.shape, lambda i, nd=arr.ndim: (0,) * nd))
        else:
            in_specs.append(pl.BlockSpec((tr, nc), lambda i, cb=cb: (i, cb)))
    out_specs, out_shapes = [], []
    for o in outs:
        if o[0] == "row":
            out_shapes.append(jax.ShapeDtypeStruct((rows, o[1]), o[2]))
            out_specs.append(pl.BlockSpec((tr, o[1]), lambda i: (i, 0)))
        else:
            out_shapes.append(jax.ShapeDtypeStruct(o[1], F32))
            out_specs.append(pl.BlockSpec(o[1], lambda i: (0, 0)))

    in_specs = in_specs + [pl.BlockSpec(memory_space=pl.ANY)] * n_dep

    def body(*refs):
        res = fn(*[r[...] for r in refs[:n_in]])
        if not isinstance(res, (tuple, list)):
            res = (res,)
        i = pl.program_id(0)
        for o, ref, r in zip(outs, refs[n_in + n_dep:], res):
            if o[0] == "row":
                ref[...] = r.astype(o[2])
            else:
                @pl.when(i == 0)
                def _(ref=ref, r=r):
                    ref[...] = r

                @pl.when(i > 0)
                def _(ref=ref, r=r):
                    ref[...] += r

    return _pcall(body, name=name, grid=(rows // tr,), in_specs=in_specs, out_specs=out_specs,
                  out_shape=out_shapes, compiler_params=_params(("arbitrary",)))(*[a for a, _, _ in ins], *deps)


def _rms(x, g):
    return x * lax.rsqrt(jnp.mean(x * x, axis=-1, keepdims=True) + EPS) * g


def _colsum(x):
    return jnp.sum(x, axis=0, keepdims=True)


def _square(r):
    rf = r.astype(F32)
    return (rf * rf).astype(r.dtype)


def _ln_silu(c, g, b):
    mu = jnp.mean(c, axis=-1, keepdims=True)
    xc = c - mu
    y = xc * lax.rsqrt(jnp.mean(xc * xc, axis=-1, keepdims=True) + EPS) * g + b
    return y * jax.nn.sigmoid(y)


def _glu(za, zg):
    return za * jax.nn.sigmoid(zg)


def _gate_norm(o, g):
    dv = o.shape[-1] // RET_HEADS
    parts = []
    for h in range(RET_HEADS):
        oh = o[:, h * dv:(h + 1) * dv]
        parts.append(oh * lax.rsqrt(jnp.mean(oh * oh, axis=-1, keepdims=True) + EPS))
    return (g * jax.nn.sigmoid(g)) * jnp.concatenate(parts, axis=-1)


def _rot(x, cos, sin, sign):
    dk = x.shape[-1] // RET_HEADS
    half = dk // 2
    parts = []
    for h in range(RET_HEADS):
        x1 = x[:, h * dk:h * dk + half]
        x2 = x[:, h * dk + half:(h + 1) * dk]
        parts.append(x1 * cos - sign * (x2 * sin))
        parts.append(sign * (x1 * sin) + x2 * cos)
    return jnp.concatenate(parts, axis=-1)


def _mm(name, mode, grid, ins, outs, epi=None, aliases=None, pre=None, deps=()):
    dn = {"nn": NN, "nt": NT, "tn": TN}[mode]
    gk = grid[2]
    n_in, n_out = len(ins), len(outs)
    n_dep = len(deps)

    def body(*refs):
        out_refs = refs[n_in + n_dep:n_in + n_dep + n_out]
        a = refs[0][...] if pre is None else pre(refs[0][...])
        part = lax.dot_general(a, refs[1][...], dn, preferred_element_type=F32)

        def finish(acc):
            res = epi(acc, *[r[...] for r in refs[2:n_in]]) if epi is not None else (acc,)
            for o, r in zip(out_refs, res):
                o[...] = r.astype(o.dtype)

        if gk == 1:
            finish(part)
        else:
            acc_ref = refs[-1]
            k = pl.program_id(2)

            @pl.when(k == 0)
            def _():
                acc_ref[...] = part

            @pl.when(k > 0)
            def _():
                acc_ref[...] += part

            @pl.when(k == gk - 1)
            def _():
                finish(acc_ref[...])

    scratch = []
    if gk > 1:
        blk = [d for d in outs[0][2].block_shape if d is not None]
        scratch = [pltpu.VMEM(tuple(blk), F32)]
    kw = {}
    if aliases:
        kw["input_output_aliases"] = aliases
    return _pcall(body, name=name, grid=grid, in_specs=[s for _, s in ins] + [pl.BlockSpec(memory_space=pl.ANY)] * n_dep,
                  out_specs=[s for _, _, s in outs],
                  out_shape=[jax.ShapeDtypeStruct(sh, dt) for sh, dt, _ in outs], scratch_shapes=scratch,
                  compiler_params=_params(("parallel", "parallel", "arbitrary")), **kw)(*[a for a, _ in ins], *deps)


def _spec(block, fn):
    return pl.BlockSpec(block, fn)


def _mm_cols(name, a, wg, lead, T, epi=None, extras=(), n_out=1, out_dtype=F32, deps=()):
    K, nb = wg.shape[-2], wg.shape[-1]
    tm = _tile(T, 2048)
    wblock = (None,) * (1 + len(lead)) + (K, nb)
    ins = [(a, _spec((tm, K), lambda i, j, k: (i, 0))), (wg, _spec(wblock, lambda i, j, k: (j,) + tuple(lead) + (0, 0)))]
    for e in extras:
        if e.shape[0] == 1:
            ins.append((e, _spec((1, nb), lambda i, j, k: (0, j))))
        else:
            ins.append((e, _spec((tm, nb), lambda i, j, k: (i, j))))
    outs = [((T, N_DEV * nb), out_dtype, _spec((tm, nb), lambda i, j, k: (i, j))) for _ in range(n_out)]
    return _mm(name, "nn", (T // tm, N_DEV, 1), ins, outs, epi, deps=deps)


def _mm_cols_grad(name, a, d, wg_shape, lead, T):
    K, nb = wg_shape[-2], wg_shape[-1]
    tk = T
    wblock = (None,) * (1 + len(lead)) + (K, nb)
    ins = [(a, _spec((tk, K), lambda i, j, k: (k, 0))), (d, _spec((tk, nb), lambda i, j, k: (k, j)))]
    outs = [(tuple(wg_shape), BF16, _spec(wblock, lambda i, j, k: (j,) + tuple(lead) + (0, 0)))]
    return _mm(name, "tn", (1, N_DEV, T // tk), ins, outs, lambda acc: (acc,))[0]


def _mm_whole_k(name, a, w3, T, split, pre=None, out_dtype=F32):
    nblk, d1, d2 = w3.shape
    kb = d1 if split == "rows" else d2
    N = d2 if split == "rows" else d1
    tm = _tile(T, 512)

    def body(a_ref, w_ref, o_ref, *scratch):
        av = a_ref[...] if pre is None else pre(a_ref[...])
        if split == "rows":
            o_ref[...] = lax.dot_general(av, w_ref[...].reshape(nblk * kb, N), NN, preferred_element_type=F32).astype(out_dtype)
        else:
            wcat, sem = scratch

            @pl.when(pl.program_id(0) == 0)
            def _():
                cps = [pltpu.make_async_copy(w_ref.at[j], wcat.at[:, pl.ds(j * kb, kb)], sem.at[j]) for j in range(nblk)]
                for cp in cps:
                    cp.start()
                for cp in cps:
                    cp.wait()

            o_ref[...] = lax.dot_general(av, wcat[...], NT, preferred_element_type=F32).astype(out_dtype)

    rows = split == "rows"
    scratch = [] if rows else [pltpu.VMEM((N, nblk * kb), w3.dtype), pltpu.SemaphoreType.DMA((nblk,))]
    return _pcall(body, name=name, grid=(T // tm,),
                  in_specs=[pl.BlockSpec((tm, nblk * kb), lambda i: (i, 0)),
                            pl.BlockSpec(w3.shape, lambda i: (0, 0, 0)) if rows else pl.BlockSpec(memory_space=pl.ANY)],
                  out_specs=pl.BlockSpec((tm, N), lambda i: (i, 0)), out_shape=jax.ShapeDtypeStruct((T, N), out_dtype),
                  scratch_shapes=scratch, compiler_params=_params(("arbitrary",)))(a, w3)


def _mm_rows(name, a, w2, T, epi=None, extras=(), out_dtype=F32):
    Kin, N = w2.shape
    tm, tn = _tile(T, 2048), _tile(N, 512)
    ins = [(a, _spec((tm, Kin), lambda i, j, k: (i, 0))), (w2, _spec((Kin, tn), lambda i, j, k: (0, j)))]
    for e in extras:
        ins.append((e, _spec((1, tn), lambda i, j, k: (0, j))))
    outs = [((T, N), out_dtype, _spec((tm, tn), lambda i, j, k: (i, j)))]
    return _mm(name, "nn", (T // tm, N // tn, 1), ins, outs, epi)[0]


def _mm_rows_t(name, d, w2, T, out_dtype=F32):
    Kin, N = w2.shape
    tm, tn = _tile(T, 2048), _tile(Kin, 512)
    ins = [(d, _spec((tm, N), lambda i, j, k: (i, 0))), (w2, _spec((tn, N), lambda i, j, k: (j, 0)))]
    outs = [((T, Kin), out_dtype, _spec((tm, tn), lambda i, j, k: (i, j)))]
    return _mm(name, "nt", (T // tm, Kin // tn, 1), ins, outs)[0]


def _mm_rows_grad(name, a, d, T):
    Kin, N = a.shape[1], d.shape[1]
    tk, tm, tn = _tile(T, 2048), _tile(Kin, 512), _tile(N, 1024)
    ins = [(a, _spec((tk, tm), lambda i, j, k: (k, i))), (d, _spec((tk, tn), lambda i, j, k: (k, j)))]
    outs = [((Kin, N), BF16, _spec((tm, tn), lambda i, j, k: (i, j)))]
    return _mm(name, "tn", (Kin // tm, N // tn, T // tk), ins, outs, lambda acc: (acc,))[0]


def _dwconv_fwd(name, x, w, bias, T):
    W, C = w.shape
    cb = min(C, LANES)
    tc = _tile(T, 256)

    def body(x_ref, w_ref, b_ref, o_ref, xp_ref):
        xp_ref[pl.ds(0, SEQ_PAD), :] = jnp.zeros((SEQ_PAD, cb), F32)

        def fill(c, _):
            base = pl.multiple_of(c * tc, tc)
            xp_ref[pl.ds(base + SEQ_PAD, tc), :] = x_ref[pl.ds(base, tc), :]
            return 0

        lax.fori_loop(0, T // tc, fill, 0)

        def chunk(c, _):
            base = pl.multiple_of(c * tc, tc)
            acc = jnp.zeros((tc, cb), F32) + b_ref[...]
            for k in range(W):
                acc = acc + w_ref[pl.ds(k, 1), :] * xp_ref[pl.ds(base + SEQ_PAD - (W - 1) + k, tc), :]
            o_ref[pl.ds(base, tc), :] = acc
            return 0

        lax.fori_loop(0, T // tc, chunk, 0)

    return _pcall(body, name=name, grid=(C // cb,),
                  in_specs=[pl.BlockSpec((T, cb), lambda i: (0, i)), pl.BlockSpec((W, cb), lambda i: (0, i)),
                            pl.BlockSpec((1, cb), lambda i: (0, i))],
                  out_specs=pl.BlockSpec((T, cb), lambda i: (0, i)), out_shape=jax.ShapeDtypeStruct((T, C), F32),
                  scratch_shapes=[pltpu.VMEM((T + SEQ_PAD, cb), F32)],
                  compiler_params=_params(("parallel",)))(x, w, bias)


def _dwconv_bwd(name, dc, x, w, T):
    W, C = w.shape
    cb = min(C, LANES)
    tc = _tile(T, 128)

    def body(dc_ref, x_ref, w_ref, dx_ref, dw_ref, xp_ref, dp_ref, acc_ref):
        xp_ref[pl.ds(0, SEQ_PAD), :] = jnp.zeros((SEQ_PAD, cb), F32)
        dp_ref[pl.ds(T, SEQ_PAD), :] = jnp.zeros((SEQ_PAD, cb), F32)
        acc_ref[...] = jnp.zeros((W * 8, cb), F32)

        def fill(c, _):
            base = pl.multiple_of(c * tc, tc)
            xp_ref[pl.ds(base + SEQ_PAD, tc), :] = x_ref[pl.ds(base, tc), :]
            dp_ref[pl.ds(base, tc), :] = dc_ref[pl.ds(base, tc), :]
            return 0

        lax.fori_loop(0, T // tc, fill, 0)

        def chunk(c, _):
            base = pl.multiple_of(c * tc, tc)
            d = dc_ref[pl.ds(base, tc), :]
            acc = jnp.zeros((tc, cb), F32)
            for k in range(W):
                acc = acc + w_ref[pl.ds(k, 1), :] * dp_ref[pl.ds(base + (W - 1) - k, tc), :]
                prod = d * xp_ref[pl.ds(base + SEQ_PAD - (W - 1) + k, tc), :]
                acc_ref[pl.ds(8 * k, 8), :] += jnp.sum(prod.reshape(tc // 8, 8, cb), axis=0)
            dx_ref[pl.ds(base, tc), :] = acc
            return 0

        lax.fori_loop(0, T // tc, chunk, 0)
        for k in range(W):
            dw_ref[pl.ds(k, 1), :] = jnp.sum(acc_ref[pl.ds(8 * k, 8), :], axis=0, keepdims=True)

    return _pcall(body, name=name, grid=(C // cb,),
                  in_specs=[pl.BlockSpec((T, cb), lambda i: (0, i)), pl.BlockSpec((T, cb), lambda i: (0, i)),
                            pl.BlockSpec((W, cb), lambda i: (0, i))],
                  out_specs=[pl.BlockSpec((T, cb), lambda i: (0, i)), pl.BlockSpec((W, cb), lambda i: (0, i))],
                  out_shape=[jax.ShapeDtypeStruct((T, C), F32), jax.ShapeDtypeStruct((W, C), F32)],
                  scratch_shapes=[pltpu.VMEM((T + SEQ_PAD, cb), F32), pltpu.VMEM((T + SEQ_PAD, cb), F32),
                                  pltpu.VMEM((W * 8, cb), F32)],
                  compiler_params=_params(("parallel",)))(dc, x, w)


def _pool_count(base, tc, cg, win):
    t = (lax.broadcasted_iota(jnp.int32, (tc, cg), 0) + base + 1).astype(F32)
    return jnp.minimum(t, float(win))


def _pool_fwd(name, u, T):
    D = u.shape[1]
    G = len(POOL_WINDOWS)
    cg = min(D // G, LANES)
    nb = (D // G) // cg
    tc = _tile(T, 128)

    def body(u_ref, o_ref, up_ref):
        g = pl.program_id(0)
        up_ref[pl.ds(0, SEQ_PAD), :] = jnp.zeros((SEQ_PAD, cg), F32)

        def fill(c, _):
            base = pl.multiple_of(c * tc, tc)
            up_ref[pl.ds(base + SEQ_PAD, tc), :] = u_ref[pl.ds(base, tc), :]
            return 0

        lax.fori_loop(0, T // tc, fill, 0)
        for gi, win in enumerate(POOL_WINDOWS):
            @pl.when(g == gi)
            def _(win=win):
                def chunk(c, _):
                    base = pl.multiple_of(c * tc, tc)
                    acc = up_ref[pl.ds(base + SEQ_PAD, tc), :]
                    for j in range(1, win):
                        acc = acc + up_ref[pl.ds(base + SEQ_PAD - j, tc), :]
                    mixed = acc / _pool_count(base, tc, cg, win) - u_ref[pl.ds(base, tc), :]
                    o_ref[pl.ds(base, tc), :] = mixed.astype(BF16)
                    return 0

                lax.fori_loop(0, T // tc, chunk, 0)

    return _pcall(body, name=name, grid=(G, nb), in_specs=[pl.BlockSpec((T, cg), lambda g, i: (0, g * nb + i))],
                  out_specs=pl.BlockSpec((T, cg), lambda g, i: (0, g * nb + i)), out_shape=jax.ShapeDtypeStruct((T, D), BF16),
                  scratch_shapes=[pltpu.VMEM((T + SEQ_PAD, cg), F32)], compiler_params=_params(("parallel", "parallel")))(u)


def _pool_bwd(name, dm, T):
    D = dm.shape[1]
    G = len(POOL_WINDOWS)
    cg = min(D // G, LANES)
    nb = (D // G) // cg
    tc = _tile(T, 128)

    def body(d_ref, o_ref, qp_ref):
        g = pl.program_id(0)
        qp_ref[pl.ds(T, SEQ_PAD), :] = jnp.zeros((SEQ_PAD, cg), F32)
        for gi, win in enumerate(POOL_WINDOWS):
            @pl.when(g == gi)
            def _(win=win):
                def fill(c, _):
                    base = pl.multiple_of(c * tc, tc)
                    qp_ref[pl.ds(base, tc), :] = d_ref[pl.ds(base, tc), :] / _pool_count(base, tc, cg, win)
                    return 0

                lax.fori_loop(0, T // tc, fill, 0)

                def chunk(c, _):
                    base = pl.multiple_of(c * tc, tc)
                    acc = qp_ref[pl.ds(base, tc), :]
                    for j in range(1, win):
                        acc = acc + qp_ref[pl.ds(base + j, tc), :]
                    o_ref[pl.ds(base, tc), :] = (acc - d_ref[pl.ds(base, tc), :]).astype(BF16)
                    return 0

                lax.fori_loop(0, T // tc, chunk, 0)

    return _pcall(body, name=name, grid=(G, nb), in_specs=[pl.BlockSpec((T, cg), lambda g, i: (0, g * nb + i))],
                  out_specs=pl.BlockSpec((T, cg), lambda g, i: (0, g * nb + i)), out_shape=jax.ShapeDtypeStruct((T, D), BF16),
                  scratch_shapes=[pltpu.VMEM((T + SEQ_PAD, cg), F32)], compiler_params=_params(("parallel", "parallel")))(dm)


def _pool_mm(name, mode, a, b, T, scale=None):
    G = len(POOL_WINDOWS)
    D = a.shape[1]
    cg = D // G
    tm = _tile(T, 2048)
    if mode == "tn":
        ins = [(a, _spec((tm, cg), lambda i, j, k: (k, j))), (b, _spec((tm, cg), lambda i, j, k: (k, j)))]
        outs = [((G, cg, cg), F32, _spec((None, cg, cg), lambda i, j, k: (j, 0, 0)))]
        return _mm(name, "tn", (1, G, T // tm), ins, outs, lambda acc: (acc,))[0]
    ins = [(a, _spec((tm, cg), lambda i, j, k: (i, j))), (b, _spec((None, cg, cg), lambda i, j, k: (j, 0, 0)))]
    if scale is not None:
        ins.append((scale, _spec((1, cg), lambda i, j, k: (0, j))))
        outs = [((T, D), BF16, _spec((tm, cg), lambda i, j, k: (i, j))), ((T, D), BF16, _spec((tm, cg), lambda i, j, k: (i, j)))]
        return _mm(name, mode, (T // tm, G, 1), ins, outs, lambda acc, s: (acc * s, acc))
    outs = [((T, D), F32, _spec((tm, cg), lambda i, j, k: (i, j)))]
    return _mm(name, mode, (T // tm, G, 1), ins, outs)[0]


def _sc_mid_fwd(name, z, w, T):
    D = w.shape[1]
    W = w.shape[0]
    cb = min(D, LANES)
    nb = D // cb
    tc = _tile(T, 256)

    def body(b_ref, c_ref, v_ref, w_ref, o_ref, pp_ref):
        pp_ref[pl.ds(0, SEQ_PAD), :] = jnp.zeros((SEQ_PAD, cb), F32)

        def fill(c, _):
            base = pl.multiple_of(c * tc, tc)
            pp_ref[pl.ds(base + SEQ_PAD, tc), :] = c_ref[pl.ds(base, tc), :].astype(F32) * v_ref[pl.ds(base, tc), :].astype(F32)
            return 0

        lax.fori_loop(0, T // tc, fill, 0)

        def chunk(c, _):
            base = pl.multiple_of(c * tc, tc)
            q = jnp.zeros((tc, cb), F32)
            for k in range(W):
                q = q + w_ref[pl.ds(k, 1), :] * pp_ref[pl.ds(base + SEQ_PAD - (W - 1) + k, tc), :]
            o_ref[pl.ds(base, tc), :] = (b_ref[pl.ds(base, tc), :].astype(F32) * q).astype(BF16)
            return 0

        lax.fori_loop(0, T // tc, chunk, 0)

    return _pcall(body, name=name, grid=(nb,),
                  in_specs=[pl.BlockSpec((T, cb), lambda i: (0, i)), pl.BlockSpec((T, cb), lambda i: (0, nb + i)),
                            pl.BlockSpec((T, cb), lambda i: (0, 2 * nb + i)), pl.BlockSpec((W, cb), lambda i: (0, i))],
                  out_specs=pl.BlockSpec((T, cb), lambda i: (0, i)), out_shape=jax.ShapeDtypeStruct((T, D), BF16),
                  scratch_shapes=[pltpu.VMEM((T + SEQ_PAD, cb), F32)],
                  compiler_params=_params(("parallel",)))(z, z, z, w)


def _sc_mid_bwd(name, z, dm, w, T):
    D = w.shape[1]
    W = w.shape[0]
    cb = min(D, LANES)
    nb = D // cb
    tc = _tile(T, 128)

    def body(b_ref, c_ref, v_ref, d_ref, w_ref, dz_ref, dw_ref, pp_ref, dq_ref, acc_ref):
        part = pl.program_id(0)
        pp_ref[pl.ds(0, SEQ_PAD), :] = jnp.zeros((SEQ_PAD, cb), F32)
        dq_ref[pl.ds(T, SEQ_PAD), :] = jnp.zeros((SEQ_PAD, cb), F32)
        acc_ref[...] = jnp.zeros((W * 8, cb), F32)

        def fill(c, _):
            base = pl.multiple_of(c * tc, tc)
            pp_ref[pl.ds(base + SEQ_PAD, tc), :] = c_ref[pl.ds(base, tc), :].astype(F32) * v_ref[pl.ds(base, tc), :].astype(F32)
            dq_ref[pl.ds(base, tc), :] = d_ref[pl.ds(base, tc), :] * b_ref[pl.ds(base, tc), :].astype(F32)
            return 0

        lax.fori_loop(0, T // tc, fill, 0)

        @pl.when(part == 0)
        def _():
            def chunk(c, _):
                base = pl.multiple_of(c * tc, tc)
                dq = dq_ref[pl.ds(base, tc), :]
                q = jnp.zeros((tc, cb), F32)
                for k in range(W):
                    p = pp_ref[pl.ds(base + SEQ_PAD - (W - 1) + k, tc), :]
                    q = q + w_ref[pl.ds(k, 1), :] * p
                    acc_ref[pl.ds(8 * k, 8), :] += jnp.sum((dq * p).reshape(tc // 8, 8, cb), axis=0)
                dz_ref[pl.ds(base, tc), :] = (d_ref[pl.ds(base, tc), :] * q).astype(BF16)
                return 0

            lax.fori_loop(0, T // tc, chunk, 0)

        @pl.when(part > 0)
        def _():
            def chunk(c, _):
                base = pl.multiple_of(c * tc, tc)
                dp = jnp.zeros((tc, cb), F32)
                for k in range(W):
                    dp = dp + w_ref[pl.ds(k, 1), :] * dq_ref[pl.ds(base + (W - 1) - k, tc), :]
                other = jnp.where(part == 1, v_ref[pl.ds(base, tc), :], c_ref[pl.ds(base, tc), :]).astype(F32)
                dz_ref[pl.ds(base, tc), :] = (dp * other).astype(BF16)
                return 0

            lax.fori_loop(0, T // tc, chunk, 0)

        for k in range(W):
            dw_ref[pl.ds(k, 1), :] = jnp.sum(acc_ref[pl.ds(8 * k, 8), :], axis=0, keepdims=True)

    dz, dw3 = _pcall(body, name=name, grid=(3, nb),
                     in_specs=[pl.BlockSpec((T, cb), lambda p, i: (0, i)), pl.BlockSpec((T, cb), lambda p, i: (0, nb + i)),
                               pl.BlockSpec((T, cb), lambda p, i: (0, 2 * nb + i)), pl.BlockSpec((T, cb), lambda p, i: (0, i)),
                               pl.BlockSpec((W, cb), lambda p, i: (0, i))],
                     out_specs=[pl.BlockSpec((T, cb), lambda p, i: (0, p * nb + i)),
                                pl.BlockSpec((None, W, cb), lambda p, i: (p, 0, i))],
                     out_shape=[jax.ShapeDtypeStruct((T, 3 * D), BF16), jax.ShapeDtypeStruct((3, W, D), F32)],
                     scratch_shapes=[pltpu.VMEM((T + SEQ_PAD, cb), F32), pltpu.VMEM((T + SEQ_PAD, cb), F32),
                                     pltpu.VMEM((W * 8, cb), F32)],
                     compiler_params=_params(("parallel", "parallel")))(z, z, z, dm, w)
    return dz, dw3


def _ret_tables(dk, dv):
    C = RET_CHUNK
    lg = np.log1p(-np.exp2(-5.0 - np.arange(RET_HEADS, dtype=np.float64)))
    idx = np.arange(C, dtype=np.float64)
    rel = idx[:, None] - idx[None, :]
    mask = np.where(rel >= 0, np.exp(lg[:, None, None] * np.maximum(rel, 0.0)), 0.0)
    qd = np.exp(lg[:, None] * (idx + 1.0))[:, :, None] * np.ones((1, 1, dk))
    kd = np.exp(lg[:, None] * (C - 1.0 - idx))[:, :, None] * np.ones((1, 1, dk))
    cd = np.exp(lg * C)[:, None, None] * np.ones((1, 1, dv))
    return tuple(jnp.asarray(a, F32) for a in (mask, qd, kd, cd))


def _ret_fwd(name, q, k, v, tabs, T):
    H, C = RET_HEADS, RET_CHUNK
    dk, dv = q.shape[1] // H, v.shape[1] // H
    nch = T // C
    mask, qd, kd, cd = tabs

    def body(q_ref, k_ref, v_ref, m_ref, qd_ref, kd_ref, cd_ref, o_ref, st_ref, s_ref):
        @pl.when(pl.program_id(0) == 0)
        def _():
            s_ref[...] = jnp.zeros((H, dk, dv), F32)

        for h in range(H):
            qv, kv = q_ref[:, h * dk:(h + 1) * dk], k_ref[:, h * dk:(h + 1) * dk]
            vv = v_ref[:, h * dv:(h + 1) * dv]
            s = s_ref[h]
            sb = s.astype(BF16)
            st_ref[h] = sb
            scores = lax.dot_general(qv, kv, NT, preferred_element_type=F32) * m_ref[h]
            intra = lax.dot_general(scores.astype(BF16), vv, NN, preferred_element_type=F32)
            cross = lax.dot_general((qv.astype(F32) * qd_ref[h]).astype(BF16), sb, NN, preferred_element_type=F32)
            o_ref[:, h * dv:(h + 1) * dv] = intra + cross
            upd = lax.dot_general((kv.astype(F32) * kd_ref[h]).astype(BF16), vv, TN, preferred_element_type=F32)
            s_ref[h] = s * cd_ref[h] + upd

    whole = lambda a: pl.BlockSpec(a.shape, lambda n: (0, 0, 0))
    return _pcall(body, name=name, grid=(nch,),
                  in_specs=[pl.BlockSpec((C, H * dk), lambda n: (n, 0)), pl.BlockSpec((C, H * dk), lambda n: (n, 0)),
                            pl.BlockSpec((C, H * dv), lambda n: (n, 0)), whole(mask), whole(qd), whole(kd), whole(cd)],
                  out_specs=[pl.BlockSpec((C, H * dv), lambda n: (n, 0)),
                             pl.BlockSpec((H, None, dk, dv), lambda n: (0, n, 0, 0))],
                  out_shape=[jax.ShapeDtypeStruct((T, H * dv), F32), jax.ShapeDtypeStruct((H, nch, dk, dv), BF16)],
                  scratch_shapes=[pltpu.VMEM((H, dk, dv), F32)],
                  compiler_params=_params(("arbitrary",)))(q, k, v, mask, qd, kd, cd)


def _ret_bwd(name, q, k, v, do, st, tabs, T):
    H, C = RET_HEADS, RET_CHUNK
    dk, dv = q.shape[1] // H, v.shape[1] // H
    nch = T // C
    mask, qd, kd, cd = tabs

    def body(q_ref, k_ref, v_ref, do_ref, st_ref, m_ref, qd_ref, kd_ref, cd_ref, dq_ref, dk_ref, dv_ref, ds_ref):
        @pl.when(pl.program_id(0) == 0)
        def _():
            ds_ref[...] = jnp.zeros((H, dk, dv), F32)

        for h in range(H):
            qv, kv = q_ref[:, h * dk:(h + 1) * dk], k_ref[:, h * dk:(h + 1) * dk]
            vv, dov = v_ref[:, h * dv:(h + 1) * dv], do_ref[:, h * dv:(h + 1) * dv]
            sb = st_ref[h]
            m = m_ref[h]
            ds = ds_ref[h]
            dsb = ds.astype(BF16)
            a = (lax.dot_general(qv, kv, NT, preferred_element_type=F32) * m).astype(BF16)
            dp = (lax.dot_general(dov, vv, NT, preferred_element_type=F32) * m).astype(BF16)
            qa = (qv.astype(F32) * qd_ref[h]).astype(BF16)
            kb = (kv.astype(F32) * kd_ref[h]).astype(BF16)
            dq_ref[:, h * dk:(h + 1) * dk] = (lax.dot_general(dp, kv, NN, preferred_element_type=F32)
                                              + qd_ref[h] * lax.dot_general(dov, sb, NT, preferred_element_type=F32))
            dk_ref[:, h * dk:(h + 1) * dk] = (lax.dot_general(dp, qv, TN, preferred_element_type=F32)
                                              + kd_ref[h] * lax.dot_general(vv, dsb, NT, preferred_element_type=F32))
            dv_ref[:, h * dv:(h + 1) * dv] = (lax.dot_general(a, dov, TN, preferred_element_type=F32)
                                              + lax.dot_general(kb, dsb, NN, preferred_element_type=F32)).astype(BF16)
            ds_ref[h] = ds * cd_ref[h] + lax.dot_general(qa, dov, TN, preferred_element_type=F32)

    rev = lambda n: (nch - 1 - n, 0)
    whole = lambda a: pl.BlockSpec(a.shape, lambda n: (0, 0, 0))
    return _pcall(body, name=name, grid=(nch,),
                  in_specs=[pl.BlockSpec((C, H * dk), rev), pl.BlockSpec((C, H * dk), rev), pl.BlockSpec((C, H * dv), rev),
                            pl.BlockSpec((C, H * dv), rev), pl.BlockSpec((H, None, dk, dv), lambda n: (0, nch - 1 - n, 0, 0)),
                            whole(mask), whole(qd), whole(kd), whole(cd)],
                  out_specs=[pl.BlockSpec((C, H * dk), rev), pl.BlockSpec((C, H * dk), rev), pl.BlockSpec((C, H * dv), rev)],
                  out_shape=[jax.ShapeDtypeStruct((T, H * dk), F32), jax.ShapeDtypeStruct((T, H * dk), F32),
                             jax.ShapeDtypeStruct((T, H * dv), BF16)],
                  scratch_shapes=[pltpu.VMEM((H, dk, dv), F32)],
                  compiler_params=_params(("arbitrary",)))(q, k, v, do, st, mask, qd, kd, cd)


def _local_step(x, pos, tgt, ng, get_w, put_g, first_deps, T, D):
    tr = _tile(T, 512)
    trw = _tile(T, 256)
    row = lambda i: ng[i:i + 1]

    u = _rowwise("rms_first", lambda xv, gv: _rms(xv, gv), [(x, D, 0), (row(0), None, None)], [("row", D, BF16)], T, tr,
                 deps=first_deps)[0]
    h = x
    saved = []
    cos = sin = None
    tabs = None
    for l in range(DEPTH):
        mixer = l % 4
        wts = get_w(l, "mix", h)
        sv = {"h_in": h, "u": u}
        if mixer == 0:
            z = _mm_cols("conv_in", u, wts["conv_w_in"], (), T, lambda acc, b: (acc + b,), (wts["conv_b_in"],), 1, BF16)[0]
            v = _rowwise("conv_glu", lambda za, zg: _glu(za.astype(F32), zg.astype(F32)), [(z, D, 0), (z, D, 1)], [("row", D, F32)], T, tr)[0]
            c = _dwconv_fwd("conv_dw", v, wts["conv_dw"], wts["conv_dw_b"], T)
            s = _rowwise("conv_ln", _ln_silu, [(c, D, 0), (wts["conv_ln_g"], None, None), (wts["conv_ln_b"], None, None)],
                         [("row", D, BF16)], T, tr)[0]
            y = _mm_rows("conv_out", s, wts["conv_w_out"], T, lambda acc, b: (acc + b,), (wts["conv_b_out"],), BF16)
            sv.update(z=z, v=v, c=c, s=s)
        elif mixer == 1:
            mixed = _pool_fwd("pool_fwd", u, T)
            y, ycat = _pool_mm("pool_mm", "nn", mixed, wts["pool_w"], T, wts["pool_scale"])
            sv.update(mixed=mixed, ycat=ycat)
        elif mixer == 2:
            z = _mm_cols("sc_in", u, wts["sc_w_in"], (), T, out_dtype=BF16)[0]
            m = _sc_mid_fwd("sc_mid", z, wts["sc_dw"], T)
            y = _mm_rows("sc_out", m, wts["sc_w_out"], T, out_dtype=BF16)
            sv.update(z=z, m=m)
        else:
            dk = D // RET_HEADS
            half = dk // 2
            inv_freq = jnp.asarray((ROPE_BASE ** (-np.arange(half, dtype=np.float64) / half)).astype(np.float32)).reshape(1, half)

            def _cs(p, f):
                ang = p * f
                return jnp.cos(ang), jnp.sin(ang)

            cos, sin = _rowwise("rope_tab", _cs, [(pos, 1, 0), (inv_freq, None, None)],
                                [("row", half, F32), ("row", half, F32)], T, tr)
            tabs = _ret_tables(dk, 2 * dk)
            z = _mm_cols("ret_in", u, wts["ret_w_in"], (), T, out_dtype=BF16)[0]

            def _pre(zq, zk, zv, cv, sn):
                return _rot(zq.astype(F32), cv, sn, 1.0), _rot(zk.astype(F32), cv, sn, 1.0) * (dk ** -0.5), zv

            qr, kr, vb = _rowwise("ret_pre", _pre, [(z, D, 0), (z, D, 1), (z, 2 * D, 1), (cos, half, 0), (sin, half, 0)],
                                  [("row", D, BF16), ("row", D, BF16), ("row", 2 * D, BF16)], T, trw)
            o, st = _ret_fwd("ret_fwd", qr, kr, vb, tabs, T)
            out = _rowwise("ret_post", lambda ov, gv: _gate_norm(ov, gv.astype(F32)), [(o, 2 * D, 0), (z, 2 * D, 2)], [("row", 2 * D, BF16)], T, trw)[0]
            y = _mm_rows("ret_out", out, wts["ret_w_out"], T, out_dtype=BF16)
            sv.update(z=z, qr=qr, kr=kr, vb=vb, o=o, st=st, out=out)
        sv["y1"] = y
        h, u = _rowwise("bound_a", lambda hv, yv, gp, gn: (lambda hn: (hn, _rms(hn, gn)))(hv + _rms(yv.astype(F32), gp)),
                        [(h, D, 0), (y, D, 0), (row(4 * l + 1), None, None), (row(4 * l + 2), None, None)],
                        [("row", D, F32), ("row", D, BF16)], T, tr)
        sv["h_mid"], sv["u2"] = h, u
        wm = get_w(l, "mlp", h)
        mlp_deps = wm.pop("_deps", [])
        wts = dict(wts, **wm)
        sv["w"] = wts
        r = _mm_cols("mlp_up", u, wts["mlp_up"], (), T, lambda acc: (jnp.maximum(acc, 0.0),), (), 1, BF16, deps=mlp_deps)[0]
        y = _mm_whole_k("mlp_down", r, wts["mlp_down"], T, "rows", pre=_square, out_dtype=BF16)
        sv.update(r=r, y2=y)
        if l + 1 < DEPTH:
            udt = F32 if (l + 1) % 4 == 1 else BF16
            h, u = _rowwise("bound_b", lambda hv, yv, gp, gn: (lambda hn: (hn, _rms(hn, gn)))(hv + _rms(yv.astype(F32), gp)),
                            [(h, D, 0), (y, D, 0), (row(4 * l + 3), None, None), (row(4 * l + 4), None, None)],
                            [("row", D, F32), ("row", D, udt)], T, tr)
        saved.append(sv)

    def _final(hv, yv, gp, tv):
        yf = yv.astype(F32)
        out, vjp = jax.vjp(_rms, yf, gp)
        e = hv + out - tv
        dhv = e * (1.0 / D)
        dy, dgp = vjp(dhv)
        return dhv, dy, _colsum(e * e), dgp

    gn = [None] * (4 * DEPTH)
    dh, dy_last, loss_row, gn[4 * DEPTH - 1] = _rowwise(
        "final", _final, [(h, D, 0), (y, D, 0), (row(4 * DEPTH - 1), None, None), (tgt, D, 0)],
        [("row", D, F32), ("row", D, BF16), ("acc", (1, D)), ("acc", (1, D))], T, tr)

    F = saved[0]["r"].shape[1]
    du_next = None
    tok = None
    for l in reversed(range(DEPTH)):
        sv = saved[l]
        wts = sv["w"]
        g = {}
        deps = [] if tok is None else [tok]
        mixer = l % 4
        if du_next is None:
            dy = dy_last
        else:
            dh, dy, gn[4 * l + 3], gn[4 * l + 4], _ = _bound_bwd("bwd_b", dh, du_next, saved[l + 1]["h_in"], sv["y2"],
                                                              row(4 * l + 3), row(4 * l + 4), T, D, tr, deps)
        F8 = F // N_DEV
        tm = _tile(T, 2048)
        da = _mm("mlp_dhid", "nt", (T // tm, N_DEV, 1),
                 [(dy, _spec((tm, D), lambda i, j, k: (i, 0))),
                  (wts["mlp_down"], _spec((None, F8, D), lambda i, j, k: (j, 0, 0))),
                  (sv["r"], _spec((tm, F8), lambda i, j, k: (i, j)))],
                 [((T, F), BF16, _spec((tm, F8), lambda i, j, k: (i, j)))],
                 lambda acc, rr: (acc * (2.0 * rr.astype(F32)),))[0]
        tk = T
        g["mlp_down"] = _mm("mlp_dwdown", "tn", (1, N_DEV, T // tk),
                            [(sv["r"], _spec((tk, F8), lambda i, j, k: (k, j))), (dy, _spec((tk, D), lambda i, j, k: (k, 0)))],
                            [(wts["mlp_down"].shape, BF16, _spec((None, F8, D), lambda i, j, k: (j, 0, 0)))],
                            lambda acc: (acc,), pre=_square)[0]
        g["mlp_up"] = _mm_cols_grad("mlp_dwup", sv["u2"], da, wts["mlp_up"].shape, (), T)
        du = _mm_whole_k("mlp_du", da, wts["mlp_up"], T, "cols", out_dtype=BF16)
        tok = put_g(l, "mlp", g, du, None)
        g = {}
        dh, dy, gn[4 * l + 1], gn[4 * l + 2], dy_sum = _bound_bwd("bwd_a", dh, du, sv["h_mid"], sv["y1"],
                                                                row(4 * l + 1), row(4 * l + 2), T, D, tr, [] if tok is None else [tok])
        u = sv["u"]
        if mixer == 0:
            g["conv_b_out"] = dy_sum
            ds = _mm_rows_t("conv_ds", dy, wts["conv_w_out"], T)
            g["conv_w_out"] = _mm_rows_grad("conv_dwout", sv["s"], dy, T)

            def _lnb(cv, gv, bv, dsv):
                _, vjp = jax.vjp(_ln_silu, cv, gv, bv)
                dc, dg, db = vjp(dsv)
                return dc, dg, db, _colsum(dc)

            dc, g["conv_ln_g"], g["conv_ln_b"], g["conv_dw_b"] = _rowwise(
                "conv_ln_bwd", _lnb, [(sv["c"], D, 0), (wts["conv_ln_g"], None, None), (wts["conv_ln_b"], None, None), (ds, D, 0)],
                [("row", D, F32), ("acc", (1, D)), ("acc", (1, D)), ("acc", (1, D))], T, tr)
            dv, g["conv_dw"] = _dwconv_bwd("conv_dw_bwd", dc, sv["v"], wts["conv_dw"], T)

            def _glub(za, zg, dvv):
                _, vjp = jax.vjp(_glu, za.astype(F32), zg.astype(F32))
                dza, dzg = vjp(dvv)
                dz = jnp.concatenate([dza, dzg], axis=-1)
                return dz, _colsum(dz)

            dz, g["conv_b_in"] = _rowwise("conv_glu_bwd", _glub, [(sv["z"], D, 0), (sv["z"], D, 1), (dv, D, 0)],
                                          [("row", 2 * D, BF16), ("acc", (1, 2 * D))], T, tr)
            g["conv_w_in"] = _mm_cols_grad("conv_dwin", u, dz, wts["conv_w_in"].shape, (), T)
            du = _mm_whole_k("conv_du", dz, wts["conv_w_in"], T, "cols", out_dtype=BF16)
        elif mixer == 1:
            def _psb(dyv, yc, sc):
                return dyv * sc, _colsum(dyv.astype(F32) * yc.astype(F32))

            dyg, g["pool_scale"] = _rowwise("pool_scale_bwd", _psb, [(dy, D, 0), (sv["ycat"], D, 0), (wts["pool_scale"], None, None)],
                                            [("row", D, BF16), ("acc", (1, D))], T, tr)
            dmix = _pool_mm("pool_dmix", "nt", dyg, wts["pool_w"], T)
            g["pool_w"] = _pool_mm("pool_dw", "tn", sv["mixed"], dyg, T)
            du = _pool_bwd("pool_bwd", dmix, T)
        elif mixer == 2:
            dm = _mm_rows_t("sc_dm", dy, wts["sc_w_out"], T)
            g["sc_w_out"] = _mm_rows_grad("sc_dwout", sv["m"], dy, T)
            dz, dw3 = _sc_mid_bwd("sc_mid_bwd", sv["z"], dm, wts["sc_dw"], T)
            g["sc_dw"] = dw3[0]
            g["sc_w_in"] = _mm_cols_grad("sc_dwin", u, dz, wts["sc_w_in"].shape, (), T)
            du = _mm_whole_k("sc_du", dz, wts["sc_w_in"], T, "cols", out_dtype=BF16)
        else:
            dk = D // RET_HEADS
            half = dk // 2
            dout = _mm_rows_t("ret_dout", dy, wts["ret_w_out"], T, BF16)
            g["ret_w_out"] = _mm_rows_grad("ret_dwout", sv["out"], dy, T)

            def _postb(ov, gv, dv_):
                _, vjp = jax.vjp(_gate_norm, ov, gv.astype(F32))
                return vjp(dv_.astype(F32))

            do, dzg = _rowwise("ret_post_bwd", _postb, [(sv["o"], 2 * D, 0), (sv["z"], 2 * D, 2), (dout, 2 * D, 0)],
                               [("row", 2 * D, BF16), ("row", 2 * D, BF16)], T, trw)
            dqr, dkr, dvb = _ret_bwd("ret_bwd", sv["qr"], sv["kr"], sv["vb"], do, sv["st"], tabs, T)

            def _preb(dq, dk_, dvv, dgv, cv, sn):
                return jnp.concatenate([_rot(dq, cv, sn, -1.0).astype(BF16), (_rot(dk_, cv, sn, -1.0) * (dk ** -0.5)).astype(BF16),
                                        dvv, dgv], axis=-1)

            dz = _rowwise("ret_pre_bwd", _preb, [(dqr, D, 0), (dkr, D, 0), (dvb, 2 * D, 0), (dzg, 2 * D, 0), (cos, half, 0), (sin, half, 0)],
                          [("row", 6 * D, BF16)], T, trw)[0]
            g["ret_w_in"] = _mm_cols_grad("ret_dwin", u, dz, wts["ret_w_in"].shape, (), T)
            du = _mm_whole_k("ret_du", dz, wts["ret_w_in"], T, "cols", out_dtype=BF16)
        du_next = du
        if l > 0:
            tok = put_g(l, "mix", g, du, None)

    def _fb(dhv, duv, xv, gv):
        _, vjp = jax.vjp(_rms, xv, gv)
        dx, dg = vjp(duv.astype(F32))
        return dhv + dx, dg

    grad_x, gn[0] = _rowwise("bwd_first", _fb, [(dh, D, 0), (du_next, D, 0), (x, D, 0), (row(0), None, None)],
                             [("row", D, F32), ("acc", (1, D))], T, tr)
    put_g(0, "mix", g, grad_x, jnp.concatenate(gn, axis=0))
    return loss_row, grad_x


def _bound_bwd(name, dh, du, h_new, y, g_post, g_next, T, D, tr, deps=()):
    def fn(dhv, duv, hn, yv, gp, gnx):
        _, vjp1 = jax.vjp(_rms, hn, gnx)
        d1, dgn = vjp1(duv.astype(F32))
        dht = dhv + d1
        _, vjp2 = jax.vjp(_rms, yv.astype(F32), gp)
        dy, dgp = vjp2(dht)
        return dht, dy, dgp, dgn, _colsum(dy)

    return _rowwise(name, fn, [(dh, D, 0), (du, D, 0), (h_new, D, 0), (y, D, 0), (g_post, None, None), (g_next, None, None)],
                    [("row", D, F32), ("row", D, BF16), ("acc", (1, D)), ("acc", (1, D)), ("acc", (1, D))], T, tr, deps)


def _adamw(name, parts, w, m, v, own=None, layer=None, bufs=None):
    P, R, C = parts.shape
    tr = _tile(R, 256)
    n_s = 1 if own is None else 2

    def body(*refs):
        p_ref = refs[0]
        w_ref, m_ref, v_ref = refs[n_s:n_s + 3]
        g_ref, d_ref, m2_ref, v2_ref = refs[-4:]

        def slot(i):
            if own is None:
                return p_ref[i].astype(F32)
            here = 2 * lax.axis_index("x") + lax.axis_index("y") == i
            return jnp.where(here, refs[1][i], p_ref[i]).astype(F32)

        g = slot(0)
        for i in range(1, P):
            g = g + slot(i)
        wv = w_ref[...]
        m2 = ADAM_B1 * m_ref[...] + (1.0 - ADAM_B1) * g
        v2 = ADAM_B2 * v_ref[...] + (1.0 - ADAM_B2) * (g * g)
        m_hat = m2 / (1.0 - ADAM_B1 ** ADAM_STEP)
        v_hat = v2 / (1.0 - ADAM_B2 ** ADAM_STEP)
        g_ref[...] = g
        d_ref[...] = -ADAM_LR * (m_hat / (jnp.sqrt(v_hat) + ADAM_EPS) + ADAM_WD * wv)
        m2_ref[...] = m2
        v2_ref[...] = v2

    pblk = pl.BlockSpec((P, tr, C), lambda i: (0, i, 0))
    stacks = [parts] if own is None else [parts, own]
    if layer is None:
        blk = pl.BlockSpec((tr, C), lambda i: (i, 0))
        return _pcall(body, name=name, grid=(R // tr,),
                      in_specs=[pblk] * n_s + [blk, blk, blk], out_specs=[blk] * 4,
                      out_shape=[jax.ShapeDtypeStruct((R, C), F32)] * 4, compiler_params=_params(("parallel",)))(*stacks, w, m, v)
    blk = pl.BlockSpec((None, tr, C), lambda i: (layer, i, 0))
    return _pcall(body, name=name, grid=(R // tr,),
                  in_specs=[pblk] * n_s + [blk, blk, blk] + [pl.BlockSpec(memory_space=pl.ANY)] * 4, out_specs=[blk] * 4,
                  out_shape=[jax.ShapeDtypeStruct(w.shape, F32)] * 4, input_output_aliases={n_s + 3 + i: i for i in range(4)},
                  compiler_params=_params(("parallel",)))(*stacks, w, m, v, *bufs)


def _pair_add(name, a, b):
    Q, _, R, C = a.shape
    tr = _tile(R, 1024)

    def body(c_ref, a_ref, b_ref, o_ref):
        o_ref[...] = (a_ref[...].astype(F32) + b_ref[...].astype(F32)).astype(BF16)

    core = lax.axis_index("c").astype(jnp.int32).reshape(1)
    blk = pl.BlockSpec((None, tr, C), lambda q, i, c: (q, i, 0))
    spec = pltpu.PrefetchScalarGridSpec(num_scalar_prefetch=1, grid=(Q, R // tr),
                                        in_specs=[pl.BlockSpec((None, None, tr, C), lambda q, i, c: (q, c[0], i, 0)), blk],
                                        out_specs=blk)
    return _pcall(body, name=name, grid_spec=spec, out_shape=jax.ShapeDtypeStruct((Q, R, C), BF16),
                  compiler_params=_params(("parallel", "parallel")))(core, a, b)


ANY = pl.BlockSpec(memory_space=pl.ANY)


def _place():
    x, y, c = lax.axis_index("x"), lax.axis_index("y"), lax.axis_index("c")
    chips = [(1 - x, y), (x, 1 - y), (1 - x, 1 - y)]
    return x, y, c, chips


def _gather_all(locs):
    n = len(locs)

    def body(*refs):
        ins, outs = refs[:n], refs[n:2 * n]
        send, recv, lsem = refs[2 * n:]
        x, y, c, chips = _place()
        me = 4 * x + 2 * y + c
        sib = (x, y, 1 - c)

        def cp(a, k, slot, to, src=None):
            return pltpu.make_async_remote_copy(src_ref=outs[a].at[slot] if src is None else src, dst_ref=outs[a].at[slot],
                                                send_sem=send.at[a, k], recv_sem=recv.at[a, k], device_id=to, device_id_type=MESH)

        mine = [pltpu.make_async_copy(ins[a], outs[a].at[me], lsem.at[a]) for a in range(n)]
        for d in mine:
            d.start()
        first = []
        for a in range(n):
            first.append(cp(a, 0, me, sib, src=ins[a]))
            for j, chip in enumerate(chips):
                first.append(cp(a, 1 + j, me, (chip[0], chip[1], c), src=ins[a]))
        for d in first:
            d.start()
        passed = []
        for j, chip in enumerate(chips):
            slot = 4 * chip[0] + 2 * chip[1] + c
            for a in range(n):
                cp(a, 1 + j, slot, sib).wait_recv()
                p = cp(a, 4 + j, slot, sib)
                p.start()
                passed.append(p)
        for a in range(n):
            cp(a, 0, 4 * x + 2 * y + (1 - c), sib).wait_recv()
        for j, chip in enumerate(chips):
            for a in range(n):
                cp(a, 4 + j, 4 * chip[0] + 2 * chip[1] + (1 - c), sib).wait_recv()
        for d in first + passed:
            d.wait_send()
        for d in mine:
            d.wait()

    return _pcall(body, name="gather_all", in_specs=[ANY] * n, out_specs=[ANY] * n,
                  out_shape=[jax.ShapeDtypeStruct((N_DEV,) + a.shape, a.dtype) for a in locs],
                  scratch_shapes=[pltpu.SemaphoreType.DMA((n, 7)), pltpu.SemaphoreType.DMA((n, 7)), pltpu.SemaphoreType.DMA((n,))],
                  compiler_params=pltpu.CompilerParams(has_side_effects=True))(*locs)


HBM = pl.BlockSpec(memory_space=pltpu.HBM)
SEM = pl.BlockSpec(memory_space=pltpu.SEMAPHORE)
DATAFLOW = pltpu.SideEffectType.DATAFLOW_SIDE_EFFECTING


def _peers():
    x, y, c = lax.axis_index("x"), lax.axis_index("y"), lax.axis_index("c")
    out = []
    for r in range(1, N_DEV):
        px = 1 - x if (r >> 2) & 1 else x
        py = 1 - y if (r >> 1) & 1 else y
        pc = 1 - c if r & 1 else c
        out.append(((px, py, pc), 4 * px + 2 * py + pc))
    return 4 * x + 2 * y + c, out


def _hbm(a):
    return pltpu.with_memory_space_constraint(a, pltpu.HBM)


def _gather_start(name, locs, lands, deps):
    n = len(locs)

    def body(*refs):
        ins, lnd = refs[:n], refs[n:2 * n]
        send, recv = refs[2 * n + len(deps)], refs[2 * n + len(deps) + 1]
        token = refs[-1]
        x, y, c, chips = _place()
        me = 4 * x + 2 * y + c
        for k, to in enumerate([(x, y, 1 - c)] + [(chip[0], chip[1], c) for chip in chips]):
            for a in range(n):
                pltpu.make_async_remote_copy(src_ref=ins[a], dst_ref=lnd[a].at[me], send_sem=send.at[a * 4 + k],
                                             recv_sem=recv.at[a * 4 + k], device_id=to, device_id_type=MESH).start()
        token[...] = jnp.zeros_like(token)

    out_shape = ((pltpu.SemaphoreType.DMA((n * 4,)), pltpu.SemaphoreType.DMA((n * 4,)))
                 + tuple(pltpu.HBM(a.shape, a.dtype) for a in locs) + tuple(pltpu.HBM(a.shape, a.dtype) for a in lands)
                 + (jax.ShapeDtypeStruct((8, LANES), F32),))
    res = _pcall(body, name=name, out_shape=out_shape, in_specs=[HBM] * (2 * n) + [ANY] * len(deps),
                 out_specs=(SEM, SEM) + (HBM,) * (2 * n) + (pl.BlockSpec(memory_space=pltpu.VMEM),),
                 input_output_aliases={a: 2 + a for a in range(2 * n)},
                 compiler_params=pltpu.CompilerParams(has_side_effects=DATAFLOW))(
        *[_hbm(a) for a in locs], *[_hbm(a) for a in lands], *deps)
    return res[0], res[1], res[2:2 + n], res[2 + n:2 + 2 * n], res[-1]


def _gather_pass(name, send, recv, shards, lands, after):
    n = len(shards)

    def body(*refs):
        ins, lnd = refs[:n], refs[n:2 * n]
        recv = refs[2 * n + 1]
        fsend, frecv = refs[2 * n + 3], refs[2 * n + 4]
        token = refs[-1]
        x, y, c, chips = _place()
        for j, chip in enumerate(chips):
            slot = 4 * chip[0] + 2 * chip[1] + c
            for a in range(n):
                pltpu.make_async_remote_copy(src_ref=ins[a], dst_ref=lnd[a].at[slot], send_sem=fsend.at[a * 3 + j],
                                             recv_sem=recv.at[a * 4 + 1 + j], device_id=(chip[0], chip[1], c),
                                             device_id_type=MESH).wait_recv()
                pltpu.make_async_remote_copy(src_ref=lnd[a].at[slot], dst_ref=lnd[a].at[slot], send_sem=fsend.at[a * 3 + j],
                                             recv_sem=frecv.at[a * 3 + j], device_id=(x, y, 1 - c), device_id_type=MESH).start()
        token[...] = jnp.zeros_like(token)

    out_shape = ((pltpu.SemaphoreType.DMA((n * 3,)), pltpu.SemaphoreType.DMA((n * 3,)))
                 + tuple(pltpu.HBM(a.shape, a.dtype) for a in shards) + tuple(pltpu.HBM(a.shape, a.dtype) for a in lands)
                 + (jax.ShapeDtypeStruct((8, LANES), F32),))
    res = _pcall(body, name=name, out_shape=out_shape, in_specs=[HBM] * (2 * n) + [SEM, SEM, ANY],
                 out_specs=(SEM, SEM) + (HBM,) * (2 * n) + (pl.BlockSpec(memory_space=pltpu.VMEM),),
                 input_output_aliases={a: 2 + a for a in range(2 * n)},
                 compiler_params=pltpu.CompilerParams(has_side_effects=DATAFLOW))(*shards, *lands, send, recv, after)
    return res[0], res[1], res[2:2 + n], res[2 + n:2 + 2 * n], res[-1]


def _gather_wait(name, send, recv, fsend, frecv, shards, lands, after):
    n = len(shards)

    def body(*refs):
        ins, lnd = refs[:n], refs[n:2 * n]
        send, recv, fsend, frecv = refs[2 * n:2 * n + 4]
        x, y, c, chips = _place()
        sib = (x, y, 1 - c)
        for a in range(n):
            own = pltpu.make_async_remote_copy(src_ref=ins[a], dst_ref=lnd[a].at[4 * x + 2 * y + (1 - c)], send_sem=send.at[a * 4],
                                               recv_sem=recv.at[a * 4], device_id=sib, device_id_type=MESH)
            own.wait_send()
            own.wait_recv()
            for j, chip in enumerate(chips):
                pltpu.make_async_remote_copy(src_ref=ins[a], dst_ref=lnd[a].at[0], send_sem=send.at[a * 4 + 1 + j],
                                             recv_sem=recv.at[a * 4 + 1 + j], device_id=(chip[0], chip[1], c),
                                             device_id_type=MESH).wait_send()
                fw = pltpu.make_async_remote_copy(src_ref=lnd[a].at[4 * chip[0] + 2 * chip[1] + c],
                                                  dst_ref=lnd[a].at[4 * chip[0] + 2 * chip[1] + (1 - c)], send_sem=fsend.at[a * 3 + j],
                                                  recv_sem=frecv.at[a * 3 + j], device_id=sib, device_id_type=MESH)
                fw.wait_send()
                fw.wait_recv()

    out_shape = tuple(pltpu.HBM(a.shape, a.dtype) for a in shards) + tuple(pltpu.HBM(a.shape, a.dtype) for a in lands)
    res = _pcall(body, name=name, out_shape=out_shape, in_specs=[HBM] * (2 * n) + [SEM, SEM, SEM, SEM, ANY], out_specs=(HBM,) * (2 * n),
                 input_output_aliases={a: a for a in range(2 * n)},
                 compiler_params=pltpu.CompilerParams(has_side_effects=DATAFLOW))(*shards, *lands, send, recv, fsend, frecv, after)
    return res[n:]


def _chip_start(name, cs, lands):
    n = len(cs)

    def body(*refs):
        ins, lnd = refs[:n], refs[n:2 * n]
        send, recv = refs[2 * n], refs[2 * n + 1]
        token = refs[-1]
        x, y, c, chips = _place()
        qme = 2 * x + y
        for j, chip in enumerate(chips):
            for a in range(n):
                pltpu.make_async_remote_copy(src_ref=ins[a].at[2 * chip[0] + chip[1]], dst_ref=lnd[a].at[qme], send_sem=send.at[a * 3 + j],
                                             recv_sem=recv.at[a * 3 + j], device_id=(chip[0], chip[1], c), device_id_type=MESH).start()
        token[...] = jnp.zeros_like(token)

    out_shape = ((pltpu.SemaphoreType.DMA((n * 3,)), pltpu.SemaphoreType.DMA((n * 3,)))
                 + tuple(pltpu.HBM(a.shape, a.dtype) for a in cs) + tuple(pltpu.HBM(a.shape, a.dtype) for a in lands)
                 + (jax.ShapeDtypeStruct((8, LANES), F32),))
    res = _pcall(body, name=name, out_shape=out_shape, in_specs=[HBM] * (2 * n),
                 out_specs=(SEM, SEM) + (HBM,) * (2 * n) + (pl.BlockSpec(memory_space=pltpu.VMEM),),
                 input_output_aliases={a: 2 + a for a in range(2 * n)},
                 compiler_params=pltpu.CompilerParams(has_side_effects=DATAFLOW))(*[_hbm(a) for a in cs], *[_hbm(a) for a in lands])
    return res[0], res[1], res[2:2 + n], res[2 + n:2 + 2 * n], res[-1]


def _chip_wait(name, send, recv, cs, lands, after):
    n = len(cs)

    def body(*refs):
        ins, lnd = refs[:n], refs[n:2 * n]
        send, recv = refs[2 * n], refs[2 * n + 1]
        x, y, c, chips = _place()
        for j, chip in enumerate(chips):
            q = 2 * chip[0] + chip[1]
            for a in range(n):
                cp = pltpu.make_async_remote_copy(src_ref=ins[a].at[q], dst_ref=lnd[a].at[q], send_sem=send.at[a * 3 + j],
                                                  recv_sem=recv.at[a * 3 + j], device_id=(chip[0], chip[1], c), device_id_type=MESH)
                cp.wait_send()
                cp.wait_recv()

    out_shape = tuple(pltpu.HBM(a.shape, a.dtype) for a in cs) + tuple(pltpu.HBM(a.shape, a.dtype) for a in lands)
    res = _pcall(body, name=name, out_shape=out_shape, in_specs=[HBM] * (2 * n) + [SEM, SEM, ANY], out_specs=(HBM,) * (2 * n),
                 input_output_aliases={a: a for a in range(2 * n)},
                 compiler_params=pltpu.CompilerParams(has_side_effects=DATAFLOW))(*cs, *lands, send, recv, after)
    return res[:n], res[n:]


def _pair_exchange(name, gs):
    n = len(gs)

    def body(*refs):
        ins, got = refs[:n], refs[n:2 * n]
        send, recv = refs[2 * n:]
        x, y, c, _ = _place()
        sib = (x, y, 1 - c)
        rem = [pltpu.make_async_remote_copy(src_ref=ins[a].at[q, 1 - c], dst_ref=got[a].at[q], send_sem=send.at[a, q],
                                            recv_sem=recv.at[a, q], device_id=sib, device_id_type=MESH)
               for a in range(n) for q in range(N_CHIP)]
        for d in rem:
            d.start()
        for d in rem:
            d.wait()

    half = [jax.ShapeDtypeStruct((N_CHIP,) + g.shape[2:], g.dtype) for g in gs]
    return _pcall(body, name=name, in_specs=[ANY] * n, out_specs=[ANY] * n, out_shape=half,
                  scratch_shapes=[pltpu.SemaphoreType.DMA((n, N_CHIP)), pltpu.SemaphoreType.DMA((n, N_CHIP))],
                  compiler_params=pltpu.CompilerParams(has_side_effects=True))(*gs)


def _chip_exchange(cs, sm_sh, sm_rep):
    n = len(cs)

    def body(*refs):
        ins, sh, rep = refs[:n], refs[n], refs[n + 1]
        outs, osh, orep = refs[n + 2:2 * n + 2], refs[2 * n + 2], refs[2 * n + 3]
        send, recv, lsem, ssend, srecv, slsem = refs[2 * n + 4:]
        x, y, c, chips = _place()
        qme = 2 * x + y
        me = 4 * x + 2 * y + c
        loc = [pltpu.make_async_copy(ins[a].at[qme], outs[a].at[qme], lsem.at[a]) for a in range(n)]
        loc.append(pltpu.make_async_copy(sh.at[me], osh.at[me], slsem.at[0]))
        loc.append(pltpu.make_async_copy(rep, orep.at[me], slsem.at[1]))
        rem, waits = [], []
        for j, chip in enumerate(chips):
            q = 2 * chip[0] + chip[1]
            to = (chip[0], chip[1], c)
            for a in range(n):
                rem.append(pltpu.make_async_remote_copy(src_ref=ins[a].at[q], dst_ref=outs[a].at[qme], send_sem=send.at[a, j],
                                                        recv_sem=recv.at[a, j], device_id=to, device_id_type=MESH))
                waits.append(pltpu.make_async_remote_copy(src_ref=ins[a].at[q], dst_ref=outs[a].at[q], send_sem=send.at[a, j],
                                                          recv_sem=recv.at[a, j], device_id=to, device_id_type=MESH))
        for r in range(1, N_DEV):
            fx, fy, fc = (r >> 2) & 1, (r >> 1) & 1, r & 1
            px = 1 - x if fx else x
            py = 1 - y if fy else y
            pc = 1 - c if fc else c
            t = 4 * px + 2 * py + pc
            to = (px, py, pc)
            rem.append(pltpu.make_async_remote_copy(src_ref=sh.at[t], dst_ref=osh.at[me], send_sem=ssend.at[0, r - 1],
                                                    recv_sem=srecv.at[0, r - 1], device_id=to, device_id_type=MESH))
            waits.append(pltpu.make_async_remote_copy(src_ref=sh.at[t], dst_ref=osh.at[t], send_sem=ssend.at[0, r - 1],
                                                      recv_sem=srecv.at[0, r - 1], device_id=to, device_id_type=MESH))
            rem.append(pltpu.make_async_remote_copy(src_ref=rep, dst_ref=orep.at[me], send_sem=ssend.at[1, r - 1],
                                                    recv_sem=srecv.at[1, r - 1], device_id=to, device_id_type=MESH))
            waits.append(pltpu.make_async_remote_copy(src_ref=rep, dst_ref=orep.at[t], send_sem=ssend.at[1, r - 1],
                                                      recv_sem=srecv.at[1, r - 1], device_id=to, device_id_type=MESH))
        for d in loc + rem:
            d.start()
        for d in waits:
            d.wait_recv()
        for d in rem:
            d.wait_send()
        for d in loc:
            d.wait()

    out_shape = [jax.ShapeDtypeStruct(a.shape, a.dtype) for a in cs]
    out_shape += [jax.ShapeDtypeStruct(sm_sh.shape, F32), jax.ShapeDtypeStruct((N_DEV,) + sm_rep.shape, F32)]
    res = _pcall(body, name="chip_exchange", in_specs=[ANY] * (n + 2), out_specs=[ANY] * (n + 2), out_shape=out_shape,
                 scratch_shapes=[pltpu.SemaphoreType.DMA((n, 3)), pltpu.SemaphoreType.DMA((n, 3)), pltpu.SemaphoreType.DMA((n,)),
                                 pltpu.SemaphoreType.DMA((2, 7)), pltpu.SemaphoreType.DMA((2, 7)), pltpu.SemaphoreType.DMA((2,))],
                 compiler_params=pltpu.CompilerParams(has_side_effects=True))(*cs, sm_sh, sm_rep)
    return res[:n], res[n], res[n + 1]


BIG = ("mlp_up", "mlp_down", "conv_w_in", "conv_w_out", "pool_w", "sc_w_in", "sc_w_out", "ret_w_in", "ret_w_out")
SMALL_SPLIT = ("norm_g", "conv_dw", "sc_dw")
SMALL_REP = ("conv_b_in", "conv_dw_b", "conv_ln_g", "conv_ln_b", "conv_b_out", "pool_scale")
WEIGHTS = ("norm_g", "mlp_up", "mlp_down", "conv_w_in", "conv_b_in", "conv_dw", "conv_dw_b", "conv_ln_g", "conv_ln_b",
           "conv_w_out", "conv_b_out", "pool_w", "pool_scale", "sc_w_in", "sc_dw", "sc_w_out", "ret_w_in", "ret_w_out")


def _pad_rows(a, rows):
    return jnp.pad(a, ((0, rows - a.shape[0]), (0, 0)))


def _split_rows(names, shards):
    out = []
    for nme in names:
        r = int(np.prod(shards[nme].shape[:-1]))
        out.append((nme, r, -(-r // 8) * 8))
    return out


def _pack_split(names, arrs):
    parts = []
    for nme, r, rp in _split_rows(names, arrs):
        parts.append(_pad_rows(arrs[nme].reshape(r, arrs[nme].shape[-1]), rp))
    return jnp.concatenate(parts, axis=0)


def _pack_rep(arrs, D):
    parts = [arrs[nme].reshape(-1, D) for nme in SMALL_REP]
    rows = sum(p.shape[0] for p in parts)
    return _pad_rows(jnp.concatenate(parts, axis=0), -(-rows // 8) * 8)


def kernel(x, positions, norm_g, mlp_up, mlp_down, conv_w_in, conv_b_in, conv_dw, conv_dw_b, conv_ln_g, conv_ln_b, conv_w_out, conv_b_out, pool_w, pool_scale, sc_w_in, sc_dw, sc_w_out, ret_w_in, ret_w_out, loss_target, m_norm_g, m_mlp_up, m_mlp_down, m_conv_w_in, m_conv_b_in, m_conv_dw, m_conv_dw_b, m_conv_ln_g, m_conv_ln_b, m_conv_w_out, m_conv_b_out, m_pool_w, m_pool_scale, m_sc_w_in, m_sc_dw, m_sc_w_out, m_ret_w_in, m_ret_w_out, v_norm_g, v_mlp_up, v_mlp_down, v_conv_w_in, v_conv_b_in, v_conv_dw, v_conv_dw_b, v_conv_ln_g, v_conv_ln_b, v_conv_w_out, v_conv_b_out, v_pool_w, v_pool_scale, v_sc_w_in, v_sc_dw, v_sc_w_out, v_ret_w_in, v_ret_w_out):
    w = dict(norm_g=norm_g, mlp_up=mlp_up, mlp_down=mlp_down, conv_w_in=conv_w_in, conv_b_in=conv_b_in, conv_dw=conv_dw,
             conv_dw_b=conv_dw_b, conv_ln_g=conv_ln_g, conv_ln_b=conv_ln_b, conv_w_out=conv_w_out, conv_b_out=conv_b_out,
             pool_w=pool_w, pool_scale=pool_scale, sc_w_in=sc_w_in, sc_dw=sc_dw, sc_w_out=sc_w_out, ret_w_in=ret_w_in, ret_w_out=ret_w_out)
    m = dict(norm_g=m_norm_g, mlp_up=m_mlp_up, mlp_down=m_mlp_down, conv_w_in=m_conv_w_in, conv_b_in=m_conv_b_in, conv_dw=m_conv_dw,
             conv_dw_b=m_conv_dw_b, conv_ln_g=m_conv_ln_g, conv_ln_b=m_conv_ln_b, conv_w_out=m_conv_w_out, conv_b_out=m_conv_b_out,
             pool_w=m_pool_w, pool_scale=m_pool_scale, sc_w_in=m_sc_w_in, sc_dw=m_sc_dw, sc_w_out=m_sc_w_out, ret_w_in=m_ret_w_in,
             ret_w_out=m_ret_w_out)
    v = dict(norm_g=v_norm_g, mlp_up=v_mlp_up, mlp_down=v_mlp_down, conv_w_in=v_conv_w_in, conv_b_in=v_conv_b_in, conv_dw=v_conv_dw,
             conv_dw_b=v_conv_dw_b, conv_ln_g=v_conv_ln_g, conv_ln_b=v_conv_ln_b, conv_w_out=v_conv_w_out, conv_b_out=v_conv_b_out,
             pool_w=v_pool_w, pool_scale=v_pool_scale, sc_w_in=v_sc_w_in, sc_dw=v_sc_dw, sc_w_out=v_sc_w_out, ret_w_in=v_ret_w_in,
             ret_w_out=v_ret_w_out)
    T, D = x.shape[1], x.shape[2]
    L = D // N_DEV
    G = len(POOL_WINDOWS)
    cg = D // G

    me = 4 * lax.axis_index("x") + 2 * lax.axis_index("y") + lax.axis_index("c")
    MIX = (("conv_w_in", "conv_w_out"), ("pool_w",), ("sc_w_in", "sc_w_out"), ("ret_w_in", "ret_w_out"))

    def shards(l, part):
        out = {k: w[k].astype(BF16) for k in MIX[l]} if part != "mlp" else {}
        if part != "mix":
            out["mlp_up"], out["mlp_down"] = w["mlp_up"][l].astype(BF16), w["mlp_down"][l].astype(BF16)
        return out

    sh0 = shards(0, "mix")
    gath0 = _gather_all(list(sh0.values()) + [_pack_split(SMALL_SPLIT, w)])
    small = gath0[-1]
    common = {k: w[k].reshape(1, -1) for k in SMALL_REP}
    off = 0
    for nme, r, rp in _split_rows(SMALL_SPLIT, w):
        common[nme] = small[:, off:off + r, :].transpose(1, 0, 2).reshape(r, D)
        off += rp
    flying = {}
    dep = small
    for l, part in [(0, "mlp")] + [(l, "both") for l in range(1, DEPTH)]:
        shl = shards(l, part)
        lands = [lax.dynamic_update_slice(lax.empty((N_DEV,) + a.shape, a.dtype), a[None], (me,) + (0,) * a.ndim) for a in shl.values()]
        send, recv, thru, lands, dep = _gather_start("gather_start_%d" % l, list(shl.values()), lands, [dep])
        flying[l] = dict(names=list(shl.keys()), send=send, recv=recv, thru=thru, lands=lands)

    def pass_on(l, after):
        f = flying[l]
        f["fsend"], f["frecv"], f["thru"], f["lands"], tok = _gather_pass("gather_pass_%d" % l, f["send"], f["recv"], f["thru"],
                                                                          f["lands"], after)
        return tok

    def arrive(l, after):
        f = flying[l]
        got = _gather_wait("gather_wait_%d" % l, f["send"], f["recv"], f["fsend"], f["frecv"], f["thru"], f["lands"], after)
        return dict(zip(f["names"], got))

    def get_w(l, part, after):
        if part == "mlp" and l > 0:
            return {"_deps": [pass_on(l + 1, after)]} if l + 1 < DEPTH else {}
        if l == 0 and part == "mix":
            gw = dict(zip(sh0.keys(), gath0[:-1]))
        elif l == 0:
            gw = arrive(0, pass_on(0, after))
            gw["_deps"] = [pass_on(1, gw["mlp_up"])]
        else:
            gw = arrive(l, after)
        out = dict(common)
        for k, a in gw.items():
            if k.endswith("_w_out"):
                out[k] = a.reshape(-1, D)
            elif k == "pool_w":
                out[k] = a.transpose(1, 0, 2, 3).reshape(G, cg, cg)
            else:
                out[k] = a
        return out

    res = {}
    mlp_res = {k: [lax.empty(w[k].shape, F32) for _ in range(4)] for k in ("mlp_up", "mlp_down")}
    pending = []
    small_g = {}

    def shard2d(src, k, l):
        a = src[k][l] if k in mlp_res else src[k]
        return a.reshape(-1, a.shape[-1])

    def adam_layer(l, names, parts, own):
        for i, k in enumerate(names):
            mine = None if own is None else own[i]
            if k in mlp_res:
                mlp_res[k] = _adamw("adamw_%s_%d" % (k, l), parts[i], w[k], m[k], v[k], mine, layer=l, bufs=mlp_res[k])
            else:
                out = _adamw("adamw_%s_%d" % (k, l), parts[i], shard2d(w, k, l), shard2d(m, k, l), shard2d(v, k, l), mine)
                res[k] = [o.reshape(w[k].shape) for o in out]

    held = {}

    def put_g(l, part, g, after, norm_grads):
        small_g.update({k: g[k] for k in g if k in SMALL_SPLIT or k in SMALL_REP})
        if part == "mlp" and l > 0:
            held.update(g)
            return None
        if pending:
            tag, lp, names, send, recv, sums, lands = pending.pop()
            sums, parts = _chip_wait("chip_wait_" + tag, send, recv, sums, lands, after)
            adam_layer(lp, names, parts, sums)
        last = l == 0 and part == "mix"
        if l > 0:
            g = dict(g, **held)
            held.clear()
            names = list(MIX[l]) + ["mlp_up", "mlp_down"]
        else:
            names = list(MIX[0]) if last else ["mlp_up", "mlp_down"]
        tag = "%d%s" % (l, "" if l > 0 else part)
        stacks = []
        for k in names:
            a = g[k]
            if k == "pool_w":
                a = a.reshape(G, N_DEV, cg // N_DEV, cg).transpose(1, 0, 2, 3).astype(BF16)
            stacks.append(a.reshape(N_CHIP, 2, -1, w[k].shape[-1]))
        got = _pair_exchange("pair_exchange_" + tag, stacks)
        sums = [_pair_add("pair_add", a, b) for a, b in zip(stacks, got)]
        if not last:
            send, recv, sums, lands, tok = _chip_start("chip_start_" + tag, sums, [lax.empty(a.shape, a.dtype) for a in sums])
            pending.append((tag, l, names, send, recv, sums, lands))
            return tok
        small_g["norm_g"] = norm_grads
        gsm = []
        for nme, r, rp in _split_rows(SMALL_SPLIT, w):
            gsm.append(jnp.pad(small_g[nme].reshape(r, N_DEV, L), ((0, rp - r), (0, 0), (0, 0))))
        sm_sh = jnp.concatenate(gsm, axis=0).transpose(1, 0, 2)
        parts, r_sh, r_rep = _chip_exchange(sums, sm_sh, _pack_rep(small_g, D))
        adam_layer(0, names, parts, None)
        out = _adamw("adamw_split", r_sh, _pack_split(SMALL_SPLIT, w), _pack_split(SMALL_SPLIT, m), _pack_split(SMALL_SPLIT, v))
        off = 0
        for nme, r, rp in _split_rows(SMALL_SPLIT, w):
            res[nme] = [o[off:off + r].reshape(w[nme].shape) for o in out]
            off += rp
        out = _adamw("adamw_rep", r_rep, _pack_rep(w, D), _pack_rep(m, D), _pack_rep(v, D))
        off = 0
        for nme in SMALL_REP:
            r = w[nme].shape[0] // D
            res[nme] = [o[off:off + r].reshape(w[nme].shape) for o in out]
            off += r
        return None

    pos = positions.reshape(T, 1).astype(F32)
    loss_row, grad_x = _local_step(x.reshape(T, D), pos, loss_target.reshape(T, D), common["norm_g"], get_w, put_g, [dep], T, D)
    loss = lax.psum(0.5 * jnp.sum(loss_row) / D, AXES)
    res.update(mlp_res)
    outs = [loss, grad_x.reshape(x.shape)]
    for i in range(4):
        outs += [res[k][i] for k in WEIGHTS]
    return tuple(outs)
```

```python
import functools
import math

import numpy as np
import jax
import jax.numpy as jnp
from jax import lax
from jax.experimental import pallas as pl
from jax.experimental.pallas import tpu as pltpu

F32, BF16 = jnp.float32, jnp.bfloat16
MESH = pl.DeviceIdType.MESH
AXES = ("x", "y", "c")
N_DEV = 8
N_CHIP = 4
EPS = 1e-6
DEPTH = 4
CONV_WIDTH = 31
POOL_WINDOWS = (2, 4, 8, 16)
SHORT_CONV_WIDTH = 3
RET_HEADS = 4
RET_CHUNK = 128
ROPE_BASE = 10000.0
ADAM_LR, ADAM_B1, ADAM_B2, ADAM_EPS, ADAM_WD, ADAM_STEP = 0.001, 0.9, 0.999, 1e-08, 0.01, 10
LANES = 128
SEQ_PAD = 32
VMEM_LIMIT = 56 * 1024 * 1024

NN = (((1,), (0,)), ((), ()))
NT = (((1,), (1,)), ((), ()))
TN = (((0,), (0,)), ((), ()))


def _pcall(body, **kw):
    return pl.pallas_call(body, **kw)


def _params(sem):
    return pltpu.CompilerParams(dimension_semantics=sem, vmem_limit_bytes=VMEM_LIMIT)


def _tile(n, pref):
    t = min(n, pref)
    while n % t:
        t //= 2
    return t


def _rowwise(name, fn, ins, outs, rows, tr, deps=()):
    n_in = len(ins)
    n_dep = len(deps)
    in_specs = []
    for arr, nc, cb in ins:
        if nc is None:
            in_specs.append(pl.BlockSpec(arr.shape, lambda i, nd=arr.ndim: (0,) * nd))
        else:
            in_specs.append(pl.BlockSpec((tr, nc), lambda i, cb=cb: (i, cb)))
    out_specs, out_shapes = [], []
    for o in outs:
        if o[0] == "row":
            out_shapes.append(jax.ShapeDtypeStruct((rows, o[1]), o[2]))
            out_specs.append(pl.BlockSpec((tr, o[1]), lambda i: (i, 0)))
        else:
            out_shapes.append(jax.ShapeDtypeStruct(o[1], F32))
            out_specs.append(pl.BlockSpec(o[1], lambda i: (0, 0)))

    in_specs = in_specs + [pl.BlockSpec(memory_space=pl.ANY)] * n_dep

    def body(*refs):
        res = fn(*[r[...] for r in refs[:n_in]])
        if not isinstance(res, (tuple, list)):
            res = (res,)
        i = pl.program_id(0)
        for o, ref, r in zip(outs, refs[n_in + n_dep:], res):
            if o[0] == "row":
                ref[...] = r.astype(o[2])
            else:
                @pl.when(i == 0)
                def _(ref=ref, r=r):
                    ref[...] = r

                @pl.when(i > 0)
                def _(ref=ref, r=r):
                    ref[...] += r

    return _pcall(body, name=name, grid=(rows // tr,), in_specs=in_specs, out_specs=out_specs,
                  out_shape=out_shapes, compiler_params=_params(("arbitrary",)))(*[a for a, _, _ in ins], *deps)


def _rms(x, g):
    return x * lax.rsqrt(jnp.mean(x * x, axis=-1, keepdims=True) + EPS) * g


def _colsum(x):
    return jnp.sum(x, axis=0, keepdims=True)


def _square(r):
    rf = r.astype(F32)
    return (rf * rf).astype(r.dtype)


def _ln_silu(c, g, b):
    mu = jnp.mean(c, axis=-1, keepdims=True)
    xc = c - mu
    y = xc * lax.rsqrt(jnp.mean(xc * xc, axis=-1, keepdims=True) + EPS) * g + b
    return y * jax.nn.sigmoid(y)


def _glu(za, zg):
    return za * jax.nn.sigmoid(zg)


def _gate_norm(o, g):
    dv = o.shape[-1] // RET_HEADS
    parts = []
    for h in range(RET_HEADS):
        oh = o[:, h * dv:(h + 1) * dv]
        parts.append(oh * lax.rsqrt(jnp.mean(oh * oh, axis=-1, keepdims=True) + EPS))
    return (g * jax.nn.sigmoid(g)) * jnp.concatenate(parts, axis=-1)


def _rot(x, cos, sin, sign):
    dk = x.shape[-1] // RET_HEADS
    half = dk // 2
    parts = []
    for h in range(RET_HEADS):
        x1 = x[:, h * dk:h * dk + half]
        x2 = x[:, h * dk + half:(h + 1) * dk]
        parts.append(x1 * cos - sign * (x2 * sin))
        parts.append(sign * (x1 * sin) + x2 * cos)
    return jnp.concatenate(parts, axis=-1)


def _mm(name, mode, grid, ins, outs, epi=None, aliases=None, pre=None, deps=()):
    dn = {"nn": NN, "nt": NT, "tn": TN}[mode]
    gk = grid[2]
    n_in, n_out = len(ins), len(outs)
    n_dep = len(deps)

    def body(*refs):
        out_refs = refs[n_in + n_dep:n_in + n_dep + n_out]
        a = refs[0][...] if pre is None else pre(refs[0][...])
        part = lax.dot_general(a, refs[1][...], dn, preferred_element_type=F32)

        def finish(acc):
            res = epi(acc, *[r[...] for r in refs[2:n_in]]) if epi is not None else (acc,)
            for o, r in zip(out_refs, res):
                o[...] = r.astype(o.dtype)

        if gk == 1:
            finish(part)
        else:
            acc_ref = refs[-1]
            k = pl.program_id(2)

            @pl.when(k == 0)
            def _():
                acc_ref[...] = part

            @pl.when(k > 0)
            def _():
                acc_ref[...] += part

            @pl.when(k == gk - 1)
            def _():
                finish(acc_ref[...])

    scratch = []
    if gk > 1:
        blk = [d for d in outs[0][2].block_shape if d is not None]
        scratch = [pltpu.VMEM(tuple(blk), F32)]
    kw = {}
    if aliases:
        kw["input_output_aliases"] = aliases
    return _pcall(body, name=name, grid=grid, in_specs=[s for _, s in ins] + [pl.BlockSpec(memory_space=pl.ANY)] * n_dep,
                  out_specs=[s for _, _, s in outs],
                  out_shape=[jax.ShapeDtypeStruct(sh, dt) for sh, dt, _ in outs], scratch_shapes=scratch,
                  compiler_params=_params(("parallel", "parallel", "arbitrary")), **kw)(*[a for a, _ in ins], *deps)


def _spec(block, fn):
    return pl.BlockSpec(block, fn)


def _mm_cols(name, a, wg, lead, T, epi=None, extras=(), n_out=1, out_dtype=F32, deps=()):
    K, nb = wg.shape[-2], wg.shape[-1]
    tm = _tile(T, 2048)
    wblock = (None,) * (1 + len(lead)) + (K, nb)
    ins = [(a, _spec((tm, K), lambda i, j, k: (i, 0))), (wg, _spec(wblock, lambda i, j, k: (j,) + tuple(lead) + (0, 0)))]
    for e in extras:
        if e.shape[0] == 1:
            ins.append((e, _spec((1, nb), lambda i, j, k: (0, j))))
        else:
            ins.append((e, _spec((tm, nb), lambda i, j, k: (i, j))))
    outs = [((T, N_DEV * nb), out_dtype, _spec((tm, nb), lambda i, j, k: (i, j))) for _ in range(n_out)]
    return _mm(name, "nn", (T // tm, N_DEV, 1), ins, outs, epi, deps=deps)


def _mm_cols_grad(name, a, d, wg_shape, lead, T):
    K, nb = wg_shape[-2], wg_shape[-1]
    tk = T
    wblock = (None,) * (1 + len(lead)) + (K, nb)
    ins = [(a, _spec((tk, K), lambda i, j, k: (k, 0))), (d, _spec((tk, nb), lambda i, j, k: (k, j)))]
    outs = [(tuple(wg_shape), BF16, _spec(wblock, lambda i, j, k: (j,) + tuple(lead) + (0, 0)))]
    return _mm(name, "tn", (1, N_DEV, T // tk), ins, outs, lambda acc: (acc,))[0]


def _mm_whole_k(name, a, w3, T, split, pre=None, out_dtype=F32):
    nblk, d1, d2 = w3.shape
    kb = d1 if split == "rows" else d2
    N = d2 if split == "rows" else d1
    tm = _tile(T, 512)

    def body(a_ref, w_ref, o_ref, *scratch):
        av = a_ref[...] if pre is None else pre(a_ref[...])
        if split == "rows":
            o_ref[...] = lax.dot_general(av, w_ref[...].reshape(nblk * kb, N), NN, preferred_element_type=F32).astype(out_dtype)
        else:
            wcat, sem = scratch

            @pl.when(pl.program_id(0) == 0)
            def _():
                cps = [pltpu.make_async_copy(w_ref.at[j], wcat.at[:, pl.ds(j * kb, kb)], sem.at[j]) for j in range(nblk)]
                for cp in cps:
                    cp.start()
                for cp in cps:
                    cp.wait()

            o_ref[...] = lax.dot_general(av, wcat[...], NT, preferred_element_type=F32).astype(out_dtype)

    rows = split == "rows"
    scratch = [] if rows else [pltpu.VMEM((N, nblk * kb), w3.dtype), pltpu.SemaphoreType.DMA((nblk,))]
    return _pcall(body, name=name, grid=(T // tm,),
                  in_specs=[pl.BlockSpec((tm, nblk * kb), lambda i: (i, 0)),
                            pl.BlockSpec(w3.shape, lambda i: (0, 0, 0)) if rows else pl.BlockSpec(memory_space=pl.ANY)],
                  out_specs=pl.BlockSpec((tm, N), lambda i: (i, 0)), out_shape=jax.ShapeDtypeStruct((T, N), out_dtype),
                  scratch_shapes=scratch, compiler_params=_params(("arbitrary",)))(a, w3)


def _mm_rows(name, a, w2, T, epi=None, extras=(), out_dtype=F32):
    Kin, N = w2.shape
    tm, tn = _tile(T, 2048), _tile(N, 512)
    ins = [(a, _spec((tm, Kin), lambda i, j, k: (i, 0))), (w2, _spec((Kin, tn), lambda i, j, k: (0, j)))]
    for e in extras:
        ins.append((e, _spec((1, tn), lambda i, j, k: (0, j))))
    outs = [((T, N), out_dtype, _spec((tm, tn), lambda i, j, k: (i, j)))]
    return _mm(name, "nn", (T // tm, N // tn, 1), ins, outs, epi)[0]


def _mm_rows_t(name, d, w2, T, out_dtype=F32):
    Kin, N = w2.shape
    tm, tn = _tile(T, 2048), _tile(Kin, 512)
    ins = [(d, _spec((tm, N), lambda i, j, k: (i, 0))), (w2, _spec((tn, N), lambda i, j, k: (j, 0)))]
    outs = [((T, Kin), out_dtype, _spec((tm, tn), lambda i, j, k: (i, j)))]
    return _mm(name, "nt", (T // tm, Kin // tn, 1), ins, outs)[0]


def _mm_rows_grad(name, a, d, T):
    Kin, N = a.shape[1], d.shape[1]
    tk, tm, tn = _tile(T, 2048), _tile(Kin, 512), _tile(N, 1024)
    ins = [(a, _spec((tk, tm), lambda i, j, k: (k, i))), (d, _spec((tk, tn), lambda i, j, k: (k, j)))]
    outs = [((Kin, N), BF16, _spec((tm, tn), lambda i, j, k: (i, j)))]
    return _mm(name, "tn", (Kin // tm, N // tn, T // tk), ins, outs, lambda acc: (acc,))[0]


def _dwconv_fwd(name, x, w, bias, T):
    W, C = w.shape
    cb = min(C, LANES)
    tc = _tile(T, 256)

    def body(x_ref, w_ref, b_ref, o_ref, xp_ref):
        xp_ref[pl.ds(0, SEQ_PAD), :] = jnp.zeros((SEQ_PAD, cb), F32)

        def fill(c, _):
            base = pl.multiple_of(c * tc, tc)
            xp_ref[pl.ds(base + SEQ_PAD, tc), :] = x_ref[pl.ds(base, tc), :]
            return 0

        lax.fori_loop(0, T // tc, fill, 0)

        def chunk(c, _):
            base = pl.multiple_of(c * tc, tc)
            acc = jnp.zeros((tc, cb), F32) + b_ref[...]
            for k in range(W):
                acc = acc + w_ref[pl.ds(k, 1), :] * xp_ref[pl.ds(base + SEQ_PAD - (W - 1) + k, tc), :]
            o_ref[pl.ds(base, tc), :] = acc
            return 0

        lax.fori_loop(0, T // tc, chunk, 0)

    return _pcall(body, name=name, grid=(C // cb,),
                  in_specs=[pl.BlockSpec((T, cb), lambda i: (0, i)), pl.BlockSpec((W, cb), lambda i: (0, i)),
                            pl.BlockSpec((1, cb), lambda i: (0, i))],
                  out_specs=pl.BlockSpec((T, cb), lambda i: (0, i)), out_shape=jax.ShapeDtypeStruct((T, C), F32),
                  scratch_shapes=[pltpu.VMEM((T + SEQ_PAD, cb), F32)],
                  compiler_params=_params(("parallel",)))(x, w, bias)


def _dwconv_bwd(name, dc, x, w, T):
    W, C = w.shape
    cb = min(C, LANES)
    tc = _tile(T, 128)

    def body(dc_ref, x_ref, w_ref, dx_ref, dw_ref, xp_ref, dp_ref, acc_ref):
        xp_ref[pl.ds(0, SEQ_PAD), :] = jnp.zeros((SEQ_PAD, cb), F32)
        dp_ref[pl.ds(T, SEQ_PAD), :] = jnp.zeros((SEQ_PAD, cb), F32)
        acc_ref[...] = jnp.zeros((W * 8, cb), F32)

        def fill(c, _):
            base = pl.multiple_of(c * tc, tc)
            xp_ref[pl.ds(base + SEQ_PAD, tc), :] = x_ref[pl.ds(base, tc), :]
            dp_ref[pl.ds(base, tc), :] = dc_ref[pl.ds(base, tc), :]
            return 0

        lax.fori_loop(0, T // tc, fill, 0)

        def chunk(c, _):
            base = pl.multiple_of(c * tc, tc)
            d = dc_ref[pl.ds(base, tc), :]
            acc = jnp.zeros((tc, cb), F32)
            for k in range(W):
                acc = acc + w_ref[pl.ds(k, 1), :] * dp_ref[pl.ds(base + (W - 1) - k, tc), :]
                prod = d * xp_ref[pl.ds(base + SEQ_PAD - (W - 1) + k, tc), :]
                acc_ref[pl.ds(8 * k, 8), :] += jnp.sum(prod.reshape(tc // 8, 8, cb), axis=0)
            dx_ref[pl.ds(base, tc), :] = acc
            return 0

        lax.fori_loop(0, T // tc, chunk, 0)
        for k in range(W):
            dw_ref[pl.ds(k, 1), :] = jnp.sum(acc_ref[pl.ds(8 * k, 8), :], axis=0, keepdims=True)

    return _pcall(body, name=name, grid=(C // cb,),
                  in_specs=[pl.BlockSpec((T, cb), lambda i: (0, i)), pl.BlockSpec((T, cb), lambda i: (0, i)),
                            pl.BlockSpec((W, cb), lambda i: (0, i))],
                  out_specs=[pl.BlockSpec((T, cb), lambda i: (0, i)), pl.BlockSpec((W, cb), lambda i: (0, i))],
                  out_shape=[jax.ShapeDtypeStruct((T, C), F32), jax.ShapeDtypeStruct((W, C), F32)],
                  scratch_shapes=[pltpu.VMEM((T + SEQ_PAD, cb), F32), pltpu.VMEM((T + SEQ_PAD, cb), F32),
                                  pltpu.VMEM((W * 8, cb), F32)],
                  compiler_params=_params(("parallel",)))(dc, x, w)


def _pool_count(base, tc, cg, win):
    t = (lax.broadcasted_iota(jnp.int32, (tc, cg), 0) + base + 1).astype(F32)
    return jnp.minimum(t, float(win))


def _pool_fwd(name, u, T):
    D = u.shape[1]
    G = len(POOL_WINDOWS)
    cg = min(D // G, LANES)
    nb = (D // G) // cg
    tc = _tile(T, 128)

    def body(u_ref, o_ref, up_ref):
        g = pl.program_id(0)
        up_ref[pl.ds(0, SEQ_PAD), :] = jnp.zeros((SEQ_PAD, cg), F32)

        def fill(c, _):
            base = pl.multiple_of(c * tc, tc)
            up_ref[pl.ds(base + SEQ_PAD, tc), :] = u_ref[pl.ds(base, tc), :]
            return 0

        lax.fori_loop(0, T // tc, fill, 0)
        for gi, win in enumerate(POOL_WINDOWS):
            @pl.when(g == gi)
            def _(win=win):
                def chunk(c, _):
                    base = pl.multiple_of(c * tc, tc)
                    acc = up_ref[pl.ds(base + SEQ_PAD, tc), :]
                    for j in range(1, win):
                        acc = acc + up_ref[pl.ds(base + SEQ_PAD - j, tc), :]
                    mixed = acc / _pool_count(base, tc, cg, win) - u_ref[pl.ds(base, tc), :]
                    o_ref[pl.ds(base, tc), :] = mixed.astype(BF16)
                    return 0

                lax.fori_loop(0, T // tc, chunk, 0)

    return _pcall(body, name=name, grid=(G, nb), in_specs=[pl.BlockSpec((T, cg), lambda g, i: (0, g * nb + i))],
                  out_specs=pl.BlockSpec((T, cg), lambda g, i: (0, g * nb + i)), out_shape=jax.ShapeDtypeStruct((T, D), BF16),
                  scratch_shapes=[pltpu.VMEM((T + SEQ_PAD, cg), F32)], compiler_params=_params(("parallel", "parallel")))(u)


def _pool_bwd(name, dm, T):
    D = dm.shape[1]
    G = len(POOL_WINDOWS)
    cg = min(D // G, LANES)
    nb = (D // G) // cg
    tc = _tile(T, 128)

    def body(d_ref, o_ref, qp_ref):
        g = pl.program_id(0)
        qp_ref[pl.ds(T, SEQ_PAD), :] = jnp.zeros((SEQ_PAD, cg), F32)
        for gi, win in enumerate(POOL_WINDOWS):
            @pl.when(g == gi)
            def _(win=win):
                def fill(c, _):
                    base = pl.multiple_of(c * tc, tc)
                    qp_ref[pl.ds(base, tc), :] = d_ref[pl.ds(base, tc), :] / _pool_count(base, tc, cg, win)
                    return 0

                lax.fori_loop(0, T // tc, fill, 0)

                def chunk(c, _):
                    base = pl.multiple_of(c * tc, tc)
                    acc = qp_ref[pl.ds(base, tc), :]
                    for j in range(1, win):
                        acc = acc + qp_ref[pl.ds(base + j, tc), :]
                    o_ref[pl.ds(base, tc), :] = (acc - d_ref[pl.ds(base, tc), :]).astype(BF16)
                    return 0

                lax.fori_loop(0, T // tc, chunk, 0)

    return _pcall(body, name=name, grid=(G, nb), in_specs=[pl.BlockSpec((T, cg), lambda g, i: (0, g * nb + i))],
                  out_specs=pl.BlockSpec((T, cg), lambda g, i: (0, g * nb + i)), out_shape=jax.ShapeDtypeStruct((T, D), BF16),
                  scratch_shapes=[pltpu.VMEM((T + SEQ_PAD, cg), F32)], compiler_params=_params(("parallel", "parallel")))(dm)


def _pool_mm(name, mode, a, b, T, scale=None):
    G = len(POOL_WINDOWS)
    D = a.shape[1]
    cg = D // G
    tm = _tile(T, 2048)
    if mode == "tn":
        ins = [(a, _spec((tm, cg), lambda i, j, k: (k, j))), (b, _spec((tm, cg), lambda i, j, k: (k, j)))]
        outs = [((G, cg, cg), F32, _spec((None, cg, cg), lambda i, j, k: (j, 0, 0)))]
        return _mm(name, "tn", (1, G, T // tm), ins, outs, lambda acc: (acc,))[0]
    ins = [(a, _spec((tm, cg), lambda i, j, k: (i, j))), (b, _spec((None, cg, cg), lambda i, j, k: (j, 0, 0)))]
    if scale is not None:
        ins.append((scale, _spec((1, cg), lambda i, j, k: (0, j))))
        outs = [((T, D), BF16, _spec((tm, cg), lambda i, j, k: (i, j))), ((T, D), BF16, _spec((tm, cg), lambda i, j, k: (i, j)))]
        return _mm(name, mode, (T // tm, G, 1), ins, outs, lambda acc, s: (acc * s, acc))
    outs = [((T, D), F32, _spec((tm, cg), lambda i, j, k: (i, j)))]
    return _mm(name, mode, (T // tm, G, 1), ins, outs)[0]


def _sc_mid_fwd(name, z, w, T):
    D = w.shape[1]
    W = w.shape[0]
    cb = min(D, LANES)
    nb = D // cb
    tc = _tile(T, 256)

    def body(b_ref, c_ref, v_ref, w_ref, o_ref, pp_ref):
        pp_ref[pl.ds(0, SEQ_PAD), :] = jnp.zeros((SEQ_PAD, cb), F32)

        def fill(c, _):
            base = pl.multiple_of(c * tc, tc)
            pp_ref[pl.ds(base + SEQ_PAD, tc), :] = c_ref[pl.ds(base, tc), :].astype(F32) * v_ref[pl.ds(base, tc), :].astype(F32)
            return 0

        lax.fori_loop(0, T // tc, fill, 0)

        def chunk(c, _):
            base = pl.multiple_of(c * tc, tc)
            q = jnp.zeros((tc, cb), F32)
            for k in range(W):
                q = q + w_ref[pl.ds(k, 1), :] * pp_ref[pl.ds(base + SEQ_PAD - (W - 1) + k, tc), :]
            o_ref[pl.ds(base, tc), :] = (b_ref[pl.ds(base, tc), :].astype(F32) * q).astype(BF16)
            return 0

        lax.fori_loop(0, T // tc, chunk, 0)

    return _pcall(body, name=name, grid=(nb,),
                  in_specs=[pl.BlockSpec((T, cb), lambda i: (0, i)), pl.BlockSpec((T, cb), lambda i: (0, nb + i)),
                            pl.BlockSpec((T, cb), lambda i: (0, 2 * nb + i)), pl.BlockSpec((W, cb), lambda i: (0, i))],
                  out_specs=pl.BlockSpec((T, cb), lambda i: (0, i)), out_shape=jax.ShapeDtypeStruct((T, D), BF16),
                  scratch_shapes=[pltpu.VMEM((T + SEQ_PAD, cb), F32)],
                  compiler_params=_params(("parallel",)))(z, z, z, w)


def _sc_mid_bwd(name, z, dm, w, T):
    D = w.shape[1]
    W = w.shape[0]
    cb = min(D, LANES)
    nb = D // cb
    tc = _tile(T, 128)

    def body(b_ref, c_ref, v_ref, d_ref, w_ref, dz_ref, dw_ref, pp_ref, dq_ref, acc_ref):
        part = pl.program_id(0)
        pp_ref[pl.ds(0, SEQ_PAD), :] = jnp.zeros((SEQ_PAD, cb), F32)
        dq_ref[pl.ds(T, SEQ_PAD), :] = jnp.zeros((SEQ_PAD, cb), F32)
        acc_ref[...] = jnp.zeros((W * 8, cb), F32)

        def fill(c, _):
            base = pl.multiple_of(c * tc, tc)
            pp_ref[pl.ds(base + SEQ_PAD, tc), :] = c_ref[pl.ds(base, tc), :].astype(F32) * v_ref[pl.ds(base, tc), :].astype(F32)
            dq_ref[pl.ds(base, tc), :] = d_ref[pl.ds(base, tc), :] * b_ref[pl.ds(base, tc), :].astype(F32)
            return 0

        lax.fori_loop(0, T // tc, fill, 0)

        @pl.when(part == 0)
        def _():
            def chunk(c, _):
                base = pl.multiple_of(c * tc, tc)
                dq = dq_ref[pl.ds(base, tc), :]
                q = jnp.zeros((tc, cb), F32)
                for k in range(W):
                    p = pp_ref[pl.ds(base + SEQ_PAD - (W - 1) + k, tc), :]
                    q = q + w_ref[pl.ds(k, 1), :] * p
                    acc_ref[pl.ds(8 * k, 8), :] += jnp.sum((dq * p).reshape(tc // 8, 8, cb), axis=0)
                dz_ref[pl.ds(base, tc), :] = (d_ref[pl.ds(base, tc), :] * q).astype(BF16)
                return 0

            lax.fori_loop(0, T // tc, chunk, 0)

        @pl.when(part > 0)
        def _():
            def chunk(c, _):
                base = pl.multiple_of(c * tc, tc)
                dp = jnp.zeros((tc, cb), F32)
                for k in range(W):
                    dp = dp + w_ref[pl.ds(k, 1), :] * dq_ref[pl.ds(base + (W - 1) - k, tc), :]
                other = jnp.where(part == 1, v_ref[pl.ds(base, tc), :], c_ref[pl.ds(base, tc), :]).astype(F32)
                dz_ref[pl.ds(base, tc), :] = (dp * other).astype(BF16)
                return 0

            lax.fori_loop(0, T // tc, chunk, 0)

        for k in range(W):
            dw_ref[pl.ds(k, 1), :] = jnp.sum(acc_ref[pl.ds(8 * k, 8), :], axis=0, keepdims=True)

    dz, dw3 = _pcall(body, name=name, grid=(3, nb),
                     in_specs=[pl.BlockSpec((T, cb), lambda p, i: (0, i)), pl.BlockSpec((T, cb), lambda p, i: (0, nb + i)),
                               pl.BlockSpec((T, cb), lambda p, i: (0, 2 * nb + i)), pl.BlockSpec((T, cb), lambda p, i: (0, i)),
                               pl.BlockSpec((W, cb), lambda p, i: (0, i))],
                     out_specs=[pl.BlockSpec((T, cb), lambda p, i: (0, p * nb + i)),
                                pl.BlockSpec((None, W, cb), lambda p, i: (p, 0, i))],
                     out_shape=[jax.ShapeDtypeStruct((T, 3 * D), BF16), jax.ShapeDtypeStruct((3, W, D), F32)],
                     scratch_shapes=[pltpu.VMEM((T + SEQ_PAD, cb), F32), pltpu.VMEM((T + SEQ_PAD, cb), F32),
                                     pltpu.VMEM((W * 8, cb), F32)],
                     compiler_params=_params(("parallel", "parallel")))(z, z, z, dm, w)
    return dz, dw3


def _ret_tables(dk, dv):
    C = RET_CHUNK
    lg = np.log1p(-np.exp2(-5.0 - np.arange(RET_HEADS, dtype=np.float64)))
    idx = np.arange(C, dtype=np.float64)
    rel = idx[:, None] - idx[None, :]
    mask = np.where(rel >= 0, np.exp(lg[:, None, None] * np.maximum(rel, 0.0)), 0.0)
    qd = np.exp(lg[:, None] * (idx + 1.0))[:, :, None] * np.ones((1, 1, dk))
    kd = np.exp(lg[:, None] * (C - 1.0 - idx))[:, :, None] * np.ones((1, 1, dk))
    cd = np.exp(lg * C)[:, None, None] * np.ones((1, 1, dv))
    return tuple(jnp.asarray(a, F32) for a in (mask, qd, kd, cd))


def _ret_fwd(name, q, k, v, tabs, T):
    H, C = RET_HEADS, RET_CHUNK
    dk, dv = q.shape[1] // H, v.shape[1] // H
    nch = T // C
    mask, qd, kd, cd = tabs

    def body(q_ref, k_ref, v_ref, m_ref, qd_ref, kd_ref, cd_ref, o_ref, st_ref, s_ref):
        @pl.when(pl.program_id(0) == 0)
        def _():
            s_ref[...] = jnp.zeros((H, dk, dv), F32)

        for h in range(H):
            qv, kv = q_ref[:, h * dk:(h + 1) * dk], k_ref[:, h * dk:(h + 1) * dk]
            vv = v_ref[:, h * dv:(h + 1) * dv]
            s = s_ref[h]
            sb = s.astype(BF16)
            st_ref[h] = sb
            scores = lax.dot_general(qv, kv, NT, preferred_element_type=F32) * m_ref[h]
            intra = lax.dot_general(scores.astype(BF16), vv, NN, preferred_element_type=F32)
            cross = lax.dot_general((qv.astype(F32) * qd_ref[h]).astype(BF16), sb, NN, preferred_element_type=F32)
            o_ref[:, h * dv:(h + 1) * dv] = intra + cross
            upd = lax.dot_general((kv.astype(F32) * kd_ref[h]).astype(BF16), vv, TN, preferred_element_type=F32)
            s_ref[h] = s * cd_ref[h] + upd

    whole = lambda a: pl.BlockSpec(a.shape, lambda n: (0, 0, 0))
    return _pcall(body, name=name, grid=(nch,),
                  in_specs=[pl.BlockSpec((C, H * dk), lambda n: (n, 0)), pl.BlockSpec((C, H * dk), lambda n: (n, 0)),
                            pl.BlockSpec((C, H * dv), lambda n: (n, 0)), whole(mask), whole(qd), whole(kd), whole(cd)],
                  out_specs=[pl.BlockSpec((C, H * dv), lambda n: (n, 0)),
                             pl.BlockSpec((H, None, dk, dv), lambda n: (0, n, 0, 0))],
                  out_shape=[jax.ShapeDtypeStruct((T, H * dv), F32), jax.ShapeDtypeStruct((H, nch, dk, dv), BF16)],
                  scratch_shapes=[pltpu.VMEM((H, dk, dv), F32)],
                  compiler_params=_params(("arbitrary",)))(q, k, v, mask, qd, kd, cd)


def _ret_bwd(name, q, k, v, do, st, tabs, T):
    H, C = RET_HEADS, RET_CHUNK
    dk, dv = q.shape[1] // H, v.shape[1] // H
    nch = T // C
    mask, qd, kd, cd = tabs

    def body(q_ref, k_ref, v_ref, do_ref, st_ref, m_ref, qd_ref, kd_ref, cd_ref, dq_ref, dk_ref, dv_ref, ds_ref):
        @pl.when(pl.program_id(0) == 0)
        def _():
            ds_ref[...] = jnp.zeros((H, dk, dv), F32)

        for h in range(H):
            qv, kv = q_ref[:, h * dk:(h + 1) * dk], k_ref[:, h * dk:(h + 1) * dk]
            vv, dov = v_ref[:, h * dv:(h + 1) * dv], do_ref[:, h * dv:(h + 1) * dv]
            sb = st_ref[h]
            m = m_ref[h]
            ds = ds_ref[h]
            dsb = ds.astype(BF16)
            a = (lax.dot_general(qv, kv, NT, preferred_element_type=F32) * m).astype(BF16)
            dp = (lax.dot_general(dov, vv, NT, preferred_element_type=F32) * m).astype(BF16)
            qa = (qv.astype(F32) * qd_ref[h]).astype(BF16)
            kb = (kv.astype(F32) * kd_ref[h]).astype(BF16)
            dq_ref[:, h * dk:(h + 1) * dk] = (lax.dot_general(dp, kv, NN, preferred_element_type=F32)
                                              + qd_ref[h] * lax.dot_general(dov, sb, NT, preferred_element_type=F32))
            dk_ref[:, h * dk:(h + 1) * dk] = (lax.dot_general(dp, qv, TN, preferred_element_type=F32)
                                              + kd_ref[h] * lax.dot_general(vv, dsb, NT, preferred_element_type=F32))
            dv_ref[:, h * dv:(h + 1) * dv] = (lax.dot_general(a, dov, TN, preferred_element_type=F32)
                                              + lax.dot_general(kb, dsb, NN, preferred_element_type=F32)).astype(BF16)
            ds_ref[h] = ds * cd_ref[h] + lax.dot_general(qa, dov, TN, preferred_element_type=F32)

    rev = lambda n: (nch - 1 - n, 0)
    whole = lambda a: pl.BlockSpec(a.shape, lambda n: (0, 0, 0))
    return _pcall(body, name=name, grid=(nch,),
                  in_specs=[pl.BlockSpec((C, H * dk), rev), pl.BlockSpec((C, H * dk), rev), pl.BlockSpec((C, H * dv), rev),
                            pl.BlockSpec((C, H * dv), rev), pl.BlockSpec((H, None, dk, dv), lambda n: (0, nch - 1 - n, 0, 0)),
                            whole(mask), whole(qd), whole(kd), whole(cd)],
                  out_specs=[pl.BlockSpec((C, H * dk), rev), pl.BlockSpec((C, H * dk), rev), pl.BlockSpec((C, H * dv), rev)],
                  out_shape=[jax.ShapeDtypeStruct((T, H * dk), F32), jax.ShapeDtypeStruct((T, H * dk), F32),
                             jax.ShapeDtypeStruct((T, H * dv), BF16)],
                  scratch_shapes=[pltpu.VMEM((H, dk, dv), F32)],
                  compiler_params=_params(("arbitrary",)))(q, k, v, do, st, mask, qd, kd, cd)


def _local_step(x, pos, tgt, ng, get_w, put_g, first_deps, T, D):
    tr = _tile(T, 512)
    trw = _tile(T, 256)
    row = lambda i: ng[i:i + 1]

    u = _rowwise("rms_first", lambda xv, gv: _rms(xv, gv), [(x, D, 0), (row(0), None, None)], [("row", D, BF16)], T, tr,
                 deps=first_deps)[0]
    h = x
    saved = []
    cos = sin = None
    tabs = None
    for l in range(DEPTH):
        mixer = l % 4
        wts = get_w(l, "mix", h)
        sv = {"h_in": h, "u": u}
        if mixer == 0:
            z = _mm_cols("conv_in", u, wts["conv_w_in"], (), T, lambda acc, b: (acc + b,), (wts["conv_b_in"],), 1, BF16)[0]
            v = _rowwise("conv_glu", lambda za, zg: _glu(za.astype(F32), zg.astype(F32)), [(z, D, 0), (z, D, 1)], [("row", D, F32)], T, tr)[0]
            c = _dwconv_fwd("conv_dw", v, wts["conv_dw"], wts["conv_dw_b"], T)
            s = _rowwise("conv_ln", _ln_silu, [(c, D, 0), (wts["conv_ln_g"], None, None), (wts["conv_ln_b"], None, None)],
                         [("row", D, BF16)], T, tr)[0]
            y = _mm_rows("conv_out", s, wts["conv_w_out"], T, lambda acc, b: (acc + b,), (wts["conv_b_out"],), BF16)
            sv.update(z=z, v=v, c=c, s=s)
        elif mixer == 1:
            mixed = _pool_fwd("pool_fwd", u, T)
            y, ycat = _pool_mm("pool_mm", "nn", mixed, wts["pool_w"], T, wts["pool_scale"])
            sv.update(mixed=mixed, ycat=ycat)
        elif mixer == 2:
            z = _mm_cols("sc_in", u, wts["sc_w_in"], (), T, out_dtype=BF16)[0]
            m = _sc_mid_fwd("sc_mid", z, wts["sc_dw"], T)
            y = _mm_rows("sc_out", m, wts["sc_w_out"], T, out_dtype=BF16)
            sv.update(z=z, m=m)
        else:
            dk = D // RET_HEADS
            half = dk // 2
            inv_freq = jnp.asarray((ROPE_BASE ** (-np.arange(half, dtype=np.float64) / half)).astype(np.float32)).reshape(1, half)

            def _cs(p, f):
                ang = p * f
                return jnp.cos(ang), jnp.sin(ang)

            cos, sin = _rowwise("rope_tab", _cs, [(pos, 1, 0), (inv_freq, None, None)],
                                [("row", half, F32), ("row", half, F32)], T, tr)
            tabs = _ret_tables(dk, 2 * dk)
            z = _mm_cols("ret_in", u, wts["ret_w_in"], (), T, out_dtype=BF16)[0]

            def _pre(zq, zk, zv, cv, sn):
                return _rot(zq.astype(F32), cv, sn, 1.0), _rot(zk.astype(F32), cv, sn, 1.0) * (dk ** -0.5), zv

            qr, kr, vb = _rowwise("ret_pre", _pre, [(z, D, 0), (z, D, 1), (z, 2 * D, 1), (cos, half, 0), (sin, half, 0)],
                                  [("row", D, BF16), ("row", D, BF16), ("row", 2 * D, BF16)], T, trw)
            o, st = _ret_fwd("ret_fwd", qr, kr, vb, tabs, T)
            out = _rowwise("ret_post", lambda ov, gv: _gate_norm(ov, gv.astype(F32)), [(o, 2 * D, 0), (z, 2 * D, 2)], [("row", 2 * D, BF16)], T, trw)[0]
            y = _mm_rows("ret_out", out, wts["ret_w_out"], T, out_dtype=BF16)
            sv.update(z=z, qr=qr, kr=kr, vb=vb, o=o, st=st, out=out)
        sv["y1"] = y
        h, u = _rowwise("bound_a", lambda hv, yv, gp, gn: (lambda hn: (hn, _rms(hn, gn)))(hv + _rms(yv.astype(F32), gp)),
                        [(h, D, 0), (y, D, 0), (row(4 * l + 1), None, None), (row(4 * l + 2), None, None)],
                        [("row", D, F32), ("row", D, BF16)], T, tr)
        sv["h_mid"], sv["u2"] = h, u
        wm = get_w(l, "mlp", h)
        mlp_deps = wm.pop("_deps", [])
        wts = dict(wts, **wm)
        sv["w"] = wts
        r = _mm_cols("mlp_up", u, wts["mlp_up"], (), T, lambda acc: (jnp.maximum(acc, 0.0),), (), 1, BF16, deps=mlp_deps)[0]
        y = _mm_whole_k("mlp_down", r, wts["mlp_down"], T, "rows", pre=_square, out_dtype=BF16)
        sv.update(r=r, y2=y)
        if l + 1 < DEPTH:
            udt = F32 if (l + 1) % 4 == 1 else BF16
            h, u = _rowwise("bound_b", lambda hv, yv, gp, gn: (lambda hn: (hn, _rms(hn, gn)))(hv + _rms(yv.astype(F32), gp)),
                            [(h, D, 0), (y, D, 0), (row(4 * l + 3), None, None), (row(4 * l + 4), None, None)],
                            [("row", D, F32), ("row", D, udt)], T, tr)
        saved.append(sv)

    def _final(hv, yv, gp, tv):
        yf = yv.astype(F32)
        out, vjp = jax.vjp(_rms, yf, gp)
        e = hv + out - tv
        dhv = e * (1.0 / D)
        dy, dgp = vjp(dhv)
        return dhv, dy, _colsum(e * e), dgp

    gn = [None] * (4 * DEPTH)
    dh, dy_last, loss_row, gn[4 * DEPTH - 1] = _rowwise(
        "final", _final, [(h, D, 0), (y, D, 0), (row(4 * DEPTH - 1), None, None), (tgt, D, 0)],
        [("row", D, F32), ("row", D, BF16), ("acc", (1, D)), ("acc", (1, D))], T, tr)

    F = saved[0]["r"].shape[1]
    du_next = None
    tok = None
    for l in reversed(range(DEPTH)):
        sv = saved[l]
        wts = sv["w"]
        g = {}
        deps = [] if tok is None else [tok]
        mixer = l % 4
        if du_next is None:
            dy = dy_last
        else:
            dh, dy, gn[4 * l + 3], gn[4 * l + 4], _ = _bound_bwd("bwd_b", dh, du_next, saved[l + 1]["h_in"], sv["y2"],
                                                              row(4 * l + 3), row(4 * l + 4), T, D, tr, deps)
        F8 = F // N_DEV
        tm = _tile(T, 2048)
        da = _mm("mlp_dhid", "nt", (T // tm, N_DEV, 1),
                 [(dy, _spec((tm, D), lambda i, j, k: (i, 0))),
                  (wts["mlp_down"], _spec((None, F8, D), lambda i, j, k: (j, 0, 0))),
                  (sv["r"], _spec((tm, F8), lambda i, j, k: (i, j)))],
                 [((T, F), BF16, _spec((tm, F8), lambda i, j, k: (i, j)))],
                 lambda acc, rr: (acc * (2.0 * rr.astype(F32)),))[0]
        tk = T
        g["mlp_down"] = _mm("mlp_dwdown", "tn", (1, N_DEV, T // tk),
                            [(sv["r"], _spec((tk, F8), lambda i, j, k: (k, j))), (dy, _spec((tk, D), lambda i, j, k: (k, 0)))],
                            [(wts["mlp_down"].shape, BF16, _spec((None, F8, D), lambda i, j, k: (j, 0, 0)))],
                            lambda acc: (acc,), pre=_square)[0]
        g["mlp_up"] = _mm_cols_grad("mlp_dwup", sv["u2"], da, wts["mlp_up"].shape, (), T)
        du = _mm_whole_k("mlp_du", da, wts["mlp_up"], T, "cols", out_dtype=BF16)
        tok = put_g(l, "mlp", g, du, None)
        g = {}
        dh, dy, gn[4 * l + 1], gn[4 * l + 2], dy_sum = _bound_bwd("bwd_a", dh, du, sv["h_mid"], sv["y1"],
                                                                row(4 * l + 1), row(4 * l + 2), T, D, tr, [] if tok is None else [tok])
        u = sv["u"]
        if mixer == 0:
            g["conv_b_out"] = dy_sum
            ds = _mm_rows_t("conv_ds", dy, wts["conv_w_out"], T)
            g["conv_w_out"] = _mm_rows_grad("conv_dwout", sv["s"], dy, T)

            def _lnb(cv, gv, bv, dsv):
                _, vjp = jax.vjp(_ln_silu, cv, gv, bv)
                dc, dg, db = vjp(dsv)
                return dc, dg, db, _colsum(dc)

            dc, g["conv_ln_g"], g["conv_ln_b"], g["conv_dw_b"] = _rowwise(
                "conv_ln_bwd", _lnb, [(sv["c"], D, 0), (wts["conv_ln_g"], None, None), (wts["conv_ln_b"], None, None), (ds, D, 0)],
                [("row", D, F32), ("acc", (1, D)), ("acc", (1, D)), ("acc", (1, D))], T, tr)
            dv, g["conv_dw"] = _dwconv_bwd("conv_dw_bwd", dc, sv["v"], wts["conv_dw"], T)

            def _glub(za, zg, dvv):
                _, vjp = jax.vjp(_glu, za.astype(F32), zg.astype(F32))
                dza, dzg = vjp(dvv)
                dz = jnp.concatenate([dza, dzg], axis=-1)
                return dz, _colsum(dz)

            dz, g["conv_b_in"] = _rowwise("conv_glu_bwd", _glub, [(sv["z"], D, 0), (sv["z"], D, 1), (dv, D, 0)],
                                          [("row", 2 * D, BF16), ("acc", (1, 2 * D))], T, tr)
            g["conv_w_in"] = _mm_cols_grad("conv_dwin", u, dz, wts["conv_w_in"].shape, (), T)
            du = _mm_whole_k("conv_du", dz, wts["conv_w_in"], T, "cols", out_dtype=BF16)
        elif mixer == 1:
            def _psb(dyv, yc, sc):
                return dyv * sc, _colsum(dyv.astype(F32) * yc.astype(F32))

            dyg, g["pool_scale"] = _rowwise("pool_scale_bwd", _psb, [(dy, D, 0), (sv["ycat"], D, 0), (wts["pool_scale"], None, None)],
                                            [("row", D, BF16), ("acc", (1, D))], T, tr)
            dmix = _pool_mm("pool_dmix", "nt", dyg, wts["pool_w"], T)
            g["pool_w"] = _pool_mm("pool_dw", "tn", sv["mixed"], dyg, T)
            du = _pool_bwd("pool_bwd", dmix, T)
        elif mixer == 2:
            dm = _mm_rows_t("sc_dm", dy, wts["sc_w_out"], T)
            g["sc_w_out"] = _mm_rows_grad("sc_dwout", sv["m"], dy, T)
            dz, dw3 = _sc_mid_bwd("sc_mid_bwd", sv["z"], dm, wts["sc_dw"], T)
            g["sc_dw"] = dw3[0]
            g["sc_w_in"] = _mm_cols_grad("sc_dwin", u, dz, wts["sc_w_in"].shape, (), T)
            du = _mm_whole_k("sc_du", dz, wts["sc_w_in"], T, "cols", out_dtype=BF16)
        else:
            dk = D // RET_HEADS
            half = dk // 2
            dout = _mm_rows_t("ret_dout", dy, wts["ret_w_out"], T, BF16)
            g["ret_w_out"] = _mm_rows_grad("ret_dwout", sv["out"], dy, T)

            def _postb(ov, gv, dv_):
                _, vjp = jax.vjp(_gate_norm, ov, gv.astype(F32))
                return vjp(dv_.astype(F32))

            do, dzg = _rowwise("ret_post_bwd", _postb, [(sv["o"], 2 * D, 0), (sv["z"], 2 * D, 2), (dout, 2 * D, 0)],
                               [("row", 2 * D, BF16), ("row", 2 * D, BF16)], T, trw)
            dqr, dkr, dvb = _ret_bwd("ret_bwd", sv["qr"], sv["kr"], sv["vb"], do, sv["st"], tabs, T)

            def _preb(dq, dk_, dvv, dgv, cv, sn):
                return jnp.concatenate([_rot(dq, cv, sn, -1.0).astype(BF16), (_rot(dk_, cv, sn, -1.0) * (dk ** -0.5)).astype(BF16),
                                        dvv, dgv], axis=-1)

            dz = _rowwise("ret_pre_bwd", _preb, [(dqr, D, 0), (dkr, D, 0), (dvb, 2 * D, 0), (dzg, 2 * D, 0), (cos, half, 0), (sin, half, 0)],
                          [("row", 6 * D, BF16)], T, trw)[0]
            g["ret_w_in"] = _mm_cols_grad("ret_dwin", u, dz, wts["ret_w_in"].shape, (), T)
            du = _mm_whole_k("ret_du", dz, wts["ret_w_in"], T, "cols", out_dtype=BF16)
        du_next = du
        if l > 0:
            tok = put_g(l, "mix", g, du, None)

    def _fb(dhv, duv, xv, gv):
        _, vjp = jax.vjp(_rms, xv, gv)
        dx, dg = vjp(duv.astype(F32))
        return dhv + dx, dg

    grad_x, gn[0] = _rowwise("bwd_first", _fb, [(dh, D, 0), (du_next, D, 0), (x, D, 0), (row(0), None, None)],
                             [("row", D, F32), ("acc", (1, D))], T, tr)
    put_g(0, "mix", g, grad_x, jnp.concatenate(gn, axis=0))
    return loss_row, grad_x


def _bound_bwd(name, dh, du, h_new, y, g_post, g_next, T, D, tr, deps=()):
    def fn(dhv, duv, hn, yv, gp, gnx):
        _, vjp1 = jax.vjp(_rms, hn, gnx)
        d1, dgn = vjp1(duv.astype(F32))
        dht = dhv + d1
        _, vjp2 = jax.vjp(_rms, yv.astype(F32), gp)
        dy, dgp = vjp2(dht)
        return dht, dy, dgp, dgn, _colsum(dy)

    return _rowwise(name, fn, [(dh, D, 0), (du, D, 0), (h_new, D, 0), (y, D, 0), (g_post, None, None), (g_next, None, None)],
                    [("row", D, F32), ("row", D, BF16), ("acc", (1, D)), ("acc", (1, D)), ("acc", (1, D))], T, tr, deps)


def _adamw(name, parts, w, m, v, own=None, layer=None, bufs=None):
    P, R, C = parts.shape
    tr = _tile(R, 256)
    n_s = 1 if own is None else 2

    def body(*refs):
        p_ref = refs[0]
        w_ref, m_ref, v_ref = refs[n_s:n_s + 3]
        g_ref, d_ref, m2_ref, v2_ref = refs[-4:]

        def slot(i):
            if own is None:
                return p_ref[i].astype(F32)
            here = 2 * lax.axis_index("x") + lax.axis_index("y") == i
            return jnp.where(here, refs[1][i], p_ref[i]).astype(F32)

        g = slot(0)
        for i in range(1, P):
            g = g + slot(i)
        wv = w_ref[...]
        m2 = ADAM_B1 * m_ref[...] + (1.0 - ADAM_B1) * g
        v2 = ADAM_B2 * v_ref[...] + (1.0 - ADAM_B2) * (g * g)
        m_hat = m2 / (1.0 - ADAM_B1 ** ADAM_STEP)
        v_hat = v2 / (1.0 - ADAM_B2 ** ADAM_STEP)
        g_ref[...] = g
        d_ref[...] = -ADAM_LR * (m_hat / (jnp.sqrt(v_hat) + ADAM_EPS) + ADAM_WD * wv)
        m2_ref[...] = m2
        v2_ref[...] = v2

    pblk = pl.BlockSpec((P, tr, C), lambda i: (0, i, 0))
    stacks = [parts] if own is None else [parts, own]
    if layer is None:
        blk = pl.BlockSpec((tr, C), lambda i: (i, 0))
        return _pcall(body, name=name, grid=(R // tr,),
                      in_specs=[pblk] * n_s + [blk, blk, blk], out_specs=[blk] * 4,
                      out_shape=[jax.ShapeDtypeStruct((R, C), F32)] * 4, compiler_params=_params(("parallel",)))(*stacks, w, m, v)
    blk = pl.BlockSpec((None, tr, C), lambda i: (layer, i, 0))
    return _pcall(body, name=name, grid=(R // tr,),
                  in_specs=[pblk] * n_s + [blk, blk, blk] + [pl.BlockSpec(memory_space=pl.ANY)] * 4, out_specs=[blk] * 4,
                  out_shape=[jax.ShapeDtypeStruct(w.shape, F32)] * 4, input_output_aliases={n_s + 3 + i: i for i in range(4)},
                  compiler_params=_params(("parallel",)))(*stacks, w, m, v, *bufs)


def _pair_add(name, a, b):
    Q, _, R, C = a.shape
    tr = _tile(R, 1024)

    def body(c_ref, a_ref, b_ref, o_ref):
        o_ref[...] = (a_ref[...].astype(F32) + b_ref[...].astype(F32)).astype(BF16)

    core = lax.axis_index("c").astype(jnp.int32).reshape(1)
    blk = pl.BlockSpec((None, tr, C), lambda q, i, c: (q, i, 0))
    spec = pltpu.PrefetchScalarGridSpec(num_scalar_prefetch=1, grid=(Q, R // tr),
                                        in_specs=[pl.BlockSpec((None, None, tr, C), lambda q, i, c: (q, c[0], i, 0)), blk],
                                        out_specs=blk)
    return _pcall(body, name=name, grid_spec=spec, out_shape=jax.ShapeDtypeStruct((Q, R, C), BF16),
                  compiler_params=_params(("parallel", "parallel")))(core, a, b)


ANY = pl.BlockSpec(memory_space=pl.ANY)


def _place():
    x, y, c = lax.axis_index("x"), lax.axis_index("y"), lax.axis_index("c")
    chips = [(1 - x, y), (x, 1 - y), (1 - x, 1 - y)]
    return x, y, c, chips


def _gather_all(locs):
    n = len(locs)

    def body(*refs):
        ins, outs = refs[:n], refs[n:2 * n]
        send, recv, lsem = refs[2 * n:]
        x, y, c, chips = _place()
        me = 4 * x + 2 * y + c
        sib = (x, y, 1 - c)

        def cp(a, k, slot, to, src=None):
            return pltpu.make_async_remote_copy(src_ref=outs[a].at[slot] if src is None else src, dst_ref=outs[a].at[slot],
                                                send_sem=send.at[a, k], recv_sem=recv.at[a, k], device_id=to, device_id_type=MESH)

        mine = [pltpu.make_async_copy(ins[a], outs[a].at[me], lsem.at[a]) for a in range(n)]
        for d in mine:
            d.start()
        first = []
        for a in range(n):
            first.append(cp(a, 0, me, sib, src=ins[a]))
            for j, chip in enumerate(chips):
                first.append(cp(a, 1 + j, me, (chip[0], chip[1], c), src=ins[a]))
        for d in first:
            d.start()
        passed = []
        for j, chip in enumerate(chips):
            slot = 4 * chip[0] + 2 * chip[1] + c
            for a in range(n):
                cp(a, 1 + j, slot, sib).wait_recv()
                p = cp(a, 4 + j, slot, sib)
                p.start()
                passed.append(p)
        for a in range(n):
            cp(a, 0, 4 * x + 2 * y + (1 - c), sib).wait_recv()
        for j, chip in enumerate(chips):
            for a in range(n):
                cp(a, 4 + j, 4 * chip[0] + 2 * chip[1] + (1 - c), sib).wait_recv()
        for d in first + passed:
            d.wait_send()
        for d in mine:
            d.wait()

    return _pcall(body, name="gather_all", in_specs=[ANY] * n, out_specs=[ANY] * n,
                  out_shape=[jax.ShapeDtypeStruct((N_DEV,) + a.shape, a.dtype) for a in locs],
                  scratch_shapes=[pltpu.SemaphoreType.DMA((n, 7)), pltpu.SemaphoreType.DMA((n, 7)), pltpu.SemaphoreType.DMA((n,))],
                  compiler_params=pltpu.CompilerParams(has_side_effects=True))(*locs)


HBM = pl.BlockSpec(memory_space=pltpu.HBM)
SEM = pl.BlockSpec(memory_space=pltpu.SEMAPHORE)
DATAFLOW = pltpu.SideEffectType.DATAFLOW_SIDE_EFFECTING


def _peers():
    x, y, c = lax.axis_index("x"), lax.axis_index("y"), lax.axis_index("c")
    out = []
    for r in range(1, N_DEV):
        px = 1 - x if (r >> 2) & 1 else x
        py = 1 - y if (r >> 1) & 1 else y
        pc = 1 - c if r & 1 else c
        out.append(((px, py, pc), 4 * px + 2 * py + pc))
    return 4 * x + 2 * y + c, out


def _hbm(a):
    return pltpu.with_memory_space_constraint(a, pltpu.HBM)


def _gather_start(name, locs, lands, deps):
    n = len(locs)

    def body(*refs):
        ins, lnd = refs[:n], refs[n:2 * n]
        send, recv = refs[2 * n + len(deps)], refs[2 * n + len(deps) + 1]
        token = refs[-1]
        x, y, c, chips = _place()
        me = 4 * x + 2 * y + c
        for k, to in enumerate([(x, y, 1 - c)] + [(chip[0], chip[1], c) for chip in chips]):
            for a in range(n):
                pltpu.make_async_remote_copy(src_ref=ins[a], dst_ref=lnd[a].at[me], send_sem=send.at[a * 4 + k],
                                             recv_sem=recv.at[a * 4 + k], device_id=to, device_id_type=MESH).start()
        token[...] = jnp.zeros_like(token)

    out_shape = ((pltpu.SemaphoreType.DMA((n * 4,)), pltpu.SemaphoreType.DMA((n * 4,)))
                 + tuple(pltpu.HBM(a.shape, a.dtype) for a in locs) + tuple(pltpu.HBM(a.shape, a.dtype) for a in lands)
                 + (jax.ShapeDtypeStruct((8, LANES), F32),))
    res = _pcall(body, name=name, out_shape=out_shape, in_specs=[HBM] * (2 * n) + [ANY] * len(deps),
                 out_specs=(SEM, SEM) + (HBM,) * (2 * n) + (pl.BlockSpec(memory_space=pltpu.VMEM),),
                 input_output_aliases={a: 2 + a for a in range(2 * n)},
                 compiler_params=pltpu.CompilerParams(has_side_effects=DATAFLOW))(
        *[_hbm(a) for a in locs], *[_hbm(a) for a in lands], *deps)
    return res[0], res[1], res[2:2 + n], res[2 + n:2 + 2 * n], res[-1]


def _gather_pass(name, send, recv, shards, lands, after):
    n = len(shards)

    def body(*refs):
        ins, lnd = refs[:n], refs[n:2 * n]
        recv = refs[2 * n + 1]
        fsend, frecv = refs[2 * n + 3], refs[2 * n + 4]
        token = refs[-1]
        x, y, c, chips = _place()
        for j, chip in enumerate(chips):
            slot = 4 * chip[0] + 2 * chip[1] + c
            for a in range(n):
                pltpu.make_async_remote_copy(src_ref=ins[a], dst_ref=lnd[a].at[slot], send_sem=fsend.at[a * 3 + j],
                                             recv_sem=recv.at[a * 4 + 1 + j], device_id=(chip[0], chip[1], c),
                                             device_id_type=MESH).wait_recv()
                pltpu.make_async_remote_copy(src_ref=lnd[a].at[slot], dst_ref=lnd[a].at[slot], send_sem=fsend.at[a * 3 + j],
                                             recv_sem=frecv.at[a * 3 + j], device_id=(x, y, 1 - c), device_id_type=MESH).start()
        token[...] = jnp.zeros_like(token)

    out_shape = ((pltpu.SemaphoreType.DMA((n * 3,)), pltpu.SemaphoreType.DMA((n * 3,)))
                 + tuple(pltpu.HBM(a.shape, a.dtype) for a in shards) + tuple(pltpu.HBM(a.shape, a.dtype) for a in lands)
                 + (jax.ShapeDtypeStruct((8, LANES), F32),))
    res = _pcall(body, name=name, out_shape=out_shape, in_specs=[HBM] * (2 * n) + [SEM, SEM, ANY],
                 out_specs=(SEM, SEM) + (HBM,) * (2 * n) + (pl.BlockSpec(memory_space=pltpu.VMEM),),
                 input_output_aliases={a: 2 + a for a in range(2 * n)},
                 compiler_params=pltpu.CompilerParams(has_side_effects=DATAFLOW))(*shards, *lands, send, recv, after)
    return res[0], res[1], res[2:2 + n], res[2 + n:2 + 2 * n], res[-1]


def _gather_wait(name, send, recv, fsend, frecv, shards, lands, after):
    n = len(shards)

    def body(*refs):
        ins, lnd = refs[:n], refs[n:2 * n]
        send, recv, fsend, frecv = refs[2 * n:2 * n + 4]
        x, y, c, chips = _place()
        sib = (x, y, 1 - c)
        for a in range(n):
            own = pltpu.make_async_remote_copy(src_ref=ins[a], dst_ref=lnd[a].at[4 * x + 2 * y + (1 - c)], send_sem=send.at[a * 4],
                                               recv_sem=recv.at[a * 4], device_id=sib, device_id_type=MESH)
            own.wait_send()
            own.wait_recv()
            for j, chip in enumerate(chips):
                pltpu.make_async_remote_copy(src_ref=ins[a], dst_ref=lnd[a].at[0], send_sem=send.at[a * 4 + 1 + j],
                                             recv_sem=recv.at[a * 4 + 1 + j], device_id=(chip[0], chip[1], c),
                                             device_id_type=MESH).wait_send()
                fw = pltpu.make_async_remote_copy(src_ref=lnd[a].at[4 * chip[0] + 2 * chip[1] + c],
                                                  dst_ref=lnd[a].at[4 * chip[0] + 2 * chip[1] + (1 - c)], send_sem=fsend.at[a * 3 + j],
                                                  recv_sem=frecv.at[a * 3 + j], device_id=sib, device_id_type=MESH)
                fw.wait_send()
                fw.wait_recv()

    out_shape = tuple(pltpu.HBM(a.shape, a.dtype) for a in shards) + tuple(pltpu.HBM(a.shape, a.dtype) for a in lands)
    res = _pcall(body, name=name, out_shape=out_shape, in_specs=[HBM] * (2 * n) + [SEM, SEM, SEM, SEM, ANY], out_specs=(HBM,) * (2 * n),
                 input_output_aliases={a: a for a in range(2 * n)},
                 compiler_params=pltpu.CompilerParams(has_side_effects=DATAFLOW))(*shards, *lands, send, recv, fsend, frecv, after)
    return res[n:]


def _chip_start(name, cs, lands):
    n = len(cs)

    def body(*refs):
        ins, lnd = refs[:n], refs[n:2 * n]
        send, recv = refs[2 * n], refs[2 * n + 1]
        token = refs[-1]
        x, y, c, chips = _place()
        qme = 2 * x + y
        for j, chip in enumerate(chips):
            for a in range(n):
                pltpu.make_async_remote_copy(src_ref=ins[a].at[2 * chip[0] + chip[1]], dst_ref=lnd[a].at[qme], send_sem=send.at[a * 3 + j],
                                             recv_sem=recv.at[a * 3 + j], device_id=(chip[0], chip[1], c), device_id_type=MESH).start()
        token[...] = jnp.zeros_like(token)

    out_shape = ((pltpu.SemaphoreType.DMA((n * 3,)), pltpu.SemaphoreType.DMA((n * 3,)))
                 + tuple(pltpu.HBM(a.shape, a.dtype) for a in cs) + tuple(pltpu.HBM(a.shape, a.dtype) for a in lands)
                 + (jax.ShapeDtypeStruct((8, LANES), F32),))
    res = _pcall(body, name=name, out_shape=out_shape, in_specs=[HBM] * (2 * n),
                 out_specs=(SEM, SEM) + (HBM,) * (2 * n) + (pl.BlockSpec(memory_space=pltpu.VMEM),),
                 input_output_aliases={a: 2 + a for a in range(2 * n)},
                 compiler_params=pltpu.CompilerParams(has_side_effects=DATAFLOW))(*[_hbm(a) for a in cs], *[_hbm(a) for a in lands])
    return res[0], res[1], res[2:2 + n], res[2 + n:2 + 2 * n], res[-1]


def _chip_wait(name, send, recv, cs, lands, after):
    n = len(cs)

    def body(*refs):
        ins, lnd = refs[:n], refs[n:2 * n]
        send, recv = refs[2 * n], refs[2 * n + 1]
        x, y, c, chips = _place()
        for j, chip in enumerate(chips):
            q = 2 * chip[0] + chip[1]
            for a in range(n):
                cp = pltpu.make_async_remote_copy(src_ref=ins[a].at[q], dst_ref=lnd[a].at[q], send_sem=send.at[a * 3 + j],
                                                  recv_sem=recv.at[a * 3 + j], device_id=(chip[0], chip[1], c), device_id_type=MESH)
                cp.wait_send()
                cp.wait_recv()

    out_shape = tuple(pltpu.HBM(a.shape, a.dtype) for a in cs) + tuple(pltpu.HBM(a.shape, a.dtype) for a in lands)
    res = _pcall(body, name=name, out_shape=out_shape, in_specs=[HBM] * (2 * n) + [SEM, SEM, ANY], out_specs=(HBM,) * (2 * n),
                 input_output_aliases={a: a for a in range(2 * n)},
                 compiler_params=pltpu.CompilerParams(has_side_effects=DATAFLOW))(*cs, *lands, send, recv, after)
    return res[:n], res[n:]


def _pair_start(name, gs, lands):
    n = len(gs)

    def body(*refs):
        ins, lnd = refs[:n], refs[n:2 * n]
        send, recv = refs[2 * n], refs[2 * n + 1]
        token = refs[-1]
        x, y, c, _ = _place()
        for a in range(n):
            for q in range(N_CHIP):
                pltpu.make_async_remote_copy(src_ref=ins[a].at[q, 1 - c], dst_ref=lnd[a].at[q], send_sem=send.at[a * N_CHIP + q],
                                             recv_sem=recv.at[a * N_CHIP + q], device_id=(x, y, 1 - c), device_id_type=MESH).start()
        token[...] = jnp.zeros_like(token)

    out_shape = ((pltpu.SemaphoreType.DMA((n * N_CHIP,)), pltpu.SemaphoreType.DMA((n * N_CHIP,)))
                 + tuple(pltpu.HBM(a.shape, a.dtype) for a in gs) + tuple(pltpu.HBM(a.shape, a.dtype) for a in lands)
                 + (jax.ShapeDtypeStruct((8, LANES), F32),))
    res = _pcall(body, name=name, out_shape=out_shape, in_specs=[HBM] * (2 * n),
                 out_specs=(SEM, SEM) + (HBM,) * (2 * n) + (pl.BlockSpec(memory_space=pltpu.VMEM),),
                 input_output_aliases={a: 2 + a for a in range(2 * n)},
                 compiler_params=pltpu.CompilerParams(has_side_effects=DATAFLOW))(*[_hbm(a) for a in gs], *[_hbm(a) for a in lands])
    return res[0], res[1], res[2:2 + n], res[2 + n:2 + 2 * n], res[-1]


def _pair_wait(name, send, recv, gs, lands, after):
    n = len(gs)

    def body(*refs):
        ins, lnd = refs[:n], refs[n:2 * n]
        send, recv = refs[2 * n], refs[2 * n + 1]
        x, y, c, _ = _place()
        for a in range(n):
            for q in range(N_CHIP):
                cp = pltpu.make_async_remote_copy(src_ref=ins[a].at[q, 1 - c], dst_ref=lnd[a].at[q], send_sem=send.at[a * N_CHIP + q],
                                                  recv_sem=recv.at[a * N_CHIP + q], device_id=(x, y, 1 - c), device_id_type=MESH)
                cp.wait_send()
                cp.wait_recv()

    out_shape = tuple(pltpu.HBM(a.shape, a.dtype) for a in gs) + tuple(pltpu.HBM(a.shape, a.dtype) for a in lands)
    res = _pcall(body, name=name, out_shape=out_shape, in_specs=[HBM] * (2 * n) + [SEM, SEM, ANY], out_specs=(HBM,) * (2 * n),
                 input_output_aliases={a: a for a in range(2 * n)},
                 compiler_params=pltpu.CompilerParams(has_side_effects=DATAFLOW))(*gs, *lands, send, recv, after)
    return res[:n], res[n:]


def _pair_exchange(name, gs):
    n = len(gs)

    def body(*refs):
        ins, got = refs[:n], refs[n:2 * n]
        send, recv = refs[2 * n:]
        x, y, c, _ = _place()
        sib = (x, y, 1 - c)
        rem = [pltpu.make_async_remote_copy(src_ref=ins[a].at[q, 1 - c], dst_ref=got[a].at[q], send_sem=send.at[a, q],
                                            recv_sem=recv.at[a, q], device_id=sib, device_id_type=MESH)
               for a in range(n) for q in range(N_CHIP)]
        for d in rem:
            d.start()
        for d in rem:
            d.wait()

    half = [jax.ShapeDtypeStruct((N_CHIP,) + g.shape[2:], g.dtype) for g in gs]
    return _pcall(body, name=name, in_specs=[ANY] * n, out_specs=[ANY] * n, out_shape=half,
                  scratch_shapes=[pltpu.SemaphoreType.DMA((n, N_CHIP)), pltpu.SemaphoreType.DMA((n, N_CHIP))],
                  compiler_params=pltpu.CompilerParams(has_side_effects=True))(*gs)


def _chip_exchange(cs, sm_sh, sm_rep):
    n = len(cs)

    def body(*refs):
        ins, sh, rep = refs[:n], refs[n], refs[n + 1]
        outs, osh, orep = refs[n + 2:2 * n + 2], refs[2 * n + 2], refs[2 * n + 3]
        send, recv, lsem, ssend, srecv, slsem = refs[2 * n + 4:]
        x, y, c, chips = _place()
        qme = 2 * x + y
        me = 4 * x + 2 * y + c
        loc = [pltpu.make_async_copy(ins[a].at[qme], outs[a].at[qme], lsem.at[a]) for a in range(n)]
        loc.append(pltpu.make_async_copy(sh.at[me], osh.at[me], slsem.at[0]))
        loc.append(pltpu.make_async_copy(rep, orep.at[me], slsem.at[1]))
        rem, waits = [], []
        for j, chip in enumerate(chips):
            q = 2 * chip[0] + chip[1]
            to = (chip[0], chip[1], c)
            for a in range(n):
                rem.append(pltpu.make_async_remote_copy(src_ref=ins[a].at[q], dst_ref=outs[a].at[qme], send_sem=send.at[a, j],
                                                        recv_sem=recv.at[a, j], device_id=to, device_id_type=MESH))
                waits.append(pltpu.make_async_remote_copy(src_ref=ins[a].at[q], dst_ref=outs[a].at[q], send_sem=send.at[a, j],
                                                          recv_sem=recv.at[a, j], device_id=to, device_id_type=MESH))
        for r in range(1, N_DEV):
            fx, fy, fc = (r >> 2) & 1, (r >> 1) & 1, r & 1
            px = 1 - x if fx else x
            py = 1 - y if fy else y
            pc = 1 - c if fc else c
            t = 4 * px + 2 * py + pc
            to = (px, py, pc)
            rem.append(pltpu.make_async_remote_copy(src_ref=sh.at[t], dst_ref=osh.at[me], send_sem=ssend.at[0, r - 1],
                                                    recv_sem=srecv.at[0, r - 1], device_id=to, device_id_type=MESH))
            waits.append(pltpu.make_async_remote_copy(src_ref=sh.at[t], dst_ref=osh.at[t], send_sem=ssend.at[0, r - 1],
                                                      recv_sem=srecv.at[0, r - 1], device_id=to, device_id_type=MESH))
            rem.append(pltpu.make_async_remote_copy(src_ref=rep, dst_ref=orep.at[me], send_sem=ssend.at[1, r - 1],
                                                    recv_sem=srecv.at[1, r - 1], device_id=to, device_id_type=MESH))
            waits.append(pltpu.make_async_remote_copy(src_ref=rep, dst_ref=orep.at[t], send_sem=ssend.at[1, r - 1],
                                                      recv_sem=srecv.at[1, r - 1], device_id=to, device_id_type=MESH))
        for d in loc + rem:
            d.start()
        for d in waits:
            d.wait_recv()
        for d in rem:
            d.wait_send()
        for d in loc:
            d.wait()

    out_shape = [jax.ShapeDtypeStruct(a.shape, a.dtype) for a in cs]
    out_shape += [jax.ShapeDtypeStruct(sm_sh.shape, F32), jax.ShapeDtypeStruct((N_DEV,) + sm_rep.shape, F32)]
    res = _pcall(body, name="chip_exchange", in_specs=[ANY] * (n + 2), out_specs=[ANY] * (n + 2), out_shape=out_shape,
                 scratch_shapes=[pltpu.SemaphoreType.DMA((n, 3)), pltpu.SemaphoreType.DMA((n, 3)), pltpu.SemaphoreType.DMA((n,)),
                                 pltpu.SemaphoreType.DMA((2, 7)), pltpu.SemaphoreType.DMA((2, 7)), pltpu.SemaphoreType.DMA((2,))],
                 compiler_params=pltpu.CompilerParams(has_side_effects=True))(*cs, sm_sh, sm_rep)
    return res[:n], res[n], res[n + 1]


BIG = ("mlp_up", "mlp_down", "conv_w_in", "conv_w_out", "pool_w", "sc_w_in", "sc_w_out", "ret_w_in", "ret_w_out")
SMALL_SPLIT = ("norm_g", "conv_dw", "sc_dw")
SMALL_REP = ("conv_b_in", "conv_dw_b", "conv_ln_g", "conv_ln_b", "conv_b_out", "pool_scale")
WEIGHTS = ("norm_g", "mlp_up", "mlp_down", "conv_w_in", "conv_b_in", "conv_dw", "conv_dw_b", "conv_ln_g", "conv_ln_b",
           "conv_w_out", "conv_b_out", "pool_w", "pool_scale", "sc_w_in", "sc_dw", "sc_w_out", "ret_w_in", "ret_w_out")


def _pad_rows(a, rows):
    return jnp.pad(a, ((0, rows - a.shape[0]), (0, 0)))


def _split_rows(names, shards):
    out = []
    for nme in names:
        r = int(np.prod(shards[nme].shape[:-1]))
        out.append((nme, r, -(-r // 8) * 8))
    return out


def _pack_split(names, arrs):
    parts = []
    for nme, r, rp in _split_rows(names, arrs):
        parts.append(_pad_rows(arrs[nme].reshape(r, arrs[nme].shape[-1]), rp))
    return jnp.concatenate(parts, axis=0)


def _pack_rep(arrs, D):
    parts = [arrs[nme].reshape(-1, D) for nme in SMALL_REP]
    rows = sum(p.shape[0] for p in parts)
    return _pad_rows(jnp.concatenate(parts, axis=0), -(-rows // 8) * 8)


def kernel(x, positions, norm_g, mlp_up, mlp_down, conv_w_in, conv_b_in, conv_dw, conv_dw_b, conv_ln_g, conv_ln_b, conv_w_out, conv_b_out, pool_w, pool_scale, sc_w_in, sc_dw, sc_w_out, ret_w_in, ret_w_out, loss_target, m_norm_g, m_mlp_up, m_mlp_down, m_conv_w_in, m_conv_b_in, m_conv_dw, m_conv_dw_b, m_conv_ln_g, m_conv_ln_b, m_conv_w_out, m_conv_b_out, m_pool_w, m_pool_scale, m_sc_w_in, m_sc_dw, m_sc_w_out, m_ret_w_in, m_ret_w_out, v_norm_g, v_mlp_up, v_mlp_down, v_conv_w_in, v_conv_b_in, v_conv_dw, v_conv_dw_b, v_conv_ln_g, v_conv_ln_b, v_conv_w_out, v_conv_b_out, v_pool_w, v_pool_scale, v_sc_w_in, v_sc_dw, v_sc_w_out, v_ret_w_in, v_ret_w_out):
    w = dict(norm_g=norm_g, mlp_up=mlp_up, mlp_down=mlp_down, conv_w_in=conv_w_in, conv_b_in=conv_b_in, conv_dw=conv_dw,
             conv_dw_b=conv_dw_b, conv_ln_g=conv_ln_g, conv_ln_b=conv_ln_b, conv_w_out=conv_w_out, conv_b_out=conv_b_out,
             pool_w=pool_w, pool_scale=pool_scale, sc_w_in=sc_w_in, sc_dw=sc_dw, sc_w_out=sc_w_out, ret_w_in=ret_w_in, ret_w_out=ret_w_out)
    m = dict(norm_g=m_norm_g, mlp_up=m_mlp_up, mlp_down=m_mlp_down, conv_w_in=m_conv_w_in, conv_b_in=m_conv_b_in, conv_dw=m_conv_dw,
             conv_dw_b=m_conv_dw_b, conv_ln_g=m_conv_ln_g, conv_ln_b=m_conv_ln_b, conv_w_out=m_conv_w_out, conv_b_out=m_conv_b_out,
             pool_w=m_pool_w, pool_scale=m_pool_scale, sc_w_in=m_sc_w_in, sc_dw=m_sc_dw, sc_w_out=m_sc_w_out, ret_w_in=m_ret_w_in,
             ret_w_out=m_ret_w_out)
    v = dict(norm_g=v_norm_g, mlp_up=v_mlp_up, mlp_down=v_mlp_down, conv_w_in=v_conv_w_in, conv_b_in=v_conv_b_in, conv_dw=v_conv_dw,
             conv_dw_b=v_conv_dw_b, conv_ln_g=v_conv_ln_g, conv_ln_b=v_conv_ln_b, conv_w_out=v_conv_w_out, conv_b_out=v_conv_b_out,
             pool_w=v_pool_w, pool_scale=v_pool_scale, sc_w_in=v_sc_w_in, sc_dw=v_sc_dw, sc_w_out=v_sc_w_out, ret_w_in=v_ret_w_in,
             ret_w_out=v_ret_w_out)
    T, D = x.shape[1], x.shape[2]
    L = D // N_DEV
    G = len(POOL_WINDOWS)
    cg = D // G

    me = 4 * lax.axis_index("x") + 2 * lax.axis_index("y") + lax.axis_index("c")
    MIX = (("conv_w_in", "conv_w_out"), ("pool_w",), ("sc_w_in", "sc_w_out"), ("ret_w_in", "ret_w_out"))

    def shards(l, part):
        out = {k: w[k].astype(BF16) for k in MIX[l]} if part != "mlp" else {}
        if part != "mix":
            out["mlp_up"], out["mlp_down"] = w["mlp_up"][l].astype(BF16), w["mlp_down"][l].astype(BF16)
        return out

    sh0 = shards(0, "mix")
    gath0 = _gather_all(list(sh0.values()) + [_pack_split(SMALL_SPLIT, w)])
    small = gath0[-1]
    common = {k: w[k].reshape(1, -1) for k in SMALL_REP}
    off = 0
    for nme, r, rp in _split_rows(SMALL_SPLIT, w):
        common[nme] = small[:, off:off + r, :].transpose(1, 0, 2).reshape(r, D)
        off += rp
    flying = {}
    dep = small
    for l, part in [(0, "mlp")] + [(l, "both") for l in range(1, DEPTH)]:
        shl = shards(l, part)
        lands = [lax.dynamic_update_slice(lax.empty((N_DEV,) + a.shape, a.dtype), a[None], (me,) + (0,) * a.ndim) for a in shl.values()]
        send, recv, thru, lands, dep = _gather_start("gather_start_%d" % l, list(shl.values()), lands, [dep])
        flying[l] = dict(names=list(shl.keys()), send=send, recv=recv, thru=thru, lands=lands)

    def pass_on(l, after):
        f = flying[l]
        f["fsend"], f["frecv"], f["thru"], f["lands"], tok = _gather_pass("gather_pass_%d" % l, f["send"], f["recv"], f["thru"],
                                                                          f["lands"], after)
        return tok

    def arrive(l, after):
        f = flying[l]
        got = _gather_wait("gather_wait_%d" % l, f["send"], f["recv"], f["fsend"], f["frecv"], f["thru"], f["lands"], after)
        return dict(zip(f["names"], got))

    def get_w(l, part, after):
        if part == "mlp" and l > 0:
            return {"_deps": [pass_on(l + 1, after)]} if l + 1 < DEPTH else {}
        if l == 0 and part == "mix":
            gw = dict(zip(sh0.keys(), gath0[:-1]))
        elif l == 0:
            gw = arrive(0, pass_on(0, after))
            gw["_deps"] = [pass_on(1, gw["mlp_up"])]
        else:
            gw = arrive(l, after)
        out = dict(common)
        for k, a in gw.items():
            if k.endswith("_w_out"):
                out[k] = a.reshape(-1, D)
            elif k == "pool_w":
                out[k] = a.transpose(1, 0, 2, 3).reshape(G, cg, cg)
            else:
                out[k] = a
        return out

    res = {}
    mlp_res = {k: [lax.empty(w[k].shape, F32) for _ in range(4)] for k in ("mlp_up", "mlp_down")}
    pending = []
    small_g = {}

    def shard2d(src, k, l):
        a = src[k][l] if k in mlp_res else src[k]
        return a.reshape(-1, a.shape[-1])

    def adam_layer(l, names, parts, own):
        for i, k in enumerate(names):
            mine = None if own is None else own[i]
            if k in mlp_res:
                mlp_res[k] = _adamw("adamw_%s_%d" % (k, l), parts[i], w[k], m[k], v[k], mine, layer=l, bufs=mlp_res[k])
            else:
                out = _adamw("adamw_%s_%d" % (k, l), parts[i], shard2d(w, k, l), shard2d(m, k, l), shard2d(v, k, l), mine)
                res[k] = [o.reshape(w[k].shape) for o in out]

    held = {}

    def put_g(l, part, g, after, norm_grads):
        small_g.update({k: g[k] for k in g if k in SMALL_SPLIT or k in SMALL_REP})
        def stack(k):
            a = g[k]
            if k == "pool_w":
                a = a.reshape(G, N_DEV, cg // N_DEV, cg).transpose(1, 0, 2, 3).astype(BF16)
            return a.reshape(N_CHIP, 2, -1, w[k].shape[-1])

        if part == "mlp" and l > 0:
            st = [stack("mlp_up"), stack("mlp_down")]
            send, recv, st, lands, tok = _pair_start("pair_start_%d" % l, st, [lax.empty((N_CHIP,) + a.shape[2:], a.dtype) for a in st])
            held.update(send=send, recv=recv, st=st, lands=lands)
            return tok
        if pending:
            tag, lp, names, send, recv, sums, lands = pending.pop()
            sums, parts = _chip_wait("chip_wait_" + tag, send, recv, sums, lands, after)
            adam_layer(lp, names, parts, sums)
        last = l == 0 and part == "mix"
        names = list(MIX[l]) if (l > 0 or last) else ["mlp_up", "mlp_down"]
        tag = "%d%s" % (l, "" if l > 0 else part)
        stacks = [stack(k) for k in names]
        got = _pair_exchange("pair_exchange_" + tag, stacks)
        if l > 0:
            st, got_mlp = _pair_wait("pair_wait_%d" % l, held["send"], held["recv"], held["st"], held["lands"], got[0])
            names, stacks, got = names + ["mlp_up", "mlp_down"], stacks + list(st), list(got) + list(got_mlp)
            held.clear()
        sums = [_pair_add("pair_add", a, b) for a, b in zip(stacks, got)]
        if not last:
            send, recv, sums, lands, tok = _chip_start("chip_start_" + tag, sums, [lax.empty(a.shape, a.dtype) for a in sums])
            pending.append((tag, l, names, send, recv, sums, lands))
            return tok
        small_g["norm_g"] = norm_grads
        gsm = []
        for nme, r, rp in _split_rows(SMALL_SPLIT, w):
            gsm.append(jnp.pad(small_g[nme].reshape(r, N_DEV, L), ((0, rp - r), (0, 0), (0, 0))))
        sm_sh = jnp.concatenate(gsm, axis=0).transpose(1, 0, 2)
        parts, r_sh, r_rep = _chip_exchange(sums, sm_sh, _pack_rep(small_g, D))
        adam_layer(0, names, parts, None)
        out = _adamw("adamw_split", r_sh, _pack_split(SMALL_SPLIT, w), _pack_split(SMALL_SPLIT, m), _pack_split(SMALL_SPLIT, v))
        off = 0
        for nme, r, rp in _split_rows(SMALL_SPLIT, w):
            res[nme] = [o[off:off + r].reshape(w[nme].shape) for o in out]
            off += rp
        out = _adamw("adamw_rep", r_rep, _pack_rep(w, D), _pack_rep(m, D), _pack_rep(v, D))
        off = 0
        for nme in SMALL_REP:
            r = w[nme].shape[0] // D
            res[nme] = [o[off:off + r].reshape(w[nme].shape) for o in out]
            off += r
        return None

    pos = positions.reshape(T, 1).astype(F32)
    loss_row, grad_x = _local_step(x.reshape(T, D), pos, loss_target.reshape(T, D), common["norm_g"], get_w, put_g, [dep], T, D)
    loss = lax.psum(0.5 * jnp.sum(loss_row) / D, AXES)
    res.update(mlp_res)
    outs = [loss, grad_x.reshape(x.shape)]
    for i in range(4):
        outs += [res[k][i] for k in WEIGHTS]
    return tuple(outs)
```

```python
import functools
import math

import numpy as np
import jax
import jax.numpy as jnp
from jax import lax
from jax.experimental import pallas as pl
from jax.experimental.pallas import tpu as pltpu

F32, BF16 = jnp.float32, jnp.bfloat16
MESH = pl.DeviceIdType.MESH
AXES = ("x", "y", "c")
N_DEV = 8
N_CHIP = 4
EPS = 1e-6
DEPTH = 4
CONV_WIDTH = 31
POOL_WINDOWS = (2, 4, 8, 16)
SHORT_CONV_WIDTH = 3
RET_HEADS = 4
RET_CHUNK = 128
ROPE_BASE = 10000.0
ADAM_LR, ADAM_B1, ADAM_B2, ADAM_EPS, ADAM_WD, ADAM_STEP = 0.001, 0.9, 0.999, 1e-08, 0.01, 10
LANES = 128
SEQ_PAD = 32
VMEM_LIMIT = 56 * 1024 * 1024

NN = (((1,), (0,)), ((), ()))
NT = (((1,), (1,)), ((), ()))
TN = (((0,), (0,)), ((), ()))


def _pcall(body, **kw):
    return pl.pallas_call(body, **kw)


def _params(sem):
    return pltpu.CompilerParams(dimension_semantics=sem, vmem_limit_bytes=VMEM_LIMIT)


def _tile(n, pref):
    t = min(n, pref)
    while n % t:
        t //= 2
    return t


def _rowwise(name, fn, ins, outs, rows, tr, deps=()):
    n_in = len(ins)
    n_dep = len(deps)
    in_specs = []
    for arr, nc, cb in ins:
        if nc is None:
            in_specs.append(pl.BlockSpec(arr.shape, lambda i, nd=arr.ndim: (0,) * nd))
        else:
            in_specs.append(pl.BlockSpec((tr, nc), lambda i, cb=cb: (i, cb)))
    out_specs, out_shapes = [], []
    for o in outs:
        if o[0] == "row":
            out_shapes.append(jax.ShapeDtypeStruct((rows, o[1]), o[2]))
            out_specs.append(pl.BlockSpec((tr, o[1]), lambda i: (i, 0)))
        else:
            out_shapes.append(jax.ShapeDtypeStruct(o[1], F32))
            out_specs.append(pl.BlockSpec(o[1], lambda i: (0, 0)))

    in_specs = in_specs + [pl.BlockSpec(memory_space=pl.ANY)] * n_dep

    def body(*refs):
        res = fn(*[r[...] for r in refs[:n_in]])
        if not isinstance(res, (tuple, list)):
            res = (res,)
        i = pl.program_id(0)
        for o, ref, r in zip(outs, refs[n_in + n_dep:], res):
            if o[0] == "row":
                ref[...] = r.astype(o[2])
            else:
                @pl.when(i == 0)
                def _(ref=ref, r=r):
                    ref[...] = r

                @pl.when(i > 0)
                def _(ref=ref, r=r):
                    ref[...] += r

    return _pcall(body, name=name, grid=(rows // tr,), in_specs=in_specs, out_specs=out_specs,
                  out_shape=out_shapes, compiler_params=_params(("arbitrary",)))(*[a for a, _, _ in ins], *deps)


def _rms(x, g):
    return x * lax.rsqrt(jnp.mean(x * x, axis=-1, keepdims=True) + EPS) * g


def _colsum(x):
    return jnp.sum(x, axis=0, keepdims=True)


def _square(r):
    rf = r.astype(F32)
    return (rf * rf).astype(r.dtype)


def _ln_silu(c, g, b):
    mu = jnp.mean(c, axis=-1, keepdims=True)
    xc = c - mu
    y = xc * lax.rsqrt(jnp.mean(xc * xc, axis=-1, keepdims=True) + EPS) * g + b
    return y * jax.nn.sigmoid(y)


def _glu(za, zg):
    return za * jax.nn.sigmoid(zg)


def _gate_norm(o, g):
    dv = o.shape[-1] // RET_HEADS
    parts = []
    for h in range(RET_HEADS):
        oh = o[:, h * dv:(h + 1) * dv]
        parts.append(oh * lax.rsqrt(jnp.mean(oh * oh, axis=-1, keepdims=True) + EPS))
    return (g * jax.nn.sigmoid(g)) * jnp.concatenate(parts, axis=-1)


def _rot(x, cos, sin, sign):
    dk = x.shape[-1] // RET_HEADS
    half = dk // 2
    parts = []
    for h in range(RET_HEADS):
        x1 = x[:, h * dk:h * dk + half]
        x2 = x[:, h * dk + half:(h + 1) * dk]
        parts.append(x1 * cos - sign * (x2 * sin))
        parts.append(sign * (x1 * sin) + x2 * cos)
    return jnp.concatenate(parts, axis=-1)


def _mm(name, mode, grid, ins, outs, epi=None, aliases=None, pre=None, deps=()):
    dn = {"nn": NN, "nt": NT, "tn": TN}[mode]
    gk = grid[2]
    n_in, n_out = len(ins), len(outs)
    n_dep = len(deps)

    def body(*refs):
        out_refs = refs[n_in + n_dep:n_in + n_dep + n_out]
        a = refs[0][...] if pre is None else pre(refs[0][...])
        part = lax.dot_general(a, refs[1][...], dn, preferred_element_type=F32)

        def finish(acc):
            res = epi(acc, *[r[...] for r in refs[2:n_in]]) if epi is not None else (acc,)
            for o, r in zip(out_refs, res):
                o[...] = r.astype(o.dtype)

        if gk == 1:
            finish(part)
        else:
            acc_ref = refs[-1]
            k = pl.program_id(2)

            @pl.when(k == 0)
            def _():
                acc_ref[...] = part

            @pl.when(k > 0)
            def _():
                acc_ref[...] += part

            @pl.when(k == gk - 1)
            def _():
                finish(acc_ref[...])

    scratch = []
    if gk > 1:
        blk = [d for d in outs[0][2].block_shape if d is not None]
        scratch = [pltpu.VMEM(tuple(blk), F32)]
    kw = {}
    if aliases:
        kw["input_output_aliases"] = aliases
    return _pcall(body, name=name, grid=grid, in_specs=[s for _, s in ins] + [pl.BlockSpec(memory_space=pl.ANY)] * n_dep,
                  out_specs=[s for _, _, s in outs],
                  out_shape=[jax.ShapeDtypeStruct(sh, dt) for sh, dt, _ in outs], scratch_shapes=scratch,
                  compiler_params=_params(("parallel", "parallel", "arbitrary")), **kw)(*[a for a, _ in ins], *deps)


def _spec(block, fn):
    return pl.BlockSpec(block, fn)


def _mm_cols(name, a, wg, lead, T, epi=None, extras=(), n_out=1, out_dtype=F32, deps=(), rows=2048):
    K, nb = wg.shape[-2], wg.shape[-1]
    tm = _tile(T, rows)
    wblock = (None,) * (1 + len(lead)) + (K, nb)
    ins = [(a, _spec((tm, K), lambda i, j, k: (i, 0))), (wg, _spec(wblock, lambda i, j, k: (j,) + tuple(lead) + (0, 0)))]
    for e in extras:
        if e.shape[0] == 1:
            ins.append((e, _spec((1, nb), lambda i, j, k: (0, j))))
        else:
            ins.append((e, _spec((tm, nb), lambda i, j, k: (i, j))))
    outs = [((T, N_DEV * nb), out_dtype, _spec((tm, nb), lambda i, j, k: (i, j))) for _ in range(n_out)]
    return _mm(name, "nn", (T // tm, N_DEV, 1), ins, outs, epi, deps=deps)


def _mm_cols_grad(name, a, d, wg_shape, lead, T):
    K, nb = wg_shape[-2], wg_shape[-1]
    tk = T
    wblock = (None,) * (1 + len(lead)) + (K, nb)
    ins = [(a, _spec((tk, K), lambda i, j, k: (k, 0))), (d, _spec((tk, nb), lambda i, j, k: (k, j)))]
    outs = [(tuple(wg_shape), BF16, _spec(wblock, lambda i, j, k: (j,) + tuple(lead) + (0, 0)))]
    return _mm(name, "tn", (1, N_DEV, T // tk), ins, outs, lambda acc: (acc,))[0]


def _mm_whole_k(name, a, w3, T, split, pre=None, out_dtype=F32):
    nblk, d1, d2 = w3.shape
    kb = d1 if split == "rows" else d2
    N = d2 if split == "rows" else d1
    tm = _tile(T, 512)

    def body(a_ref, w_ref, o_ref, *scratch):
        av = a_ref[...] if pre is None else pre(a_ref[...])
        if split == "rows":
            o_ref[...] = lax.dot_general(av, w_ref[...].reshape(nblk * kb, N), NN, preferred_element_type=F32).astype(out_dtype)
        else:
            wcat, sem = scratch

            @pl.when(pl.program_id(0) == 0)
            def _():
                cps = [pltpu.make_async_copy(w_ref.at[j], wcat.at[:, pl.ds(j * kb, kb)], sem.at[j]) for j in range(nblk)]
                for cp in cps:
                    cp.start()
                for cp in cps:
                    cp.wait()

            o_ref[...] = lax.dot_general(av, wcat[...], NT, preferred_element_type=F32).astype(out_dtype)

    rows = split == "rows"
    scratch = [] if rows else [pltpu.VMEM((N, nblk * kb), w3.dtype), pltpu.SemaphoreType.DMA((nblk,))]
    return _pcall(body, name=name, grid=(T // tm,),
                  in_specs=[pl.BlockSpec((tm, nblk * kb), lambda i: (i, 0)),
                            pl.BlockSpec(w3.shape, lambda i: (0, 0, 0)) if rows else pl.BlockSpec(memory_space=pl.ANY)],
                  out_specs=pl.BlockSpec((tm, N), lambda i: (i, 0)), out_shape=jax.ShapeDtypeStruct((T, N), out_dtype),
                  scratch_shapes=scratch, compiler_params=_params(("arbitrary",)))(a, w3)


def _mm_rows(name, a, w2, T, epi=None, extras=(), out_dtype=F32):
    Kin, N = w2.shape
    tm, tn = _tile(T, 2048), _tile(N, 512)
    ins = [(a, _spec((tm, Kin), lambda i, j, k: (i, 0))), (w2, _spec((Kin, tn), lambda i, j, k: (0, j)))]
    for e in extras:
        ins.append((e, _spec((1, tn), lambda i, j, k: (0, j))))
    outs = [((T, N), out_dtype, _spec((tm, tn), lambda i, j, k: (i, j)))]
    return _mm(name, "nn", (T // tm, N // tn, 1), ins, outs, epi)[0]


def _mm_rows_t(name, d, w2, T, out_dtype=F32):
    Kin, N = w2.shape
    tm, tn = _tile(T, 2048), _tile(Kin, 512)
    ins = [(d, _spec((tm, N), lambda i, j, k: (i, 0))), (w2, _spec((tn, N), lambda i, j, k: (j, 0)))]
    outs = [((T, Kin), out_dtype, _spec((tm, tn), lambda i, j, k: (i, j)))]
    return _mm(name, "nt", (T // tm, Kin // tn, 1), ins, outs)[0]


def _mm_rows_grad(name, a, d, T):
    Kin, N = a.shape[1], d.shape[1]
    tk, tm, tn = _tile(T, 2048), _tile(Kin, 512), _tile(N, 1024)
    ins = [(a, _spec((tk, tm), lambda i, j, k: (k, i))), (d, _spec((tk, tn), lambda i, j, k: (k, j)))]
    outs = [((Kin, N), BF16, _spec((tm, tn), lambda i, j, k: (i, j)))]
    return _mm(name, "tn", (Kin // tm, N // tn, T // tk), ins, outs, lambda acc: (acc,))[0]


def _dwconv_fwd(name, x, w, bias, T):
    W, C = w.shape
    cb = min(C, LANES)
    tc = _tile(T, 256)

    def body(x_ref, w_ref, b_ref, o_ref, xp_ref):
        xp_ref[pl.ds(0, SEQ_PAD), :] = jnp.zeros((SEQ_PAD, cb), F32)

        def fill(c, _):
            base = pl.multiple_of(c * tc, tc)
            xp_ref[pl.ds(base + SEQ_PAD, tc), :] = x_ref[pl.ds(base, tc), :]
            return 0

        lax.fori_loop(0, T // tc, fill, 0)

        def chunk(c, _):
            base = pl.multiple_of(c * tc, tc)
            acc = jnp.zeros((tc, cb), F32) + b_ref[...]
            for k in range(W):
                acc = acc + w_ref[pl.ds(k, 1), :] * xp_ref[pl.ds(base + SEQ_PAD - (W - 1) + k, tc), :]
            o_ref[pl.ds(base, tc), :] = acc
            return 0

        lax.fori_loop(0, T // tc, chunk, 0)

    return _pcall(body, name=name, grid=(C // cb,),
                  in_specs=[pl.BlockSpec((T, cb), lambda i: (0, i)), pl.BlockSpec((W, cb), lambda i: (0, i)),
                            pl.BlockSpec((1, cb), lambda i: (0, i))],
                  out_specs=pl.BlockSpec((T, cb), lambda i: (0, i)), out_shape=jax.ShapeDtypeStruct((T, C), F32),
                  scratch_shapes=[pltpu.VMEM((T + SEQ_PAD, cb), F32)],
                  compiler_params=_params(("parallel",)))(x, w, bias)


def _dwconv_bwd(name, dc, x, w, T):
    W, C = w.shape
    cb = min(C, LANES)
    tc = _tile(T, 128)

    def body(dc_ref, x_ref, w_ref, dx_ref, dw_ref, xp_ref, dp_ref, acc_ref):
        xp_ref[pl.ds(0, SEQ_PAD), :] = jnp.zeros((SEQ_PAD, cb), F32)
        dp_ref[pl.ds(T, SEQ_PAD), :] = jnp.zeros((SEQ_PAD, cb), F32)
        acc_ref[...] = jnp.zeros((W * 8, cb), F32)

        def fill(c, _):
            base = pl.multiple_of(c * tc, tc)
            xp_ref[pl.ds(base + SEQ_PAD, tc), :] = x_ref[pl.ds(base, tc), :]
            dp_ref[pl.ds(base, tc), :] = dc_ref[pl.ds(base, tc), :]
            return 0

        lax.fori_loop(0, T // tc, fill, 0)

        def chunk(c, _):
            base = pl.multiple_of(c * tc, tc)
            d = dc_ref[pl.ds(base, tc), :]
            acc = jnp.zeros((tc, cb), F32)
            for k in range(W):
                acc = acc + w_ref[pl.ds(k, 1), :] * dp_ref[pl.ds(base + (W - 1) - k, tc), :]
                prod = d * xp_ref[pl.ds(base + SEQ_PAD - (W - 1) + k, tc), :]
                acc_ref[pl.ds(8 * k, 8), :] += jnp.sum(prod.reshape(tc // 8, 8, cb), axis=0)
            dx_ref[pl.ds(base, tc), :] = acc
            return 0

        lax.fori_loop(0, T // tc, chunk, 0)
        for k in range(W):
            dw_ref[pl.ds(k, 1), :] = jnp.sum(acc_ref[pl.ds(8 * k, 8), :], axis=0, keepdims=True)

    return _pcall(body, name=name, grid=(C // cb,),
                  in_specs=[pl.BlockSpec((T, cb), lambda i: (0, i)), pl.BlockSpec((T, cb), lambda i: (0, i)),
                            pl.BlockSpec((W, cb), lambda i: (0, i))],
                  out_specs=[pl.BlockSpec((T, cb), lambda i: (0, i)), pl.BlockSpec((W, cb), lambda i: (0, i))],
                  out_shape=[jax.ShapeDtypeStruct((T, C), F32), jax.ShapeDtypeStruct((W, C), F32)],
                  scratch_shapes=[pltpu.VMEM((T + SEQ_PAD, cb), F32), pltpu.VMEM((T + SEQ_PAD, cb), F32),
                                  pltpu.VMEM((W * 8, cb), F32)],
                  compiler_params=_params(("parallel",)))(dc, x, w)


def _pool_count(base, tc, cg, win):
    t = (lax.broadcasted_iota(jnp.int32, (tc, cg), 0) + base + 1).astype(F32)
    return jnp.minimum(t, float(win))


def _pool_fwd(name, u, T):
    D = u.shape[1]
    G = len(POOL_WINDOWS)
    cg = min(D // G, LANES)
    nb = (D // G) // cg
    tc = _tile(T, 128)

    def body(u_ref, o_ref, up_ref):
        g = pl.program_id(0)
        up_ref[pl.ds(0, SEQ_PAD), :] = jnp.zeros((SEQ_PAD, cg), F32)

        def fill(c, _):
            base = pl.multiple_of(c * tc, tc)
            up_ref[pl.ds(base + SEQ_PAD, tc), :] = u_ref[pl.ds(base, tc), :]
            return 0

        lax.fori_loop(0, T // tc, fill, 0)
        for gi, win in enumerate(POOL_WINDOWS):
            @pl.when(g == gi)
            def _(win=win):
                def chunk(c, _):
                    base = pl.multiple_of(c * tc, tc)
                    acc = up_ref[pl.ds(base + SEQ_PAD, tc), :]
                    for j in range(1, win):
                        acc = acc + up_ref[pl.ds(base + SEQ_PAD - j, tc), :]
                    mixed = acc / _pool_count(base, tc, cg, win) - u_ref[pl.ds(base, tc), :]
                    o_ref[pl.ds(base, tc), :] = mixed.astype(BF16)
                    return 0

                lax.fori_loop(0, T // tc, chunk, 0)

    return _pcall(body, name=name, grid=(G, nb), in_specs=[pl.BlockSpec((T, cg), lambda g, i: (0, g * nb + i))],
                  out_specs=pl.BlockSpec((T, cg), lambda g, i: (0, g * nb + i)), out_shape=jax.ShapeDtypeStruct((T, D), BF16),
                  scratch_shapes=[pltpu.VMEM((T + SEQ_PAD, cg), F32)], compiler_params=_params(("parallel", "parallel")))(u)


def _pool_bwd(name, dm, T):
    D = dm.shape[1]
    G = len(POOL_WINDOWS)
    cg = min(D // G, LANES)
    nb = (D // G) // cg
    tc = _tile(T, 128)

    def body(d_ref, o_ref, qp_ref):
        g = pl.program_id(0)
        qp_ref[pl.ds(T, SEQ_PAD), :] = jnp.zeros((SEQ_PAD, cg), F32)
        for gi, win in enumerate(POOL_WINDOWS):
            @pl.when(g == gi)
            def _(win=win):
                def fill(c, _):
                    base = pl.multiple_of(c * tc, tc)
                    qp_ref[pl.ds(base, tc), :] = d_ref[pl.ds(base, tc), :] / _pool_count(base, tc, cg, win)
                    return 0

                lax.fori_loop(0, T // tc, fill, 0)

                def chunk(c, _):
                    base = pl.multiple_of(c * tc, tc)
                    acc = qp_ref[pl.ds(base, tc), :]
                    for j in range(1, win):
                        acc = acc + qp_ref[pl.ds(base + j, tc), :]
                    o_ref[pl.ds(base, tc), :] = (acc - d_ref[pl.ds(base, tc), :]).astype(BF16)
                    return 0

                lax.fori_loop(0, T // tc, chunk, 0)

    return _pcall(body, name=name, grid=(G, nb), in_specs=[pl.BlockSpec((T, cg), lambda g, i: (0, g * nb + i))],
                  out_specs=pl.BlockSpec((T, cg), lambda g, i: (0, g * nb + i)), out_shape=jax.ShapeDtypeStruct((T, D), BF16),
                  scratch_shapes=[pltpu.VMEM((T + SEQ_PAD, cg), F32)], compiler_params=_params(("parallel", "parallel")))(dm)


def _pool_mm(name, mode, a, b, T, scale=None):
    G = len(POOL_WINDOWS)
    D = a.shape[1]
    cg = D // G
    tm = _tile(T, 2048)
    if mode == "tn":
        ins = [(a, _spec((tm, cg), lambda i, j, k: (k, j))), (b, _spec((tm, cg), lambda i, j, k: (k, j)))]
        outs = [((G, cg, cg), F32, _spec((None, cg, cg), lambda i, j, k: (j, 0, 0)))]
        return _mm(name, "tn", (1, G, T // tm), ins, outs, lambda acc: (acc,))[0]
    ins = [(a, _spec((tm, cg), lambda i, j, k: (i, j))), (b, _spec((None, cg, cg), lambda i, j, k: (j, 0, 0)))]
    if scale is not None:
        ins.append((scale, _spec((1, cg), lambda i, j, k: (0, j))))
        outs = [((T, D), BF16, _spec((tm, cg), lambda i, j, k: (i, j))), ((T, D), BF16, _spec((tm, cg), lambda i, j, k: (i, j)))]
        return _mm(name, mode, (T // tm, G, 1), ins, outs, lambda acc, s: (acc * s, acc))
    outs = [((T, D), F32, _spec((tm, cg), lambda i, j, k: (i, j)))]
    return _mm(name, mode, (T // tm, G, 1), ins, outs)[0]


def _sc_mid_fwd(name, z, w, T):
    D = w.shape[1]
    W = w.shape[0]
    cb = min(D, LANES)
    nb = D // cb
    tc = _tile(T, 256)

    def body(b_ref, c_ref, v_ref, w_ref, o_ref, pp_ref):
        pp_ref[pl.ds(0, SEQ_PAD), :] = jnp.zeros((SEQ_PAD, cb), F32)

        def fill(c, _):
            base = pl.multiple_of(c * tc, tc)
            pp_ref[pl.ds(base + SEQ_PAD, tc), :] = c_ref[pl.ds(base, tc), :].astype(F32) * v_ref[pl.ds(base, tc), :].astype(F32)
            return 0

        lax.fori_loop(0, T // tc, fill, 0)

        def chunk(c, _):
            base = pl.multiple_of(c * tc, tc)
            q = jnp.zeros((tc, cb), F32)
            for k in range(W):
                q = q + w_ref[pl.ds(k, 1), :] * pp_ref[pl.ds(base + SEQ_PAD - (W - 1) + k, tc), :]
            o_ref[pl.ds(base, tc), :] = (b_ref[pl.ds(base, tc), :].astype(F32) * q).astype(BF16)
            return 0

        lax.fori_loop(0, T // tc, chunk, 0)

    return _pcall(body, name=name, grid=(nb,),
                  in_specs=[pl.BlockSpec((T, cb), lambda i: (0, i)), pl.BlockSpec((T, cb), lambda i: (0, nb + i)),
                            pl.BlockSpec((T, cb), lambda i: (0, 2 * nb + i)), pl.BlockSpec((W, cb), lambda i: (0, i))],
                  out_specs=pl.BlockSpec((T, cb), lambda i: (0, i)), out_shape=jax.ShapeDtypeStruct((T, D), BF16),
                  scratch_shapes=[pltpu.VMEM((T + SEQ_PAD, cb), F32)],
                  compiler_params=_params(("parallel",)))(z, z, z, w)


def _sc_mid_bwd(name, z, dm, w, T):
    D = w.shape[1]
    W = w.shape[0]
    cb = min(D, LANES)
    nb = D // cb
    tc = _tile(T, 128)

    def body(b_ref, c_ref, v_ref, d_ref, w_ref, dz_ref, dw_ref, pp_ref, dq_ref, acc_ref):
        part = pl.program_id(0)
        pp_ref[pl.ds(0, SEQ_PAD), :] = jnp.zeros((SEQ_PAD, cb), F32)
        dq_ref[pl.ds(T, SEQ_PAD), :] = jnp.zeros((SEQ_PAD, cb), F32)
        acc_ref[...] = jnp.zeros((W * 8, cb), F32)

        def fill(c, _):
            base = pl.multiple_of(c * tc, tc)
            pp_ref[pl.ds(base + SEQ_PAD, tc), :] = c_ref[pl.ds(base, tc), :].astype(F32) * v_ref[pl.ds(base, tc), :].astype(F32)
            dq_ref[pl.ds(base, tc), :] = d_ref[pl.ds(base, tc), :] * b_ref[pl.ds(base, tc), :].astype(F32)
            return 0

        lax.fori_loop(0, T // tc, fill, 0)

        @pl.when(part == 0)
        def _():
            def chunk(c, _):
                base = pl.multiple_of(c * tc, tc)
                dq = dq_ref[pl.ds(base, tc), :]
                q = jnp.zeros((tc, cb), F32)
                for k in range(W):
                    p = pp_ref[pl.ds(base + SEQ_PAD - (W - 1) + k, tc), :]
                    q = q + w_ref[pl.ds(k, 1), :] * p
                    acc_ref[pl.ds(8 * k, 8), :] += jnp.sum((dq * p).reshape(tc // 8, 8, cb), axis=0)
                dz_ref[pl.ds(base, tc), :] = (d_ref[pl.ds(base, tc), :] * q).astype(BF16)
                return 0

            lax.fori_loop(0, T // tc, chunk, 0)

        @pl.when(part > 0)
        def _():
            def chunk(c, _):
                base = pl.multiple_of(c * tc, tc)
                dp = jnp.zeros((tc, cb), F32)
                for k in range(W):
                    dp = dp + w_ref[pl.ds(k, 1), :] * dq_ref[pl.ds(base + (W - 1) - k, tc), :]
                other = jnp.where(part == 1, v_ref[pl.ds(base, tc), :], c_ref[pl.ds(base, tc), :]).astype(F32)
                dz_ref[pl.ds(base, tc), :] = (dp * other).astype(BF16)
                return 0

            lax.fori_loop(0, T // tc, chunk, 0)

        for k in range(W):
            dw_ref[pl.ds(k, 1), :] = jnp.sum(acc_ref[pl.ds(8 * k, 8), :], axis=0, keepdims=True)

    dz, dw3 = _pcall(body, name=name, grid=(3, nb),
                     in_specs=[pl.BlockSpec((T, cb), lambda p, i: (0, i)), pl.BlockSpec((T, cb), lambda p, i: (0, nb + i)),
                               pl.BlockSpec((T, cb), lambda p, i: (0, 2 * nb + i)), pl.BlockSpec((T, cb), lambda p, i: (0, i)),
                               pl.BlockSpec((W, cb), lambda p, i: (0, i))],
                     out_specs=[pl.BlockSpec((T, cb), lambda p, i: (0, p * nb + i)),
                                pl.BlockSpec((None, W, cb), lambda p, i: (p, 0, i))],
                     out_shape=[jax.ShapeDtypeStruct((T, 3 * D), BF16), jax.ShapeDtypeStruct((3, W, D), F32)],
                     scratch_shapes=[pltpu.VMEM((T + SEQ_PAD, cb), F32), pltpu.VMEM((T + SEQ_PAD, cb), F32),
                                     pltpu.VMEM((W * 8, cb), F32)],
                     compiler_params=_params(("parallel", "parallel")))(z, z, z, dm, w)
    return dz, dw3


def _ret_tables(dk, dv):
    C = RET_CHUNK
    lg = np.log1p(-np.exp2(-5.0 - np.arange(RET_HEADS, dtype=np.float64)))
    idx = np.arange(C, dtype=np.float64)
    rel = idx[:, None] - idx[None, :]
    mask = np.where(rel >= 0, np.exp(lg[:, None, None] * np.maximum(rel, 0.0)), 0.0)
    qd = np.exp(lg[:, None] * (idx + 1.0))[:, :, None] * np.ones((1, 1, dk))
    kd = np.exp(lg[:, None] * (C - 1.0 - idx))[:, :, None] * np.ones((1, 1, dk))
    cd = np.exp(lg * C)[:, None, None] * np.ones((1, 1, dv))
    return tuple(jnp.asarray(a, F32) for a in (mask, qd, kd, cd))


def _ret_fwd(name, q, k, v, tabs, T):
    H, C = RET_HEADS, RET_CHUNK
    dk, dv = q.shape[1] // H, v.shape[1] // H
    nch = T // C
    mask, qd, kd, cd = tabs

    def body(q_ref, k_ref, v_ref, m_ref, qd_ref, kd_ref, cd_ref, o_ref, st_ref, s_ref):
        @pl.when(pl.program_id(0) == 0)
        def _():
            s_ref[...] = jnp.zeros((H, dk, dv), F32)

        for h in range(H):
            qv, kv = q_ref[:, h * dk:(h + 1) * dk], k_ref[:, h * dk:(h + 1) * dk]
            vv = v_ref[:, h * dv:(h + 1) * dv]
            s = s_ref[h]
            sb = s.astype(BF16)
            st_ref[h] = sb
            scores = lax.dot_general(qv, kv, NT, preferred_element_type=F32) * m_ref[h]
            intra = lax.dot_general(scores.astype(BF16), vv, NN, preferred_element_type=F32)
            cross = lax.dot_general((qv.astype(F32) * qd_ref[h]).astype(BF16), sb, NN, preferred_element_type=F32)
            o_ref[:, h * dv:(h + 1) * dv] = intra + cross
            upd = lax.dot_general((kv.astype(F32) * kd_ref[h]).astype(BF16), vv, TN, preferred_element_type=F32)
            s_ref[h] = s * cd_ref[h] + upd

    whole = lambda a: pl.BlockSpec(a.shape, lambda n: (0, 0, 0))
    return _pcall(body, name=name, grid=(nch,),
                  in_specs=[pl.BlockSpec((C, H * dk), lambda n: (n, 0)), pl.BlockSpec((C, H * dk), lambda n: (n, 0)),
                            pl.BlockSpec((C, H * dv), lambda n: (n, 0)), whole(mask), whole(qd), whole(kd), whole(cd)],
                  out_specs=[pl.BlockSpec((C, H * dv), lambda n: (n, 0)),
                             pl.BlockSpec((H, None, dk, dv), lambda n: (0, n, 0, 0))],
                  out_shape=[jax.ShapeDtypeStruct((T, H * dv), F32), jax.ShapeDtypeStruct((H, nch, dk, dv), BF16)],
                  scratch_shapes=[pltpu.VMEM((H, dk, dv), F32)],
                  compiler_params=_params(("arbitrary",)))(q, k, v, mask, qd, kd, cd)


def _ret_bwd(name, q, k, v, do, st, tabs, T):
    H, C = RET_HEADS, RET_CHUNK
    dk, dv = q.shape[1] // H, v.shape[1] // H
    nch = T // C
    mask, qd, kd, cd = tabs

    def body(q_ref, k_ref, v_ref, do_ref, st_ref, m_ref, qd_ref, kd_ref, cd_ref, dq_ref, dk_ref, dv_ref, ds_ref):
        @pl.when(pl.program_id(0) == 0)
        def _():
            ds_ref[...] = jnp.zeros((H, dk, dv), F32)

        for h in range(H):
            qv, kv = q_ref[:, h * dk:(h + 1) * dk], k_ref[:, h * dk:(h + 1) * dk]
            vv, dov = v_ref[:, h * dv:(h + 1) * dv], do_ref[:, h * dv:(h + 1) * dv]
            sb = st_ref[h]
            m = m_ref[h]
            ds = ds_ref[h]
            dsb = ds.astype(BF16)
            a = (lax.dot_general(qv, kv, NT, preferred_element_type=F32) * m).astype(BF16)
            dp = (lax.dot_general(dov, vv, NT, preferred_element_type=F32) * m).astype(BF16)
            qa = (qv.astype(F32) * qd_ref[h]).astype(BF16)
            kb = (kv.astype(F32) * kd_ref[h]).astype(BF16)
            dq_ref[:, h * dk:(h + 1) * dk] = (lax.dot_general(dp, kv, NN, preferred_element_type=F32)
                                              + qd_ref[h] * lax.dot_general(dov, sb, NT, preferred_element_type=F32))
            dk_ref[:, h * dk:(h + 1) * dk] = (lax.dot_general(dp, qv, TN, preferred_element_type=F32)
                                              + kd_ref[h] * lax.dot_general(vv, dsb, NT, preferred_element_type=F32))
            dv_ref[:, h * dv:(h + 1) * dv] = (lax.dot_general(a, dov, TN, preferred_element_type=F32)
                                              + lax.dot_general(kb, dsb, NN, preferred_element_type=F32)).astype(BF16)
            ds_ref[h] = ds * cd_ref[h] + lax.dot_general(qa, dov, TN, preferred_element_type=F32)

    rev = lambda n: (nch - 1 - n, 0)
    whole = lambda a: pl.BlockSpec(a.shape, lambda n: (0, 0, 0))
    return _pcall(body, name=name, grid=(nch,),
                  in_specs=[pl.BlockSpec((C, H * dk), rev), pl.BlockSpec((C, H * dk), rev), pl.BlockSpec((C, H * dv), rev),
                            pl.BlockSpec((C, H * dv), rev), pl.BlockSpec((H, None, dk, dv), lambda n: (0, nch - 1 - n, 0, 0)),
                            whole(mask), whole(qd), whole(kd), whole(cd)],
                  out_specs=[pl.BlockSpec((C, H * dk), rev), pl.BlockSpec((C, H * dk), rev), pl.BlockSpec((C, H * dv), rev)],
                  out_shape=[jax.ShapeDtypeStruct((T, H * dk), F32), jax.ShapeDtypeStruct((T, H * dk), F32),
                             jax.ShapeDtypeStruct((T, H * dv), BF16)],
                  scratch_shapes=[pltpu.VMEM((H, dk, dv), F32)],
                  compiler_params=_params(("arbitrary",)))(q, k, v, do, st, mask, qd, kd, cd)


def _local_step(x, pos, tgt, ng, get_w, put_g, first_deps, T, D):
    tr = _tile(T, 512)
    trw = _tile(T, 256)
    row = lambda i: ng[i:i + 1]

    u = _rowwise("rms_first", lambda xv, gv: _rms(xv, gv), [(x, D, 0), (row(0), None, None)], [("row", D, BF16)], T, tr,
                 deps=first_deps)[0]
    h = x
    saved = []
    cos = sin = None
    tabs = None
    for l in range(DEPTH):
        mixer = l % 4
        wts = get_w(l, "mix", h)
        sv = {"h_in": h, "u": u}
        if mixer == 0:
            z = _mm_cols("conv_in", u, wts["conv_w_in"], (), T, lambda acc, b: (acc + b,), (wts["conv_b_in"],), 1, BF16)[0]
            v = _rowwise("conv_glu", lambda za, zg: _glu(za.astype(F32), zg.astype(F32)), [(z, D, 0), (z, D, 1)], [("row", D, F32)], T, tr)[0]
            c = _dwconv_fwd("conv_dw", v, wts["conv_dw"], wts["conv_dw_b"], T)
            s = _rowwise("conv_ln", _ln_silu, [(c, D, 0), (wts["conv_ln_g"], None, None), (wts["conv_ln_b"], None, None)],
                         [("row", D, BF16)], T, tr)[0]
            y = _mm_rows("conv_out", s, wts["conv_w_out"], T, lambda acc, b: (acc + b,), (wts["conv_b_out"],), BF16)
            sv.update(z=z, v=v, c=c, s=s)
        elif mixer == 1:
            mixed = _pool_fwd("pool_fwd", u, T)
            y, ycat = _pool_mm("pool_mm", "nn", mixed, wts["pool_w"], T, wts["pool_scale"])
            sv.update(mixed=mixed, ycat=ycat)
        elif mixer == 2:
            z = _mm_cols("sc_in", u, wts["sc_w_in"], (), T, out_dtype=BF16)[0]
            m = _sc_mid_fwd("sc_mid", z, wts["sc_dw"], T)
            y = _mm_rows("sc_out", m, wts["sc_w_out"], T, out_dtype=BF16)
            sv.update(z=z, m=m)
        else:
            dk = D // RET_HEADS
            half = dk // 2
            inv_freq = jnp.asarray((ROPE_BASE ** (-np.arange(half, dtype=np.float64) / half)).astype(np.float32)).reshape(1, half)

            def _cs(p, f):
                ang = p * f
                return jnp.cos(ang), jnp.sin(ang)

            cos, sin = _rowwise("rope_tab", _cs, [(pos, 1, 0), (inv_freq, None, None)],
                                [("row", half, F32), ("row", half, F32)], T, tr)
            tabs = _ret_tables(dk, 2 * dk)
            z = _mm_cols("ret_in", u, wts["ret_w_in"], (), T, out_dtype=BF16)[0]

            def _pre(zq, zk, zv, cv, sn):
                return _rot(zq.astype(F32), cv, sn, 1.0), _rot(zk.astype(F32), cv, sn, 1.0) * (dk ** -0.5), zv

            qr, kr, vb = _rowwise("ret_pre", _pre, [(z, D, 0), (z, D, 1), (z, 2 * D, 1), (cos, half, 0), (sin, half, 0)],
                                  [("row", D, BF16), ("row", D, BF16), ("row", 2 * D, BF16)], T, trw)
            o, st = _ret_fwd("ret_fwd", qr, kr, vb, tabs, T)
            out = _rowwise("ret_post", lambda ov, gv: _gate_norm(ov, gv.astype(F32)), [(o, 2 * D, 0), (z, 2 * D, 2)], [("row", 2 * D, BF16)], T, trw)[0]
            y = _mm_rows("ret_out", out, wts["ret_w_out"], T, out_dtype=BF16)
            sv.update(z=z, qr=qr, kr=kr, vb=vb, o=o, st=st, out=out)
        sv["y1"] = y
        h, u = _rowwise("bound_a", lambda hv, yv, gp, gn: (lambda hn: (hn, _rms(hn, gn)))(hv + _rms(yv.astype(F32), gp)),
                        [(h, D, 0), (y, D, 0), (row(4 * l + 1), None, None), (row(4 * l + 2), None, None)],
                        [("row", D, F32), ("row", D, BF16)], T, tr)
        sv["h_mid"], sv["u2"] = h, u
        wm = get_w(l, "mlp", h)
        mlp_deps = wm.pop("_deps", [])
        wts = dict(wts, **wm)
        sv["w"] = wts
        r = _mm_cols("mlp_up", u, wts["mlp_up"], (), T, lambda acc: (jnp.maximum(acc, 0.0),), (), 1, BF16, deps=mlp_deps, rows=4096)[0]
        y = _mm_whole_k("mlp_down", r, wts["mlp_down"], T, "rows", pre=_square, out_dtype=BF16)
        sv.update(r=r, y2=y)
        if l + 1 < DEPTH:
            udt = F32 if (l + 1) % 4 == 1 else BF16
            h, u = _rowwise("bound_b", lambda hv, yv, gp, gn: (lambda hn: (hn, _rms(hn, gn)))(hv + _rms(yv.astype(F32), gp)),
                            [(h, D, 0), (y, D, 0), (row(4 * l + 3), None, None), (row(4 * l + 4), None, None)],
                            [("row", D, F32), ("row", D, udt)], T, tr)
        saved.append(sv)

    def _final(hv, yv, gp, tv):
        yf = yv.astype(F32)
        out, vjp = jax.vjp(_rms, yf, gp)
        e = hv + out - tv
        dhv = e * (1.0 / D)
        dy, dgp = vjp(dhv)
        return dhv, dy, _colsum(e * e), dgp

    gn = [None] * (4 * DEPTH)
    dh, dy_last, loss_row, gn[4 * DEPTH - 1] = _rowwise(
        "final", _final, [(h, D, 0), (y, D, 0), (row(4 * DEPTH - 1), None, None), (tgt, D, 0)],
        [("row", D, F32), ("row", D, BF16), ("acc", (1, D)), ("acc", (1, D))], T, tr)

    F = saved[0]["r"].shape[1]
    du_next = None
    tok = None
    for l in reversed(range(DEPTH)):
        sv = saved[l]
        wts = sv["w"]
        g = {}
        deps = [] if tok is None else [tok]
        mixer = l % 4
        if du_next is None:
            dy = dy_last
        else:
            dh, dy, gn[4 * l + 3], gn[4 * l + 4], _ = _bound_bwd("bwd_b", dh, du_next, saved[l + 1]["h_in"], sv["y2"],
                                                              row(4 * l + 3), row(4 * l + 4), T, D, tr, deps)
        F8 = F // N_DEV
        tm = _tile(T, 2048)
        da = _mm("mlp_dhid", "nt", (T // tm, N_DEV, 1),
                 [(dy, _spec((tm, D), lambda i, j, k: (i, 0))),
                  (wts["mlp_down"], _spec((None, F8, D), lambda i, j, k: (j, 0, 0))),
                  (sv["r"], _spec((tm, F8), lambda i, j, k: (i, j)))],
                 [((T, F), BF16, _spec((tm, F8), lambda i, j, k: (i, j)))],
                 lambda acc, rr: (acc * (2.0 * rr.astype(F32)),))[0]
        tk = T
        g["mlp_down"] = _mm("mlp_dwdown", "tn", (1, N_DEV, T // tk),
                            [(sv["r"], _spec((tk, F8), lambda i, j, k: (k, j))), (dy, _spec((tk, D), lambda i, j, k: (k, 0)))],
                            [(wts["mlp_down"].shape, BF16, _spec((None, F8, D), lambda i, j, k: (j, 0, 0)))],
                            lambda acc: (acc,), pre=_square)[0]
        g["mlp_up"] = _mm_cols_grad("mlp_dwup", sv["u2"], da, wts["mlp_up"].shape, (), T)
        du = _mm_whole_k("mlp_du", da, wts["mlp_up"], T, "cols", out_dtype=BF16)
        tok = put_g(l, "mlp", g, du, None)
        g = {}
        dh, dy, gn[4 * l + 1], gn[4 * l + 2], dy_sum = _bound_bwd("bwd_a", dh, du, sv["h_mid"], sv["y1"],
                                                                row(4 * l + 1), row(4 * l + 2), T, D, tr, [] if tok is None else [tok])
        u = sv["u"]
        if mixer == 0:
            g["conv_b_out"] = dy_sum
            ds = _mm_rows_t("conv_ds", dy, wts["conv_w_out"], T)
            g["conv_w_out"] = _mm_rows_grad("conv_dwout", sv["s"], dy, T)

            def _lnb(cv, gv, bv, dsv):
                _, vjp = jax.vjp(_ln_silu, cv, gv, bv)
                dc, dg, db = vjp(dsv)
                return dc, dg, db, _colsum(dc)

            dc, g["conv_ln_g"], g["conv_ln_b"], g["conv_dw_b"] = _rowwise(
                "conv_ln_bwd", _lnb, [(sv["c"], D, 0), (wts["conv_ln_g"], None, None), (wts["conv_ln_b"], None, None), (ds, D, 0)],
                [("row", D, F32), ("acc", (1, D)), ("acc", (1, D)), ("acc", (1, D))], T, tr)
            dv, g["conv_dw"] = _dwconv_bwd("conv_dw_bwd", dc, sv["v"], wts["conv_dw"], T)

            def _glub(za, zg, dvv):
                _, vjp = jax.vjp(_glu, za.astype(F32), zg.astype(F32))
                dza, dzg = vjp(dvv)
                dz = jnp.concatenate([dza, dzg], axis=-1)
                return dz, _colsum(dz)

            dz, g["conv_b_in"] = _rowwise("conv_glu_bwd", _glub, [(sv["z"], D, 0), (sv["z"], D, 1), (dv, D, 0)],
                                          [("row", 2 * D, BF16), ("acc", (1, 2 * D))], T, tr)
            g["conv_w_in"] = _mm_cols_grad("conv_dwin", u, dz, wts["conv_w_in"].shape, (), T)
            du = _mm_whole_k("conv_du", dz, wts["conv_w_in"], T, "cols", out_dtype=BF16)
        elif mixer == 1:
            def _psb(dyv, yc, sc):
                return dyv * sc, _colsum(dyv.astype(F32) * yc.astype(F32))

            dyg, g["pool_scale"] = _rowwise("pool_scale_bwd", _psb, [(dy, D, 0), (sv["ycat"], D, 0), (wts["pool_scale"], None, None)],
                                            [("row", D, BF16), ("acc", (1, D))], T, tr)
            dmix = _pool_mm("pool_dmix", "nt", dyg, wts["pool_w"], T)
            g["pool_w"] = _pool_mm("pool_dw", "tn", sv["mixed"], dyg, T)
            du = _pool_bwd("pool_bwd", dmix, T)
        elif mixer == 2:
            dm = _mm_rows_t("sc_dm", dy, wts["sc_w_out"], T)
            g["sc_w_out"] = _mm_rows_grad("sc_dwout", sv["m"], dy, T)
            dz, dw3 = _sc_mid_bwd("sc_mid_bwd", sv["z"], dm, wts["sc_dw"], T)
            g["sc_dw"] = dw3[0]
            g["sc_w_in"] = _mm_cols_grad("sc_dwin", u, dz, wts["sc_w_in"].shape, (), T)
            du = _mm_whole_k("sc_du", dz, wts["sc_w_in"], T, "cols", out_dtype=BF16)
        else:
            dk = D // RET_HEADS
            half = dk // 2
            dout = _mm_rows_t("ret_dout", dy, wts["ret_w_out"], T, BF16)
            g["ret_w_out"] = _mm_rows_grad("ret_dwout", sv["out"], dy, T)

            def _postb(ov, gv, dv_):
                _, vjp = jax.vjp(_gate_norm, ov, gv.astype(F32))
                return vjp(dv_.astype(F32))

            do, dzg = _rowwise("ret_post_bwd", _postb, [(sv["o"], 2 * D, 0), (sv["z"], 2 * D, 2), (dout, 2 * D, 0)],
                               [("row", 2 * D, BF16), ("row", 2 * D, BF16)], T, trw)
            dqr, dkr, dvb = _ret_bwd("ret_bwd", sv["qr"], sv["kr"], sv["vb"], do, sv["st"], tabs, T)

            def _preb(dq, dk_, dvv, dgv, cv, sn):
                return jnp.concatenate([_rot(dq, cv, sn, -1.0).astype(BF16), (_rot(dk_, cv, sn, -1.0) * (dk ** -0.5)).astype(BF16),
                                        dvv, dgv], axis=-1)

            dz = _rowwise("ret_pre_bwd", _preb, [(dqr, D, 0), (dkr, D, 0), (dvb, 2 * D, 0), (dzg, 2 * D, 0), (cos, half, 0), (sin, half, 0)],
                          [("row", 6 * D, BF16)], T, trw)[0]
            g["ret_w_in"] = _mm_cols_grad("ret_dwin", u, dz, wts["ret_w_in"].shape, (), T)
            du = _mm_whole_k("ret_du", dz, wts["ret_w_in"], T, "cols", out_dtype=BF16)
        du_next = du
        if l > 0:
            tok = put_g(l, "mix", g, du, None)

    def _fb(dhv, duv, xv, gv):
        _, vjp = jax.vjp(_rms, xv, gv)
        dx, dg = vjp(duv.astype(F32))
        return dhv + dx, dg

    grad_x, gn[0] = _rowwise("bwd_first", _fb, [(dh, D, 0), (du_next, D, 0), (x, D, 0), (row(0), None, None)],
                             [("row", D, F32), ("acc", (1, D))], T, tr)
    put_g(0, "mix", g, grad_x, jnp.concatenate(gn, axis=0))
    return loss_row, grad_x


def _bound_bwd(name, dh, du, h_new, y, g_post, g_next, T, D, tr, deps=()):
    def fn(dhv, duv, hn, yv, gp, gnx):
        _, vjp1 = jax.vjp(_rms, hn, gnx)
        d1, dgn = vjp1(duv.astype(F32))
        dht = dhv + d1
        _, vjp2 = jax.vjp(_rms, yv.astype(F32), gp)
        dy, dgp = vjp2(dht)
        return dht, dy, dgp, dgn, _colsum(dy)

    return _rowwise(name, fn, [(dh, D, 0), (du, D, 0), (h_new, D, 0), (y, D, 0), (g_post, None, None), (g_next, None, None)],
                    [("row", D, F32), ("row", D, BF16), ("acc", (1, D)), ("acc", (1, D)), ("acc", (1, D))], T, tr, deps)


def _adamw(name, parts, w, m, v, own=None, layer=None, bufs=None):
    P, R, C = parts.shape
    tr = _tile(R, 512)
    n_s = 1 if own is None else 2

    def body(*refs):
        p_ref = refs[0]
        w_ref, m_ref, v_ref = refs[n_s:n_s + 3]
        g_ref, d_ref, m2_ref, v2_ref = refs[-4:]

        def slot(i):
            if own is None:
                return p_ref[i].astype(F32)
            here = 2 * lax.axis_index("x") + lax.axis_index("y") == i
            return jnp.where(here, refs[1][i], p_ref[i]).astype(F32)

        g = slot(0)
        for i in range(1, P):
            g = g + slot(i)
        wv = w_ref[...]
        m2 = ADAM_B1 * m_ref[...] + (1.0 - ADAM_B1) * g
        v2 = ADAM_B2 * v_ref[...] + (1.0 - ADAM_B2) * (g * g)
        m_hat = m2 / (1.0 - ADAM_B1 ** ADAM_STEP)
        v_hat = v2 / (1.0 - ADAM_B2 ** ADAM_STEP)
        g_ref[...] = g
        d_ref[...] = -ADAM_LR * (m_hat / (jnp.sqrt(v_hat) + ADAM_EPS) + ADAM_WD * wv)
        m2_ref[...] = m2
        v2_ref[...] = v2

    pblk = pl.BlockSpec((P, tr, C), lambda i: (0, i, 0))
    stacks = [parts] if own is None else [parts, own]
    if layer is None:
        blk = pl.BlockSpec((tr, C), lambda i: (i, 0))
        return _pcall(body, name=name, grid=(R // tr,),
                      in_specs=[pblk] * n_s + [blk, blk, blk], out_specs=[blk] * 4,
                      out_shape=[jax.ShapeDtypeStruct((R, C), F32)] * 4, compiler_params=_params(("parallel",)))(*stacks, w, m, v)
    blk = pl.BlockSpec((None, tr, C), lambda i: (layer, i, 0))
    return _pcall(body, name=name, grid=(R // tr,),
                  in_specs=[pblk] * n_s + [blk, blk, blk] + [pl.BlockSpec(memory_space=pl.ANY)] * 4, out_specs=[blk] * 4,
                  out_shape=[jax.ShapeDtypeStruct(w.shape, F32)] * 4, input_output_aliases={n_s + 3 + i: i for i in range(4)},
                  compiler_params=_params(("parallel",)))(*stacks, w, m, v, *bufs)


def _pair_add(name, a, b):
    Q, _, R, C = a.shape
    tr = _tile(R, 1024)

    def body(c_ref, a_ref, b_ref, o_ref):
        o_ref[...] = (a_ref[...].astype(F32) + b_ref[...].astype(F32)).astype(BF16)

    core = lax.axis_index("c").astype(jnp.int32).reshape(1)
    blk = pl.BlockSpec((None, tr, C), lambda q, i, c: (q, i, 0))
    spec = pltpu.PrefetchScalarGridSpec(num_scalar_prefetch=1, grid=(Q, R // tr),
                                        in_specs=[pl.BlockSpec((None, None, tr, C), lambda q, i, c: (q, c[0], i, 0)), blk],
                                        out_specs=blk)
    return _pcall(body, name=name, grid_spec=spec, out_shape=jax.ShapeDtypeStruct((Q, R, C), BF16),
                  compiler_params=_params(("parallel", "parallel")))(core, a, b)


ANY = pl.BlockSpec(memory_space=pl.ANY)


def _place():
    x, y, c = lax.axis_index("x"), lax.axis_index("y"), lax.axis_index("c")
    chips = [(1 - x, y), (x, 1 - y), (1 - x, 1 - y)]
    return x, y, c, chips


def _gather_all(locs):
    n = len(locs)

    def body(*refs):
        ins, outs = refs[:n], refs[n:2 * n]
        send, recv, lsem = refs[2 * n:]
        x, y, c, chips = _place()
        me = 4 * x + 2 * y + c
        sib = (x, y, 1 - c)

        def cp(a, k, slot, to, src=None):
            return pltpu.make_async_remote_copy(src_ref=outs[a].at[slot] if src is None else src, dst_ref=outs[a].at[slot],
                                                send_sem=send.at[a, k], recv_sem=recv.at[a, k], device_id=to, device_id_type=MESH)

        mine = [pltpu.make_async_copy(ins[a], outs[a].at[me], lsem.at[a]) for a in range(n)]
        for d in mine:
            d.start()
        first = []
        for a in range(n):
            first.append(cp(a, 0, me, sib, src=ins[a]))
            for j, chip in enumerate(chips):
                first.append(cp(a, 1 + j, me, (chip[0], chip[1], c), src=ins[a]))
        for d in first:
            d.start()
        passed = []
        for j, chip in enumerate(chips):
            slot = 4 * chip[0] + 2 * chip[1] + c
            for a in range(n):
                cp(a, 1 + j, slot, sib).wait_recv()
                p = cp(a, 4 + j, slot, sib)
                p.start()
                passed.append(p)
        for a in range(n):
            cp(a, 0, 4 * x + 2 * y + (1 - c), sib).wait_recv()
        for j, chip in enumerate(chips):
            for a in range(n):
                cp(a, 4 + j, 4 * chip[0] + 2 * chip[1] + (1 - c), sib).wait_recv()
        for d in first + passed:
            d.wait_send()
        for d in mine:
            d.wait()

    return _pcall(body, name="gather_all", in_specs=[ANY] * n, out_specs=[ANY] * n,
                  out_shape=[jax.ShapeDtypeStruct((N_DEV,) + a.shape, a.dtype) for a in locs],
                  scratch_shapes=[pltpu.SemaphoreType.DMA((n, 7)), pltpu.SemaphoreType.DMA((n, 7)), pltpu.SemaphoreType.DMA((n,))],
                  compiler_params=pltpu.CompilerParams(has_side_effects=True))(*locs)


HBM = pl.BlockSpec(memory_space=pltpu.HBM)
SEM = pl.BlockSpec(memory_space=pltpu.SEMAPHORE)
DATAFLOW = pltpu.SideEffectType.DATAFLOW_SIDE_EFFECTING


def _peers():
    x, y, c = lax.axis_index("x"), lax.axis_index("y"), lax.axis_index("c")
    out = []
    for r in range(1, N_DEV):
        px = 1 - x if (r >> 2) & 1 else x
        py = 1 - y if (r >> 1) & 1 else y
        pc = 1 - c if r & 1 else c
        out.append(((px, py, pc), 4 * px + 2 * py + pc))
    return 4 * x + 2 * y + c, out


def _hbm(a):
    return pltpu.with_memory_space_constraint(a, pltpu.HBM)


def _gather_start(name, locs, lands, deps):
    n = len(locs)

    def body(*refs):
        ins, lnd = refs[:n], refs[n:2 * n]
        send, recv = refs[2 * n + len(deps)], refs[2 * n + len(deps) + 1]
        token = refs[-1]
        x, y, c, chips = _place()
        me = 4 * x + 2 * y + c
        for k, to in enumerate([(x, y, 1 - c)] + [(chip[0], chip[1], c) for chip in chips]):
            for a in range(n):
                pltpu.make_async_remote_copy(src_ref=ins[a], dst_ref=lnd[a].at[me], send_sem=send.at[a * 4 + k],
                                             recv_sem=recv.at[a * 4 + k], device_id=to, device_id_type=MESH).start()
        token[...] = jnp.zeros_like(token)

    out_shape = ((pltpu.SemaphoreType.DMA((n * 4,)), pltpu.SemaphoreType.DMA((n * 4,)))
                 + tuple(pltpu.HBM(a.shape, a.dtype) for a in locs) + tuple(pltpu.HBM(a.shape, a.dtype) for a in lands)
                 + (jax.ShapeDtypeStruct((8, LANES), F32),))
    res = _pcall(body, name=name, out_shape=out_shape, in_specs=[HBM] * (2 * n) + [ANY] * len(deps),
                 out_specs=(SEM, SEM) + (HBM,) * (2 * n) + (pl.BlockSpec(memory_space=pltpu.VMEM),),
                 input_output_aliases={a: 2 + a for a in range(2 * n)},
                 compiler_params=pltpu.CompilerParams(has_side_effects=DATAFLOW))(
        *[_hbm(a) for a in locs], *[_hbm(a) for a in lands], *deps)
    return res[0], res[1], res[2:2 + n], res[2 + n:2 + 2 * n], res[-1]


def _gather_pass(name, send, recv, shards, lands, after):
    n = len(shards)

    def body(*refs):
        ins, lnd = refs[:n], refs[n:2 * n]
        recv = refs[2 * n + 1]
        fsend, frecv = refs[2 * n + 3], refs[2 * n + 4]
        token = refs[-1]
        x, y, c, chips = _place()
        for j, chip in enumerate(chips):
            slot = 4 * chip[0] + 2 * chip[1] + c
            for a in range(n):
                pltpu.make_async_remote_copy(src_ref=ins[a], dst_ref=lnd[a].at[slot], send_sem=fsend.at[a * 3 + j],
                                             recv_sem=recv.at[a * 4 + 1 + j], device_id=(chip[0], chip[1], c),
                                             device_id_type=MESH).wait_recv()
                pltpu.make_async_remote_copy(src_ref=lnd[a].at[slot], dst_ref=lnd[a].at[slot], send_sem=fsend.at[a * 3 + j],
                                             recv_sem=frecv.at[a * 3 + j], device_id=(x, y, 1 - c), device_id_type=MESH).start()
        token[...] = jnp.zeros_like(token)

    out_shape = ((pltpu.SemaphoreType.DMA((n * 3,)), pltpu.SemaphoreType.DMA((n * 3,)))
                 + tuple(pltpu.HBM(a.shape, a.dtype) for a in shards) + tuple(pltpu.HBM(a.shape, a.dtype) for a in lands)
                 + (jax.ShapeDtypeStruct((8, LANES), F32),))
    res = _pcall(body, name=name, out_shape=out_shape, in_specs=[HBM] * (2 * n) + [SEM, SEM, ANY],
                 out_specs=(SEM, SEM) + (HBM,) * (2 * n) + (pl.BlockSpec(memory_space=pltpu.VMEM),),
                 input_output_aliases={a: 2 + a for a in range(2 * n)},
                 compiler_params=pltpu.CompilerParams(has_side_effects=DATAFLOW))(*shards, *lands, send, recv, after)
    return res[0], res[1], res[2:2 + n], res[2 + n:2 + 2 * n], res[-1]


def _gather_wait(name, send, recv, fsend, frecv, shards, lands, after):
    n = len(shards)

    def body(*refs):
        ins, lnd = refs[:n], refs[n:2 * n]
        send, recv, fsend, frecv = refs[2 * n:2 * n + 4]
        x, y, c, chips = _place()
        sib = (x, y, 1 - c)
        for a in range(n):
            own = pltpu.make_async_remote_copy(src_ref=ins[a], dst_ref=lnd[a].at[4 * x + 2 * y + (1 - c)], send_sem=send.at[a * 4],
                                               recv_sem=recv.at[a * 4], device_id=sib, device_id_type=MESH)
            own.wait_send()
            own.wait_recv()
            for j, chip in enumerate(chips):
                pltpu.make_async_remote_copy(src_ref=ins[a], dst_ref=lnd[a].at[0], send_sem=send.at[a * 4 + 1 + j],
                                             recv_sem=recv.at[a * 4 + 1 + j], device_id=(chip[0], chip[1], c),
                                             device_id_type=MESH).wait_send()
                fw = pltpu.make_async_remote_copy(src_ref=lnd[a].at[4 * chip[0] + 2 * chip[1] + c],
                                                  dst_ref=lnd[a].at[4 * chip[0] + 2 * chip[1] + (1 - c)], send_sem=fsend.at[a * 3 + j],
                                                  recv_sem=frecv.at[a * 3 + j], device_id=sib, device_id_type=MESH)
                fw.wait_send()
                fw.wait_recv()

    out_shape = tuple(pltpu.HBM(a.shape, a.dtype) for a in shards) + tuple(pltpu.HBM(a.shape, a.dtype) for a in lands)
    res = _pcall(body, name=name, out_shape=out_shape, in_specs=[HBM] * (2 * n) + [SEM, SEM, SEM, SEM, ANY], out_specs=(HBM,) * (2 * n),
                 input_output_aliases={a: a for a in range(2 * n)},
                 compiler_params=pltpu.CompilerParams(has_side_effects=DATAFLOW))(*shards, *lands, send, recv, fsend, frecv, after)
    return res[n:]


def _chip_start(name, cs, lands):
    n = len(cs)

    def body(*refs):
        ins, lnd = refs[:n], refs[n:2 * n]
        send, recv = refs[2 * n], refs[2 * n + 1]
        token = refs[-1]
        x, y, c, chips = _place()
        qme = 2 * x + y
        for j, chip in enumerate(chips):
            for a in range(n):
                pltpu.make_async_remote_copy(src_ref=ins[a].at[2 * chip[0] + chip[1]], dst_ref=lnd[a].at[qme], send_sem=send.at[a * 3 + j],
                                             recv_sem=recv.at[a * 3 + j], device_id=(chip[0], chip[1], c), device_id_type=MESH).start()
        token[...] = jnp.zeros_like(token)

    out_shape = ((pltpu.SemaphoreType.DMA((n * 3,)), pltpu.SemaphoreType.DMA((n * 3,)))
                 + tuple(pltpu.HBM(a.shape, a.dtype) for a in cs) + tuple(pltpu.HBM(a.shape, a.dtype) for a in lands)
                 + (jax.ShapeDtypeStruct((8, LANES), F32),))
    res = _pcall(body, name=name, out_shape=out_shape, in_specs=[HBM] * (2 * n),
                 out_specs=(SEM, SEM) + (HBM,) * (2 * n) + (pl.BlockSpec(memory_space=pltpu.VMEM),),
                 input_output_aliases={a: 2 + a for a in range(2 * n)},
                 compiler_params=pltpu.CompilerParams(has_side_effects=DATAFLOW))(*[_hbm(a) for a in cs], *[_hbm(a) for a in lands])
    return res[0], res[1], res[2:2 + n], res[2 + n:2 + 2 * n], res[-1]


def _chip_wait(name, send, recv, cs, lands, after):
    n = len(cs)

    def body(*refs):
        ins, lnd = refs[:n], refs[n:2 * n]
        send, recv = refs[2 * n], refs[2 * n + 1]
        x, y, c, chips = _place()
        for j, chip in enumerate(chips):
            q = 2 * chip[0] + chip[1]
            for a in range(n):
                cp = pltpu.make_async_remote_copy(src_ref=ins[a].at[q], dst_ref=lnd[a].at[q], send_sem=send.at[a * 3 + j],
                                                  recv_sem=recv.at[a * 3 + j], device_id=(chip[0], chip[1], c), device_id_type=MESH)
                cp.wait_send()
                cp.wait_recv()

    out_shape = tuple(pltpu.HBM(a.shape, a.dtype) for a in cs) + tuple(pltpu.HBM(a.shape, a.dtype) for a in lands)
    res = _pcall(body, name=name, out_shape=out_shape, in_specs=[HBM] * (2 * n) + [SEM, SEM, ANY], out_specs=(HBM,) * (2 * n),
                 input_output_aliases={a: a for a in range(2 * n)},
                 compiler_params=pltpu.CompilerParams(has_side_effects=DATAFLOW))(*cs, *lands, send, recv, after)
    return res[:n], res[n:]


def _pair_start(name, gs, lands):
    n = len(gs)

    def body(*refs):
        ins, lnd = refs[:n], refs[n:2 * n]
        send, recv = refs[2 * n], refs[2 * n + 1]
        token = refs[-1]
        x, y, c, _ = _place()
        for a in range(n):
            for q in range(N_CHIP):
                pltpu.make_async_remote_copy(src_ref=ins[a].at[q, 1 - c], dst_ref=lnd[a].at[q], send_sem=send.at[a * N_CHIP + q],
                                             recv_sem=recv.at[a * N_CHIP + q], device_id=(x, y, 1 - c), device_id_type=MESH).start()
        token[...] = jnp.zeros_like(token)

    out_shape = ((pltpu.SemaphoreType.DMA((n * N_CHIP,)), pltpu.SemaphoreType.DMA((n * N_CHIP,)))
                 + tuple(pltpu.HBM(a.shape, a.dtype) for a in gs) + tuple(pltpu.HBM(a.shape, a.dtype) for a in lands)
                 + (jax.ShapeDtypeStruct((8, LANES), F32),))
    res = _pcall(body, name=name, out_shape=out_shape, in_specs=[HBM] * (2 * n),
                 out_specs=(SEM, SEM) + (HBM,) * (2 * n) + (pl.BlockSpec(memory_space=pltpu.VMEM),),
                 input_output_aliases={a: 2 + a for a in range(2 * n)},
                 compiler_params=pltpu.CompilerParams(has_side_effects=DATAFLOW))(*[_hbm(a) for a in gs], *[_hbm(a) for a in lands])
    return res[0], res[1], res[2:2 + n], res[2 + n:2 + 2 * n], res[-1]


def _pair_wait(name, send, recv, gs, lands, after):
    n = len(gs)

    def body(*refs):
        ins, lnd = refs[:n], refs[n:2 * n]
        send, recv = refs[2 * n], refs[2 * n + 1]
        x, y, c, _ = _place()
        for a in range(n):
            for q in range(N_CHIP):
                cp = pltpu.make_async_remote_copy(src_ref=ins[a].at[q, 1 - c], dst_ref=lnd[a].at[q], send_sem=send.at[a * N_CHIP + q],
                                                  recv_sem=recv.at[a * N_CHIP + q], device_id=(x, y, 1 - c), device_id_type=MESH)
                cp.wait_send()
                cp.wait_recv()

    out_shape = tuple(pltpu.HBM(a.shape, a.dtype) for a in gs) + tuple(pltpu.HBM(a.shape, a.dtype) for a in lands)
    res = _pcall(body, name=name, out_shape=out_shape, in_specs=[HBM] * (2 * n) + [SEM, SEM, ANY], out_specs=(HBM,) * (2 * n),
                 input_output_aliases={a: a for a in range(2 * n)},
                 compiler_params=pltpu.CompilerParams(has_side_effects=DATAFLOW))(*gs, *lands, send, recv, after)
    return res[:n], res[n:]


def _pair_exchange(name, gs):
    n = len(gs)

    def body(*refs):
        ins, got = refs[:n], refs[n:2 * n]
        send, recv = refs[2 * n:]
        x, y, c, _ = _place()
        sib = (x, y, 1 - c)
        rem = [pltpu.make_async_remote_copy(src_ref=ins[a].at[q, 1 - c], dst_ref=got[a].at[q], send_sem=send.at[a, q],
                                            recv_sem=recv.at[a, q], device_id=sib, device_id_type=MESH)
               for a in range(n) for q in range(N_CHIP)]
        for d in rem:
            d.start()
        for d in rem:
            d.wait()

    half = [jax.ShapeDtypeStruct((N_CHIP,) + g.shape[2:], g.dtype) for g in gs]
    return _pcall(body, name=name, in_specs=[ANY] * n, out_specs=[ANY] * n, out_shape=half,
                  scratch_shapes=[pltpu.SemaphoreType.DMA((n, N_CHIP)), pltpu.SemaphoreType.DMA((n, N_CHIP))],
                  compiler_params=pltpu.CompilerParams(has_side_effects=True))(*gs)


def _chip_exchange(cs, sm_sh, sm_rep):
    n = len(cs)

    def body(*refs):
        ins, sh, rep = refs[:n], refs[n], refs[n + 1]
        outs, osh, orep = refs[n + 2:2 * n + 2], refs[2 * n + 2], refs[2 * n + 3]
        send, recv, lsem, ssend, srecv, slsem = refs[2 * n + 4:]
        x, y, c, chips = _place()
        qme = 2 * x + y
        me = 4 * x + 2 * y + c
        loc = [pltpu.make_async_copy(ins[a].at[qme], outs[a].at[qme], lsem.at[a]) for a in range(n)]
        loc.append(pltpu.make_async_copy(sh.at[me], osh.at[me], slsem.at[0]))
        loc.append(pltpu.make_async_copy(rep, orep.at[me], slsem.at[1]))
        rem, waits = [], []
        for j, chip in enumerate(chips):
            q = 2 * chip[0] + chip[1]
            to = (chip[0], chip[1], c)
            for a in range(n):
                rem.append(pltpu.make_async_remote_copy(src_ref=ins[a].at[q], dst_ref=outs[a].at[qme], send_sem=send.at[a, j],
                                                        recv_sem=recv.at[a, j], device_id=to, device_id_type=MESH))
                waits.append(pltpu.make_async_remote_copy(src_ref=ins[a].at[q], dst_ref=outs[a].at[q], send_sem=send.at[a, j],
                                                          recv_sem=recv.at[a, j], device_id=to, device_id_type=MESH))
        for r in range(1, N_DEV):
            fx, fy, fc = (r >> 2) & 1, (r >> 1) & 1, r & 1
            px = 1 - x if fx else x
            py = 1 - y if fy else y
            pc = 1 - c if fc else c
            t = 4 * px + 2 * py + pc
            to = (px, py, pc)
            rem.append(pltpu.make_async_remote_copy(src_ref=sh.at[t], dst_ref=osh.at[me], send_sem=ssend.at[0, r - 1],
                                                    recv_sem=srecv.at[0, r - 1], device_id=to, device_id_type=MESH))
            waits.append(pltpu.make_async_remote_copy(src_ref=sh.at[t], dst_ref=osh.at[t], send_sem=ssend.at[0, r - 1],
                                                      recv_sem=srecv.at[0, r - 1], device_id=to, device_id_type=MESH))
            rem.append(pltpu.make_async_remote_copy(src_ref=rep, dst_ref=orep.at[me], send_sem=ssend.at[1, r - 1],
                                                    recv_sem=srecv.at[1, r - 1], device_id=to, device_id_type=MESH))
            waits.append(pltpu.make_async_remote_copy(src_ref=rep, dst_ref=orep.at[t], send_sem=ssend.at[1, r - 1],
                                                      recv_sem=srecv.at[1, r - 1], device_id=to, device_id_type=MESH))
        for d in loc + rem:
            d.start()
        for d in waits:
            d.wait_recv()
        for d in rem:
            d.wait_send()
        for d in loc:
            d.wait()

    out_shape = [jax.ShapeDtypeStruct(a.shape, a.dtype) for a in cs]
    out_shape += [jax.ShapeDtypeStruct(sm_sh.shape, F32), jax.ShapeDtypeStruct((N_DEV,) + sm_rep.shape, F32)]
    res = _pcall(body, name="chip_exchange", in_specs=[ANY] * (n + 2), out_specs=[ANY] * (n + 2), out_shape=out_shape,
                 scratch_shapes=[pltpu.SemaphoreType.DMA((n, 3)), pltpu.SemaphoreType.DMA((n, 3)), pltpu.SemaphoreType.DMA((n,)),
                                 pltpu.SemaphoreType.DMA((2, 7)), pltpu.SemaphoreType.DMA((2, 7)), pltpu.SemaphoreType.DMA((2,))],
                 compiler_params=pltpu.CompilerParams(has_side_effects=True))(*cs, sm_sh, sm_rep)
    return res[:n], res[n], res[n + 1]


BIG = ("mlp_up", "mlp_down", "conv_w_in", "conv_w_out", "pool_w", "sc_w_in", "sc_w_out", "ret_w_in", "ret_w_out")
SMALL_SPLIT = ("norm_g", "conv_dw", "sc_dw")
SMALL_REP = ("conv_b_in", "conv_dw_b", "conv_ln_g", "conv_ln_b", "conv_b_out", "pool_scale")
WEIGHTS = ("norm_g", "mlp_up", "mlp_down", "conv_w_in", "conv_b_in", "conv_dw", "conv_dw_b", "conv_ln_g", "conv_ln_b",
           "conv_w_out", "conv_b_out", "pool_w", "pool_scale", "sc_w_in", "sc_dw", "sc_w_out", "ret_w_in", "ret_w_out")


def _pad_rows(a, rows):
    return jnp.pad(a, ((0, rows - a.shape[0]), (0, 0)))


def _split_rows(names, shards):
    out = []
    for nme in names:
        r = int(np.prod(shards[nme].shape[:-1]))
        out.append((nme, r, -(-r // 8) * 8))
    return out


def _pack_split(names, arrs):
    parts = []
    for nme, r, rp in _split_rows(names, arrs):
        parts.append(_pad_rows(arrs[nme].reshape(r, arrs[nme].shape[-1]), rp))
    return jnp.concatenate(parts, axis=0)


def _pack_rep(arrs, D):
    parts = [arrs[nme].reshape(-1, D) for nme in SMALL_REP]
    rows = sum(p.shape[0] for p in parts)
    return _pad_rows(jnp.concatenate(parts, axis=0), -(-rows // 8) * 8)


def kernel(x, positions, norm_g, mlp_up, mlp_down, conv_w_in, conv_b_in, conv_dw, conv_dw_b, conv_ln_g, conv_ln_b, conv_w_out, conv_b_out, pool_w, pool_scale, sc_w_in, sc_dw, sc_w_out, ret_w_in, ret_w_out, loss_target, m_norm_g, m_mlp_up, m_mlp_down, m_conv_w_in, m_conv_b_in, m_conv_dw, m_conv_dw_b, m_conv_ln_g, m_conv_ln_b, m_conv_w_out, m_conv_b_out, m_pool_w, m_pool_scale, m_sc_w_in, m_sc_dw, m_sc_w_out, m_ret_w_in, m_ret_w_out, v_norm_g, v_mlp_up, v_mlp_down, v_conv_w_in, v_conv_b_in, v_conv_dw, v_conv_dw_b, v_conv_ln_g, v_conv_ln_b, v_conv_w_out, v_conv_b_out, v_pool_w, v_pool_scale, v_sc_w_in, v_sc_dw, v_sc_w_out, v_ret_w_in, v_ret_w_out):
    w = dict(norm_g=norm_g, mlp_up=mlp_up, mlp_down=mlp_down, conv_w_in=conv_w_in, conv_b_in=conv_b_in, conv_dw=conv_dw,
             conv_dw_b=conv_dw_b, conv_ln_g=conv_ln_g, conv_ln_b=conv_ln_b, conv_w_out=conv_w_out, conv_b_out=conv_b_out,
             pool_w=pool_w, pool_scale=pool_scale, sc_w_in=sc_w_in, sc_dw=sc_dw, sc_w_out=sc_w_out, ret_w_in=ret_w_in, ret_w_out=ret_w_out)
    m = dict(norm_g=m_norm_g, mlp_up=m_mlp_up, mlp_down=m_mlp_down, conv_w_in=m_conv_w_in, conv_b_in=m_conv_b_in, conv_dw=m_conv_dw,
             conv_dw_b=m_conv_dw_b, conv_ln_g=m_conv_ln_g, conv_ln_b=m_conv_ln_b, conv_w_out=m_conv_w_out, conv_b_out=m_conv_b_out,
             pool_w=m_pool_w, pool_scale=m_pool_scale, sc_w_in=m_sc_w_in, sc_dw=m_sc_dw, sc_w_out=m_sc_w_out, ret_w_in=m_ret_w_in,
             ret_w_out=m_ret_w_out)
    v = dict(norm_g=v_norm_g, mlp_up=v_mlp_up, mlp_down=v_mlp_down, conv_w_in=v_conv_w_in, conv_b_in=v_conv_b_in, conv_dw=v_conv_dw,
             conv_dw_b=v_conv_dw_b, conv_ln_g=v_conv_ln_g, conv_ln_b=v_conv_ln_b, conv_w_out=v_conv_w_out, conv_b_out=v_conv_b_out,
             pool_w=v_pool_w, pool_scale=v_pool_scale, sc_w_in=v_sc_w_in, sc_dw=v_sc_dw, sc_w_out=v_sc_w_out, ret_w_in=v_ret_w_in,
             ret_w_out=v_ret_w_out)
    T, D = x.shape[1], x.shape[2]
    L = D // N_DEV
    G = len(POOL_WINDOWS)
    cg = D // G

    me = 4 * lax.axis_index("x") + 2 * lax.axis_index("y") + lax.axis_index("c")
    MIX = (("conv_w_in", "conv_w_out"), ("pool_w",), ("sc_w_in", "sc_w_out"), ("ret_w_in", "ret_w_out"))

    def shards(l, part):
        out = {k: w[k].astype(BF16) for k in MIX[l]} if part != "mlp" else {}
        if part != "mix":
            out["mlp_up"], out["mlp_down"] = w["mlp_up"][l].astype(BF16), w["mlp_down"][l].astype(BF16)
        return out

    sh0 = shards(0, "mix")
    gath0 = _gather_all(list(sh0.values()) + [_pack_split(SMALL_SPLIT, w)])
    small = gath0[-1]
    common = {k: w[k].reshape(1, -1) for k in SMALL_REP}
    off = 0
    for nme, r, rp in _split_rows(SMALL_SPLIT, w):
        common[nme] = small[:, off:off + r, :].transpose(1, 0, 2).reshape(r, D)
        off += rp
    flying = {}
    dep = small
    for l, part in [(0, "mlp")] + [(l, "both") for l in range(1, DEPTH)]:
        shl = shards(l, part)
        lands = [lax.dynamic_update_slice(lax.empty((N_DEV,) + a.shape, a.dtype), a[None], (me,) + (0,) * a.ndim) for a in shl.values()]
        send, recv, thru, lands, dep = _gather_start("gather_start_%d" % l, list(shl.values()), lands, [dep])
        flying[l] = dict(names=list(shl.keys()), send=send, recv=recv, thru=thru, lands=lands)

    def pass_on(l, after):
        f = flying[l]
        f["fsend"], f["frecv"], f["thru"], f["lands"], tok = _gather_pass("gather_pass_%d" % l, f["send"], f["recv"], f["thru"],
                                                                          f["lands"], after)
        return tok

    def arrive(l, after):
        f = flying[l]
        got = _gather_wait("gather_wait_%d" % l, f["send"], f["recv"], f["fsend"], f["frecv"], f["thru"], f["lands"], after)
        return dict(zip(f["names"], got))

    def get_w(l, part, after):
        if part == "mlp" and l > 0:
            return {"_deps": [pass_on(l + 1, after)]} if l + 1 < DEPTH else {}
        if l == 0 and part == "mix":
            gw = dict(zip(sh0.keys(), gath0[:-1]))
        elif l == 0:
            gw = arrive(0, pass_on(0, after))
            gw["_deps"] = [pass_on(1, gw["mlp_up"])]
        else:
            gw = arrive(l, after)
        out = dict(common)
        for k, a in gw.items():
            if k.endswith("_w_out"):
                out[k] = a.reshape(-1, D)
            elif k == "pool_w":
                out[k] = a.transpose(1, 0, 2, 3).reshape(G, cg, cg)
            else:
                out[k] = a
        return out

    res = {}
    mlp_res = {k: [lax.empty(w[k].shape, F32) for _ in range(4)] for k in ("mlp_up", "mlp_down")}
    pending = []
    small_g = {}

    def shard2d(src, k, l):
        a = src[k][l] if k in mlp_res else src[k]
        return a.reshape(-1, a.shape[-1])

    def adam_layer(l, names, parts, own):
        for i, k in enumerate(names):
            mine = None if own is None else own[i]
            if k in mlp_res:
                mlp_res[k] = _adamw("adamw_%s_%d" % (k, l), parts[i], w[k], m[k], v[k], mine, layer=l, bufs=mlp_res[k])
            else:
                out = _adamw("adamw_%s_%d" % (k, l), parts[i], shard2d(w, k, l), shard2d(m, k, l), shard2d(v, k, l), mine)
                res[k] = [o.reshape(w[k].shape) for o in out]

    held = {}

    def put_g(l, part, g, after, norm_grads):
        small_g.update({k: g[k] for k in g if k in SMALL_SPLIT or k in SMALL_REP})
        def stack(k):
            a = g[k]
            if k == "pool_w":
                a = a.reshape(G, N_DEV, cg // N_DEV, cg).transpose(1, 0, 2, 3).astype(BF16)
            return a.reshape(N_CHIP, 2, -1, w[k].shape[-1])

        if part == "mlp" and l > 0:
            st = [stack("mlp_up"), stack("mlp_down")]
            send, recv, st, lands, tok = _pair_start("pair_start_%d" % l, st, [lax.empty((N_CHIP,) + a.shape[2:], a.dtype) for a in st])
            held.update(send=send, recv=recv, st=st, lands=lands)
            return tok
        if pending:
            tag, lp, names, send, recv, sums, lands = pending.pop()
            sums, parts = _chip_wait("chip_wait_" + tag, send, recv, sums, lands, after)
            adam_layer(lp, names, parts, sums)
        last = l == 0 and part == "mix"
        names = list(MIX[l]) if (l > 0 or last) else ["mlp_up", "mlp_down"]
        tag = "%d%s" % (l, "" if l > 0 else part)
        stacks = [stack(k) for k in names]
        got = _pair_exchange("pair_exchange_" + tag, stacks)
        if l > 0:
            st, got_mlp = _pair_wait("pair_wait_%d" % l, held["send"], held["recv"], held["st"], held["lands"], got[0])
            names, stacks, got = names + ["mlp_up", "mlp_down"], stacks + list(st), list(got) + list(got_mlp)
            held.clear()
        sums = [_pair_add("pair_add", a, b) for a, b in zip(stacks, got)]
        if not last:
            send, recv, sums, lands, tok = _chip_start("chip_start_" + tag, sums, [lax.empty(a.shape, a.dtype) for a in sums])
            pending.append((tag, l, names, send, recv, sums, lands))
            return tok
        small_g["norm_g"] = norm_grads
        gsm = []
        for nme, r, rp in _split_rows(SMALL_SPLIT, w):
            gsm.append(jnp.pad(small_g[nme].reshape(r, N_DEV, L), ((0, rp - r), (0, 0), (0, 0))))
        sm_sh = jnp.concatenate(gsm, axis=0).transpose(1, 0, 2)
        parts, r_sh, r_rep = _chip_exchange(sums, sm_sh, _pack_rep(small_g, D))
        adam_layer(0, names, parts, None)
        out = _adamw("adamw_split", r_sh, _pack_split(SMALL_SPLIT, w), _pack_split(SMALL_SPLIT, m), _pack_split(SMALL_SPLIT, v))
        off = 0
        for nme, r, rp in _split_rows(SMALL_SPLIT, w):
            res[nme] = [o[off:off + r].reshape(w[nme].shape) for o in out]
            off += rp
        out = _adamw("adamw_rep", r_rep, _pack_rep(w, D), _pack_rep(m, D), _pack_rep(v, D))
        off = 0
        for nme in SMALL_REP:
            r = w[nme].shape[0] // D
            res[nme] = [o[off:off + r].reshape(w[nme].shape) for o in out]
            off += r
        return None

    pos = positions.reshape(T, 1).astype(F32)
    loss_row, grad_x = _local_step(x.reshape(T, D), pos, loss_target.reshape(T, D), common["norm_g"], get_w, put_g, [dep], T, D)
    loss = lax.psum(0.5 * jnp.sum(loss_row) / D, AXES)
    res.update(mlp_res)
    outs = [loss, grad_x.reshape(x.shape)]
    for i in range(4):
        outs += [res[k][i] for k in WEIGHTS]
    return tuple(outs)
```

```python
import functools
import math

import numpy as np
import jax
import jax.numpy as jnp
from jax import lax
from jax.experimental import pallas as pl
from jax.experimental.pallas import tpu as pltpu

F32, BF16 = jnp.float32, jnp.bfloat16
MESH = pl.DeviceIdType.MESH
AXES = ("x", "y", "c")
N_DEV = 8
N_CHIP = 4
EPS = 1e-6
DEPTH = 4
CONV_WIDTH = 31
POOL_WINDOWS = (2, 4, 8, 16)
SHORT_CONV_WIDTH = 3
RET_HEADS = 4
RET_CHUNK = 128
ROPE_BASE = 10000.0
ADAM_LR, ADAM_B1, ADAM_B2, ADAM_EPS, ADAM_WD, ADAM_STEP = 0.001, 0.9, 0.999, 1e-08, 0.01, 10
LANES = 128
SEQ_PAD = 32
VMEM_LIMIT = 56 * 1024 * 1024

NN = (((1,), (0,)), ((), ()))
NT = (((1,), (1,)), ((), ()))
TN = (((0,), (0,)), ((), ()))


def _pcall(body, **kw):
    return pl.pallas_call(body, **kw)


def _params(sem):
    return pltpu.CompilerParams(dimension_semantics=sem, vmem_limit_bytes=VMEM_LIMIT)


def _tile(n, pref):
    t = min(n, pref)
    while n % t:
        t //= 2
    return t


def _rowwise(name, fn, ins, outs, rows, tr, deps=()):
    n_in = len(ins)
    n_dep = len(deps)
    in_specs = []
    for arr, nc, cb in ins:
        if nc is None:
            in_specs.append(pl.BlockSpec(arr.shape, lambda i, nd=arr.ndim: (0,) * nd))
        else:
            in_specs.append(pl.BlockSpec((tr, nc), lambda i, cb=cb: (i, cb)))
    out_specs, out_shapes = [], []
    for o in outs:
        if o[0] == "row":
            out_shapes.append(jax.ShapeDtypeStruct((rows, o[1]), o[2]))
            out_specs.append(pl.BlockSpec((tr, o[1]), lambda i: (i, 0)))
        else:
            out_shapes.append(jax.ShapeDtypeStruct(o[1], F32))
            out_specs.append(pl.BlockSpec(o[1], lambda i: (0, 0)))

    in_specs = in_specs + [pl.BlockSpec(memory_space=pl.ANY)] * n_dep

    def body(*refs):
        res = fn(*[r[...] for r in refs[:n_in]])
        if not isinstance(res, (tuple, list)):
            res = (res,)
        i = pl.program_id(0)
        for o, ref, r in zip(outs, refs[n_in + n_dep:], res):
            if o[0] == "row":
                ref[...] = r.astype(o[2])
            else:
                @pl.when(i == 0)
                def _(ref=ref, r=r):
                    ref[...] = r

                @pl.when(i > 0)
                def _(ref=ref, r=r):
                    ref[...] += r

    return _pcall(body, name=name, grid=(rows // tr,), in_specs=in_specs, out_specs=out_specs,
                  out_shape=out_shapes, compiler_params=_params(("arbitrary",)))(*[a for a, _, _ in ins], *deps)


def _rms(x, g):
    return x * lax.rsqrt(jnp.mean(x * x, axis=-1, keepdims=True) + EPS) * g


def _colsum(x):
    return jnp.sum(x, axis=0, keepdims=True)


def _square(r):
    rf = r.astype(F32)
    return (rf * rf).astype(r.dtype)


def _ln_silu(c, g, b):
    mu = jnp.mean(c, axis=-1, keepdims=True)
    xc = c - mu
    y = xc * lax.rsqrt(jnp.mean(xc * xc, axis=-1, keepdims=True) + EPS) * g + b
    return y * jax.nn.sigmoid(y)


def _glu(za, zg):
    return za * jax.nn.sigmoid(zg)


def _gate_norm(o, g):
    dv = o.shape[-1] // RET_HEADS
    parts = []
    for h in range(RET_HEADS):
        oh = o[:, h * dv:(h + 1) * dv]
        parts.append(oh * lax.rsqrt(jnp.mean(oh * oh, axis=-1, keepdims=True) + EPS))
    return (g * jax.nn.sigmoid(g)) * jnp.concatenate(parts, axis=-1)


def _rot(x, cos, sin, sign):
    dk = x.shape[-1] // RET_HEADS
    half = dk // 2
    parts = []
    for h in range(RET_HEADS):
        x1 = x[:, h * dk:h * dk + half]
        x2 = x[:, h * dk + half:(h + 1) * dk]
        parts.append(x1 * cos - sign * (x2 * sin))
        parts.append(sign * (x1 * sin) + x2 * cos)
    return jnp.concatenate(parts, axis=-1)


def _mm(name, mode, grid, ins, outs, epi=None, aliases=None, pre=None, deps=()):
    dn = {"nn": NN, "nt": NT, "tn": TN}[mode]
    gk = grid[2]
    n_in, n_out = len(ins), len(outs)
    n_dep = len(deps)

    def body(*refs):
        out_refs = refs[n_in + n_dep:n_in + n_dep + n_out]
        a = refs[0][...] if pre is None else pre(refs[0][...])
        part = lax.dot_general(a, refs[1][...], dn, preferred_element_type=F32)

        def finish(acc):
            res = epi(acc, *[r[...] for r in refs[2:n_in]]) if epi is not None else (acc,)
            for o, r in zip(out_refs, res):
                o[...] = r.astype(o.dtype)

        if gk == 1:
            finish(part)
        else:
            acc_ref = refs[-1]
            k = pl.program_id(2)

            @pl.when(k == 0)
            def _():
                acc_ref[...] = part

            @pl.when(k > 0)
            def _():
                acc_ref[...] += part

            @pl.when(k == gk - 1)
            def _():
                finish(acc_ref[...])

    scratch = []
    if gk > 1:
        blk = [d for d in outs[0][2].block_shape if d is not None]
        scratch = [pltpu.VMEM(tuple(blk), F32)]
    kw = {}
    if aliases:
        kw["input_output_aliases"] = aliases
    return _pcall(body, name=name, grid=grid, in_specs=[s for _, s in ins] + [pl.BlockSpec(memory_space=pl.ANY)] * n_dep,
                  out_specs=[s for _, _, s in outs],
                  out_shape=[jax.ShapeDtypeStruct(sh, dt) for sh, dt, _ in outs], scratch_shapes=scratch,
                  compiler_params=_params(("parallel", "parallel", "arbitrary")), **kw)(*[a for a, _ in ins], *deps)


def _spec(block, fn):
    return pl.BlockSpec(block, fn)


def _mm_cols(name, a, wg, lead, T, epi=None, extras=(), n_out=1, out_dtype=F32, deps=(), rows=2048):
    K, nb = wg.shape[-2], wg.shape[-1]
    tm = _tile(T, rows)
    wblock = (None,) * (1 + len(lead)) + (K, nb)
    ins = [(a, _spec((tm, K), lambda i, j, k: (i, 0))), (wg, _spec(wblock, lambda i, j, k: (j,) + tuple(lead) + (0, 0)))]
    for e in extras:
        if e.shape[0] == 1:
            ins.append((e, _spec((1, nb), lambda i, j, k: (0, j))))
        else:
            ins.append((e, _spec((tm, nb), lambda i, j, k: (i, j))))
    outs = [((T, N_DEV * nb), out_dtype, _spec((tm, nb), lambda i, j, k: (i, j))) for _ in range(n_out)]
    return _mm(name, "nn", (T // tm, N_DEV, 1), ins, outs, epi, deps=deps)


def _mm_cols_grad(name, a, d, wg_shape, lead, T):
    K, nb = wg_shape[-2], wg_shape[-1]
    tk = T
    wblock = (None,) * (1 + len(lead)) + (K, nb)
    ins = [(a, _spec((tk, K), lambda i, j, k: (k, 0))), (d, _spec((tk, nb), lambda i, j, k: (k, j)))]
    outs = [(tuple(wg_shape), BF16, _spec(wblock, lambda i, j, k: (j,) + tuple(lead) + (0, 0)))]
    return _mm(name, "tn", (1, N_DEV, T // tk), ins, outs, lambda acc: (acc,))[0]


def _mm_whole_k(name, a, w3, T, split, pre=None, out_dtype=F32):
    nblk, d1, d2 = w3.shape
    kb = d1 if split == "rows" else d2
    N = d2 if split == "rows" else d1
    tm = _tile(T, 512)

    def body(a_ref, w_ref, o_ref, *scratch):
        av = a_ref[...] if pre is None else pre(a_ref[...])
        if split == "rows":
            o_ref[...] = lax.dot_general(av, w_ref[...].reshape(nblk * kb, N), NN, preferred_element_type=F32).astype(out_dtype)
        else:
            wcat, sem = scratch

            @pl.when(pl.program_id(0) == 0)
            def _():
                cps = [pltpu.make_async_copy(w_ref.at[j], wcat.at[:, pl.ds(j * kb, kb)], sem.at[j]) for j in range(nblk)]
                for cp in cps:
                    cp.start()
                for cp in cps:
                    cp.wait()

            o_ref[...] = lax.dot_general(av, wcat[...], NT, preferred_element_type=F32).astype(out_dtype)

    rows = split == "rows"
    scratch = [] if rows else [pltpu.VMEM((N, nblk * kb), w3.dtype), pltpu.SemaphoreType.DMA((nblk,))]
    return _pcall(body, name=name, grid=(T // tm,),
                  in_specs=[pl.BlockSpec((tm, nblk * kb), lambda i: (i, 0)),
                            pl.BlockSpec(w3.shape, lambda i: (0, 0, 0)) if rows else pl.BlockSpec(memory_space=pl.ANY)],
                  out_specs=pl.BlockSpec((tm, N), lambda i: (i, 0)), out_shape=jax.ShapeDtypeStruct((T, N), out_dtype),
                  scratch_shapes=scratch, compiler_params=_params(("arbitrary",)))(a, w3)


def _mm_rows(name, a, w2, T, epi=None, extras=(), out_dtype=F32):
    Kin, N = w2.shape
    tm, tn = _tile(T, 2048), _tile(N, 512)
    ins = [(a, _spec((tm, Kin), lambda i, j, k: (i, 0))), (w2, _spec((Kin, tn), lambda i, j, k: (0, j)))]
    for e in extras:
        ins.append((e, _spec((1, tn), lambda i, j, k: (0, j))))
    outs = [((T, N), out_dtype, _spec((tm, tn), lambda i, j, k: (i, j)))]
    return _mm(name, "nn", (T // tm, N // tn, 1), ins, outs, epi)[0]


def _mm_rows_t(name, d, w2, T, out_dtype=F32):
    Kin, N = w2.shape
    tm, tn = _tile(T, 2048), _tile(Kin, 512)
    ins = [(d, _spec((tm, N), lambda i, j, k: (i, 0))), (w2, _spec((tn, N), lambda i, j, k: (j, 0)))]
    outs = [((T, Kin), out_dtype, _spec((tm, tn), lambda i, j, k: (i, j)))]
    return _mm(name, "nt", (T // tm, Kin // tn, 1), ins, outs)[0]


def _mm_rows_grad(name, a, d, T):
    Kin, N = a.shape[1], d.shape[1]
    tk, tm, tn = _tile(T, 2048), _tile(Kin, 512), _tile(N, 1024)
    ins = [(a, _spec((tk, tm), lambda i, j, k: (k, i))), (d, _spec((tk, tn), lambda i, j, k: (k, j)))]
    outs = [((Kin, N), BF16, _spec((tm, tn), lambda i, j, k: (i, j)))]
    return _mm(name, "tn", (Kin // tm, N // tn, T // tk), ins, outs, lambda acc: (acc,))[0]


def _dwconv_fwd(name, x, w, bias, T):
    W, C = w.shape
    cb = min(C, LANES)
    tc = _tile(T, 256)

    def body(x_ref, w_ref, b_ref, o_ref, xp_ref):
        xp_ref[pl.ds(0, SEQ_PAD), :] = jnp.zeros((SEQ_PAD, cb), F32)

        def fill(c, _):
            base = pl.multiple_of(c * tc, tc)
            xp_ref[pl.ds(base + SEQ_PAD, tc), :] = x_ref[pl.ds(base, tc), :]
            return 0

        lax.fori_loop(0, T // tc, fill, 0)

        def chunk(c, _):
            base = pl.multiple_of(c * tc, tc)
            acc = jnp.zeros((tc, cb), F32) + b_ref[...]
            for k in range(W):
                acc = acc + w_ref[pl.ds(k, 1), :] * xp_ref[pl.ds(base + SEQ_PAD - (W - 1) + k, tc), :]
            o_ref[pl.ds(base, tc), :] = acc
            return 0

        lax.fori_loop(0, T // tc, chunk, 0)

    return _pcall(body, name=name, grid=(C // cb,),
                  in_specs=[pl.BlockSpec((T, cb), lambda i: (0, i)), pl.BlockSpec((W, cb), lambda i: (0, i)),
                            pl.BlockSpec((1, cb), lambda i: (0, i))],
                  out_specs=pl.BlockSpec((T, cb), lambda i: (0, i)), out_shape=jax.ShapeDtypeStruct((T, C), F32),
                  scratch_shapes=[pltpu.VMEM((T + SEQ_PAD, cb), F32)],
                  compiler_params=_params(("parallel",)))(x, w, bias)


def _dwconv_bwd(name, dc, x, w, T):
    W, C = w.shape
    cb = min(C, LANES)
    tc = _tile(T, 128)

    def body(dc_ref, x_ref, w_ref, dx_ref, dw_ref, xp_ref, dp_ref, acc_ref):
        xp_ref[pl.ds(0, SEQ_PAD), :] = jnp.zeros((SEQ_PAD, cb), F32)
        dp_ref[pl.ds(T, SEQ_PAD), :] = jnp.zeros((SEQ_PAD, cb), F32)
        acc_ref[...] = jnp.zeros((W * 8, cb), F32)

        def fill(c, _):
            base = pl.multiple_of(c * tc, tc)
            xp_ref[pl.ds(base + SEQ_PAD, tc), :] = x_ref[pl.ds(base, tc), :]
            dp_ref[pl.ds(base, tc), :] = dc_ref[pl.ds(base, tc), :]
            return 0

        lax.fori_loop(0, T // tc, fill, 0)

        def chunk(c, _):
            base = pl.multiple_of(c * tc, tc)
            d = dc_ref[pl.ds(base, tc), :]
            acc = jnp.zeros((tc, cb), F32)
            for k in range(W):
                acc = acc + w_ref[pl.ds(k, 1), :] * dp_ref[pl.ds(base + (W - 1) - k, tc), :]
                prod = d * xp_ref[pl.ds(base + SEQ_PAD - (W - 1) + k, tc), :]
                acc_ref[pl.ds(8 * k, 8), :] += jnp.sum(prod.reshape(tc // 8, 8, cb), axis=0)
            dx_ref[pl.ds(base, tc), :] = acc
            return 0

        lax.fori_loop(0, T // tc, chunk, 0)
        for k in range(W):
            dw_ref[pl.ds(k, 1), :] = jnp.sum(acc_ref[pl.ds(8 * k, 8), :], axis=0, keepdims=True)

    return _pcall(body, name=name, grid=(C // cb,),
                  in_specs=[pl.BlockSpec((T, cb), lambda i: (0, i)), pl.BlockSpec((T, cb), lambda i: (0, i)),
                            pl.BlockSpec((W, cb), lambda i: (0, i))],
                  out_specs=[pl.BlockSpec((T, cb), lambda i: (0, i)), pl.BlockSpec((W, cb), lambda i: (0, i))],
                  out_shape=[jax.ShapeDtypeStruct((T, C), F32), jax.ShapeDtypeStruct((W, C), F32)],
                  scratch_shapes=[pltpu.VMEM((T + SEQ_PAD, cb), F32), pltpu.VMEM((T + SEQ_PAD, cb), F32),
                                  pltpu.VMEM((W * 8, cb), F32)],
                  compiler_params=_params(("parallel",)))(dc, x, w)


def _pool_count(base, tc, cg, win):
    t = (lax.broadcasted_iota(jnp.int32, (tc, cg), 0) + base + 1).astype(F32)
    return jnp.minimum(t, float(win))


def _pool_fwd(name, u, T):
    D = u.shape[1]
    G = len(POOL_WINDOWS)
    cg = min(D // G, LANES)
    nb = (D // G) // cg
    tc = _tile(T, 128)

    def body(u_ref, o_ref, up_ref):
        g = pl.program_id(0)
        up_ref[pl.ds(0, SEQ_PAD), :] = jnp.zeros((SEQ_PAD, cg), F32)

        def fill(c, _):
            base = pl.multiple_of(c * tc, tc)
            up_ref[pl.ds(base + SEQ_PAD, tc), :] = u_ref[pl.ds(base, tc), :]
            return 0

        lax.fori_loop(0, T // tc, fill, 0)
        for gi, win in enumerate(POOL_WINDOWS):
            @pl.when(g == gi)
            def _(win=win):
                def chunk(c, _):
                    base = pl.multiple_of(c * tc, tc)
                    acc = up_ref[pl.ds(base + SEQ_PAD, tc), :]
                    for j in range(1, win):
                        acc = acc + up_ref[pl.ds(base + SEQ_PAD - j, tc), :]
                    mixed = acc / _pool_count(base, tc, cg, win) - u_ref[pl.ds(base, tc), :]
                    o_ref[pl.ds(base, tc), :] = mixed.astype(BF16)
                    return 0

                lax.fori_loop(0, T // tc, chunk, 0)

    return _pcall(body, name=name, grid=(G, nb), in_specs=[pl.BlockSpec((T, cg), lambda g, i: (0, g * nb + i))],
                  out_specs=pl.BlockSpec((T, cg), lambda g, i: (0, g * nb + i)), out_shape=jax.ShapeDtypeStruct((T, D), BF16),
                  scratch_shapes=[pltpu.VMEM((T + SEQ_PAD, cg), F32)], compiler_params=_params(("parallel", "parallel")))(u)


def _pool_bwd(name, dm, T):
    D = dm.shape[1]
    G = len(POOL_WINDOWS)
    cg = min(D // G, LANES)
    nb = (D // G) // cg
    tc = _tile(T, 128)

    def body(d_ref, o_ref, qp_ref):
        g = pl.program_id(0)
        qp_ref[pl.ds(T, SEQ_PAD), :] = jnp.zeros((SEQ_PAD, cg), F32)
        for gi, win in enumerate(POOL_WINDOWS):
            @pl.when(g == gi)
            def _(win=win):
                def fill(c, _):
                    base = pl.multiple_of(c * tc, tc)
                    qp_ref[pl.ds(base, tc), :] = d_ref[pl.ds(base, tc), :] / _pool_count(base, tc, cg, win)
                    return 0

                lax.fori_loop(0, T // tc, fill, 0)

                def chunk(c, _):
                    base = pl.multiple_of(c * tc, tc)
                    acc = qp_ref[pl.ds(base, tc), :]
                    for j in range(1, win):
                        acc = acc + qp_ref[pl.ds(base + j, tc), :]
                    o_ref[pl.ds(base, tc), :] = (acc - d_ref[pl.ds(base, tc), :]).astype(BF16)
                    return 0

                lax.fori_loop(0, T // tc, chunk, 0)

    return _pcall(body, name=name, grid=(G, nb), in_specs=[pl.BlockSpec((T, cg), lambda g, i: (0, g * nb + i))],
                  out_specs=pl.BlockSpec((T, cg), lambda g, i: (0, g * nb + i)), out_shape=jax.ShapeDtypeStruct((T, D), BF16),
                  scratch_shapes=[pltpu.VMEM((T + SEQ_PAD, cg), F32)], compiler_params=_params(("parallel", "parallel")))(dm)


def _pool_mm(name, mode, a, b, T, scale=None):
    G = len(POOL_WINDOWS)
    D = a.shape[1]
    cg = D // G
    tm = _tile(T, 2048)
    if mode == "tn":
        ins = [(a, _spec((tm, cg), lambda i, j, k: (k, j))), (b, _spec((tm, cg), lambda i, j, k: (k, j)))]
        outs = [((G, cg, cg), F32, _spec((None, cg, cg), lambda i, j, k: (j, 0, 0)))]
        return _mm(name, "tn", (1, G, T // tm), ins, outs, lambda acc: (acc,))[0]
    ins = [(a, _spec((tm, cg), lambda i, j, k: (i, j))), (b, _spec((None, cg, cg), lambda i, j, k: (j, 0, 0)))]
    if scale is not None:
        ins.append((scale, _spec((1, cg), lambda i, j, k: (0, j))))
        outs = [((T, D), BF16, _spec((tm, cg), lambda i, j, k: (i, j))), ((T, D), BF16, _spec((tm, cg), lambda i, j, k: (i, j)))]
        return _mm(name, mode, (T // tm, G, 1), ins, outs, lambda acc, s: (acc * s, acc))
    outs = [((T, D), F32, _spec((tm, cg), lambda i, j, k: (i, j)))]
    return _mm(name, mode, (T // tm, G, 1), ins, outs)[0]


def _sc_mid_fwd(name, z, w, T):
    D = w.shape[1]
    W = w.shape[0]
    cb = min(D, LANES)
    nb = D // cb
    tc = _tile(T, 256)

    def body(b_ref, c_ref, v_ref, w_ref, o_ref, pp_ref):
        pp_ref[pl.ds(0, SEQ_PAD), :] = jnp.zeros((SEQ_PAD, cb), F32)

        def fill(c, _):
            base = pl.multiple_of(c * tc, tc)
            pp_ref[pl.ds(base + SEQ_PAD, tc), :] = c_ref[pl.ds(base, tc), :].astype(F32) * v_ref[pl.ds(base, tc), :].astype(F32)
            return 0

        lax.fori_loop(0, T // tc, fill, 0)

        def chunk(c, _):
            base = pl.multiple_of(c * tc, tc)
            q = jnp.zeros((tc, cb), F32)
            for k in range(W):
                q = q + w_ref[pl.ds(k, 1), :] * pp_ref[pl.ds(base + SEQ_PAD - (W - 1) + k, tc), :]
            o_ref[pl.ds(base, tc), :] = (b_ref[pl.ds(base, tc), :].astype(F32) * q).astype(BF16)
            return 0

        lax.fori_loop(0, T // tc, chunk, 0)

    return _pcall(body, name=name, grid=(nb,),
                  in_specs=[pl.BlockSpec((T, cb), lambda i: (0, i)), pl.BlockSpec((T, cb), lambda i: (0, nb + i)),
                            pl.BlockSpec((T, cb), lambda i: (0, 2 * nb + i)), pl.BlockSpec((W, cb), lambda i: (0, i))],
                  out_specs=pl.BlockSpec((T, cb), lambda i: (0, i)), out_shape=jax.ShapeDtypeStruct((T, D), BF16),
                  scratch_shapes=[pltpu.VMEM((T + SEQ_PAD, cb), F32)],
                  compiler_params=_params(("parallel",)))(z, z, z, w)


def _sc_mid_bwd(name, z, dm, w, T):
    D = w.shape[1]
    W = w.shape[0]
    cb = min(D, LANES)
    nb = D // cb
    tc = _tile(T, 128)

    def body(b_ref, c_ref, v_ref, d_ref, w_ref, dz_ref, dw_ref, pp_ref, dq_ref, acc_ref):
        part = pl.program_id(0)
        pp_ref[pl.ds(0, SEQ_PAD), :] = jnp.zeros((SEQ_PAD, cb), F32)
        dq_ref[pl.ds(T, SEQ_PAD), :] = jnp.zeros((SEQ_PAD, cb), F32)
        acc_ref[...] = jnp.zeros((W * 8, cb), F32)

        def fill(c, _):
            base = pl.multiple_of(c * tc, tc)
            pp_ref[pl.ds(base + SEQ_PAD, tc), :] = c_ref[pl.ds(base, tc), :].astype(F32) * v_ref[pl.ds(base, tc), :].astype(F32)
            dq_ref[pl.ds(base, tc), :] = d_ref[pl.ds(base, tc), :] * b_ref[pl.ds(base, tc), :].astype(F32)
            return 0

        lax.fori_loop(0, T // tc, fill, 0)

        @pl.when(part == 0)
        def _():
            def chunk(c, _):
                base = pl.multiple_of(c * tc, tc)
                dq = dq_ref[pl.ds(base, tc), :]
                q = jnp.zeros((tc, cb), F32)
                for k in range(W):
                    p = pp_ref[pl.ds(base + SEQ_PAD - (W - 1) + k, tc), :]
                    q = q + w_ref[pl.ds(k, 1), :] * p
                    acc_ref[pl.ds(8 * k, 8), :] += jnp.sum((dq * p).reshape(tc // 8, 8, cb), axis=0)
                dz_ref[pl.ds(base, tc), :] = (d_ref[pl.ds(base, tc), :] * q).astype(BF16)
                return 0

            lax.fori_loop(0, T // tc, chunk, 0)

        @pl.when(part > 0)
        def _():
            def chunk(c, _):
                base = pl.multiple_of(c * tc, tc)
                dp = jnp.zeros((tc, cb), F32)
                for k in range(W):
                    dp = dp + w_ref[pl.ds(k, 1), :] * dq_ref[pl.ds(base + (W - 1) - k, tc), :]
                other = jnp.where(part == 1, v_ref[pl.ds(base, tc), :], c_ref[pl.ds(base, tc), :]).astype(F32)
                dz_ref[pl.ds(base, tc), :] = (dp * other).astype(BF16)
                return 0

            lax.fori_loop(0, T // tc, chunk, 0)

        for k in range(W):
            dw_ref[pl.ds(k, 1), :] = jnp.sum(acc_ref[pl.ds(8 * k, 8), :], axis=0, keepdims=True)

    dz, dw3 = _pcall(body, name=name, grid=(3, nb),
                     in_specs=[pl.BlockSpec((T, cb), lambda p, i: (0, i)), pl.BlockSpec((T, cb), lambda p, i: (0, nb + i)),
                               pl.BlockSpec((T, cb), lambda p, i: (0, 2 * nb + i)), pl.BlockSpec((T, cb), lambda p, i: (0, i)),
                               pl.BlockSpec((W, cb), lambda p, i: (0, i))],
                     out_specs=[pl.BlockSpec((T, cb), lambda p, i: (0, p * nb + i)),
                                pl.BlockSpec((None, W, cb), lambda p, i: (p, 0, i))],
                     out_shape=[jax.ShapeDtypeStruct((T, 3 * D), BF16), jax.ShapeDtypeStruct((3, W, D), F32)],
                     scratch_shapes=[pltpu.VMEM((T + SEQ_PAD, cb), F32), pltpu.VMEM((T + SEQ_PAD, cb), F32),
                                     pltpu.VMEM((W * 8, cb), F32)],
                     compiler_params=_params(("parallel", "parallel")))(z, z, z, dm, w)
    return dz, dw3


def _ret_tables(dk, dv):
    C = RET_CHUNK
    lg = np.log1p(-np.exp2(-5.0 - np.arange(RET_HEADS, dtype=np.float64)))
    idx = np.arange(C, dtype=np.float64)
    rel = idx[:, None] - idx[None, :]
    mask = np.where(rel >= 0, np.exp(lg[:, None, None] * np.maximum(rel, 0.0)), 0.0)
    qd = np.exp(lg[:, None] * (idx + 1.0))[:, :, None] * np.ones((1, 1, dk))
    kd = np.exp(lg[:, None] * (C - 1.0 - idx))[:, :, None] * np.ones((1, 1, dk))
    cd = np.exp(lg * C)[:, None, None] * np.ones((1, 1, dv))
    return tuple(jnp.asarray(a, F32) for a in (mask, qd, kd, cd))


def _ret_fwd(name, q, k, v, tabs, T):
    H, C = RET_HEADS, RET_CHUNK
    dk, dv = q.shape[1] // H, v.shape[1] // H
    nch = T // C
    mask, qd, kd, cd = tabs

    def body(q_ref, k_ref, v_ref, m_ref, qd_ref, kd_ref, cd_ref, o_ref, st_ref, s_ref):
        @pl.when(pl.program_id(0) == 0)
        def _():
            s_ref[...] = jnp.zeros((H, dk, dv), F32)

        for h in range(H):
            qv, kv = q_ref[:, h * dk:(h + 1) * dk], k_ref[:, h * dk:(h + 1) * dk]
            vv = v_ref[:, h * dv:(h + 1) * dv]
            s = s_ref[h]
            sb = s.astype(BF16)
            st_ref[h] = sb
            scores = lax.dot_general(qv, kv, NT, preferred_element_type=F32) * m_ref[h]
            intra = lax.dot_general(scores.astype(BF16), vv, NN, preferred_element_type=F32)
            cross = lax.dot_general((qv.astype(F32) * qd_ref[h]).astype(BF16), sb, NN, preferred_element_type=F32)
            o_ref[:, h * dv:(h + 1) * dv] = intra + cross
            upd = lax.dot_general((kv.astype(F32) * kd_ref[h]).astype(BF16), vv, TN, preferred_element_type=F32)
            s_ref[h] = s * cd_ref[h] + upd

    whole = lambda a: pl.BlockSpec(a.shape, lambda n: (0, 0, 0))
    return _pcall(body, name=name, grid=(nch,),
                  in_specs=[pl.BlockSpec((C, H * dk), lambda n: (n, 0)), pl.BlockSpec((C, H * dk), lambda n: (n, 0)),
                            pl.BlockSpec((C, H * dv), lambda n: (n, 0)), whole(mask), whole(qd), whole(kd), whole(cd)],
                  out_specs=[pl.BlockSpec((C, H * dv), lambda n: (n, 0)),
                             pl.BlockSpec((H, None, dk, dv), lambda n: (0, n, 0, 0))],
                  out_shape=[jax.ShapeDtypeStruct((T, H * dv), F32), jax.ShapeDtypeStruct((H, nch, dk, dv), BF16)],
                  scratch_shapes=[pltpu.VMEM((H, dk, dv), F32)],
                  compiler_params=_params(("arbitrary",)))(q, k, v, mask, qd, kd, cd)


def _ret_bwd(name, q, k, v, do, st, tabs, T):
    H, C = RET_HEADS, RET_CHUNK
    dk, dv = q.shape[1] // H, v.shape[1] // H
    nch = T // C
    mask, qd, kd, cd = tabs

    def body(q_ref, k_ref, v_ref, do_ref, st_ref, m_ref, qd_ref, kd_ref, cd_ref, dq_ref, dk_ref, dv_ref, ds_ref):
        @pl.when(pl.program_id(0) == 0)
        def _():
            ds_ref[...] = jnp.zeros((H, dk, dv), F32)

        for h in range(H):
            qv, kv = q_ref[:, h * dk:(h + 1) * dk], k_ref[:, h * dk:(h + 1) * dk]
            vv, dov = v_ref[:, h * dv:(h + 1) * dv], do_ref[:, h * dv:(h + 1) * dv]
            sb = st_ref[h]
            m = m_ref[h]
            ds = ds_ref[h]
            dsb = ds.astype(BF16)
            a = (lax.dot_general(qv, kv, NT, preferred_element_type=F32) * m).astype(BF16)
            dp = (lax.dot_general(dov, vv, NT, preferred_element_type=F32) * m).astype(BF16)
            qa = (qv.astype(F32) * qd_ref[h]).astype(BF16)
            kb = (kv.astype(F32) * kd_ref[h]).astype(BF16)
            dq_ref[:, h * dk:(h + 1) * dk] = (lax.dot_general(dp, kv, NN, preferred_element_type=F32)
                                              + qd_ref[h] * lax.dot_general(dov, sb, NT, preferred_element_type=F32))
            dk_ref[:, h * dk:(h + 1) * dk] = (lax.dot_general(dp, qv, TN, preferred_element_type=F32)
                                              + kd_ref[h] * lax.dot_general(vv, dsb, NT, preferred_element_type=F32))
            dv_ref[:, h * dv:(h + 1) * dv] = (lax.dot_general(a, dov, TN, preferred_element_type=F32)
                                              + lax.dot_general(kb, dsb, NN, preferred_element_type=F32)).astype(BF16)
            ds_ref[h] = ds * cd_ref[h] + lax.dot_general(qa, dov, TN, preferred_element_type=F32)

    rev = lambda n: (nch - 1 - n, 0)
    whole = lambda a: pl.BlockSpec(a.shape, lambda n: (0, 0, 0))
    return _pcall(body, name=name, grid=(nch,),
                  in_specs=[pl.BlockSpec((C, H * dk), rev), pl.BlockSpec((C, H * dk), rev), pl.BlockSpec((C, H * dv), rev),
                            pl.BlockSpec((C, H * dv), rev), pl.BlockSpec((H, None, dk, dv), lambda n: (0, nch - 1 - n, 0, 0)),
                            whole(mask), whole(qd), whole(kd), whole(cd)],
                  out_specs=[pl.BlockSpec((C, H * dk), rev), pl.BlockSpec((C, H * dk), rev), pl.BlockSpec((C, H * dv), rev)],
                  out_shape=[jax.ShapeDtypeStruct((T, H * dk), F32), jax.ShapeDtypeStruct((T, H * dk), F32),
                             jax.ShapeDtypeStruct((T, H * dv), BF16)],
                  scratch_shapes=[pltpu.VMEM((H, dk, dv), F32)],
                  compiler_params=_params(("arbitrary",)))(q, k, v, do, st, mask, qd, kd, cd)


def _local_step(x, pos, tgt, ng, get_w, put_g, first_deps, T, D):
    tr = _tile(T, 512)
    trw = _tile(T, 256)
    row = lambda i: ng[i:i + 1]

    u = _rowwise("rms_first", lambda xv, gv: _rms(xv, gv), [(x, D, 0), (row(0), None, None)], [("row", D, BF16)], T, tr,
                 deps=first_deps)[0]
    h = x
    saved = []
    cos = sin = None
    tabs = None
    for l in range(DEPTH):
        mixer = l % 4
        wts = get_w(l, "mix", h)
        sv = {"h_in": h, "u": u}
        if mixer == 0:
            z = _mm_cols("conv_in", u, wts["conv_w_in"], (), T, lambda acc, b: (acc + b,), (wts["conv_b_in"],), 1, BF16)[0]
            v = _rowwise("conv_glu", lambda za, zg: _glu(za.astype(F32), zg.astype(F32)), [(z, D, 0), (z, D, 1)], [("row", D, F32)], T, tr)[0]
            c = _dwconv_fwd("conv_dw", v, wts["conv_dw"], wts["conv_dw_b"], T)
            s = _rowwise("conv_ln", _ln_silu, [(c, D, 0), (wts["conv_ln_g"], None, None), (wts["conv_ln_b"], None, None)],
                         [("row", D, BF16)], T, tr)[0]
            y = _mm_rows("conv_out", s, wts["conv_w_out"], T, lambda acc, b: (acc + b,), (wts["conv_b_out"],), BF16)
            sv.update(z=z, v=v, c=c, s=s)
        elif mixer == 1:
            mixed = _pool_fwd("pool_fwd", u, T)
            y, ycat = _pool_mm("pool_mm", "nn", mixed, wts["pool_w"], T, wts["pool_scale"])
            sv.update(mixed=mixed, ycat=ycat)
        elif mixer == 2:
            z = _mm_cols("sc_in", u, wts["sc_w_in"], (), T, out_dtype=BF16)[0]
            m = _sc_mid_fwd("sc_mid", z, wts["sc_dw"], T)
            y = _mm_rows("sc_out", m, wts["sc_w_out"], T, out_dtype=BF16)
            sv.update(z=z, m=m)
        else:
            dk = D // RET_HEADS
            half = dk // 2
            inv_freq = jnp.asarray((ROPE_BASE ** (-np.arange(half, dtype=np.float64) / half)).astype(np.float32)).reshape(1, half)

            def _cs(p, f):
                ang = p * f
                return jnp.cos(ang), jnp.sin(ang)

            cos, sin = _rowwise("rope_tab", _cs, [(pos, 1, 0), (inv_freq, None, None)],
                                [("row", half, F32), ("row", half, F32)], T, tr)
            tabs = _ret_tables(dk, 2 * dk)
            z = _mm_cols("ret_in", u, wts["ret_w_in"], (), T, out_dtype=BF16)[0]

            def _pre(zq, zk, zv, cv, sn):
                return _rot(zq.astype(F32), cv, sn, 1.0), _rot(zk.astype(F32), cv, sn, 1.0) * (dk ** -0.5), zv

            qr, kr, vb = _rowwise("ret_pre", _pre, [(z, D, 0), (z, D, 1), (z, 2 * D, 1), (cos, half, 0), (sin, half, 0)],
                                  [("row", D, BF16), ("row", D, BF16), ("row", 2 * D, BF16)], T, trw)
            o, st = _ret_fwd("ret_fwd", qr, kr, vb, tabs, T)
            out = _rowwise("ret_post", lambda ov, gv: _gate_norm(ov, gv.astype(F32)), [(o, 2 * D, 0), (z, 2 * D, 2)], [("row", 2 * D, BF16)], T, trw)[0]
            y = _mm_rows("ret_out", out, wts["ret_w_out"], T, out_dtype=BF16)
            sv.update(z=z, qr=qr, kr=kr, vb=vb, o=o, st=st, out=out)
        sv["y1"] = y
        h, u = _rowwise("bound_a", lambda hv, yv, gp, gn: (lambda hn: (hn, _rms(hn, gn)))(hv + _rms(yv.astype(F32), gp)),
                        [(h, D, 0), (y, D, 0), (row(4 * l + 1), None, None), (row(4 * l + 2), None, None)],
                        [("row", D, F32), ("row", D, BF16)], T, tr)
        sv["h_mid"], sv["u2"] = h, u
        wm = get_w(l, "mlp", h)
        mlp_deps = wm.pop("_deps", [])
        wts = dict(wts, **wm)
        sv["w"] = wts
        r = _mm_cols("mlp_up", u, wts["mlp_up"], (), T, lambda acc: (jnp.maximum(acc, 0.0),), (), 1, BF16, deps=mlp_deps, rows=4096)[0]
        y = _mm_whole_k("mlp_down", r, wts["mlp_down"], T, "rows", pre=_square, out_dtype=BF16)
        sv.update(r=r, y2=y)
        if l + 1 < DEPTH:
            udt = F32 if (l + 1) % 4 == 1 else BF16
            h, u = _rowwise("bound_b", lambda hv, yv, gp, gn: (lambda hn: (hn, _rms(hn, gn)))(hv + _rms(yv.astype(F32), gp)),
                            [(h, D, 0), (y, D, 0), (row(4 * l + 3), None, None), (row(4 * l + 4), None, None)],
                            [("row", D, F32), ("row", D, udt)], T, tr)
        saved.append(sv)

    def _final(hv, yv, gp, tv):
        yf = yv.astype(F32)
        out, vjp = jax.vjp(_rms, yf, gp)
        e = hv + out - tv
        dhv = e * (1.0 / D)
        dy, dgp = vjp(dhv)
        return dhv, dy, _colsum(e * e), dgp

    gn = [None] * (4 * DEPTH)
    dh, dy_last, loss_row, gn[4 * DEPTH - 1] = _rowwise(
        "final", _final, [(h, D, 0), (y, D, 0), (row(4 * DEPTH - 1), None, None), (tgt, D, 0)],
        [("row", D, F32), ("row", D, BF16), ("acc", (1, D)), ("acc", (1, D))], T, tr)

    F = saved[0]["r"].shape[1]
    du_next = None
    tok = None
    for l in reversed(range(DEPTH)):
        sv = saved[l]
        wts = sv["w"]
        g = {}
        deps = [] if tok is None else [tok]
        mixer = l % 4
        if du_next is None:
            dy = dy_last
        else:
            dh, dy, gn[4 * l + 3], gn[4 * l + 4], _ = _bound_bwd("bwd_b", dh, du_next, saved[l + 1]["h_in"], sv["y2"],
                                                              row(4 * l + 3), row(4 * l + 4), T, D, tr, deps)
        F8 = F // N_DEV
        tm = _tile(T, 2048)
        da = _mm("mlp_dhid", "nt", (T // tm, N_DEV, 1),
                 [(dy, _spec((tm, D), lambda i, j, k: (i, 0))),
                  (wts["mlp_down"], _spec((None, F8, D), lambda i, j, k: (j, 0, 0))),
                  (sv["r"], _spec((tm, F8), lambda i, j, k: (i, j)))],
                 [((T, F), BF16, _spec((tm, F8), lambda i, j, k: (i, j)))],
                 lambda acc, rr: (acc * (2.0 * rr.astype(F32)),))[0]
        tk = T
        g["mlp_down"] = _mm("mlp_dwdown", "tn", (1, N_DEV, T // tk),
                            [(sv["r"], _spec((tk, F8), lambda i, j, k: (k, j))), (dy, _spec((tk, D), lambda i, j, k: (k, 0)))],
                            [(wts["mlp_down"].shape, BF16, _spec((None, F8, D), lambda i, j, k: (j, 0, 0)))],
                            lambda acc: (acc,), pre=_square)[0]
        g["mlp_up"] = _mm_cols_grad("mlp_dwup", sv["u2"], da, wts["mlp_up"].shape, (), T)
        du = _mm_whole_k("mlp_du", da, wts["mlp_up"], T, "cols", out_dtype=BF16)
        tok = put_g(l, "mlp", g, du, None)
        g = {}
        dh, dy, gn[4 * l + 1], gn[4 * l + 2], dy_sum = _bound_bwd("bwd_a", dh, du, sv["h_mid"], sv["y1"],
                                                                row(4 * l + 1), row(4 * l + 2), T, D, tr, [] if tok is None else [tok])
        u = sv["u"]
        if mixer == 0:
            g["conv_b_out"] = dy_sum
            ds = _mm_rows_t("conv_ds", dy, wts["conv_w_out"], T)
            g["conv_w_out"] = _mm_rows_grad("conv_dwout", sv["s"], dy, T)

            def _lnb(cv, gv, bv, dsv):
                _, vjp = jax.vjp(_ln_silu, cv, gv, bv)
                dc, dg, db = vjp(dsv)
                return dc, dg, db, _colsum(dc)

            dc, g["conv_ln_g"], g["conv_ln_b"], g["conv_dw_b"] = _rowwise(
                "conv_ln_bwd", _lnb, [(sv["c"], D, 0), (wts["conv_ln_g"], None, None), (wts["conv_ln_b"], None, None), (ds, D, 0)],
                [("row", D, F32), ("acc", (1, D)), ("acc", (1, D)), ("acc", (1, D))], T, tr)
            dv, g["conv_dw"] = _dwconv_bwd("conv_dw_bwd", dc, sv["v"], wts["conv_dw"], T)

            def _glub(za, zg, dvv):
                _, vjp = jax.vjp(_glu, za.astype(F32), zg.astype(F32))
                dza, dzg = vjp(dvv)
                dz = jnp.concatenate([dza, dzg], axis=-1)
                return dz, _colsum(dz)

            dz, g["conv_b_in"] = _rowwise("conv_glu_bwd", _glub, [(sv["z"], D, 0), (sv["z"], D, 1), (dv, D, 0)],
                                          [("row", 2 * D, BF16), ("acc", (1, 2 * D))], T, tr)
            g["conv_w_in"] = _mm_cols_grad("conv_dwin", u, dz, wts["conv_w_in"].shape, (), T)
            du = _mm_whole_k("conv_du", dz, wts["conv_w_in"], T, "cols", out_dtype=BF16)
        elif mixer == 1:
            def _psb(dyv, yc, sc):
                return dyv * sc, _colsum(dyv.astype(F32) * yc.astype(F32))

            dyg, g["pool_scale"] = _rowwise("pool_scale_bwd", _psb, [(dy, D, 0), (sv["ycat"], D, 0), (wts["pool_scale"], None, None)],
                                            [("row", D, BF16), ("acc", (1, D))], T, tr)
            dmix = _pool_mm("pool_dmix", "nt", dyg, wts["pool_w"], T)
            g["pool_w"] = _pool_mm("pool_dw", "tn", sv["mixed"], dyg, T)
            du = _pool_bwd("pool_bwd", dmix, T)
        elif mixer == 2:
            dm = _mm_rows_t("sc_dm", dy, wts["sc_w_out"], T)
            g["sc_w_out"] = _mm_rows_grad("sc_dwout", sv["m"], dy, T)
            dz, dw3 = _sc_mid_bwd("sc_mid_bwd", sv["z"], dm, wts["sc_dw"], T)
            g["sc_dw"] = dw3[0]
            g["sc_w_in"] = _mm_cols_grad("sc_dwin", u, dz, wts["sc_w_in"].shape, (), T)
            du = _mm_whole_k("sc_du", dz, wts["sc_w_in"], T, "cols", out_dtype=BF16)
        else:
            dk = D // RET_HEADS
            half = dk // 2
            dout = _mm_rows_t("ret_dout", dy, wts["ret_w_out"], T, BF16)
            g["ret_w_out"] = _mm_rows_grad("ret_dwout", sv["out"], dy, T)

            def _postb(ov, gv, dv_):
                _, vjp = jax.vjp(_gate_norm, ov, gv.astype(F32))
                return vjp(dv_.astype(F32))

            do, dzg = _rowwise("ret_post_bwd", _postb, [(sv["o"], 2 * D, 0), (sv["z"], 2 * D, 2), (dout, 2 * D, 0)],
                               [("row", 2 * D, BF16), ("row", 2 * D, BF16)], T, trw)
            dqr, dkr, dvb = _ret_bwd("ret_bwd", sv["qr"], sv["kr"], sv["vb"], do, sv["st"], tabs, T)

            def _preb(dq, dk_, dvv, dgv, cv, sn):
                return jnp.concatenate([_rot(dq, cv, sn, -1.0).astype(BF16), (_rot(dk_, cv, sn, -1.0) * (dk ** -0.5)).astype(BF16),
                                        dvv, dgv], axis=-1)

            dz = _rowwise("ret_pre_bwd", _preb, [(dqr, D, 0), (dkr, D, 0), (dvb, 2 * D, 0), (dzg, 2 * D, 0), (cos, half, 0), (sin, half, 0)],
                          [("row", 6 * D, BF16)], T, trw)[0]
            g["ret_w_in"] = _mm_cols_grad("ret_dwin", u, dz, wts["ret_w_in"].shape, (), T)
            du = _mm_whole_k("ret_du", dz, wts["ret_w_in"], T, "cols", out_dtype=BF16)
        du_next = du
        if l > 0:
            tok = put_g(l, "mix", g, du, None)

    def _fb(dhv, duv, xv, gv):
        _, vjp = jax.vjp(_rms, xv, gv)
        dx, dg = vjp(duv.astype(F32))
        return dhv + dx, dg

    grad_x, gn[0] = _rowwise("bwd_first", _fb, [(dh, D, 0), (du_next, D, 0), (x, D, 0), (row(0), None, None)],
                             [("row", D, F32), ("acc", (1, D))], T, tr)
    put_g(0, "mix", g, grad_x, jnp.concatenate(gn, axis=0))
    return loss_row, grad_x


def _bound_bwd(name, dh, du, h_new, y, g_post, g_next, T, D, tr, deps=()):
    def fn(dhv, duv, hn, yv, gp, gnx):
        _, vjp1 = jax.vjp(_rms, hn, gnx)
        d1, dgn = vjp1(duv.astype(F32))
        dht = dhv + d1
        _, vjp2 = jax.vjp(_rms, yv.astype(F32), gp)
        dy, dgp = vjp2(dht)
        return dht, dy, dgp, dgn, _colsum(dy)

    return _rowwise(name, fn, [(dh, D, 0), (du, D, 0), (h_new, D, 0), (y, D, 0), (g_post, None, None), (g_next, None, None)],
                    [("row", D, F32), ("row", D, BF16), ("acc", (1, D)), ("acc", (1, D)), ("acc", (1, D))], T, tr, deps)


def _adamw(name, parts, w, m, v, own=None, layer=None, bufs=None):
    P, R, C = parts.shape
    tr = _tile(R, 256)
    n_s = 1 if own is None else 2
    n_pre = 0 if own is None else 1

    def body(*refs):
        q_ref, refs = (refs[0], refs[1:]) if n_pre else (None, refs)
        p_ref = refs[0]
        w_ref, m_ref, v_ref = refs[n_s:n_s + 3]
        g_ref, d_ref, m2_ref, v2_ref = refs[-4:]

        def slot(i):
            if own is None:
                return p_ref[i].astype(F32)
            return jnp.where(q_ref[0] == i, refs[1][...], p_ref[i]).astype(F32)

        g = slot(0)
        for i in range(1, P):
            g = g + slot(i)
        wv = w_ref[...]
        m2 = ADAM_B1 * m_ref[...] + (1.0 - ADAM_B1) * g
        v2 = ADAM_B2 * v_ref[...] + (1.0 - ADAM_B2) * (g * g)
        m_hat = m2 / (1.0 - ADAM_B1 ** ADAM_STEP)
        v_hat = v2 / (1.0 - ADAM_B2 ** ADAM_STEP)
        g_ref[...] = g
        d_ref[...] = -ADAM_LR * (m_hat / (jnp.sqrt(v_hat) + ADAM_EPS) + ADAM_WD * wv)
        m2_ref[...] = m2
        v2_ref[...] = v2

    stack_specs = [pl.BlockSpec((P, tr, C), lambda i, *q: (0, i, 0))]
    stacks = [parts]
    pre = []
    if own is not None:
        stack_specs.append(pl.BlockSpec((None, tr, C), lambda i, q: (q[0], i, 0)))
        stacks.append(own)
        pre = [(2 * lax.axis_index("x") + lax.axis_index("y")).astype(jnp.int32).reshape(1)]
    if layer is None:
        blk = pl.BlockSpec((tr, C), lambda i, *q: (i, 0))
        spec = pltpu.PrefetchScalarGridSpec(num_scalar_prefetch=n_pre, grid=(R // tr,), in_specs=stack_specs + [blk, blk, blk],
                                            out_specs=[blk] * 4)
        return _pcall(body, name=name, grid_spec=spec, out_shape=[jax.ShapeDtypeStruct((R, C), F32)] * 4,
                      compiler_params=_params(("parallel",)))(*pre, *stacks, w, m, v)
    blk = pl.BlockSpec((None, tr, C), lambda i, *q: (layer, i, 0))
    spec = pltpu.PrefetchScalarGridSpec(num_scalar_prefetch=n_pre, grid=(R // tr,),
                                        in_specs=stack_specs + [blk, blk, blk] + [pl.BlockSpec(memory_space=pl.ANY)] * 4,
                                        out_specs=[blk] * 4)
    return _pcall(body, name=name, grid_spec=spec, out_shape=[jax.ShapeDtypeStruct(w.shape, F32)] * 4,
                  input_output_aliases={n_pre + n_s + 3 + i: i for i in range(4)},
                  compiler_params=_params(("parallel",)))(*pre, *stacks, w, m, v, *bufs)


def _pair_add(name, a, b):
    Q, _, R, C = a.shape
    tr = _tile(R, 1024)

    def body(c_ref, a_ref, b_ref, o_ref):
        o_ref[...] = (a_ref[...].astype(F32) + b_ref[...].astype(F32)).astype(BF16)

    core = lax.axis_index("c").astype(jnp.int32).reshape(1)
    blk = pl.BlockSpec((None, tr, C), lambda q, i, c: (q, i, 0))
    spec = pltpu.PrefetchScalarGridSpec(num_scalar_prefetch=1, grid=(Q, R // tr),
                                        in_specs=[pl.BlockSpec((None, None, tr, C), lambda q, i, c: (q, c[0], i, 0)), blk],
                                        out_specs=blk)
    return _pcall(body, name=name, grid_spec=spec, out_shape=jax.ShapeDtypeStruct((Q, R, C), BF16),
                  compiler_params=_params(("parallel", "parallel")))(core, a, b)


ANY = pl.BlockSpec(memory_space=pl.ANY)


def _place():
    x, y, c = lax.axis_index("x"), lax.axis_index("y"), lax.axis_index("c")
    chips = [(1 - x, y), (x, 1 - y), (1 - x, 1 - y)]
    return x, y, c, chips


def _gather_all(locs):
    n = len(locs)

    def body(*refs):
        ins, outs = refs[:n], refs[n:2 * n]
        send, recv, lsem = refs[2 * n:]
        x, y, c, chips = _place()
        me = 4 * x + 2 * y + c
        sib = (x, y, 1 - c)

        def cp(a, k, slot, to, src=None):
            return pltpu.make_async_remote_copy(src_ref=outs[a].at[slot] if src is None else src, dst_ref=outs[a].at[slot],
                                                send_sem=send.at[a, k], recv_sem=recv.at[a, k], device_id=to, device_id_type=MESH)

        mine = [pltpu.make_async_copy(ins[a], outs[a].at[me], lsem.at[a]) for a in range(n)]
        for d in mine:
            d.start()
        first = []
        for a in range(n):
            first.append(cp(a, 0, me, sib, src=ins[a]))
            for j, chip in enumerate(chips):
                first.append(cp(a, 1 + j, me, (chip[0], chip[1], c), src=ins[a]))
        for d in first:
            d.start()
        passed = []
        for j, chip in enumerate(chips):
            slot = 4 * chip[0] + 2 * chip[1] + c
            for a in range(n):
                cp(a, 1 + j, slot, sib).wait_recv()
                p = cp(a, 4 + j, slot, sib)
                p.start()
                passed.append(p)
        for a in range(n):
            cp(a, 0, 4 * x + 2 * y + (1 - c), sib).wait_recv()
        for j, chip in enumerate(chips):
            for a in range(n):
                cp(a, 4 + j, 4 * chip[0] + 2 * chip[1] + (1 - c), sib).wait_recv()
        for d in first + passed:
            d.wait_send()
        for d in mine:
            d.wait()

    return _pcall(body, name="gather_all", in_specs=[ANY] * n, out_specs=[ANY] * n,
                  out_shape=[jax.ShapeDtypeStruct((N_DEV,) + a.shape, a.dtype) for a in locs],
                  scratch_shapes=[pltpu.SemaphoreType.DMA((n, 7)), pltpu.SemaphoreType.DMA((n, 7)), pltpu.SemaphoreType.DMA((n,))],
                  compiler_params=pltpu.CompilerParams(has_side_effects=True))(*locs)


HBM = pl.BlockSpec(memory_space=pltpu.HBM)
SEM = pl.BlockSpec(memory_space=pltpu.SEMAPHORE)
DATAFLOW = pltpu.SideEffectType.DATAFLOW_SIDE_EFFECTING


def _peers():
    x, y, c = lax.axis_index("x"), lax.axis_index("y"), lax.axis_index("c")
    out = []
    for r in range(1, N_DEV):
        px = 1 - x if (r >> 2) & 1 else x
        py = 1 - y if (r >> 1) & 1 else y
        pc = 1 - c if r & 1 else c
        out.append(((px, py, pc), 4 * px + 2 * py + pc))
    return 4 * x + 2 * y + c, out


def _hbm(a):
    return pltpu.with_memory_space_constraint(a, pltpu.HBM)


def _gather_start(name, locs, lands, deps):
    n = len(locs)

    def body(*refs):
        ins, lnd = refs[:n], refs[n:2 * n]
        send, recv = refs[2 * n + len(deps)], refs[2 * n + len(deps) + 1]
        token = refs[-1]
        x, y, c, chips = _place()
        me = 4 * x + 2 * y + c
        for k, to in enumerate([(x, y, 1 - c)] + [(chip[0], chip[1], c) for chip in chips]):
            for a in range(n):
                pltpu.make_async_remote_copy(src_ref=ins[a], dst_ref=lnd[a].at[me], send_sem=send.at[a * 4 + k],
                                             recv_sem=recv.at[a * 4 + k], device_id=to, device_id_type=MESH).start()
        token[...] = jnp.zeros_like(token)

    out_shape = ((pltpu.SemaphoreType.DMA((n * 4,)), pltpu.SemaphoreType.DMA((n * 4,)))
                 + tuple(pltpu.HBM(a.shape, a.dtype) for a in locs) + tuple(pltpu.HBM(a.shape, a.dtype) for a in lands)
                 + (jax.ShapeDtypeStruct((8, LANES), F32),))
    res = _pcall(body, name=name, out_shape=out_shape, in_specs=[HBM] * (2 * n) + [ANY] * len(deps),
                 out_specs=(SEM, SEM) + (HBM,) * (2 * n) + (pl.BlockSpec(memory_space=pltpu.VMEM),),
                 input_output_aliases={a: 2 + a for a in range(2 * n)},
                 compiler_params=pltpu.CompilerParams(has_side_effects=DATAFLOW))(
        *[_hbm(a) for a in locs], *[_hbm(a) for a in lands], *deps)
    return res[0], res[1], res[2:2 + n], res[2 + n:2 + 2 * n], res[-1]


def _gather_pass(name, send, recv, shards, lands, after):
    n = len(shards)

    def body(*refs):
        ins, lnd = refs[:n], refs[n:2 * n]
        recv = refs[2 * n + 1]
        fsend, frecv = refs[2 * n + 3], refs[2 * n + 4]
        token = refs[-1]
        x, y, c, chips = _place()
        for j, chip in enumerate(chips):
            slot = 4 * chip[0] + 2 * chip[1] + c
            for a in range(n):
                pltpu.make_async_remote_copy(src_ref=ins[a], dst_ref=lnd[a].at[slot], send_sem=fsend.at[a * 3 + j],
                                             recv_sem=recv.at[a * 4 + 1 + j], device_id=(chip[0], chip[1], c),
                                             device_id_type=MESH).wait_recv()
                pltpu.make_async_remote_copy(src_ref=lnd[a].at[slot], dst_ref=lnd[a].at[slot], send_sem=fsend.at[a * 3 + j],
                                             recv_sem=frecv.at[a * 3 + j], device_id=(x, y, 1 - c), device_id_type=MESH).start()
        token[...] = jnp.zeros_like(token)

    out_shape = ((pltpu.SemaphoreType.DMA((n * 3,)), pltpu.SemaphoreType.DMA((n * 3,)))
                 + tuple(pltpu.HBM(a.shape, a.dtype) for a in shards) + tuple(pltpu.HBM(a.shape, a.dtype) for a in lands)
                 + (jax.ShapeDtypeStruct((8, LANES), F32),))
    res = _pcall(body, name=name, out_shape=out_shape, in_specs=[HBM] * (2 * n) + [SEM, SEM, ANY],
                 out_specs=(SEM, SEM) + (HBM,) * (2 * n) + (pl.BlockSpec(memory_space=pltpu.VMEM),),
                 input_output_aliases={a: 2 + a for a in range(2 * n)},
                 compiler_params=pltpu.CompilerParams(has_side_effects=DATAFLOW))(*shards, *lands, send, recv, after)
    return res[0], res[1], res[2:2 + n], res[2 + n:2 + 2 * n], res[-1]


def _gather_wait(name, send, recv, fsend, frecv, shards, lands, after):
    n = len(shards)

    def body(*refs):
        ins, lnd = refs[:n], refs[n:2 * n]
        send, recv, fsend, frecv = refs[2 * n:2 * n + 4]
        x, y, c, chips = _place()
        sib = (x, y, 1 - c)
        for a in range(n):
            own = pltpu.make_async_remote_copy(src_ref=ins[a], dst_ref=lnd[a].at[4 * x + 2 * y + (1 - c)], send_sem=send.at[a * 4],
                                               recv_sem=recv.at[a * 4], device_id=sib, device_id_type=MESH)
            own.wait_send()
            own.wait_recv()
            for j, chip in enumerate(chips):
                pltpu.make_async_remote_copy(src_ref=ins[a], dst_ref=lnd[a].at[0], send_sem=send.at[a * 4 + 1 + j],
                                             recv_sem=recv.at[a * 4 + 1 + j], device_id=(chip[0], chip[1], c),
                                             device_id_type=MESH).wait_send()
                fw = pltpu.make_async_remote_copy(src_ref=lnd[a].at[4 * chip[0] + 2 * chip[1] + c],
                                                  dst_ref=lnd[a].at[4 * chip[0] + 2 * chip[1] + (1 - c)], send_sem=fsend.at[a * 3 + j],
                                                  recv_sem=frecv.at[a * 3 + j], device_id=sib, device_id_type=MESH)
                fw.wait_send()
                fw.wait_recv()

    out_shape = tuple(pltpu.HBM(a.shape, a.dtype) for a in shards) + tuple(pltpu.HBM(a.shape, a.dtype) for a in lands)
    res = _pcall(body, name=name, out_shape=out_shape, in_specs=[HBM] * (2 * n) + [SEM, SEM, SEM, SEM, ANY], out_specs=(HBM,) * (2 * n),
                 input_output_aliases={a: a for a in range(2 * n)},
                 compiler_params=pltpu.CompilerParams(has_side_effects=DATAFLOW))(*shards, *lands, send, recv, fsend, frecv, after)
    return res[n:]


def _chip_start(name, cs, lands):
    n = len(cs)

    def body(*refs):
        ins, lnd = refs[:n], refs[n:2 * n]
        send, recv = refs[2 * n], refs[2 * n + 1]
        token = refs[-1]
        x, y, c, chips = _place()
        qme = 2 * x + y
        for j, chip in enumerate(chips):
            for a in range(n):
                pltpu.make_async_remote_copy(src_ref=ins[a].at[2 * chip[0] + chip[1]], dst_ref=lnd[a].at[qme], send_sem=send.at[a * 3 + j],
                                             recv_sem=recv.at[a * 3 + j], device_id=(chip[0], chip[1], c), device_id_type=MESH).start()
        token[...] = jnp.zeros_like(token)

    out_shape = ((pltpu.SemaphoreType.DMA((n * 3,)), pltpu.SemaphoreType.DMA((n * 3,)))
                 + tuple(pltpu.HBM(a.shape, a.dtype) for a in cs) + tuple(pltpu.HBM(a.shape, a.dtype) for a in lands)
                 + (jax.ShapeDtypeStruct((8, LANES), F32),))
    res = _pcall(body, name=name, out_shape=out_shape, in_specs=[HBM] * (2 * n),
                 out_specs=(SEM, SEM) + (HBM,) * (2 * n) + (pl.BlockSpec(memory_space=pltpu.VMEM),),
                 input_output_aliases={a: 2 + a for a in range(2 * n)},
                 compiler_params=pltpu.CompilerParams(has_side_effects=DATAFLOW))(*[_hbm(a) for a in cs], *[_hbm(a) for a in lands])
    return res[0], res[1], res[2:2 + n], res[2 + n:2 + 2 * n], res[-1]


def _chip_wait(name, send, recv, cs, lands, after):
    n = len(cs)

    def body(*refs):
        ins, lnd = refs[:n], refs[n:2 * n]
        send, recv = refs[2 * n], refs[2 * n + 1]
        x, y, c, chips = _place()
        for j, chip in enumerate(chips):
            q = 2 * chip[0] + chip[1]
            for a in range(n):
                cp = pltpu.make_async_remote_copy(src_ref=ins[a].at[q], dst_ref=lnd[a].at[q], send_sem=send.at[a * 3 + j],
                                                  recv_sem=recv.at[a * 3 + j], device_id=(chip[0], chip[1], c), device_id_type=MESH)
                cp.wait_send()
                cp.wait_recv()

    out_shape = tuple(pltpu.HBM(a.shape, a.dtype) for a in cs) + tuple(pltpu.HBM(a.shape, a.dtype) for a in lands)
    res = _pcall(body, name=name, out_shape=out_shape, in_specs=[HBM] * (2 * n) + [SEM, SEM, ANY], out_specs=(HBM,) * (2 * n),
                 input_output_aliases={a: a for a in range(2 * n)},
                 compiler_params=pltpu.CompilerParams(has_side_effects=DATAFLOW))(*cs, *lands, send, recv, after)
    return res[:n], res[n:]


def _pair_start(name, gs, lands):
    n = len(gs)

    def body(*refs):
        ins, lnd = refs[:n], refs[n:2 * n]
        send, recv = refs[2 * n], refs[2 * n + 1]
        token = refs[-1]
        x, y, c, _ = _place()
        for a in range(n):
            for q in range(N_CHIP):
                pltpu.make_async_remote_copy(src_ref=ins[a].at[q, 1 - c], dst_ref=lnd[a].at[q], send_sem=send.at[a * N_CHIP + q],
                                             recv_sem=recv.at[a * N_CHIP + q], device_id=(x, y, 1 - c), device_id_type=MESH).start()
        token[...] = jnp.zeros_like(token)

    out_shape = ((pltpu.SemaphoreType.DMA((n * N_CHIP,)), pltpu.SemaphoreType.DMA((n * N_CHIP,)))
                 + tuple(pltpu.HBM(a.shape, a.dtype) for a in gs) + tuple(pltpu.HBM(a.shape, a.dtype) for a in lands)
                 + (jax.ShapeDtypeStruct((8, LANES), F32),))
    res = _pcall(body, name=name, out_shape=out_shape, in_specs=[HBM] * (2 * n),
                 out_specs=(SEM, SEM) + (HBM,) * (2 * n) + (pl.BlockSpec(memory_space=pltpu.VMEM),),
                 input_output_aliases={a: 2 + a for a in range(2 * n)},
                 compiler_params=pltpu.CompilerParams(has_side_effects=DATAFLOW))(*[_hbm(a) for a in gs], *[_hbm(a) for a in lands])
    return res[0], res[1], res[2:2 + n], res[2 + n:2 + 2 * n], res[-1]


def _pair_wait(name, send, recv, gs, lands, after):
    n = len(gs)

    def body(*refs):
        ins, lnd = refs[:n], refs[n:2 * n]
        send, recv = refs[2 * n], refs[2 * n + 1]
        x, y, c, _ = _place()
        for a in range(n):
            for q in range(N_CHIP):
                cp = pltpu.make_async_remote_copy(src_ref=ins[a].at[q, 1 - c], dst_ref=lnd[a].at[q], send_sem=send.at[a * N_CHIP + q],
                                                  recv_sem=recv.at[a * N_CHIP + q], device_id=(x, y, 1 - c), device_id_type=MESH)
                cp.wait_send()
                cp.wait_recv()

    out_shape = tuple(pltpu.HBM(a.shape, a.dtype) for a in gs) + tuple(pltpu.HBM(a.shape, a.dtype) for a in lands)
    res = _pcall(body, name=name, out_shape=out_shape, in_specs=[HBM] * (2 * n) + [SEM, SEM, ANY], out_specs=(HBM,) * (2 * n),
                 input_output_aliases={a: a for a in range(2 * n)},
                 compiler_params=pltpu.CompilerParams(has_side_effects=DATAFLOW))(*gs, *lands, send, recv, after)
    return res[:n], res[n:]


def _pair_exchange(name, gs):
    n = len(gs)

    def body(*refs):
        ins, got = refs[:n], refs[n:2 * n]
        send, recv = refs[2 * n:]
        x, y, c, _ = _place()
        sib = (x, y, 1 - c)
        rem = [pltpu.make_async_remote_copy(src_ref=ins[a].at[q, 1 - c], dst_ref=got[a].at[q], send_sem=send.at[a, q],
                                            recv_sem=recv.at[a, q], device_id=sib, device_id_type=MESH)
               for a in range(n) for q in range(N_CHIP)]
        for d in rem:
            d.start()
        for d in rem:
            d.wait()

    half = [jax.ShapeDtypeStruct((N_CHIP,) + g.shape[2:], g.dtype) for g in gs]
    return _pcall(body, name=name, in_specs=[ANY] * n, out_specs=[ANY] * n, out_shape=half,
                  scratch_shapes=[pltpu.SemaphoreType.DMA((n, N_CHIP)), pltpu.SemaphoreType.DMA((n, N_CHIP))],
                  compiler_params=pltpu.CompilerParams(has_side_effects=True))(*gs)


def _chip_exchange(cs, sm_sh, sm_rep):
    n = len(cs)

    def body(*refs):
        ins, sh, rep = refs[:n], refs[n], refs[n + 1]
        outs, osh, orep = refs[n + 2:2 * n + 2], refs[2 * n + 2], refs[2 * n + 3]
        send, recv, lsem, ssend, srecv, slsem = refs[2 * n + 4:]
        x, y, c, chips = _place()
        qme = 2 * x + y
        me = 4 * x + 2 * y + c
        loc = [pltpu.make_async_copy(ins[a].at[qme], outs[a].at[qme], lsem.at[a]) for a in range(n)]
        loc.append(pltpu.make_async_copy(sh.at[me], osh.at[me], slsem.at[0]))
        loc.append(pltpu.make_async_copy(rep, orep.at[me], slsem.at[1]))
        rem, waits = [], []
        for j, chip in enumerate(chips):
            q = 2 * chip[0] + chip[1]
            to = (chip[0], chip[1], c)
            for a in range(n):
                rem.append(pltpu.make_async_remote_copy(src_ref=ins[a].at[q], dst_ref=outs[a].at[qme], send_sem=send.at[a, j],
                                                        recv_sem=recv.at[a, j], device_id=to, device_id_type=MESH))
                waits.append(pltpu.make_async_remote_copy(src_ref=ins[a].at[q], dst_ref=outs[a].at[q], send_sem=send.at[a, j],
                                                          recv_sem=recv.at[a, j], device_id=to, device_id_type=MESH))
        for r in range(1, N_DEV):
            fx, fy, fc = (r >> 2) & 1, (r >> 1) & 1, r & 1
            px = 1 - x if fx else x
            py = 1 - y if fy else y
            pc = 1 - c if fc else c
            t = 4 * px + 2 * py + pc
            to = (px, py, pc)
            rem.append(pltpu.make_async_remote_copy(src_ref=sh.at[t], dst_ref=osh.at[me], send_sem=ssend.at[0, r - 1],
                                                    recv_sem=srecv.at[0, r - 1], device_id=to, device_id_type=MESH))
            waits.append(pltpu.make_async_remote_copy(src_ref=sh.at[t], dst_ref=osh.at[t], send_sem=ssend.at[0, r - 1],
                                                      recv_sem=srecv.at[0, r - 1], device_id=to, device_id_type=MESH))
            rem.append(pltpu.make_async_remote_copy(src_ref=rep, dst_ref=orep.at[me], send_sem=ssend.at[1, r - 1],
                                                    recv_sem=srecv.at[1, r - 1], device_id=to, device_id_type=MESH))
            waits.append(pltpu.make_async_remote_copy(src_ref=rep, dst_ref=orep.at[t], send_sem=ssend.at[1, r - 1],
                                                      recv_sem=srecv.at[1, r - 1], device_id=to, device_id_type=MESH))
        for d in loc + rem:
            d.start()
        for d in waits:
            d.wait_recv()
        for d in rem:
            d.wait_send()
        for d in loc:
            d.wait()

    out_shape = [jax.ShapeDtypeStruct(a.shape, a.dtype) for a in cs]
    out_shape += [jax.ShapeDtypeStruct(sm_sh.shape, F32), jax.ShapeDtypeStruct((N_DEV,) + sm_rep.shape, F32)]
    res = _pcall(body, name="chip_exchange", in_specs=[ANY] * (n + 2), out_specs=[ANY] * (n + 2), out_shape=out_shape,
                 scratch_shapes=[pltpu.SemaphoreType.DMA((n, 3)), pltpu.SemaphoreType.DMA((n, 3)), pltpu.SemaphoreType.DMA((n,)),
                                 pltpu.SemaphoreType.DMA((2, 7)), pltpu.SemaphoreType.DMA((2, 7)), pltpu.SemaphoreType.DMA((2,))],
                 compiler_params=pltpu.CompilerParams(has_side_effects=True))(*cs, sm_sh, sm_rep)
    return res[:n], res[n], res[n + 1]


BIG = ("mlp_up", "mlp_down", "conv_w_in", "conv_w_out", "pool_w", "sc_w_in", "sc_w_out", "ret_w_in", "ret_w_out")
SMALL_SPLIT = ("norm_g", "conv_dw", "sc_dw")
SMALL_REP = ("conv_b_in", "conv_dw_b", "conv_ln_g", "conv_ln_b", "conv_b_out", "pool_scale")
WEIGHTS = ("norm_g", "mlp_up", "mlp_down", "conv_w_in", "conv_b_in", "conv_dw", "conv_dw_b", "conv_ln_g", "conv_ln_b",
           "conv_w_out", "conv_b_out", "pool_w", "pool_scale", "sc_w_in", "sc_dw", "sc_w_out", "ret_w_in", "ret_w_out")


def _pad_rows(a, rows):
    return jnp.pad(a, ((0, rows - a.shape[0]), (0, 0)))


def _split_rows(names, shards):
    out = []
    for nme in names:
        r = int(np.prod(shards[nme].shape[:-1]))
        out.append((nme, r, -(-r // 8) * 8))
    return out


def _pack_split(names, arrs):
    parts = []
    for nme, r, rp in _split_rows(names, arrs):
        parts.append(_pad_rows(arrs[nme].reshape(r, arrs[nme].shape[-1]), rp))
    return jnp.concatenate(parts, axis=0)


def _pack_rep(arrs, D):
    parts = [arrs[nme].reshape(-1, D) for nme in SMALL_REP]
    rows = sum(p.shape[0] for p in parts)
    return _pad_rows(jnp.concatenate(parts, axis=0), -(-rows // 8) * 8)


def kernel(x, positions, norm_g, mlp_up, mlp_down, conv_w_in, conv_b_in, conv_dw, conv_dw_b, conv_ln_g, conv_ln_b, conv_w_out, conv_b_out, pool_w, pool_scale, sc_w_in, sc_dw, sc_w_out, ret_w_in, ret_w_out, loss_target, m_norm_g, m_mlp_up, m_mlp_down, m_conv_w_in, m_conv_b_in, m_conv_dw, m_conv_dw_b, m_conv_ln_g, m_conv_ln_b, m_conv_w_out, m_conv_b_out, m_pool_w, m_pool_scale, m_sc_w_in, m_sc_dw, m_sc_w_out, m_ret_w_in, m_ret_w_out, v_norm_g, v_mlp_up, v_mlp_down, v_conv_w_in, v_conv_b_in, v_conv_dw, v_conv_dw_b, v_conv_ln_g, v_conv_ln_b, v_conv_w_out, v_conv_b_out, v_pool_w, v_pool_scale, v_sc_w_in, v_sc_dw, v_sc_w_out, v_ret_w_in, v_ret_w_out):
    w = dict(norm_g=norm_g, mlp_up=mlp_up, mlp_down=mlp_down, conv_w_in=conv_w_in, conv_b_in=conv_b_in, conv_dw=conv_dw,
             conv_dw_b=conv_dw_b, conv_ln_g=conv_ln_g, conv_ln_b=conv_ln_b, conv_w_out=conv_w_out, conv_b_out=conv_b_out,
             pool_w=pool_w, pool_scale=pool_scale, sc_w_in=sc_w_in, sc_dw=sc_dw, sc_w_out=sc_w_out, ret_w_in=ret_w_in, ret_w_out=ret_w_out)
    m = dict(norm_g=m_norm_g, mlp_up=m_mlp_up, mlp_down=m_mlp_down, conv_w_in=m_conv_w_in, conv_b_in=m_conv_b_in, conv_dw=m_conv_dw,
             conv_dw_b=m_conv_dw_b, conv_ln_g=m_conv_ln_g, conv_ln_b=m_conv_ln_b, conv_w_out=m_conv_w_out, conv_b_out=m_conv_b_out,
             pool_w=m_pool_w, pool_scale=m_pool_scale, sc_w_in=m_sc_w_in, sc_dw=m_sc_dw, sc_w_out=m_sc_w_out, ret_w_in=m_ret_w_in,
             ret_w_out=m_ret_w_out)
    v = dict(norm_g=v_norm_g, mlp_up=v_mlp_up, mlp_down=v_mlp_down, conv_w_in=v_conv_w_in, conv_b_in=v_conv_b_in, conv_dw=v_conv_dw,
             conv_dw_b=v_conv_dw_b, conv_ln_g=v_conv_ln_g, conv_ln_b=v_conv_ln_b, conv_w_out=v_conv_w_out, conv_b_out=v_conv_b_out,
             pool_w=v_pool_w, pool_scale=v_pool_scale, sc_w_in=v_sc_w_in, sc_dw=v_sc_dw, sc_w_out=v_sc_w_out, ret_w_in=v_ret_w_in,
             ret_w_out=v_ret_w_out)
    T, D = x.shape[1], x.shape[2]
    L = D // N_DEV
    G = len(POOL_WINDOWS)
    cg = D // G

    me = 4 * lax.axis_index("x") + 2 * lax.axis_index("y") + lax.axis_index("c")
    MIX = (("conv_w_in", "conv_w_out"), ("pool_w",), ("sc_w_in", "sc_w_out"), ("ret_w_in", "ret_w_out"))

    def shards(l, part):
        out = {k: w[k].astype(BF16) for k in MIX[l]} if part != "mlp" else {}
        if part != "mix":
            out["mlp_up"], out["mlp_down"] = w["mlp_up"][l].astype(BF16), w["mlp_down"][l].astype(BF16)
        return out

    sh0 = shards(0, "mix")
    gath0 = _gather_all(list(sh0.values()) + [_pack_split(SMALL_SPLIT, w)])
    small = gath0[-1]
    common = {k: w[k].reshape(1, -1) for k in SMALL_REP}
    off = 0
    for nme, r, rp in _split_rows(SMALL_SPLIT, w):
        common[nme] = small[:, off:off + r, :].transpose(1, 0, 2).reshape(r, D)
        off += rp
    flying = {}
    dep = small
    for l, part in [(0, "mlp")] + [(l, "both") for l in range(1, DEPTH)]:
        shl = shards(l, part)
        lands = [lax.dynamic_update_slice(lax.empty((N_DEV,) + a.shape, a.dtype), a[None], (me,) + (0,) * a.ndim) for a in shl.values()]
        send, recv, thru, lands, dep = _gather_start("gather_start_%d" % l, list(shl.values()), lands, [dep])
        flying[l] = dict(names=list(shl.keys()), send=send, recv=recv, thru=thru, lands=lands)

    def pass_on(l, after):
        f = flying[l]
        f["fsend"], f["frecv"], f["thru"], f["lands"], tok = _gather_pass("gather_pass_%d" % l, f["send"], f["recv"], f["thru"],
                                                                          f["lands"], after)
        return tok

    def arrive(l, after):
        f = flying[l]
        got = _gather_wait("gather_wait_%d" % l, f["send"], f["recv"], f["fsend"], f["frecv"], f["thru"], f["lands"], after)
        return dict(zip(f["names"], got))

    def get_w(l, part, after):
        if part == "mlp" and l > 0:
            return {"_deps": [pass_on(l + 1, after)]} if l + 1 < DEPTH else {}
        if l == 0 and part == "mix":
            gw = dict(zip(sh0.keys(), gath0[:-1]))
        elif l == 0:
            gw = arrive(0, pass_on(0, after))
            gw["_deps"] = [pass_on(1, gw["mlp_up"])]
        else:
            gw = arrive(l, after)
        out = dict(common)
        for k, a in gw.items():
            if k.endswith("_w_out"):
                out[k] = a.reshape(-1, D)
            elif k == "pool_w":
                out[k] = a.transpose(1, 0, 2, 3).reshape(G, cg, cg)
            else:
                out[k] = a
        return out

    res = {}
    mlp_res = {k: [lax.empty(w[k].shape, F32) for _ in range(4)] for k in ("mlp_up", "mlp_down")}
    pending = []
    small_g = {}

    def shard2d(src, k, l):
        a = src[k][l] if k in mlp_res else src[k]
        return a.reshape(-1, a.shape[-1])

    def adam_layer(l, names, parts, own):
        for i, k in enumerate(names):
            mine = None if own is None else own[i]
            if k in mlp_res:
                mlp_res[k] = _adamw("adamw_%s_%d" % (k, l), parts[i], w[k], m[k], v[k], mine, layer=l, bufs=mlp_res[k])
            else:
                out = _adamw("adamw_%s_%d" % (k, l), parts[i], shard2d(w, k, l), shard2d(m, k, l), shard2d(v, k, l), mine)
                res[k] = [o.reshape(w[k].shape) for o in out]

    held = {}

    def put_g(l, part, g, after, norm_grads):
        small_g.update({k: g[k] for k in g if k in SMALL_SPLIT or k in SMALL_REP})
        def stack(k):
            a = g[k]
            if k == "pool_w":
                a = a.reshape(G, N_DEV, cg // N_DEV, cg).transpose(1, 0, 2, 3).astype(BF16)
            return a.reshape(N_CHIP, 2, -1, w[k].shape[-1])

        if part == "mlp" and l > 0:
            st = [stack("mlp_up"), stack("mlp_down")]
            send, recv, st, lands, tok = _pair_start("pair_start_%d" % l, st, [lax.empty((N_CHIP,) + a.shape[2:], a.dtype) for a in st])
            held.update(send=send, recv=recv, st=st, lands=lands)
            return tok
        if pending:
            tag, lp, names, send, recv, sums, lands = pending.pop()
            sums, parts = _chip_wait("chip_wait_" + tag, send, recv, sums, lands, after)
            adam_layer(lp, names, parts, sums)
        last = l == 0 and part == "mix"
        names = list(MIX[l]) if (l > 0 or last) else ["mlp_up", "mlp_down"]
        tag = "%d%s" % (l, "" if l > 0 else part)
        stacks = [stack(k) for k in names]
        got = _pair_exchange("pair_exchange_" + tag, stacks)
        if l > 0:
            st, got_mlp = _pair_wait("pair_wait_%d" % l, held["send"], held["recv"], held["st"], held["lands"], got[0])
            names, stacks, got = names + ["mlp_up", "mlp_down"], stacks + list(st), list(got) + list(got_mlp)
            held.clear()
        sums = [_pair_add("pair_add", a, b) for a, b in zip(stacks, got)]
        if not last:
            send, recv, sums, lands, tok = _chip_start("chip_start_" + tag, sums, [lax.empty(a.shape, a.dtype) for a in sums])
            pending.append((tag, l, names, send, recv, sums, lands))
            return tok
        small_g["norm_g"] = norm_grads
        gsm = []
        for nme, r, rp in _split_rows(SMALL_SPLIT, w):
            gsm.append(jnp.pad(small_g[nme].reshape(r, N_DEV, L), ((0, rp - r), (0, 0), (0, 0))))
        sm_sh = jnp.concatenate(gsm, axis=0).transpose(1, 0, 2)
        parts, r_sh, r_rep = _chip_exchange(sums, sm_sh, _pack_rep(small_g, D))
        adam_layer(0, names, parts, None)
        out = _adamw("adamw_split", r_sh, _pack_split(SMALL_SPLIT, w), _pack_split(SMALL_SPLIT, m), _pack_split(SMALL_SPLIT, v))
        off = 0
        for nme, r, rp in _split_rows(SMALL_SPLIT, w):
            res[nme] = [o[off:off + r].reshape(w[nme].shape) for o in out]
            off += rp
        out = _adamw("adamw_rep", r_rep, _pack_rep(w, D), _pack_rep(m, D), _pack_rep(v, D))
        off = 0
        for nme in SMALL_REP:
            r = w[nme].shape[0] // D
            res[nme] = [o[off:off + r].reshape(w[nme].shape) for o in out]
            off += r
        return None

    pos = positions.reshape(T, 1).astype(F32)
    loss_row, grad_x = _local_step(x.reshape(T, D), pos, loss_target.reshape(T, D), common["norm_g"], get_w, put_g, [dep], T, D)
    loss = lax.psum(0.5 * jnp.sum(loss_row) / D, AXES)
    res.update(mlp_res)
    outs = [loss, grad_x.reshape(x.shape)]
    for i in range(4):
        outs += [res[k][i] for k in WEIGHTS]
    return tuple(outs)
```
